```python
import jax, jax.numpy as jnp
from jax import lax
import numpy as np


D_MODEL = 1024
BATCH = 2
SEQ = 8192
DEPTH = 1

HEAD_DIM = 64
A_HEADS = 8
A_WIDTH = A_HEADS * HEAD_DIM
B_HEADS = 8
B_WIDTH = B_HEADS * HEAD_DIM
B_KV = 2
B_HPG = B_HEADS // B_KV
KV_W = B_KV * HEAD_DIM
N_GATES = B_HEADS * 3
CHUNK = 128
L_CMP = 32
STRIDE_CMP = 16
CMP_HIDDEN = 256
L_SEL = 64
N_SEL = 16
WIN = 512
Q_BLOCK = 128
ROPE_THETA = 10000.0
N_GROUPS = 4
EXPERTS_PER_GROUP = 4
N_EXPERTS = N_GROUPS * EXPERTS_PER_GROUP
TOP_K_IN_GROUP = 2
D_FF_EXPERT = 512
D_PLE = 256
EPS = 1e-6
NEG = -1e30
FORCE = 1e6

OFF_Q = 2 * A_WIDTH
OFF_KV = OFF_Q + B_WIDTH
OFF_GATE = OFF_KV + 6 * KV_W
D_IN = OFF_GATE + N_GATES

kernel_name = 'hymba_gmlp_nsa_hiermoe_ple'


def rmsnorm(x, g):
    xf = x.astype(jnp.float32)
    y = xf * lax.rsqrt(jnp.mean(xf * xf, axis=-1, keepdims=True) + EPS)
    return (y * g.astype(jnp.float32)).astype(x.dtype)


def rope(x, pos):
    half = HEAD_DIM // 2
    inv = 1.0 / (ROPE_THETA ** (jnp.arange(half, dtype=jnp.float32) / half))
    ang = pos.astype(jnp.float32)[:, None] * inv[None, :]
    cos = jnp.cos(ang)[None, :, None, :]
    sin = jnp.sin(ang)[None, :, None, :]
    xf = x.astype(jnp.float32)
    x1, x2 = xf[..., :half], xf[..., half:]
    return jnp.concatenate([x1 * cos - x2 * sin, x1 * sin + x2 * cos], axis=-1).astype(x.dtype)


def masked_softmax(s, mask):
    s = jnp.where(mask, s.astype(jnp.float32), NEG)
    m = jnp.max(s, axis=-1, keepdims=True)
    e = jnp.where(mask, jnp.exp(s - m), 0.0)
    return e / jnp.maximum(jnp.sum(e, axis=-1, keepdims=True), 1e-30)


def chunked_gmlp(u, v, g_v, w_s, b_s):
    bsz, t_len, _ = u.shape
    v = rmsnorm(v, g_v).reshape(bsz, t_len // CHUNK, CHUNK, A_HEADS, HEAD_DIM)
    causal = jnp.tril(jnp.ones((CHUNK, CHUNK), dtype=bool))
    w = jnp.where(causal[None], w_s, 0.0).astype(v.dtype)
    mixed = jnp.einsum('gts,bcsgd->bctgd', w, v) + b_s.T[None, None, :, :, None].astype(v.dtype)
    return u * mixed.reshape(bsz, t_len, A_WIDTH)


def compress(kx, pe, w1, w2):
    bsz, g, t_len, _ = kx.shape
    nc = (t_len - L_CMP) // STRIDE_CMP + 1
    idx = jnp.arange(nc)[:, None] * STRIDE_CMP + jnp.arange(L_CMP)[None, :]
    blocks = kx[:, :, idx] + pe
    flat = blocks.reshape(bsz, g, nc, L_CMP * HEAD_DIM)
    return jax.nn.gelu(flat @ w1) @ w2


def nsa_attention(q, k_cmp, v_cmp, k_slc, v_slc, k_win, v_win, gates,
                  pe_k, w1_k, w2_k, pe_v, w1_v, w2_v):
    bsz, t_len = q.shape[0], q.shape[1]
    qh = q.reshape(bsz, t_len, B_KV, B_HPG, HEAD_DIM).transpose(0, 2, 3, 1, 4)
    gh = gates.reshape(bsz, t_len, B_KV, B_HPG, 3).transpose(0, 2, 3, 1, 4)
    to_g = lambda a: a.transpose(0, 2, 1, 3)

    kc = compress(to_g(k_cmp), pe_k, w1_k, w2_k)
    vc = compress(to_g(v_cmp), pe_v, w1_v, w2_v)
    nc = kc.shape[2]
    c_start = jnp.arange(nc) * STRIDE_CMP
    cmp_end = c_start + L_CMP - 1

    n_sb = t_len // L_SEL
    k_top = min(N_SEL, n_sb)
    s_start = jnp.arange(n_sb) * L_SEL
    overlap = ((c_start[:, None] < s_start[None, :] + L_SEL)
               & (c_start[:, None] + L_CMP > s_start[None, :])).astype(jnp.float32)
    ks_blocks = to_g(k_slc).reshape(bsz, B_KV, n_sb, L_SEL, HEAD_DIM)
    vs_blocks = to_g(v_slc).reshape(bsz, B_KV, n_sb, L_SEL, HEAD_DIM)

    pad = ((0, 0), (0, 0), (WIN, 0), (0, 0))
    kw_pad = jnp.pad(to_g(k_win), pad)
    vw_pad = jnp.pad(to_g(v_win), pad)
    b_idx = jnp.arange(bsz)[:, None, None, None]
    g_idx = jnp.arange(B_KV)[None, :, None, None]
    blk = jnp.arange(n_sb)

    def block(qb):
        start = qb * Q_BLOCK
        qi = lax.dynamic_slice_in_dim(qh, start, Q_BLOCK, axis=3)
        gi = lax.dynamic_slice_in_dim(gh, start, Q_BLOCK, axis=3).astype(jnp.float32)
        t = start + jnp.arange(Q_BLOCK)

        s_c = jnp.einsum('bghqd,bgcd->bghqc', qi, kc)
        p_c = masked_softmax(s_c, cmp_end[None, :] <= t[:, None])
        o_c = jnp.einsum('bghqc,bgcd->bghqd', p_c.astype(vc.dtype), vc)

        imp = jnp.einsum('bgqc,cs->bgqs', jnp.sum(p_c, axis=2), overlap)
        cur = (t // L_SEL)[:, None]
        forced = (blk[None, :] == 0) | (blk[None, :] == cur) | (blk[None, :] == cur - 1)
        valid = s_start[None, :] <= t[:, None]
        score = jnp.where(forced, FORCE, jnp.where(valid, imp, -FORCE))
        _, sel = lax.top_k(score, k_top)

        kg = ks_blocks[b_idx, g_idx, sel]
        vg = vs_blocks[b_idx, g_idx, sel]
        s_s = jnp.einsum('bghqd,bgqkld->bghqkl', qi, kg)
        pos = sel[..., None] * L_SEL + jnp.arange(L_SEL)
        m_s = (pos <= t[None, None, :, None, None])[:, :, None]
        flat_shape = s_s.shape[:4] + (k_top * L_SEL,)
        p_s = masked_softmax(s_s.reshape(flat_shape),
                             m_s.reshape(bsz, B_KV, 1, Q_BLOCK, k_top * L_SEL)).reshape(s_s.shape)
        o_s = jnp.einsum('bghqkl,bgqkld->bghqd', p_s.astype(vg.dtype), vg)

        kwi = lax.dynamic_slice_in_dim(kw_pad, start, Q_BLOCK + WIN, axis=2)
        vwi = lax.dynamic_slice_in_dim(vw_pad, start, Q_BLOCK + WIN, axis=2)
        kpos = start - WIN + jnp.arange(Q_BLOCK + WIN)
        diff = t[:, None] - kpos[None, :]
        m_w = (diff >= 0) & (diff < WIN) & (kpos[None, :] >= 0)
        s_w = jnp.einsum('bghqd,bgkd->bghqk', qi, kwi)
        p_w = masked_softmax(s_w, m_w)
        o_w = jnp.einsum('bghqk,bgkd->bghqd', p_w.astype(vwi.dtype), vwi)

        o = (gi[..., 0:1] * o_c.astype(jnp.float32) + gi[..., 1:2] * o_s.astype(jnp.float32)
             + gi[..., 2:3] * o_w.astype(jnp.float32))
        return o.astype(q.dtype)

    out = lax.map(block, jnp.arange(t_len // Q_BLOCK))
    return out.transpose(1, 0, 4, 2, 3, 5).reshape(bsz, t_len, B_WIDTH)


def hybrid_mixer(hn, w_in, g_v, w_s, b_s, pe_k, w1_k, w2_k, pe_v, w1_v, w2_v, g_out_a, g_out_b, w_o):
    bsz, t_len, _ = hn.shape
    z = hn @ w_in
    zu = jax.nn.gelu(z[..., :2 * A_WIDTH])
    u, v = zu[..., :A_WIDTH], zu[..., A_WIDTH:]
    pos = jnp.arange(t_len)
    q = rope(z[..., OFF_Q:OFF_KV].reshape(bsz, t_len, B_HEADS, HEAD_DIM), pos) * (HEAD_DIM ** -0.5)
    kv = [z[..., OFF_KV + j * KV_W:OFF_KV + (j + 1) * KV_W].reshape(bsz, t_len, B_KV, HEAD_DIM)
          for j in range(6)]
    k_cmp, v_cmp, k_slc, v_slc, k_win, v_win = kv
    k_cmp, k_slc, k_win = rope(k_cmp, pos), rope(k_slc, pos), rope(k_win, pos)
    gates = jax.nn.sigmoid(z[..., OFF_GATE:D_IN].astype(jnp.float32)).reshape(bsz, t_len, B_HEADS, 3)

    oa = chunked_gmlp(u, v, g_v, w_s, b_s)
    ob = nsa_attention(q, k_cmp, v_cmp, k_slc, v_slc, k_win, v_win, gates,
                       pe_k, w1_k, w2_k, pe_v, w1_v, w2_v)
    cat = jnp.concatenate([rmsnorm(oa, g_out_a), rmsnorm(ob, g_out_b)], axis=-1)
    return cat @ w_o


def hier_moe(hn, r_group, r_expert, w_gate, w_up, w_down):
    bsz, t_len, d = hn.shape
    xt = hn.reshape(-1, d)
    pg = jax.nn.softmax((xt @ r_group).astype(jnp.float32), axis=-1)
    pg_top, g_sel = lax.top_k(pg, 1)
    le = (xt @ r_expert).astype(jnp.float32).reshape(-1, N_GROUPS, EXPERTS_PER_GROUP)
    le_sel = jnp.take_along_axis(le, g_sel[:, :, None], axis=1)[:, 0]
    pe = jax.nn.softmax(le_sel, axis=-1)
    pe_top, e_sel = lax.top_k(pe, TOP_K_IN_GROUP)
    w = pg_top * pe_top / jnp.sum(pe_top, axis=-1, keepdims=True)
    eid = g_sel * EXPERTS_PER_GROUP + e_sel
    comb = jnp.einsum('nk,nke->ne', w, jax.nn.one_hot(eid, N_EXPERTS, dtype=jnp.float32))
    y = jnp.zeros(xt.shape, jnp.float32)
    for e in range(N_EXPERTS):
        hid = jax.nn.silu(xt @ w_gate[e]) * (xt @ w_up[e])
        y = y + comb[:, e:e + 1] * (hid @ w_down[e]).astype(jnp.float32)
    return y.astype(hn.dtype).reshape(bsz, t_len, d)


def setup_inputs(seed: int = 0) -> dict:
    key = jax.random.key(seed)
    ks = jax.random.split(key, 26)
    f = jnp.float32
    nrm = lambda k, shape, s: jax.random.normal(k, shape, f) * s
    gain = lambda k, shape: 1.0 + 0.02 * jax.random.normal(k, shape, f)
    return {
        'x': nrm(ks[0], (BATCH, SEQ, D_MODEL), 1.0),
        'p': nrm(ks[1], (DEPTH, BATCH, SEQ, D_PLE), 1.0),
        'norm_mix': gain(ks[2], (DEPTH, D_MODEL)),
        'w_in': nrm(ks[3], (DEPTH, D_MODEL, D_IN), D_MODEL ** -0.5),
        'gmlp_v_norm': gain(ks[4], (DEPTH, A_WIDTH)),
        'gmlp_w_s': nrm(ks[5], (DEPTH, A_HEADS, CHUNK, CHUNK), CHUNK ** -0.5),
        'gmlp_b_s': gain(ks[6], (DEPTH, A_HEADS, CHUNK)),
        'cmp_pe_k': nrm(ks[7], (DEPTH, L_CMP, HEAD_DIM), 0.02),
        'cmp_w1_k': nrm(ks[8], (DEPTH, L_CMP * HEAD_DIM, CMP_HIDDEN), (L_CMP * HEAD_DIM) ** -0.5),
        'cmp_w2_k': nrm(ks[9], (DEPTH, CMP_HIDDEN, HEAD_DIM), CMP_HIDDEN ** -0.5),
        'cmp_pe_v': nrm(ks[10], (DEPTH, L_CMP, HEAD_DIM), 0.02),
        'cmp_w1_v': nrm(ks[11], (DEPTH, L_CMP * HEAD_DIM, CMP_HIDDEN), (L_CMP * HEAD_DIM) ** -0.5),
        'cmp_w2_v': nrm(ks[12], (DEPTH, CMP_HIDDEN, HEAD_DIM), CMP_HIDDEN ** -0.5),
        'out_norm_a': gain(ks[13], (DEPTH, A_WIDTH)),
        'out_norm_b': gain(ks[14], (DEPTH, B_WIDTH)),
        'w_o': nrm(ks[15], (DEPTH, A_WIDTH + B_WIDTH, D_MODEL), (A_WIDTH + B_WIDTH) ** -0.5),
        'norm_moe': gain(ks[16], (DEPTH, D_MODEL)),
        'router_group': nrm(ks[17], (DEPTH, D_MODEL, N_GROUPS), D_MODEL ** -0.5),
        'router_expert': nrm(ks[18], (DEPTH, D_MODEL, N_EXPERTS), D_MODEL ** -0.5),
        'moe_w_gate': nrm(ks[19], (DEPTH, N_EXPERTS, D_MODEL, D_FF_EXPERT), D_MODEL ** -0.5),
        'moe_w_up': nrm(ks[20], (DEPTH, N_EXPERTS, D_MODEL, D_FF_EXPERT), D_MODEL ** -0.5),
        'moe_w_down': nrm(ks[21], (DEPTH, N_EXPERTS, D_FF_EXPERT, D_MODEL), D_FF_EXPERT ** -0.5),
        'norm_ple': gain(ks[22], (DEPTH, D_MODEL)),
        'w_ple_proj': nrm(ks[23], (DEPTH, D_PLE, D_MODEL), D_PLE ** -0.5),
        'w_ple_gate': nrm(ks[24], (DEPTH, D_MODEL, D_MODEL), D_MODEL ** -0.5),
        'norm_final': gain(ks[25], (D_MODEL,)),
    }


def reference(x, p, norm_mix, w_in, gmlp_v_norm, gmlp_w_s, gmlp_b_s,
              cmp_pe_k, cmp_w1_k, cmp_w2_k, cmp_pe_v, cmp_w1_v, cmp_w2_v,
              out_norm_a, out_norm_b, w_o, norm_moe, router_group, router_expert,
              moe_w_gate, moe_w_up, moe_w_down, norm_ple, w_ple_proj, w_ple_gate, norm_final):
    h = x
    for i in range(DEPTH):
        hn = rmsnorm(h, norm_mix[i])
        h = h + hybrid_mixer(hn, w_in[i], gmlp_v_norm[i], gmlp_w_s[i], gmlp_b_s[i],
                             cmp_pe_k[i], cmp_w1_k[i], cmp_w2_k[i],
                             cmp_pe_v[i], cmp_w1_v[i], cmp_w2_v[i],
                             out_norm_a[i], out_norm_b[i], w_o[i])
        h = h + hier_moe(rmsnorm(h, norm_moe[i]), router_group[i], router_expert[i],
                         moe_w_gate[i], moe_w_up[i], moe_w_down[i])
        gate = jax.nn.sigmoid(rmsnorm(h, norm_ple[i]) @ w_ple_gate[i])
        h = h + (p[i] @ w_ple_proj[i]) * gate
    return rmsnorm(h, norm_final)
```

```python
import functools

import numpy as np
import jax
import jax.numpy as jnp
from jax import lax
from jax.experimental import pallas as pl
from jax.experimental.pallas import tpu as pltpu

D_MODEL = 1024
HEAD_DIM = 64
A_HEADS = 8
A_WIDTH = A_HEADS * HEAD_DIM
B_HEADS = 8
B_WIDTH = B_HEADS * HEAD_DIM
B_KV = 2
B_HPG = B_HEADS // B_KV
KV_W = B_KV * HEAD_DIM
N_GATES = B_HEADS * 3
CHUNK = 128
L_CMP = 32
STRIDE_CMP = 16
CMP_HIDDEN = 256
L_SEL = 64
N_SEL = 16
WIN = 512
Q_BLOCK = 128
ROPE_THETA = 10000.0
N_GROUPS = 4
EXPERTS_PER_GROUP = 4
N_EXPERTS = N_GROUPS * EXPERTS_PER_GROUP
D_FF_EXPERT = 512
D_PLE = 256
EPS = 1e-6
NEG = -1e30
FORCE = 1e6

OFF_Q = 2 * A_WIDTH
OFF_KV = OFF_Q + B_WIDTH
OFF_GATE = OFF_KV + 6 * KV_W
D_IN = OFF_GATE + N_GATES

LANES = 128
GATE_PAD = LANES
ROUTER_OFF = N_GROUPS
W_ALL = OFF_GATE + GATE_PAD

TM_PROJ = 512
TM_POST = 512
TM_MOE = 1024
SUB_MOE = 256
SEL_CK = 512
VMEM_LIMIT = 56 * 1024 * 1024

_PERM_HEADS = [0, 4, 1, 5, 2, 6, 3, 7]
_PERM_CH = np.concatenate([np.arange(HEAD_DIM) + HEAD_DIM * h for h in _PERM_HEADS])

_F32 = jnp.float32
_BF16 = jnp.bfloat16


def _dot(a, b):
    return jnp.dot(a, b, preferred_element_type=_F32)


def _dot_nt(a, b):
    return lax.dot_general(a, b, (((1,), (1,)), ((), ())), preferred_element_type=_F32)


def _rms(x, g):
    return x * lax.rsqrt(jnp.mean(x * x, axis=-1, keepdims=True) + EPS) * g


def _gelu(x):
    return 0.5 * x * (1.0 + jnp.tanh(0.7978845608028654 * (x + 0.044715 * (x * x * x))))


def _rope_tile(x, cos, sin_signed):
    lane = lax.broadcasted_iota(jnp.int32, x.shape, 1)
    first_half = (lane % HEAD_DIM) < (HEAD_DIM // 2)
    rot = jnp.where(first_half, pltpu.roll(x, LANES - HEAD_DIM // 2, 1), pltpu.roll(x, HEAD_DIM // 2, 1))
    return x * cos + rot * sin_signed


def _proj_kernel(x_ref, gmix_ref, w_ref, cos_ref, sin_ref, gv_ref, ws_ref, bs_ref, goa_ref,
                 oa_ref, q_ref, kc_ref, vc_ref, ks_ref, vs_ref, kw_ref, vw_ref, gate_ref):
    tm = x_ref.shape[0]
    hn = _rms(x_ref[...], gmix_ref[...]).astype(_BF16)
    cos = cos_ref[...]
    sin = sin_ref[...]

    zq = _dot(hn, w_ref[:, OFF_Q:OFF_KV])
    scale = HEAD_DIM ** -0.5
    for j in range(B_WIDTH // LANES):
        blk = _rope_tile(zq[:, j * LANES:(j + 1) * LANES], cos, sin) * scale
        q_ref[:, j * LANES:(j + 1) * LANES] = blk.astype(q_ref.dtype)

    zkv = _dot(hn, w_ref[:, OFF_KV:OFF_GATE])
    outs = (kc_ref, vc_ref, ks_ref, vs_ref, kw_ref, vw_ref)
    for j, o_ref in enumerate(outs):
        blk = zkv[:, j * KV_W:(j + 1) * KV_W]
        if j % 2 == 0:
            blk = _rope_tile(blk, cos, sin)
        o_ref[...] = blk.astype(o_ref.dtype)

    zg = _dot(hn, w_ref[:, OFF_GATE:W_ALL])
    gate_ref[...] = jax.nn.sigmoid(zg)

    zu = _gelu(_dot(hn, w_ref[:, 0:A_WIDTH]))
    zv = _gelu(_dot(hn, w_ref[:, A_WIDTH:2 * A_WIDTH]))
    vn = _rms(zv, gv_ref[...]).astype(_BF16)

    t_io = lax.broadcasted_iota(jnp.int32, (CHUNK, 2 * CHUNK), 0)
    s_io = lax.broadcasted_iota(jnp.int32, (CHUNK, 2 * CHUNK), 1) % CHUNK
    causal = s_io <= t_io
    lane = lax.broadcasted_iota(jnp.int32, (CHUNK, LANES), 1)
    lo = lane < HEAD_DIM
    bs = bs_ref[...]
    chunks = []
    for c in range(tm // CHUNK):
        tiles = []
        for pr in range(A_HEADS // 2):
            wcat = jnp.where(causal, ws_ref[pr], 0.0).astype(_BF16)
            vblk = vn[c * CHUNK:(c + 1) * CHUNK, pr * LANES:(pr + 1) * LANES]
            zero = jnp.zeros_like(vblk)
            rhs = jnp.concatenate([jnp.where(lo, vblk, zero), jnp.where(lo, zero, vblk)], axis=0)
            tiles.append(_dot(wcat, rhs))
        chunks.append(jnp.concatenate(tiles, axis=1) + bs)
    mixed = jnp.concatenate(chunks, axis=0)
    oa = zu * mixed
    oa_ref[...] = _rms(oa, goa_ref[...]).astype(oa_ref.dtype)


def _proj_call(x2, gmix, w_all, cos_t, sin_t, gv, ws_pairs, bs_exp, goa, seq):
    n = x2.shape[0]
    tm = TM_PROJ
    n_t = seq // tm
    row = lambda i: (i, 0)
    const2 = lambda i: (0, 0)
    pos = lambda i: (i % n_t, 0)
    out_shapes = [jax.ShapeDtypeStruct((n, A_WIDTH), _BF16), jax.ShapeDtypeStruct((n, B_WIDTH), _BF16)]
    out_shapes += [jax.ShapeDtypeStruct((n, KV_W), _BF16)] * 6
    out_shapes += [jax.ShapeDtypeStruct((n, GATE_PAD), _F32)]
    out_specs = [pl.BlockSpec((tm, A_WIDTH), row), pl.BlockSpec((tm, B_WIDTH), row)]
    out_specs += [pl.BlockSpec((tm, KV_W), row)] * 6
    out_specs += [pl.BlockSpec((tm, GATE_PAD), row)]
    return pl.pallas_call(
        _proj_kernel,
        grid=(n // tm,),
        in_specs=[
            pl.BlockSpec((tm, D_MODEL), row),
            pl.BlockSpec((1, D_MODEL), const2),
            pl.BlockSpec((D_MODEL, W_ALL), const2),
            pl.BlockSpec((tm, LANES), pos),
            pl.BlockSpec((tm, LANES), pos),
            pl.BlockSpec((1, A_WIDTH), const2),
            pl.BlockSpec((A_HEADS // 2, CHUNK, 2 * CHUNK), lambda i: (0, 0, 0)),
            pl.BlockSpec((CHUNK, A_WIDTH), const2),
            pl.BlockSpec((1, A_WIDTH), const2),
        ],
        out_specs=out_specs,
        out_shape=out_shapes,
        compiler_params=pltpu.CompilerParams(dimension_semantics=("arbitrary",), vmem_limit_bytes=VMEM_LIMIT),
        name="proj",
    )(x2, gmix, w_all, cos_t, sin_t, gv, ws_pairs, bs_exp, goa)


def _compress_kernel(rk_ref, rv_ref, pek_ref, pev_ref, w1k_ref, w1v_ref,
                     tk_ref, bk_ref, w2k_ref, tv_ref, bv_ref, w2v_ref, kc_ref, vc_ref):
    def one(r_ref, pe_ref, w1_ref, top_ref, bot_ref, w2_ref, o_ref):
        r = r_ref[0]
        nr = r.shape[0]
        a = _dot(r, top_ref[...])
        b = _dot(r, bot_ref[...])
        pe_h = _dot(pe_ref[...], w1_ref[...])
        pe2 = jnp.concatenate([pe_h[0:1], pe_h[0:1]], axis=1)
        hid = a + pltpu.roll(b, nr - 1, 0) + pe2
        o_ref[0] = _dot(_gelu(hid).astype(_BF16), w2_ref[...]).astype(o_ref.dtype)

    one(rk_ref, pek_ref, w1k_ref, tk_ref, bk_ref, w2k_ref, kc_ref)
    one(rv_ref, pev_ref, w1v_ref, tv_ref, bv_ref, w2v_ref, vc_ref)


def _compress_weights(w1, w2, pe):
    half = L_CMP // 2
    w1r = w1.reshape(L_CMP, HEAD_DIM, CMP_HIDDEN)
    eye = jnp.eye(B_KV, dtype=w1.dtype)
    place = lambda part: jnp.einsum('ldj,gh->lgdhj', part, eye).reshape(half * KV_W, B_KV * CMP_HIDDEN)
    top = place(w1r[:half]).astype(_BF16)
    bot = place(w1r[half:]).astype(_BF16)
    w2bd = jnp.einsum('jd,gh->gjhd', w2, eye).reshape(B_KV * CMP_HIDDEN, KV_W).astype(_BF16)
    pe8 = jnp.broadcast_to(pe.reshape(1, L_CMP * HEAD_DIM), (8, L_CMP * HEAD_DIM)).astype(_BF16)
    return pe8, w1.astype(_BF16), top, bot, w2bd


def _compress_call(k_cmp, v_cmp, wk, wv, bsz, seq):
    nr = seq // STRIDE_CMP
    rk = k_cmp.reshape(bsz, nr, STRIDE_CMP * KV_W)
    rv = v_cmp.reshape(bsz, nr, STRIDE_CMP * KV_W)
    pek, w1k, tk, bk, w2k = wk
    pev, w1v, tv, bv, w2v = wv
    rspec = pl.BlockSpec((1, nr, STRIDE_CMP * KV_W), lambda b: (b, 0, 0))
    full = lambda a: pl.BlockSpec(a.shape, lambda b: (0,) * a.ndim)
    ospec = pl.BlockSpec((1, nr, KV_W), lambda b: (b, 0, 0))
    return pl.pallas_call(
        _compress_kernel,
        grid=(bsz,),
        in_specs=[rspec, rspec, full(pek), full(pev), full(w1k), full(w1v),
                  full(tk), full(bk), full(w2k), full(tv), full(bv), full(w2v)],
        out_specs=[ospec, ospec],
        out_shape=[jax.ShapeDtypeStruct((bsz, nr, KV_W), _BF16)] * 2,
        compiler_params=pltpu.CompilerParams(dimension_semantics=("arbitrary",), vmem_limit_bytes=VMEM_LIMIT),
        name="compress",
    )(rk, rv, pek, pev, w1k, w1v, tk, bk, w2k, tv, bv, w2v)


def _topk_rows_mask(sc_t, k):
    n_rows = sc_t.shape[0]
    row = lax.broadcasted_iota(jnp.int32, sc_t.shape, 0)
    sel = jnp.zeros(sc_t.shape, _F32)
    for _ in range(k):
        m = jnp.max(sc_t, axis=0, keepdims=True)
        idx = jnp.min(jnp.where(sc_t == m, row, n_rows), axis=0, keepdims=True)
        chosen = row == idx
        sel = jnp.where(chosen, 1.0, sel)
        sc_t = jnp.where(chosen, -3e38, sc_t)
    return sel


def _attn_kernel(q_ref, gate_ref, kc_ref, vc_ref, ks_ref, vs_ref, kw_ref, vw_ref, gob_ref, o_ref, *, seq):
    qb = pl.program_id(1)
    t0 = qb * Q_BLOCK
    n_cmp = kc_ref.shape[1]
    n_sb = seq // L_SEL
    k_top = min(N_SEL, n_sb)
    rows = B_HPG * Q_BLOCK

    gates = gate_ref[0]
    lane_q = lax.broadcasted_iota(jnp.int32, (Q_BLOCK, LANES), 1)
    t_col = t0 + lax.broadcasted_iota(jnp.int32, (Q_BLOCK, 1), 0)

    c_lane = lax.broadcasted_iota(jnp.int32, (Q_BLOCK, n_cmp), 1)
    bias_c = jnp.where(c_lane * STRIDE_CMP + (L_CMP - 1) <= t_col, 0.0, NEG)
    has_c = (t_col >= L_CMP - 1).astype(_F32)
    c_row = lax.broadcasted_iota(jnp.int32, (n_cmp, n_sb), 0) * STRIDE_CMP
    s_col = lax.broadcasted_iota(jnp.int32, (n_cmp, n_sb), 1) * L_SEL
    overlap = jnp.where((c_row < s_col + L_SEL) & (c_row + L_CMP > s_col), 1.0, 0.0).astype(_BF16)

    blk = lax.broadcasted_iota(jnp.int32, (Q_BLOCK, n_sb), 1)
    cur = t_col // L_SEL
    forced = (blk == 0) | (blk == cur) | (blk == cur - 1)
    valid = blk * L_SEL <= t_col

    w_start = pl.multiple_of(jnp.maximum(t0 - WIN, 0), Q_BLOCK)
    w_len = WIN + Q_BLOCK
    kpos_w = w_start + lax.broadcasted_iota(jnp.int32, (Q_BLOCK, w_len), 1)
    diff_w = t_col - kpos_w
    bias_w = jnp.where((diff_w >= 0) & (diff_w < WIN), 0.0, NEG)

    n_ck = (t0 + Q_BLOCK + SEL_CK - 1) // SEL_CK
    blk_per_ck = SEL_CK // L_SEL
    e_row = lax.broadcasted_iota(jnp.int32, (n_sb, SEL_CK), 0)
    e_col = lax.broadcasted_iota(jnp.int32, (n_sb, SEL_CK), 1) // L_SEL
    key_lane = lax.broadcasted_iota(jnp.int32, (Q_BLOCK, SEL_CK), 1)

    def add_bias(s, bias):
        return (s.reshape(B_HPG, Q_BLOCK, s.shape[-1]) + bias[None]).reshape(rows, s.shape[-1])

    group_out = []
    for g in range(B_KV):
        half_g = (lane_q // HEAD_DIM) == g
        qs = jnp.concatenate(
            [jnp.where(half_g, q_ref[0, :, j * LANES:(j + 1) * LANES], jnp.zeros((), q_ref.dtype))
             for j in range(B_HPG)], axis=0)

        s_c = add_bias(_dot_nt(qs, kc_ref[0]), bias_c)
        m_c = jnp.max(s_c, axis=-1, keepdims=True)
        e_c = jnp.exp(s_c - m_c)
        p_c = e_c * (1.0 / jnp.maximum(jnp.sum(e_c, axis=-1, keepdims=True), 1e-30))
        p_c = (p_c.reshape(B_HPG, Q_BLOCK, n_cmp) * has_c[None]).reshape(rows, n_cmp)
        o_c = _dot(p_c.astype(_BF16), vc_ref[0])

        p_sum = jnp.sum(p_c.reshape(B_HPG, Q_BLOCK, n_cmp), axis=0)
        p_hi = p_sum.astype(_BF16)
        p_lo = (p_sum - p_hi.astype(_F32)).astype(_BF16)
        imp = _dot(p_hi, overlap) + _dot(p_lo, overlap)
        score = jnp.where(forced, FORCE, jnp.where(valid, imp, -FORCE))
        sel = _topk_rows_mask(score.T, k_top).T.astype(_BF16)

        def sel_body(ci, carry):
            m, l, acc = carry
            k0 = pl.multiple_of(ci * SEL_CK, SEL_CK)
            kk = ks_ref[0, pl.ds(k0, SEL_CK), :]
            vv = vs_ref[0, pl.ds(k0, SEL_CK), :]
            expand = jnp.where(e_row == ci * blk_per_ck + e_col, 1.0, 0.0).astype(_BF16)
            chosen = _dot(sel, expand)
            ok = (chosen > 0.5) & (k0 + key_lane <= t_col)
            s = add_bias(_dot_nt(qs, kk), jnp.where(ok, 0.0, NEG))
            m_new = jnp.maximum(m, jnp.max(s, axis=-1, keepdims=True))
            p = jnp.exp(s - m_new)
            alpha = jnp.exp(m - m_new)
            l = alpha * l + jnp.sum(p, axis=-1, keepdims=True)
            acc = alpha * acc + _dot(p.astype(_BF16), vv)
            return m_new, l, acc

        init = (jnp.full((rows, 1), NEG, _F32), jnp.zeros((rows, 1), _F32), jnp.zeros((rows, LANES), _F32))
        _, l_s, acc_s = lax.fori_loop(0, n_ck, sel_body, init)
        o_s = acc_s * (1.0 / jnp.maximum(l_s, 1e-30))

        kw = kw_ref[0, pl.ds(w_start, w_len), :]
        vw = vw_ref[0, pl.ds(w_start, w_len), :]
        s_w = add_bias(_dot_nt(qs, kw), bias_w)
        m_w = jnp.max(s_w, axis=-1, keepdims=True)
        e_w = jnp.exp(s_w - m_w)
        p_w = e_w * (1.0 / jnp.maximum(jnp.sum(e_w, axis=-1, keepdims=True), 1e-30))
        o_w = _dot(p_w.astype(_BF16), vw)

        slots = []
        for j in range(B_HPG):
            h = g * B_HPG + j
            r0, r1 = j * Q_BLOCK, (j + 1) * Q_BLOCK
            slots.append(gates[:, 3 * h:3 * h + 1] * o_c[r0:r1]
                         + gates[:, 3 * h + 1:3 * h + 2] * o_s[r0:r1]
                         + gates[:, 3 * h + 2:3 * h + 3] * o_w[r0:r1])
        group_out.append(slots)

    lo = lane_q < HEAD_DIM
    ob = jnp.concatenate([jnp.where(lo, group_out[0][j], group_out[1][j]) for j in range(B_HPG)], axis=1)
    o_ref[0] = _rms(ob, gob_ref[...]).astype(o_ref.dtype)


def _attn_call(q, gates, kc, vc, ks, vs, kw, vw, gob, bsz, seq):
    n_cmp = kc.shape[1]
    qspec = lambda w: pl.BlockSpec((1, Q_BLOCK, w), lambda b, i: (b, i, 0))
    full = lambda r: pl.BlockSpec((1, r, KV_W), lambda b, i: (b, 0, 0))
    return pl.pallas_call(
        functools.partial(_attn_kernel, seq=seq),
        grid=(bsz, seq // Q_BLOCK),
        in_specs=[qspec(B_WIDTH), qspec(GATE_PAD), full(n_cmp), full(n_cmp),
                  full(seq), full(seq), full(seq), full(seq),
                  pl.BlockSpec((1, B_WIDTH), lambda b, i: (0, 0))],
        out_specs=qspec(B_WIDTH),
        out_shape=jax.ShapeDtypeStruct((bsz, seq, B_WIDTH), _BF16),
        compiler_params=pltpu.CompilerParams(dimension_semantics=("arbitrary", "arbitrary"),
                                             vmem_limit_bytes=VMEM_LIMIT),
        name="attn",
    )(q, gates, kc, vc, ks, vs, kw, vw, gob)


def _post_kernel(x_ref, oa_ref, ob_ref, woa_ref, wob_ref, gmoe_ref, r_ref, h1_ref, hn_ref, comb_ref):
    h1 = x_ref[...] + _dot(oa_ref[...], woa_ref[...]) + _dot(ob_ref[...], wob_ref[...])
    h1_ref[...] = h1
    hn = _rms(h1, gmoe_ref[...])
    hn_ref[...] = hn.astype(hn_ref.dtype)

    logits = jnp.dot(hn, r_ref[...], preferred_element_type=_F32, precision=lax.Precision.HIGHEST)
    lane = lax.broadcasted_iota(jnp.int32, logits.shape, 1)
    first_idx = lambda hit: jnp.min(jnp.where(hit, lane, LANES), axis=-1, keepdims=True)

    is_g = lane < N_GROUPS
    lg = jnp.where(is_g, logits, NEG)
    mg = jnp.max(lg, axis=-1, keepdims=True)
    sg = jnp.sum(jnp.where(is_g, jnp.exp(lg - mg), 0.0), axis=-1, keepdims=True)
    pg_top = 1.0 / sg
    g_sel = first_idx(is_g & (lg == mg))

    e_lo = ROUTER_OFF + g_sel * EXPERTS_PER_GROUP
    is_e = (lane >= e_lo) & (lane < e_lo + EXPERTS_PER_GROUP)
    le = jnp.where(is_e, logits, NEG)
    m1 = jnp.max(le, axis=-1, keepdims=True)
    se = jnp.sum(jnp.where(is_e, jnp.exp(le - m1), 0.0), axis=-1, keepdims=True)
    i1 = first_idx(is_e & (le == m1))
    le2 = jnp.where(lane == i1, NEG, le)
    m2 = jnp.max(le2, axis=-1, keepdims=True)
    i2 = first_idx(is_e & (lane != i1) & (le2 == m2))
    pe1 = 1.0 / se
    pe2 = jnp.exp(m2 - m1) / se
    denom = pe1 + pe2
    comb_ref[...] = (jnp.where(lane == i1, pg_top * pe1 / denom, 0.0)
                     + jnp.where(lane == i2, pg_top * pe2 / denom, 0.0))


def _post_call(x2, oa, ob, woa, wob, gmoe, r_cat):
    n = x2.shape[0]
    tm = TM_POST
    row = lambda i: (i, 0)
    const2 = lambda i: (0, 0)
    return pl.pallas_call(
        _post_kernel,
        grid=(n // tm,),
        in_specs=[pl.BlockSpec((tm, D_MODEL), row), pl.BlockSpec((tm, A_WIDTH), row),
                  pl.BlockSpec((tm, B_WIDTH), row), pl.BlockSpec((A_WIDTH, D_MODEL), const2),
                  pl.BlockSpec((B_WIDTH, D_MODEL), const2), pl.BlockSpec((1, D_MODEL), const2),
                  pl.BlockSpec((D_MODEL, GATE_PAD), const2)],
        out_specs=[pl.BlockSpec((tm, D_MODEL), row), pl.BlockSpec((tm, D_MODEL), row),
                   pl.BlockSpec((tm, GATE_PAD), row)],
        out_shape=[jax.ShapeDtypeStruct((n, D_MODEL), _F32), jax.ShapeDtypeStruct((n, D_MODEL), _BF16),
                   jax.ShapeDtypeStruct((n, GATE_PAD), _F32)],
        compiler_params=pltpu.CompilerParams(dimension_semantics=("arbitrary",), vmem_limit_bytes=VMEM_LIMIT),
        name="post",
    )(x2, oa, ob, woa, wob, gmoe, r_cat)


def _moe_kernel(hn_ref, comb_ref, h1_ref, p_ref, wgu_ref, wd_ref, gple_ref, wpg_ref, wpp_ref, gfin_ref,
                o_ref, acc_ref):
    e = pl.program_id(1)
    n_sub = hn_ref.shape[0] // SUB_MOE

    @pl.when(e == 0)
    def _():
        acc_ref[...] = jnp.zeros_like(acc_ref)

    def sub(r, carry):
        r0 = pl.multiple_of(r * SUB_MOE, SUB_MOE)
        rows = pl.ds(r0, SUB_MOE)
        gu = _dot(hn_ref[rows, :], wgu_ref[0])
        gate = gu[:, :D_FF_EXPERT]
        hid = gate * jax.nn.sigmoid(gate) * gu[:, D_FF_EXPERT:]
        comb = comb_ref[rows, :]
        lane = lax.broadcasted_iota(jnp.int32, comb.shape, 1)
        w_e = jnp.sum(jnp.where(lane == ROUTER_OFF + e, comb, 0.0), axis=-1, keepdims=True)
        acc_ref[rows, :] += _dot((hid * w_e).astype(_BF16), wd_ref[0])
        return carry

    lax.fori_loop(0, n_sub, sub, 0)

    @pl.when(e == N_EXPERTS - 1)
    def _():
        def fin(r, carry):
            r0 = pl.multiple_of(r * SUB_MOE, SUB_MOE)
            rows = pl.ds(r0, SUB_MOE)
            h2 = h1_ref[rows, :] + acc_ref[rows, :]
            gate = jax.nn.sigmoid(_dot(_rms(h2, gple_ref[...]).astype(_BF16), wpg_ref[...]))
            h3 = h2 + _dot(p_ref[rows, :].astype(_BF16), wpp_ref[...]) * gate
            o_ref[rows, :] = _rms(h3, gfin_ref[...])
            return carry

        lax.fori_loop(0, n_sub, fin, 0)


def _moe_call(hn2, comb, h1, p2, wgu, wd, gple, wpg, wpp, gfin):
    n = hn2.shape[0]
    tm = TM_MOE
    row = lambda i, e: (i, 0)
    const2 = lambda i, e: (0, 0)
    return pl.pallas_call(
        _moe_kernel,
        grid=(n // tm, N_EXPERTS),
        in_specs=[pl.BlockSpec((tm, D_MODEL), row), pl.BlockSpec((tm, GATE_PAD), row),
                  pl.BlockSpec((tm, D_MODEL), row), pl.BlockSpec((tm, D_PLE), row),
                  pl.BlockSpec((1, D_MODEL, 2 * D_FF_EXPERT), lambda i, e: (e, 0, 0)),
                  pl.BlockSpec((1, D_FF_EXPERT, D_MODEL), lambda i, e: (e, 0, 0)),
                  pl.BlockSpec((1, D_MODEL), const2), pl.BlockSpec((D_MODEL, D_MODEL), const2),
                  pl.BlockSpec((D_PLE, D_MODEL), const2), pl.BlockSpec((1, D_MODEL), const2)],
        out_specs=pl.BlockSpec((tm, D_MODEL), row),
        out_shape=jax.ShapeDtypeStruct((n, D_MODEL), _F32),
        scratch_shapes=[pltpu.VMEM((tm, D_MODEL), _F32)],
        compiler_params=pltpu.CompilerParams(dimension_semantics=("arbitrary", "arbitrary"),
                                             vmem_limit_bytes=VMEM_LIMIT),
        name="moe",
    )(hn2, comb, h1, p2, wgu, wd, gple, wpg, wpp, gfin)


def _rope_tables(seq):
    half = HEAD_DIM // 2
    inv = 1.0 / (ROPE_THETA ** (jnp.arange(half, dtype=_F32) / half))
    ang = jnp.arange(seq, dtype=_F32)[:, None] * inv[None, :]
    cos, sin = jnp.cos(ang), jnp.sin(ang)
    reps = LANES // HEAD_DIM
    cos_t = jnp.tile(jnp.concatenate([cos, cos], axis=1), (1, reps))
    sin_t = jnp.tile(jnp.concatenate([-sin, sin], axis=1), (1, reps))
    return cos_t, sin_t


def _layer(h, p_i, norm_mix, w_in, gmlp_v_norm, gmlp_w_s, gmlp_b_s,
           cmp_pe_k, cmp_w1_k, cmp_w2_k, cmp_pe_v, cmp_w1_v, cmp_w2_v,
           out_norm_a, out_norm_b, w_o, norm_moe, router_group, router_expert,
           moe_w_gate, moe_w_up, moe_w_down, norm_ple, w_ple_proj, w_ple_gate, norm_final):
    bsz, seq, _ = h.shape
    n = bsz * seq
    x2 = h.reshape(n, D_MODEL)
    row = lambda v: v.reshape(1, -1).astype(_F32)

    w_q = w_in[:, OFF_Q:OFF_KV][:, _PERM_CH]
    w_gate = jnp.pad(w_in[:, OFF_GATE:D_IN], ((0, 0), (0, GATE_PAD - N_GATES)))
    w_all = jnp.concatenate([w_in[:, :OFF_Q], w_q, w_in[:, OFF_KV:OFF_GATE], w_gate], axis=1).astype(_BF16)
    cos_t, sin_t = _rope_tables(seq)
    ws_pairs = gmlp_w_s.reshape(A_HEADS // 2, 2, CHUNK, CHUNK).transpose(0, 2, 1, 3).reshape(
        A_HEADS // 2, CHUNK, 2 * CHUNK)
    bs_exp = jnp.repeat(gmlp_b_s.T, HEAD_DIM, axis=1)

    oa, q, k_cmp, v_cmp, k_slc, v_slc, k_win, v_win, gates = _proj_call(
        x2, row(norm_mix), w_all, cos_t, sin_t, row(gmlp_v_norm), ws_pairs, bs_exp, row(out_norm_a), seq)

    kc, vc = _compress_call(k_cmp, v_cmp, _compress_weights(cmp_w1_k, cmp_w2_k, cmp_pe_k),
                            _compress_weights(cmp_w1_v, cmp_w2_v, cmp_pe_v), bsz, seq)

    b3 = lambda a: a.reshape(bsz, seq, a.shape[-1])
    ob = _attn_call(b3(q), b3(gates), kc, vc, b3(k_slc), b3(v_slc), b3(k_win), b3(v_win),
                    row(out_norm_b[_PERM_CH]), bsz, seq)

    r_cat = jnp.pad(jnp.concatenate([router_group, router_expert], axis=1),
                    ((0, 0), (0, GATE_PAD - N_GROUPS - N_EXPERTS)))
    h1, hn2, comb = _post_call(x2, oa, ob.reshape(n, B_WIDTH), w_o[:A_WIDTH].astype(_BF16),
                               w_o[A_WIDTH:][_PERM_CH].astype(_BF16), row(norm_moe), r_cat)

    wgu = jnp.concatenate([moe_w_gate, moe_w_up], axis=-1).astype(_BF16)
    out = _moe_call(hn2, comb, h1, p_i.reshape(n, D_PLE), wgu, moe_w_down.astype(_BF16), row(norm_ple),
                    w_ple_gate.astype(_BF16), w_ple_proj.astype(_BF16), row(norm_final))
    return out.reshape(bsz, seq, D_MODEL)


def kernel(x, p, norm_mix, w_in, gmlp_v_norm, gmlp_w_s, gmlp_b_s, cmp_pe_k, cmp_w1_k, cmp_w2_k,
           cmp_pe_v, cmp_w1_v, cmp_w2_v, out_norm_a, out_norm_b, w_o, norm_moe, router_group,
           router_expert, moe_w_gate, moe_w_up, moe_w_down, norm_ple, w_ple_proj, w_ple_gate, norm_final):
    assert p.shape[0] == 1, "single-layer trunk"
    assert x.shape[1] % SEL_CK == 0 and x.shape[1] >= WIN + Q_BLOCK
    assert (x.shape[0] * x.shape[1]) % TM_MOE == 0
    return _layer(x, p[0], norm_mix[0], w_in[0], gmlp_v_norm[0], gmlp_w_s[0], gmlp_b_s[0],
                  cmp_pe_k[0], cmp_w1_k[0], cmp_w2_k[0], cmp_pe_v[0], cmp_w1_v[0], cmp_w2_v[0],
                  out_norm_a[0], out_norm_b[0], w_o[0], norm_moe[0], router_group[0], router_expert[0],
                  moe_w_gate[0], moe_w_up[0], moe_w_down[0], norm_ple[0], w_ple_proj[0], w_ple_gate[0],
                  norm_final)
```

```python
import functools

import numpy as np
import jax
import jax.numpy as jnp
from jax import lax
from jax.experimental import pallas as pl
from jax.experimental.pallas import tpu as pltpu

D_MODEL = 1024
HEAD_DIM = 64
A_HEADS = 8
A_WIDTH = A_HEADS * HEAD_DIM
B_HEADS = 8
B_WIDTH = B_HEADS * HEAD_DIM
B_KV = 2
B_HPG = B_HEADS // B_KV
KV_W = B_KV * HEAD_DIM
N_GATES = B_HEADS * 3
CHUNK = 128
L_CMP = 32
STRIDE_CMP = 16
CMP_HIDDEN = 256
L_SEL = 64
N_SEL = 16
WIN = 512
Q_BLOCK = 128
ROPE_THETA = 10000.0
N_GROUPS = 4
EXPERTS_PER_GROUP = 4
N_EXPERTS = N_GROUPS * EXPERTS_PER_GROUP
D_FF_EXPERT = 512
D_PLE = 256
EPS = 1e-6
LOG2E = 1.4426950408889634
NEG = -1e30
FORCE = 1e6

OFF_Q = 2 * A_WIDTH
OFF_KV = OFF_Q + B_WIDTH
OFF_GATE = OFF_KV + 6 * KV_W
D_IN = OFF_GATE + N_GATES

LANES = 128
GATE_PAD = LANES
ROUTER_OFF = N_GROUPS
W_ALL = OFF_GATE + GATE_PAD

TM_PROJ = 512
TM_POST = 512
TM_MOE = 1024
SUB_MOE = 256
SEL_CK = 512
VMEM_LIMIT = 56 * 1024 * 1024

_PERM_HEADS = [0, 4, 1, 5, 2, 6, 3, 7]
_PERM_CH = np.concatenate([np.arange(HEAD_DIM) + HEAD_DIM * h for h in _PERM_HEADS])

_F32 = jnp.float32
_BF16 = jnp.bfloat16


def _dot(a, b):
    return jnp.dot(a, b, preferred_element_type=_F32)


def _dot_nt(a, b):
    return lax.dot_general(a, b, (((1,), (1,)), ((), ())), preferred_element_type=_F32)


def _rms(x, g):
    return x * lax.rsqrt(jnp.mean(x * x, axis=-1, keepdims=True) + EPS) * g


def _gelu(x):
    return 0.5 * x * (1.0 + jnp.tanh(0.7978845608028654 * (x + 0.044715 * (x * x * x))))


def _rope_tile(x, cos, sin_signed):
    lane = lax.broadcasted_iota(jnp.int32, x.shape, 1)
    first_half = (lane % HEAD_DIM) < (HEAD_DIM // 2)
    rot = jnp.where(first_half, pltpu.roll(x, LANES - HEAD_DIM // 2, 1), pltpu.roll(x, HEAD_DIM // 2, 1))
    return x * cos + rot * sin_signed


def _proj_kernel(x_ref, gmix_ref, w_ref, cos_ref, sin_ref, gv_ref, ws_ref, bs_ref, goa_ref,
                 oa_ref, q_ref, kc_ref, vc_ref, ks0_ref, ks1_ref, vs0_ref, vs1_ref, kw_ref, vw_ref, gate_ref):
    tm = x_ref.shape[0]
    hn = _rms(x_ref[...], gmix_ref[...]).astype(_BF16)
    cos = cos_ref[...]
    sin = sin_ref[...]

    zq = _dot(hn, w_ref[:, OFF_Q:OFF_KV])
    scale = HEAD_DIM ** -0.5 * LOG2E
    for j in range(B_WIDTH // LANES):
        blk = _rope_tile(zq[:, j * LANES:(j + 1) * LANES], cos, sin) * scale
        q_ref[:, j * LANES:(j + 1) * LANES] = blk.astype(q_ref.dtype)

    zkv = _dot(hn, w_ref[:, OFF_KV:OFF_GATE])
    kv = []
    for j in range(6):
        blk = zkv[:, j * KV_W:(j + 1) * KV_W]
        kv.append(_rope_tile(blk, cos, sin) if j % 2 == 0 else blk)
    k_cmp, v_cmp, k_slc, v_slc, k_win, v_win = kv
    kc_ref[...] = k_cmp.astype(kc_ref.dtype)
    vc_ref[...] = v_cmp.astype(vc_ref.dtype)
    kw_ref[...] = k_win.astype(kw_ref.dtype)
    vw_ref[...] = v_win.astype(vw_ref.dtype)

    assert tm % SEL_CK == 0
    row = lax.broadcasted_iota(jnp.int32, (tm, LANES), 0)
    lane = lax.broadcasted_iota(jnp.int32, (tm, LANES), 1)
    blk_in_chunk = (row % SEL_CK) // L_SEL
    for g, (ks_ref, vs_ref) in enumerate(((ks0_ref, vs0_ref), (ks1_ref, vs1_ref))):
        own = (lane // HEAD_DIM) == g
        onehot = jnp.where((lane % HEAD_DIM) == blk_in_chunk, 1.0, 0.0)
        ks_ref[...] = jnp.where(own, k_slc, onehot).astype(ks_ref.dtype)
        vs_ref[...] = jnp.where(own, v_slc, 1.0).astype(vs_ref.dtype)

    zg = _dot(hn, w_ref[:, OFF_GATE:W_ALL])
    gate_ref[...] = jax.nn.sigmoid(zg)

    zu = _gelu(_dot(hn, w_ref[:, 0:A_WIDTH]))
    zv = _gelu(_dot(hn, w_ref[:, A_WIDTH:2 * A_WIDTH]))
    vn = _rms(zv, gv_ref[...]).astype(_BF16)

    t_io = lax.broadcasted_iota(jnp.int32, (CHUNK, 2 * CHUNK), 0)
    s_io = lax.broadcasted_iota(jnp.int32, (CHUNK, 2 * CHUNK), 1) % CHUNK
    causal = s_io <= t_io
    lane = lax.broadcasted_iota(jnp.int32, (CHUNK, LANES), 1)
    lo = lane < HEAD_DIM
    bs = bs_ref[...]
    chunks = []
    for c in range(tm // CHUNK):
        tiles = []
        for pr in range(A_HEADS // 2):
            wcat = jnp.where(causal, ws_ref[pr], 0.0).astype(_BF16)
            vblk = vn[c * CHUNK:(c + 1) * CHUNK, pr * LANES:(pr + 1) * LANES]
            zero = jnp.zeros_like(vblk)
            rhs = jnp.concatenate([jnp.where(lo, vblk, zero), jnp.where(lo, zero, vblk)], axis=0)
            tiles.append(_dot(wcat, rhs))
        chunks.append(jnp.concatenate(tiles, axis=1) + bs)
    mixed = jnp.concatenate(chunks, axis=0)
    oa = zu * mixed
    oa_ref[...] = _rms(oa, goa_ref[...]).astype(oa_ref.dtype)


def _proj_call(x2, gmix, w_all, cos_t, sin_t, gv, ws_pairs, bs_exp, goa, seq):
    n = x2.shape[0]
    tm = TM_PROJ
    n_t = seq // tm
    row = lambda i: (i, 0)
    const2 = lambda i: (0, 0)
    pos = lambda i: (i % n_t, 0)
    out_shapes = [jax.ShapeDtypeStruct((n, A_WIDTH), _BF16), jax.ShapeDtypeStruct((n, B_WIDTH), _BF16)]
    out_shapes += [jax.ShapeDtypeStruct((n, KV_W), _BF16)] * 8
    out_shapes += [jax.ShapeDtypeStruct((n, GATE_PAD), _F32)]
    out_specs = [pl.BlockSpec((tm, A_WIDTH), row), pl.BlockSpec((tm, B_WIDTH), row)]
    out_specs += [pl.BlockSpec((tm, KV_W), row)] * 8
    out_specs += [pl.BlockSpec((tm, GATE_PAD), row)]
    return pl.pallas_call(
        _proj_kernel,
        grid=(n // tm,),
        in_specs=[
            pl.BlockSpec((tm, D_MODEL), row),
            pl.BlockSpec((1, D_MODEL), const2),
            pl.BlockSpec((D_MODEL, W_ALL), const2),
            pl.BlockSpec((tm, LANES), pos),
            pl.BlockSpec((tm, LANES), pos),
            pl.BlockSpec((1, A_WIDTH), const2),
            pl.BlockSpec((A_HEADS // 2, CHUNK, 2 * CHUNK), lambda i: (0, 0, 0)),
            pl.BlockSpec((CHUNK, A_WIDTH), const2),
            pl.BlockSpec((1, A_WIDTH), const2),
        ],
        out_specs=out_specs,
        out_shape=out_shapes,
        compiler_params=pltpu.CompilerParams(dimension_semantics=("arbitrary",), vmem_limit_bytes=VMEM_LIMIT),
        name="proj",
    )(x2, gmix, w_all, cos_t, sin_t, gv, ws_pairs, bs_exp, goa)


def _compress_kernel(rk_ref, rv_ref, pek_ref, pev_ref, w1k_ref, w1v_ref,
                     tk_ref, bk_ref, w2k_ref, tv_ref, bv_ref, w2v_ref, kc_ref, vc_ref):
    def one(r_ref, pe_ref, w1_ref, top_ref, bot_ref, w2_ref, o_ref):
        r = r_ref[0]
        nr = r.shape[0]
        a = _dot(r, top_ref[...])
        b = _dot(r, bot_ref[...])
        pe_h = _dot(pe_ref[...], w1_ref[...])
        pe2 = jnp.concatenate([pe_h[0:1], pe_h[0:1]], axis=1)
        hid = a + pltpu.roll(b, nr - 1, 0) + pe2
        o_ref[0] = _dot(_gelu(hid).astype(_BF16), w2_ref[...]).astype(o_ref.dtype)

    one(rk_ref, pek_ref, w1k_ref, tk_ref, bk_ref, w2k_ref, kc_ref)
    one(rv_ref, pev_ref, w1v_ref, tv_ref, bv_ref, w2v_ref, vc_ref)


def _compress_weights(w1, w2, pe):
    half = L_CMP // 2
    w1r = w1.reshape(L_CMP, HEAD_DIM, CMP_HIDDEN)
    eye = jnp.eye(B_KV, dtype=w1.dtype)
    place = lambda part: jnp.einsum('ldj,gh->lgdhj', part, eye).reshape(half * KV_W, B_KV * CMP_HIDDEN)
    top = place(w1r[:half]).astype(_BF16)
    bot = place(w1r[half:]).astype(_BF16)
    w2bd = jnp.einsum('jd,gh->gjhd', w2, eye).reshape(B_KV * CMP_HIDDEN, KV_W).astype(_BF16)
    pe8 = jnp.broadcast_to(pe.reshape(1, L_CMP * HEAD_DIM), (8, L_CMP * HEAD_DIM)).astype(_BF16)
    return pe8, w1.astype(_BF16), top, bot, w2bd


def _compress_call(k_cmp, v_cmp, wk, wv, bsz, seq):
    nr = seq // STRIDE_CMP
    rk = k_cmp.reshape(bsz, nr, STRIDE_CMP * KV_W)
    rv = v_cmp.reshape(bsz, nr, STRIDE_CMP * KV_W)
    pek, w1k, tk, bk, w2k = wk
    pev, w1v, tv, bv, w2v = wv
    rspec = pl.BlockSpec((1, nr, STRIDE_CMP * KV_W), lambda b: (b, 0, 0))
    full = lambda a: pl.BlockSpec(a.shape, lambda b: (0,) * a.ndim)
    ospec = pl.BlockSpec((1, nr, KV_W), lambda b: (b, 0, 0))
    return pl.pallas_call(
        _compress_kernel,
        grid=(bsz,),
        in_specs=[rspec, rspec, full(pek), full(pev), full(w1k), full(w1v),
                  full(tk), full(bk), full(w2k), full(tv), full(bv), full(w2v)],
        out_specs=[ospec, ospec],
        out_shape=[jax.ShapeDtypeStruct((bsz, nr, KV_W), _BF16)] * 2,
        compiler_params=pltpu.CompilerParams(dimension_semantics=("arbitrary",), vmem_limit_bytes=VMEM_LIMIT),
        name="compress",
    )(rk, rv, pek, pev, w1k, w1v, tk, bk, w2k, tv, bv, w2v)


def _topk_rows_mask(sc_t, k):
    n_rows = sc_t.shape[0]
    row = lax.broadcasted_iota(jnp.int32, sc_t.shape, 0)
    sel = jnp.zeros(sc_t.shape, _F32)
    for _ in range(k):
        m = jnp.max(sc_t, axis=0, keepdims=True)
        idx = jnp.min(jnp.where(sc_t == m, row, n_rows), axis=0, keepdims=True)
        chosen = row == idx
        sel = jnp.where(chosen, 1.0, sel)
        sc_t = jnp.where(chosen, -3e38, sc_t)
    return sel


def _attn_kernel(q_ref, gate_ref, gexp_ref, kc_ref, vc_ref, ks0_ref, ks1_ref, vs0_ref, vs1_ref,
                 kw_ref, vw_ref, gob_ref, o_ref, *, seq):
    qb = pl.program_id(1)
    t0 = qb * Q_BLOCK
    n_cmp = kc_ref.shape[1]
    n_sb = seq // L_SEL
    k_top = min(N_SEL, n_sb)
    rows = B_HPG * Q_BLOCK
    blk_per_ck = SEL_CK // L_SEL
    ks_refs, vs_refs = (ks0_ref, ks1_ref), (vs0_ref, vs1_ref)

    lane_q = lax.broadcasted_iota(jnp.int32, (Q_BLOCK, LANES), 1)
    lo = lane_q < HEAD_DIM
    t_col = t0 + lax.broadcasted_iota(jnp.int32, (Q_BLOCK, 1), 0)

    def per_head(x):
        return x.reshape(B_HPG, Q_BLOCK, x.shape[-1])

    def add_bias(s, bias):
        return (per_head(s) + bias[None]).reshape(rows, s.shape[-1])

    qs = []
    for g in range(B_KV):
        own = (lane_q // HEAD_DIM) == g
        qs.append(jnp.concatenate(
            [jnp.where(own, q_ref[0, :, j * LANES:(j + 1) * LANES], jnp.zeros((), q_ref.dtype))
             for j in range(B_HPG)], axis=0))

    c_lane = lax.broadcasted_iota(jnp.int32, (Q_BLOCK, n_cmp), 1)
    bias_c = jnp.where(c_lane * STRIDE_CMP + (L_CMP - 1) <= t_col, 0.0, NEG)
    has_c = (t_col >= L_CMP - 1).astype(_F32)
    c_row = lax.broadcasted_iota(jnp.int32, (n_cmp, n_sb), 0) * STRIDE_CMP
    s_col = lax.broadcasted_iota(jnp.int32, (n_cmp, n_sb), 1) * L_SEL
    overlap = jnp.where((c_row < s_col + L_SEL) & (c_row + L_CMP > s_col), 1.0, 0.0).astype(_BF16)
    blk = lax.broadcasted_iota(jnp.int32, (Q_BLOCK, n_sb), 1)
    cur = t_col // L_SEL
    forced = (blk == 0) | (blk == cur) | (blk == cur - 1)
    valid = blk * L_SEL <= t_col

    o_c, sel_bias = [], []
    for g in range(B_KV):
        s_c = add_bias(_dot_nt(qs[g], kc_ref[0]), bias_c)
        e_c = jnp.exp2(s_c - jnp.max(s_c, axis=-1, keepdims=True))
        inv = per_head(1.0 / jnp.maximum(jnp.sum(e_c, axis=-1, keepdims=True), 1e-30)) * has_c[None]
        p_c = (per_head(e_c) * inv).reshape(rows, n_cmp)
        o_c.append(_dot(p_c.astype(_BF16), vc_ref[0]))

        p_sum = jnp.sum(per_head(p_c), axis=0)
        p_hi = p_sum.astype(_BF16)
        p_lo = (p_sum - p_hi.astype(_F32)).astype(_BF16)
        imp = _dot(p_hi, overlap) + _dot(p_lo, overlap)
        score = jnp.where(forced, FORCE, jnp.where(valid, imp, -FORCE))
        chosen = _topk_rows_mask(score.T, k_top).T > 0.5
        sb = jnp.where(chosen & valid, 0.0, NEG)
        if n_sb < LANES:
            sb = jnp.concatenate([sb, jnp.full((Q_BLOCK, LANES - n_sb), NEG, _F32)], axis=1)
        sel_bias.append(sb)

    n_ck = (t0 + Q_BLOCK + SEL_CK - 1) // SEL_CK
    key_lane = lax.broadcasted_iota(jnp.int32, (Q_BLOCK, SEL_CK), 1)
    bias_diag = jnp.where((n_ck - 1) * SEL_CK + key_lane <= t_col, 0.0, NEG)

    def sel_chunk(ci, carry, diag):
        k0 = pl.multiple_of(ci * SEL_CK, SEL_CK)
        out = []
        for g in range(B_KV):
            m, acc = carry[g]
            aug0 = (1 - g) * HEAD_DIM
            shift = (aug0 + LANES - ci * blk_per_ck) % LANES
            in_aug = (lane_q >= aug0) & (lane_q < aug0 + blk_per_ck)
            aug = jnp.where(in_aug, pltpu.roll(sel_bias[g], shift, 1), 0.0).astype(_BF16)
            q_aug = qs[g] + jnp.concatenate([aug] * B_HPG, axis=0)
            s = _dot_nt(q_aug, ks_refs[g][0, pl.ds(k0, SEL_CK), :])
            if diag:
                s = add_bias(s, bias_diag)
            m_new = jnp.maximum(m, jnp.max(s, axis=-1, keepdims=True))
            p = jnp.exp2(s - m_new).astype(_BF16)
            acc = jnp.exp2(m - m_new) * acc + _dot(p, vs_refs[g][0, pl.ds(k0, SEL_CK), :])
            out.append((m_new, acc))
        return tuple(out)

    init = tuple((jnp.full((rows, 1), NEG, _F32), jnp.zeros((rows, LANES), _F32)) for _ in range(B_KV))
    carry = lax.fori_loop(0, n_ck - 1, lambda ci, c: sel_chunk(ci, c, False), init)
    carry = sel_chunk(n_ck - 1, carry, True)
    acc_s = [carry[g][1] for g in range(B_KV)]

    w_start = pl.multiple_of(jnp.maximum(t0 - WIN, 0), Q_BLOCK)
    w_len = WIN + Q_BLOCK
    kpos_w = w_start + lax.broadcasted_iota(jnp.int32, (Q_BLOCK, w_len), 1)
    diff_w = t_col - kpos_w
    bias_w = jnp.where((diff_w >= 0) & (diff_w < WIN), 0.0, NEG)
    kw = kw_ref[0, pl.ds(w_start, w_len), :]
    vw = vw_ref[0, pl.ds(w_start, w_len), :]
    lane_w = lax.broadcasted_iota(jnp.int32, (w_len, LANES), 1)
    acc_w = []
    for g in range(B_KV):
        s_w = add_bias(_dot_nt(qs[g], kw), bias_w)
        e_w = jnp.exp2(s_w - jnp.max(s_w, axis=-1, keepdims=True)).astype(_BF16)
        vw_aug = jnp.where((lane_w // HEAD_DIM) == g, vw, jnp.ones((), vw.dtype))
        acc_w.append(_dot(e_w, vw_aug))

    def numer(acc):
        return jnp.concatenate([jnp.where(lo, acc[0][j * Q_BLOCK:(j + 1) * Q_BLOCK],
                                          acc[1][j * Q_BLOCK:(j + 1) * Q_BLOCK]) for j in range(B_HPG)], axis=1)

    def denom(acc):
        return jnp.concatenate([pltpu.roll(jnp.where(lo, acc[1][j * Q_BLOCK:(j + 1) * Q_BLOCK],
                                                     acc[0][j * Q_BLOCK:(j + 1) * Q_BLOCK]), HEAD_DIM, 1)
                                for j in range(B_HPG)], axis=1)

    gates = gate_ref[0]
    g_hi = gates.astype(_BF16)
    g_lo = (gates - g_hi.astype(_F32)).astype(_BF16)
    gate_of = lambda r: _dot(g_hi, gexp_ref[r]) + _dot(g_lo, gexp_ref[r])
    ob = (gate_of(0) * numer(o_c)
          + gate_of(1) * numer(acc_s) * (1.0 / jnp.maximum(denom(acc_s), 1e-30))
          + gate_of(2) * numer(acc_w) * (1.0 / jnp.maximum(denom(acc_w), 1e-30)))
    o_ref[0] = _rms(ob, gob_ref[...]).astype(o_ref.dtype)


def _gate_expand():
    x = np.zeros((3, GATE_PAD, B_WIDTH), np.float32)
    for slot, h in enumerate(_PERM_HEADS):
        for r in range(3):
            x[r, 3 * h + r, slot * HEAD_DIM:(slot + 1) * HEAD_DIM] = 1.0
    return jnp.asarray(x, _BF16)


def _attn_call(q, gates, kc, vc, ks0, ks1, vs0, vs1, kw, vw, gob, bsz, seq):
    assert seq // L_SEL <= LANES and SEL_CK // L_SEL <= HEAD_DIM
    n_cmp = kc.shape[1]
    qspec = lambda w: pl.BlockSpec((1, Q_BLOCK, w), lambda b, i: (b, i, 0))
    full = lambda r: pl.BlockSpec((1, r, KV_W), lambda b, i: (b, 0, 0))
    return pl.pallas_call(
        functools.partial(_attn_kernel, seq=seq),
        grid=(bsz, seq // Q_BLOCK),
        in_specs=[qspec(B_WIDTH), qspec(GATE_PAD),
                  pl.BlockSpec((3, GATE_PAD, B_WIDTH), lambda b, i: (0, 0, 0)),
                  full(n_cmp), full(n_cmp), full(seq), full(seq), full(seq), full(seq), full(seq), full(seq),
                  pl.BlockSpec((1, B_WIDTH), lambda b, i: (0, 0))],
        out_specs=qspec(B_WIDTH),
        out_shape=jax.ShapeDtypeStruct((bsz, seq, B_WIDTH), _BF16),
        compiler_params=pltpu.CompilerParams(dimension_semantics=("arbitrary", "arbitrary"),
                                             vmem_limit_bytes=VMEM_LIMIT),
        name="attn",
    )(q, gates, _gate_expand(), kc, vc, ks0, ks1, vs0, vs1, kw, vw, gob)


def _post_kernel(x_ref, oa_ref, ob_ref, woa_ref, wob_ref, gmoe_ref, r_ref, h1_ref, hn_ref, comb_ref):
    h1 = x_ref[...] + _dot(oa_ref[...], woa_ref[...]) + _dot(ob_ref[...], wob_ref[...])
    h1_ref[...] = h1
    hn = _rms(h1, gmoe_ref[...])
    hn_ref[...] = hn.astype(hn_ref.dtype)

    logits = jnp.dot(hn, r_ref[...], preferred_element_type=_F32, precision=lax.Precision.HIGHEST)
    lane = lax.broadcasted_iota(jnp.int32, logits.shape, 1)
    first_idx = lambda hit: jnp.min(jnp.where(hit, lane, LANES), axis=-1, keepdims=True)

    is_g = lane < N_GROUPS
    lg = jnp.where(is_g, logits, NEG)
    mg = jnp.max(lg, axis=-1, keepdims=True)
    sg = jnp.sum(jnp.where(is_g, jnp.exp(lg - mg), 0.0), axis=-1, keepdims=True)
    pg_top = 1.0 / sg
    g_sel = first_idx(is_g & (lg == mg))

    e_lo = ROUTER_OFF + g_sel * EXPERTS_PER_GROUP
    is_e = (lane >= e_lo) & (lane < e_lo + EXPERTS_PER_GROUP)
    le = jnp.where(is_e, logits, NEG)
    m1 = jnp.max(le, axis=-1, keepdims=True)
    se = jnp.sum(jnp.where(is_e, jnp.exp(le - m1), 0.0), axis=-1, keepdims=True)
    i1 = first_idx(is_e & (le == m1))
    le2 = jnp.where(lane == i1, NEG, le)
    m2 = jnp.max(le2, axis=-1, keepdims=True)
    i2 = first_idx(is_e & (lane != i1) & (le2 == m2))
    pe1 = 1.0 / se
    pe2 = jnp.exp(m2 - m1) / se
    denom = pe1 + pe2
    comb_ref[...] = (jnp.where(lane == i1, pg_top * pe1 / denom, 0.0)
                     + jnp.where(lane == i2, pg_top * pe2 / denom, 0.0))


def _post_call(x2, oa, ob, woa, wob, gmoe, r_cat):
    n = x2.shape[0]
    tm = TM_POST
    row = lambda i: (i, 0)
    const2 = lambda i: (0, 0)
    return pl.pallas_call(
        _post_kernel,
        grid=(n // tm,),
        in_specs=[pl.BlockSpec((tm, D_MODEL), row), pl.BlockSpec((tm, A_WIDTH), row),
                  pl.BlockSpec((tm, B_WIDTH), row), pl.BlockSpec((A_WIDTH, D_MODEL), const2),
                  pl.BlockSpec((B_WIDTH, D_MODEL), const2), pl.BlockSpec((1, D_MODEL), const2),
                  pl.BlockSpec((D_MODEL, GATE_PAD), const2)],
        out_specs=[pl.BlockSpec((tm, D_MODEL), row), pl.BlockSpec((tm, D_MODEL), row),
                   pl.BlockSpec((tm, GATE_PAD), row)],
        out_shape=[jax.ShapeDtypeStruct((n, D_MODEL), _F32), jax.ShapeDtypeStruct((n, D_MODEL), _BF16),
                   jax.ShapeDtypeStruct((n, GATE_PAD), _F32)],
        compiler_params=pltpu.CompilerParams(dimension_semantics=("arbitrary",), vmem_limit_bytes=VMEM_LIMIT),
        name="post",
    )(x2, oa, ob, woa, wob, gmoe, r_cat)


def _moe_kernel(hn_ref, comb_ref, h1_ref, p_ref, wgu_ref, wd_ref, gple_ref, wpg_ref, wpp_ref, gfin_ref,
                o_ref, acc_ref):
    e = pl.program_id(1)
    n_sub = hn_ref.shape[0] // SUB_MOE

    @pl.when(e == 0)
    def _():
        acc_ref[...] = jnp.zeros_like(acc_ref)

    def sub(r, carry):
        r0 = pl.multiple_of(r * SUB_MOE, SUB_MOE)
        rows = pl.ds(r0, SUB_MOE)
        gu = _dot(hn_ref[rows, :], wgu_ref[0])
        gate = gu[:, :D_FF_EXPERT]
        hid = gate * jax.nn.sigmoid(gate) * gu[:, D_FF_EXPERT:]
        comb = comb_ref[rows, :]
        lane = lax.broadcasted_iota(jnp.int32, comb.shape, 1)
        w_e = jnp.sum(jnp.where(lane == ROUTER_OFF + e, comb, 0.0), axis=-1, keepdims=True)
        acc_ref[rows, :] += _dot((hid * w_e).astype(_BF16), wd_ref[0])
        return carry

    lax.fori_loop(0, n_sub, sub, 0)

    @pl.when(e == N_EXPERTS - 1)
    def _():
        def fin(r, carry):
            r0 = pl.multiple_of(r * SUB_MOE, SUB_MOE)
            rows = pl.ds(r0, SUB_MOE)
            h2 = h1_ref[rows, :] + acc_ref[rows, :]
            gate = jax.nn.sigmoid(_dot(_rms(h2, gple_ref[...]).astype(_BF16), wpg_ref[...]))
            h3 = h2 + _dot(p_ref[rows, :].astype(_BF16), wpp_ref[...]) * gate
            o_ref[rows, :] = _rms(h3, gfin_ref[...])
            return carry

        lax.fori_loop(0, n_sub, fin, 0)


def _moe_call(hn2, comb, h1, p2, wgu, wd, gple, wpg, wpp, gfin):
    n = hn2.shape[0]
    tm = TM_MOE
    row = lambda i, e: (i, 0)
    const2 = lambda i, e: (0, 0)
    return pl.pallas_call(
        _moe_kernel,
        grid=(n // tm, N_EXPERTS),
        in_specs=[pl.BlockSpec((tm, D_MODEL), row), pl.BlockSpec((tm, GATE_PAD), row),
                  pl.BlockSpec((tm, D_MODEL), row), pl.BlockSpec((tm, D_PLE), row),
                  pl.BlockSpec((1, D_MODEL, 2 * D_FF_EXPERT), lambda i, e: (e, 0, 0)),
                  pl.BlockSpec((1, D_FF_EXPERT, D_MODEL), lambda i, e: (e, 0, 0)),
                  pl.BlockSpec((1, D_MODEL), const2), pl.BlockSpec((D_MODEL, D_MODEL), const2),
                  pl.BlockSpec((D_PLE, D_MODEL), const2), pl.BlockSpec((1, D_MODEL), const2)],
        out_specs=pl.BlockSpec((tm, D_MODEL), row),
        out_shape=jax.ShapeDtypeStruct((n, D_MODEL), _F32),
        scratch_shapes=[pltpu.VMEM((tm, D_MODEL), _F32)],
        compiler_params=pltpu.CompilerParams(dimension_semantics=("arbitrary", "arbitrary"),
                                             vmem_limit_bytes=VMEM_LIMIT),
        name="moe",
    )(hn2, comb, h1, p2, wgu, wd, gple, wpg, wpp, gfin)


def _rope_tables(seq):
    half = HEAD_DIM // 2
    inv = 1.0 / (ROPE_THETA ** (jnp.arange(half, dtype=_F32) / half))
    ang = jnp.arange(seq, dtype=_F32)[:, None] * inv[None, :]
    cos, sin = jnp.cos(ang), jnp.sin(ang)
    reps = LANES // HEAD_DIM
    cos_t = jnp.tile(jnp.concatenate([cos, cos], axis=1), (1, reps))
    sin_t = jnp.tile(jnp.concatenate([-sin, sin], axis=1), (1, reps))
    return cos_t, sin_t


def _layer(h, p_i, norm_mix, w_in, gmlp_v_norm, gmlp_w_s, gmlp_b_s,
           cmp_pe_k, cmp_w1_k, cmp_w2_k, cmp_pe_v, cmp_w1_v, cmp_w2_v,
           out_norm_a, out_norm_b, w_o, norm_moe, router_group, router_expert,
           moe_w_gate, moe_w_up, moe_w_down, norm_ple, w_ple_proj, w_ple_gate, norm_final):
    bsz, seq, _ = h.shape
    n = bsz * seq
    x2 = h.reshape(n, D_MODEL)
    row = lambda v: v.reshape(1, -1).astype(_F32)

    w_q = w_in[:, OFF_Q:OFF_KV][:, _PERM_CH]
    w_gate = jnp.pad(w_in[:, OFF_GATE:D_IN], ((0, 0), (0, GATE_PAD - N_GATES)))
    w_all = jnp.concatenate([w_in[:, :OFF_Q], w_q, w_in[:, OFF_KV:OFF_GATE], w_gate], axis=1).astype(_BF16)
    cos_t, sin_t = _rope_tables(seq)
    ws_pairs = gmlp_w_s.reshape(A_HEADS // 2, 2, CHUNK, CHUNK).transpose(0, 2, 1, 3).reshape(
        A_HEADS // 2, CHUNK, 2 * CHUNK)
    bs_exp = jnp.repeat(gmlp_b_s.T, HEAD_DIM, axis=1)

    oa, q, k_cmp, v_cmp, ks0, ks1, vs0, vs1, k_win, v_win, gates = _proj_call(
        x2, row(norm_mix), w_all, cos_t, sin_t, row(gmlp_v_norm), ws_pairs, bs_exp, row(out_norm_a), seq)

    kc, vc = _compress_call(k_cmp, v_cmp, _compress_weights(cmp_w1_k, cmp_w2_k, cmp_pe_k),
                            _compress_weights(cmp_w1_v, cmp_w2_v, cmp_pe_v), bsz, seq)

    b3 = lambda a: a.reshape(bsz, seq, a.shape[-1])
    ob = _attn_call(b3(q), b3(gates), kc, vc, b3(ks0), b3(ks1), b3(vs0), b3(vs1), b3(k_win), b3(v_win),
                    row(out_norm_b[_PERM_CH]), bsz, seq)

    r_cat = jnp.pad(jnp.concatenate([router_group, router_expert], axis=1),
                    ((0, 0), (0, GATE_PAD - N_GROUPS - N_EXPERTS)))
    h1, hn2, comb = _post_call(x2, oa, ob.reshape(n, B_WIDTH), w_o[:A_WIDTH].astype(_BF16),
                               w_o[A_WIDTH:][_PERM_CH].astype(_BF16), row(norm_moe), r_cat)

    wgu = jnp.concatenate([moe_w_gate, moe_w_up], axis=-1).astype(_BF16)
    out = _moe_call(hn2, comb, h1, p_i.reshape(n, D_PLE), wgu, moe_w_down.astype(_BF16), row(norm_ple),
                    w_ple_gate.astype(_BF16), w_ple_proj.astype(_BF16), row(norm_final))
    return out.reshape(bsz, seq, D_MODEL)


def kernel(x, p, norm_mix, w_in, gmlp_v_norm, gmlp_w_s, gmlp_b_s, cmp_pe_k, cmp_w1_k, cmp_w2_k,
           cmp_pe_v, cmp_w1_v, cmp_w2_v, out_norm_a, out_norm_b, w_o, norm_moe, router_group,
           router_expert, moe_w_gate, moe_w_up, moe_w_down, norm_ple, w_ple_proj, w_ple_gate, norm_final):
    assert p.shape[0] == 1, "single-layer trunk"
    assert x.shape[1] % SEL_CK == 0 and x.shape[1] >= WIN + Q_BLOCK
    assert (x.shape[0] * x.shape[1]) % TM_MOE == 0
    return _layer(x, p[0], norm_mix[0], w_in[0], gmlp_v_norm[0], gmlp_w_s[0], gmlp_b_s[0],
                  cmp_pe_k[0], cmp_w1_k[0], cmp_w2_k[0], cmp_pe_v[0], cmp_w1_v[0], cmp_w2_v[0],
                  out_norm_a[0], out_norm_b[0], w_o[0], norm_moe[0], router_group[0], router_expert[0],
                  moe_w_gate[0], moe_w_up[0], moe_w_down[0], norm_ple[0], w_ple_proj[0], w_ple_gate[0],
                  norm_final)
```

```python
import functools

import numpy as np
import jax
import jax.numpy as jnp
from jax import lax
from jax.experimental import pallas as pl
from jax.experimental.pallas import tpu as pltpu

D_MODEL = 1024
HEAD_DIM = 64
A_HEADS = 8
A_WIDTH = A_HEADS * HEAD_DIM
B_HEADS = 8
B_WIDTH = B_HEADS * HEAD_DIM
B_KV = 2
B_HPG = B_HEADS // B_KV
KV_W = B_KV * HEAD_DIM
N_GATES = B_HEADS * 3
CHUNK = 128
L_CMP = 32
STRIDE_CMP = 16
CMP_HIDDEN = 256
L_SEL = 64
N_SEL = 16
WIN = 512
Q_BLOCK = 128
ROPE_THETA = 10000.0
N_GROUPS = 4
EXPERTS_PER_GROUP = 4
N_EXPERTS = N_GROUPS * EXPERTS_PER_GROUP
D_FF_EXPERT = 512
D_PLE = 256
EPS = 1e-6
LOG2E = 1.4426950408889634
NEG = -1e30
FORCE = 1e6

OFF_Q = 2 * A_WIDTH
OFF_KV = OFF_Q + B_WIDTH
OFF_GATE = OFF_KV + 6 * KV_W
D_IN = OFF_GATE + N_GATES

LANES = 128
GATE_PAD = LANES
ROUTER_OFF = N_GROUPS
W_ALL = OFF_GATE + GATE_PAD

TM_PROJ = 512
TM_POST = 512
TM_MOE = 1024
SUB_MOE = 256
SEL_CK = 512
VMEM_LIMIT = 56 * 1024 * 1024

_PERM_HEADS = [0, 4, 1, 5, 2, 6, 3, 7]
_PERM_CH = np.concatenate([np.arange(HEAD_DIM) + HEAD_DIM * h for h in _PERM_HEADS])

_F32 = jnp.float32
_BF16 = jnp.bfloat16


def _dot(a, b):
    return jnp.dot(a, b, preferred_element_type=_F32)


def _dot_nt(a, b):
    return lax.dot_general(a, b, (((1,), (1,)), ((), ())), preferred_element_type=_F32)


def _rms(x, g):
    return x * lax.rsqrt(jnp.mean(x * x, axis=-1, keepdims=True) + EPS) * g


def _gelu(x):
    return 0.5 * x * (1.0 + jnp.tanh(0.7978845608028654 * (x + 0.044715 * (x * x * x))))


def _rope_tile(x, cos, sin_signed):
    lane = lax.broadcasted_iota(jnp.int32, x.shape, 1)
    first_half = (lane % HEAD_DIM) < (HEAD_DIM // 2)
    rot = jnp.where(first_half, pltpu.roll(x, LANES - HEAD_DIM // 2, 1), pltpu.roll(x, HEAD_DIM // 2, 1))
    return x * cos + rot * sin_signed


def _proj_kernel(x_ref, gmix_ref, w_ref, cos_ref, sin_ref, gv_ref, ws_ref, bs_ref, goa_ref,
                 oa_ref, q_ref, kc_ref, vc_ref, ks0_ref, ks1_ref, vs0_ref, vs1_ref, kw_ref, vw_ref, gate_ref):
    tm = x_ref.shape[0]
    hn = _rms(x_ref[...], gmix_ref[...]).astype(_BF16)
    cos = cos_ref[...]
    sin = sin_ref[...]

    zq = _dot(hn, w_ref[:, OFF_Q:OFF_KV])
    scale = HEAD_DIM ** -0.5 * LOG2E
    for j in range(B_WIDTH // LANES):
        blk = _rope_tile(zq[:, j * LANES:(j + 1) * LANES], cos, sin) * scale
        q_ref[:, j * LANES:(j + 1) * LANES] = blk.astype(q_ref.dtype)

    zkv = _dot(hn, w_ref[:, OFF_KV:OFF_GATE])
    kv = []
    for j in range(6):
        blk = zkv[:, j * KV_W:(j + 1) * KV_W]
        kv.append(_rope_tile(blk, cos, sin) if j % 2 == 0 else blk)
    k_cmp, v_cmp, k_slc, v_slc, k_win, v_win = kv
    kc_ref[...] = k_cmp.astype(kc_ref.dtype)
    vc_ref[...] = v_cmp.astype(vc_ref.dtype)
    kw_ref[...] = k_win.astype(kw_ref.dtype)
    vw_ref[...] = v_win.astype(vw_ref.dtype)

    assert tm % SEL_CK == 0
    row = lax.broadcasted_iota(jnp.int32, (tm, LANES), 0)
    lane = lax.broadcasted_iota(jnp.int32, (tm, LANES), 1)
    blk_in_chunk = (row % SEL_CK) // L_SEL
    for g, (ks_ref, vs_ref) in enumerate(((ks0_ref, vs0_ref), (ks1_ref, vs1_ref))):
        own = (lane // HEAD_DIM) == g
        onehot = jnp.where((lane % HEAD_DIM) == blk_in_chunk, 1.0, 0.0)
        ks_ref[...] = jnp.where(own, k_slc, onehot).astype(ks_ref.dtype)
        vs_ref[...] = jnp.where(own, v_slc, 1.0).astype(vs_ref.dtype)

    zg = _dot(hn, w_ref[:, OFF_GATE:W_ALL])
    gate_ref[...] = jax.nn.sigmoid(zg)

    zu = _gelu(_dot(hn, w_ref[:, 0:A_WIDTH]))
    zv = _gelu(_dot(hn, w_ref[:, A_WIDTH:2 * A_WIDTH]))
    vn = _rms(zv, gv_ref[...]).astype(_BF16)

    t_io = lax.broadcasted_iota(jnp.int32, (CHUNK, 2 * CHUNK), 0)
    s_io = lax.broadcasted_iota(jnp.int32, (CHUNK, 2 * CHUNK), 1) % CHUNK
    causal = s_io <= t_io
    lane = lax.broadcasted_iota(jnp.int32, (CHUNK, LANES), 1)
    lo = lane < HEAD_DIM
    bs = bs_ref[...]
    chunks = []
    for c in range(tm // CHUNK):
        tiles = []
        for pr in range(A_HEADS // 2):
            wcat = jnp.where(causal, ws_ref[pr], 0.0).astype(_BF16)
            vblk = vn[c * CHUNK:(c + 1) * CHUNK, pr * LANES:(pr + 1) * LANES]
            zero = jnp.zeros_like(vblk)
            rhs = jnp.concatenate([jnp.where(lo, vblk, zero), jnp.where(lo, zero, vblk)], axis=0)
            tiles.append(_dot(wcat, rhs))
        chunks.append(jnp.concatenate(tiles, axis=1) + bs)
    mixed = jnp.concatenate(chunks, axis=0)
    oa = zu * mixed
    oa_ref[...] = _rms(oa, goa_ref[...]).astype(oa_ref.dtype)


def _proj_call(x2, gmix, w_all, cos_t, sin_t, gv, ws_pairs, bs_exp, goa, seq):
    n = x2.shape[0]
    tm = TM_PROJ
    n_t = seq // tm
    row = lambda i: (i, 0)
    const2 = lambda i: (0, 0)
    pos = lambda i: (i % n_t, 0)
    out_shapes = [jax.ShapeDtypeStruct((n, A_WIDTH), _BF16), jax.ShapeDtypeStruct((n, B_WIDTH), _BF16)]
    out_shapes += [jax.ShapeDtypeStruct((n, KV_W), _BF16)] * 8
    out_shapes += [jax.ShapeDtypeStruct((n, GATE_PAD), _F32)]
    out_specs = [pl.BlockSpec((tm, A_WIDTH), row), pl.BlockSpec((tm, B_WIDTH), row)]
    out_specs += [pl.BlockSpec((tm, KV_W), row)] * 8
    out_specs += [pl.BlockSpec((tm, GATE_PAD), row)]
    return pl.pallas_call(
        _proj_kernel,
        grid=(n // tm,),
        in_specs=[
            pl.BlockSpec((tm, D_MODEL), row),
            pl.BlockSpec((1, D_MODEL), const2),
            pl.BlockSpec((D_MODEL, W_ALL), const2),
            pl.BlockSpec((tm, LANES), pos),
            pl.BlockSpec((tm, LANES), pos),
            pl.BlockSpec((1, A_WIDTH), const2),
            pl.BlockSpec((A_HEADS // 2, CHUNK, 2 * CHUNK), lambda i: (0, 0, 0)),
            pl.BlockSpec((CHUNK, A_WIDTH), const2),
            pl.BlockSpec((1, A_WIDTH), const2),
        ],
        out_specs=out_specs,
        out_shape=out_shapes,
        compiler_params=pltpu.CompilerParams(dimension_semantics=("arbitrary",), vmem_limit_bytes=VMEM_LIMIT),
        name="proj",
    )(x2, gmix, w_all, cos_t, sin_t, gv, ws_pairs, bs_exp, goa)


def _compress_kernel(rk_ref, rv_ref, pek_ref, pev_ref, w1k_ref, w1v_ref,
                     tk_ref, bk_ref, w2k_ref, tv_ref, bv_ref, w2v_ref, kc_ref, vc_ref):
    def one(r_ref, pe_ref, w1_ref, top_ref, bot_ref, w2_ref, o_ref):
        r = r_ref[0]
        nr = r.shape[0]
        a = _dot(r, top_ref[...])
        b = _dot(r, bot_ref[...])
        pe_h = _dot(pe_ref[...], w1_ref[...])
        pe2 = jnp.concatenate([pe_h[0:1], pe_h[0:1]], axis=1)
        hid = a + pltpu.roll(b, nr - 1, 0) + pe2
        o_ref[0] = _dot(_gelu(hid).astype(_BF16), w2_ref[...]).astype(o_ref.dtype)

    one(rk_ref, pek_ref, w1k_ref, tk_ref, bk_ref, w2k_ref, kc_ref)
    one(rv_ref, pev_ref, w1v_ref, tv_ref, bv_ref, w2v_ref, vc_ref)


def _compress_weights(w1, w2, pe):
    half = L_CMP // 2
    w1r = w1.reshape(L_CMP, HEAD_DIM, CMP_HIDDEN)
    eye = jnp.eye(B_KV, dtype=w1.dtype)
    place = lambda part: jnp.einsum('ldj,gh->lgdhj', part, eye).reshape(half * KV_W, B_KV * CMP_HIDDEN)
    top = place(w1r[:half]).astype(_BF16)
    bot = place(w1r[half:]).astype(_BF16)
    w2bd = jnp.einsum('jd,gh->gjhd', w2, eye).reshape(B_KV * CMP_HIDDEN, KV_W).astype(_BF16)
    pe8 = jnp.broadcast_to(pe.reshape(1, L_CMP * HEAD_DIM), (8, L_CMP * HEAD_DIM)).astype(_BF16)
    return pe8, w1.astype(_BF16), top, bot, w2bd


def _compress_call(k_cmp, v_cmp, wk, wv, bsz, seq):
    nr = seq // STRIDE_CMP
    rk = k_cmp.reshape(bsz, nr, STRIDE_CMP * KV_W)
    rv = v_cmp.reshape(bsz, nr, STRIDE_CMP * KV_W)
    pek, w1k, tk, bk, w2k = wk
    pev, w1v, tv, bv, w2v = wv
    rspec = pl.BlockSpec((1, nr, STRIDE_CMP * KV_W), lambda b: (b, 0, 0))
    full = lambda a: pl.BlockSpec(a.shape, lambda b: (0,) * a.ndim)
    ospec = pl.BlockSpec((1, nr, KV_W), lambda b: (b, 0, 0))
    return pl.pallas_call(
        _compress_kernel,
        grid=(bsz,),
        in_specs=[rspec, rspec, full(pek), full(pev), full(w1k), full(w1v),
                  full(tk), full(bk), full(w2k), full(tv), full(bv), full(w2v)],
        out_specs=[ospec, ospec],
        out_shape=[jax.ShapeDtypeStruct((bsz, nr, KV_W), _BF16)] * 2,
        compiler_params=pltpu.CompilerParams(dimension_semantics=("arbitrary",), vmem_limit_bytes=VMEM_LIMIT),
        name="compress",
    )(rk, rv, pek, pev, w1k, w1v, tk, bk, w2k, tv, bv, w2v)


def _topk_rows_mask(sc_t, k):
    n_rows = sc_t.shape[0]
    row = lax.broadcasted_iota(jnp.int32, sc_t.shape, 0)
    sel = jnp.zeros(sc_t.shape, _F32)
    for _ in range(k):
        m = jnp.max(sc_t, axis=0, keepdims=True)
        idx = jnp.min(jnp.where(sc_t == m, row, n_rows), axis=0, keepdims=True)
        chosen = row == idx
        sel = jnp.where(chosen, 1.0, sel)
        sc_t = jnp.where(chosen, -3e38, sc_t)
    return sel


def _attn_kernel(q_ref, gate_ref, gexp_ref, kc_ref, vc_ref, ks0_ref, ks1_ref, vs0_ref, vs1_ref,
                 kw_ref, vw_ref, gob_ref, o_ref, *, seq):
    qb = pl.program_id(1)
    t0 = qb * Q_BLOCK
    n_cmp = kc_ref.shape[1]
    n_sb = seq // L_SEL
    k_top = min(N_SEL, n_sb)
    rows = B_HPG * Q_BLOCK
    blk_per_ck = SEL_CK // L_SEL
    ks_refs, vs_refs = (ks0_ref, ks1_ref), (vs0_ref, vs1_ref)

    lane_q = lax.broadcasted_iota(jnp.int32, (Q_BLOCK, LANES), 1)
    lo = lane_q < HEAD_DIM
    t_col = t0 + lax.broadcasted_iota(jnp.int32, (Q_BLOCK, 1), 0)

    def per_head(x):
        return x.reshape(B_HPG, Q_BLOCK, x.shape[-1])

    def add_bias(s, bias):
        return (per_head(s) + bias[None]).reshape(rows, s.shape[-1])

    qs = []
    for g in range(B_KV):
        own = (lane_q // HEAD_DIM) == g
        qs.append(jnp.concatenate(
            [jnp.where(own, q_ref[0, :, j * LANES:(j + 1) * LANES], jnp.zeros((), q_ref.dtype))
             for j in range(B_HPG)], axis=0))

    c_lane = lax.broadcasted_iota(jnp.int32, (Q_BLOCK, n_cmp), 1)
    bias_c = jnp.where(c_lane * STRIDE_CMP + (L_CMP - 1) <= t_col, 0.0, NEG)
    has_c = (t_col >= L_CMP - 1).astype(_F32)
    c_row = lax.broadcasted_iota(jnp.int32, (n_cmp, n_sb), 0) * STRIDE_CMP
    s_col = lax.broadcasted_iota(jnp.int32, (n_cmp, n_sb), 1) * L_SEL
    overlap = jnp.where((c_row < s_col + L_SEL) & (c_row + L_CMP > s_col), 1.0, 0.0).astype(_BF16)
    blk = lax.broadcasted_iota(jnp.int32, (Q_BLOCK, n_sb), 1)
    cur = t_col // L_SEL
    forced = (blk == 0) | (blk == cur) | (blk == cur - 1)
    valid = blk * L_SEL <= t_col

    o_c, sel_bias = [], []
    for g in range(B_KV):
        s_c = add_bias(_dot_nt(qs[g], kc_ref[0]), bias_c)
        e_c = jnp.exp2(s_c - jnp.max(s_c, axis=-1, keepdims=True))
        inv = per_head(1.0 / jnp.maximum(jnp.sum(e_c, axis=-1, keepdims=True), 1e-30)) * has_c[None]
        p_c = (per_head(e_c) * inv).reshape(rows, n_cmp)
        o_c.append(_dot(p_c.astype(_BF16), vc_ref[0]))

        p_sum = jnp.sum(per_head(p_c), axis=0)
        p_hi = p_sum.astype(_BF16)
        p_lo = (p_sum - p_hi.astype(_F32)).astype(_BF16)
        imp = _dot(p_hi, overlap) + _dot(p_lo, overlap)
        score = jnp.where(forced, FORCE, jnp.where(valid, imp, -FORCE))
        chosen = _topk_rows_mask(score.T, k_top).T > 0.5
        sb = jnp.where(chosen & valid, 0.0, NEG)
        if n_sb < LANES:
            sb = jnp.concatenate([sb, jnp.full((Q_BLOCK, LANES - n_sb), NEG, _F32)], axis=1)
        sel_bias.append(sb)

    n_ck = (t0 + Q_BLOCK + SEL_CK - 1) // SEL_CK
    key_lane = lax.broadcasted_iota(jnp.int32, (Q_BLOCK, SEL_CK), 1)
    bias_diag = jnp.where((n_ck - 1) * SEL_CK + key_lane <= t_col, 0.0, NEG)

    def sel_chunk(ci, carry, diag):
        k0 = pl.multiple_of(ci * SEL_CK, SEL_CK)
        out = []
        for g in range(B_KV):
            m, acc = carry[g]
            aug0 = (1 - g) * HEAD_DIM
            shift = (aug0 + LANES - ci * blk_per_ck) % LANES
            in_aug = (lane_q >= aug0) & (lane_q < aug0 + blk_per_ck)
            aug = jnp.where(in_aug, pltpu.roll(sel_bias[g], shift, 1), 0.0).astype(_BF16)
            q_aug = qs[g] + jnp.concatenate([aug] * B_HPG, axis=0)
            s = _dot_nt(q_aug, ks_refs[g][0, pl.ds(k0, SEL_CK), :])
            if diag:
                s = add_bias(s, bias_diag)
            m_new = jnp.maximum(m, jnp.max(s, axis=-1, keepdims=True))
            p = jnp.exp2(s - m_new).astype(_BF16)
            acc = jnp.exp2(m - m_new) * acc + _dot(p, vs_refs[g][0, pl.ds(k0, SEL_CK), :])
            out.append((m_new, acc))
        return tuple(out)

    init = tuple((jnp.full((rows, 1), NEG, _F32), jnp.zeros((rows, LANES), _F32)) for _ in range(B_KV))
    carry = lax.fori_loop(0, n_ck - 1, lambda ci, c: sel_chunk(ci, c, False), init)
    carry = sel_chunk(n_ck - 1, carry, True)
    acc_s = [carry[g][1] for g in range(B_KV)]

    w_start = pl.multiple_of(jnp.maximum(t0 - WIN, 0), Q_BLOCK)
    w_len = WIN + Q_BLOCK
    kpos_w = w_start + lax.broadcasted_iota(jnp.int32, (Q_BLOCK, w_len), 1)
    diff_w = t_col - kpos_w
    bias_w = jnp.where((diff_w >= 0) & (diff_w < WIN), 0.0, NEG)
    kw = kw_ref[0, pl.ds(w_start, w_len), :]
    vw = vw_ref[0, pl.ds(w_start, w_len), :]
    lane_w = lax.broadcasted_iota(jnp.int32, (w_len, LANES), 1)
    acc_w = []
    for g in range(B_KV):
        s_w = add_bias(_dot_nt(qs[g], kw), bias_w)
        e_w = jnp.exp2(s_w - jnp.max(s_w, axis=-1, keepdims=True)).astype(_BF16)
        vw_aug = jnp.where((lane_w // HEAD_DIM) == g, vw, jnp.ones((), vw.dtype))
        acc_w.append(_dot(e_w, vw_aug))

    def numer(acc):
        return jnp.concatenate([jnp.where(lo, acc[0][j * Q_BLOCK:(j + 1) * Q_BLOCK],
                                          acc[1][j * Q_BLOCK:(j + 1) * Q_BLOCK]) for j in range(B_HPG)], axis=1)

    def denom(acc):
        return jnp.concatenate([pltpu.roll(jnp.where(lo, acc[1][j * Q_BLOCK:(j + 1) * Q_BLOCK],
                                                     acc[0][j * Q_BLOCK:(j + 1) * Q_BLOCK]), HEAD_DIM, 1)
                                for j in range(B_HPG)], axis=1)

    gates = gate_ref[0]
    g_hi = gates.astype(_BF16)
    g_lo = (gates - g_hi.astype(_F32)).astype(_BF16)
    gate_of = lambda r: _dot(g_hi, gexp_ref[r]) + _dot(g_lo, gexp_ref[r])
    ob = (gate_of(0) * numer(o_c)
          + gate_of(1) * numer(acc_s) * (1.0 / jnp.maximum(denom(acc_s), 1e-30))
          + gate_of(2) * numer(acc_w) * (1.0 / jnp.maximum(denom(acc_w), 1e-30)))
    o_ref[0] = _rms(ob, gob_ref[...]).astype(o_ref.dtype)


def _gate_expand():
    x = np.zeros((3, GATE_PAD, B_WIDTH), np.float32)
    for slot, h in enumerate(_PERM_HEADS):
        for r in range(3):
            x[r, 3 * h + r, slot * HEAD_DIM:(slot + 1) * HEAD_DIM] = 1.0
    return jnp.asarray(x, _BF16)


def _attn_call(q, gates, kc, vc, ks0, ks1, vs0, vs1, kw, vw, gob, bsz, seq):
    assert seq // L_SEL <= LANES and SEL_CK // L_SEL <= HEAD_DIM
    n_cmp = kc.shape[1]
    qspec = lambda w: pl.BlockSpec((1, Q_BLOCK, w), lambda b, i: (b, i, 0))
    full = lambda r: pl.BlockSpec((1, r, KV_W), lambda b, i: (b, 0, 0))
    return pl.pallas_call(
        functools.partial(_attn_kernel, seq=seq),
        grid=(bsz, seq // Q_BLOCK),
        in_specs=[qspec(B_WIDTH), qspec(GATE_PAD),
                  pl.BlockSpec((3, GATE_PAD, B_WIDTH), lambda b, i: (0, 0, 0)),
                  full(n_cmp), full(n_cmp), full(seq), full(seq), full(seq), full(seq), full(seq), full(seq),
                  pl.BlockSpec((1, B_WIDTH), lambda b, i: (0, 0))],
        out_specs=qspec(B_WIDTH),
        out_shape=jax.ShapeDtypeStruct((bsz, seq, B_WIDTH), _BF16),
        compiler_params=pltpu.CompilerParams(dimension_semantics=("arbitrary", "arbitrary"),
                                             vmem_limit_bytes=VMEM_LIMIT),
        name="attn",
    )(q, gates, _gate_expand(), kc, vc, ks0, ks1, vs0, vs1, kw, vw, gob)


def _post_kernel(x_ref, oa_ref, ob_ref, woa_ref, wob_ref, gmoe_ref, r_ref, h1_ref, hn_ref, comb_ref):
    h1 = x_ref[...] + _dot(oa_ref[...], woa_ref[...]) + _dot(ob_ref[...], wob_ref[...])
    h1_ref[...] = h1
    hn = _rms(h1, gmoe_ref[...])
    hn_ref[...] = hn.astype(hn_ref.dtype)

    hn_hi = hn.astype(_BF16)
    hn_lo = (hn - hn_hi.astype(_F32)).astype(_BF16)
    hi_both = _dot(hn_hi, r_ref[...])
    logits = hi_both[:, :GATE_PAD] + (_dot(hn_lo, r_ref[:, :GATE_PAD]) + hi_both[:, GATE_PAD:])
    lane = lax.broadcasted_iota(jnp.int32, logits.shape, 1)
    first_idx = lambda hit: jnp.min(jnp.where(hit, lane, LANES), axis=-1, keepdims=True)

    is_g = lane < N_GROUPS
    lg = jnp.where(is_g, logits, NEG)
    mg = jnp.max(lg, axis=-1, keepdims=True)
    sg = jnp.sum(jnp.where(is_g, jnp.exp(lg - mg), 0.0), axis=-1, keepdims=True)
    pg_top = 1.0 / sg
    g_sel = first_idx(is_g & (lg == mg))

    e_lo = ROUTER_OFF + g_sel * EXPERTS_PER_GROUP
    is_e = (lane >= e_lo) & (lane < e_lo + EXPERTS_PER_GROUP)
    le = jnp.where(is_e, logits, NEG)
    m1 = jnp.max(le, axis=-1, keepdims=True)
    se = jnp.sum(jnp.where(is_e, jnp.exp(le - m1), 0.0), axis=-1, keepdims=True)
    i1 = first_idx(is_e & (le == m1))
    le2 = jnp.where(lane == i1, NEG, le)
    m2 = jnp.max(le2, axis=-1, keepdims=True)
    i2 = first_idx(is_e & (lane != i1) & (le2 == m2))
    pe1 = 1.0 / se
    pe2 = jnp.exp(m2 - m1) / se
    denom = pe1 + pe2
    comb_ref[...] = (jnp.where(lane == i1, pg_top * pe1 / denom, 0.0)
                     + jnp.where(lane == i2, pg_top * pe2 / denom, 0.0))


def _post_call(x2, oa, ob, woa, wob, gmoe, r_cat):
    n = x2.shape[0]
    tm = TM_POST
    row = lambda i: (i, 0)
    const2 = lambda i: (0, 0)
    return pl.pallas_call(
        _post_kernel,
        grid=(n // tm,),
        in_specs=[pl.BlockSpec((tm, D_MODEL), row), pl.BlockSpec((tm, A_WIDTH), row),
                  pl.BlockSpec((tm, B_WIDTH), row), pl.BlockSpec((A_WIDTH, D_MODEL), const2),
                  pl.BlockSpec((B_WIDTH, D_MODEL), const2), pl.BlockSpec((1, D_MODEL), const2),
                  pl.BlockSpec((D_MODEL, 2 * GATE_PAD), const2)],
        out_specs=[pl.BlockSpec((tm, D_MODEL), row), pl.BlockSpec((tm, D_MODEL), row),
                   pl.BlockSpec((tm, GATE_PAD), row)],
        out_shape=[jax.ShapeDtypeStruct((n, D_MODEL), _F32), jax.ShapeDtypeStruct((n, D_MODEL), _BF16),
                   jax.ShapeDtypeStruct((n, GATE_PAD), _F32)],
        compiler_params=pltpu.CompilerParams(dimension_semantics=("arbitrary",), vmem_limit_bytes=VMEM_LIMIT),
        name="post",
    )(x2, oa, ob, woa, wob, gmoe, r_cat)


def _moe_kernel(hn_ref, comb_ref, h1_ref, p_ref, wgu_ref, wd_ref, gple_ref, wpg_ref, wpp_ref, gfin_ref,
                o_ref, acc_ref):
    e = pl.program_id(1)
    n_sub = hn_ref.shape[0] // SUB_MOE

    @pl.when(e == 0)
    def _():
        acc_ref[...] = jnp.zeros_like(acc_ref)

    for r in range(n_sub):
        rows = slice(r * SUB_MOE, (r + 1) * SUB_MOE)
        gu = _dot(hn_ref[rows, :], wgu_ref[0])
        gate = gu[:, :D_FF_EXPERT]
        hid = gate * jax.nn.sigmoid(gate) * gu[:, D_FF_EXPERT:]
        comb = comb_ref[rows, :]
        lane = lax.broadcasted_iota(jnp.int32, comb.shape, 1)
        w_e = jnp.sum(jnp.where(lane == ROUTER_OFF + e, comb, 0.0), axis=-1, keepdims=True)
        acc_ref[rows, :] += _dot((hid * w_e).astype(_BF16), wd_ref[0])

    @pl.when(e == N_EXPERTS - 1)
    def _():
        for r in range(n_sub):
            rows = slice(r * SUB_MOE, (r + 1) * SUB_MOE)
            h2 = h1_ref[rows, :] + acc_ref[rows, :]
            gate = jax.nn.sigmoid(_dot(_rms(h2, gple_ref[...]).astype(_BF16), wpg_ref[...]))
            h3 = h2 + _dot(p_ref[rows, :].astype(_BF16), wpp_ref[...]) * gate
            o_ref[rows, :] = _rms(h3, gfin_ref[...])


def _moe_call(hn2, comb, h1, p2, wgu, wd, gple, wpg, wpp, gfin):
    n = hn2.shape[0]
    tm = TM_MOE
    row = lambda i, e: (i, 0)
    const2 = lambda i, e: (0, 0)
    return pl.pallas_call(
        _moe_kernel,
        grid=(n // tm, N_EXPERTS),
        in_specs=[pl.BlockSpec((tm, D_MODEL), row), pl.BlockSpec((tm, GATE_PAD), row),
                  pl.BlockSpec((tm, D_MODEL), row), pl.BlockSpec((tm, D_PLE), row),
                  pl.BlockSpec((1, D_MODEL, 2 * D_FF_EXPERT), lambda i, e: (e, 0, 0)),
                  pl.BlockSpec((1, D_FF_EXPERT, D_MODEL), lambda i, e: (e, 0, 0)),
                  pl.BlockSpec((1, D_MODEL), const2), pl.BlockSpec((D_MODEL, D_MODEL), const2),
                  pl.BlockSpec((D_PLE, D_MODEL), const2), pl.BlockSpec((1, D_MODEL), const2)],
        out_specs=pl.BlockSpec((tm, D_MODEL), row),
        out_shape=jax.ShapeDtypeStruct((n, D_MODEL), _F32),
        scratch_shapes=[pltpu.VMEM((tm, D_MODEL), _F32)],
        compiler_params=pltpu.CompilerParams(dimension_semantics=("arbitrary", "arbitrary"),
                                             vmem_limit_bytes=VMEM_LIMIT),
        name="moe",
    )(hn2, comb, h1, p2, wgu, wd, gple, wpg, wpp, gfin)


def _rope_tables(seq):
    half = HEAD_DIM // 2
    inv = 1.0 / (ROPE_THETA ** (jnp.arange(half, dtype=_F32) / half))
    ang = jnp.arange(seq, dtype=_F32)[:, None] * inv[None, :]
    cos, sin = jnp.cos(ang), jnp.sin(ang)
    reps = LANES // HEAD_DIM
    cos_t = jnp.tile(jnp.concatenate([cos, cos], axis=1), (1, reps))
    sin_t = jnp.tile(jnp.concatenate([-sin, sin], axis=1), (1, reps))
    return cos_t, sin_t


def _layer(h, p_i, norm_mix, w_in, gmlp_v_norm, gmlp_w_s, gmlp_b_s,
           cmp_pe_k, cmp_w1_k, cmp_w2_k, cmp_pe_v, cmp_w1_v, cmp_w2_v,
           out_norm_a, out_norm_b, w_o, norm_moe, router_group, router_expert,
           moe_w_gate, moe_w_up, moe_w_down, norm_ple, w_ple_proj, w_ple_gate, norm_final):
    bsz, seq, _ = h.shape
    n = bsz * seq
    x2 = h.reshape(n, D_MODEL)
    row = lambda v: v.reshape(1, -1).astype(_F32)

    w_q = w_in[:, OFF_Q:OFF_KV][:, _PERM_CH]
    w_gate = jnp.pad(w_in[:, OFF_GATE:D_IN], ((0, 0), (0, GATE_PAD - N_GATES)))
    w_all = jnp.concatenate([w_in[:, :OFF_Q], w_q, w_in[:, OFF_KV:OFF_GATE], w_gate], axis=1).astype(_BF16)
    cos_t, sin_t = _rope_tables(seq)
    ws_pairs = gmlp_w_s.reshape(A_HEADS // 2, 2, CHUNK, CHUNK).transpose(0, 2, 1, 3).reshape(
        A_HEADS // 2, CHUNK, 2 * CHUNK)
    bs_exp = jnp.repeat(gmlp_b_s.T, HEAD_DIM, axis=1)

    oa, q, k_cmp, v_cmp, ks0, ks1, vs0, vs1, k_win, v_win, gates = _proj_call(
        x2, row(norm_mix), w_all, cos_t, sin_t, row(gmlp_v_norm), ws_pairs, bs_exp, row(out_norm_a), seq)

    kc, vc = _compress_call(k_cmp, v_cmp, _compress_weights(cmp_w1_k, cmp_w2_k, cmp_pe_k),
                            _compress_weights(cmp_w1_v, cmp_w2_v, cmp_pe_v), bsz, seq)

    b3 = lambda a: a.reshape(bsz, seq, a.shape[-1])
    ob = _attn_call(b3(q), b3(gates), kc, vc, b3(ks0), b3(ks1), b3(vs0), b3(vs1), b3(k_win), b3(v_win),
                    row(out_norm_b[_PERM_CH]), bsz, seq)

    r_cat = jnp.pad(jnp.concatenate([router_group, router_expert], axis=1),
                    ((0, 0), (0, GATE_PAD - N_GROUPS - N_EXPERTS)))
    r_hi = r_cat.astype(_BF16)
    r_cat = jnp.concatenate([r_hi, (r_cat - r_hi.astype(_F32)).astype(_BF16)], axis=1)
    h1, hn2, comb = _post_call(x2, oa, ob.reshape(n, B_WIDTH), w_o[:A_WIDTH].astype(_BF16),
                               w_o[A_WIDTH:][_PERM_CH].astype(_BF16), row(norm_moe), r_cat)

    wgu = jnp.concatenate([moe_w_gate, moe_w_up], axis=-1).astype(_BF16)
    out = _moe_call(hn2, comb, h1, p_i.reshape(n, D_PLE), wgu, moe_w_down.astype(_BF16), row(norm_ple),
                    w_ple_gate.astype(_BF16), w_ple_proj.astype(_BF16), row(norm_final))
    return out.reshape(bsz, seq, D_MODEL)


def kernel(x, p, norm_mix, w_in, gmlp_v_norm, gmlp_w_s, gmlp_b_s, cmp_pe_k, cmp_w1_k, cmp_w2_k,
           cmp_pe_v, cmp_w1_v, cmp_w2_v, out_norm_a, out_norm_b, w_o, norm_moe, router_group,
           router_expert, moe_w_gate, moe_w_up, moe_w_down, norm_ple, w_ple_proj, w_ple_gate, norm_final):
    assert p.shape[0] == 1, "single-layer trunk"
    assert x.shape[1] % SEL_CK == 0 and x.shape[1] >= WIN + Q_BLOCK
    assert (x.shape[0] * x.shape[1]) % TM_MOE == 0
    return _layer(x, p[0], norm_mix[0], w_in[0], gmlp_v_norm[0], gmlp_w_s[0], gmlp_b_s[0],
                  cmp_pe_k[0], cmp_w1_k[0], cmp_w2_k[0], cmp_pe_v[0], cmp_w1_v[0], cmp_w2_v[0],
                  out_norm_a[0], out_norm_b[0], w_o[0], norm_moe[0], router_group[0], router_expert[0],
                  moe_w_gate[0], moe_w_up[0], moe_w_down[0], norm_ple[0], w_ple_proj[0], w_ple_gate[0],
                  norm_final)
```

```python
import functools

import numpy as np
import jax
import jax.numpy as jnp
from jax import lax
from jax.experimental import pallas as pl
from jax.experimental.pallas import tpu as pltpu
from jax.experimental.pallas import tpu_sc as plsc

D_MODEL = 1024
HEAD_DIM = 64
A_HEADS = 8
A_WIDTH = A_HEADS * HEAD_DIM
B_HEADS = 8
B_WIDTH = B_HEADS * HEAD_DIM
B_KV = 2
B_HPG = B_HEADS // B_KV
KV_W = B_KV * HEAD_DIM
N_GATES = B_HEADS * 3
CHUNK = 128
L_CMP = 32
STRIDE_CMP = 16
CMP_HIDDEN = 256
L_SEL = 64
N_SEL = 16
WIN = 512
Q_BLOCK = 128
ROPE_THETA = 10000.0
N_GROUPS = 4
EXPERTS_PER_GROUP = 4
N_EXPERTS = N_GROUPS * EXPERTS_PER_GROUP
D_FF_EXPERT = 512
D_PLE = 256
EPS = 1e-6
LOG2E = 1.4426950408889634
NEG = -1e30
FORCE = 1e6

OFF_Q = 2 * A_WIDTH
OFF_KV = OFF_Q + B_WIDTH
OFF_GATE = OFF_KV + 6 * KV_W
D_IN = OFF_GATE + N_GATES

LANES = 128
GATE_PAD = LANES
ROUTER_OFF = N_GROUPS
W_ALL = OFF_GATE + GATE_PAD

TM_PROJ = 512
TM_POST = 512
TR_GMM = 512
SUB_MOE = 256
RT_W = 2 * N_EXPERTS
SC_CORES = 2
SC_SUBCORES = 16
SC_CHUNK = 128
SEL_CK = 512
VMEM_LIMIT = 56 * 1024 * 1024

_PERM_HEADS = [0, 4, 1, 5, 2, 6, 3, 7]
_PERM_CH = np.concatenate([np.arange(HEAD_DIM) + HEAD_DIM * h for h in _PERM_HEADS])

_F32 = jnp.float32
_BF16 = jnp.bfloat16


def _dot(a, b):
    return jnp.dot(a, b, preferred_element_type=_F32)


def _dot_nt(a, b):
    return lax.dot_general(a, b, (((1,), (1,)), ((), ())), preferred_element_type=_F32)


def _rms(x, g):
    return x * lax.rsqrt(jnp.mean(x * x, axis=-1, keepdims=True) + EPS) * g


def _gelu(x):
    return 0.5 * x * (1.0 + jnp.tanh(0.7978845608028654 * (x + 0.044715 * (x * x * x))))


def _rope_tile(x, cos, sin_signed):
    lane = lax.broadcasted_iota(jnp.int32, x.shape, 1)
    first_half = (lane % HEAD_DIM) < (HEAD_DIM // 2)
    rot = jnp.where(first_half, pltpu.roll(x, LANES - HEAD_DIM // 2, 1), pltpu.roll(x, HEAD_DIM // 2, 1))
    return x * cos + rot * sin_signed


def _proj_kernel(x_ref, gmix_ref, w_ref, cos_ref, sin_ref, gv_ref, ws_ref, bs_ref, goa_ref,
                 oa_ref, q_ref, kc_ref, vc_ref, ks0_ref, ks1_ref, vs0_ref, vs1_ref, kw_ref, vw_ref, gate_ref):
    tm = x_ref.shape[0]
    hn = _rms(x_ref[...], gmix_ref[...]).astype(_BF16)
    cos = cos_ref[...]
    sin = sin_ref[...]

    zq = _dot(hn, w_ref[:, OFF_Q:OFF_KV])
    scale = HEAD_DIM ** -0.5 * LOG2E
    for j in range(B_WIDTH // LANES):
        blk = _rope_tile(zq[:, j * LANES:(j + 1) * LANES], cos, sin) * scale
        q_ref[:, j * LANES:(j + 1) * LANES] = blk.astype(q_ref.dtype)

    zkv = _dot(hn, w_ref[:, OFF_KV:OFF_GATE])
    kv = []
    for j in range(6):
        blk = zkv[:, j * KV_W:(j + 1) * KV_W]
        kv.append(_rope_tile(blk, cos, sin) if j % 2 == 0 else blk)
    k_cmp, v_cmp, k_slc, v_slc, k_win, v_win = kv
    kc_ref[...] = k_cmp.astype(kc_ref.dtype)
    vc_ref[...] = v_cmp.astype(vc_ref.dtype)
    kw_ref[...] = k_win.astype(kw_ref.dtype)
    vw_ref[...] = v_win.astype(vw_ref.dtype)

    assert tm % SEL_CK == 0
    row = lax.broadcasted_iota(jnp.int32, (tm, LANES), 0)
    lane = lax.broadcasted_iota(jnp.int32, (tm, LANES), 1)
    blk_in_chunk = (row % SEL_CK) // L_SEL
    for g, (ks_ref, vs_ref) in enumerate(((ks0_ref, vs0_ref), (ks1_ref, vs1_ref))):
        own = (lane // HEAD_DIM) == g
        onehot = jnp.where((lane % HEAD_DIM) == blk_in_chunk, 1.0, 0.0)
        ks_ref[...] = jnp.where(own, k_slc, onehot).astype(ks_ref.dtype)
        vs_ref[...] = jnp.where(own, v_slc, 1.0).astype(vs_ref.dtype)

    zg = _dot(hn, w_ref[:, OFF_GATE:W_ALL])
    gate_ref[...] = jax.nn.sigmoid(zg)

    zu = _gelu(_dot(hn, w_ref[:, 0:A_WIDTH]))
    zv = _gelu(_dot(hn, w_ref[:, A_WIDTH:2 * A_WIDTH]))
    vn = _rms(zv, gv_ref[...]).astype(_BF16)

    t_io = lax.broadcasted_iota(jnp.int32, (CHUNK, 2 * CHUNK), 0)
    s_io = lax.broadcasted_iota(jnp.int32, (CHUNK, 2 * CHUNK), 1) % CHUNK
    causal = s_io <= t_io
    lane = lax.broadcasted_iota(jnp.int32, (CHUNK, LANES), 1)
    lo = lane < HEAD_DIM
    bs = bs_ref[...]
    chunks = []
    for c in range(tm // CHUNK):
        tiles = []
        for pr in range(A_HEADS // 2):
            wcat = jnp.where(causal, ws_ref[pr], 0.0).astype(_BF16)
            vblk = vn[c * CHUNK:(c + 1) * CHUNK, pr * LANES:(pr + 1) * LANES]
            zero = jnp.zeros_like(vblk)
            rhs = jnp.concatenate([jnp.where(lo, vblk, zero), jnp.where(lo, zero, vblk)], axis=0)
            tiles.append(_dot(wcat, rhs))
        chunks.append(jnp.concatenate(tiles, axis=1) + bs)
    mixed = jnp.concatenate(chunks, axis=0)
    oa = zu * mixed
    oa_ref[...] = _rms(oa, goa_ref[...]).astype(oa_ref.dtype)


def _proj_call(x2, gmix, w_all, cos_t, sin_t, gv, ws_pairs, bs_exp, goa, seq):
    n = x2.shape[0]
    tm = TM_PROJ
    n_t = seq // tm
    row = lambda i: (i, 0)
    const2 = lambda i: (0, 0)
    pos = lambda i: (i % n_t, 0)
    out_shapes = [jax.ShapeDtypeStruct((n, A_WIDTH), _BF16), jax.ShapeDtypeStruct((n, B_WIDTH), _BF16)]
    out_shapes += [jax.ShapeDtypeStruct((n, KV_W), _BF16)] * 8
    out_shapes += [jax.ShapeDtypeStruct((n, GATE_PAD), _F32)]
    out_specs = [pl.BlockSpec((tm, A_WIDTH), row), pl.BlockSpec((tm, B_WIDTH), row)]
    out_specs += [pl.BlockSpec((tm, KV_W), row)] * 8
    out_specs += [pl.BlockSpec((tm, GATE_PAD), row)]
    return pl.pallas_call(
        _proj_kernel,
        grid=(n // tm,),
        in_specs=[
            pl.BlockSpec((tm, D_MODEL), row),
            pl.BlockSpec((1, D_MODEL), const2),
            pl.BlockSpec((D_MODEL, W_ALL), const2),
            pl.BlockSpec((tm, LANES), pos),
            pl.BlockSpec((tm, LANES), pos),
            pl.BlockSpec((1, A_WIDTH), const2),
            pl.BlockSpec((A_HEADS // 2, CHUNK, 2 * CHUNK), lambda i: (0, 0, 0)),
            pl.BlockSpec((CHUNK, A_WIDTH), const2),
            pl.BlockSpec((1, A_WIDTH), const2),
        ],
        out_specs=out_specs,
        out_shape=out_shapes,
        compiler_params=pltpu.CompilerParams(dimension_semantics=("arbitrary",), vmem_limit_bytes=VMEM_LIMIT),
        name="proj",
    )(x2, gmix, w_all, cos_t, sin_t, gv, ws_pairs, bs_exp, goa)


def _compress_kernel(rk_ref, rv_ref, pek_ref, pev_ref, w1k_ref, w1v_ref,
                     tk_ref, bk_ref, w2k_ref, tv_ref, bv_ref, w2v_ref, kc_ref, vc_ref):
    def one(r_ref, pe_ref, w1_ref, top_ref, bot_ref, w2_ref, o_ref):
        r = r_ref[0]
        nr = r.shape[0]
        a = _dot(r, top_ref[...])
        b = _dot(r, bot_ref[...])
        pe_h = _dot(pe_ref[...], w1_ref[...])
        pe2 = jnp.concatenate([pe_h[0:1], pe_h[0:1]], axis=1)
        hid = a + pltpu.roll(b, nr - 1, 0) + pe2
        o_ref[0] = _dot(_gelu(hid).astype(_BF16), w2_ref[...]).astype(o_ref.dtype)

    one(rk_ref, pek_ref, w1k_ref, tk_ref, bk_ref, w2k_ref, kc_ref)
    one(rv_ref, pev_ref, w1v_ref, tv_ref, bv_ref, w2v_ref, vc_ref)


def _compress_weights(w1, w2, pe):
    half = L_CMP // 2
    w1r = w1.reshape(L_CMP, HEAD_DIM, CMP_HIDDEN)
    eye = jnp.eye(B_KV, dtype=w1.dtype)
    place = lambda part: jnp.einsum('ldj,gh->lgdhj', part, eye).reshape(half * KV_W, B_KV * CMP_HIDDEN)
    top = place(w1r[:half]).astype(_BF16)
    bot = place(w1r[half:]).astype(_BF16)
    w2bd = jnp.einsum('jd,gh->gjhd', w2, eye).reshape(B_KV * CMP_HIDDEN, KV_W).astype(_BF16)
    pe8 = jnp.broadcast_to(pe.reshape(1, L_CMP * HEAD_DIM), (8, L_CMP * HEAD_DIM)).astype(_BF16)
    return pe8, w1.astype(_BF16), top, bot, w2bd


def _compress_call(k_cmp, v_cmp, wk, wv, bsz, seq):
    nr = seq // STRIDE_CMP
    rk = k_cmp.reshape(bsz, nr, STRIDE_CMP * KV_W)
    rv = v_cmp.reshape(bsz, nr, STRIDE_CMP * KV_W)
    pek, w1k, tk, bk, w2k = wk
    pev, w1v, tv, bv, w2v = wv
    rspec = pl.BlockSpec((1, nr, STRIDE_CMP * KV_W), lambda b: (b, 0, 0))
    full = lambda a: pl.BlockSpec(a.shape, lambda b: (0,) * a.ndim)
    ospec = pl.BlockSpec((1, nr, KV_W), lambda b: (b, 0, 0))
    return pl.pallas_call(
        _compress_kernel,
        grid=(bsz,),
        in_specs=[rspec, rspec, full(pek), full(pev), full(w1k), full(w1v),
                  full(tk), full(bk), full(w2k), full(tv), full(bv), full(w2v)],
        out_specs=[ospec, ospec],
        out_shape=[jax.ShapeDtypeStruct((bsz, nr, KV_W), _BF16)] * 2,
        compiler_params=pltpu.CompilerParams(dimension_semantics=("arbitrary",), vmem_limit_bytes=VMEM_LIMIT),
        name="compress",
    )(rk, rv, pek, pev, w1k, w1v, tk, bk, w2k, tv, bv, w2v)


def _topk_rows_mask(sc_t, k):
    n_rows = sc_t.shape[0]
    row = lax.broadcasted_iota(jnp.int32, sc_t.shape, 0)
    sel = jnp.zeros(sc_t.shape, _F32)
    for _ in range(k):
        m = jnp.max(sc_t, axis=0, keepdims=True)
        idx = jnp.min(jnp.where(sc_t == m, row, n_rows), axis=0, keepdims=True)
        chosen = row == idx
        sel = jnp.where(chosen, 1.0, sel)
        sc_t = jnp.where(chosen, -3e38, sc_t)
    return sel


def _attn_kernel(q_ref, gate_ref, gexp_ref, kc_ref, vc_ref, ks0_ref, ks1_ref, vs0_ref, vs1_ref,
                 kw_ref, vw_ref, gob_ref, o_ref, *, seq):
    qb = pl.program_id(1)
    t0 = qb * Q_BLOCK
    n_cmp = kc_ref.shape[1]
    n_sb = seq // L_SEL
    k_top = min(N_SEL, n_sb)
    rows = B_HPG * Q_BLOCK
    blk_per_ck = SEL_CK // L_SEL
    ks_refs, vs_refs = (ks0_ref, ks1_ref), (vs0_ref, vs1_ref)

    lane_q = lax.broadcasted_iota(jnp.int32, (Q_BLOCK, LANES), 1)
    lo = lane_q < HEAD_DIM
    t_col = t0 + lax.broadcasted_iota(jnp.int32, (Q_BLOCK, 1), 0)

    def per_head(x):
        return x.reshape(B_HPG, Q_BLOCK, x.shape[-1])

    def add_bias(s, bias):
        return (per_head(s) + bias[None]).reshape(rows, s.shape[-1])

    qs = []
    for g in range(B_KV):
        own = (lane_q // HEAD_DIM) == g
        qs.append(jnp.concatenate(
            [jnp.where(own, q_ref[0, :, j * LANES:(j + 1) * LANES], jnp.zeros((), q_ref.dtype))
             for j in range(B_HPG)], axis=0))

    c_lane = lax.broadcasted_iota(jnp.int32, (Q_BLOCK, n_cmp), 1)
    bias_c = jnp.where(c_lane * STRIDE_CMP + (L_CMP - 1) <= t_col, 0.0, NEG)
    has_c = (t_col >= L_CMP - 1).astype(_F32)
    c_row = lax.broadcasted_iota(jnp.int32, (n_cmp, n_sb), 0) * STRIDE_CMP
    s_col = lax.broadcasted_iota(jnp.int32, (n_cmp, n_sb), 1) * L_SEL
    overlap = jnp.where((c_row < s_col + L_SEL) & (c_row + L_CMP > s_col), 1.0, 0.0).astype(_BF16)
    blk = lax.broadcasted_iota(jnp.int32, (Q_BLOCK, n_sb), 1)
    cur = t_col // L_SEL
    forced = (blk == 0) | (blk == cur) | (blk == cur - 1)
    valid = blk * L_SEL <= t_col

    o_c, sel_bias = [], []
    for g in range(B_KV):
        s_c = add_bias(_dot_nt(qs[g], kc_ref[0]), bias_c)
        e_c = jnp.exp2(s_c - jnp.max(s_c, axis=-1, keepdims=True))
        inv = per_head(1.0 / jnp.maximum(jnp.sum(e_c, axis=-1, keepdims=True), 1e-30)) * has_c[None]
        p_c = (per_head(e_c) * inv).reshape(rows, n_cmp)
        o_c.append(_dot(p_c.astype(_BF16), vc_ref[0]))

        p_sum = jnp.sum(per_head(p_c), axis=0)
        p_hi = p_sum.astype(_BF16)
        p_lo = (p_sum - p_hi.astype(_F32)).astype(_BF16)
        imp = _dot(p_hi, overlap) + _dot(p_lo, overlap)
        score = jnp.where(forced, FORCE, jnp.where(valid, imp, -FORCE))
        chosen = _topk_rows_mask(score.T, k_top).T > 0.5
        sb = jnp.where(chosen & valid, 0.0, NEG)
        if n_sb < LANES:
            sb = jnp.concatenate([sb, jnp.full((Q_BLOCK, LANES - n_sb), NEG, _F32)], axis=1)
        sel_bias.append(sb)

    n_ck = (t0 + Q_BLOCK + SEL_CK - 1) // SEL_CK
    key_lane = lax.broadcasted_iota(jnp.int32, (Q_BLOCK, SEL_CK), 1)
    bias_diag = jnp.where((n_ck - 1) * SEL_CK + key_lane <= t_col, 0.0, NEG)

    def sel_chunk(ci, carry, diag):
        k0 = pl.multiple_of(ci * SEL_CK, SEL_CK)
        out = []
        for g in range(B_KV):
            m, acc = carry[g]
            aug0 = (1 - g) * HEAD_DIM
            shift = (aug0 + LANES - ci * blk_per_ck) % LANES
            in_aug = (lane_q >= aug0) & (lane_q < aug0 + blk_per_ck)
            aug = jnp.where(in_aug, pltpu.roll(sel_bias[g], shift, 1), 0.0).astype(_BF16)
            q_aug = qs[g] + jnp.concatenate([aug] * B_HPG, axis=0)
            s = _dot_nt(q_aug, ks_refs[g][0, pl.ds(k0, SEL_CK), :])
            if diag:
                s = add_bias(s, bias_diag)
            m_new = jnp.maximum(m, jnp.max(s, axis=-1, keepdims=True))
            p = jnp.exp2(s - m_new).astype(_BF16)
            acc = jnp.exp2(m - m_new) * acc + _dot(p, vs_refs[g][0, pl.ds(k0, SEL_CK), :])
            out.append((m_new, acc))
        return tuple(out)

    init = tuple((jnp.full((rows, 1), NEG, _F32), jnp.zeros((rows, LANES), _F32)) for _ in range(B_KV))
    carry = lax.fori_loop(0, n_ck - 1, lambda ci, c: sel_chunk(ci, c, False), init)
    carry = sel_chunk(n_ck - 1, carry, True)
    acc_s = [carry[g][1] for g in range(B_KV)]

    w_start = pl.multiple_of(jnp.maximum(t0 - WIN, 0), Q_BLOCK)
    w_len = WIN + Q_BLOCK
    kpos_w = w_start + lax.broadcasted_iota(jnp.int32, (Q_BLOCK, w_len), 1)
    diff_w = t_col - kpos_w
    bias_w = jnp.where((diff_w >= 0) & (diff_w < WIN), 0.0, NEG)
    kw = kw_ref[0, pl.ds(w_start, w_len), :]
    vw = vw_ref[0, pl.ds(w_start, w_len), :]
    lane_w = lax.broadcasted_iota(jnp.int32, (w_len, LANES), 1)
    acc_w = []
    for g in range(B_KV):
        s_w = add_bias(_dot_nt(qs[g], kw), bias_w)
        e_w = jnp.exp2(s_w - jnp.max(s_w, axis=-1, keepdims=True)).astype(_BF16)
        vw_aug = jnp.where((lane_w // HEAD_DIM) == g, vw, jnp.ones((), vw.dtype))
        acc_w.append(_dot(e_w, vw_aug))

    def numer(acc):
        return jnp.concatenate([jnp.where(lo, acc[0][j * Q_BLOCK:(j + 1) * Q_BLOCK],
                                          acc[1][j * Q_BLOCK:(j + 1) * Q_BLOCK]) for j in range(B_HPG)], axis=1)

    def denom(acc):
        return jnp.concatenate([pltpu.roll(jnp.where(lo, acc[1][j * Q_BLOCK:(j + 1) * Q_BLOCK],
                                                     acc[0][j * Q_BLOCK:(j + 1) * Q_BLOCK]), HEAD_DIM, 1)
                                for j in range(B_HPG)], axis=1)

    gates = gate_ref[0]
    g_hi = gates.astype(_BF16)
    g_lo = (gates - g_hi.astype(_F32)).astype(_BF16)
    gate_of = lambda r: _dot(g_hi, gexp_ref[r]) + _dot(g_lo, gexp_ref[r])
    ob = (gate_of(0) * numer(o_c)
          + gate_of(1) * numer(acc_s) * (1.0 / jnp.maximum(denom(acc_s), 1e-30))
          + gate_of(2) * numer(acc_w) * (1.0 / jnp.maximum(denom(acc_w), 1e-30)))
    o_ref[0] = _rms(ob, gob_ref[...]).astype(o_ref.dtype)


def _gate_expand():
    x = np.zeros((3, GATE_PAD, B_WIDTH), np.float32)
    for slot, h in enumerate(_PERM_HEADS):
        for r in range(3):
            x[r, 3 * h + r, slot * HEAD_DIM:(slot + 1) * HEAD_DIM] = 1.0
    return jnp.asarray(x, _BF16)


def _attn_call(q, gates, kc, vc, ks0, ks1, vs0, vs1, kw, vw, gob, bsz, seq):
    assert seq // L_SEL <= LANES and SEL_CK // L_SEL <= HEAD_DIM
    n_cmp = kc.shape[1]
    qspec = lambda w: pl.BlockSpec((1, Q_BLOCK, w), lambda b, i: (b, i, 0))
    full = lambda r: pl.BlockSpec((1, r, KV_W), lambda b, i: (b, 0, 0))
    return pl.pallas_call(
        functools.partial(_attn_kernel, seq=seq),
        grid=(bsz, seq // Q_BLOCK),
        in_specs=[qspec(B_WIDTH), qspec(GATE_PAD),
                  pl.BlockSpec((3, GATE_PAD, B_WIDTH), lambda b, i: (0, 0, 0)),
                  full(n_cmp), full(n_cmp), full(seq), full(seq), full(seq), full(seq), full(seq), full(seq),
                  pl.BlockSpec((1, B_WIDTH), lambda b, i: (0, 0))],
        out_specs=qspec(B_WIDTH),
        out_shape=jax.ShapeDtypeStruct((bsz, seq, B_WIDTH), _BF16),
        compiler_params=pltpu.CompilerParams(dimension_semantics=("arbitrary", "arbitrary"),
                                             vmem_limit_bytes=VMEM_LIMIT),
        name="attn",
    )(q, gates, _gate_expand(), kc, vc, ks0, ks1, vs0, vs1, kw, vw, gob)


def _pack_halves(x):
    w = x.shape[1] // 2
    bits = lambda v: lax.bitcast_convert_type(v.astype(_BF16).astype(_F32), jnp.uint32)
    return lax.bitcast_convert_type(bits(x[:, :w]) | (bits(x[:, w:]) >> 16), jnp.int32)


def _unpack_halves(p):
    u = lax.bitcast_convert_type(p, jnp.uint32)
    return (lax.bitcast_convert_type(u & jnp.uint32(0xFFFF0000), _F32),
            lax.bitcast_convert_type(u << 16, _F32))


def _post_kernel(x_ref, oa_ref, ob_ref, woa_ref, wob_ref, gmoe_ref, r_ref, h1_ref, hn_ref, rt_ref):
    h1 = x_ref[...] + _dot(oa_ref[...], woa_ref[...]) + _dot(ob_ref[...], wob_ref[...])
    h1_ref[...] = h1
    hn = _rms(h1, gmoe_ref[...])
    hn_ref[...] = _pack_halves(hn)

    hn_hi = hn.astype(_BF16)
    hn_lo = (hn - hn_hi.astype(_F32)).astype(_BF16)
    hi_both = _dot(hn_hi, r_ref[...])
    logits = hi_both[:, :GATE_PAD] + (_dot(hn_lo, r_ref[:, :GATE_PAD]) + hi_both[:, GATE_PAD:])
    lane = lax.broadcasted_iota(jnp.int32, logits.shape, 1)
    first_idx = lambda hit: jnp.min(jnp.where(hit, lane, LANES), axis=-1, keepdims=True)

    is_g = lane < N_GROUPS
    lg = jnp.where(is_g, logits, NEG)
    mg = jnp.max(lg, axis=-1, keepdims=True)
    sg = jnp.sum(jnp.where(is_g, jnp.exp(lg - mg), 0.0), axis=-1, keepdims=True)
    pg_top = 1.0 / sg
    g_sel = first_idx(is_g & (lg == mg))

    e_lo = ROUTER_OFF + g_sel * EXPERTS_PER_GROUP
    is_e = (lane >= e_lo) & (lane < e_lo + EXPERTS_PER_GROUP)
    le = jnp.where(is_e, logits, NEG)
    m1 = jnp.max(le, axis=-1, keepdims=True)
    se = jnp.sum(jnp.where(is_e, jnp.exp(le - m1), 0.0), axis=-1, keepdims=True)
    i1 = first_idx(is_e & (le == m1))
    le2 = jnp.where(lane == i1, NEG, le)
    m2 = jnp.max(le2, axis=-1, keepdims=True)
    i2 = first_idx(is_e & (lane != i1) & (le2 == m2))
    pe1 = 1.0 / se
    pe2 = jnp.exp(m2 - m1) / se
    denom = pe1 + pe2
    rt_ref[...] = (jnp.where(lane == i1 - ROUTER_OFF, 1.0, 0.0)
                   + jnp.where(lane == i2 - ROUTER_OFF + N_EXPERTS, 1.0, 0.0)
                   + jnp.where(lane == RT_W, pg_top * pe1 / denom, 0.0)
                   + jnp.where(lane == RT_W + 1, pg_top * pe2 / denom, 0.0))


def _post_call(x2, oa, ob, woa, wob, gmoe, r_cat):
    n = x2.shape[0]
    tm = TM_POST
    row = lambda i: (i, 0)
    const2 = lambda i: (0, 0)
    return pl.pallas_call(
        _post_kernel,
        grid=(n // tm,),
        in_specs=[pl.BlockSpec((tm, D_MODEL), row), pl.BlockSpec((tm, A_WIDTH), row),
                  pl.BlockSpec((tm, B_WIDTH), row), pl.BlockSpec((A_WIDTH, D_MODEL), const2),
                  pl.BlockSpec((B_WIDTH, D_MODEL), const2), pl.BlockSpec((1, D_MODEL), const2),
                  pl.BlockSpec((D_MODEL, 2 * GATE_PAD), const2)],
        out_specs=[pl.BlockSpec((tm, D_MODEL), row), pl.BlockSpec((tm, D_MODEL // 2), row),
                   pl.BlockSpec((tm, GATE_PAD), row)],
        out_shape=[jax.ShapeDtypeStruct((n, D_MODEL), _F32), jax.ShapeDtypeStruct((n, D_MODEL // 2), jnp.int32),
                   jax.ShapeDtypeStruct((n, GATE_PAD), _F32)],
        compiler_params=pltpu.CompilerParams(dimension_semantics=("arbitrary",), vmem_limit_bytes=VMEM_LIMIT),
        name="post",
    )(x2, oa, ob, woa, wob, gmoe, r_cat)


def _route_kernel(rt_ref, dest_ref, meta_ref, cnt_ref, off_ref, run_ref):
    phase = pl.program_id(0)
    i = pl.program_id(1)
    tm = rt_ref.shape[0]
    lane = lax.broadcasted_iota(jnp.int32, (1, LANES), 1)
    first = lane < N_EXPERTS
    onehot = jnp.where(lane < 2 * N_EXPERTS, rt_ref[...], 0.0)

    @pl.when((phase == 0) & (i == 0))
    def _():
        cnt_ref[...] = jnp.zeros_like(cnt_ref)

    @pl.when(phase == 0)
    def _():
        cnt_ref[...] += jnp.sum(onehot, axis=0, keepdims=True)

    @pl.when((phase == 1) & (i == 0))
    def _():
        cnt = cnt_ref[...]
        c1 = jnp.where(first, cnt, 0.0)
        tot = c1 + jnp.where(first, pltpu.roll(cnt, LANES - N_EXPERTS, 1), 0.0)
        tiles = jnp.floor((tot + (TR_GMM - 1)) * (1.0 / TR_GMM))
        e_row = lax.broadcasted_iota(jnp.int32, (LANES, LANES), 0)
        e_col = lax.broadcasted_iota(jnp.int32, (LANES, LANES), 1)
        before = jnp.where(e_row < e_col, 1.0, 0.0).astype(_BF16)
        base = _dot(tiles.astype(_BF16), before) * TR_GMM
        off_ref[...] = jnp.where(first, base, 0.0) + pltpu.roll(jnp.where(first, base + c1, 0.0), N_EXPERTS, 1)
        run_ref[...] = jnp.zeros_like(run_ref)
        meta_ref[...] = tiles

    @pl.when(phase == 1)
    def _():
        r_io = lax.broadcasted_iota(jnp.int32, (tm, tm), 0)
        c_io = lax.broadcasted_iota(jnp.int32, (tm, tm), 1)
        earlier = jnp.where(c_io < r_io, 1.0, 0.0).astype(_BF16)
        rank = _dot(earlier, onehot.astype(_BF16)) + run_ref[0:1, :]
        slot = onehot * (rank + off_ref[0:1, :])
        d1 = jnp.sum(jnp.where(first, slot, 0.0), axis=-1, keepdims=True)
        d2 = jnp.sum(jnp.where(first, 0.0, slot), axis=-1, keepdims=True)
        dest_ref[...] = (jnp.where(lane == 0, d1, 0.0) + jnp.where(lane == 1, d2, 0.0)).astype(jnp.int32)
        run_ref[...] += jnp.sum(onehot, axis=0, keepdims=True)


def _route_call(rt):
    n = rt.shape[0]
    tm = TM_POST
    return pl.pallas_call(
        _route_kernel,
        grid=(2, n // tm),
        in_specs=[pl.BlockSpec((tm, GATE_PAD), lambda ph, i: (i, 0))],
        out_specs=[pl.BlockSpec((tm, LANES), lambda ph, i: (i * ph, 0)),
                   pl.BlockSpec((8, LANES), lambda ph, i: (0, 0))],
        out_shape=[jax.ShapeDtypeStruct((n, LANES), jnp.int32), jax.ShapeDtypeStruct((8, LANES), _F32)],
        scratch_shapes=[pltpu.VMEM((8, LANES), _F32)] * 3,
        compiler_params=pltpu.CompilerParams(dimension_semantics=("arbitrary", "arbitrary")),
        name="route",
    )(rt)


def _sc_mesh():
    return plsc.VectorSubcoreMesh(core_axis_name="c", subcore_axis_name="s")


def _sc_worker(n_rows):
    per = n_rows // (SC_CORES * SC_SUBCORES)
    return (lax.axis_index("s") * SC_CORES + lax.axis_index("c")) * per, per


def _dispatch_call(xp, d1, d2, n_slots):
    n, w = xp.shape
    assert n % (SC_CORES * SC_SUBCORES * SC_CHUNK) == 0

    @functools.partial(
        pl.kernel, mesh=_sc_mesh(), out_type=jax.ShapeDtypeStruct((n_slots, w), xp.dtype),
        scratch_types=[pltpu.VMEM((SC_CHUNK,), jnp.int32), pltpu.VMEM((SC_CHUNK,), jnp.int32),
                       pltpu.VMEM((SC_CHUNK, w), xp.dtype), pltpu.SemaphoreType.DMA],
        name="dispatch")
    def k(x_hbm, d1_hbm, d2_hbm, xs_hbm, i1_v, i2_v, rows_v, sem):
        row0, per = _sc_worker(n)

        @pl.loop(0, per // SC_CHUNK)
        def _(j):
            src = pl.ds(row0 + j * SC_CHUNK, SC_CHUNK)
            pltpu.sync_copy(d1_hbm.at[src], i1_v)
            pltpu.sync_copy(d2_hbm.at[src], i2_v)
            pltpu.sync_copy(x_hbm.at[src], rows_v)
            first = pltpu.async_copy(rows_v, xs_hbm.at[i1_v], sem)
            second = pltpu.async_copy(rows_v, xs_hbm.at[i2_v], sem)
            first.wait()
            second.wait()

    return k(xp, d1, d2)


def _combine_call(ys, d1, d2):
    n = d1.shape[0]
    w = ys.shape[1]
    assert n % (SC_CORES * SC_SUBCORES * SC_CHUNK) == 0
    out = jax.ShapeDtypeStruct((n, w), ys.dtype)

    @functools.partial(
        pl.kernel, mesh=_sc_mesh(), out_type=(out, out),
        scratch_types=[pltpu.VMEM((SC_CHUNK,), jnp.int32), pltpu.VMEM((SC_CHUNK, w), ys.dtype),
                       pltpu.SemaphoreType.DMA],
        name="combine")
    def k(ys_hbm, d1_hbm, d2_hbm, y1_hbm, y2_hbm, i_v, rows_v, sem):
        row0, per = _sc_worker(n)

        @pl.loop(0, per // SC_CHUNK)
        def _(j):
            dst = pl.ds(row0 + j * SC_CHUNK, SC_CHUNK)
            for d_hbm, y_hbm in ((d1_hbm, y1_hbm), (d2_hbm, y2_hbm)):
                pltpu.sync_copy(d_hbm.at[dst], i_v)
                pltpu.async_copy(ys_hbm.at[i_v], rows_v, sem).wait()
                pltpu.sync_copy(rows_v, y_hbm.at[dst])

    return k(ys, d1, d2)


def _gmm_kernel(te_ref, nu_ref, xs_ref, wgu_ref, wd_ref, ys_ref):
    half = D_MODEL // 2

    @pl.when(pl.program_id(0) < nu_ref[0])
    def _():
        for r in range(TR_GMM // SUB_MOE):
            rows = slice(r * SUB_MOE, (r + 1) * SUB_MOE)
            a, b = _unpack_halves(xs_ref[rows, :])
            gu = _dot(a.astype(_BF16), wgu_ref[0, :half, :]) + _dot(b.astype(_BF16), wgu_ref[0, half:, :])
            gate = gu[:, :D_FF_EXPERT]
            hid = gate * jax.nn.sigmoid(gate) * gu[:, D_FF_EXPERT:]
            ys_ref[rows, :] = _pack_halves(_dot(hid.astype(_BF16), wd_ref[0]))


def _gmm_call(tile_expert, n_used, xs, wgu, wd):
    n_slots, w = xs.shape
    rows = lambda t, te, nu: (jnp.minimum(t, nu[0] - 1), 0)
    expert = lambda t, te, nu: (te[t], 0, 0)
    return pl.pallas_call(
        _gmm_kernel,
        grid_spec=pltpu.PrefetchScalarGridSpec(
            num_scalar_prefetch=2, grid=(n_slots // TR_GMM,),
            in_specs=[pl.BlockSpec((TR_GMM, w), rows),
                      pl.BlockSpec((1, D_MODEL, 2 * D_FF_EXPERT), expert),
                      pl.BlockSpec((1, D_FF_EXPERT, D_MODEL), expert)],
            out_specs=pl.BlockSpec((TR_GMM, w), rows)),
        out_shape=jax.ShapeDtypeStruct((n_slots, w), xs.dtype),
        compiler_params=pltpu.CompilerParams(dimension_semantics=("arbitrary",), vmem_limit_bytes=VMEM_LIMIT),
        name="gmm",
    )(tile_expert, n_used, xs, wgu, wd)


def _final_kernel(h1_ref, y1_ref, y2_ref, rt_ref, p_ref, gple_ref, wpg_ref, wpp_ref, gfin_ref, o_ref):
    for r in range(h1_ref.shape[0] // SUB_MOE):
        rows = slice(r * SUB_MOE, (r + 1) * SUB_MOE)
        rt = rt_ref[rows, :]
        w1, w2 = rt[:, RT_W:RT_W + 1], rt[:, RT_W + 1:RT_W + 2]
        a1, b1 = _unpack_halves(y1_ref[rows, :])
        a2, b2 = _unpack_halves(y2_ref[rows, :])
        h2 = h1_ref[rows, :] + jnp.concatenate([w1 * a1 + w2 * a2, w1 * b1 + w2 * b2], axis=1)
        gate = jax.nn.sigmoid(_dot(_rms(h2, gple_ref[...]).astype(_BF16), wpg_ref[...]))
        h3 = h2 + _dot(p_ref[rows, :].astype(_BF16), wpp_ref[...]) * gate
        o_ref[rows, :] = _rms(h3, gfin_ref[...])


def _final_call(h1, y1, y2, rt, p2, gple, wpg, wpp, gfin):
    n = h1.shape[0]
    tm = TM_POST
    row = lambda i: (i, 0)
    const2 = lambda i: (0, 0)
    return pl.pallas_call(
        _final_kernel,
        grid=(n // tm,),
        in_specs=[pl.BlockSpec((tm, D_MODEL), row), pl.BlockSpec((tm, D_MODEL // 2), row),
                  pl.BlockSpec((tm, D_MODEL // 2), row), pl.BlockSpec((tm, GATE_PAD), row),
                  pl.BlockSpec((tm, D_PLE), row), pl.BlockSpec((1, D_MODEL), const2),
                  pl.BlockSpec((D_MODEL, D_MODEL), const2), pl.BlockSpec((D_PLE, D_MODEL), const2),
                  pl.BlockSpec((1, D_MODEL), const2)],
        out_specs=pl.BlockSpec((tm, D_MODEL), row),
        out_shape=jax.ShapeDtypeStruct((n, D_MODEL), _F32),
        compiler_params=pltpu.CompilerParams(dimension_semantics=("arbitrary",), vmem_limit_bytes=VMEM_LIMIT),
        name="final",
    )(h1, y1, y2, rt, p2, gple, wpg, wpp, gfin)


def _rope_tables(seq):
    half = HEAD_DIM // 2
    inv = 1.0 / (ROPE_THETA ** (jnp.arange(half, dtype=_F32) / half))
    ang = jnp.arange(seq, dtype=_F32)[:, None] * inv[None, :]
    cos, sin = jnp.cos(ang), jnp.sin(ang)
    reps = LANES // HEAD_DIM
    cos_t = jnp.tile(jnp.concatenate([cos, cos], axis=1), (1, reps))
    sin_t = jnp.tile(jnp.concatenate([-sin, sin], axis=1), (1, reps))
    return cos_t, sin_t


def _layer(h, p_i, norm_mix, w_in, gmlp_v_norm, gmlp_w_s, gmlp_b_s,
           cmp_pe_k, cmp_w1_k, cmp_w2_k, cmp_pe_v, cmp_w1_v, cmp_w2_v,
           out_norm_a, out_norm_b, w_o, norm_moe, router_group, router_expert,
           moe_w_gate, moe_w_up, moe_w_down, norm_ple, w_ple_proj, w_ple_gate, norm_final):
    bsz, seq, _ = h.shape
    n = bsz * seq
    x2 = h.reshape(n, D_MODEL)
    row = lambda v: v.reshape(1, -1).astype(_F32)

    w_q = w_in[:, OFF_Q:OFF_KV][:, _PERM_CH]
    w_gate = jnp.pad(w_in[:, OFF_GATE:D_IN], ((0, 0), (0, GATE_PAD - N_GATES)))
    w_all = jnp.concatenate([w_in[:, :OFF_Q], w_q, w_in[:, OFF_KV:OFF_GATE], w_gate], axis=1).astype(_BF16)
    cos_t, sin_t = _rope_tables(seq)
    ws_pairs = gmlp_w_s.reshape(A_HEADS // 2, 2, CHUNK, CHUNK).transpose(0, 2, 1, 3).reshape(
        A_HEADS // 2, CHUNK, 2 * CHUNK)
    bs_exp = jnp.repeat(gmlp_b_s.T, HEAD_DIM, axis=1)

    oa, q, k_cmp, v_cmp, ks0, ks1, vs0, vs1, k_win, v_win, gates = _proj_call(
        x2, row(norm_mix), w_all, cos_t, sin_t, row(gmlp_v_norm), ws_pairs, bs_exp, row(out_norm_a), seq)

    kc, vc = _compress_call(k_cmp, v_cmp, _compress_weights(cmp_w1_k, cmp_w2_k, cmp_pe_k),
                            _compress_weights(cmp_w1_v, cmp_w2_v, cmp_pe_v), bsz, seq)

    b3 = lambda a: a.reshape(bsz, seq, a.shape[-1])
    ob = _attn_call(b3(q), b3(gates), kc, vc, b3(ks0), b3(ks1), b3(vs0), b3(vs1), b3(k_win), b3(v_win),
                    row(out_norm_b[_PERM_CH]), bsz, seq)

    r_cat = jnp.pad(jnp.concatenate([router_group, router_expert], axis=1),
                    ((0, 0), (0, GATE_PAD - N_GROUPS - N_EXPERTS)))
    r_hi = r_cat.astype(_BF16)
    r_cat = jnp.concatenate([r_hi, (r_cat - r_hi.astype(_F32)).astype(_BF16)], axis=1)
    h1, xp, rt = _post_call(x2, oa, ob.reshape(n, B_WIDTH), w_o[:A_WIDTH].astype(_BF16),
                            w_o[A_WIDTH:][_PERM_CH].astype(_BF16), row(norm_moe), r_cat)

    dest, meta = _route_call(rt)
    d1, d2 = dest[:, 0], dest[:, 1]
    n_tiles = 2 * n // TR_GMM + N_EXPERTS
    ends = jnp.cumsum(meta[0, :N_EXPERTS].astype(jnp.int32))
    tile_expert = jnp.minimum(jnp.sum(ends[None, :] <= jnp.arange(n_tiles)[:, None], axis=1),
                              N_EXPERTS - 1).astype(jnp.int32)
    xs = _dispatch_call(xp, d1, d2, n_tiles * TR_GMM)
    wgu = jnp.concatenate([moe_w_gate, moe_w_up], axis=-1).astype(_BF16)
    ys = _gmm_call(tile_expert, ends[-1:], xs, wgu, moe_w_down.astype(_BF16))
    y1, y2 = _combine_call(ys, d1, d2)
    out = _final_call(h1, y1, y2, rt, p_i.reshape(n, D_PLE), row(norm_ple),
                      w_ple_gate.astype(_BF16), w_ple_proj.astype(_BF16), row(norm_final))
    return out.reshape(bsz, seq, D_MODEL)


def kernel(x, p, norm_mix, w_in, gmlp_v_norm, gmlp_w_s, gmlp_b_s, cmp_pe_k, cmp_w1_k, cmp_w2_k,
           cmp_pe_v, cmp_w1_v, cmp_w2_v, out_norm_a, out_norm_b, w_o, norm_moe, router_group,
           router_expert, moe_w_gate, moe_w_up, moe_w_down, norm_ple, w_ple_proj, w_ple_gate, norm_final):
    assert p.shape[0] == 1, "single-layer trunk"
    assert x.shape[1] % SEL_CK == 0 and x.shape[1] >= WIN + Q_BLOCK
    assert (x.shape[0] * x.shape[1]) % (SC_CORES * SC_SUBCORES * SC_CHUNK) == 0
    return _layer(x, p[0], norm_mix[0], w_in[0], gmlp_v_norm[0], gmlp_w_s[0], gmlp_b_s[0],
                  cmp_pe_k[0], cmp_w1_k[0], cmp_w2_k[0], cmp_pe_v[0], cmp_w1_v[0], cmp_w2_v[0],
                  out_norm_a[0], out_norm_b[0], w_o[0], norm_moe[0], router_group[0], router_expert[0],
                  moe_w_gate[0], moe_w_up[0], moe_w_down[0], norm_ple[0], w_ple_proj[0], w_ple_gate[0],
                  norm_final)
```

```python
import functools

import numpy as np
import jax
import jax.numpy as jnp
from jax import lax
from jax.experimental import pallas as pl
from jax.experimental.pallas import tpu as pltpu
from jax.experimental.pallas import tpu_sc as plsc

D_MODEL = 1024
HEAD_DIM = 64
A_HEADS = 8
A_WIDTH = A_HEADS * HEAD_DIM
B_HEADS = 8
B_WIDTH = B_HEADS * HEAD_DIM
B_KV = 2
B_HPG = B_HEADS // B_KV
KV_W = B_KV * HEAD_DIM
N_GATES = B_HEADS * 3
CHUNK = 128
L_CMP = 32
STRIDE_CMP = 16
CMP_HIDDEN = 256
L_SEL = 64
N_SEL = 16
WIN = 512
Q_BLOCK = 256
ROPE_THETA = 10000.0
N_GROUPS = 4
EXPERTS_PER_GROUP = 4
N_EXPERTS = N_GROUPS * EXPERTS_PER_GROUP
D_FF_EXPERT = 512
D_PLE = 256
EPS = 1e-6
LOG2E = 1.4426950408889634
NEG = -1e30
FORCE = 1e6

OFF_Q = 2 * A_WIDTH
OFF_KV = OFF_Q + B_WIDTH
OFF_GATE = OFF_KV + 6 * KV_W
D_IN = OFF_GATE + N_GATES

LANES = 128
GATE_PAD = LANES
ROUTER_OFF = N_GROUPS
W_ALL = OFF_GATE + GATE_PAD

TM_PROJ = 512
TM_POST = 512
TR_GMM = 512
SUB_MOE = 256
RT_W = 2 * N_EXPERTS
SC_CORES = 2
SC_SUBCORES = 16
SC_CHUNK = 128
SEL_CK = 512
VMEM_LIMIT = 56 * 1024 * 1024

_PERM_HEADS = [0, 4, 1, 5, 2, 6, 3, 7]
_PERM_CH = np.concatenate([np.arange(HEAD_DIM) + HEAD_DIM * h for h in _PERM_HEADS])

_F32 = jnp.float32
_BF16 = jnp.bfloat16


def _dot(a, b):
    return jnp.dot(a, b, preferred_element_type=_F32)


def _dot_nt(a, b):
    return lax.dot_general(a, b, (((1,), (1,)), ((), ())), preferred_element_type=_F32)


def _rms(x, g):
    return x * lax.rsqrt(jnp.mean(x * x, axis=-1, keepdims=True) + EPS) * g


def _gelu(x):
    return 0.5 * x * (1.0 + jnp.tanh(0.7978845608028654 * (x + 0.044715 * (x * x * x))))


def _rope_tile(x, cos, sin_signed):
    lane = lax.broadcasted_iota(jnp.int32, x.shape, 1)
    first_half = (lane % HEAD_DIM) < (HEAD_DIM // 2)
    rot = jnp.where(first_half, pltpu.roll(x, LANES - HEAD_DIM // 2, 1), pltpu.roll(x, HEAD_DIM // 2, 1))
    return x * cos + rot * sin_signed


def _proj_kernel(x_ref, gmix_ref, w_ref, cos_ref, sin_ref, gv_ref, ws_ref, bs_ref, goa_ref,
                 oa_ref, q_ref, kc_ref, vc_ref, ks0_ref, ks1_ref, vs0_ref, vs1_ref, kw_ref, vw_ref, gate_ref,
                 *, seq):
    tm = x_ref.shape[0]
    hn = _rms(x_ref[...], gmix_ref[...]).astype(_BF16)
    cos = cos_ref[...]
    sin = sin_ref[...]

    zq = _dot(hn, w_ref[:, OFF_Q:OFF_KV])
    scale = HEAD_DIM ** -0.5 * LOG2E
    for j in range(B_WIDTH // LANES):
        blk = _rope_tile(zq[:, j * LANES:(j + 1) * LANES], cos, sin) * scale
        q_ref[:, j * LANES:(j + 1) * LANES] = blk.astype(q_ref.dtype)

    zkv = _dot(hn, w_ref[:, OFF_KV:OFF_GATE])
    kv = []
    for j in range(6):
        blk = zkv[:, j * KV_W:(j + 1) * KV_W]
        kv.append(_rope_tile(blk, cos, sin) if j % 2 == 0 else blk)
    k_cmp, v_cmp, k_slc, v_slc, k_win, v_win = kv
    kc_ref[...] = k_cmp.astype(kc_ref.dtype)
    vc_ref[...] = v_cmp.astype(vc_ref.dtype)
    kw_ref[...] = k_win.astype(kw_ref.dtype)
    vw_ref[...] = v_win.astype(vw_ref.dtype)

    row = lax.broadcasted_iota(jnp.int32, (tm, LANES), 0)
    lane = lax.broadcasted_iota(jnp.int32, (tm, LANES), 1)
    key_block = ((pl.program_id(0) % (seq // tm)) * tm + row) // L_SEL
    onehot = jnp.where(lane == key_block, 1.0, 0.0).astype(ks0_ref.dtype)
    for g, (ks_ref, vs_ref) in enumerate(((ks0_ref, vs0_ref), (ks1_ref, vs1_ref))):
        own = (lane // HEAD_DIM) == g
        ks_ref[:, :LANES] = jnp.where(own, k_slc, 0.0).astype(ks_ref.dtype)
        ks_ref[:, LANES:] = onehot
        vs_ref[...] = jnp.where(own, v_slc, 1.0).astype(vs_ref.dtype)

    zg = _dot(hn, w_ref[:, OFF_GATE:W_ALL])
    gate_ref[...] = jax.nn.sigmoid(zg)

    zu = _gelu(_dot(hn, w_ref[:, 0:A_WIDTH]))
    zv = _gelu(_dot(hn, w_ref[:, A_WIDTH:2 * A_WIDTH]))
    vn = _rms(zv, gv_ref[...]).astype(_BF16)

    t_io = lax.broadcasted_iota(jnp.int32, (CHUNK, 2 * CHUNK), 0)
    s_io = lax.broadcasted_iota(jnp.int32, (CHUNK, 2 * CHUNK), 1) % CHUNK
    causal = s_io <= t_io
    lane = lax.broadcasted_iota(jnp.int32, (CHUNK, LANES), 1)
    lo = lane < HEAD_DIM
    bs = bs_ref[...]
    chunks = []
    for c in range(tm // CHUNK):
        tiles = []
        for pr in range(A_HEADS // 2):
            wcat = jnp.where(causal, ws_ref[pr], 0.0).astype(_BF16)
            vblk = vn[c * CHUNK:(c + 1) * CHUNK, pr * LANES:(pr + 1) * LANES]
            zero = jnp.zeros_like(vblk)
            rhs = jnp.concatenate([jnp.where(lo, vblk, zero), jnp.where(lo, zero, vblk)], axis=0)
            tiles.append(_dot(wcat, rhs))
        chunks.append(jnp.concatenate(tiles, axis=1) + bs)
    mixed = jnp.concatenate(chunks, axis=0)
    oa = zu * mixed
    oa_ref[...] = _rms(oa, goa_ref[...]).astype(oa_ref.dtype)


def _proj_call(x2, gmix, w_all, cos_t, sin_t, gv, ws_pairs, bs_exp, goa, seq):
    n = x2.shape[0]
    tm = TM_PROJ
    n_t = seq // tm
    row = lambda i: (i, 0)
    const2 = lambda i: (0, 0)
    pos = lambda i: (i % n_t, 0)
    out_shapes = [jax.ShapeDtypeStruct((n, A_WIDTH), _BF16), jax.ShapeDtypeStruct((n, B_WIDTH), _BF16)]
    kv_widths = [KV_W, KV_W, KV_W + LANES, KV_W + LANES, KV_W, KV_W, KV_W, KV_W]
    out_shapes += [jax.ShapeDtypeStruct((n, w), _BF16) for w in kv_widths]
    out_shapes += [jax.ShapeDtypeStruct((n, GATE_PAD), _F32)]
    out_specs = [pl.BlockSpec((tm, A_WIDTH), row), pl.BlockSpec((tm, B_WIDTH), row)]
    out_specs += [pl.BlockSpec((tm, w), row) for w in kv_widths]
    out_specs += [pl.BlockSpec((tm, GATE_PAD), row)]
    return pl.pallas_call(
        functools.partial(_proj_kernel, seq=seq),
        grid=(n // tm,),
        in_specs=[
            pl.BlockSpec((tm, D_MODEL), row),
            pl.BlockSpec((1, D_MODEL), const2),
            pl.BlockSpec((D_MODEL, W_ALL), const2),
            pl.BlockSpec((tm, LANES), pos),
            pl.BlockSpec((tm, LANES), pos),
            pl.BlockSpec((1, A_WIDTH), const2),
            pl.BlockSpec((A_HEADS // 2, CHUNK, 2 * CHUNK), lambda i: (0, 0, 0)),
            pl.BlockSpec((CHUNK, A_WIDTH), const2),
            pl.BlockSpec((1, A_WIDTH), const2),
        ],
        out_specs=out_specs,
        out_shape=out_shapes,
        compiler_params=pltpu.CompilerParams(dimension_semantics=("arbitrary",), vmem_limit_bytes=VMEM_LIMIT),
        name="proj",
    )(x2, gmix, w_all, cos_t, sin_t, gv, ws_pairs, bs_exp, goa)


def _compress_kernel(rk_ref, rv_ref, pek_ref, pev_ref, w1k_ref, w1v_ref,
                     tk_ref, bk_ref, w2k_ref, tv_ref, bv_ref, w2v_ref, kc_ref, vc_ref):
    def one(r_ref, pe_ref, w1_ref, top_ref, bot_ref, w2_ref, o_ref):
        r = r_ref[0]
        nr = r.shape[0]
        a = _dot(r, top_ref[...])
        b = _dot(r, bot_ref[...])
        pe_h = _dot(pe_ref[...], w1_ref[...])
        pe2 = jnp.concatenate([pe_h[0:1], pe_h[0:1]], axis=1)
        hid = a + pltpu.roll(b, nr - 1, 0) + pe2
        o_ref[0] = _dot(_gelu(hid).astype(_BF16), w2_ref[...]).astype(o_ref.dtype)

    one(rk_ref, pek_ref, w1k_ref, tk_ref, bk_ref, w2k_ref, kc_ref)
    one(rv_ref, pev_ref, w1v_ref, tv_ref, bv_ref, w2v_ref, vc_ref)


def _compress_weights(w1, w2, pe):
    half = L_CMP // 2
    w1r = w1.reshape(L_CMP, HEAD_DIM, CMP_HIDDEN)
    eye = jnp.eye(B_KV, dtype=w1.dtype)
    place = lambda part: jnp.einsum('ldj,gh->lgdhj', part, eye).reshape(half * KV_W, B_KV * CMP_HIDDEN)
    top = place(w1r[:half]).astype(_BF16)
    bot = place(w1r[half:]).astype(_BF16)
    w2bd = jnp.einsum('jd,gh->gjhd', w2, eye).reshape(B_KV * CMP_HIDDEN, KV_W).astype(_BF16)
    pe8 = jnp.broadcast_to(pe.reshape(1, L_CMP * HEAD_DIM), (8, L_CMP * HEAD_DIM)).astype(_BF16)
    return pe8, w1.astype(_BF16), top, bot, w2bd


def _compress_call(k_cmp, v_cmp, wk, wv, bsz, seq):
    nr = seq // STRIDE_CMP
    rk = k_cmp.reshape(bsz, nr, STRIDE_CMP * KV_W)
    rv = v_cmp.reshape(bsz, nr, STRIDE_CMP * KV_W)
    pek, w1k, tk, bk, w2k = wk
    pev, w1v, tv, bv, w2v = wv
    rspec = pl.BlockSpec((1, nr, STRIDE_CMP * KV_W), lambda b: (b, 0, 0))
    full = lambda a: pl.BlockSpec(a.shape, lambda b: (0,) * a.ndim)
    ospec = pl.BlockSpec((1, nr, KV_W), lambda b: (b, 0, 0))
    return pl.pallas_call(
        _compress_kernel,
        grid=(bsz,),
        in_specs=[rspec, rspec, full(pek), full(pev), full(w1k), full(w1v),
                  full(tk), full(bk), full(w2k), full(tv), full(bv), full(w2v)],
        out_specs=[ospec, ospec],
        out_shape=[jax.ShapeDtypeStruct((bsz, nr, KV_W), _BF16)] * 2,
        compiler_params=pltpu.CompilerParams(dimension_semantics=("arbitrary",), vmem_limit_bytes=VMEM_LIMIT),
        name="compress",
    )(rk, rv, pek, pev, w1k, w1v, tk, bk, w2k, tv, bv, w2v)


def _topk_rows_mask(sc_t, k):
    n_rows = sc_t.shape[0]
    row = lax.broadcasted_iota(jnp.int32, sc_t.shape, 0)
    sel = jnp.zeros(sc_t.shape, _F32)
    for _ in range(k):
        m = jnp.max(sc_t, axis=0, keepdims=True)
        idx = jnp.min(jnp.where(sc_t == m, row, n_rows), axis=0, keepdims=True)
        chosen = row == idx
        sel = jnp.where(chosen, 1.0, sel)
        sc_t = jnp.where(chosen, -3e38, sc_t)
    return sel


def _attn_kernel(q_ref, gate_ref, gexp_ref, kc_ref, vc_ref, ks0_ref, ks1_ref, vs0_ref, vs1_ref,
                 kw_ref, vw_ref, gob_ref, o_ref, m_scr, acc_scr, *, seq):
    qb = pl.program_id(1)
    t0 = qb * Q_BLOCK
    n_cmp = kc_ref.shape[1]
    n_sb = seq // L_SEL
    k_top = min(N_SEL, n_sb)
    rows = B_HPG * Q_BLOCK
    ks_refs, vs_refs = (ks0_ref, ks1_ref), (vs0_ref, vs1_ref)

    lane_q = lax.broadcasted_iota(jnp.int32, (Q_BLOCK, LANES), 1)
    lo = lane_q < HEAD_DIM
    t_col = t0 + lax.broadcasted_iota(jnp.int32, (Q_BLOCK, 1), 0)

    def per_head(x):
        return x.reshape(B_HPG, Q_BLOCK, x.shape[-1])

    def add_bias(s, bias):
        return (per_head(s) + bias[None]).reshape(rows, s.shape[-1])

    qs = []
    for g in range(B_KV):
        own = (lane_q // HEAD_DIM) == g
        qs.append(jnp.concatenate(
            [jnp.where(own, q_ref[0, :, j * LANES:(j + 1) * LANES], jnp.zeros((), q_ref.dtype))
             for j in range(B_HPG)], axis=0))

    c_lane = lax.broadcasted_iota(jnp.int32, (Q_BLOCK, n_cmp), 1)
    bias_c = jnp.where(c_lane * STRIDE_CMP + (L_CMP - 1) <= t_col, 0.0, NEG)
    has_c = (t_col >= L_CMP - 1).astype(_F32)
    c_row = lax.broadcasted_iota(jnp.int32, (n_cmp, n_sb), 0) * STRIDE_CMP
    s_col = lax.broadcasted_iota(jnp.int32, (n_cmp, n_sb), 1) * L_SEL
    overlap = jnp.where((c_row < s_col + L_SEL) & (c_row + L_CMP > s_col), 1.0, 0.0).astype(_BF16)
    blk = lax.broadcasted_iota(jnp.int32, (Q_BLOCK, n_sb), 1)
    cur = t_col // L_SEL
    forced = (blk == 0) | (blk == cur) | (blk == cur - 1)
    valid = blk * L_SEL <= t_col

    o_c, sel_bias = [], []
    for g in range(B_KV):
        s_c = add_bias(_dot_nt(qs[g], kc_ref[0]), bias_c)
        e_c = jnp.exp2(s_c - jnp.max(s_c, axis=-1, keepdims=True))
        inv = per_head(1.0 / jnp.maximum(jnp.sum(e_c, axis=-1, keepdims=True), 1e-30)) * has_c[None]
        p_c = (per_head(e_c) * inv).reshape(rows, n_cmp)
        o_c.append(_dot(p_c.astype(_BF16), vc_ref[0]))

        p_sum = jnp.sum(per_head(p_c), axis=0)
        p_hi = p_sum.astype(_BF16)
        p_lo = (p_sum - p_hi.astype(_F32)).astype(_BF16)
        imp = _dot(p_hi, overlap) + _dot(p_lo, overlap)
        score = jnp.where(forced, FORCE, jnp.where(valid, imp, -FORCE))
        chosen = _topk_rows_mask(score.T, k_top).T > 0.5
        sb = jnp.where(chosen & valid, 0.0, NEG)
        if n_sb < LANES:
            sb = jnp.concatenate([sb, jnp.full((Q_BLOCK, LANES - n_sb), NEG, _F32)], axis=1)
        sel_bias.append(sb)

    n_ck = (t0 + Q_BLOCK + SEL_CK - 1) // SEL_CK
    key_lane = lax.broadcasted_iota(jnp.int32, (Q_BLOCK, SEL_CK), 1)
    bias_diag = jnp.where((n_ck - 1) * SEL_CK + key_lane <= t_col, 0.0, NEG)
    q_wide = [jnp.concatenate([qs[g], jnp.concatenate([sel_bias[g].astype(_BF16)] * B_HPG, axis=0)], axis=1)
              for g in range(B_KV)]

    m_scr[...] = jnp.full(m_scr.shape, NEG, _F32)
    acc_scr[...] = jnp.zeros(acc_scr.shape, _F32)

    def sel_chunk(ci, diag):
        k0 = pl.multiple_of(ci * SEL_CK, SEL_CK)
        for g in range(B_KV):
            m = m_scr[g]
            s = _dot_nt(q_wide[g], ks_refs[g][0, pl.ds(k0, SEL_CK), :])
            if diag:
                s = add_bias(s, bias_diag)
            m_new = jnp.maximum(m, jnp.max(s, axis=-1, keepdims=True))
            p = jnp.exp2(s - jnp.concatenate([m_new] * (SEL_CK // LANES), axis=1)).astype(_BF16)
            acc_scr[g] = jnp.exp2(m - m_new) * acc_scr[g] + _dot(p, vs_refs[g][0, pl.ds(k0, SEL_CK), :])
            m_scr[g] = m_new

    @pl.loop(0, n_ck - 1)
    def _(ci):
        sel_chunk(ci, False)

    sel_chunk(n_ck - 1, True)
    acc_s = [acc_scr[g] for g in range(B_KV)]

    w_start = pl.multiple_of(jnp.maximum(t0 - WIN, 0), Q_BLOCK)
    w_len = WIN + Q_BLOCK
    kpos_w = w_start + lax.broadcasted_iota(jnp.int32, (Q_BLOCK, w_len), 1)
    diff_w = t_col - kpos_w
    bias_w = jnp.where((diff_w >= 0) & (diff_w < WIN), 0.0, NEG)
    kw = kw_ref[0, pl.ds(w_start, w_len), :]
    vw = vw_ref[0, pl.ds(w_start, w_len), :]
    lane_w = lax.broadcasted_iota(jnp.int32, (w_len, LANES), 1)
    acc_w = []
    for g in range(B_KV):
        s_w = add_bias(_dot_nt(qs[g], kw), bias_w)
        e_w = jnp.exp2(s_w - jnp.max(s_w, axis=-1, keepdims=True)).astype(_BF16)
        vw_aug = jnp.where((lane_w // HEAD_DIM) == g, vw, jnp.ones((), vw.dtype))
        acc_w.append(_dot(e_w, vw_aug))

    def numer(acc):
        return jnp.concatenate([jnp.where(lo, acc[0][j * Q_BLOCK:(j + 1) * Q_BLOCK],
                                          acc[1][j * Q_BLOCK:(j + 1) * Q_BLOCK]) for j in range(B_HPG)], axis=1)

    def denom(acc):
        return jnp.concatenate([pltpu.roll(jnp.where(lo, acc[1][j * Q_BLOCK:(j + 1) * Q_BLOCK],
                                                     acc[0][j * Q_BLOCK:(j + 1) * Q_BLOCK]), HEAD_DIM, 1)
                                for j in range(B_HPG)], axis=1)

    gates = gate_ref[0]
    g_hi = gates.astype(_BF16)
    g_lo = (gates - g_hi.astype(_F32)).astype(_BF16)
    gate_of = lambda r: _dot(g_hi, gexp_ref[r]) + _dot(g_lo, gexp_ref[r])
    ob = (gate_of(0) * numer(o_c)
          + gate_of(1) * numer(acc_s) * (1.0 / jnp.maximum(denom(acc_s), 1e-30))
          + gate_of(2) * numer(acc_w) * (1.0 / jnp.maximum(denom(acc_w), 1e-30)))
    o_ref[0] = _rms(ob, gob_ref[...]).astype(o_ref.dtype)


def _gate_expand():
    x = np.zeros((3, GATE_PAD, B_WIDTH), np.float32)
    for slot, h in enumerate(_PERM_HEADS):
        for r in range(3):
            x[r, 3 * h + r, slot * HEAD_DIM:(slot + 1) * HEAD_DIM] = 1.0
    return jnp.asarray(x, _BF16)


def _attn_call(q, gates, kc, vc, ks0, ks1, vs0, vs1, kw, vw, gob, bsz, seq):
    assert seq // L_SEL <= LANES
    n_cmp = kc.shape[1]
    qspec = lambda w: pl.BlockSpec((1, Q_BLOCK, w), lambda b, i: (b, i, 0))
    full = lambda r, w=KV_W: pl.BlockSpec((1, r, w), lambda b, i: (b, 0, 0))
    return pl.pallas_call(
        functools.partial(_attn_kernel, seq=seq),
        grid=(bsz, seq // Q_BLOCK),
        in_specs=[qspec(B_WIDTH), qspec(GATE_PAD),
                  pl.BlockSpec((3, GATE_PAD, B_WIDTH), lambda b, i: (0, 0, 0)),
                  full(n_cmp), full(n_cmp), full(seq, KV_W + LANES), full(seq, KV_W + LANES),
                  full(seq), full(seq), full(seq), full(seq),
                  pl.BlockSpec((1, B_WIDTH), lambda b, i: (0, 0))],
        out_specs=qspec(B_WIDTH),
        out_shape=jax.ShapeDtypeStruct((bsz, seq, B_WIDTH), _BF16),
        scratch_shapes=[pltpu.VMEM((B_KV, B_HPG * Q_BLOCK, LANES), _F32),
                        pltpu.VMEM((B_KV, B_HPG * Q_BLOCK, LANES), _F32)],
        compiler_params=pltpu.CompilerParams(dimension_semantics=("arbitrary", "arbitrary"),
                                             vmem_limit_bytes=VMEM_LIMIT),
        name="attn",
    )(q, gates, _gate_expand(), kc, vc, ks0, ks1, vs0, vs1, kw, vw, gob)


def _pack_halves(x):
    w = x.shape[1] // 2
    bits = lambda v: lax.bitcast_convert_type(v.astype(_BF16).astype(_F32), jnp.uint32)
    return lax.bitcast_convert_type(bits(x[:, :w]) | (bits(x[:, w:]) >> 16), jnp.int32)


def _unpack_halves(p):
    u = lax.bitcast_convert_type(p, jnp.uint32)
    return (lax.bitcast_convert_type(u & jnp.uint32(0xFFFF0000), _F32),
            lax.bitcast_convert_type(u << 16, _F32))


def _post_kernel(x_ref, oa_ref, ob_ref, woa_ref, wob_ref, gmoe_ref, r_ref, h1_ref, hn_ref, rt_ref):
    h1 = x_ref[...] + _dot(oa_ref[...], woa_ref[...]) + _dot(ob_ref[...], wob_ref[...])
    h1_ref[...] = h1
    hn = _rms(h1, gmoe_ref[...])
    hn_ref[...] = _pack_halves(hn)

    hn_hi = hn.astype(_BF16)
    hn_lo = (hn - hn_hi.astype(_F32)).astype(_BF16)
    hi_both = _dot(hn_hi, r_ref[...])
    logits = hi_both[:, :GATE_PAD] + (_dot(hn_lo, r_ref[:, :GATE_PAD]) + hi_both[:, GATE_PAD:])
    lane = lax.broadcasted_iota(jnp.int32, logits.shape, 1)
    first_idx = lambda hit: jnp.min(jnp.where(hit, lane, LANES), axis=-1, keepdims=True)

    is_g = lane < N_GROUPS
    lg = jnp.where(is_g, logits, NEG)
    mg = jnp.max(lg, axis=-1, keepdims=True)
    sg = jnp.sum(jnp.where(is_g, jnp.exp(lg - mg), 0.0), axis=-1, keepdims=True)
    pg_top = 1.0 / sg
    g_sel = first_idx(is_g & (lg == mg))

    e_lo = ROUTER_OFF + g_sel * EXPERTS_PER_GROUP
    is_e = (lane >= e_lo) & (lane < e_lo + EXPERTS_PER_GROUP)
    le = jnp.where(is_e, logits, NEG)
    m1 = jnp.max(le, axis=-1, keepdims=True)
    se = jnp.sum(jnp.where(is_e, jnp.exp(le - m1), 0.0), axis=-1, keepdims=True)
    i1 = first_idx(is_e & (le == m1))
    le2 = jnp.where(lane == i1, NEG, le)
    m2 = jnp.max(le2, axis=-1, keepdims=True)
    i2 = first_idx(is_e & (lane != i1) & (le2 == m2))
    pe1 = 1.0 / se
    pe2 = jnp.exp(m2 - m1) / se
    denom = pe1 + pe2
    rt_ref[...] = (jnp.where(lane == i1 - ROUTER_OFF, 1.0, 0.0)
                   + jnp.where(lane == i2 - ROUTER_OFF + N_EXPERTS, 1.0, 0.0)
                   + jnp.where(lane == RT_W, pg_top * pe1 / denom, 0.0)
                   + jnp.where(lane == RT_W + 1, pg_top * pe2 / denom, 0.0))


def _post_call(x2, oa, ob, woa, wob, gmoe, r_cat):
    n = x2.shape[0]
    tm = TM_POST
    row = lambda i: (i, 0)
    const2 = lambda i: (0, 0)
    return pl.pallas_call(
        _post_kernel,
        grid=(n // tm,),
        in_specs=[pl.BlockSpec((tm, D_MODEL), row), pl.BlockSpec((tm, A_WIDTH), row),
                  pl.BlockSpec((tm, B_WIDTH), row), pl.BlockSpec((A_WIDTH, D_MODEL), const2),
                  pl.BlockSpec((B_WIDTH, D_MODEL), const2), pl.BlockSpec((1, D_MODEL), const2),
                  pl.BlockSpec((D_MODEL, 2 * GATE_PAD), const2)],
        out_specs=[pl.BlockSpec((tm, D_MODEL), row), pl.BlockSpec((tm, D_MODEL // 2), row),
                   pl.BlockSpec((tm, GATE_PAD), row)],
        out_shape=[jax.ShapeDtypeStruct((n, D_MODEL), _F32), jax.ShapeDtypeStruct((n, D_MODEL // 2), jnp.int32),
                   jax.ShapeDtypeStruct((n, GATE_PAD), _F32)],
        compiler_params=pltpu.CompilerParams(dimension_semantics=("arbitrary",), vmem_limit_bytes=VMEM_LIMIT),
        name="post",
    )(x2, oa, ob, woa, wob, gmoe, r_cat)


def _route_kernel(rt_ref, dest_ref, meta_ref, cnt_ref, off_ref, run_ref):
    phase = pl.program_id(0)
    i = pl.program_id(1)
    tm = rt_ref.shape[0]
    lane = lax.broadcasted_iota(jnp.int32, (1, LANES), 1)
    first = lane < N_EXPERTS
    onehot = jnp.where(lane < 2 * N_EXPERTS, rt_ref[...], 0.0)

    @pl.when((phase == 0) & (i == 0))
    def _():
        cnt_ref[...] = jnp.zeros_like(cnt_ref)

    @pl.when(phase == 0)
    def _():
        cnt_ref[...] += jnp.sum(onehot, axis=0, keepdims=True)

    @pl.when((phase == 1) & (i == 0))
    def _():
        cnt = cnt_ref[...]
        c1 = jnp.where(first, cnt, 0.0)
        tot = c1 + jnp.where(first, pltpu.roll(cnt, LANES - N_EXPERTS, 1), 0.0)
        tiles = jnp.floor((tot + (TR_GMM - 1)) * (1.0 / TR_GMM))
        e_row = lax.broadcasted_iota(jnp.int32, (LANES, LANES), 0)
        e_col = lax.broadcasted_iota(jnp.int32, (LANES, LANES), 1)
        before = jnp.where(e_row < e_col, 1.0, 0.0).astype(_BF16)
        base = _dot(tiles.astype(_BF16), before) * TR_GMM
        off_ref[...] = jnp.where(first, base, 0.0) + pltpu.roll(jnp.where(first, base + c1, 0.0), N_EXPERTS, 1)
        run_ref[...] = jnp.zeros_like(run_ref)
        meta_ref[...] = tiles

    @pl.when(phase == 1)
    def _():
        r_io = lax.broadcasted_iota(jnp.int32, (tm, tm), 0)
        c_io = lax.broadcasted_iota(jnp.int32, (tm, tm), 1)
        earlier = jnp.where(c_io < r_io, 1.0, 0.0).astype(_BF16)
        rank = _dot(earlier, onehot.astype(_BF16)) + run_ref[0:1, :]
        slot = onehot * (rank + off_ref[0:1, :])
        d1 = jnp.sum(jnp.where(first, slot, 0.0), axis=-1, keepdims=True)
        d2 = jnp.sum(jnp.where(first, 0.0, slot), axis=-1, keepdims=True)
        dest_ref[...] = (jnp.where(lane == 0, d1, 0.0) + jnp.where(lane == 1, d2, 0.0)).astype(jnp.int32)
        run_ref[...] += jnp.sum(onehot, axis=0, keepdims=True)


def _route_call(rt):
    n = rt.shape[0]
    tm = TM_POST
    return pl.pallas_call(
        _route_kernel,
        grid=(2, n // tm),
        in_specs=[pl.BlockSpec((tm, GATE_PAD), lambda ph, i: (i, 0))],
        out_specs=[pl.BlockSpec((tm, LANES), lambda ph, i: (i * ph, 0)),
                   pl.BlockSpec((8, LANES), lambda ph, i: (0, 0))],
        out_shape=[jax.ShapeDtypeStruct((n, LANES), jnp.int32), jax.ShapeDtypeStruct((8, LANES), _F32)],
        scratch_shapes=[pltpu.VMEM((8, LANES), _F32)] * 3,
        compiler_params=pltpu.CompilerParams(dimension_semantics=("arbitrary", "arbitrary")),
        name="route",
    )(rt)


def _sc_mesh():
    return plsc.VectorSubcoreMesh(core_axis_name="c", subcore_axis_name="s")


def _sc_worker(n_rows):
    per = n_rows // (SC_CORES * SC_SUBCORES)
    return (lax.axis_index("s") * SC_CORES + lax.axis_index("c")) * per, per


def _dispatch_call(xp, d1, d2, n_slots):
    n, w = xp.shape
    assert n % (SC_CORES * SC_SUBCORES * SC_CHUNK) == 0

    @functools.partial(
        pl.kernel, mesh=_sc_mesh(), out_type=jax.ShapeDtypeStruct((n_slots, w), xp.dtype),
        scratch_types=[pltpu.VMEM((SC_CHUNK,), jnp.int32), pltpu.VMEM((SC_CHUNK,), jnp.int32),
                       pltpu.VMEM((SC_CHUNK, w), xp.dtype), pltpu.SemaphoreType.DMA],
        name="dispatch")
    def k(x_hbm, d1_hbm, d2_hbm, xs_hbm, i1_v, i2_v, rows_v, sem):
        row0, per = _sc_worker(n)

        @pl.loop(0, per // SC_CHUNK)
        def _(j):
            src = pl.ds(row0 + j * SC_CHUNK, SC_CHUNK)
            pltpu.sync_copy(d1_hbm.at[src], i1_v)
            pltpu.sync_copy(d2_hbm.at[src], i2_v)
            pltpu.sync_copy(x_hbm.at[src], rows_v)
            first = pltpu.async_copy(rows_v, xs_hbm.at[i1_v], sem)
            second = pltpu.async_copy(rows_v, xs_hbm.at[i2_v], sem)
            first.wait()
            second.wait()

    return k(xp, d1, d2)


def _combine_call(ys, d1, d2):
    n = d1.shape[0]
    w = ys.shape[1]
    assert n % (SC_CORES * SC_SUBCORES * SC_CHUNK) == 0
    out = jax.ShapeDtypeStruct((n, w), ys.dtype)

    @functools.partial(
        pl.kernel, mesh=_sc_mesh(), out_type=(out, out),
        scratch_types=[pltpu.VMEM((SC_CHUNK,), jnp.int32), pltpu.VMEM((SC_CHUNK, w), ys.dtype),
                       pltpu.SemaphoreType.DMA],
        name="combine")
    def k(ys_hbm, d1_hbm, d2_hbm, y1_hbm, y2_hbm, i_v, rows_v, sem):
        row0, per = _sc_worker(n)

        @pl.loop(0, per // SC_CHUNK)
        def _(j):
            dst = pl.ds(row0 + j * SC_CHUNK, SC_CHUNK)
            for d_hbm, y_hbm in ((d1_hbm, y1_hbm), (d2_hbm, y2_hbm)):
                pltpu.sync_copy(d_hbm.at[dst], i_v)
                pltpu.async_copy(ys_hbm.at[i_v], rows_v, sem).wait()
                pltpu.sync_copy(rows_v, y_hbm.at[dst])

    return k(ys, d1, d2)


def _gmm_kernel(te_ref, nu_ref, xs_ref, wgu_ref, wd_ref, ys_ref):
    half = D_MODEL // 2

    @pl.when(pl.program_id(0) < nu_ref[0])
    def _():
        for r in range(TR_GMM // SUB_MOE):
            rows = slice(r * SUB_MOE, (r + 1) * SUB_MOE)
            a, b = _unpack_halves(xs_ref[rows, :])
            gu = _dot(a.astype(_BF16), wgu_ref[0, :half, :]) + _dot(b.astype(_BF16), wgu_ref[0, half:, :])
            gate = gu[:, :D_FF_EXPERT]
            hid = gate * jax.nn.sigmoid(gate) * gu[:, D_FF_EXPERT:]
            ys_ref[rows, :] = _pack_halves(_dot(hid.astype(_BF16), wd_ref[0]))


def _gmm_call(tile_expert, n_used, xs, wgu, wd):
    n_slots, w = xs.shape
    rows = lambda t, te, nu: (jnp.minimum(t, nu[0] - 1), 0)
    expert = lambda t, te, nu: (te[t], 0, 0)
    return pl.pallas_call(
        _gmm_kernel,
        grid_spec=pltpu.PrefetchScalarGridSpec(
            num_scalar_prefetch=2, grid=(n_slots // TR_GMM,),
            in_specs=[pl.BlockSpec((TR_GMM, w), rows),
                      pl.BlockSpec((1, D_MODEL, 2 * D_FF_EXPERT), expert),
                      pl.BlockSpec((1, D_FF_EXPERT, D_MODEL), expert)],
            out_specs=pl.BlockSpec((TR_GMM, w), rows)),
        out_shape=jax.ShapeDtypeStruct((n_slots, w), xs.dtype),
        compiler_params=pltpu.CompilerParams(dimension_semantics=("arbitrary",), vmem_limit_bytes=VMEM_LIMIT),
        name="gmm",
    )(tile_expert, n_used, xs, wgu, wd)


def _final_kernel(h1_ref, y1_ref, y2_ref, rt_ref, p_ref, gple_ref, wpg_ref, wpp_ref, gfin_ref, o_ref):
    for r in range(h1_ref.shape[0] // SUB_MOE):
        rows = slice(r * SUB_MOE, (r + 1) * SUB_MOE)
        rt = rt_ref[rows, :]
        w1, w2 = rt[:, RT_W:RT_W + 1], rt[:, RT_W + 1:RT_W + 2]
        a1, b1 = _unpack_halves(y1_ref[rows, :])
        a2, b2 = _unpack_halves(y2_ref[rows, :])
        h2 = h1_ref[rows, :] + jnp.concatenate([w1 * a1 + w2 * a2, w1 * b1 + w2 * b2], axis=1)
        gate = jax.nn.sigmoid(_dot(_rms(h2, gple_ref[...]).astype(_BF16), wpg_ref[...]))
        h3 = h2 + _dot(p_ref[rows, :].astype(_BF16), wpp_ref[...]) * gate
        o_ref[rows, :] = _rms(h3, gfin_ref[...])


def _final_call(h1, y1, y2, rt, p2, gple, wpg, wpp, gfin):
    n = h1.shape[0]
    tm = TM_POST
    row = lambda i: (i, 0)
    const2 = lambda i: (0, 0)
    return pl.pallas_call(
        _final_kernel,
        grid=(n // tm,),
        in_specs=[pl.BlockSpec((tm, D_MODEL), row), pl.BlockSpec((tm, D_MODEL // 2), row),
                  pl.BlockSpec((tm, D_MODEL // 2), row), pl.BlockSpec((tm, GATE_PAD), row),
                  pl.BlockSpec((tm, D_PLE), row), pl.BlockSpec((1, D_MODEL), const2),
                  pl.BlockSpec((D_MODEL, D_MODEL), const2), pl.BlockSpec((D_PLE, D_MODEL), const2),
                  pl.BlockSpec((1, D_MODEL), const2)],
        out_specs=pl.BlockSpec((tm, D_MODEL), row),
        out_shape=jax.ShapeDtypeStruct((n, D_MODEL), _F32),
        compiler_params=pltpu.CompilerParams(dimension_semantics=("arbitrary",), vmem_limit_bytes=VMEM_LIMIT),
        name="final",
    )(h1, y1, y2, rt, p2, gple, wpg, wpp, gfin)


def _rope_tables(seq):
    half = HEAD_DIM // 2
    inv = 1.0 / (ROPE_THETA ** (jnp.arange(half, dtype=_F32) / half))
    ang = jnp.arange(seq, dtype=_F32)[:, None] * inv[None, :]
    cos, sin = jnp.cos(ang), jnp.sin(ang)
    reps = LANES // HEAD_DIM
    cos_t = jnp.tile(jnp.concatenate([cos, cos], axis=1), (1, reps))
    sin_t = jnp.tile(jnp.concatenate([-sin, sin], axis=1), (1, reps))
    return cos_t, sin_t


def _layer(h, p_i, norm_mix, w_in, gmlp_v_norm, gmlp_w_s, gmlp_b_s,
           cmp_pe_k, cmp_w1_k, cmp_w2_k, cmp_pe_v, cmp_w1_v, cmp_w2_v,
           out_norm_a, out_norm_b, w_o, norm_moe, router_group, router_expert,
           moe_w_gate, moe_w_up, moe_w_down, norm_ple, w_ple_proj, w_ple_gate, norm_final):
    bsz, seq, _ = h.shape
    n = bsz * seq
    x2 = h.reshape(n, D_MODEL)
    row = lambda v: v.reshape(1, -1).astype(_F32)

    w_q = w_in[:, OFF_Q:OFF_KV][:, _PERM_CH]
    w_gate = jnp.pad(w_in[:, OFF_GATE:D_IN], ((0, 0), (0, GATE_PAD - N_GATES)))
    w_all = jnp.concatenate([w_in[:, :OFF_Q], w_q, w_in[:, OFF_KV:OFF_GATE], w_gate], axis=1).astype(_BF16)
    cos_t, sin_t = _rope_tables(seq)
    ws_pairs = gmlp_w_s.reshape(A_HEADS // 2, 2, CHUNK, CHUNK).transpose(0, 2, 1, 3).reshape(
        A_HEADS // 2, CHUNK, 2 * CHUNK)
    bs_exp = jnp.repeat(gmlp_b_s.T, HEAD_DIM, axis=1)

    oa, q, k_cmp, v_cmp, ks0, ks1, vs0, vs1, k_win, v_win, gates = _proj_call(
        x2, row(norm_mix), w_all, cos_t, sin_t, row(gmlp_v_norm), ws_pairs, bs_exp, row(out_norm_a), seq)

    kc, vc = _compress_call(k_cmp, v_cmp, _compress_weights(cmp_w1_k, cmp_w2_k, cmp_pe_k),
                            _compress_weights(cmp_w1_v, cmp_w2_v, cmp_pe_v), bsz, seq)

    b3 = lambda a: a.reshape(bsz, seq, a.shape[-1])
    ob = _attn_call(b3(q), b3(gates), kc, vc, b3(ks0), b3(ks1), b3(vs0), b3(vs1), b3(k_win), b3(v_win),
                    row(out_norm_b[_PERM_CH]), bsz, seq)

    r_cat = jnp.pad(jnp.concatenate([router_group, router_expert], axis=1),
                    ((0, 0), (0, GATE_PAD - N_GROUPS - N_EXPERTS)))
    r_hi = r_cat.astype(_BF16)
    r_cat = jnp.concatenate([r_hi, (r_cat - r_hi.astype(_F32)).astype(_BF16)], axis=1)
    h1, xp, rt = _post_call(x2, oa, ob.reshape(n, B_WIDTH), w_o[:A_WIDTH].astype(_BF16),
                            w_o[A_WIDTH:][_PERM_CH].astype(_BF16), row(norm_moe), r_cat)

    dest, meta = _route_call(rt)
    d1, d2 = dest[:, 0], dest[:, 1]
    n_tiles = 2 * n // TR_GMM + N_EXPERTS
    ends = jnp.cumsum(meta[0, :N_EXPERTS].astype(jnp.int32))
    tile_expert = jnp.minimum(jnp.sum(ends[None, :] <= jnp.arange(n_tiles)[:, None], axis=1),
                              N_EXPERTS - 1).astype(jnp.int32)
    xs = _dispatch_call(xp, d1, d2, n_tiles * TR_GMM)
    wgu = jnp.concatenate([moe_w_gate, moe_w_up], axis=-1).astype(_BF16)
    ys = _gmm_call(tile_expert, ends[-1:], xs, wgu, moe_w_down.astype(_BF16))
    y1, y2 = _combine_call(ys, d1, d2)
    out = _final_call(h1, y1, y2, rt, p_i.reshape(n, D_PLE), row(norm_ple),
                      w_ple_gate.astype(_BF16), w_ple_proj.astype(_BF16), row(norm_final))
    return out.reshape(bsz, seq, D_MODEL)


def kernel(x, p, norm_mix, w_in, gmlp_v_norm, gmlp_w_s, gmlp_b_s, cmp_pe_k, cmp_w1_k, cmp_w2_k,
           cmp_pe_v, cmp_w1_v, cmp_w2_v, out_norm_a, out_norm_b, w_o, norm_moe, router_group,
           router_expert, moe_w_gate, moe_w_up, moe_w_down, norm_ple, w_ple_proj, w_ple_gate, norm_final):
    assert p.shape[0] == 1, "single-layer trunk"
    assert x.shape[1] % SEL_CK == 0 and x.shape[1] >= WIN + Q_BLOCK
    assert (x.shape[0] * x.shape[1]) % (SC_CORES * SC_SUBCORES * SC_CHUNK) == 0
    return _layer(x, p[0], norm_mix[0], w_in[0], gmlp_v_norm[0], gmlp_w_s[0], gmlp_b_s[0],
                  cmp_pe_k[0], cmp_w1_k[0], cmp_w2_k[0], cmp_pe_v[0], cmp_w1_v[0], cmp_w2_v[0],
                  out_norm_a[0], out_norm_b[0], w_o[0], norm_moe[0], router_group[0], router_expert[0],
                  moe_w_gate[0], moe_w_up[0], moe_w_down[0], norm_ple[0], w_ple_proj[0], w_ple_gate[0],
                  norm_final)
```

```python
import functools

import numpy as np
import jax
import jax.numpy as jnp
from jax import lax
from jax.experimental import pallas as pl
from jax.experimental.pallas import tpu as pltpu
from jax.experimental.pallas import tpu_sc as plsc

D_MODEL = 1024
HEAD_DIM = 64
A_HEADS = 8
A_WIDTH = A_HEADS * HEAD_DIM
B_HEADS = 8
B_WIDTH = B_HEADS * HEAD_DIM
B_KV = 2
B_HPG = B_HEADS // B_KV
KV_W = B_KV * HEAD_DIM
N_GATES = B_HEADS * 3
CHUNK = 128
L_CMP = 32
STRIDE_CMP = 16
CMP_HIDDEN = 256
L_SEL = 64
N_SEL = 16
WIN = 512
Q_BLOCK = 256
ROPE_THETA = 10000.0
N_GROUPS = 4
EXPERTS_PER_GROUP = 4
N_EXPERTS = N_GROUPS * EXPERTS_PER_GROUP
D_FF_EXPERT = 512
D_PLE = 256
EPS = 1e-6
LOG2E = 1.4426950408889634
NEG = -1e30
FORCE = 1e6

OFF_Q = 2 * A_WIDTH
OFF_KV = OFF_Q + B_WIDTH
OFF_GATE = OFF_KV + 6 * KV_W
D_IN = OFF_GATE + N_GATES

LANES = 128
GATE_PAD = LANES
ROUTER_OFF = N_GROUPS
W_ALL = OFF_GATE + GATE_PAD

TM_PROJ = 512
TM_POST = 512
TR_GMM = 512
SUB_MOE = 256
RT_W = 2 * N_EXPERTS
SC_CORES = 2
SC_SUBCORES = 16
SC_CHUNK = 128
SEL_CK = 512
VMEM_LIMIT = 56 * 1024 * 1024

_PERM_HEADS = [0, 4, 1, 5, 2, 6, 3, 7]
_PERM_CH = np.concatenate([np.arange(HEAD_DIM) + HEAD_DIM * h for h in _PERM_HEADS])

_F32 = jnp.float32
_BF16 = jnp.bfloat16


def _dot(a, b):
    return jnp.dot(a, b, preferred_element_type=_F32)


def _dot_nt(a, b):
    return lax.dot_general(a, b, (((1,), (1,)), ((), ())), preferred_element_type=_F32)


def _rms(x, g):
    return x * lax.rsqrt(jnp.mean(x * x, axis=-1, keepdims=True) + EPS) * g


def _gelu(x):
    return 0.5 * x * (1.0 + jnp.tanh(0.7978845608028654 * (x + 0.044715 * (x * x * x))))


def _rope_tile(x, cos, sin_signed):
    lane = lax.broadcasted_iota(jnp.int32, x.shape, 1)
    first_half = (lane % HEAD_DIM) < (HEAD_DIM // 2)
    rot = jnp.where(first_half, pltpu.roll(x, LANES - HEAD_DIM // 2, 1), pltpu.roll(x, HEAD_DIM // 2, 1))
    return x * cos + rot * sin_signed


def _proj_kernel(x_ref, gmix_ref, w_ref, cos_ref, sin_ref, gv_ref, ws_ref, bs_ref, goa_ref,
                 oa_ref, q_ref, kc_ref, vc_ref, ks0_ref, ks1_ref, vs0_ref, vs1_ref, kw_ref, vw_ref, gate_ref,
                 *, seq):
    tm = x_ref.shape[0]
    hn = _rms(x_ref[...], gmix_ref[...]).astype(_BF16)
    cos = cos_ref[...]
    sin = sin_ref[...]

    zq = _dot(hn, w_ref[:, OFF_Q:OFF_KV])
    scale = HEAD_DIM ** -0.5 * LOG2E
    for j in range(B_WIDTH // LANES):
        blk = _rope_tile(zq[:, j * LANES:(j + 1) * LANES], cos, sin) * scale
        q_ref[:, j * LANES:(j + 1) * LANES] = blk.astype(q_ref.dtype)

    zkv = _dot(hn, w_ref[:, OFF_KV:OFF_GATE])
    kv = []
    for j in range(6):
        blk = zkv[:, j * KV_W:(j + 1) * KV_W]
        kv.append(_rope_tile(blk, cos, sin) if j % 2 == 0 else blk)
    k_cmp, v_cmp, k_slc, v_slc, k_win, v_win = kv
    kc_ref[...] = k_cmp.astype(kc_ref.dtype)
    vc_ref[...] = v_cmp.astype(vc_ref.dtype)
    kw_ref[...] = k_win.astype(kw_ref.dtype)
    vw_ref[...] = v_win.astype(vw_ref.dtype)

    row = lax.broadcasted_iota(jnp.int32, (tm, LANES), 0)
    lane = lax.broadcasted_iota(jnp.int32, (tm, LANES), 1)
    key_block = ((pl.program_id(0) % (seq // tm)) * tm + row) // L_SEL
    onehot = jnp.where(lane == key_block, 1.0, 0.0).astype(ks0_ref.dtype)
    for g, (ks_ref, vs_ref) in enumerate(((ks0_ref, vs0_ref), (ks1_ref, vs1_ref))):
        own = (lane // HEAD_DIM) == g
        ks_ref[:, :LANES] = jnp.where(own, k_slc, 0.0).astype(ks_ref.dtype)
        ks_ref[:, LANES:] = onehot
        vs_ref[...] = jnp.where(own, v_slc, 1.0).astype(vs_ref.dtype)

    zg = _dot(hn, w_ref[:, OFF_GATE:W_ALL])
    gate_ref[...] = jax.nn.sigmoid(zg)

    zu = _gelu(_dot(hn, w_ref[:, 0:A_WIDTH]))
    zv = _gelu(_dot(hn, w_ref[:, A_WIDTH:2 * A_WIDTH]))
    vn = _rms(zv, gv_ref[...]).astype(_BF16)

    t_io = lax.broadcasted_iota(jnp.int32, (CHUNK, 2 * CHUNK), 0)
    s_io = lax.broadcasted_iota(jnp.int32, (CHUNK, 2 * CHUNK), 1) % CHUNK
    causal = s_io <= t_io
    lane = lax.broadcasted_iota(jnp.int32, (CHUNK, LANES), 1)
    lo = lane < HEAD_DIM
    bs = bs_ref[...]
    chunks = []
    for c in range(tm // CHUNK):
        tiles = []
        for pr in range(A_HEADS // 2):
            wcat = jnp.where(causal, ws_ref[pr], 0.0).astype(_BF16)
            vblk = vn[c * CHUNK:(c + 1) * CHUNK, pr * LANES:(pr + 1) * LANES]
            zero = jnp.zeros_like(vblk)
            rhs = jnp.concatenate([jnp.where(lo, vblk, zero), jnp.where(lo, zero, vblk)], axis=0)
            tiles.append(_dot(wcat, rhs))
        chunks.append(jnp.concatenate(tiles, axis=1) + bs)
    mixed = jnp.concatenate(chunks, axis=0)
    oa = zu * mixed
    oa_ref[...] = _rms(oa, goa_ref[...]).astype(oa_ref.dtype)


def _proj_call(x2, gmix, w_all, cos_t, sin_t, gv, ws_pairs, bs_exp, goa, seq):
    n = x2.shape[0]
    tm = TM_PROJ
    n_t = seq // tm
    row = lambda i: (i, 0)
    const2 = lambda i: (0, 0)
    pos = lambda i: (i % n_t, 0)
    out_shapes = [jax.ShapeDtypeStruct((n, A_WIDTH), _BF16), jax.ShapeDtypeStruct((n, B_WIDTH), _BF16)]
    kv_widths = [KV_W, KV_W, KV_W + LANES, KV_W + LANES, KV_W, KV_W, KV_W, KV_W]
    out_shapes += [jax.ShapeDtypeStruct((n, w), _BF16) for w in kv_widths]
    out_shapes += [jax.ShapeDtypeStruct((n, GATE_PAD), _F32)]
    out_specs = [pl.BlockSpec((tm, A_WIDTH), row), pl.BlockSpec((tm, B_WIDTH), row)]
    out_specs += [pl.BlockSpec((tm, w), row) for w in kv_widths]
    out_specs += [pl.BlockSpec((tm, GATE_PAD), row)]
    return pl.pallas_call(
        functools.partial(_proj_kernel, seq=seq),
        grid=(n // tm,),
        in_specs=[
            pl.BlockSpec((tm, D_MODEL), row),
            pl.BlockSpec((1, D_MODEL), const2),
            pl.BlockSpec((D_MODEL, W_ALL), const2),
            pl.BlockSpec((tm, LANES), pos),
            pl.BlockSpec((tm, LANES), pos),
            pl.BlockSpec((1, A_WIDTH), const2),
            pl.BlockSpec((A_HEADS // 2, CHUNK, 2 * CHUNK), lambda i: (0, 0, 0)),
            pl.BlockSpec((CHUNK, A_WIDTH), const2),
            pl.BlockSpec((1, A_WIDTH), const2),
        ],
        out_specs=out_specs,
        out_shape=out_shapes,
        compiler_params=pltpu.CompilerParams(dimension_semantics=("arbitrary",), vmem_limit_bytes=VMEM_LIMIT),
        name="proj",
    )(x2, gmix, w_all, cos_t, sin_t, gv, ws_pairs, bs_exp, goa)


def _compress_kernel(rk_ref, rv_ref, pek_ref, pev_ref, w1k_ref, w1v_ref,
                     tk_ref, bk_ref, w2k_ref, tv_ref, bv_ref, w2v_ref, kc_ref, vc_ref):
    def one(r_ref, pe_ref, w1_ref, top_ref, bot_ref, w2_ref, o_ref):
        r = r_ref[0]
        nr = r.shape[0]
        a = _dot(r, top_ref[...])
        b = _dot(r, bot_ref[...])
        pe_h = _dot(pe_ref[...], w1_ref[...])
        pe2 = jnp.concatenate([pe_h[0:1], pe_h[0:1]], axis=1)
        hid = a + pltpu.roll(b, nr - 1, 0) + pe2
        o_ref[0] = _dot(_gelu(hid).astype(_BF16), w2_ref[...]).astype(o_ref.dtype)

    one(rk_ref, pek_ref, w1k_ref, tk_ref, bk_ref, w2k_ref, kc_ref)
    one(rv_ref, pev_ref, w1v_ref, tv_ref, bv_ref, w2v_ref, vc_ref)


def _compress_weights(w1, w2, pe):
    half = L_CMP // 2
    w1r = w1.reshape(L_CMP, HEAD_DIM, CMP_HIDDEN)
    eye = jnp.eye(B_KV, dtype=w1.dtype)
    place = lambda part: jnp.einsum('ldj,gh->lgdhj', part, eye).reshape(half * KV_W, B_KV * CMP_HIDDEN)
    top = place(w1r[:half]).astype(_BF16)
    bot = place(w1r[half:]).astype(_BF16)
    w2bd = jnp.einsum('jd,gh->gjhd', w2, eye).reshape(B_KV * CMP_HIDDEN, KV_W).astype(_BF16)
    pe8 = jnp.broadcast_to(pe.reshape(1, L_CMP * HEAD_DIM), (8, L_CMP * HEAD_DIM)).astype(_BF16)
    return pe8, w1.astype(_BF16), top, bot, w2bd


def _compress_call(k_cmp, v_cmp, wk, wv, bsz, seq):
    nr = seq // STRIDE_CMP
    rk = k_cmp.reshape(bsz, nr, STRIDE_CMP * KV_W)
    rv = v_cmp.reshape(bsz, nr, STRIDE_CMP * KV_W)
    pek, w1k, tk, bk, w2k = wk
    pev, w1v, tv, bv, w2v = wv
    rspec = pl.BlockSpec((1, nr, STRIDE_CMP * KV_W), lambda b: (b, 0, 0))
    full = lambda a: pl.BlockSpec(a.shape, lambda b: (0,) * a.ndim)
    ospec = pl.BlockSpec((1, nr, KV_W), lambda b: (b, 0, 0))
    return pl.pallas_call(
        _compress_kernel,
        grid=(bsz,),
        in_specs=[rspec, rspec, full(pek), full(pev), full(w1k), full(w1v),
                  full(tk), full(bk), full(w2k), full(tv), full(bv), full(w2v)],
        out_specs=[ospec, ospec],
        out_shape=[jax.ShapeDtypeStruct((bsz, nr, KV_W), _BF16)] * 2,
        compiler_params=pltpu.CompilerParams(dimension_semantics=("arbitrary",), vmem_limit_bytes=VMEM_LIMIT),
        name="compress",
    )(rk, rv, pek, pev, w1k, w1v, tk, bk, w2k, tv, bv, w2v)


def _topk_rows_mask(sc_t, k):
    n_rows = sc_t.shape[0]
    row = lax.broadcasted_iota(jnp.int32, sc_t.shape, 0)
    taken = -3e38
    for _ in range(k):
        m = jnp.max(sc_t, axis=0, keepdims=True)
        idx = jnp.min(jnp.where(sc_t == m, row, n_rows), axis=0, keepdims=True)
        sc_t = jnp.where(row == idx, taken, sc_t)
    return jnp.where(sc_t == taken, 1.0, 0.0)


def _attn_kernel(q_ref, gate_ref, gexp_ref, kc_ref, vc_ref, ks0_ref, ks1_ref, vs0_ref, vs1_ref,
                 kw_ref, vw_ref, gob_ref, o_ref, m_scr, acc_scr, *, seq):
    qb = pl.program_id(1)
    t0 = qb * Q_BLOCK
    n_cmp = kc_ref.shape[1]
    n_sb = seq // L_SEL
    k_top = min(N_SEL, n_sb)
    rows = B_HPG * Q_BLOCK
    ks_refs, vs_refs = (ks0_ref, ks1_ref), (vs0_ref, vs1_ref)

    lane_q = lax.broadcasted_iota(jnp.int32, (Q_BLOCK, LANES), 1)
    lo = lane_q < HEAD_DIM
    t_col = t0 + lax.broadcasted_iota(jnp.int32, (Q_BLOCK, 1), 0)

    def per_head(x):
        return x.reshape(B_HPG, Q_BLOCK, x.shape[-1])

    def add_bias(s, bias):
        return (per_head(s) + bias[None]).reshape(rows, s.shape[-1])

    qs = []
    for g in range(B_KV):
        own = (lane_q // HEAD_DIM) == g
        qs.append(jnp.concatenate(
            [jnp.where(own, q_ref[0, :, j * LANES:(j + 1) * LANES], jnp.zeros((), q_ref.dtype))
             for j in range(B_HPG)], axis=0))

    c_lane = lax.broadcasted_iota(jnp.int32, (Q_BLOCK, n_cmp), 1)
    bias_c = jnp.where(c_lane * STRIDE_CMP + (L_CMP - 1) <= t_col, 0.0, NEG)
    has_c = (t_col >= L_CMP - 1).astype(_F32)
    c_row = lax.broadcasted_iota(jnp.int32, (n_cmp, n_sb), 0) * STRIDE_CMP
    s_col = lax.broadcasted_iota(jnp.int32, (n_cmp, n_sb), 1) * L_SEL
    overlap = jnp.where((c_row < s_col + L_SEL) & (c_row + L_CMP > s_col), 1.0, 0.0).astype(_BF16)
    blk = lax.broadcasted_iota(jnp.int32, (Q_BLOCK, n_sb), 1)
    cur = t_col // L_SEL
    forced = (blk == 0) | (blk == cur) | (blk == cur - 1)
    valid = blk * L_SEL <= t_col

    o_c, sel_bias = [], []
    for g in range(B_KV):
        s_c = add_bias(_dot_nt(qs[g], kc_ref[0]), bias_c)
        e_c = jnp.exp2(s_c - jnp.max(s_c, axis=-1, keepdims=True))
        inv = per_head(1.0 / jnp.maximum(jnp.sum(e_c, axis=-1, keepdims=True), 1e-30)) * has_c[None]
        p_c = (per_head(e_c) * inv).reshape(rows, n_cmp)
        o_c.append(_dot(p_c.astype(_BF16), vc_ref[0]))

        p_sum = jnp.sum(per_head(p_c), axis=0)
        p_hi = p_sum.astype(_BF16)
        p_lo = (p_sum - p_hi.astype(_F32)).astype(_BF16)
        imp = _dot(p_hi, overlap) + _dot(p_lo, overlap)
        score = jnp.where(valid & jnp.logical_not(forced), imp, -FORCE)
        chosen = forced | (_topk_rows_mask(score.T, k_top - 3).T > 0.5)
        sb = jnp.where(chosen & valid, 0.0, NEG)
        if n_sb < LANES:
            sb = jnp.concatenate([sb, jnp.full((Q_BLOCK, LANES - n_sb), NEG, _F32)], axis=1)
        sel_bias.append(sb)

    w_start = pl.multiple_of(jnp.maximum(t0 - WIN, 0), Q_BLOCK)
    w_len = WIN + Q_BLOCK
    kpos_w = w_start + lax.broadcasted_iota(jnp.int32, (Q_BLOCK, w_len), 1)
    diff_w = t_col - kpos_w
    bias_w = jnp.where((diff_w >= 0) & (diff_w < WIN), 0.0, NEG)
    kw = kw_ref[0, pl.ds(w_start, w_len), :]
    vw = vw_ref[0, pl.ds(w_start, w_len), :]
    lane_w = lax.broadcasted_iota(jnp.int32, (w_len, LANES), 1)
    acc_w = []
    for g in range(B_KV):
        s_w = add_bias(_dot_nt(qs[g], kw), bias_w)
        e_w = jnp.exp2(s_w - jnp.max(s_w, axis=-1, keepdims=True)).astype(_BF16)
        vw_aug = jnp.where((lane_w // HEAD_DIM) == g, vw, jnp.ones((), vw.dtype))
        acc_w.append(_dot(e_w, vw_aug))

    n_ck = (t0 + Q_BLOCK + SEL_CK - 1) // SEL_CK
    key_lane = lax.broadcasted_iota(jnp.int32, (Q_BLOCK, SEL_CK), 1)
    bias_diag = jnp.where((n_ck - 1) * SEL_CK + key_lane <= t_col, 0.0, NEG)
    q_wide = [jnp.concatenate([qs[g], jnp.concatenate([sel_bias[g].astype(_BF16)] * B_HPG, axis=0)], axis=1)
              for g in range(B_KV)]

    m_scr[...] = jnp.full(m_scr.shape, NEG, _F32)
    acc_scr[...] = jnp.zeros(acc_scr.shape, _F32)

    def sel_chunk(ci, diag):
        k0 = pl.multiple_of(ci * SEL_CK, SEL_CK)
        for g in range(B_KV):
            m = m_scr[g]
            s = _dot_nt(q_wide[g], ks_refs[g][0, pl.ds(k0, SEL_CK), :])
            if diag:
                s = add_bias(s, bias_diag)
            m_new = jnp.maximum(m, jnp.max(s, axis=-1, keepdims=True))
            p = jnp.exp2(s - jnp.concatenate([m_new] * (SEL_CK // LANES), axis=1)).astype(_BF16)
            acc_scr[g] = jnp.exp2(m - m_new) * acc_scr[g] + _dot(p, vs_refs[g][0, pl.ds(k0, SEL_CK), :])
            m_scr[g] = m_new

    @pl.loop(0, n_ck - 1)
    def _(ci):
        sel_chunk(ci, False)

    sel_chunk(n_ck - 1, True)
    acc_s = [acc_scr[g] for g in range(B_KV)]

    def numer(acc):
        return jnp.concatenate([jnp.where(lo, acc[0][j * Q_BLOCK:(j + 1) * Q_BLOCK],
                                          acc[1][j * Q_BLOCK:(j + 1) * Q_BLOCK]) for j in range(B_HPG)], axis=1)

    def denom(acc):
        return jnp.concatenate([pltpu.roll(jnp.where(lo, acc[1][j * Q_BLOCK:(j + 1) * Q_BLOCK],
                                                     acc[0][j * Q_BLOCK:(j + 1) * Q_BLOCK]), HEAD_DIM, 1)
                                for j in range(B_HPG)], axis=1)

    gates = gate_ref[0]
    g_hi = gates.astype(_BF16)
    g_lo = (gates - g_hi.astype(_F32)).astype(_BF16)
    gate_of = lambda r: _dot(g_hi, gexp_ref[r]) + _dot(g_lo, gexp_ref[r])
    ob = (gate_of(0) * numer(o_c)
          + gate_of(1) * numer(acc_s) * (1.0 / jnp.maximum(denom(acc_s), 1e-30))
          + gate_of(2) * numer(acc_w) * (1.0 / jnp.maximum(denom(acc_w), 1e-30)))
    o_ref[0] = _rms(ob, gob_ref[...]).astype(o_ref.dtype)


def _gate_expand():
    x = np.zeros((3, GATE_PAD, B_WIDTH), np.float32)
    for slot, h in enumerate(_PERM_HEADS):
        for r in range(3):
            x[r, 3 * h + r, slot * HEAD_DIM:(slot + 1) * HEAD_DIM] = 1.0
    return jnp.asarray(x, _BF16)


def _attn_call(q, gates, kc, vc, ks0, ks1, vs0, vs1, kw, vw, gob, bsz, seq):
    assert seq // L_SEL <= LANES
    n_cmp = kc.shape[1]
    qspec = lambda w: pl.BlockSpec((1, Q_BLOCK, w), lambda b, i: (b, i, 0))
    full = lambda r, w=KV_W: pl.BlockSpec((1, r, w), lambda b, i: (b, 0, 0))
    return pl.pallas_call(
        functools.partial(_attn_kernel, seq=seq),
        grid=(bsz, seq // Q_BLOCK),
        in_specs=[qspec(B_WIDTH), qspec(GATE_PAD),
                  pl.BlockSpec((3, GATE_PAD, B_WIDTH), lambda b, i: (0, 0, 0)),
                  full(n_cmp), full(n_cmp), full(seq, KV_W + LANES), full(seq, KV_W + LANES),
                  full(seq), full(seq), full(seq), full(seq),
                  pl.BlockSpec((1, B_WIDTH), lambda b, i: (0, 0))],
        out_specs=qspec(B_WIDTH),
        out_shape=jax.ShapeDtypeStruct((bsz, seq, B_WIDTH), _BF16),
        scratch_shapes=[pltpu.VMEM((B_KV, B_HPG * Q_BLOCK, LANES), _F32),
                        pltpu.VMEM((B_KV, B_HPG * Q_BLOCK, LANES), _F32)],
        compiler_params=pltpu.CompilerParams(dimension_semantics=("arbitrary", "arbitrary"),
                                             vmem_limit_bytes=VMEM_LIMIT),
        name="attn",
    )(q, gates, _gate_expand(), kc, vc, ks0, ks1, vs0, vs1, kw, vw, gob)


def _pack_halves(x):
    w = x.shape[1] // 2
    bits = lambda v: lax.bitcast_convert_type(v.astype(_BF16).astype(_F32), jnp.uint32)
    return lax.bitcast_convert_type(bits(x[:, :w]) | (bits(x[:, w:]) >> 16), jnp.int32)


def _unpack_halves(p):
    u = lax.bitcast_convert_type(p, jnp.uint32)
    return (lax.bitcast_convert_type(u & jnp.uint32(0xFFFF0000), _F32),
            lax.bitcast_convert_type(u << 16, _F32))


def _post_kernel(x_ref, oa_ref, ob_ref, woa_ref, wob_ref, gmoe_ref, r_ref, h1_ref, hn_ref, rt_ref):
    h1 = x_ref[...] + _dot(oa_ref[...], woa_ref[...]) + _dot(ob_ref[...], wob_ref[...])
    h1_ref[...] = h1
    hn = _rms(h1, gmoe_ref[...])
    hn_ref[...] = _pack_halves(hn)

    hn_hi = hn.astype(_BF16)
    hn_lo = (hn - hn_hi.astype(_F32)).astype(_BF16)
    hi_both = _dot(hn_hi, r_ref[...])
    logits = hi_both[:, :GATE_PAD] + (_dot(hn_lo, r_ref[:, :GATE_PAD]) + hi_both[:, GATE_PAD:])
    lane = lax.broadcasted_iota(jnp.int32, logits.shape, 1)
    first_idx = lambda hit: jnp.min(jnp.where(hit, lane, LANES), axis=-1, keepdims=True)

    is_g = lane < N_GROUPS
    lg = jnp.where(is_g, logits, NEG)
    mg = jnp.max(lg, axis=-1, keepdims=True)
    sg = jnp.sum(jnp.where(is_g, jnp.exp(lg - mg), 0.0), axis=-1, keepdims=True)
    pg_top = 1.0 / sg
    g_sel = first_idx(is_g & (lg == mg))

    e_lo = ROUTER_OFF + g_sel * EXPERTS_PER_GROUP
    is_e = (lane >= e_lo) & (lane < e_lo + EXPERTS_PER_GROUP)
    le = jnp.where(is_e, logits, NEG)
    m1 = jnp.max(le, axis=-1, keepdims=True)
    se = jnp.sum(jnp.where(is_e, jnp.exp(le - m1), 0.0), axis=-1, keepdims=True)
    i1 = first_idx(is_e & (le == m1))
    le2 = jnp.where(lane == i1, NEG, le)
    m2 = jnp.max(le2, axis=-1, keepdims=True)
    i2 = first_idx(is_e & (lane != i1) & (le2 == m2))
    pe1 = 1.0 / se
    pe2 = jnp.exp(m2 - m1) / se
    denom = pe1 + pe2
    rt_ref[...] = (jnp.where(lane == i1 - ROUTER_OFF, 1.0, 0.0)
                   + jnp.where(lane == i2 - ROUTER_OFF + N_EXPERTS, 1.0, 0.0)
                   + jnp.where(lane == RT_W, pg_top * pe1 / denom, 0.0)
                   + jnp.where(lane == RT_W + 1, pg_top * pe2 / denom, 0.0))


def _post_call(x2, oa, ob, woa, wob, gmoe, r_cat):
    n = x2.shape[0]
    tm = TM_POST
    row = lambda i: (i, 0)
    const2 = lambda i: (0, 0)
    return pl.pallas_call(
        _post_kernel,
        grid=(n // tm,),
        in_specs=[pl.BlockSpec((tm, D_MODEL), row), pl.BlockSpec((tm, A_WIDTH), row),
                  pl.BlockSpec((tm, B_WIDTH), row), pl.BlockSpec((A_WIDTH, D_MODEL), const2),
                  pl.BlockSpec((B_WIDTH, D_MODEL), const2), pl.BlockSpec((1, D_MODEL), const2),
                  pl.BlockSpec((D_MODEL, 2 * GATE_PAD), const2)],
        out_specs=[pl.BlockSpec((tm, D_MODEL), row), pl.BlockSpec((tm, D_MODEL // 2), row),
                   pl.BlockSpec((tm, GATE_PAD), row)],
        out_shape=[jax.ShapeDtypeStruct((n, D_MODEL), _F32), jax.ShapeDtypeStruct((n, D_MODEL // 2), jnp.int32),
                   jax.ShapeDtypeStruct((n, GATE_PAD), _F32)],
        compiler_params=pltpu.CompilerParams(dimension_semantics=("arbitrary",), vmem_limit_bytes=VMEM_LIMIT),
        name="post",
    )(x2, oa, ob, woa, wob, gmoe, r_cat)


def _route_kernel(rt_ref, dest_ref, meta_ref, cnt_ref, off_ref, run_ref):
    phase = pl.program_id(0)
    i = pl.program_id(1)
    tm = rt_ref.shape[0]
    lane = lax.broadcasted_iota(jnp.int32, (1, LANES), 1)
    first = lane < N_EXPERTS
    onehot = jnp.where(lane < 2 * N_EXPERTS, rt_ref[...], 0.0)

    @pl.when((phase == 0) & (i == 0))
    def _():
        cnt_ref[...] = jnp.zeros_like(cnt_ref)

    @pl.when(phase == 0)
    def _():
        cnt_ref[...] += jnp.sum(onehot, axis=0, keepdims=True)

    @pl.when((phase == 1) & (i == 0))
    def _():
        cnt = cnt_ref[...]
        c1 = jnp.where(first, cnt, 0.0)
        tot = c1 + jnp.where(first, pltpu.roll(cnt, LANES - N_EXPERTS, 1), 0.0)
        tiles = jnp.floor((tot + (TR_GMM - 1)) * (1.0 / TR_GMM))
        e_row = lax.broadcasted_iota(jnp.int32, (LANES, LANES), 0)
        e_col = lax.broadcasted_iota(jnp.int32, (LANES, LANES), 1)
        before = jnp.where(e_row < e_col, 1.0, 0.0).astype(_BF16)
        base = _dot(tiles.astype(_BF16), before) * TR_GMM
        off_ref[...] = jnp.where(first, base, 0.0) + pltpu.roll(jnp.where(first, base + c1, 0.0), N_EXPERTS, 1)
        run_ref[...] = jnp.zeros_like(run_ref)
        meta_ref[...] = tiles

    @pl.when(phase == 1)
    def _():
        r_io = lax.broadcasted_iota(jnp.int32, (tm, tm), 0)
        c_io = lax.broadcasted_iota(jnp.int32, (tm, tm), 1)
        earlier = jnp.where(c_io < r_io, 1.0, 0.0).astype(_BF16)
        rank = _dot(earlier, onehot.astype(_BF16)) + run_ref[0:1, :]
        slot = onehot * (rank + off_ref[0:1, :])
        d1 = jnp.sum(jnp.where(first, slot, 0.0), axis=-1, keepdims=True)
        d2 = jnp.sum(jnp.where(first, 0.0, slot), axis=-1, keepdims=True)
        dest_ref[...] = (jnp.where(lane == 0, d1, 0.0) + jnp.where(lane == 1, d2, 0.0)).astype(jnp.int32)
        run_ref[...] += jnp.sum(onehot, axis=0, keepdims=True)


def _route_call(rt):
    n = rt.shape[0]
    tm = TM_POST
    return pl.pallas_call(
        _route_kernel,
        grid=(2, n // tm),
        in_specs=[pl.BlockSpec((tm, GATE_PAD), lambda ph, i: (i, 0))],
        out_specs=[pl.BlockSpec((tm, LANES), lambda ph, i: (i * ph, 0)),
                   pl.BlockSpec((8, LANES), lambda ph, i: (0, 0))],
        out_shape=[jax.ShapeDtypeStruct((n, LANES), jnp.int32), jax.ShapeDtypeStruct((8, LANES), _F32)],
        scratch_shapes=[pltpu.VMEM((8, LANES), _F32)] * 3,
        compiler_params=pltpu.CompilerParams(dimension_semantics=("arbitrary", "arbitrary")),
        name="route",
    )(rt)


def _sc_mesh():
    return plsc.VectorSubcoreMesh(core_axis_name="c", subcore_axis_name="s")


def _sc_worker(n_rows):
    per = n_rows // (SC_CORES * SC_SUBCORES)
    return (lax.axis_index("s") * SC_CORES + lax.axis_index("c")) * per, per


def _dispatch_call(xp, d1, d2, n_slots):
    n, w = xp.shape
    assert n % (SC_CORES * SC_SUBCORES * SC_CHUNK) == 0

    @functools.partial(
        pl.kernel, mesh=_sc_mesh(), out_type=jax.ShapeDtypeStruct((n_slots, w), xp.dtype),
        scratch_types=[pltpu.VMEM((SC_CHUNK,), jnp.int32), pltpu.VMEM((SC_CHUNK,), jnp.int32),
                       pltpu.VMEM((SC_CHUNK, w), xp.dtype), pltpu.SemaphoreType.DMA],
        name="dispatch")
    def k(x_hbm, d1_hbm, d2_hbm, xs_hbm, i1_v, i2_v, rows_v, sem):
        row0, per = _sc_worker(n)

        @pl.loop(0, per // SC_CHUNK)
        def _(j):
            src = pl.ds(row0 + j * SC_CHUNK, SC_CHUNK)
            pltpu.sync_copy(d1_hbm.at[src], i1_v)
            pltpu.sync_copy(d2_hbm.at[src], i2_v)
            pltpu.sync_copy(x_hbm.at[src], rows_v)
            first = pltpu.async_copy(rows_v, xs_hbm.at[i1_v], sem)
            second = pltpu.async_copy(rows_v, xs_hbm.at[i2_v], sem)
            first.wait()
            second.wait()

    return k(xp, d1, d2)


def _combine_call(ys, d1, d2):
    n = d1.shape[0]
    w = ys.shape[1]
    assert n % (SC_CORES * SC_SUBCORES * SC_CHUNK) == 0
    out = jax.ShapeDtypeStruct((n, w), ys.dtype)

    @functools.partial(
        pl.kernel, mesh=_sc_mesh(), out_type=(out, out),
        scratch_types=[pltpu.VMEM((SC_CHUNK,), jnp.int32), pltpu.VMEM((SC_CHUNK, w), ys.dtype),
                       pltpu.SemaphoreType.DMA],
        name="combine")
    def k(ys_hbm, d1_hbm, d2_hbm, y1_hbm, y2_hbm, i_v, rows_v, sem):
        row0, per = _sc_worker(n)

        @pl.loop(0, per // SC_CHUNK)
        def _(j):
            dst = pl.ds(row0 + j * SC_CHUNK, SC_CHUNK)
            for d_hbm, y_hbm in ((d1_hbm, y1_hbm), (d2_hbm, y2_hbm)):
                pltpu.sync_copy(d_hbm.at[dst], i_v)
                pltpu.async_copy(ys_hbm.at[i_v], rows_v, sem).wait()
                pltpu.sync_copy(rows_v, y_hbm.at[dst])

    return k(ys, d1, d2)


def _gmm_kernel(te_ref, nu_ref, xs_ref, wg_ref, wu_ref, wd_ref, ys_ref, wg_s, wu_s, wd_s):
    half = D_MODEL // 2
    t = pl.program_id(0)

    @pl.when(t < nu_ref[0])
    def _():
        @pl.when((t == 0) | (te_ref[t] != te_ref[jnp.maximum(t - 1, 0)]))
        def _():
            wg_s[...] = wg_ref[0].astype(_BF16)
            wu_s[...] = wu_ref[0].astype(_BF16)
            wd_s[...] = wd_ref[0].astype(_BF16)

        for r in range(TR_GMM // SUB_MOE):
            rows = slice(r * SUB_MOE, (r + 1) * SUB_MOE)
            a, b = _unpack_halves(xs_ref[rows, :])
            a, b = a.astype(_BF16), b.astype(_BF16)
            gate = _dot(a, wg_s[:half, :]) + _dot(b, wg_s[half:, :])
            up = _dot(a, wu_s[:half, :]) + _dot(b, wu_s[half:, :])
            hid = gate * jax.nn.sigmoid(gate) * up
            ys_ref[rows, :] = _pack_halves(_dot(hid.astype(_BF16), wd_s[...]))


def _gmm_call(tile_expert, n_used, xs, w_gate, w_up, w_down):
    n_slots, w = xs.shape
    rows = lambda t, te, nu: (jnp.minimum(t, nu[0] - 1), 0)
    expert = lambda t, te, nu: (te[t], 0, 0)
    return pl.pallas_call(
        _gmm_kernel,
        grid_spec=pltpu.PrefetchScalarGridSpec(
            num_scalar_prefetch=2, grid=(n_slots // TR_GMM,),
            in_specs=[pl.BlockSpec((TR_GMM, w), rows),
                      pl.BlockSpec((1, D_MODEL, D_FF_EXPERT), expert),
                      pl.BlockSpec((1, D_MODEL, D_FF_EXPERT), expert),
                      pl.BlockSpec((1, D_FF_EXPERT, D_MODEL), expert)],
            out_specs=pl.BlockSpec((TR_GMM, w), rows),
            scratch_shapes=[pltpu.VMEM((D_MODEL, D_FF_EXPERT), _BF16), pltpu.VMEM((D_MODEL, D_FF_EXPERT), _BF16),
                            pltpu.VMEM((D_FF_EXPERT, D_MODEL), _BF16)]),
        out_shape=jax.ShapeDtypeStruct((n_slots, w), xs.dtype),
        compiler_params=pltpu.CompilerParams(dimension_semantics=("arbitrary",), vmem_limit_bytes=VMEM_LIMIT),
        name="gmm",
    )(tile_expert, n_used, xs, w_gate, w_up, w_down)


def _final_kernel(h1_ref, y1_ref, y2_ref, rt_ref, p_ref, gple_ref, wpg_ref, wpp_ref, gfin_ref, o_ref):
    for r in range(h1_ref.shape[0] // SUB_MOE):
        rows = slice(r * SUB_MOE, (r + 1) * SUB_MOE)
        rt = rt_ref[rows, :]
        w1, w2 = rt[:, RT_W:RT_W + 1], rt[:, RT_W + 1:RT_W + 2]
        a1, b1 = _unpack_halves(y1_ref[rows, :])
        a2, b2 = _unpack_halves(y2_ref[rows, :])
        h2 = h1_ref[rows, :] + jnp.concatenate([w1 * a1 + w2 * a2, w1 * b1 + w2 * b2], axis=1)
        gate = jax.nn.sigmoid(_dot(_rms(h2, gple_ref[...]).astype(_BF16), wpg_ref[...]))
        h3 = h2 + _dot(p_ref[rows, :].astype(_BF16), wpp_ref[...]) * gate
        o_ref[rows, :] = _rms(h3, gfin_ref[...])


def _final_call(h1, y1, y2, rt, p2, gple, wpg, wpp, gfin):
    n = h1.shape[0]
    tm = TM_POST
    row = lambda i: (i, 0)
    const2 = lambda i: (0, 0)
    return pl.pallas_call(
        _final_kernel,
        grid=(n // tm,),
        in_specs=[pl.BlockSpec((tm, D_MODEL), row), pl.BlockSpec((tm, D_MODEL // 2), row),
                  pl.BlockSpec((tm, D_MODEL // 2), row), pl.BlockSpec((tm, GATE_PAD), row),
                  pl.BlockSpec((tm, D_PLE), row), pl.BlockSpec((1, D_MODEL), const2),
                  pl.BlockSpec((D_MODEL, D_MODEL), const2), pl.BlockSpec((D_PLE, D_MODEL), const2),
                  pl.BlockSpec((1, D_MODEL), const2)],
        out_specs=pl.BlockSpec((tm, D_MODEL), row),
        out_shape=jax.ShapeDtypeStruct((n, D_MODEL), _F32),
        compiler_params=pltpu.CompilerParams(dimension_semantics=("arbitrary",), vmem_limit_bytes=VMEM_LIMIT),
        name="final",
    )(h1, y1, y2, rt, p2, gple, wpg, wpp, gfin)


def _rope_tables(seq):
    half = HEAD_DIM // 2
    inv = 1.0 / (ROPE_THETA ** (jnp.arange(half, dtype=_F32) / half))
    ang = jnp.arange(seq, dtype=_F32)[:, None] * inv[None, :]
    cos, sin = jnp.cos(ang), jnp.sin(ang)
    reps = LANES // HEAD_DIM
    cos_t = jnp.tile(jnp.concatenate([cos, cos], axis=1), (1, reps))
    sin_t = jnp.tile(jnp.concatenate([-sin, sin], axis=1), (1, reps))
    return cos_t, sin_t


def _layer(h, p_i, norm_mix, w_in, gmlp_v_norm, gmlp_w_s, gmlp_b_s,
           cmp_pe_k, cmp_w1_k, cmp_w2_k, cmp_pe_v, cmp_w1_v, cmp_w2_v,
           out_norm_a, out_norm_b, w_o, norm_moe, router_group, router_expert,
           moe_w_gate, moe_w_up, moe_w_down, norm_ple, w_ple_proj, w_ple_gate, norm_final):
    bsz, seq, _ = h.shape
    n = bsz * seq
    x2 = h.reshape(n, D_MODEL)
    row = lambda v: v.reshape(1, -1).astype(_F32)

    w_q = w_in[:, OFF_Q:OFF_KV][:, _PERM_CH]
    w_gate = jnp.pad(w_in[:, OFF_GATE:D_IN], ((0, 0), (0, GATE_PAD - N_GATES)))
    w_all = jnp.concatenate([w_in[:, :OFF_Q], w_q, w_in[:, OFF_KV:OFF_GATE], w_gate], axis=1).astype(_BF16)
    cos_t, sin_t = _rope_tables(seq)
    ws_pairs = gmlp_w_s.reshape(A_HEADS // 2, 2, CHUNK, CHUNK).transpose(0, 2, 1, 3).reshape(
        A_HEADS // 2, CHUNK, 2 * CHUNK)
    bs_exp = jnp.repeat(gmlp_b_s.T, HEAD_DIM, axis=1)

    oa, q, k_cmp, v_cmp, ks0, ks1, vs0, vs1, k_win, v_win, gates = _proj_call(
        x2, row(norm_mix), w_all, cos_t, sin_t, row(gmlp_v_norm), ws_pairs, bs_exp, row(out_norm_a), seq)

    kc, vc = _compress_call(k_cmp, v_cmp, _compress_weights(cmp_w1_k, cmp_w2_k, cmp_pe_k),
                            _compress_weights(cmp_w1_v, cmp_w2_v, cmp_pe_v), bsz, seq)

    b3 = lambda a: a.reshape(bsz, seq, a.shape[-1])
    ob = _attn_call(b3(q), b3(gates), kc, vc, b3(ks0), b3(ks1), b3(vs0), b3(vs1), b3(k_win), b3(v_win),
                    row(out_norm_b[_PERM_CH]), bsz, seq)

    r_cat = jnp.pad(jnp.concatenate([router_group, router_expert], axis=1),
                    ((0, 0), (0, GATE_PAD - N_GROUPS - N_EXPERTS)))
    r_hi = r_cat.astype(_BF16)
    r_cat = jnp.concatenate([r_hi, (r_cat - r_hi.astype(_F32)).astype(_BF16)], axis=1)
    h1, xp, rt = _post_call(x2, oa, ob.reshape(n, B_WIDTH), w_o[:A_WIDTH].astype(_BF16),
                            w_o[A_WIDTH:][_PERM_CH].astype(_BF16), row(norm_moe), r_cat)

    dest, meta = _route_call(rt)
    d1, d2 = dest[:, 0], dest[:, 1]
    n_tiles = 2 * n // TR_GMM + N_EXPERTS
    ends = jnp.cumsum(meta[0, :N_EXPERTS].astype(jnp.int32))
    tile_expert = jnp.minimum(jnp.sum(ends[None, :] <= jnp.arange(n_tiles)[:, None], axis=1),
                              N_EXPERTS - 1).astype(jnp.int32)
    xs = _dispatch_call(xp, d1, d2, n_tiles * TR_GMM)
    ys = _gmm_call(tile_expert, ends[-1:], xs, moe_w_gate, moe_w_up, moe_w_down)
    y1, y2 = _combine_call(ys, d1, d2)
    out = _final_call(h1, y1, y2, rt, p_i.reshape(n, D_PLE), row(norm_ple),
                      w_ple_gate.astype(_BF16), w_ple_proj.astype(_BF16), row(norm_final))
    return out.reshape(bsz, seq, D_MODEL)


def kernel(x, p, norm_mix, w_in, gmlp_v_norm, gmlp_w_s, gmlp_b_s, cmp_pe_k, cmp_w1_k, cmp_w2_k,
           cmp_pe_v, cmp_w1_v, cmp_w2_v, out_norm_a, out_norm_b, w_o, norm_moe, router_group,
           router_expert, moe_w_gate, moe_w_up, moe_w_down, norm_ple, w_ple_proj, w_ple_gate, norm_final):
    assert p.shape[0] == 1, "single-layer trunk"
    assert x.shape[1] % SEL_CK == 0 and x.shape[1] >= WIN + Q_BLOCK
    assert (x.shape[0] * x.shape[1]) % (SC_CORES * SC_SUBCORES * SC_CHUNK) == 0
    return _layer(x, p[0], norm_mix[0], w_in[0], gmlp_v_norm[0], gmlp_w_s[0], gmlp_b_s[0],
                  cmp_pe_k[0], cmp_w1_k[0], cmp_w2_k[0], cmp_pe_v[0], cmp_w1_v[0], cmp_w2_v[0],
                  out_norm_a[0], out_norm_b[0], w_o[0], norm_moe[0], router_group[0], router_expert[0],
                  moe_w_gate[0], moe_w_up[0], moe_w_down[0], norm_ple[0], w_ple_proj[0], w_ple_gate[0],
                  norm_final)
```

```python
import functools

import numpy as np
import jax
import jax.numpy as jnp
from jax import lax
from jax.experimental import pallas as pl
from jax.experimental.pallas import tpu as pltpu
from jax.experimental.pallas import tpu_sc as plsc

D_MODEL = 1024
HEAD_DIM = 64
A_HEADS = 8
A_WIDTH = A_HEADS * HEAD_DIM
B_HEADS = 8
B_WIDTH = B_HEADS * HEAD_DIM
B_KV = 2
B_HPG = B_HEADS // B_KV
KV_W = B_KV * HEAD_DIM
N_GATES = B_HEADS * 3
CHUNK = 128
L_CMP = 32
STRIDE_CMP = 16
CMP_HIDDEN = 256
L_SEL = 64
N_SEL = 16
WIN = 512
Q_BLOCK = 256
ROPE_THETA = 10000.0
N_GROUPS = 4
EXPERTS_PER_GROUP = 4
N_EXPERTS = N_GROUPS * EXPERTS_PER_GROUP
D_FF_EXPERT = 512
D_PLE = 256
EPS = 1e-6
LOG2E = 1.4426950408889634
NEG = -1e30
FORCE = 1e6

OFF_Q = 2 * A_WIDTH
OFF_KV = OFF_Q + B_WIDTH
OFF_GATE = OFF_KV + 6 * KV_W
D_IN = OFF_GATE + N_GATES

LANES = 128
GATE_PAD = LANES
ROUTER_OFF = N_GROUPS
W_ALL = OFF_GATE + GATE_PAD

TM_PROJ = 1024
TM_POST = 1024
TM_ROUTE = 512
TR_GMM = 512
SUB_MOE = 512
RT_W = 2 * N_EXPERTS
SC_CORES = 2
SC_SUBCORES = 16
SC_CHUNK = 128
SEL_CK = 512
VMEM_LIMIT = 56 * 1024 * 1024

_PERM_HEADS = [0, 4, 1, 5, 2, 6, 3, 7]
_PERM_CH = np.concatenate([np.arange(HEAD_DIM) + HEAD_DIM * h for h in _PERM_HEADS])

_F32 = jnp.float32
_BF16 = jnp.bfloat16


def _dot(a, b):
    return jnp.dot(a, b, preferred_element_type=_F32)


def _dot_nt(a, b):
    return lax.dot_general(a, b, (((1,), (1,)), ((), ())), preferred_element_type=_F32)


def _rms(x, g):
    return x * lax.rsqrt(jnp.mean(x * x, axis=-1, keepdims=True) + EPS) * g


def _gelu(x):
    return 0.5 * x * (1.0 + jnp.tanh(0.7978845608028654 * (x + 0.044715 * (x * x * x))))


def _rope_tile(x, cos, sin_signed):
    lane = lax.broadcasted_iota(jnp.int32, x.shape, 1)
    first_half = (lane % HEAD_DIM) < (HEAD_DIM // 2)
    rot = jnp.where(first_half, pltpu.roll(x, LANES - HEAD_DIM // 2, 1), pltpu.roll(x, HEAD_DIM // 2, 1))
    return x * cos + rot * sin_signed


def _proj_kernel(x_ref, gmix_ref, w_ref, cos_ref, sin_ref, gv_ref, ws_ref, bs_ref, goa_ref,
                 oa_ref, q_ref, kc_ref, vc_ref, ks0_ref, ks1_ref, vs0_ref, vs1_ref, kw_ref, vw_ref, gate_ref,
                 *, seq):
    tm = x_ref.shape[0]
    hn = _rms(x_ref[...], gmix_ref[...]).astype(_BF16)
    cos = cos_ref[...]
    sin = sin_ref[...]

    zq = _dot(hn, w_ref[:, OFF_Q:OFF_KV])
    scale = HEAD_DIM ** -0.5 * LOG2E
    for j in range(B_WIDTH // LANES):
        blk = _rope_tile(zq[:, j * LANES:(j + 1) * LANES], cos, sin) * scale
        q_ref[:, j * LANES:(j + 1) * LANES] = blk.astype(q_ref.dtype)

    zkv = _dot(hn, w_ref[:, OFF_KV:OFF_GATE])
    kv = []
    for j in range(6):
        blk = zkv[:, j * KV_W:(j + 1) * KV_W]
        kv.append(_rope_tile(blk, cos, sin) if j % 2 == 0 else blk)
    k_cmp, v_cmp, k_slc, v_slc, k_win, v_win = kv
    kc_ref[...] = k_cmp.astype(kc_ref.dtype)
    vc_ref[...] = v_cmp.astype(vc_ref.dtype)
    kw_ref[...] = k_win.astype(kw_ref.dtype)
    vw_ref[...] = v_win.astype(vw_ref.dtype)

    row = lax.broadcasted_iota(jnp.int32, (tm, LANES), 0)
    lane = lax.broadcasted_iota(jnp.int32, (tm, LANES), 1)
    key_block = ((pl.program_id(0) % (seq // tm)) * tm + row) // L_SEL
    onehot = jnp.where(lane == key_block, 1.0, 0.0).astype(ks0_ref.dtype)
    for g, (ks_ref, vs_ref) in enumerate(((ks0_ref, vs0_ref), (ks1_ref, vs1_ref))):
        own = (lane // HEAD_DIM) == g
        ks_ref[:, :LANES] = jnp.where(own, k_slc, 0.0).astype(ks_ref.dtype)
        ks_ref[:, LANES:] = onehot
        vs_ref[...] = jnp.where(own, v_slc, 1.0).astype(vs_ref.dtype)

    zg = _dot(hn, w_ref[:, OFF_GATE:W_ALL])
    gate_ref[...] = jax.nn.sigmoid(zg)

    zu = _gelu(_dot(hn, w_ref[:, 0:A_WIDTH]))
    zv = _gelu(_dot(hn, w_ref[:, A_WIDTH:2 * A_WIDTH]))
    vn = _rms(zv, gv_ref[...]).astype(_BF16)

    t_io = lax.broadcasted_iota(jnp.int32, (CHUNK, 2 * CHUNK), 0)
    s_io = lax.broadcasted_iota(jnp.int32, (CHUNK, 2 * CHUNK), 1) % CHUNK
    causal = s_io <= t_io
    lane = lax.broadcasted_iota(jnp.int32, (CHUNK, LANES), 1)
    lo = lane < HEAD_DIM
    bs = bs_ref[...]
    chunks = []
    for c in range(tm // CHUNK):
        tiles = []
        for pr in range(A_HEADS // 2):
            wcat = jnp.where(causal, ws_ref[pr], 0.0).astype(_BF16)
            vblk = vn[c * CHUNK:(c + 1) * CHUNK, pr * LANES:(pr + 1) * LANES]
            zero = jnp.zeros_like(vblk)
            rhs = jnp.concatenate([jnp.where(lo, vblk, zero), jnp.where(lo, zero, vblk)], axis=0)
            tiles.append(_dot(wcat, rhs))
        chunks.append(jnp.concatenate(tiles, axis=1) + bs)
    mixed = jnp.concatenate(chunks, axis=0)
    oa = zu * mixed
    oa_ref[...] = _rms(oa, goa_ref[...]).astype(oa_ref.dtype)


def _proj_call(x2, gmix, w_all, cos_t, sin_t, gv, ws_pairs, bs_exp, goa, seq):
    n = x2.shape[0]
    tm = TM_PROJ
    n_t = seq // tm
    row = lambda i: (i, 0)
    const2 = lambda i: (0, 0)
    pos = lambda i: (i % n_t, 0)
    out_shapes = [jax.ShapeDtypeStruct((n, A_WIDTH), _BF16), jax.ShapeDtypeStruct((n, B_WIDTH), _BF16)]
    kv_widths = [KV_W, KV_W, KV_W + LANES, KV_W + LANES, KV_W, KV_W, KV_W, KV_W]
    out_shapes += [jax.ShapeDtypeStruct((n, w), _BF16) for w in kv_widths]
    out_shapes += [jax.ShapeDtypeStruct((n, GATE_PAD), _F32)]
    out_specs = [pl.BlockSpec((tm, A_WIDTH), row), pl.BlockSpec((tm, B_WIDTH), row)]
    out_specs += [pl.BlockSpec((tm, w), row) for w in kv_widths]
    out_specs += [pl.BlockSpec((tm, GATE_PAD), row)]
    return pl.pallas_call(
        functools.partial(_proj_kernel, seq=seq),
        grid=(n // tm,),
        in_specs=[
            pl.BlockSpec((tm, D_MODEL), row),
            pl.BlockSpec((1, D_MODEL), const2),
            pl.BlockSpec((D_MODEL, W_ALL), const2),
            pl.BlockSpec((tm, LANES), pos),
            pl.BlockSpec((tm, LANES), pos),
            pl.BlockSpec((1, A_WIDTH), const2),
            pl.BlockSpec((A_HEADS // 2, CHUNK, 2 * CHUNK), lambda i: (0, 0, 0)),
            pl.BlockSpec((CHUNK, A_WIDTH), const2),
            pl.BlockSpec((1, A_WIDTH), const2),
        ],
        out_specs=out_specs,
        out_shape=out_shapes,
        compiler_params=pltpu.CompilerParams(dimension_semantics=("arbitrary",), vmem_limit_bytes=VMEM_LIMIT),
        name="proj",
    )(x2, gmix, w_all, cos_t, sin_t, gv, ws_pairs, bs_exp, goa)


def _compress_kernel(rk_ref, rv_ref, pek_ref, pev_ref, w1k_ref, w1v_ref,
                     tk_ref, bk_ref, w2k_ref, tv_ref, bv_ref, w2v_ref, kc_ref, vc_ref):
    def one(r_ref, pe_ref, w1_ref, top_ref, bot_ref, w2_ref, o_ref):
        r = r_ref[0]
        nr = r.shape[0]
        a = _dot(r, top_ref[...])
        b = _dot(r, bot_ref[...])
        pe_h = _dot(pe_ref[...], w1_ref[...])
        pe2 = jnp.concatenate([pe_h[0:1], pe_h[0:1]], axis=1)
        hid = a + pltpu.roll(b, nr - 1, 0) + pe2
        o_ref[0] = _dot(_gelu(hid).astype(_BF16), w2_ref[...]).astype(o_ref.dtype)

    one(rk_ref, pek_ref, w1k_ref, tk_ref, bk_ref, w2k_ref, kc_ref)
    one(rv_ref, pev_ref, w1v_ref, tv_ref, bv_ref, w2v_ref, vc_ref)


def _compress_weights(w1, w2, pe):
    half = L_CMP // 2
    w1r = w1.reshape(L_CMP, HEAD_DIM, CMP_HIDDEN)
    eye = jnp.eye(B_KV, dtype=w1.dtype)
    place = lambda part: jnp.einsum('ldj,gh->lgdhj', part, eye).reshape(half * KV_W, B_KV * CMP_HIDDEN)
    top = place(w1r[:half]).astype(_BF16)
    bot = place(w1r[half:]).astype(_BF16)
    w2bd = jnp.einsum('jd,gh->gjhd', w2, eye).reshape(B_KV * CMP_HIDDEN, KV_W).astype(_BF16)
    pe8 = jnp.broadcast_to(pe.reshape(1, L_CMP * HEAD_DIM), (8, L_CMP * HEAD_DIM)).astype(_BF16)
    return pe8, w1.astype(_BF16), top, bot, w2bd


def _compress_call(k_cmp, v_cmp, wk, wv, bsz, seq):
    nr = seq // STRIDE_CMP
    rk = k_cmp.reshape(bsz, nr, STRIDE_CMP * KV_W)
    rv = v_cmp.reshape(bsz, nr, STRIDE_CMP * KV_W)
    pek, w1k, tk, bk, w2k = wk
    pev, w1v, tv, bv, w2v = wv
    rspec = pl.BlockSpec((1, nr, STRIDE_CMP * KV_W), lambda b: (b, 0, 0))
    full = lambda a: pl.BlockSpec(a.shape, lambda b: (0,) * a.ndim)
    ospec = pl.BlockSpec((1, nr, KV_W), lambda b: (b, 0, 0))
    return pl.pallas_call(
        _compress_kernel,
        grid=(bsz,),
        in_specs=[rspec, rspec, full(pek), full(pev), full(w1k), full(w1v),
                  full(tk), full(bk), full(w2k), full(tv), full(bv), full(w2v)],
        out_specs=[ospec, ospec],
        out_shape=[jax.ShapeDtypeStruct((bsz, nr, KV_W), _BF16)] * 2,
        compiler_params=pltpu.CompilerParams(dimension_semantics=("arbitrary",), vmem_limit_bytes=VMEM_LIMIT),
        name="compress",
    )(rk, rv, pek, pev, w1k, w1v, tk, bk, w2k, tv, bv, w2v)


def _topk_rows_mask(sc_t, k):
    n_rows = sc_t.shape[0]
    row = lax.broadcasted_iota(jnp.int32, sc_t.shape, 0)
    taken = -3e38
    for _ in range(k):
        m = jnp.max(sc_t, axis=0, keepdims=True)
        idx = jnp.min(jnp.where(sc_t == m, row, n_rows), axis=0, keepdims=True)
        sc_t = jnp.where(row == idx, taken, sc_t)
    return jnp.where(sc_t == taken, 1.0, 0.0)


def _attn_kernel(q_ref, gate_ref, gexp_ref, kc_ref, vc_ref, ks0_ref, ks1_ref, vs0_ref, vs1_ref,
                 kw_ref, vw_ref, gob_ref, o_ref, m_scr, acc_scr, *, seq):
    qb = pl.program_id(1)
    t0 = qb * Q_BLOCK
    n_cmp = kc_ref.shape[1]
    n_sb = seq // L_SEL
    k_top = min(N_SEL, n_sb)
    rows = B_HPG * Q_BLOCK
    ks_refs, vs_refs = (ks0_ref, ks1_ref), (vs0_ref, vs1_ref)

    lane_q = lax.broadcasted_iota(jnp.int32, (Q_BLOCK, LANES), 1)
    lo = lane_q < HEAD_DIM
    t_col = t0 + lax.broadcasted_iota(jnp.int32, (Q_BLOCK, 1), 0)

    def per_head(x):
        return x.reshape(B_HPG, Q_BLOCK, x.shape[-1])

    def add_bias(s, bias):
        return (per_head(s) + bias[None]).reshape(rows, s.shape[-1])

    qs = []
    for g in range(B_KV):
        own = (lane_q // HEAD_DIM) == g
        qs.append(jnp.concatenate(
            [jnp.where(own, q_ref[0, :, j * LANES:(j + 1) * LANES], jnp.zeros((), q_ref.dtype))
             for j in range(B_HPG)], axis=0))

    c_lane = lax.broadcasted_iota(jnp.int32, (Q_BLOCK, n_cmp), 1)
    bias_c = jnp.where(c_lane * STRIDE_CMP + (L_CMP - 1) <= t_col, 0.0, NEG)
    has_c = (t_col >= L_CMP - 1).astype(_F32)
    c_row = lax.broadcasted_iota(jnp.int32, (n_cmp, n_sb), 0) * STRIDE_CMP
    s_col = lax.broadcasted_iota(jnp.int32, (n_cmp, n_sb), 1) * L_SEL
    overlap = jnp.where((c_row < s_col + L_SEL) & (c_row + L_CMP > s_col), 1.0, 0.0).astype(_BF16)
    blk = lax.broadcasted_iota(jnp.int32, (Q_BLOCK, n_sb), 1)
    cur = t_col // L_SEL
    forced = (blk == 0) | (blk == cur) | (blk == cur - 1)
    valid = blk * L_SEL <= t_col

    o_c, sel_bias = [], []
    for g in range(B_KV):
        s_c = add_bias(_dot_nt(qs[g], kc_ref[0]), bias_c)
        e_c = jnp.exp2(s_c - jnp.max(s_c, axis=-1, keepdims=True))
        inv = per_head(1.0 / jnp.maximum(jnp.sum(e_c, axis=-1, keepdims=True), 1e-30)) * has_c[None]
        p_c = (per_head(e_c) * inv).reshape(rows, n_cmp)
        o_c.append(_dot(p_c.astype(_BF16), vc_ref[0]))

        p_sum = jnp.sum(per_head(p_c), axis=0)
        p_hi = p_sum.astype(_BF16)
        p_lo = (p_sum - p_hi.astype(_F32)).astype(_BF16)
        imp = _dot(p_hi, overlap) + _dot(p_lo, overlap)
        score = jnp.where(valid & jnp.logical_not(forced), imp, -FORCE)
        chosen = forced | (_topk_rows_mask(score.T, k_top - 3).T > 0.5)
        sb = jnp.where(chosen & valid, 0.0, NEG)
        if n_sb < LANES:
            sb = jnp.concatenate([sb, jnp.full((Q_BLOCK, LANES - n_sb), NEG, _F32)], axis=1)
        sel_bias.append(sb)

    w_start = pl.multiple_of(jnp.maximum(t0 - WIN, 0), Q_BLOCK)
    w_len = WIN + Q_BLOCK
    kpos_w = w_start + lax.broadcasted_iota(jnp.int32, (Q_BLOCK, w_len), 1)
    diff_w = t_col - kpos_w
    bias_w = jnp.where((diff_w >= 0) & (diff_w < WIN), 0.0, NEG)
    kw = kw_ref[0, pl.ds(w_start, w_len), :]
    vw = vw_ref[0, pl.ds(w_start, w_len), :]
    lane_w = lax.broadcasted_iota(jnp.int32, (w_len, LANES), 1)
    acc_w = []
    for g in range(B_KV):
        s_w = add_bias(_dot_nt(qs[g], kw), bias_w)
        e_w = jnp.exp2(s_w - jnp.max(s_w, axis=-1, keepdims=True)).astype(_BF16)
        vw_aug = jnp.where((lane_w // HEAD_DIM) == g, vw, jnp.ones((), vw.dtype))
        acc_w.append(_dot(e_w, vw_aug))

    n_ck = (t0 + Q_BLOCK + SEL_CK - 1) // SEL_CK
    key_lane = lax.broadcasted_iota(jnp.int32, (Q_BLOCK, SEL_CK), 1)
    bias_diag = jnp.where((n_ck - 1) * SEL_CK + key_lane <= t_col, 0.0, NEG)
    q_wide = [jnp.concatenate([qs[g], jnp.concatenate([sel_bias[g].astype(_BF16)] * B_HPG, axis=0)], axis=1)
              for g in range(B_KV)]

    m_scr[...] = jnp.full(m_scr.shape, NEG, _F32)
    acc_scr[...] = jnp.zeros(acc_scr.shape, _F32)

    def sel_chunk(ci, diag):
        k0 = pl.multiple_of(ci * SEL_CK, SEL_CK)
        for g in range(B_KV):
            m = m_scr[g]
            s = _dot_nt(q_wide[g], ks_refs[g][0, pl.ds(k0, SEL_CK), :])
            if diag:
                s = add_bias(s, bias_diag)
            m_new = jnp.maximum(m, jnp.max(s, axis=-1, keepdims=True))
            p = jnp.exp2(s - jnp.concatenate([m_new] * (SEL_CK // LANES), axis=1)).astype(_BF16)
            acc_scr[g] = jnp.exp2(m - m_new) * acc_scr[g] + _dot(p, vs_refs[g][0, pl.ds(k0, SEL_CK), :])
            m_scr[g] = m_new

    @pl.loop(0, n_ck - 1)
    def _(ci):
        sel_chunk(ci, False)

    sel_chunk(n_ck - 1, True)
    acc_s = [acc_scr[g] for g in range(B_KV)]

    def numer(acc):
        return jnp.concatenate([jnp.where(lo, acc[0][j * Q_BLOCK:(j + 1) * Q_BLOCK],
                                          acc[1][j * Q_BLOCK:(j + 1) * Q_BLOCK]) for j in range(B_HPG)], axis=1)

    def denom(acc):
        return jnp.concatenate([pltpu.roll(jnp.where(lo, acc[1][j * Q_BLOCK:(j + 1) * Q_BLOCK],
                                                     acc[0][j * Q_BLOCK:(j + 1) * Q_BLOCK]), HEAD_DIM, 1)
                                for j in range(B_HPG)], axis=1)

    gates = gate_ref[0]
    g_hi = gates.astype(_BF16)
    g_lo = (gates - g_hi.astype(_F32)).astype(_BF16)
    gate_of = lambda r: _dot(g_hi, gexp_ref[r]) + _dot(g_lo, gexp_ref[r])
    ob = (gate_of(0) * numer(o_c)
          + gate_of(1) * numer(acc_s) * (1.0 / jnp.maximum(denom(acc_s), 1e-30))
          + gate_of(2) * numer(acc_w) * (1.0 / jnp.maximum(denom(acc_w), 1e-30)))
    o_ref[0] = _rms(ob, gob_ref[...]).astype(o_ref.dtype)


def _gate_expand():
    x = np.zeros((3, GATE_PAD, B_WIDTH), np.float32)
    for slot, h in enumerate(_PERM_HEADS):
        for r in range(3):
            x[r, 3 * h + r, slot * HEAD_DIM:(slot + 1) * HEAD_DIM] = 1.0
    return jnp.asarray(x, _BF16)


def _attn_call(q, gates, kc, vc, ks0, ks1, vs0, vs1, kw, vw, gob, bsz, seq):
    assert seq // L_SEL <= LANES
    n_cmp = kc.shape[1]
    qspec = lambda w: pl.BlockSpec((1, Q_BLOCK, w), lambda b, i: (b, i, 0))
    full = lambda r, w=KV_W: pl.BlockSpec((1, r, w), lambda b, i: (b, 0, 0))
    return pl.pallas_call(
        functools.partial(_attn_kernel, seq=seq),
        grid=(bsz, seq // Q_BLOCK),
        in_specs=[qspec(B_WIDTH), qspec(GATE_PAD),
                  pl.BlockSpec((3, GATE_PAD, B_WIDTH), lambda b, i: (0, 0, 0)),
                  full(n_cmp), full(n_cmp), full(seq, KV_W + LANES), full(seq, KV_W + LANES),
                  full(seq), full(seq), full(seq), full(seq),
                  pl.BlockSpec((1, B_WIDTH), lambda b, i: (0, 0))],
        out_specs=qspec(B_WIDTH),
        out_shape=jax.ShapeDtypeStruct((bsz, seq, B_WIDTH), _BF16),
        scratch_shapes=[pltpu.VMEM((B_KV, B_HPG * Q_BLOCK, LANES), _F32),
                        pltpu.VMEM((B_KV, B_HPG * Q_BLOCK, LANES), _F32)],
        compiler_params=pltpu.CompilerParams(dimension_semantics=("arbitrary", "arbitrary"),
                                             vmem_limit_bytes=VMEM_LIMIT),
        name="attn",
    )(q, gates, _gate_expand(), kc, vc, ks0, ks1, vs0, vs1, kw, vw, gob)


def _pack_halves(x):
    w = x.shape[1] // 2
    bits = lambda v: lax.bitcast_convert_type(v.astype(_BF16).astype(_F32), jnp.uint32)
    return lax.bitcast_convert_type(bits(x[:, :w]) | (bits(x[:, w:]) >> 16), jnp.int32)


def _unpack_halves(p):
    u = lax.bitcast_convert_type(p, jnp.uint32)
    return (lax.bitcast_convert_type(u & jnp.uint32(0xFFFF0000), _F32),
            lax.bitcast_convert_type(u << 16, _F32))


def _post_kernel(x_ref, oa_ref, ob_ref, woa_ref, wob_ref, gmoe_ref, r_ref, h1_ref, hn_ref, rt_ref, cnt_ref):
    h1 = x_ref[...] + _dot(oa_ref[...], woa_ref[...]) + _dot(ob_ref[...], wob_ref[...])
    h1_ref[...] = h1
    hn = _rms(h1, gmoe_ref[...])
    hn_ref[...] = _pack_halves(hn)

    hn_hi = hn.astype(_BF16)
    hn_lo = (hn - hn_hi.astype(_F32)).astype(_BF16)
    hi_both = _dot(hn_hi, r_ref[...])
    logits = hi_both[:, :GATE_PAD] + (_dot(hn_lo, r_ref[:, :GATE_PAD]) + hi_both[:, GATE_PAD:])
    lane = lax.broadcasted_iota(jnp.int32, logits.shape, 1)
    first_idx = lambda hit: jnp.min(jnp.where(hit, lane, LANES), axis=-1, keepdims=True)

    is_g = lane < N_GROUPS
    lg = jnp.where(is_g, logits, NEG)
    mg = jnp.max(lg, axis=-1, keepdims=True)
    sg = jnp.sum(jnp.where(is_g, jnp.exp(lg - mg), 0.0), axis=-1, keepdims=True)
    pg_top = 1.0 / sg
    g_sel = first_idx(is_g & (lg == mg))

    e_lo = ROUTER_OFF + g_sel * EXPERTS_PER_GROUP
    is_e = (lane >= e_lo) & (lane < e_lo + EXPERTS_PER_GROUP)
    le = jnp.where(is_e, logits, NEG)
    m1 = jnp.max(le, axis=-1, keepdims=True)
    se = jnp.sum(jnp.where(is_e, jnp.exp(le - m1), 0.0), axis=-1, keepdims=True)
    i1 = first_idx(is_e & (le == m1))
    le2 = jnp.where(lane == i1, NEG, le)
    m2 = jnp.max(le2, axis=-1, keepdims=True)
    i2 = first_idx(is_e & (lane != i1) & (le2 == m2))
    pe1 = 1.0 / se
    pe2 = jnp.exp(m2 - m1) / se
    denom = pe1 + pe2
    rt = (jnp.where(lane == i1 - ROUTER_OFF, 1.0, 0.0)
          + jnp.where(lane == i2 - ROUTER_OFF + N_EXPERTS, 1.0, 0.0)
          + jnp.where(lane == RT_W, pg_top * pe1 / denom, 0.0)
          + jnp.where(lane == RT_W + 1, pg_top * pe2 / denom, 0.0))
    rt_ref[...] = rt

    @pl.when(pl.program_id(0) == 0)
    def _():
        cnt_ref[...] = jnp.zeros_like(cnt_ref)

    cnt_ref[...] += jnp.sum(rt, axis=0, keepdims=True)


def _post_call(x2, oa, ob, woa, wob, gmoe, r_cat):
    n = x2.shape[0]
    tm = TM_POST
    row = lambda i: (i, 0)
    const2 = lambda i: (0, 0)
    return pl.pallas_call(
        _post_kernel,
        grid=(n // tm,),
        in_specs=[pl.BlockSpec((tm, D_MODEL), row), pl.BlockSpec((tm, A_WIDTH), row),
                  pl.BlockSpec((tm, B_WIDTH), row), pl.BlockSpec((A_WIDTH, D_MODEL), const2),
                  pl.BlockSpec((B_WIDTH, D_MODEL), const2), pl.BlockSpec((1, D_MODEL), const2),
                  pl.BlockSpec((D_MODEL, 2 * GATE_PAD), const2)],
        out_specs=[pl.BlockSpec((tm, D_MODEL), row), pl.BlockSpec((tm, D_MODEL // 2), row),
                   pl.BlockSpec((tm, GATE_PAD), row), pl.BlockSpec((8, LANES), const2)],
        out_shape=[jax.ShapeDtypeStruct((n, D_MODEL), _F32), jax.ShapeDtypeStruct((n, D_MODEL // 2), jnp.int32),
                   jax.ShapeDtypeStruct((n, GATE_PAD), _F32), jax.ShapeDtypeStruct((8, LANES), _F32)],
        compiler_params=pltpu.CompilerParams(dimension_semantics=("arbitrary",), vmem_limit_bytes=VMEM_LIMIT),
        name="post",
    )(x2, oa, ob, woa, wob, gmoe, r_cat)


def _route_kernel(rt_ref, cnt_ref, dest_ref, meta_ref, off_ref, run_ref):
    tm = rt_ref.shape[0]
    lane = lax.broadcasted_iota(jnp.int32, (1, LANES), 1)
    first = lane < N_EXPERTS
    onehot = jnp.where(lane < 2 * N_EXPERTS, rt_ref[...], 0.0)

    @pl.when(pl.program_id(0) == 0)
    def _():
        cnt = jnp.where(lane < 2 * N_EXPERTS, cnt_ref[...], 0.0)
        c1 = jnp.where(first, cnt, 0.0)
        tot = c1 + jnp.where(first, pltpu.roll(cnt, LANES - N_EXPERTS, 1), 0.0)
        tiles = jnp.floor((tot + (TR_GMM - 1)) * (1.0 / TR_GMM))
        e_row = lax.broadcasted_iota(jnp.int32, (LANES, LANES), 0)
        e_col = lax.broadcasted_iota(jnp.int32, (LANES, LANES), 1)
        before = jnp.where(e_row < e_col, 1.0, 0.0).astype(_BF16)
        base = _dot(tiles.astype(_BF16), before) * TR_GMM
        off_ref[...] = jnp.where(first, base, 0.0) + pltpu.roll(jnp.where(first, base + c1, 0.0), N_EXPERTS, 1)
        run_ref[...] = jnp.zeros_like(run_ref)
        meta_ref[...] = tiles

    r_io = lax.broadcasted_iota(jnp.int32, (tm, tm), 0)
    c_io = lax.broadcasted_iota(jnp.int32, (tm, tm), 1)
    earlier = jnp.where(c_io < r_io, 1.0, 0.0).astype(_BF16)
    rank = _dot(earlier, onehot.astype(_BF16)) + run_ref[0:1, :]
    slot = onehot * (rank + off_ref[0:1, :])
    d1 = jnp.sum(jnp.where(first, slot, 0.0), axis=-1, keepdims=True)
    d2 = jnp.sum(jnp.where(first, 0.0, slot), axis=-1, keepdims=True)
    dest_ref[...] = (jnp.where(lane == 0, d1, 0.0) + jnp.where(lane == 1, d2, 0.0)).astype(jnp.int32)
    run_ref[...] += jnp.sum(onehot, axis=0, keepdims=True)


def _route_call(rt, cnt):
    n = rt.shape[0]
    tm = TM_ROUTE
    return pl.pallas_call(
        _route_kernel,
        grid=(n // tm,),
        in_specs=[pl.BlockSpec((tm, GATE_PAD), lambda i: (i, 0)), pl.BlockSpec((8, LANES), lambda i: (0, 0))],
        out_specs=[pl.BlockSpec((tm, LANES), lambda i: (i, 0)), pl.BlockSpec((8, LANES), lambda i: (0, 0))],
        out_shape=[jax.ShapeDtypeStruct((n, LANES), jnp.int32), jax.ShapeDtypeStruct((8, LANES), _F32)],
        scratch_shapes=[pltpu.VMEM((8, LANES), _F32)] * 2,
        compiler_params=pltpu.CompilerParams(dimension_semantics=("arbitrary",)),
        name="route",
    )(rt, cnt)


def _sc_mesh():
    return plsc.VectorSubcoreMesh(core_axis_name="c", subcore_axis_name="s")


def _sc_worker(n_rows):
    per = n_rows // (SC_CORES * SC_SUBCORES)
    return (lax.axis_index("s") * SC_CORES + lax.axis_index("c")) * per, per


def _dispatch_call(xp, d1, d2, n_slots):
    n, w = xp.shape
    assert n % (SC_CORES * SC_SUBCORES * SC_CHUNK) == 0

    @functools.partial(
        pl.kernel, mesh=_sc_mesh(), out_type=jax.ShapeDtypeStruct((n_slots, w), xp.dtype),
        scratch_types=[pltpu.VMEM((SC_CHUNK,), jnp.int32), pltpu.VMEM((SC_CHUNK,), jnp.int32),
                       pltpu.VMEM((SC_CHUNK, w), xp.dtype), pltpu.SemaphoreType.DMA],
        name="dispatch")
    def k(x_hbm, d1_hbm, d2_hbm, xs_hbm, i1_v, i2_v, rows_v, sem):
        row0, per = _sc_worker(n)

        @pl.loop(0, per // SC_CHUNK)
        def _(j):
            src = pl.ds(row0 + j * SC_CHUNK, SC_CHUNK)
            pltpu.sync_copy(d1_hbm.at[src], i1_v)
            pltpu.sync_copy(d2_hbm.at[src], i2_v)
            pltpu.sync_copy(x_hbm.at[src], rows_v)
            first = pltpu.async_copy(rows_v, xs_hbm.at[i1_v], sem)
            second = pltpu.async_copy(rows_v, xs_hbm.at[i2_v], sem)
            first.wait()
            second.wait()

    return k(xp, d1, d2)


def _combine_call(ys, d1, d2):
    n = d1.shape[0]
    w = ys.shape[1]
    assert n % (SC_CORES * SC_SUBCORES * SC_CHUNK) == 0
    out = jax.ShapeDtypeStruct((n, w), ys.dtype)

    @functools.partial(
        pl.kernel, mesh=_sc_mesh(), out_type=(out, out),
        scratch_types=[pltpu.VMEM((SC_CHUNK,), jnp.int32), pltpu.VMEM((SC_CHUNK, w), ys.dtype),
                       pltpu.SemaphoreType.DMA],
        name="combine")
    def k(ys_hbm, d1_hbm, d2_hbm, y1_hbm, y2_hbm, i_v, rows_v, sem):
        row0, per = _sc_worker(n)

        @pl.loop(0, per // SC_CHUNK)
        def _(j):
            dst = pl.ds(row0 + j * SC_CHUNK, SC_CHUNK)
            for d_hbm, y_hbm in ((d1_hbm, y1_hbm), (d2_hbm, y2_hbm)):
                pltpu.sync_copy(d_hbm.at[dst], i_v)
                pltpu.async_copy(ys_hbm.at[i_v], rows_v, sem).wait()
                pltpu.sync_copy(rows_v, y_hbm.at[dst])

    return k(ys, d1, d2)


def _gmm_kernel(te_ref, nu_ref, xs_ref, wg_ref, wu_ref, wd_ref, ys_ref):
    half = D_MODEL // 2

    @pl.when(pl.program_id(0) < nu_ref[0])
    def _():
        for r in range(TR_GMM // SUB_MOE):
            rows = slice(r * SUB_MOE, (r + 1) * SUB_MOE)
            a, b = _unpack_halves(xs_ref[rows, :])
            a, b = a.astype(_BF16), b.astype(_BF16)
            gate = _dot(a, wg_ref[0, :half, :]) + _dot(b, wg_ref[0, half:, :])
            up = _dot(a, wu_ref[0, :half, :]) + _dot(b, wu_ref[0, half:, :])
            hid = gate * jax.nn.sigmoid(gate) * up
            ys_ref[rows, :] = _pack_halves(_dot(hid.astype(_BF16), wd_ref[0]))


def _gmm_call(tile_expert, n_used, xs, w_gate, w_up, w_down):
    n_slots, w = xs.shape
    rows = lambda t, te, nu: (jnp.minimum(t, nu[0] - 1), 0)
    expert = lambda t, te, nu: (te[t], 0, 0)
    return pl.pallas_call(
        _gmm_kernel,
        grid_spec=pltpu.PrefetchScalarGridSpec(
            num_scalar_prefetch=2, grid=(n_slots // TR_GMM,),
            in_specs=[pl.BlockSpec((TR_GMM, w), rows),
                      pl.BlockSpec((1, D_MODEL, D_FF_EXPERT), expert),
                      pl.BlockSpec((1, D_MODEL, D_FF_EXPERT), expert),
                      pl.BlockSpec((1, D_FF_EXPERT, D_MODEL), expert)],
            out_specs=pl.BlockSpec((TR_GMM, w), rows)),
        out_shape=jax.ShapeDtypeStruct((n_slots, w), xs.dtype),
        compiler_params=pltpu.CompilerParams(dimension_semantics=("arbitrary",), vmem_limit_bytes=VMEM_LIMIT),
        name="gmm",
    )(tile_expert, n_used, xs, w_gate, w_up, w_down)


def _final_kernel(h1_ref, y1_ref, y2_ref, rt_ref, p_ref, gple_ref, wpg_ref, wpp_ref, gfin_ref, o_ref):
    for r in range(h1_ref.shape[0] // SUB_MOE):
        rows = slice(r * SUB_MOE, (r + 1) * SUB_MOE)
        rt = rt_ref[rows, :]
        w1, w2 = rt[:, RT_W:RT_W + 1], rt[:, RT_W + 1:RT_W + 2]
        a1, b1 = _unpack_halves(y1_ref[rows, :])
        a2, b2 = _unpack_halves(y2_ref[rows, :])
        h2 = h1_ref[rows, :] + jnp.concatenate([w1 * a1 + w2 * a2, w1 * b1 + w2 * b2], axis=1)
        gate = jax.nn.sigmoid(_dot(_rms(h2, gple_ref[...]).astype(_BF16), wpg_ref[...]))
        h3 = h2 + _dot(p_ref[rows, :].astype(_BF16), wpp_ref[...]) * gate
        o_ref[rows, :] = _rms(h3, gfin_ref[...])


def _final_call(h1, y1, y2, rt, p2, gple, wpg, wpp, gfin):
    n = h1.shape[0]
    tm = TM_POST
    row = lambda i: (i, 0)
    const2 = lambda i: (0, 0)
    return pl.pallas_call(
        _final_kernel,
        grid=(n // tm,),
        in_specs=[pl.BlockSpec((tm, D_MODEL), row), pl.BlockSpec((tm, D_MODEL // 2), row),
                  pl.BlockSpec((tm, D_MODEL // 2), row), pl.BlockSpec((tm, GATE_PAD), row),
                  pl.BlockSpec((tm, D_PLE), row), pl.BlockSpec((1, D_MODEL), const2),
                  pl.BlockSpec((D_MODEL, D_MODEL), const2), pl.BlockSpec((D_PLE, D_MODEL), const2),
                  pl.BlockSpec((1, D_MODEL), const2)],
        out_specs=pl.BlockSpec((tm, D_MODEL), row),
        out_shape=jax.ShapeDtypeStruct((n, D_MODEL), _F32),
        compiler_params=pltpu.CompilerParams(dimension_semantics=("arbitrary",), vmem_limit_bytes=VMEM_LIMIT),
        name="final",
    )(h1, y1, y2, rt, p2, gple, wpg, wpp, gfin)


def _rope_tables(seq):
    half = HEAD_DIM // 2
    inv = 1.0 / (ROPE_THETA ** (jnp.arange(half, dtype=_F32) / half))
    ang = jnp.arange(seq, dtype=_F32)[:, None] * inv[None, :]
    cos, sin = jnp.cos(ang), jnp.sin(ang)
    reps = LANES // HEAD_DIM
    cos_t = jnp.tile(jnp.concatenate([cos, cos], axis=1), (1, reps))
    sin_t = jnp.tile(jnp.concatenate([-sin, sin], axis=1), (1, reps))
    return cos_t, sin_t


def _layer(h, p_i, norm_mix, w_in, gmlp_v_norm, gmlp_w_s, gmlp_b_s,
           cmp_pe_k, cmp_w1_k, cmp_w2_k, cmp_pe_v, cmp_w1_v, cmp_w2_v,
           out_norm_a, out_norm_b, w_o, norm_moe, router_group, router_expert,
           moe_w_gate, moe_w_up, moe_w_down, norm_ple, w_ple_proj, w_ple_gate, norm_final):
    bsz, seq, _ = h.shape
    n = bsz * seq
    x2 = h.reshape(n, D_MODEL)
    row = lambda v: v.reshape(1, -1).astype(_F32)

    w_q = w_in[:, OFF_Q:OFF_KV][:, _PERM_CH]
    w_gate = jnp.pad(w_in[:, OFF_GATE:D_IN], ((0, 0), (0, GATE_PAD - N_GATES)))
    w_all = jnp.concatenate([w_in[:, :OFF_Q], w_q, w_in[:, OFF_KV:OFF_GATE], w_gate], axis=1).astype(_BF16)
    cos_t, sin_t = _rope_tables(seq)
    ws_pairs = gmlp_w_s.reshape(A_HEADS // 2, 2, CHUNK, CHUNK).transpose(0, 2, 1, 3).reshape(
        A_HEADS // 2, CHUNK, 2 * CHUNK)
    bs_exp = jnp.repeat(gmlp_b_s.T, HEAD_DIM, axis=1)

    oa, q, k_cmp, v_cmp, ks0, ks1, vs0, vs1, k_win, v_win, gates = _proj_call(
        x2, row(norm_mix), w_all, cos_t, sin_t, row(gmlp_v_norm), ws_pairs, bs_exp, row(out_norm_a), seq)

    kc, vc = _compress_call(k_cmp, v_cmp, _compress_weights(cmp_w1_k, cmp_w2_k, cmp_pe_k),
                            _compress_weights(cmp_w1_v, cmp_w2_v, cmp_pe_v), bsz, seq)

    b3 = lambda a: a.reshape(bsz, seq, a.shape[-1])
    ob = _attn_call(b3(q), b3(gates), kc, vc, b3(ks0), b3(ks1), b3(vs0), b3(vs1), b3(k_win), b3(v_win),
                    row(out_norm_b[_PERM_CH]), bsz, seq)

    r_cat = jnp.pad(jnp.concatenate([router_group, router_expert], axis=1),
                    ((0, 0), (0, GATE_PAD - N_GROUPS - N_EXPERTS)))
    r_hi = r_cat.astype(_BF16)
    r_cat = jnp.concatenate([r_hi, (r_cat - r_hi.astype(_F32)).astype(_BF16)], axis=1)
    h1, xp, rt, cnt = _post_call(x2, oa, ob.reshape(n, B_WIDTH), w_o[:A_WIDTH].astype(_BF16),
                            w_o[A_WIDTH:][_PERM_CH].astype(_BF16), row(norm_moe), r_cat)

    dest, meta = _route_call(rt, cnt)
    d1, d2 = dest[:, 0], dest[:, 1]
    n_tiles = 2 * n // TR_GMM + N_EXPERTS
    ends = jnp.cumsum(meta[0, :N_EXPERTS].astype(jnp.int32))
    tile_expert = jnp.minimum(jnp.sum(ends[None, :] <= jnp.arange(n_tiles)[:, None], axis=1),
                              N_EXPERTS - 1).astype(jnp.int32)
    xs = _dispatch_call(xp, d1, d2, n_tiles * TR_GMM)
    ys = _gmm_call(tile_expert, ends[-1:], xs, moe_w_gate.astype(_BF16), moe_w_up.astype(_BF16),
                   moe_w_down.astype(_BF16))
    y1, y2 = _combine_call(ys, d1, d2)
    out = _final_call(h1, y1, y2, rt, p_i.reshape(n, D_PLE), row(norm_ple),
                      w_ple_gate.astype(_BF16), w_ple_proj.astype(_BF16), row(norm_final))
    return out.reshape(bsz, seq, D_MODEL)


def kernel(x, p, norm_mix, w_in, gmlp_v_norm, gmlp_w_s, gmlp_b_s, cmp_pe_k, cmp_w1_k, cmp_w2_k,
           cmp_pe_v, cmp_w1_v, cmp_w2_v, out_norm_a, out_norm_b, w_o, norm_moe, router_group,
           router_expert, moe_w_gate, moe_w_up, moe_w_down, norm_ple, w_ple_proj, w_ple_gate, norm_final):
    assert p.shape[0] == 1, "single-layer trunk"
    assert x.shape[1] % SEL_CK == 0 and x.shape[1] >= WIN + Q_BLOCK
    assert (x.shape[0] * x.shape[1]) % (SC_CORES * SC_SUBCORES * SC_CHUNK) == 0
    return _layer(x, p[0], norm_mix[0], w_in[0], gmlp_v_norm[0], gmlp_w_s[0], gmlp_b_s[0],
                  cmp_pe_k[0], cmp_w1_k[0], cmp_w2_k[0], cmp_pe_v[0], cmp_w1_v[0], cmp_w2_v[0],
                  out_norm_a[0], out_norm_b[0], w_o[0], norm_moe[0], router_group[0], router_expert[0],
                  moe_w_gate[0], moe_w_up[0], moe_w_down[0], norm_ple[0], w_ple_proj[0], w_ple_gate[0],
                  norm_final)
```

```python
import functools

import numpy as np
import jax
import jax.numpy as jnp
from jax import lax
from jax.experimental import pallas as pl
from jax.experimental.pallas import tpu as pltpu
from jax.experimental.pallas import tpu_sc as plsc

D_MODEL = 1024
HEAD_DIM = 64
A_HEADS = 8
A_WIDTH = A_HEADS * HEAD_DIM
B_HEADS = 8
B_WIDTH = B_HEADS * HEAD_DIM
B_KV = 2
B_HPG = B_HEADS // B_KV
KV_W = B_KV * HEAD_DIM
N_GATES = B_HEADS * 3
CHUNK = 128
L_CMP = 32
STRIDE_CMP = 16
CMP_HIDDEN = 256
L_SEL = 64
N_SEL = 16
WIN = 512
WIN_Q = 256
Q_BLOCK = 512
ROPE_THETA = 10000.0
N_GROUPS = 4
EXPERTS_PER_GROUP = 4
N_EXPERTS = N_GROUPS * EXPERTS_PER_GROUP
D_FF_EXPERT = 512
D_PLE = 256
EPS = 1e-6
LOG2E = 1.4426950408889634
NEG = -1e30
FORCE = 1e6

OFF_Q = 2 * A_WIDTH
OFF_KV = OFF_Q + B_WIDTH
OFF_GATE = OFF_KV + 6 * KV_W
D_IN = OFF_GATE + N_GATES

LANES = 128
GATE_PAD = LANES
ROUTER_OFF = N_GROUPS
W_ALL = OFF_GATE + GATE_PAD

TM_PROJ = 1024
TM_POST = 1024
TM_ROUTE = 512
TR_GMM = 512
SUB_MOE = 512
RT_W = 2 * N_EXPERTS
SC_CORES = 2
SC_SUBCORES = 16
SC_CHUNK = 128
SEL_CK = 512
VMEM_LIMIT = 56 * 1024 * 1024

_PERM_HEADS = [0, 4, 1, 5, 2, 6, 3, 7]
_PERM_CH = np.concatenate([np.arange(HEAD_DIM) + HEAD_DIM * h for h in _PERM_HEADS])

_F32 = jnp.float32
_BF16 = jnp.bfloat16


def _dot(a, b):
    return jnp.dot(a, b, preferred_element_type=_F32)


def _dot_nt(a, b):
    return lax.dot_general(a, b, (((1,), (1,)), ((), ())), preferred_element_type=_F32)


def _rms(x, g):
    return x * lax.rsqrt(jnp.mean(x * x, axis=-1, keepdims=True) + EPS) * g


def _gelu(x):
    return 0.5 * x * (1.0 + jnp.tanh(0.7978845608028654 * (x + 0.044715 * (x * x * x))))


def _rope_tile(x, cos, sin_signed):
    lane = lax.broadcasted_iota(jnp.int32, x.shape, 1)
    first_half = (lane % HEAD_DIM) < (HEAD_DIM // 2)
    rot = jnp.where(first_half, pltpu.roll(x, LANES - HEAD_DIM // 2, 1), pltpu.roll(x, HEAD_DIM // 2, 1))
    return x * cos + rot * sin_signed


def _proj_kernel(x_ref, gmix_ref, w_ref, cos_ref, sin_ref, gv_ref, ws_ref, bs_ref, goa_ref,
                 oa_ref, q_ref, kc_ref, vc_ref, ks0_ref, ks1_ref, vs0_ref, vs1_ref, kw_ref, vw_ref, gate_ref,
                 *, seq):
    tm = x_ref.shape[0]
    hn = _rms(x_ref[...], gmix_ref[...]).astype(_BF16)
    cos = cos_ref[...]
    sin = sin_ref[...]

    zq = _dot(hn, w_ref[:, OFF_Q:OFF_KV])
    scale = HEAD_DIM ** -0.5 * LOG2E
    for j in range(B_WIDTH // LANES):
        blk = _rope_tile(zq[:, j * LANES:(j + 1) * LANES], cos, sin) * scale
        q_ref[:, j * LANES:(j + 1) * LANES] = blk.astype(q_ref.dtype)

    zkv = _dot(hn, w_ref[:, OFF_KV:OFF_GATE])
    kv = []
    for j in range(6):
        blk = zkv[:, j * KV_W:(j + 1) * KV_W]
        kv.append(_rope_tile(blk, cos, sin) if j % 2 == 0 else blk)
    k_cmp, v_cmp, k_slc, v_slc, k_win, v_win = kv
    kc_ref[...] = k_cmp.astype(kc_ref.dtype)
    vc_ref[...] = v_cmp.astype(vc_ref.dtype)
    kw_ref[...] = k_win.astype(kw_ref.dtype)
    vw_ref[...] = v_win.astype(vw_ref.dtype)

    row = lax.broadcasted_iota(jnp.int32, (tm, LANES), 0)
    lane = lax.broadcasted_iota(jnp.int32, (tm, LANES), 1)
    key_block = ((pl.program_id(0) % (seq // tm)) * tm + row) // L_SEL
    onehot = jnp.where(lane == key_block, 1.0, 0.0).astype(ks0_ref.dtype)
    for g, (ks_ref, vs_ref) in enumerate(((ks0_ref, vs0_ref), (ks1_ref, vs1_ref))):
        own = (lane // HEAD_DIM) == g
        ks_ref[:, :LANES] = jnp.where(own, k_slc, 0.0).astype(ks_ref.dtype)
        ks_ref[:, LANES:] = onehot
        vs_ref[...] = jnp.where(own, v_slc, 1.0).astype(vs_ref.dtype)

    zg = _dot(hn, w_ref[:, OFF_GATE:W_ALL])
    gate_ref[...] = jax.nn.sigmoid(zg)

    zu = _gelu(_dot(hn, w_ref[:, 0:A_WIDTH]))
    zv = _gelu(_dot(hn, w_ref[:, A_WIDTH:2 * A_WIDTH]))
    vn = _rms(zv, gv_ref[...]).astype(_BF16)

    t_io = lax.broadcasted_iota(jnp.int32, (CHUNK, 2 * CHUNK), 0)
    s_io = lax.broadcasted_iota(jnp.int32, (CHUNK, 2 * CHUNK), 1) % CHUNK
    causal = s_io <= t_io
    lane = lax.broadcasted_iota(jnp.int32, (CHUNK, LANES), 1)
    lo = lane < HEAD_DIM
    bs = bs_ref[...]
    chunks = []
    for c in range(tm // CHUNK):
        tiles = []
        for pr in range(A_HEADS // 2):
            wcat = jnp.where(causal, ws_ref[pr], 0.0).astype(_BF16)
            vblk = vn[c * CHUNK:(c + 1) * CHUNK, pr * LANES:(pr + 1) * LANES]
            zero = jnp.zeros_like(vblk)
            rhs = jnp.concatenate([jnp.where(lo, vblk, zero), jnp.where(lo, zero, vblk)], axis=0)
            tiles.append(_dot(wcat, rhs))
        chunks.append(jnp.concatenate(tiles, axis=1) + bs)
    mixed = jnp.concatenate(chunks, axis=0)
    oa = zu * mixed
    oa_ref[...] = _rms(oa, goa_ref[...]).astype(oa_ref.dtype)


def _proj_call(x2, gmix, w_all, cos_t, sin_t, gv, ws_pairs, bs_exp, goa, seq):
    n = x2.shape[0]
    tm = TM_PROJ
    n_t = seq // tm
    row = lambda i: (i, 0)
    const2 = lambda i: (0, 0)
    pos = lambda i: (i % n_t, 0)
    out_shapes = [jax.ShapeDtypeStruct((n, A_WIDTH), _BF16), jax.ShapeDtypeStruct((n, B_WIDTH), _BF16)]
    kv_widths = [KV_W, KV_W, KV_W + LANES, KV_W + LANES, KV_W, KV_W, KV_W, KV_W]
    out_shapes += [jax.ShapeDtypeStruct((n, w), _BF16) for w in kv_widths]
    out_shapes += [jax.ShapeDtypeStruct((n, GATE_PAD), _F32)]
    out_specs = [pl.BlockSpec((tm, A_WIDTH), row), pl.BlockSpec((tm, B_WIDTH), row)]
    out_specs += [pl.BlockSpec((tm, w), row) for w in kv_widths]
    out_specs += [pl.BlockSpec((tm, GATE_PAD), row)]
    return pl.pallas_call(
        functools.partial(_proj_kernel, seq=seq),
        grid=(n // tm,),
        in_specs=[
            pl.BlockSpec((tm, D_MODEL), row),
            pl.BlockSpec((1, D_MODEL), const2),
            pl.BlockSpec((D_MODEL, W_ALL), const2),
            pl.BlockSpec((tm, LANES), pos),
            pl.BlockSpec((tm, LANES), pos),
            pl.BlockSpec((1, A_WIDTH), const2),
            pl.BlockSpec((A_HEADS // 2, CHUNK, 2 * CHUNK), lambda i: (0, 0, 0)),
            pl.BlockSpec((CHUNK, A_WIDTH), const2),
            pl.BlockSpec((1, A_WIDTH), const2),
        ],
        out_specs=out_specs,
        out_shape=out_shapes,
        compiler_params=pltpu.CompilerParams(dimension_semantics=("arbitrary",), vmem_limit_bytes=VMEM_LIMIT),
        name="proj",
    )(x2, gmix, w_all, cos_t, sin_t, gv, ws_pairs, bs_exp, goa)


def _compress_kernel(rk_ref, rv_ref, pek_ref, pev_ref, w1k_ref, w1v_ref,
                     tk_ref, bk_ref, w2k_ref, tv_ref, bv_ref, w2v_ref, kc_ref, vc_ref):
    def one(r_ref, pe_ref, w1_ref, top_ref, bot_ref, w2_ref, o_ref):
        r = r_ref[0]
        nr = r.shape[0]
        a = _dot(r, top_ref[...])
        b = _dot(r, bot_ref[...])
        pe_h = _dot(pe_ref[...], w1_ref[...])
        pe2 = jnp.concatenate([pe_h[0:1], pe_h[0:1]], axis=1)
        hid = a + pltpu.roll(b, nr - 1, 0) + pe2
        o_ref[0] = _dot(_gelu(hid).astype(_BF16), w2_ref[...]).astype(o_ref.dtype)

    one(rk_ref, pek_ref, w1k_ref, tk_ref, bk_ref, w2k_ref, kc_ref)
    one(rv_ref, pev_ref, w1v_ref, tv_ref, bv_ref, w2v_ref, vc_ref)


def _compress_weights(w1, w2, pe):
    half = L_CMP // 2
    w1r = w1.reshape(L_CMP, HEAD_DIM, CMP_HIDDEN)
    eye = jnp.eye(B_KV, dtype=w1.dtype)
    place = lambda part: jnp.einsum('ldj,gh->lgdhj', part, eye).reshape(half * KV_W, B_KV * CMP_HIDDEN)
    top = place(w1r[:half]).astype(_BF16)
    bot = place(w1r[half:]).astype(_BF16)
    w2bd = jnp.einsum('jd,gh->gjhd', w2, eye).reshape(B_KV * CMP_HIDDEN, KV_W).astype(_BF16)
    pe8 = jnp.broadcast_to(pe.reshape(1, L_CMP * HEAD_DIM), (8, L_CMP * HEAD_DIM)).astype(_BF16)
    return pe8, w1.astype(_BF16), top, bot, w2bd


def _compress_call(k_cmp, v_cmp, wk, wv, bsz, seq):
    nr = seq // STRIDE_CMP
    rk = k_cmp.reshape(bsz, nr, STRIDE_CMP * KV_W)
    rv = v_cmp.reshape(bsz, nr, STRIDE_CMP * KV_W)
    pek, w1k, tk, bk, w2k = wk
    pev, w1v, tv, bv, w2v = wv
    rspec = pl.BlockSpec((1, nr, STRIDE_CMP * KV_W), lambda b: (b, 0, 0))
    full = lambda a: pl.BlockSpec(a.shape, lambda b: (0,) * a.ndim)
    ospec = pl.BlockSpec((1, nr, KV_W), lambda b: (b, 0, 0))
    return pl.pallas_call(
        _compress_kernel,
        grid=(bsz,),
        in_specs=[rspec, rspec, full(pek), full(pev), full(w1k), full(w1v),
                  full(tk), full(bk), full(w2k), full(tv), full(bv), full(w2v)],
        out_specs=[ospec, ospec],
        out_shape=[jax.ShapeDtypeStruct((bsz, nr, KV_W), _BF16)] * 2,
        compiler_params=pltpu.CompilerParams(dimension_semantics=("arbitrary",), vmem_limit_bytes=VMEM_LIMIT),
        name="compress",
    )(rk, rv, pek, pev, w1k, w1v, tk, bk, w2k, tv, bv, w2v)


def _topk_rows_mask(sc_t, k):
    n_rows = sc_t.shape[0]
    row = lax.broadcasted_iota(jnp.int32, sc_t.shape, 0)
    taken = -3e38
    for _ in range(k):
        m = jnp.max(sc_t, axis=0, keepdims=True)
        idx = jnp.min(jnp.where(sc_t == m, row, n_rows), axis=0, keepdims=True)
        sc_t = jnp.where(row == idx, taken, sc_t)
    return jnp.where(sc_t == taken, 1.0, 0.0)


def _attn_kernel(q_ref, gate_ref, gexp_ref, kc_ref, vc_ref, ks0_ref, ks1_ref, vs0_ref, vs1_ref,
                 kw_ref, vw_ref, gob_ref, o_ref, m_scr, acc_scr, *, seq):
    qb = pl.program_id(1)
    t0 = qb * Q_BLOCK
    n_cmp = kc_ref.shape[1]
    n_sb = seq // L_SEL
    k_top = min(N_SEL, n_sb)
    rows = B_HPG * Q_BLOCK
    ks_refs, vs_refs = (ks0_ref, ks1_ref), (vs0_ref, vs1_ref)

    lane_q = lax.broadcasted_iota(jnp.int32, (Q_BLOCK, LANES), 1)
    lo = lane_q < HEAD_DIM
    t_col = t0 + lax.broadcasted_iota(jnp.int32, (Q_BLOCK, 1), 0)

    def per_head(x):
        return x.reshape(B_HPG, Q_BLOCK, x.shape[-1])

    def add_bias(s, bias):
        return (per_head(s) + bias[None]).reshape(rows, s.shape[-1])

    qs = []
    for g in range(B_KV):
        own = (lane_q // HEAD_DIM) == g
        qs.append(jnp.concatenate(
            [jnp.where(own, q_ref[0, :, j * LANES:(j + 1) * LANES], jnp.zeros((), q_ref.dtype))
             for j in range(B_HPG)], axis=0))

    c_lane = lax.broadcasted_iota(jnp.int32, (Q_BLOCK, n_cmp), 1)
    bias_c = jnp.where(c_lane * STRIDE_CMP + (L_CMP - 1) <= t_col, 0.0, NEG)
    has_c = (t_col >= L_CMP - 1).astype(_F32)
    c_row = lax.broadcasted_iota(jnp.int32, (n_cmp, n_sb), 0) * STRIDE_CMP
    s_col = lax.broadcasted_iota(jnp.int32, (n_cmp, n_sb), 1) * L_SEL
    overlap = jnp.where((c_row < s_col + L_SEL) & (c_row + L_CMP > s_col), 1.0, 0.0).astype(_BF16)
    blk = lax.broadcasted_iota(jnp.int32, (Q_BLOCK, n_sb), 1)
    cur = t_col // L_SEL
    forced = (blk == 0) | (blk == cur) | (blk == cur - 1)
    valid = blk * L_SEL <= t_col

    o_c, sel_bias = [], []
    for g in range(B_KV):
        s_c = add_bias(_dot_nt(qs[g], kc_ref[0]), bias_c)
        e_c = jnp.exp2(s_c - jnp.max(s_c, axis=-1, keepdims=True))
        inv = per_head(1.0 / jnp.maximum(jnp.sum(e_c, axis=-1, keepdims=True), 1e-30)) * has_c[None]
        p_c = (per_head(e_c) * inv).reshape(rows, n_cmp)
        o_c.append(_dot(p_c.astype(_BF16), vc_ref[0]))

        p_sum = jnp.sum(per_head(p_c), axis=0)
        p_hi = p_sum.astype(_BF16)
        p_lo = (p_sum - p_hi.astype(_F32)).astype(_BF16)
        imp = _dot(p_hi, overlap) + _dot(p_lo, overlap)
        score = jnp.where(valid & jnp.logical_not(forced), imp, -FORCE)
        chosen = forced | (_topk_rows_mask(score.T, k_top - 3).T > 0.5)
        sb = jnp.where(chosen & valid, 0.0, NEG)
        if n_sb < LANES:
            sb = jnp.concatenate([sb, jnp.full((Q_BLOCK, LANES - n_sb), NEG, _F32)], axis=1)
        sel_bias.append(sb)

    w_len = WIN + WIN_Q
    lane_w = lax.broadcasted_iota(jnp.int32, (w_len, LANES), 1)
    acc_w_sub = [[] for _ in range(B_KV)]
    for hh in range(Q_BLOCK // WIN_Q):
        w_start = pl.multiple_of(jnp.maximum(t0 + hh * WIN_Q - WIN, 0), WIN_Q)
        t_sub = t_col[hh * WIN_Q:(hh + 1) * WIN_Q]
        diff_w = t_sub - (w_start + lax.broadcasted_iota(jnp.int32, (WIN_Q, w_len), 1))
        bias_w = jnp.where((diff_w >= 0) & (diff_w < WIN), 0.0, NEG)
        kw = kw_ref[0, pl.ds(w_start, w_len), :]
        vw = vw_ref[0, pl.ds(w_start, w_len), :]
        for g in range(B_KV):
            q_sub = per_head(qs[g])[:, hh * WIN_Q:(hh + 1) * WIN_Q, :].reshape(B_HPG * WIN_Q, LANES)
            s_w = (_dot_nt(q_sub, kw).reshape(B_HPG, WIN_Q, w_len) + bias_w[None]).reshape(B_HPG * WIN_Q, w_len)
            e_w = jnp.exp2(s_w - jnp.max(s_w, axis=-1, keepdims=True)).astype(_BF16)
            vw_aug = jnp.where((lane_w // HEAD_DIM) == g, vw, jnp.ones((), vw.dtype))
            acc_w_sub[g].append(_dot(e_w, vw_aug).reshape(B_HPG, WIN_Q, LANES))
    acc_w = [jnp.concatenate(acc_w_sub[g], axis=1).reshape(rows, LANES) for g in range(B_KV)]

    n_ck = (t0 + Q_BLOCK + SEL_CK - 1) // SEL_CK
    key_lane = lax.broadcasted_iota(jnp.int32, (Q_BLOCK, SEL_CK), 1)
    bias_diag = jnp.where((n_ck - 1) * SEL_CK + key_lane <= t_col, 0.0, NEG)
    q_wide = [jnp.concatenate([qs[g], jnp.concatenate([sel_bias[g].astype(_BF16)] * B_HPG, axis=0)], axis=1)
              for g in range(B_KV)]

    m_scr[...] = jnp.full(m_scr.shape, NEG, _F32)
    acc_scr[...] = jnp.zeros(acc_scr.shape, _F32)

    def sel_chunk(ci, diag):
        k0 = pl.multiple_of(ci * SEL_CK, SEL_CK)
        for g in range(B_KV):
            m = m_scr[g]
            s = _dot_nt(q_wide[g], ks_refs[g][0, pl.ds(k0, SEL_CK), :])
            if diag:
                s = add_bias(s, bias_diag)
            m_new = jnp.maximum(m, jnp.max(s, axis=-1, keepdims=True))
            p = jnp.exp2(s - jnp.concatenate([m_new] * (SEL_CK // LANES), axis=1)).astype(_BF16)
            acc_scr[g] = jnp.exp2(m - m_new) * acc_scr[g] + _dot(p, vs_refs[g][0, pl.ds(k0, SEL_CK), :])
            m_scr[g] = m_new

    @pl.loop(0, n_ck - 1)
    def _(ci):
        sel_chunk(ci, False)

    sel_chunk(n_ck - 1, True)
    acc_s = [acc_scr[g] for g in range(B_KV)]

    def numer(acc):
        return jnp.concatenate([jnp.where(lo, acc[0][j * Q_BLOCK:(j + 1) * Q_BLOCK],
                                          acc[1][j * Q_BLOCK:(j + 1) * Q_BLOCK]) for j in range(B_HPG)], axis=1)

    def denom(acc):
        return jnp.concatenate([pltpu.roll(jnp.where(lo, acc[1][j * Q_BLOCK:(j + 1) * Q_BLOCK],
                                                     acc[0][j * Q_BLOCK:(j + 1) * Q_BLOCK]), HEAD_DIM, 1)
                                for j in range(B_HPG)], axis=1)

    gates = gate_ref[0]
    g_hi = gates.astype(_BF16)
    g_lo = (gates - g_hi.astype(_F32)).astype(_BF16)
    gate_of = lambda r: _dot(g_hi, gexp_ref[r]) + _dot(g_lo, gexp_ref[r])
    ob = (gate_of(0) * numer(o_c)
          + gate_of(1) * numer(acc_s) * (1.0 / jnp.maximum(denom(acc_s), 1e-30))
          + gate_of(2) * numer(acc_w) * (1.0 / jnp.maximum(denom(acc_w), 1e-30)))
    o_ref[0] = _rms(ob, gob_ref[...]).astype(o_ref.dtype)


def _gate_expand():
    x = np.zeros((3, GATE_PAD, B_WIDTH), np.float32)
    for slot, h in enumerate(_PERM_HEADS):
        for r in range(3):
            x[r, 3 * h + r, slot * HEAD_DIM:(slot + 1) * HEAD_DIM] = 1.0
    return jnp.asarray(x, _BF16)


def _attn_call(q, gates, kc, vc, ks0, ks1, vs0, vs1, kw, vw, gob, bsz, seq):
    assert seq // L_SEL <= LANES
    n_cmp = kc.shape[1]
    qspec = lambda w: pl.BlockSpec((1, Q_BLOCK, w), lambda b, i: (b, i, 0))
    full = lambda r, w=KV_W: pl.BlockSpec((1, r, w), lambda b, i: (b, 0, 0), pipeline_mode=pl.Buffered(1))
    return pl.pallas_call(
        functools.partial(_attn_kernel, seq=seq),
        grid=(bsz, seq // Q_BLOCK),
        in_specs=[qspec(B_WIDTH), qspec(GATE_PAD),
                  pl.BlockSpec((3, GATE_PAD, B_WIDTH), lambda b, i: (0, 0, 0)),
                  full(n_cmp), full(n_cmp), full(seq, KV_W + LANES), full(seq, KV_W + LANES),
                  full(seq), full(seq), full(seq), full(seq),
                  pl.BlockSpec((1, B_WIDTH), lambda b, i: (0, 0))],
        out_specs=qspec(B_WIDTH),
        out_shape=jax.ShapeDtypeStruct((bsz, seq, B_WIDTH), _BF16),
        scratch_shapes=[pltpu.VMEM((B_KV, B_HPG * Q_BLOCK, LANES), _F32),
                        pltpu.VMEM((B_KV, B_HPG * Q_BLOCK, LANES), _F32)],
        compiler_params=pltpu.CompilerParams(dimension_semantics=("arbitrary", "arbitrary"),
                                             vmem_limit_bytes=VMEM_LIMIT),
        name="attn",
    )(q, gates, _gate_expand(), kc, vc, ks0, ks1, vs0, vs1, kw, vw, gob)


def _pack_halves(x):
    w = x.shape[1] // 2
    bits = lambda v: lax.bitcast_convert_type(v.astype(_BF16).astype(_F32), jnp.uint32)
    return lax.bitcast_convert_type(bits(x[:, :w]) | (bits(x[:, w:]) >> 16), jnp.int32)


def _unpack_halves(p):
    u = lax.bitcast_convert_type(p, jnp.uint32)
    return (lax.bitcast_convert_type(u & jnp.uint32(0xFFFF0000), _F32),
            lax.bitcast_convert_type(u << 16, _F32))


def _post_kernel(x_ref, oa_ref, ob_ref, woa_ref, wob_ref, gmoe_ref, r_ref, h1_ref, hn_ref, rt_ref, cnt_ref):
    h1 = x_ref[...] + _dot(oa_ref[...], woa_ref[...]) + _dot(ob_ref[...], wob_ref[...])
    h1_ref[...] = h1
    hn = _rms(h1, gmoe_ref[...])
    hn_ref[...] = _pack_halves(hn)

    hn_hi = hn.astype(_BF16)
    hn_lo = (hn - hn_hi.astype(_F32)).astype(_BF16)
    hi_both = _dot(hn_hi, r_ref[...])
    logits = hi_both[:, :GATE_PAD] + (_dot(hn_lo, r_ref[:, :GATE_PAD]) + hi_both[:, GATE_PAD:])
    lane = lax.broadcasted_iota(jnp.int32, logits.shape, 1)
    first_idx = lambda hit: jnp.min(jnp.where(hit, lane, LANES), axis=-1, keepdims=True)

    is_g = lane < N_GROUPS
    lg = jnp.where(is_g, logits, NEG)
    mg = jnp.max(lg, axis=-1, keepdims=True)
    sg = jnp.sum(jnp.where(is_g, jnp.exp(lg - mg), 0.0), axis=-1, keepdims=True)
    pg_top = 1.0 / sg
    g_sel = first_idx(is_g & (lg == mg))

    e_lo = ROUTER_OFF + g_sel * EXPERTS_PER_GROUP
    is_e = (lane >= e_lo) & (lane < e_lo + EXPERTS_PER_GROUP)
    le = jnp.where(is_e, logits, NEG)
    m1 = jnp.max(le, axis=-1, keepdims=True)
    se = jnp.sum(jnp.where(is_e, jnp.exp(le - m1), 0.0), axis=-1, keepdims=True)
    i1 = first_idx(is_e & (le == m1))
    le2 = jnp.where(lane == i1, NEG, le)
    m2 = jnp.max(le2, axis=-1, keepdims=True)
    i2 = first_idx(is_e & (lane != i1) & (le2 == m2))
    pe1 = 1.0 / se
    pe2 = jnp.exp(m2 - m1) / se
    denom = pe1 + pe2
    rt = (jnp.where(lane == i1 - ROUTER_OFF, 1.0, 0.0)
          + jnp.where(lane == i2 - ROUTER_OFF + N_EXPERTS, 1.0, 0.0)
          + jnp.where(lane == RT_W, pg_top * pe1 / denom, 0.0)
          + jnp.where(lane == RT_W + 1, pg_top * pe2 / denom, 0.0))
    rt_ref[...] = rt

    @pl.when(pl.program_id(0) == 0)
    def _():
        cnt_ref[...] = jnp.zeros_like(cnt_ref)

    cnt_ref[...] += jnp.sum(rt, axis=0, keepdims=True)


def _post_call(x2, oa, ob, woa, wob, gmoe, r_cat):
    n = x2.shape[0]
    tm = TM_POST
    row = lambda i: (i, 0)
    const2 = lambda i: (0, 0)
    return pl.pallas_call(
        _post_kernel,
        grid=(n // tm,),
        in_specs=[pl.BlockSpec((tm, D_MODEL), row), pl.BlockSpec((tm, A_WIDTH), row),
                  pl.BlockSpec((tm, B_WIDTH), row), pl.BlockSpec((A_WIDTH, D_MODEL), const2),
                  pl.BlockSpec((B_WIDTH, D_MODEL), const2), pl.BlockSpec((1, D_MODEL), const2),
                  pl.BlockSpec((D_MODEL, 2 * GATE_PAD), const2)],
        out_specs=[pl.BlockSpec((tm, D_MODEL), row), pl.BlockSpec((tm, D_MODEL // 2), row),
                   pl.BlockSpec((tm, GATE_PAD), row), pl.BlockSpec((8, LANES), const2)],
        out_shape=[jax.ShapeDtypeStruct((n, D_MODEL), _F32), jax.ShapeDtypeStruct((n, D_MODEL // 2), jnp.int32),
                   jax.ShapeDtypeStruct((n, GATE_PAD), _F32), jax.ShapeDtypeStruct((8, LANES), _F32)],
        compiler_params=pltpu.CompilerParams(dimension_semantics=("arbitrary",), vmem_limit_bytes=VMEM_LIMIT),
        name="post",
    )(x2, oa, ob, woa, wob, gmoe, r_cat)


def _route_kernel(rt_ref, cnt_ref, dest_ref, meta_ref, off_ref, run_ref):
    tm = rt_ref.shape[0]
    lane = lax.broadcasted_iota(jnp.int32, (1, LANES), 1)
    first = lane < N_EXPERTS
    onehot = jnp.where(lane < 2 * N_EXPERTS, rt_ref[...], 0.0)

    @pl.when(pl.program_id(0) == 0)
    def _():
        cnt = jnp.where(lane < 2 * N_EXPERTS, cnt_ref[...], 0.0)
        c1 = jnp.where(first, cnt, 0.0)
        tot = c1 + jnp.where(first, pltpu.roll(cnt, LANES - N_EXPERTS, 1), 0.0)
        tiles = jnp.floor((tot + (TR_GMM - 1)) * (1.0 / TR_GMM))
        e_row = lax.broadcasted_iota(jnp.int32, (LANES, LANES), 0)
        e_col = lax.broadcasted_iota(jnp.int32, (LANES, LANES), 1)
        before = jnp.where(e_row < e_col, 1.0, 0.0).astype(_BF16)
        base = _dot(tiles.astype(_BF16), before) * TR_GMM
        off_ref[...] = jnp.where(first, base, 0.0) + pltpu.roll(jnp.where(first, base + c1, 0.0), N_EXPERTS, 1)
        run_ref[...] = jnp.zeros_like(run_ref)
        meta_ref[...] = tiles

    r_io = lax.broadcasted_iota(jnp.int32, (tm, tm), 0)
    c_io = lax.broadcasted_iota(jnp.int32, (tm, tm), 1)
    earlier = jnp.where(c_io < r_io, 1.0, 0.0).astype(_BF16)
    rank = _dot(earlier, onehot.astype(_BF16)) + run_ref[0:1, :]
    slot = onehot * (rank + off_ref[0:1, :])
    d1 = jnp.sum(jnp.where(first, slot, 0.0), axis=-1, keepdims=True)
    d2 = jnp.sum(jnp.where(first, 0.0, slot), axis=-1, keepdims=True)
    dest_ref[...] = (jnp.where(lane == 0, d1, 0.0) + jnp.where(lane == 1, d2, 0.0)).astype(jnp.int32)
    run_ref[...] += jnp.sum(onehot, axis=0, keepdims=True)


def _route_call(rt, cnt):
    n = rt.shape[0]
    tm = TM_ROUTE
    return pl.pallas_call(
        _route_kernel,
        grid=(n // tm,),
        in_specs=[pl.BlockSpec((tm, GATE_PAD), lambda i: (i, 0)), pl.BlockSpec((8, LANES), lambda i: (0, 0))],
        out_specs=[pl.BlockSpec((tm, LANES), lambda i: (i, 0)), pl.BlockSpec((8, LANES), lambda i: (0, 0))],
        out_shape=[jax.ShapeDtypeStruct((n, LANES), jnp.int32), jax.ShapeDtypeStruct((8, LANES), _F32)],
        scratch_shapes=[pltpu.VMEM((8, LANES), _F32)] * 2,
        compiler_params=pltpu.CompilerParams(dimension_semantics=("arbitrary",)),
        name="route",
    )(rt, cnt)


def _sc_mesh():
    return plsc.VectorSubcoreMesh(core_axis_name="c", subcore_axis_name="s")


def _sc_worker(n_rows):
    per = n_rows // (SC_CORES * SC_SUBCORES)
    return (lax.axis_index("s") * SC_CORES + lax.axis_index("c")) * per, per


def _dispatch_call(xp, d1, d2, n_slots):
    n, w = xp.shape
    assert n % (SC_CORES * SC_SUBCORES * SC_CHUNK) == 0

    @functools.partial(
        pl.kernel, mesh=_sc_mesh(), out_type=jax.ShapeDtypeStruct((n_slots, w), xp.dtype),
        scratch_types=[pltpu.VMEM((SC_CHUNK,), jnp.int32), pltpu.VMEM((SC_CHUNK,), jnp.int32),
                       pltpu.VMEM((SC_CHUNK, w), xp.dtype), pltpu.SemaphoreType.DMA],
        name="dispatch")
    def k(x_hbm, d1_hbm, d2_hbm, xs_hbm, i1_v, i2_v, rows_v, sem):
        row0, per = _sc_worker(n)

        @pl.loop(0, per // SC_CHUNK)
        def _(j):
            src = pl.ds(row0 + j * SC_CHUNK, SC_CHUNK)
            pltpu.sync_copy(d1_hbm.at[src], i1_v)
            pltpu.sync_copy(d2_hbm.at[src], i2_v)
            pltpu.sync_copy(x_hbm.at[src], rows_v)
            first = pltpu.async_copy(rows_v, xs_hbm.at[i1_v], sem)
            second = pltpu.async_copy(rows_v, xs_hbm.at[i2_v], sem)
            first.wait()
            second.wait()

    return k(xp, d1, d2)


def _combine_call(ys, d1, d2):
    n = d1.shape[0]
    w = ys.shape[1]
    assert n % (SC_CORES * SC_SUBCORES * SC_CHUNK) == 0
    out = jax.ShapeDtypeStruct((n, w), ys.dtype)

    @functools.partial(
        pl.kernel, mesh=_sc_mesh(), out_type=(out, out),
        scratch_types=[pltpu.VMEM((SC_CHUNK,), jnp.int32), pltpu.VMEM((SC_CHUNK, w), ys.dtype),
                       pltpu.SemaphoreType.DMA],
        name="combine")
    def k(ys_hbm, d1_hbm, d2_hbm, y1_hbm, y2_hbm, i_v, rows_v, sem):
        row0, per = _sc_worker(n)

        @pl.loop(0, per // SC_CHUNK)
        def _(j):
            dst = pl.ds(row0 + j * SC_CHUNK, SC_CHUNK)
            for d_hbm, y_hbm in ((d1_hbm, y1_hbm), (d2_hbm, y2_hbm)):
                pltpu.sync_copy(d_hbm.at[dst], i_v)
                pltpu.async_copy(ys_hbm.at[i_v], rows_v, sem).wait()
                pltpu.sync_copy(rows_v, y_hbm.at[dst])

    return k(ys, d1, d2)


def _gmm_kernel(te_ref, nu_ref, xs_ref, wg_ref, wu_ref, wd_ref, ys_ref):
    half = D_MODEL // 2

    @pl.when(pl.program_id(0) < nu_ref[0])
    def _():
        for r in range(TR_GMM // SUB_MOE):
            rows = slice(r * SUB_MOE, (r + 1) * SUB_MOE)
            a, b = _unpack_halves(xs_ref[rows, :])
            a, b = a.astype(_BF16), b.astype(_BF16)
            gate = _dot(a, wg_ref[0, :half, :]) + _dot(b, wg_ref[0, half:, :])
            up = _dot(a, wu_ref[0, :half, :]) + _dot(b, wu_ref[0, half:, :])
            hid = gate * jax.nn.sigmoid(gate) * up
            ys_ref[rows, :] = _pack_halves(_dot(hid.astype(_BF16), wd_ref[0]))


def _gmm_call(tile_expert, n_used, xs, w_gate, w_up, w_down):
    n_slots, w = xs.shape
    rows = lambda t, te, nu: (jnp.minimum(t, nu[0] - 1), 0)
    expert = lambda t, te, nu: (te[t], 0, 0)
    return pl.pallas_call(
        _gmm_kernel,
        grid_spec=pltpu.PrefetchScalarGridSpec(
            num_scalar_prefetch=2, grid=(n_slots // TR_GMM,),
            in_specs=[pl.BlockSpec((TR_GMM, w), rows),
                      pl.BlockSpec((1, D_MODEL, D_FF_EXPERT), expert),
                      pl.BlockSpec((1, D_MODEL, D_FF_EXPERT), expert),
                      pl.BlockSpec((1, D_FF_EXPERT, D_MODEL), expert)],
            out_specs=pl.BlockSpec((TR_GMM, w), rows)),
        out_shape=jax.ShapeDtypeStruct((n_slots, w), xs.dtype),
        compiler_params=pltpu.CompilerParams(dimension_semantics=("arbitrary",), vmem_limit_bytes=VMEM_LIMIT),
        name="gmm",
    )(tile_expert, n_used, xs, w_gate, w_up, w_down)


def _final_kernel(h1_ref, y1_ref, y2_ref, rt_ref, p_ref, gple_ref, wpg_ref, wpp_ref, gfin_ref, o_ref):
    for r in range(h1_ref.shape[0] // SUB_MOE):
        rows = slice(r * SUB_MOE, (r + 1) * SUB_MOE)
        rt = rt_ref[rows, :]
        w1, w2 = rt[:, RT_W:RT_W + 1], rt[:, RT_W + 1:RT_W + 2]
        a1, b1 = _unpack_halves(y1_ref[rows, :])
        a2, b2 = _unpack_halves(y2_ref[rows, :])
        h2 = h1_ref[rows, :] + jnp.concatenate([w1 * a1 + w2 * a2, w1 * b1 + w2 * b2], axis=1)
        gate = jax.nn.sigmoid(_dot(_rms(h2, gple_ref[...]).astype(_BF16), wpg_ref[...]))
        h3 = h2 + _dot(p_ref[rows, :].astype(_BF16), wpp_ref[...]) * gate
        o_ref[rows, :] = _rms(h3, gfin_ref[...])


def _final_call(h1, y1, y2, rt, p2, gple, wpg, wpp, gfin):
    n = h1.shape[0]
    tm = TM_POST
    row = lambda i: (i, 0)
    const2 = lambda i: (0, 0)
    return pl.pallas_call(
        _final_kernel,
        grid=(n // tm,),
        in_specs=[pl.BlockSpec((tm, D_MODEL), row), pl.BlockSpec((tm, D_MODEL // 2), row),
                  pl.BlockSpec((tm, D_MODEL // 2), row), pl.BlockSpec((tm, GATE_PAD), row),
                  pl.BlockSpec((tm, D_PLE), row), pl.BlockSpec((1, D_MODEL), const2),
                  pl.BlockSpec((D_MODEL, D_MODEL), const2), pl.BlockSpec((D_PLE, D_MODEL), const2),
                  pl.BlockSpec((1, D_MODEL), const2)],
        out_specs=pl.BlockSpec((tm, D_MODEL), row),
        out_shape=jax.ShapeDtypeStruct((n, D_MODEL), _F32),
        compiler_params=pltpu.CompilerParams(dimension_semantics=("arbitrary",), vmem_limit_bytes=VMEM_LIMIT),
        name="final",
    )(h1, y1, y2, rt, p2, gple, wpg, wpp, gfin)


def _rope_tables(seq):
    half = HEAD_DIM // 2
    inv = 1.0 / (ROPE_THETA ** (jnp.arange(half, dtype=_F32) / half))
    ang = jnp.arange(seq, dtype=_F32)[:, None] * inv[None, :]
    cos, sin = jnp.cos(ang), jnp.sin(ang)
    reps = LANES // HEAD_DIM
    cos_t = jnp.tile(jnp.concatenate([cos, cos], axis=1), (1, reps))
    sin_t = jnp.tile(jnp.concatenate([-sin, sin], axis=1), (1, reps))
    return cos_t, sin_t


def _layer(h, p_i, norm_mix, w_in, gmlp_v_norm, gmlp_w_s, gmlp_b_s,
           cmp_pe_k, cmp_w1_k, cmp_w2_k, cmp_pe_v, cmp_w1_v, cmp_w2_v,
           out_norm_a, out_norm_b, w_o, norm_moe, router_group, router_expert,
           moe_w_gate, moe_w_up, moe_w_down, norm_ple, w_ple_proj, w_ple_gate, norm_final):
    bsz, seq, _ = h.shape
    n = bsz * seq
    x2 = h.reshape(n, D_MODEL)
    row = lambda v: v.reshape(1, -1).astype(_F32)

    w_q = w_in[:, OFF_Q:OFF_KV][:, _PERM_CH]
    w_gate = jnp.pad(w_in[:, OFF_GATE:D_IN], ((0, 0), (0, GATE_PAD - N_GATES)))
    w_all = jnp.concatenate([w_in[:, :OFF_Q], w_q, w_in[:, OFF_KV:OFF_GATE], w_gate], axis=1).astype(_BF16)
    cos_t, sin_t = _rope_tables(seq)
    ws_pairs = gmlp_w_s.reshape(A_HEADS // 2, 2, CHUNK, CHUNK).transpose(0, 2, 1, 3).reshape(
        A_HEADS // 2, CHUNK, 2 * CHUNK)
    bs_exp = jnp.repeat(gmlp_b_s.T, HEAD_DIM, axis=1)

    oa, q, k_cmp, v_cmp, ks0, ks1, vs0, vs1, k_win, v_win, gates = _proj_call(
        x2, row(norm_mix), w_all, cos_t, sin_t, row(gmlp_v_norm), ws_pairs, bs_exp, row(out_norm_a), seq)

    kc, vc = _compress_call(k_cmp, v_cmp, _compress_weights(cmp_w1_k, cmp_w2_k, cmp_pe_k),
                            _compress_weights(cmp_w1_v, cmp_w2_v, cmp_pe_v), bsz, seq)

    b3 = lambda a: a.reshape(bsz, seq, a.shape[-1])
    ob = _attn_call(b3(q), b3(gates), kc, vc, b3(ks0), b3(ks1), b3(vs0), b3(vs1), b3(k_win), b3(v_win),
                    row(out_norm_b[_PERM_CH]), bsz, seq)

    r_cat = jnp.pad(jnp.concatenate([router_group, router_expert], axis=1),
                    ((0, 0), (0, GATE_PAD - N_GROUPS - N_EXPERTS)))
    r_hi = r_cat.astype(_BF16)
    r_cat = jnp.concatenate([r_hi, (r_cat - r_hi.astype(_F32)).astype(_BF16)], axis=1)
    h1, xp, rt, cnt = _post_call(x2, oa, ob.reshape(n, B_WIDTH), w_o[:A_WIDTH].astype(_BF16),
                            w_o[A_WIDTH:][_PERM_CH].astype(_BF16), row(norm_moe), r_cat)

    dest, meta = _route_call(rt, cnt)
    d1, d2 = dest[:, 0], dest[:, 1]
    n_tiles = 2 * n // TR_GMM + N_EXPERTS
    ends = jnp.cumsum(meta[0, :N_EXPERTS].astype(jnp.int32))
    tile_expert = jnp.minimum(jnp.sum(ends[None, :] <= jnp.arange(n_tiles)[:, None], axis=1),
                              N_EXPERTS - 1).astype(jnp.int32)
    xs = _dispatch_call(xp, d1, d2, n_tiles * TR_GMM)
    ys = _gmm_call(tile_expert, ends[-1:], xs, moe_w_gate.astype(_BF16), moe_w_up.astype(_BF16),
                   moe_w_down.astype(_BF16))
    y1, y2 = _combine_call(ys, d1, d2)
    out = _final_call(h1, y1, y2, rt, p_i.reshape(n, D_PLE), row(norm_ple),
                      w_ple_gate.astype(_BF16), w_ple_proj.astype(_BF16), row(norm_final))
    return out.reshape(bsz, seq, D_MODEL)


def kernel(x, p, norm_mix, w_in, gmlp_v_norm, gmlp_w_s, gmlp_b_s, cmp_pe_k, cmp_w1_k, cmp_w2_k,
           cmp_pe_v, cmp_w1_v, cmp_w2_v, out_norm_a, out_norm_b, w_o, norm_moe, router_group,
           router_expert, moe_w_gate, moe_w_up, moe_w_down, norm_ple, w_ple_proj, w_ple_gate, norm_final):
    assert p.shape[0] == 1, "single-layer trunk"
    assert x.shape[1] % SEL_CK == 0 and x.shape[1] >= WIN + Q_BLOCK
    assert (x.shape[0] * x.shape[1]) % (SC_CORES * SC_SUBCORES * SC_CHUNK) == 0
    return _layer(x, p[0], norm_mix[0], w_in[0], gmlp_v_norm[0], gmlp_w_s[0], gmlp_b_s[0],
                  cmp_pe_k[0], cmp_w1_k[0], cmp_w2_k[0], cmp_pe_v[0], cmp_w1_v[0], cmp_w2_v[0],
                  out_norm_a[0], out_norm_b[0], w_o[0], norm_moe[0], router_group[0], router_expert[0],
                  moe_w_gate[0], moe_w_up[0], moe_w_down[0], norm_ple[0], w_ple_proj[0], w_ple_gate[0],
                  norm_final)
```

```python
import functools

import numpy as np
import jax
import jax.numpy as jnp
from jax import lax
from jax.experimental import pallas as pl
from jax.experimental.pallas import tpu as pltpu
from jax.experimental.pallas import tpu_sc as plsc

D_MODEL = 1024
HEAD_DIM = 64
A_HEADS = 8
A_WIDTH = A_HEADS * HEAD_DIM
B_HEADS = 8
B_WIDTH = B_HEADS * HEAD_DIM
B_KV = 2
B_HPG = B_HEADS // B_KV
KV_W = B_KV * HEAD_DIM
N_GATES = B_HEADS * 3
CHUNK = 128
L_CMP = 32
STRIDE_CMP = 16
CMP_HIDDEN = 256
L_SEL = 64
N_SEL = 16
WIN = 512
WIN_Q = 256
Q_BLOCK = 512
ROPE_THETA = 10000.0
N_GROUPS = 4
EXPERTS_PER_GROUP = 4
N_EXPERTS = N_GROUPS * EXPERTS_PER_GROUP
D_FF_EXPERT = 512
D_PLE = 256
EPS = 1e-6
LOG2E = 1.4426950408889634
NEG = -1e30
FORCE = 1e6

OFF_Q = 2 * A_WIDTH
OFF_KV = OFF_Q + B_WIDTH
OFF_GATE = OFF_KV + 6 * KV_W
D_IN = OFF_GATE + N_GATES

LANES = 128
GATE_PAD = LANES
ROUTER_OFF = N_GROUPS
W_ALL = OFF_GATE + GATE_PAD

TM_PROJ = 1024
TM_POST = 1024
TM_ROUTE = 512
TR_GMM = 512
SUB_MOE = 512
RT_W = 2 * N_EXPERTS
SC_CORES = 2
SC_SUBCORES = 16
SC_CHUNK = 128
SEL_CK = 512
VMEM_LIMIT = 56 * 1024 * 1024
VMEM_LIMIT_ATTN = 60 * 1024 * 1024

_PERM_HEADS = [0, 4, 1, 5, 2, 6, 3, 7]


def _perm_heads(a, axis):
    return jnp.concatenate([lax.slice_in_dim(a, h * HEAD_DIM, (h + 1) * HEAD_DIM, axis=axis) for h in _PERM_HEADS],
                           axis=axis)

_F32 = jnp.float32
_BF16 = jnp.bfloat16


def _dot(a, b):
    return jnp.dot(a, b, preferred_element_type=_F32)


def _dot_nt(a, b):
    return lax.dot_general(a, b, (((1,), (1,)), ((), ())), preferred_element_type=_F32)


def _rms(x, g):
    return x * lax.rsqrt(jnp.mean(x * x, axis=-1, keepdims=True) + EPS) * g


def _gelu(x):
    return 0.5 * x * (1.0 + jnp.tanh(0.7978845608028654 * (x + 0.044715 * (x * x * x))))


def _rope_tile(x, cos, sin_signed):
    lane = lax.broadcasted_iota(jnp.int32, x.shape, 1)
    first_half = (lane % HEAD_DIM) < (HEAD_DIM // 2)
    rot = jnp.where(first_half, pltpu.roll(x, LANES - HEAD_DIM // 2, 1), pltpu.roll(x, HEAD_DIM // 2, 1))
    return x * cos + rot * sin_signed


def _proj_kernel(x_ref, gmix_ref, w_ref, cos_ref, sin_ref, gv_ref, ws_ref, bs_ref, goa_ref,
                 oa_ref, q_ref, kc_ref, vc_ref, ks0_ref, ks1_ref, vs0_ref, vs1_ref, kw_ref, vw_ref, gate_ref,
                 *, seq):
    tm = x_ref.shape[0]
    hn = _rms(x_ref[...], gmix_ref[...]).astype(_BF16)
    cos = cos_ref[...]
    sin = sin_ref[...]

    zq = _dot(hn, w_ref[:, OFF_Q:OFF_KV])
    scale = HEAD_DIM ** -0.5 * LOG2E
    for j in range(B_WIDTH // LANES):
        blk = _rope_tile(zq[:, j * LANES:(j + 1) * LANES], cos, sin) * scale
        q_ref[:, j * LANES:(j + 1) * LANES] = blk.astype(q_ref.dtype)

    zkv = _dot(hn, w_ref[:, OFF_KV:OFF_GATE])
    kv = []
    for j in range(6):
        blk = zkv[:, j * KV_W:(j + 1) * KV_W]
        kv.append(_rope_tile(blk, cos, sin) if j % 2 == 0 else blk)
    k_cmp, v_cmp, k_slc, v_slc, k_win, v_win = kv
    kc_ref[...] = k_cmp.astype(kc_ref.dtype)
    vc_ref[...] = v_cmp.astype(vc_ref.dtype)
    kw_ref[...] = k_win.astype(kw_ref.dtype)
    vw_ref[...] = v_win.astype(vw_ref.dtype)

    row = lax.broadcasted_iota(jnp.int32, (tm, LANES), 0)
    lane = lax.broadcasted_iota(jnp.int32, (tm, LANES), 1)
    key_block = ((pl.program_id(0) % (seq // tm)) * tm + row) // L_SEL
    onehot = jnp.where(lane == key_block, 1.0, 0.0).astype(ks0_ref.dtype)
    for g, (ks_ref, vs_ref) in enumerate(((ks0_ref, vs0_ref), (ks1_ref, vs1_ref))):
        own = (lane // HEAD_DIM) == g
        ks_ref[:, :LANES] = jnp.where(own, k_slc, 0.0).astype(ks_ref.dtype)
        ks_ref[:, LANES:] = onehot
        vs_ref[...] = jnp.where(own, v_slc, 1.0).astype(vs_ref.dtype)

    zg = _dot(hn, w_ref[:, OFF_GATE:W_ALL])
    gate_ref[...] = jax.nn.sigmoid(zg)

    zu = _gelu(_dot(hn, w_ref[:, 0:A_WIDTH]))
    zv = _gelu(_dot(hn, w_ref[:, A_WIDTH:2 * A_WIDTH]))
    vn = _rms(zv, gv_ref[...]).astype(_BF16)

    t_io = lax.broadcasted_iota(jnp.int32, (CHUNK, 2 * CHUNK), 0)
    s_io = lax.broadcasted_iota(jnp.int32, (CHUNK, 2 * CHUNK), 1) % CHUNK
    causal = s_io <= t_io
    lane = lax.broadcasted_iota(jnp.int32, (CHUNK, LANES), 1)
    lo = lane < HEAD_DIM
    bs = bs_ref[...]
    chunks = []
    for c in range(tm // CHUNK):
        tiles = []
        for pr in range(A_HEADS // 2):
            wcat = jnp.where(causal, ws_ref[pr], 0.0).astype(_BF16)
            vblk = vn[c * CHUNK:(c + 1) * CHUNK, pr * LANES:(pr + 1) * LANES]
            zero = jnp.zeros_like(vblk)
            rhs = jnp.concatenate([jnp.where(lo, vblk, zero), jnp.where(lo, zero, vblk)], axis=0)
            tiles.append(_dot(wcat, rhs))
        chunks.append(jnp.concatenate(tiles, axis=1) + bs)
    mixed = jnp.concatenate(chunks, axis=0)
    oa = zu * mixed
    oa_ref[...] = _rms(oa, goa_ref[...]).astype(oa_ref.dtype)


def _proj_call(x2, gmix, w_all, cos_t, sin_t, gv, ws_pairs, bs_exp, goa, seq):
    n = x2.shape[0]
    tm = TM_PROJ
    n_t = seq // tm
    row = lambda i: (i, 0)
    const2 = lambda i: (0, 0)
    pos = lambda i: (i % n_t, 0)
    out_shapes = [jax.ShapeDtypeStruct((n, A_WIDTH), _BF16), jax.ShapeDtypeStruct((n, B_WIDTH), _BF16)]
    kv_widths = [KV_W, KV_W, KV_W + LANES, KV_W + LANES, KV_W, KV_W, KV_W, KV_W]
    out_shapes += [jax.ShapeDtypeStruct((n, w), _BF16) for w in kv_widths]
    out_shapes += [jax.ShapeDtypeStruct((n, GATE_PAD), _F32)]
    out_specs = [pl.BlockSpec((tm, A_WIDTH), row), pl.BlockSpec((tm, B_WIDTH), row)]
    out_specs += [pl.BlockSpec((tm, w), row) for w in kv_widths]
    out_specs += [pl.BlockSpec((tm, GATE_PAD), row)]
    return pl.pallas_call(
        functools.partial(_proj_kernel, seq=seq),
        grid=(n // tm,),
        in_specs=[
            pl.BlockSpec((tm, D_MODEL), row),
            pl.BlockSpec((1, D_MODEL), const2),
            pl.BlockSpec((D_MODEL, W_ALL), const2),
            pl.BlockSpec((tm, LANES), pos),
            pl.BlockSpec((tm, LANES), pos),
            pl.BlockSpec((1, A_WIDTH), const2),
            pl.BlockSpec((A_HEADS // 2, CHUNK, 2 * CHUNK), lambda i: (0, 0, 0)),
            pl.BlockSpec((CHUNK, A_WIDTH), const2),
            pl.BlockSpec((1, A_WIDTH), const2),
        ],
        out_specs=out_specs,
        out_shape=out_shapes,
        compiler_params=pltpu.CompilerParams(dimension_semantics=("arbitrary",), vmem_limit_bytes=VMEM_LIMIT),
        name="proj",
    )(x2, gmix, w_all, cos_t, sin_t, gv, ws_pairs, bs_exp, goa)


def _compress_kernel(rk_ref, rv_ref, pek_ref, pev_ref, w1k_ref, w1v_ref,
                     tk_ref, bk_ref, w2k_ref, tv_ref, bv_ref, w2v_ref, kc_ref, vc_ref):
    def one(r_ref, pe_ref, w1_ref, top_ref, bot_ref, w2_ref, o_ref):
        r = r_ref[0]
        nr = r.shape[0]
        a = _dot(r, top_ref[...])
        b = _dot(r, bot_ref[...])
        pe_h = _dot(pe_ref[...], w1_ref[...])
        pe2 = jnp.concatenate([pe_h[0:1], pe_h[0:1]], axis=1)
        hid = a + pltpu.roll(b, nr - 1, 0) + pe2
        o_ref[0] = _dot(_gelu(hid).astype(_BF16), w2_ref[...]).astype(o_ref.dtype)

    one(rk_ref, pek_ref, w1k_ref, tk_ref, bk_ref, w2k_ref, kc_ref)
    one(rv_ref, pev_ref, w1v_ref, tv_ref, bv_ref, w2v_ref, vc_ref)


def _compress_weights(w1, w2, pe):
    half = L_CMP // 2
    w1r = w1.reshape(L_CMP, HEAD_DIM, CMP_HIDDEN)
    eye = jnp.eye(B_KV, dtype=w1.dtype)
    place = lambda part: jnp.einsum('ldj,gh->lgdhj', part, eye).reshape(half * KV_W, B_KV * CMP_HIDDEN)
    top = place(w1r[:half]).astype(_BF16)
    bot = place(w1r[half:]).astype(_BF16)
    w2bd = jnp.einsum('jd,gh->gjhd', w2, eye).reshape(B_KV * CMP_HIDDEN, KV_W).astype(_BF16)
    pe8 = jnp.broadcast_to(pe.reshape(1, L_CMP * HEAD_DIM), (8, L_CMP * HEAD_DIM)).astype(_BF16)
    return pe8, w1.astype(_BF16), top, bot, w2bd


def _compress_call(k_cmp, v_cmp, wk, wv, bsz, seq):
    nr = seq // STRIDE_CMP
    rk = k_cmp.reshape(bsz, nr, STRIDE_CMP * KV_W)
    rv = v_cmp.reshape(bsz, nr, STRIDE_CMP * KV_W)
    pek, w1k, tk, bk, w2k = wk
    pev, w1v, tv, bv, w2v = wv
    rspec = pl.BlockSpec((1, nr, STRIDE_CMP * KV_W), lambda b: (b, 0, 0))
    full = lambda a: pl.BlockSpec(a.shape, lambda b: (0,) * a.ndim)
    ospec = pl.BlockSpec((1, nr, KV_W), lambda b: (b, 0, 0))
    return pl.pallas_call(
        _compress_kernel,
        grid=(bsz,),
        in_specs=[rspec, rspec, full(pek), full(pev), full(w1k), full(w1v),
                  full(tk), full(bk), full(w2k), full(tv), full(bv), full(w2v)],
        out_specs=[ospec, ospec],
        out_shape=[jax.ShapeDtypeStruct((bsz, nr, KV_W), _BF16)] * 2,
        compiler_params=pltpu.CompilerParams(dimension_semantics=("arbitrary",), vmem_limit_bytes=VMEM_LIMIT),
        name="compress",
    )(rk, rv, pek, pev, w1k, w1v, tk, bk, w2k, tv, bv, w2v)


def _topk_rows_mask(sc_t, k):
    n_rows = sc_t.shape[0]
    row = lax.broadcasted_iota(jnp.int32, sc_t.shape, 0)
    taken = -3e38
    for _ in range(k):
        m = jnp.max(sc_t, axis=0, keepdims=True)
        idx = jnp.min(jnp.where(sc_t == m, row, n_rows), axis=0, keepdims=True)
        sc_t = jnp.where(row == idx, taken, sc_t)
    return jnp.where(sc_t == taken, 1.0, 0.0)


def _attn_kernel(q_ref, gate_ref, gexp_ref, kc_ref, vc_ref, ks0_ref, ks1_ref, vs0_ref, vs1_ref,
                 kw_ref, vw_ref, gob_ref, *rest, seq, n_cast):
    cast_in, o_ref, cast_out = rest[:n_cast], rest[n_cast], rest[n_cast + 1:2 * n_cast + 1]
    m_scr, acc_scr = rest[2 * n_cast + 1:]
    for src, dst in zip(cast_in, cast_out):
        dst[...] = src[...].astype(dst.dtype)

    qb = pl.program_id(1)
    t0 = qb * Q_BLOCK
    n_cmp = kc_ref.shape[1]
    n_sb = seq // L_SEL
    k_top = min(N_SEL, n_sb)
    rows = B_HPG * Q_BLOCK
    ks_refs, vs_refs = (ks0_ref, ks1_ref), (vs0_ref, vs1_ref)

    lane_q = lax.broadcasted_iota(jnp.int32, (Q_BLOCK, LANES), 1)
    lo = lane_q < HEAD_DIM
    t_col = t0 + lax.broadcasted_iota(jnp.int32, (Q_BLOCK, 1), 0)

    def per_head(x):
        return x.reshape(B_HPG, Q_BLOCK, x.shape[-1])

    def add_bias(s, bias):
        return (per_head(s) + bias[None]).reshape(rows, s.shape[-1])

    qs = []
    for g in range(B_KV):
        own = (lane_q // HEAD_DIM) == g
        qs.append(jnp.concatenate(
            [jnp.where(own, q_ref[0, :, j * LANES:(j + 1) * LANES], jnp.zeros((), q_ref.dtype))
             for j in range(B_HPG)], axis=0))

    c_lane = lax.broadcasted_iota(jnp.int32, (Q_BLOCK, n_cmp), 1)
    bias_c = jnp.where(c_lane * STRIDE_CMP + (L_CMP - 1) <= t_col, 0.0, NEG)
    has_c = (t_col >= L_CMP - 1).astype(_F32)
    c_row = lax.broadcasted_iota(jnp.int32, (n_cmp, n_sb), 0) * STRIDE_CMP
    s_col = lax.broadcasted_iota(jnp.int32, (n_cmp, n_sb), 1) * L_SEL
    overlap = jnp.where((c_row < s_col + L_SEL) & (c_row + L_CMP > s_col), 1.0, 0.0).astype(_BF16)
    blk = lax.broadcasted_iota(jnp.int32, (Q_BLOCK, n_sb), 1)
    cur = t_col // L_SEL
    forced = (blk == 0) | (blk == cur) | (blk == cur - 1)
    valid = blk * L_SEL <= t_col

    o_c, sel_bias = [], []
    for g in range(B_KV):
        s_c = add_bias(_dot_nt(qs[g], kc_ref[0]), bias_c)
        e_c = jnp.exp2(s_c - jnp.max(s_c, axis=-1, keepdims=True))
        inv = per_head(1.0 / jnp.maximum(jnp.sum(e_c, axis=-1, keepdims=True), 1e-30)) * has_c[None]
        p_c = (per_head(e_c) * inv).reshape(rows, n_cmp)
        o_c.append(_dot(p_c.astype(_BF16), vc_ref[0]))

        p_sum = jnp.sum(per_head(p_c), axis=0)
        p_hi = p_sum.astype(_BF16)
        p_lo = (p_sum - p_hi.astype(_F32)).astype(_BF16)
        imp = _dot(p_hi, overlap) + _dot(p_lo, overlap)
        score = jnp.where(valid & jnp.logical_not(forced), imp, -FORCE)
        chosen = forced | (_topk_rows_mask(score.T, k_top - 3).T > 0.5)
        sb = jnp.where(chosen & valid, 0.0, NEG)
        if n_sb < LANES:
            sb = jnp.concatenate([sb, jnp.full((Q_BLOCK, LANES - n_sb), NEG, _F32)], axis=1)
        sel_bias.append(sb)

    w_len = WIN + WIN_Q
    lane_w = lax.broadcasted_iota(jnp.int32, (w_len, LANES), 1)
    acc_w_sub = [[] for _ in range(B_KV)]
    for hh in range(Q_BLOCK // WIN_Q):
        w_start = pl.multiple_of(jnp.maximum(t0 + hh * WIN_Q - WIN, 0), WIN_Q)
        t_sub = t_col[hh * WIN_Q:(hh + 1) * WIN_Q]
        diff_w = t_sub - (w_start + lax.broadcasted_iota(jnp.int32, (WIN_Q, w_len), 1))
        bias_w = jnp.where((diff_w >= 0) & (diff_w < WIN), 0.0, NEG)
        kw = kw_ref[0, pl.ds(w_start, w_len), :]
        vw = vw_ref[0, pl.ds(w_start, w_len), :]
        for g in range(B_KV):
            q_sub = per_head(qs[g])[:, hh * WIN_Q:(hh + 1) * WIN_Q, :].reshape(B_HPG * WIN_Q, LANES)
            s_w = (_dot_nt(q_sub, kw).reshape(B_HPG, WIN_Q, w_len) + bias_w[None]).reshape(B_HPG * WIN_Q, w_len)
            e_w = jnp.exp2(s_w - jnp.max(s_w, axis=-1, keepdims=True)).astype(_BF16)
            vw_aug = jnp.where((lane_w // HEAD_DIM) == g, vw, jnp.ones((), vw.dtype))
            acc_w_sub[g].append(_dot(e_w, vw_aug).reshape(B_HPG, WIN_Q, LANES))
    acc_w = [jnp.concatenate(acc_w_sub[g], axis=1).reshape(rows, LANES) for g in range(B_KV)]

    n_ck = (t0 + Q_BLOCK + SEL_CK - 1) // SEL_CK
    key_lane = lax.broadcasted_iota(jnp.int32, (Q_BLOCK, SEL_CK), 1)
    bias_diag = jnp.where((n_ck - 1) * SEL_CK + key_lane <= t_col, 0.0, NEG)
    q_wide = [jnp.concatenate([qs[g], jnp.concatenate([sel_bias[g].astype(_BF16)] * B_HPG, axis=0)], axis=1)
              for g in range(B_KV)]

    m_scr[...] = jnp.full(m_scr.shape, NEG, _F32)
    acc_scr[...] = jnp.zeros(acc_scr.shape, _F32)

    def sel_chunk(ci, diag):
        k0 = pl.multiple_of(ci * SEL_CK, SEL_CK)
        for g in range(B_KV):
            m = m_scr[g]
            s = _dot_nt(q_wide[g], ks_refs[g][0, pl.ds(k0, SEL_CK), :])
            if diag:
                s = add_bias(s, bias_diag)
            m_new = jnp.maximum(m, jnp.max(s, axis=-1, keepdims=True))
            p = jnp.exp2(s - jnp.concatenate([m_new] * (SEL_CK // LANES), axis=1)).astype(_BF16)
            acc_scr[g] = jnp.exp2(m - m_new) * acc_scr[g] + _dot(p, vs_refs[g][0, pl.ds(k0, SEL_CK), :])
            m_scr[g] = m_new

    @pl.loop(0, n_ck - 1)
    def _(ci):
        sel_chunk(ci, False)

    sel_chunk(n_ck - 1, True)
    acc_s = [acc_scr[g] for g in range(B_KV)]

    def numer(acc):
        return jnp.concatenate([jnp.where(lo, acc[0][j * Q_BLOCK:(j + 1) * Q_BLOCK],
                                          acc[1][j * Q_BLOCK:(j + 1) * Q_BLOCK]) for j in range(B_HPG)], axis=1)

    def denom(acc):
        return jnp.concatenate([pltpu.roll(jnp.where(lo, acc[1][j * Q_BLOCK:(j + 1) * Q_BLOCK],
                                                     acc[0][j * Q_BLOCK:(j + 1) * Q_BLOCK]), HEAD_DIM, 1)
                                for j in range(B_HPG)], axis=1)

    gates = gate_ref[0]
    g_hi = gates.astype(_BF16)
    g_lo = (gates - g_hi.astype(_F32)).astype(_BF16)
    gate_of = lambda r: _dot(g_hi, gexp_ref[r]) + _dot(g_lo, gexp_ref[r])
    ob = (gate_of(0) * numer(o_c)
          + gate_of(1) * numer(acc_s) * (1.0 / jnp.maximum(denom(acc_s), 1e-30))
          + gate_of(2) * numer(acc_w) * (1.0 / jnp.maximum(denom(acc_w), 1e-30)))
    o_ref[0] = _rms(ob, gob_ref[...]).astype(o_ref.dtype)


def _gate_expand():
    x = np.zeros((3, GATE_PAD, B_WIDTH), np.float32)
    for slot, h in enumerate(_PERM_HEADS):
        for r in range(3):
            x[r, 3 * h + r, slot * HEAD_DIM:(slot + 1) * HEAD_DIM] = 1.0
    return jnp.asarray(x, _BF16)


def _attn_call(q, gates, kc, vc, ks0, ks1, vs0, vs1, kw, vw, gob, to_cast, bsz, seq):
    assert seq // L_SEL <= LANES
    n_cmp = kc.shape[1]
    n_q = seq // Q_BLOCK
    steps = bsz * n_q
    qspec = lambda w: pl.BlockSpec((1, Q_BLOCK, w), lambda b, i: (b, i, 0))
    full = lambda r, w=KV_W: pl.BlockSpec((1, r, w), lambda b, i: (b, 0, 0), pipeline_mode=pl.Buffered(1))
    sliced = [a.reshape(steps, a.size // (steps * a.shape[-1]), a.shape[-1]) for a in to_cast]
    cast_specs = [pl.BlockSpec((1,) + a.shape[1:], lambda b, i: (b * n_q + i, 0, 0)) for a in sliced]
    outs = pl.pallas_call(
        functools.partial(_attn_kernel, seq=seq, n_cast=len(sliced)),
        grid=(bsz, n_q),
        in_specs=[qspec(B_WIDTH), qspec(GATE_PAD),
                  pl.BlockSpec((3, GATE_PAD, B_WIDTH), lambda b, i: (0, 0, 0)),
                  full(n_cmp), full(n_cmp), full(seq, KV_W + LANES), full(seq, KV_W + LANES),
                  full(seq), full(seq), full(seq), full(seq),
                  pl.BlockSpec((1, B_WIDTH), lambda b, i: (0, 0))] + cast_specs,
        out_specs=[qspec(B_WIDTH)] + cast_specs,
        out_shape=[jax.ShapeDtypeStruct((bsz, seq, B_WIDTH), _BF16)]
        + [jax.ShapeDtypeStruct(a.shape, _BF16) for a in sliced],
        scratch_shapes=[pltpu.VMEM((B_KV, B_HPG * Q_BLOCK, LANES), _F32),
                        pltpu.VMEM((B_KV, B_HPG * Q_BLOCK, LANES), _F32)],
        compiler_params=pltpu.CompilerParams(dimension_semantics=("arbitrary", "arbitrary"),
                                             vmem_limit_bytes=VMEM_LIMIT_ATTN),
        name="attn",
    )(q, gates, _gate_expand(), kc, vc, ks0, ks1, vs0, vs1, kw, vw, gob, *sliced)
    return outs[0], [o.reshape(a.shape) for o, a in zip(outs[1:], to_cast)]


def _pack_halves(x):
    w = x.shape[1] // 2
    bits = lambda v: lax.bitcast_convert_type(v.astype(_BF16).astype(_F32), jnp.uint32)
    return lax.bitcast_convert_type(bits(x[:, :w]) | (bits(x[:, w:]) >> 16), jnp.int32)


def _unpack_halves(p):
    u = lax.bitcast_convert_type(p, jnp.uint32)
    return (lax.bitcast_convert_type(u & jnp.uint32(0xFFFF0000), _F32),
            lax.bitcast_convert_type(u << 16, _F32))


def _post_kernel(x_ref, oa_ref, ob_ref, woa_ref, wob_ref, gmoe_ref, r_ref, h1_ref, hn_ref, rt_ref, cnt_ref):
    h1 = x_ref[...] + _dot(oa_ref[...], woa_ref[...]) + _dot(ob_ref[...], wob_ref[...])
    h1_ref[...] = h1
    hn = _rms(h1, gmoe_ref[...])
    hn_ref[...] = _pack_halves(hn)

    hn_hi = hn.astype(_BF16)
    hn_lo = (hn - hn_hi.astype(_F32)).astype(_BF16)
    hi_both = _dot(hn_hi, r_ref[...])
    logits = hi_both[:, :GATE_PAD] + (_dot(hn_lo, r_ref[:, :GATE_PAD]) + hi_both[:, GATE_PAD:])
    lane = lax.broadcasted_iota(jnp.int32, logits.shape, 1)
    first_idx = lambda hit: jnp.min(jnp.where(hit, lane, LANES), axis=-1, keepdims=True)

    is_g = lane < N_GROUPS
    lg = jnp.where(is_g, logits, NEG)
    mg = jnp.max(lg, axis=-1, keepdims=True)
    sg = jnp.sum(jnp.where(is_g, jnp.exp(lg - mg), 0.0), axis=-1, keepdims=True)
    pg_top = 1.0 / sg
    g_sel = first_idx(is_g & (lg == mg))

    e_lo = ROUTER_OFF + g_sel * EXPERTS_PER_GROUP
    is_e = (lane >= e_lo) & (lane < e_lo + EXPERTS_PER_GROUP)
    le = jnp.where(is_e, logits, NEG)
    m1 = jnp.max(le, axis=-1, keepdims=True)
    se = jnp.sum(jnp.where(is_e, jnp.exp(le - m1), 0.0), axis=-1, keepdims=True)
    i1 = first_idx(is_e & (le == m1))
    le2 = jnp.where(lane == i1, NEG, le)
    m2 = jnp.max(le2, axis=-1, keepdims=True)
    i2 = first_idx(is_e & (lane != i1) & (le2 == m2))
    pe1 = 1.0 / se
    pe2 = jnp.exp(m2 - m1) / se
    denom = pe1 + pe2
    rt = (jnp.where(lane == i1 - ROUTER_OFF, 1.0, 0.0)
          + jnp.where(lane == i2 - ROUTER_OFF + N_EXPERTS, 1.0, 0.0)
          + jnp.where(lane == RT_W, pg_top * pe1 / denom, 0.0)
          + jnp.where(lane == RT_W + 1, pg_top * pe2 / denom, 0.0))
    rt_ref[...] = rt

    @pl.when(pl.program_id(0) == 0)
    def _():
        cnt_ref[...] = jnp.zeros_like(cnt_ref)

    cnt_ref[...] += jnp.sum(rt, axis=0, keepdims=True)


def _post_call(x2, oa, ob, woa, wob, gmoe, r_cat):
    n = x2.shape[0]
    tm = TM_POST
    row = lambda i: (i, 0)
    const2 = lambda i: (0, 0)
    return pl.pallas_call(
        _post_kernel,
        grid=(n // tm,),
        in_specs=[pl.BlockSpec((tm, D_MODEL), row), pl.BlockSpec((tm, A_WIDTH), row),
                  pl.BlockSpec((tm, B_WIDTH), row), pl.BlockSpec((A_WIDTH, D_MODEL), const2),
                  pl.BlockSpec((B_WIDTH, D_MODEL), const2), pl.BlockSpec((1, D_MODEL), const2),
                  pl.BlockSpec((D_MODEL, 2 * GATE_PAD), const2)],
        out_specs=[pl.BlockSpec((tm, D_MODEL), row), pl.BlockSpec((tm, D_MODEL // 2), row),
                   pl.BlockSpec((tm, GATE_PAD), row), pl.BlockSpec((8, LANES), const2)],
        out_shape=[jax.ShapeDtypeStruct((n, D_MODEL), _F32), jax.ShapeDtypeStruct((n, D_MODEL // 2), jnp.int32),
                   jax.ShapeDtypeStruct((n, GATE_PAD), _F32), jax.ShapeDtypeStruct((8, LANES), _F32)],
        compiler_params=pltpu.CompilerParams(dimension_semantics=("arbitrary",), vmem_limit_bytes=VMEM_LIMIT),
        name="post",
    )(x2, oa, ob, woa, wob, gmoe, r_cat)


def _route_kernel(rt_ref, cnt_ref, dest_ref, meta_ref, off_ref, run_ref):
    tm = rt_ref.shape[0]
    lane = lax.broadcasted_iota(jnp.int32, (1, LANES), 1)
    first = lane < N_EXPERTS
    onehot = jnp.where(lane < 2 * N_EXPERTS, rt_ref[...], 0.0)

    @pl.when(pl.program_id(0) == 0)
    def _():
        cnt = jnp.where(lane < 2 * N_EXPERTS, cnt_ref[...], 0.0)
        c1 = jnp.where(first, cnt, 0.0)
        tot = c1 + jnp.where(first, pltpu.roll(cnt, LANES - N_EXPERTS, 1), 0.0)
        tiles = jnp.floor((tot + (TR_GMM - 1)) * (1.0 / TR_GMM))
        e_row = lax.broadcasted_iota(jnp.int32, (LANES, LANES), 0)
        e_col = lax.broadcasted_iota(jnp.int32, (LANES, LANES), 1)
        before = jnp.where(e_row < e_col, 1.0, 0.0).astype(_BF16)
        base = _dot(tiles.astype(_BF16), before) * TR_GMM
        off_ref[...] = jnp.where(first, base, 0.0) + pltpu.roll(jnp.where(first, base + c1, 0.0), N_EXPERTS, 1)
        run_ref[...] = jnp.zeros_like(run_ref)
        meta_ref[...] = tiles

    r_io = lax.broadcasted_iota(jnp.int32, (tm, tm), 0)
    c_io = lax.broadcasted_iota(jnp.int32, (tm, tm), 1)
    earlier = jnp.where(c_io < r_io, 1.0, 0.0).astype(_BF16)
    rank = _dot(earlier, onehot.astype(_BF16)) + run_ref[0:1, :]
    slot = onehot * (rank + off_ref[0:1, :])
    d1 = jnp.sum(jnp.where(first, slot, 0.0), axis=-1, keepdims=True)
    d2 = jnp.sum(jnp.where(first, 0.0, slot), axis=-1, keepdims=True)
    dest_ref[...] = (jnp.where(lane == 0, d1, 0.0) + jnp.where(lane == 1, d2, 0.0)).astype(jnp.int32)
    run_ref[...] += jnp.sum(onehot, axis=0, keepdims=True)


def _route_call(rt, cnt):
    n = rt.shape[0]
    tm = TM_ROUTE
    return pl.pallas_call(
        _route_kernel,
        grid=(n // tm,),
        in_specs=[pl.BlockSpec((tm, GATE_PAD), lambda i: (i, 0)), pl.BlockSpec((8, LANES), lambda i: (0, 0))],
        out_specs=[pl.BlockSpec((tm, LANES), lambda i: (i, 0)), pl.BlockSpec((8, LANES), lambda i: (0, 0))],
        out_shape=[jax.ShapeDtypeStruct((n, LANES), jnp.int32), jax.ShapeDtypeStruct((8, LANES), _F32)],
        scratch_shapes=[pltpu.VMEM((8, LANES), _F32)] * 2,
        compiler_params=pltpu.CompilerParams(dimension_semantics=("arbitrary",)),
        name="route",
    )(rt, cnt)


def _sc_mesh():
    return plsc.VectorSubcoreMesh(core_axis_name="c", subcore_axis_name="s")


def _sc_worker(n_rows):
    per = n_rows // (SC_CORES * SC_SUBCORES)
    return (lax.axis_index("s") * SC_CORES + lax.axis_index("c")) * per, per


def _dispatch_call(xp, d1, d2, n_slots):
    n, w = xp.shape
    assert n % (SC_CORES * SC_SUBCORES * SC_CHUNK) == 0

    @functools.partial(
        pl.kernel, mesh=_sc_mesh(), out_type=jax.ShapeDtypeStruct((n_slots, w), xp.dtype),
        scratch_types=[pltpu.VMEM((SC_CHUNK,), jnp.int32), pltpu.VMEM((SC_CHUNK,), jnp.int32),
                       pltpu.VMEM((SC_CHUNK, w), xp.dtype), pltpu.SemaphoreType.DMA],
        name="dispatch")
    def k(x_hbm, d1_hbm, d2_hbm, xs_hbm, i1_v, i2_v, rows_v, sem):
        row0, per = _sc_worker(n)

        @pl.loop(0, per // SC_CHUNK)
        def _(j):
            src = pl.ds(row0 + j * SC_CHUNK, SC_CHUNK)
            pltpu.sync_copy(d1_hbm.at[src], i1_v)
            pltpu.sync_copy(d2_hbm.at[src], i2_v)
            pltpu.sync_copy(x_hbm.at[src], rows_v)
            first = pltpu.async_copy(rows_v, xs_hbm.at[i1_v], sem)
            second = pltpu.async_copy(rows_v, xs_hbm.at[i2_v], sem)
            first.wait()
            second.wait()

    return k(xp, d1, d2)


def _combine_call(ys, d1, d2):
    n = d1.shape[0]
    w = ys.shape[1]
    assert n % (SC_CORES * SC_SUBCORES * SC_CHUNK) == 0
    out = jax.ShapeDtypeStruct((n, w), ys.dtype)

    @functools.partial(
        pl.kernel, mesh=_sc_mesh(), out_type=(out, out),
        scratch_types=[pltpu.VMEM((SC_CHUNK,), jnp.int32), pltpu.VMEM((SC_CHUNK, w), ys.dtype),
                       pltpu.SemaphoreType.DMA],
        name="combine")
    def k(ys_hbm, d1_hbm, d2_hbm, y1_hbm, y2_hbm, i_v, rows_v, sem):
        row0, per = _sc_worker(n)

        @pl.loop(0, per // SC_CHUNK)
        def _(j):
            dst = pl.ds(row0 + j * SC_CHUNK, SC_CHUNK)
            for d_hbm, y_hbm in ((d1_hbm, y1_hbm), (d2_hbm, y2_hbm)):
                pltpu.sync_copy(d_hbm.at[dst], i_v)
                pltpu.async_copy(ys_hbm.at[i_v], rows_v, sem).wait()
                pltpu.sync_copy(rows_v, y_hbm.at[dst])

    return k(ys, d1, d2)


def _gmm_kernel(te_ref, nu_ref, xs_ref, wg_ref, wu_ref, wd_ref, ys_ref):
    half = D_MODEL // 2

    @pl.when(pl.program_id(0) < nu_ref[0])
    def _():
        for r in range(TR_GMM // SUB_MOE):
            rows = slice(r * SUB_MOE, (r + 1) * SUB_MOE)
            a, b = _unpack_halves(xs_ref[rows, :])
            a, b = a.astype(_BF16), b.astype(_BF16)
            gate = _dot(a, wg_ref[0, :half, :]) + _dot(b, wg_ref[0, half:, :])
            up = _dot(a, wu_ref[0, :half, :]) + _dot(b, wu_ref[0, half:, :])
            hid = gate * jax.nn.sigmoid(gate) * up
            ys_ref[rows, :] = _pack_halves(_dot(hid.astype(_BF16), wd_ref[0]))


def _gmm_call(tile_expert, n_used, xs, w_gate, w_up, w_down):
    n_slots, w = xs.shape
    rows = lambda t, te, nu: (jnp.minimum(t, nu[0] - 1), 0)
    expert = lambda t, te, nu: (te[t], 0, 0)
    return pl.pallas_call(
        _gmm_kernel,
        grid_spec=pltpu.PrefetchScalarGridSpec(
            num_scalar_prefetch=2, grid=(n_slots // TR_GMM,),
            in_specs=[pl.BlockSpec((TR_GMM, w), rows),
                      pl.BlockSpec((1, D_MODEL, D_FF_EXPERT), expert),
                      pl.BlockSpec((1, D_MODEL, D_FF_EXPERT), expert),
                      pl.BlockSpec((1, D_FF_EXPERT, D_MODEL), expert)],
            out_specs=pl.BlockSpec((TR_GMM, w), rows)),
        out_shape=jax.ShapeDtypeStruct((n_slots, w), xs.dtype),
        compiler_params=pltpu.CompilerParams(dimension_semantics=("arbitrary",), vmem_limit_bytes=VMEM_LIMIT),
        name="gmm",
    )(tile_expert, n_used, xs, w_gate, w_up, w_down)


def _final_kernel(h1_ref, y1_ref, y2_ref, rt_ref, p_ref, gple_ref, wpg_ref, wpp_ref, gfin_ref, o_ref):
    for r in range(h1_ref.shape[0] // SUB_MOE):
        rows = slice(r * SUB_MOE, (r + 1) * SUB_MOE)
        rt = rt_ref[rows, :]
        w1, w2 = rt[:, RT_W:RT_W + 1], rt[:, RT_W + 1:RT_W + 2]
        a1, b1 = _unpack_halves(y1_ref[rows, :])
        a2, b2 = _unpack_halves(y2_ref[rows, :])
        h2 = h1_ref[rows, :] + jnp.concatenate([w1 * a1 + w2 * a2, w1 * b1 + w2 * b2], axis=1)
        gate = jax.nn.sigmoid(_dot(_rms(h2, gple_ref[...]).astype(_BF16), wpg_ref[...]))
        h3 = h2 + _dot(p_ref[rows, :].astype(_BF16), wpp_ref[...]) * gate
        o_ref[rows, :] = _rms(h3, gfin_ref[...])


def _final_call(h1, y1, y2, rt, p2, gple, wpg, wpp, gfin):
    n = h1.shape[0]
    tm = TM_POST
    row = lambda i: (i, 0)
    const2 = lambda i: (0, 0)
    return pl.pallas_call(
        _final_kernel,
        grid=(n // tm,),
        in_specs=[pl.BlockSpec((tm, D_MODEL), row), pl.BlockSpec((tm, D_MODEL // 2), row),
                  pl.BlockSpec((tm, D_MODEL // 2), row), pl.BlockSpec((tm, GATE_PAD), row),
                  pl.BlockSpec((tm, D_PLE), row), pl.BlockSpec((1, D_MODEL), const2),
                  pl.BlockSpec((D_MODEL, D_MODEL), const2), pl.BlockSpec((D_PLE, D_MODEL), const2),
                  pl.BlockSpec((1, D_MODEL), const2)],
        out_specs=pl.BlockSpec((tm, D_MODEL), row),
        out_shape=jax.ShapeDtypeStruct((n, D_MODEL), _F32),
        compiler_params=pltpu.CompilerParams(dimension_semantics=("arbitrary",), vmem_limit_bytes=VMEM_LIMIT),
        name="final",
    )(h1, y1, y2, rt, p2, gple, wpg, wpp, gfin)


def _rope_tables(seq):
    half = HEAD_DIM // 2
    inv = 1.0 / (ROPE_THETA ** (jnp.arange(half, dtype=_F32) / half))
    ang = jnp.arange(seq, dtype=_F32)[:, None] * inv[None, :]
    cos, sin = jnp.cos(ang), jnp.sin(ang)
    reps = LANES // HEAD_DIM
    cos_t = jnp.tile(jnp.concatenate([cos, cos], axis=1), (1, reps))
    sin_t = jnp.tile(jnp.concatenate([-sin, sin], axis=1), (1, reps))
    return cos_t, sin_t


def _layer(h, p_i, norm_mix, w_in, gmlp_v_norm, gmlp_w_s, gmlp_b_s,
           cmp_pe_k, cmp_w1_k, cmp_w2_k, cmp_pe_v, cmp_w1_v, cmp_w2_v,
           out_norm_a, out_norm_b, w_o, norm_moe, router_group, router_expert,
           moe_w_gate, moe_w_up, moe_w_down, norm_ple, w_ple_proj, w_ple_gate, norm_final):
    bsz, seq, _ = h.shape
    n = bsz * seq
    x2 = h.reshape(n, D_MODEL)
    row = lambda v: v.reshape(1, -1).astype(_F32)

    w_q = _perm_heads(w_in[:, OFF_Q:OFF_KV], 1)
    w_gate = jnp.pad(w_in[:, OFF_GATE:D_IN], ((0, 0), (0, GATE_PAD - N_GATES)))
    w_all = jnp.concatenate([w_in[:, :OFF_Q], w_q, w_in[:, OFF_KV:OFF_GATE], w_gate], axis=1).astype(_BF16)
    cos_t, sin_t = _rope_tables(seq)
    ws_pairs = gmlp_w_s.reshape(A_HEADS // 2, 2, CHUNK, CHUNK).transpose(0, 2, 1, 3).reshape(
        A_HEADS // 2, CHUNK, 2 * CHUNK)
    bs_exp = jnp.repeat(gmlp_b_s.T, HEAD_DIM, axis=1)

    oa, q, k_cmp, v_cmp, ks0, ks1, vs0, vs1, k_win, v_win, gates = _proj_call(
        x2, row(norm_mix), w_all, cos_t, sin_t, row(gmlp_v_norm), ws_pairs, bs_exp, row(out_norm_a), seq)

    kc, vc = _compress_call(k_cmp, v_cmp, _compress_weights(cmp_w1_k, cmp_w2_k, cmp_pe_k),
                            _compress_weights(cmp_w1_v, cmp_w2_v, cmp_pe_v), bsz, seq)

    b3 = lambda a: a.reshape(bsz, seq, a.shape[-1])
    ob, (wg_bf, wu_bf, wd_bf) = _attn_call(
        b3(q), b3(gates), kc, vc, b3(ks0), b3(ks1), b3(vs0), b3(vs1), b3(k_win), b3(v_win),
        row(_perm_heads(out_norm_b, 0)), [moe_w_gate, moe_w_up, moe_w_down], bsz, seq)

    r_cat = jnp.pad(jnp.concatenate([router_group, router_expert], axis=1),
                    ((0, 0), (0, GATE_PAD - N_GROUPS - N_EXPERTS)))
    r_hi = r_cat.astype(_BF16)
    r_cat = jnp.concatenate([r_hi, (r_cat - r_hi.astype(_F32)).astype(_BF16)], axis=1)
    h1, xp, rt, cnt = _post_call(x2, oa, ob.reshape(n, B_WIDTH), w_o[:A_WIDTH].astype(_BF16),
                            _perm_heads(w_o[A_WIDTH:], 0).astype(_BF16), row(norm_moe), r_cat)

    dest, meta = _route_call(rt, cnt)
    d1, d2 = dest[:, 0], dest[:, 1]
    n_tiles = 2 * n // TR_GMM + N_EXPERTS
    ends = jnp.cumsum(meta[0, :N_EXPERTS].astype(jnp.int32))
    tile_expert = jnp.minimum(jnp.sum(ends[None, :] <= jnp.arange(n_tiles)[:, None], axis=1),
                              N_EXPERTS - 1).astype(jnp.int32)
    xs = _dispatch_call(xp, d1, d2, n_tiles * TR_GMM)
    ys = _gmm_call(tile_expert, ends[-1:], xs, wg_bf, wu_bf, wd_bf)
    y1, y2 = _combine_call(ys, d1, d2)
    out = _final_call(h1, y1, y2, rt, p_i.reshape(n, D_PLE), row(norm_ple),
                      w_ple_gate.astype(_BF16), w_ple_proj.astype(_BF16), row(norm_final))
    return out.reshape(bsz, seq, D_MODEL)


def kernel(x, p, norm_mix, w_in, gmlp_v_norm, gmlp_w_s, gmlp_b_s, cmp_pe_k, cmp_w1_k, cmp_w2_k,
           cmp_pe_v, cmp_w1_v, cmp_w2_v, out_norm_a, out_norm_b, w_o, norm_moe, router_group,
           router_expert, moe_w_gate, moe_w_up, moe_w_down, norm_ple, w_ple_proj, w_ple_gate, norm_final):
    assert p.shape[0] == 1, "single-layer trunk"
    assert x.shape[1] % SEL_CK == 0 and x.shape[1] >= WIN + Q_BLOCK
    assert (x.shape[0] * x.shape[1]) % (SC_CORES * SC_SUBCORES * SC_CHUNK) == 0
    return _layer(x, p[0], norm_mix[0], w_in[0], gmlp_v_norm[0], gmlp_w_s[0], gmlp_b_s[0],
                  cmp_pe_k[0], cmp_w1_k[0], cmp_w2_k[0], cmp_pe_v[0], cmp_w1_v[0], cmp_w2_v[0],
                  out_norm_a[0], out_norm_b[0], w_o[0], norm_moe[0], router_group[0], router_expert[0],
                  moe_w_gate[0], moe_w_up[0], moe_w_down[0], norm_ple[0], w_ple_proj[0], w_ple_gate[0],
                  norm_final)
```

```python
import functools

import numpy as np
import jax
import jax.numpy as jnp
from jax import lax
from jax.experimental import pallas as pl
from jax.experimental.pallas import tpu as pltpu
from jax.experimental.pallas import tpu_sc as plsc

D_MODEL = 1024
HEAD_DIM = 64
A_HEADS = 8
A_WIDTH = A_HEADS * HEAD_DIM
B_HEADS = 8
B_WIDTH = B_HEADS * HEAD_DIM
B_KV = 2
B_HPG = B_HEADS // B_KV
KV_W = B_KV * HEAD_DIM
N_GATES = B_HEADS * 3
CHUNK = 128
L_CMP = 32
STRIDE_CMP = 16
CMP_HIDDEN = 256
L_SEL = 64
N_SEL = 16
WIN = 512
WIN_Q = 256
Q_BLOCK = 512
ROPE_THETA = 10000.0
N_GROUPS = 4
EXPERTS_PER_GROUP = 4
N_EXPERTS = N_GROUPS * EXPERTS_PER_GROUP
D_FF_EXPERT = 512
D_PLE = 256
EPS = 1e-6
LOG2E = 1.4426950408889634
NEG = -1e30
FORCE = 1e6

OFF_Q = 2 * A_WIDTH
OFF_KV = OFF_Q + B_WIDTH
OFF_GATE = OFF_KV + 6 * KV_W
D_IN = OFF_GATE + N_GATES

LANES = 128
GATE_PAD = LANES
ROUTER_OFF = N_GROUPS
W_ALL = OFF_GATE + GATE_PAD

TM_PROJ = 1024
TM_POST = 1024
TM_ROUTE = 512
TR_GMM = 512
SUB_MOE = 512
RT_W = 2 * N_EXPERTS
SC_CORES = 2
SC_SUBCORES = 16
SC_CHUNK = 128
SEL_CK = 512
VMEM_LIMIT = 56 * 1024 * 1024
VMEM_LIMIT_ATTN = 60 * 1024 * 1024

_PERM_HEADS = [0, 4, 1, 5, 2, 6, 3, 7]


def _perm_heads(a, axis):
    return jnp.concatenate([lax.slice_in_dim(a, h * HEAD_DIM, (h + 1) * HEAD_DIM, axis=axis) for h in _PERM_HEADS],
                           axis=axis)

_F32 = jnp.float32
_BF16 = jnp.bfloat16


def _dot(a, b):
    return jnp.dot(a, b, preferred_element_type=_F32)


def _dot_nt(a, b):
    return lax.dot_general(a, b, (((1,), (1,)), ((), ())), preferred_element_type=_F32)


def _rms(x, g):
    return x * lax.rsqrt(jnp.mean(x * x, axis=-1, keepdims=True) + EPS) * g


def _gelu(x):
    return 0.5 * x * (1.0 + jnp.tanh(0.7978845608028654 * (x + 0.044715 * (x * x * x))))


def _rope_tile(x, cos, sin_signed):
    lane = lax.broadcasted_iota(jnp.int32, x.shape, 1)
    first_half = (lane % HEAD_DIM) < (HEAD_DIM // 2)
    rot = jnp.where(first_half, pltpu.roll(x, LANES - HEAD_DIM // 2, 1), pltpu.roll(x, HEAD_DIM // 2, 1))
    return x * cos + rot * sin_signed


def _proj_kernel(x_ref, gmix_ref, w_ref, cos_ref, sin_ref, gv_ref, ws_ref, bs_ref, goa_ref,
                 oa_ref, q_ref, kc_ref, vc_ref, ks0_ref, ks1_ref, vs0_ref, vs1_ref, kw_ref, vw_ref, gate_ref,
                 *, seq):
    tm = x_ref.shape[0]
    hn = _rms(x_ref[...], gmix_ref[...]).astype(_BF16)
    cos = cos_ref[...]
    sin = sin_ref[...]

    zq = _dot(hn, w_ref[:, OFF_Q:OFF_KV])
    scale = HEAD_DIM ** -0.5 * LOG2E
    for j in range(B_WIDTH // LANES):
        blk = _rope_tile(zq[:, j * LANES:(j + 1) * LANES], cos, sin) * scale
        q_ref[:, j * LANES:(j + 1) * LANES] = blk.astype(q_ref.dtype)

    zkv = _dot(hn, w_ref[:, OFF_KV:OFF_GATE])
    kv = []
    for j in range(6):
        blk = zkv[:, j * KV_W:(j + 1) * KV_W]
        kv.append(_rope_tile(blk, cos, sin) if j % 2 == 0 else blk)
    k_cmp, v_cmp, k_slc, v_slc, k_win, v_win = kv
    kc_ref[...] = k_cmp.astype(kc_ref.dtype)
    vc_ref[...] = v_cmp.astype(vc_ref.dtype)
    kw_ref[...] = k_win.astype(kw_ref.dtype)
    vw_ref[...] = v_win.astype(vw_ref.dtype)

    row = lax.broadcasted_iota(jnp.int32, (tm, LANES), 0)
    lane = lax.broadcasted_iota(jnp.int32, (tm, LANES), 1)
    key_block = ((pl.program_id(0) % (seq // tm)) * tm + row) // L_SEL
    onehot = jnp.where(lane == key_block, 1.0, 0.0).astype(ks0_ref.dtype)
    for g, (ks_ref, vs_ref) in enumerate(((ks0_ref, vs0_ref), (ks1_ref, vs1_ref))):
        own = (lane // HEAD_DIM) == g
        ks_ref[:, :LANES] = jnp.where(own, k_slc, 0.0).astype(ks_ref.dtype)
        ks_ref[:, LANES:] = onehot
        vs_ref[...] = jnp.where(own, v_slc, 1.0).astype(vs_ref.dtype)

    zg = _dot(hn, w_ref[:, OFF_GATE:W_ALL])
    gate_ref[...] = jax.nn.sigmoid(zg)

    zu = _gelu(_dot(hn, w_ref[:, 0:A_WIDTH]))
    zv = _gelu(_dot(hn, w_ref[:, A_WIDTH:2 * A_WIDTH]))
    vn = _rms(zv, gv_ref[...]).astype(_BF16)

    t_io = lax.broadcasted_iota(jnp.int32, (CHUNK, 2 * CHUNK), 0)
    s_io = lax.broadcasted_iota(jnp.int32, (CHUNK, 2 * CHUNK), 1) % CHUNK
    causal = s_io <= t_io
    lane = lax.broadcasted_iota(jnp.int32, (CHUNK, LANES), 1)
    lo = lane < HEAD_DIM
    bs = bs_ref[...]
    chunks = []
    for c in range(tm // CHUNK):
        tiles = []
        for pr in range(A_HEADS // 2):
            wcat = jnp.where(causal, ws_ref[pr], 0.0).astype(_BF16)
            vblk = vn[c * CHUNK:(c + 1) * CHUNK, pr * LANES:(pr + 1) * LANES]
            zero = jnp.zeros_like(vblk)
            rhs = jnp.concatenate([jnp.where(lo, vblk, zero), jnp.where(lo, zero, vblk)], axis=0)
            tiles.append(_dot(wcat, rhs))
        chunks.append(jnp.concatenate(tiles, axis=1) + bs)
    mixed = jnp.concatenate(chunks, axis=0)
    oa = zu * mixed
    oa_ref[...] = _rms(oa, goa_ref[...]).astype(oa_ref.dtype)


def _proj_call(x2, gmix, w_all, cos_t, sin_t, gv, ws_pairs, bs_exp, goa, seq):
    n = x2.shape[0]
    tm = TM_PROJ
    n_t = seq // tm
    row = lambda i: (i, 0)
    const2 = lambda i: (0, 0)
    pos = lambda i: (i % n_t, 0)
    out_shapes = [jax.ShapeDtypeStruct((n, A_WIDTH), _BF16), jax.ShapeDtypeStruct((n, B_WIDTH), _BF16)]
    kv_widths = [KV_W, KV_W, KV_W + LANES, KV_W + LANES, KV_W, KV_W, KV_W, KV_W]
    out_shapes += [jax.ShapeDtypeStruct((n, w), _BF16) for w in kv_widths]
    out_shapes += [jax.ShapeDtypeStruct((n, GATE_PAD), _F32)]
    out_specs = [pl.BlockSpec((tm, A_WIDTH), row), pl.BlockSpec((tm, B_WIDTH), row)]
    out_specs += [pl.BlockSpec((tm, w), row) for w in kv_widths]
    out_specs += [pl.BlockSpec((tm, GATE_PAD), row)]
    return pl.pallas_call(
        functools.partial(_proj_kernel, seq=seq),
        grid=(n // tm,),
        in_specs=[
            pl.BlockSpec((tm, D_MODEL), row),
            pl.BlockSpec((1, D_MODEL), const2),
            pl.BlockSpec((D_MODEL, W_ALL), const2),
            pl.BlockSpec((tm, LANES), pos),
            pl.BlockSpec((tm, LANES), pos),
            pl.BlockSpec((1, A_WIDTH), const2),
            pl.BlockSpec((A_HEADS // 2, CHUNK, 2 * CHUNK), lambda i: (0, 0, 0)),
            pl.BlockSpec((CHUNK, A_WIDTH), const2),
            pl.BlockSpec((1, A_WIDTH), const2),
        ],
        out_specs=out_specs,
        out_shape=out_shapes,
        compiler_params=pltpu.CompilerParams(dimension_semantics=("arbitrary",), vmem_limit_bytes=VMEM_LIMIT),
        name="proj",
    )(x2, gmix, w_all, cos_t, sin_t, gv, ws_pairs, bs_exp, goa)


def _compress_kernel(rk_ref, rv_ref, pek_ref, pev_ref, w1k_ref, w1v_ref,
                     tk_ref, bk_ref, w2k_ref, tv_ref, bv_ref, w2v_ref, kc_ref, vc_ref):
    def one(r_ref, pe_ref, w1_ref, top_ref, bot_ref, w2_ref, o_ref):
        r = r_ref[0]
        nr = r.shape[0]
        a = _dot(r, top_ref[...])
        b = _dot(r, bot_ref[...])
        pe_h = _dot(pe_ref[...], w1_ref[...])
        pe2 = jnp.concatenate([pe_h[0:1], pe_h[0:1]], axis=1)
        hid = a + pltpu.roll(b, nr - 1, 0) + pe2
        o_ref[0] = _dot(_gelu(hid).astype(_BF16), w2_ref[...]).astype(o_ref.dtype)

    one(rk_ref, pek_ref, w1k_ref, tk_ref, bk_ref, w2k_ref, kc_ref)
    one(rv_ref, pev_ref, w1v_ref, tv_ref, bv_ref, w2v_ref, vc_ref)


def _compress_weights(w1, w2, pe):
    half = L_CMP // 2
    w1r = w1.reshape(L_CMP, HEAD_DIM, CMP_HIDDEN)
    eye = jnp.eye(B_KV, dtype=w1.dtype)
    place = lambda part: jnp.einsum('ldj,gh->lgdhj', part, eye).reshape(half * KV_W, B_KV * CMP_HIDDEN)
    top = place(w1r[:half]).astype(_BF16)
    bot = place(w1r[half:]).astype(_BF16)
    w2bd = jnp.einsum('jd,gh->gjhd', w2, eye).reshape(B_KV * CMP_HIDDEN, KV_W).astype(_BF16)
    pe8 = jnp.broadcast_to(pe.reshape(1, L_CMP * HEAD_DIM), (8, L_CMP * HEAD_DIM)).astype(_BF16)
    return pe8, w1.astype(_BF16), top, bot, w2bd


def _compress_call(k_cmp, v_cmp, wk, wv, bsz, seq):
    nr = seq // STRIDE_CMP
    rk = k_cmp.reshape(bsz, nr, STRIDE_CMP * KV_W)
    rv = v_cmp.reshape(bsz, nr, STRIDE_CMP * KV_W)
    pek, w1k, tk, bk, w2k = wk
    pev, w1v, tv, bv, w2v = wv
    rspec = pl.BlockSpec((1, nr, STRIDE_CMP * KV_W), lambda b: (b, 0, 0))
    full = lambda a: pl.BlockSpec(a.shape, lambda b: (0,) * a.ndim)
    ospec = pl.BlockSpec((1, nr, KV_W), lambda b: (b, 0, 0))
    return pl.pallas_call(
        _compress_kernel,
        grid=(bsz,),
        in_specs=[rspec, rspec, full(pek), full(pev), full(w1k), full(w1v),
                  full(tk), full(bk), full(w2k), full(tv), full(bv), full(w2v)],
        out_specs=[ospec, ospec],
        out_shape=[jax.ShapeDtypeStruct((bsz, nr, KV_W), _BF16)] * 2,
        compiler_params=pltpu.CompilerParams(dimension_semantics=("arbitrary",), vmem_limit_bytes=VMEM_LIMIT),
        name="compress",
    )(rk, rv, pek, pev, w1k, w1v, tk, bk, w2k, tv, bv, w2v)


def _topk_rows_mask(sc_t, k):
    n_rows = sc_t.shape[0]
    row = lax.broadcasted_iota(jnp.int32, sc_t.shape, 0)
    taken = -3e38
    for _ in range(k):
        m = jnp.max(sc_t, axis=0, keepdims=True)
        idx = jnp.min(jnp.where(sc_t == m, row, n_rows), axis=0, keepdims=True)
        sc_t = jnp.where(row == idx, taken, sc_t)
    return jnp.where(sc_t == taken, 1.0, 0.0)


def _attn_kernel(q_ref, gate_ref, gexp_ref, kc_ref, vc_ref, ks0_ref, ks1_ref, vs0_ref, vs1_ref,
                 kw_ref, vw_ref, gob_ref, *rest, seq, n_cast):
    cast_in, o_ref, cast_out = rest[:n_cast], rest[n_cast], rest[n_cast + 1:2 * n_cast + 1]
    m_scr, acc_scr = rest[2 * n_cast + 1:]
    for src, dst in zip(cast_in, cast_out):
        dst[...] = src[...].astype(dst.dtype)

    qb = pl.program_id(1)
    t0 = qb * Q_BLOCK
    n_cmp = kc_ref.shape[1]
    n_sb = seq // L_SEL
    k_top = min(N_SEL, n_sb)
    rows = B_HPG * Q_BLOCK
    ks_refs, vs_refs = (ks0_ref, ks1_ref), (vs0_ref, vs1_ref)

    lane_q = lax.broadcasted_iota(jnp.int32, (Q_BLOCK, LANES), 1)
    lo = lane_q < HEAD_DIM
    t_col = t0 + lax.broadcasted_iota(jnp.int32, (Q_BLOCK, 1), 0)

    def per_head(x):
        return x.reshape(B_HPG, Q_BLOCK, x.shape[-1])

    def add_bias(s, bias):
        return (per_head(s) + bias[None]).reshape(rows, s.shape[-1])

    qs = []
    for g in range(B_KV):
        own = (lane_q // HEAD_DIM) == g
        qs.append(jnp.concatenate(
            [jnp.where(own, q_ref[0, :, j * LANES:(j + 1) * LANES], jnp.zeros((), q_ref.dtype))
             for j in range(B_HPG)], axis=0))

    c_lane = lax.broadcasted_iota(jnp.int32, (Q_BLOCK, n_cmp), 1)
    bias_c = jnp.where(c_lane * STRIDE_CMP + (L_CMP - 1) <= t_col, 0.0, NEG)
    has_c = (t_col >= L_CMP - 1).astype(_F32)
    c_row = lax.broadcasted_iota(jnp.int32, (n_cmp, n_sb), 0) * STRIDE_CMP
    s_col = lax.broadcasted_iota(jnp.int32, (n_cmp, n_sb), 1) * L_SEL
    overlap = jnp.where((c_row < s_col + L_SEL) & (c_row + L_CMP > s_col), 1.0, 0.0).astype(_BF16)
    blk = lax.broadcasted_iota(jnp.int32, (Q_BLOCK, n_sb), 1)
    cur = t_col // L_SEL
    forced = (blk == 0) | (blk == cur) | (blk == cur - 1)
    valid = blk * L_SEL <= t_col

    o_c, sel_bias = [], []
    for g in range(B_KV):
        s_c = add_bias(_dot_nt(qs[g], kc_ref[0]), bias_c)
        e_c = jnp.exp2(s_c - jnp.max(s_c, axis=-1, keepdims=True))
        inv = per_head(1.0 / jnp.maximum(jnp.sum(e_c, axis=-1, keepdims=True), 1e-30)) * has_c[None]
        e_bf = e_c.astype(_BF16)
        o_c.append((per_head(_dot(e_bf, vc_ref[0])) * inv).reshape(rows, LANES))
        p_bf = per_head(e_bf) * inv.astype(_BF16)
        imp = _dot(functools.reduce(lambda a, b: a + b, [p_bf[j] for j in range(B_HPG)]), overlap)
        score = jnp.where(valid & jnp.logical_not(forced), imp, -FORCE)
        chosen = forced | (_topk_rows_mask(score.T, k_top - 3).T > 0.5)
        sb = jnp.where(chosen & valid, 0.0, NEG)
        if n_sb < LANES:
            sb = jnp.concatenate([sb, jnp.full((Q_BLOCK, LANES - n_sb), NEG, _F32)], axis=1)
        sel_bias.append(sb)

    w_len = WIN + WIN_Q
    lane_w = lax.broadcasted_iota(jnp.int32, (w_len, LANES), 1)
    acc_w_sub = [[] for _ in range(B_KV)]
    for hh in range(Q_BLOCK // WIN_Q):
        w_start = pl.multiple_of(jnp.maximum(t0 + hh * WIN_Q - WIN, 0), WIN_Q)
        t_sub = t_col[hh * WIN_Q:(hh + 1) * WIN_Q]
        diff_w = t_sub - (w_start + lax.broadcasted_iota(jnp.int32, (WIN_Q, w_len), 1))
        bias_w = jnp.where((diff_w >= 0) & (diff_w < WIN), 0.0, NEG)
        kw = kw_ref[0, pl.ds(w_start, w_len), :]
        vw = vw_ref[0, pl.ds(w_start, w_len), :]
        for g in range(B_KV):
            q_sub = per_head(qs[g])[:, hh * WIN_Q:(hh + 1) * WIN_Q, :].reshape(B_HPG * WIN_Q, LANES)
            s_w = (_dot_nt(q_sub, kw).reshape(B_HPG, WIN_Q, w_len) + bias_w[None]).reshape(B_HPG * WIN_Q, w_len)
            e_w = jnp.exp2(s_w - jnp.max(s_w, axis=-1, keepdims=True)).astype(_BF16)
            vw_aug = jnp.where((lane_w // HEAD_DIM) == g, vw, jnp.ones((), vw.dtype))
            acc_w_sub[g].append(_dot(e_w, vw_aug).reshape(B_HPG, WIN_Q, LANES))
    acc_w = [jnp.concatenate(acc_w_sub[g], axis=1).reshape(rows, LANES) for g in range(B_KV)]

    n_ck = (t0 + Q_BLOCK + SEL_CK - 1) // SEL_CK
    key_lane = lax.broadcasted_iota(jnp.int32, (Q_BLOCK, SEL_CK), 1)
    bias_diag = jnp.where((n_ck - 1) * SEL_CK + key_lane <= t_col, 0.0, NEG)
    q_wide = [jnp.concatenate([qs[g], jnp.concatenate([sel_bias[g].astype(_BF16)] * B_HPG, axis=0)], axis=1)
              for g in range(B_KV)]

    m_scr[...] = jnp.full(m_scr.shape, NEG, _F32)
    acc_scr[...] = jnp.zeros(acc_scr.shape, _F32)

    def sel_chunk(ci, diag):
        k0 = pl.multiple_of(ci * SEL_CK, SEL_CK)
        for g in range(B_KV):
            m = m_scr[g]
            s = _dot_nt(q_wide[g], ks_refs[g][0, pl.ds(k0, SEL_CK), :])
            if diag:
                s = add_bias(s, bias_diag)
            m_new = jnp.maximum(m, jnp.max(s, axis=-1, keepdims=True))
            p = jnp.exp2(s - jnp.concatenate([m_new] * (SEL_CK // LANES), axis=1)).astype(_BF16)
            acc_scr[g] = jnp.exp2(m - m_new) * acc_scr[g] + _dot(p, vs_refs[g][0, pl.ds(k0, SEL_CK), :])
            m_scr[g] = m_new

    @pl.loop(0, n_ck - 1)
    def _(ci):
        sel_chunk(ci, False)

    sel_chunk(n_ck - 1, True)
    acc_s = [acc_scr[g] for g in range(B_KV)]

    def numer(acc):
        return jnp.concatenate([jnp.where(lo, acc[0][j * Q_BLOCK:(j + 1) * Q_BLOCK],
                                          acc[1][j * Q_BLOCK:(j + 1) * Q_BLOCK]) for j in range(B_HPG)], axis=1)

    def denom(acc):
        return jnp.concatenate([pltpu.roll(jnp.where(lo, acc[1][j * Q_BLOCK:(j + 1) * Q_BLOCK],
                                                     acc[0][j * Q_BLOCK:(j + 1) * Q_BLOCK]), HEAD_DIM, 1)
                                for j in range(B_HPG)], axis=1)

    gates = gate_ref[0]
    g_hi = gates.astype(_BF16)
    g_lo = (gates - g_hi.astype(_F32)).astype(_BF16)
    gate_of = lambda r: _dot(g_hi, gexp_ref[r]) + _dot(g_lo, gexp_ref[r])
    ob = (gate_of(0) * numer(o_c)
          + gate_of(1) * numer(acc_s) * (1.0 / jnp.maximum(denom(acc_s), 1e-30))
          + gate_of(2) * numer(acc_w) * (1.0 / jnp.maximum(denom(acc_w), 1e-30)))
    o_ref[0] = _rms(ob, gob_ref[...]).astype(o_ref.dtype)


def _gate_expand():
    x = np.zeros((3, GATE_PAD, B_WIDTH), np.float32)
    for slot, h in enumerate(_PERM_HEADS):
        for r in range(3):
            x[r, 3 * h + r, slot * HEAD_DIM:(slot + 1) * HEAD_DIM] = 1.0
    return jnp.asarray(x, _BF16)


def _attn_call(q, gates, kc, vc, ks0, ks1, vs0, vs1, kw, vw, gob, to_cast, bsz, seq):
    assert seq // L_SEL <= LANES
    n_cmp = kc.shape[1]
    n_q = seq // Q_BLOCK
    steps = bsz * n_q
    qspec = lambda w: pl.BlockSpec((1, Q_BLOCK, w), lambda b, i: (b, i, 0))
    full = lambda r, w=KV_W: pl.BlockSpec((1, r, w), lambda b, i: (b, 0, 0), pipeline_mode=pl.Buffered(1))
    sliced = [a.reshape(steps, a.size // (steps * a.shape[-1]), a.shape[-1]) for a in to_cast]
    cast_specs = [pl.BlockSpec((1,) + a.shape[1:], lambda b, i: (b * n_q + i, 0, 0)) for a in sliced]
    outs = pl.pallas_call(
        functools.partial(_attn_kernel, seq=seq, n_cast=len(sliced)),
        grid=(bsz, n_q),
        in_specs=[qspec(B_WIDTH), qspec(GATE_PAD),
                  pl.BlockSpec((3, GATE_PAD, B_WIDTH), lambda b, i: (0, 0, 0)),
                  full(n_cmp), full(n_cmp), full(seq, KV_W + LANES), full(seq, KV_W + LANES),
                  full(seq), full(seq), full(seq), full(seq),
                  pl.BlockSpec((1, B_WIDTH), lambda b, i: (0, 0))] + cast_specs,
        out_specs=[qspec(B_WIDTH)] + cast_specs,
        out_shape=[jax.ShapeDtypeStruct((bsz, seq, B_WIDTH), _BF16)]
        + [jax.ShapeDtypeStruct(a.shape, _BF16) for a in sliced],
        scratch_shapes=[pltpu.VMEM((B_KV, B_HPG * Q_BLOCK, LANES), _F32),
                        pltpu.VMEM((B_KV, B_HPG * Q_BLOCK, LANES), _F32)],
        compiler_params=pltpu.CompilerParams(dimension_semantics=("arbitrary", "arbitrary"),
                                             vmem_limit_bytes=VMEM_LIMIT_ATTN),
        name="attn",
    )(q, gates, _gate_expand(), kc, vc, ks0, ks1, vs0, vs1, kw, vw, gob, *sliced)
    return outs[0], [o.reshape(a.shape) for o, a in zip(outs[1:], to_cast)]


def _pack_halves(x):
    w = x.shape[1] // 2
    bits = lambda v: lax.bitcast_convert_type(v.astype(_BF16).astype(_F32), jnp.uint32)
    return lax.bitcast_convert_type(bits(x[:, :w]) | (bits(x[:, w:]) >> 16), jnp.int32)


def _unpack_halves(p):
    u = lax.bitcast_convert_type(p, jnp.uint32)
    return (lax.bitcast_convert_type(u & jnp.uint32(0xFFFF0000), _F32),
            lax.bitcast_convert_type(u << 16, _F32))


def _post_kernel(x_ref, oa_ref, ob_ref, woa_ref, wob_ref, gmoe_ref, r_ref, h1_ref, hn_ref, rt_ref, cnt_ref):
    h1 = x_ref[...] + _dot(oa_ref[...], woa_ref[...]) + _dot(ob_ref[...], wob_ref[...])
    h1_ref[...] = h1
    hn = _rms(h1, gmoe_ref[...])
    hn_ref[...] = _pack_halves(hn)

    hn_hi = hn.astype(_BF16)
    hn_lo = (hn - hn_hi.astype(_F32)).astype(_BF16)
    hi_both = _dot(hn_hi, r_ref[...])
    logits = hi_both[:, :GATE_PAD] + (_dot(hn_lo, r_ref[:, :GATE_PAD]) + hi_both[:, GATE_PAD:])
    lane = lax.broadcasted_iota(jnp.int32, logits.shape, 1)
    first_idx = lambda hit: jnp.min(jnp.where(hit, lane, LANES), axis=-1, keepdims=True)

    is_g = lane < N_GROUPS
    lg = jnp.where(is_g, logits, NEG)
    mg = jnp.max(lg, axis=-1, keepdims=True)
    sg = jnp.sum(jnp.where(is_g, jnp.exp(lg - mg), 0.0), axis=-1, keepdims=True)
    pg_top = 1.0 / sg
    g_sel = first_idx(is_g & (lg == mg))

    e_lo = ROUTER_OFF + g_sel * EXPERTS_PER_GROUP
    is_e = (lane >= e_lo) & (lane < e_lo + EXPERTS_PER_GROUP)
    le = jnp.where(is_e, logits, NEG)
    m1 = jnp.max(le, axis=-1, keepdims=True)
    se = jnp.sum(jnp.where(is_e, jnp.exp(le - m1), 0.0), axis=-1, keepdims=True)
    i1 = first_idx(is_e & (le == m1))
    le2 = jnp.where(lane == i1, NEG, le)
    m2 = jnp.max(le2, axis=-1, keepdims=True)
    i2 = first_idx(is_e & (lane != i1) & (le2 == m2))
    pe1 = 1.0 / se
    pe2 = jnp.exp(m2 - m1) / se
    denom = pe1 + pe2
    rt = (jnp.where(lane == i1 - ROUTER_OFF, 1.0, 0.0)
          + jnp.where(lane == i2 - ROUTER_OFF + N_EXPERTS, 1.0, 0.0)
          + jnp.where(lane == RT_W, pg_top * pe1 / denom, 0.0)
          + jnp.where(lane == RT_W + 1, pg_top * pe2 / denom, 0.0))
    rt_ref[...] = rt

    @pl.when(pl.program_id(0) == 0)
    def _():
        cnt_ref[...] = jnp.zeros_like(cnt_ref)

    cnt_ref[...] += jnp.sum(rt, axis=0, keepdims=True)


def _post_call(x2, oa, ob, woa, wob, gmoe, r_cat):
    n = x2.shape[0]
    tm = TM_POST
    row = lambda i: (i, 0)
    const2 = lambda i: (0, 0)
    return pl.pallas_call(
        _post_kernel,
        grid=(n // tm,),
        in_specs=[pl.BlockSpec((tm, D_MODEL), row), pl.BlockSpec((tm, A_WIDTH), row),
                  pl.BlockSpec((tm, B_WIDTH), row), pl.BlockSpec((A_WIDTH, D_MODEL), const2),
                  pl.BlockSpec((B_WIDTH, D_MODEL), const2), pl.BlockSpec((1, D_MODEL), const2),
                  pl.BlockSpec((D_MODEL, 2 * GATE_PAD), const2)],
        out_specs=[pl.BlockSpec((tm, D_MODEL), row), pl.BlockSpec((tm, D_MODEL // 2), row),
                   pl.BlockSpec((tm, GATE_PAD), row), pl.BlockSpec((8, LANES), const2)],
        out_shape=[jax.ShapeDtypeStruct((n, D_MODEL), _F32), jax.ShapeDtypeStruct((n, D_MODEL // 2), jnp.int32),
                   jax.ShapeDtypeStruct((n, GATE_PAD), _F32), jax.ShapeDtypeStruct((8, LANES), _F32)],
        compiler_params=pltpu.CompilerParams(dimension_semantics=("arbitrary",), vmem_limit_bytes=VMEM_LIMIT),
        name="post",
    )(x2, oa, ob, woa, wob, gmoe, r_cat)


def _route_kernel(rt_ref, cnt_ref, dest_ref, meta_ref, off_ref, run_ref):
    tm = rt_ref.shape[0]
    lane = lax.broadcasted_iota(jnp.int32, (1, LANES), 1)
    first = lane < N_EXPERTS
    onehot = jnp.where(lane < 2 * N_EXPERTS, rt_ref[...], 0.0)

    @pl.when(pl.program_id(0) == 0)
    def _():
        cnt = jnp.where(lane < 2 * N_EXPERTS, cnt_ref[...], 0.0)
        c1 = jnp.where(first, cnt, 0.0)
        tot = c1 + jnp.where(first, pltpu.roll(cnt, LANES - N_EXPERTS, 1), 0.0)
        tiles = jnp.floor((tot + (TR_GMM - 1)) * (1.0 / TR_GMM))
        e_row = lax.broadcasted_iota(jnp.int32, (LANES, LANES), 0)
        e_col = lax.broadcasted_iota(jnp.int32, (LANES, LANES), 1)
        before = jnp.where(e_row < e_col, 1.0, 0.0).astype(_BF16)
        base = _dot(tiles.astype(_BF16), before) * TR_GMM
        off_ref[...] = jnp.where(first, base, 0.0) + pltpu.roll(jnp.where(first, base + c1, 0.0), N_EXPERTS, 1)
        run_ref[...] = jnp.zeros_like(run_ref)
        meta_ref[...] = tiles

    r_io = lax.broadcasted_iota(jnp.int32, (tm, tm), 0)
    c_io = lax.broadcasted_iota(jnp.int32, (tm, tm), 1)
    earlier = jnp.where(c_io < r_io, 1.0, 0.0).astype(_BF16)
    rank = _dot(earlier, onehot.astype(_BF16)) + run_ref[0:1, :]
    slot = onehot * (rank + off_ref[0:1, :])
    d1 = jnp.sum(jnp.where(first, slot, 0.0), axis=-1, keepdims=True)
    d2 = jnp.sum(jnp.where(first, 0.0, slot), axis=-1, keepdims=True)
    dest_ref[...] = (jnp.where(lane == 0, d1, 0.0) + jnp.where(lane == 1, d2, 0.0)).astype(jnp.int32)
    run_ref[...] += jnp.sum(onehot, axis=0, keepdims=True)


def _route_call(rt, cnt):
    n = rt.shape[0]
    tm = TM_ROUTE
    return pl.pallas_call(
        _route_kernel,
        grid=(n // tm,),
        in_specs=[pl.BlockSpec((tm, GATE_PAD), lambda i: (i, 0)), pl.BlockSpec((8, LANES), lambda i: (0, 0))],
        out_specs=[pl.BlockSpec((tm, LANES), lambda i: (i, 0)), pl.BlockSpec((8, LANES), lambda i: (0, 0))],
        out_shape=[jax.ShapeDtypeStruct((n, LANES), jnp.int32), jax.ShapeDtypeStruct((8, LANES), _F32)],
        scratch_shapes=[pltpu.VMEM((8, LANES), _F32)] * 2,
        compiler_params=pltpu.CompilerParams(dimension_semantics=("arbitrary",)),
        name="route",
    )(rt, cnt)


def _sc_mesh():
    return plsc.VectorSubcoreMesh(core_axis_name="c", subcore_axis_name="s")


def _sc_worker(n_rows):
    per = n_rows // (SC_CORES * SC_SUBCORES)
    return (lax.axis_index("s") * SC_CORES + lax.axis_index("c")) * per, per


def _dispatch_call(xp, d1, d2, n_slots):
    n, w = xp.shape
    assert n % (SC_CORES * SC_SUBCORES * SC_CHUNK) == 0

    @functools.partial(
        pl.kernel, mesh=_sc_mesh(), out_type=jax.ShapeDtypeStruct((n_slots, w), xp.dtype),
        scratch_types=[pltpu.VMEM((SC_CHUNK,), jnp.int32), pltpu.VMEM((SC_CHUNK,), jnp.int32),
                       pltpu.VMEM((SC_CHUNK, w), xp.dtype), pltpu.SemaphoreType.DMA],
        name="dispatch")
    def k(x_hbm, d1_hbm, d2_hbm, xs_hbm, i1_v, i2_v, rows_v, sem):
        row0, per = _sc_worker(n)

        @pl.loop(0, per // SC_CHUNK)
        def _(j):
            src = pl.ds(row0 + j * SC_CHUNK, SC_CHUNK)
            pltpu.sync_copy(d1_hbm.at[src], i1_v)
            pltpu.sync_copy(d2_hbm.at[src], i2_v)
            pltpu.sync_copy(x_hbm.at[src], rows_v)
            first = pltpu.async_copy(rows_v, xs_hbm.at[i1_v], sem)
            second = pltpu.async_copy(rows_v, xs_hbm.at[i2_v], sem)
            first.wait()
            second.wait()

    return k(xp, d1, d2)


def _combine_call(ys, d1, d2):
    n = d1.shape[0]
    w = ys.shape[1]
    assert n % (SC_CORES * SC_SUBCORES * SC_CHUNK) == 0
    out = jax.ShapeDtypeStruct((n, w), ys.dtype)

    @functools.partial(
        pl.kernel, mesh=_sc_mesh(), out_type=(out, out),
        scratch_types=[pltpu.VMEM((SC_CHUNK,), jnp.int32), pltpu.VMEM((SC_CHUNK, w), ys.dtype),
                       pltpu.SemaphoreType.DMA],
        name="combine")
    def k(ys_hbm, d1_hbm, d2_hbm, y1_hbm, y2_hbm, i_v, rows_v, sem):
        row0, per = _sc_worker(n)

        @pl.loop(0, per // SC_CHUNK)
        def _(j):
            dst = pl.ds(row0 + j * SC_CHUNK, SC_CHUNK)
            for d_hbm, y_hbm in ((d1_hbm, y1_hbm), (d2_hbm, y2_hbm)):
                pltpu.sync_copy(d_hbm.at[dst], i_v)
                pltpu.async_copy(ys_hbm.at[i_v], rows_v, sem).wait()
                pltpu.sync_copy(rows_v, y_hbm.at[dst])

    return k(ys, d1, d2)


def _gmm_kernel(te_ref, nu_ref, xs_ref, wg_ref, wu_ref, wd_ref, ys_ref):
    half = D_MODEL // 2

    @pl.when(pl.program_id(0) < nu_ref[0])
    def _():
        for r in range(TR_GMM // SUB_MOE):
            rows = slice(r * SUB_MOE, (r + 1) * SUB_MOE)
            a, b = _unpack_halves(xs_ref[rows, :])
            a, b = a.astype(_BF16), b.astype(_BF16)
            gate = _dot(a, wg_ref[0, :half, :]) + _dot(b, wg_ref[0, half:, :])
            up = _dot(a, wu_ref[0, :half, :]) + _dot(b, wu_ref[0, half:, :])
            hid = gate * jax.nn.sigmoid(gate) * up
            ys_ref[rows, :] = _pack_halves(_dot(hid.astype(_BF16), wd_ref[0]))


def _gmm_call(tile_expert, n_used, xs, w_gate, w_up, w_down):
    n_slots, w = xs.shape
    rows = lambda t, te, nu: (jnp.minimum(t, nu[0] - 1), 0)
    expert = lambda t, te, nu: (te[t], 0, 0)
    return pl.pallas_call(
        _gmm_kernel,
        grid_spec=pltpu.PrefetchScalarGridSpec(
            num_scalar_prefetch=2, grid=(n_slots // TR_GMM,),
            in_specs=[pl.BlockSpec((TR_GMM, w), rows),
                      pl.BlockSpec((1, D_MODEL, D_FF_EXPERT), expert),
                      pl.BlockSpec((1, D_MODEL, D_FF_EXPERT), expert),
                      pl.BlockSpec((1, D_FF_EXPERT, D_MODEL), expert)],
            out_specs=pl.BlockSpec((TR_GMM, w), rows)),
        out_shape=jax.ShapeDtypeStruct((n_slots, w), xs.dtype),
        compiler_params=pltpu.CompilerParams(dimension_semantics=("arbitrary",), vmem_limit_bytes=VMEM_LIMIT),
        name="gmm",
    )(tile_expert, n_used, xs, w_gate, w_up, w_down)


def _final_kernel(h1_ref, y1_ref, y2_ref, rt_ref, p_ref, gple_ref, wpg_ref, wpp_ref, gfin_ref, o_ref):
    for r in range(h1_ref.shape[0] // SUB_MOE):
        rows = slice(r * SUB_MOE, (r + 1) * SUB_MOE)
        rt = rt_ref[rows, :]
        w1, w2 = rt[:, RT_W:RT_W + 1], rt[:, RT_W + 1:RT_W + 2]
        a1, b1 = _unpack_halves(y1_ref[rows, :])
        a2, b2 = _unpack_halves(y2_ref[rows, :])
        h2 = h1_ref[rows, :] + jnp.concatenate([w1 * a1 + w2 * a2, w1 * b1 + w2 * b2], axis=1)
        gate = jax.nn.sigmoid(_dot(_rms(h2, gple_ref[...]).astype(_BF16), wpg_ref[...]))
        h3 = h2 + _dot(p_ref[rows, :].astype(_BF16), wpp_ref[...]) * gate
        o_ref[rows, :] = _rms(h3, gfin_ref[...])


def _final_call(h1, y1, y2, rt, p2, gple, wpg, wpp, gfin):
    n = h1.shape[0]
    tm = TM_POST
    row = lambda i: (i, 0)
    const2 = lambda i: (0, 0)
    return pl.pallas_call(
        _final_kernel,
        grid=(n // tm,),
        in_specs=[pl.BlockSpec((tm, D_MODEL), row), pl.BlockSpec((tm, D_MODEL // 2), row),
                  pl.BlockSpec((tm, D_MODEL // 2), row), pl.BlockSpec((tm, GATE_PAD), row),
                  pl.BlockSpec((tm, D_PLE), row), pl.BlockSpec((1, D_MODEL), const2),
                  pl.BlockSpec((D_MODEL, D_MODEL), const2), pl.BlockSpec((D_PLE, D_MODEL), const2),
                  pl.BlockSpec((1, D_MODEL), const2)],
        out_specs=pl.BlockSpec((tm, D_MODEL), row),
        out_shape=jax.ShapeDtypeStruct((n, D_MODEL), _F32),
        compiler_params=pltpu.CompilerParams(dimension_semantics=("arbitrary",), vmem_limit_bytes=VMEM_LIMIT),
        name="final",
    )(h1, y1, y2, rt, p2, gple, wpg, wpp, gfin)


def _rope_tables(seq):
    half = HEAD_DIM // 2
    inv = 1.0 / (ROPE_THETA ** (jnp.arange(half, dtype=_F32) / half))
    ang = jnp.arange(seq, dtype=_F32)[:, None] * inv[None, :]
    cos, sin = jnp.cos(ang), jnp.sin(ang)
    reps = LANES // HEAD_DIM
    cos_t = jnp.tile(jnp.concatenate([cos, cos], axis=1), (1, reps))
    sin_t = jnp.tile(jnp.concatenate([-sin, sin], axis=1), (1, reps))
    return cos_t, sin_t


def _layer(h, p_i, norm_mix, w_in, gmlp_v_norm, gmlp_w_s, gmlp_b_s,
           cmp_pe_k, cmp_w1_k, cmp_w2_k, cmp_pe_v, cmp_w1_v, cmp_w2_v,
           out_norm_a, out_norm_b, w_o, norm_moe, router_group, router_expert,
           moe_w_gate, moe_w_up, moe_w_down, norm_ple, w_ple_proj, w_ple_gate, norm_final):
    bsz, seq, _ = h.shape
    n = bsz * seq
    x2 = h.reshape(n, D_MODEL)
    row = lambda v: v.reshape(1, -1).astype(_F32)

    w_q = _perm_heads(w_in[:, OFF_Q:OFF_KV], 1)
    w_gate = jnp.pad(w_in[:, OFF_GATE:D_IN], ((0, 0), (0, GATE_PAD - N_GATES)))
    w_all = jnp.concatenate([w_in[:, :OFF_Q], w_q, w_in[:, OFF_KV:OFF_GATE], w_gate], axis=1).astype(_BF16)
    cos_t, sin_t = _rope_tables(seq)
    ws_pairs = gmlp_w_s.reshape(A_HEADS // 2, 2, CHUNK, CHUNK).transpose(0, 2, 1, 3).reshape(
        A_HEADS // 2, CHUNK, 2 * CHUNK)
    bs_exp = jnp.repeat(gmlp_b_s.T, HEAD_DIM, axis=1)

    oa, q, k_cmp, v_cmp, ks0, ks1, vs0, vs1, k_win, v_win, gates = _proj_call(
        x2, row(norm_mix), w_all, cos_t, sin_t, row(gmlp_v_norm), ws_pairs, bs_exp, row(out_norm_a), seq)

    kc, vc = _compress_call(k_cmp, v_cmp, _compress_weights(cmp_w1_k, cmp_w2_k, cmp_pe_k),
                            _compress_weights(cmp_w1_v, cmp_w2_v, cmp_pe_v), bsz, seq)

    b3 = lambda a: a.reshape(bsz, seq, a.shape[-1])
    ob, (wg_bf, wu_bf, wd_bf) = _attn_call(
        b3(q), b3(gates), kc, vc, b3(ks0), b3(ks1), b3(vs0), b3(vs1), b3(k_win), b3(v_win),
        row(_perm_heads(out_norm_b, 0)), [moe_w_gate, moe_w_up, moe_w_down], bsz, seq)

    r_cat = jnp.pad(jnp.concatenate([router_group, router_expert], axis=1),
                    ((0, 0), (0, GATE_PAD - N_GROUPS - N_EXPERTS)))
    r_hi = r_cat.astype(_BF16)
    r_cat = jnp.concatenate([r_hi, (r_cat - r_hi.astype(_F32)).astype(_BF16)], axis=1)
    h1, xp, rt, cnt = _post_call(x2, oa, ob.reshape(n, B_WIDTH), w_o[:A_WIDTH].astype(_BF16),
                            _perm_heads(w_o[A_WIDTH:], 0).astype(_BF16), row(norm_moe), r_cat)

    dest, meta = _route_call(rt, cnt)
    d1, d2 = dest[:, 0], dest[:, 1]
    n_tiles = 2 * n // TR_GMM + N_EXPERTS
    ends = jnp.cumsum(meta[0, :N_EXPERTS].astype(jnp.int32))
    tile_expert = jnp.minimum(jnp.sum(ends[None, :] <= jnp.arange(n_tiles)[:, None], axis=1),
                              N_EXPERTS - 1).astype(jnp.int32)
    xs = _dispatch_call(xp, d1, d2, n_tiles * TR_GMM)
    ys = _gmm_call(tile_expert, ends[-1:], xs, wg_bf, wu_bf, wd_bf)
    y1, y2 = _combine_call(ys, d1, d2)
    out = _final_call(h1, y1, y2, rt, p_i.reshape(n, D_PLE), row(norm_ple),
                      w_ple_gate.astype(_BF16), w_ple_proj.astype(_BF16), row(norm_final))
    return out.reshape(bsz, seq, D_MODEL)


def kernel(x, p, norm_mix, w_in, gmlp_v_norm, gmlp_w_s, gmlp_b_s, cmp_pe_k, cmp_w1_k, cmp_w2_k,
           cmp_pe_v, cmp_w1_v, cmp_w2_v, out_norm_a, out_norm_b, w_o, norm_moe, router_group,
           router_expert, moe_w_gate, moe_w_up, moe_w_down, norm_ple, w_ple_proj, w_ple_gate, norm_final):
    assert p.shape[0] == 1, "single-layer trunk"
    assert x.shape[1] % SEL_CK == 0 and x.shape[1] >= WIN + Q_BLOCK
    assert (x.shape[0] * x.shape[1]) % (SC_CORES * SC_SUBCORES * SC_CHUNK) == 0
    return _layer(x, p[0], norm_mix[0], w_in[0], gmlp_v_norm[0], gmlp_w_s[0], gmlp_b_s[0],
                  cmp_pe_k[0], cmp_w1_k[0], cmp_w2_k[0], cmp_pe_v[0], cmp_w1_v[0], cmp_w2_v[0],
                  out_norm_a[0], out_norm_b[0], w_o[0], norm_moe[0], router_group[0], router_expert[0],
                  moe_w_gate[0], moe_w_up[0], moe_w_down[0], norm_ple[0], w_ple_proj[0], w_ple_gate[0],
                  norm_final)
```

```python
import functools

import numpy as np
import jax
import jax.numpy as jnp
from jax import lax
from jax.experimental import pallas as pl
from jax.experimental.pallas import tpu as pltpu
from jax.experimental.pallas import tpu_sc as plsc

D_MODEL = 1024
HEAD_DIM = 64
A_HEADS = 8
A_WIDTH = A_HEADS * HEAD_DIM
B_HEADS = 8
B_WIDTH = B_HEADS * HEAD_DIM
B_KV = 2
B_HPG = B_HEADS // B_KV
KV_W = B_KV * HEAD_DIM
N_GATES = B_HEADS * 3
CHUNK = 128
L_CMP = 32
STRIDE_CMP = 16
CMP_HIDDEN = 256
L_SEL = 64
N_SEL = 16
WIN = 512
WIN_Q = 128
Q_BLOCK = 512
ROPE_THETA = 10000.0
N_GROUPS = 4
EXPERTS_PER_GROUP = 4
N_EXPERTS = N_GROUPS * EXPERTS_PER_GROUP
D_FF_EXPERT = 512
D_PLE = 256
EPS = 1e-6
LOG2E = 1.4426950408889634
NEG = -1e30
FORCE = 1e6

OFF_Q = 2 * A_WIDTH
OFF_KV = OFF_Q + B_WIDTH
OFF_GATE = OFF_KV + 6 * KV_W
D_IN = OFF_GATE + N_GATES

LANES = 128
GATE_PAD = LANES
ROUTER_OFF = N_GROUPS
W_ALL = OFF_GATE + GATE_PAD

TM_PROJ = 1024
TM_POST = 1024
TM_ROUTE = 512
TR_GMM = 512
SUB_MOE = 512
RT_W = 2 * N_EXPERTS
SC_CORES = 2
SC_SUBCORES = 16
SC_CHUNK = 128
SEL_CK = 512
VMEM_LIMIT = 56 * 1024 * 1024
VMEM_LIMIT_ATTN = 60 * 1024 * 1024

_PERM_HEADS = [0, 4, 1, 5, 2, 6, 3, 7]


def _perm_heads(a, axis):
    return jnp.concatenate([lax.slice_in_dim(a, h * HEAD_DIM, (h + 1) * HEAD_DIM, axis=axis) for h in _PERM_HEADS],
                           axis=axis)

_F32 = jnp.float32
_BF16 = jnp.bfloat16


def _dot(a, b):
    return jnp.dot(a, b, preferred_element_type=_F32)


def _dot_nt(a, b):
    return lax.dot_general(a, b, (((1,), (1,)), ((), ())), preferred_element_type=_F32)


def _rms(x, g):
    return x * lax.rsqrt(jnp.mean(x * x, axis=-1, keepdims=True) + EPS) * g


def _gelu(x):
    return 0.5 * x * (1.0 + jnp.tanh(0.7978845608028654 * (x + 0.044715 * (x * x * x))))


def _rope_tile(x, cos, sin_signed):
    lane = lax.broadcasted_iota(jnp.int32, x.shape, 1)
    first_half = (lane % HEAD_DIM) < (HEAD_DIM // 2)
    rot = jnp.where(first_half, pltpu.roll(x, LANES - HEAD_DIM // 2, 1), pltpu.roll(x, HEAD_DIM // 2, 1))
    return x * cos + rot * sin_signed


def _proj_kernel(x_ref, gmix_ref, w_ref, cos_ref, sin_ref, gv_ref, ws_ref, bs_ref, goa_ref,
                 oa_ref, q_ref, kc_ref, vc_ref, ks0_ref, ks1_ref, vs0_ref, vs1_ref, kw_ref, vw_ref, gate_ref,
                 *, seq):
    tm = x_ref.shape[0]
    hn = _rms(x_ref[...], gmix_ref[...]).astype(_BF16)
    cos = cos_ref[...]
    sin = sin_ref[...]

    zu = _gelu(_dot(hn, w_ref[:, 0:A_WIDTH]))
    zv = _gelu(_dot(hn, w_ref[:, A_WIDTH:2 * A_WIDTH]))
    vn = _rms(zv, gv_ref[...]).astype(_BF16)

    t_io = lax.broadcasted_iota(jnp.int32, (CHUNK, 2 * CHUNK), 0)
    s_io = lax.broadcasted_iota(jnp.int32, (CHUNK, 2 * CHUNK), 1) % CHUNK
    causal = s_io <= t_io
    lane = lax.broadcasted_iota(jnp.int32, (CHUNK, LANES), 1)
    lo = lane < HEAD_DIM
    bs = bs_ref[...]
    n_chunks = tm // CHUNK
    pair_cols = []
    for pr in range(A_HEADS // 2):
        wcat = jnp.where(causal, ws_ref[pr], 0.0).astype(_BF16)
        rhs = []
        for c in range(n_chunks):
            vblk = vn[c * CHUNK:(c + 1) * CHUNK, pr * LANES:(pr + 1) * LANES]
            zero = jnp.zeros_like(vblk)
            rhs.append(jnp.concatenate([jnp.where(lo, vblk, zero), jnp.where(lo, zero, vblk)], axis=0))
        out = _dot(wcat, jnp.concatenate(rhs, axis=1))
        pair_cols.append(jnp.concatenate([out[:, c * LANES:(c + 1) * LANES] for c in range(n_chunks)], axis=0))
    mixed = jnp.concatenate(pair_cols, axis=1) + jnp.concatenate([bs] * n_chunks, axis=0)
    oa = zu * mixed
    oa_ref[...] = _rms(oa, goa_ref[...]).astype(oa_ref.dtype)

    zq = _dot(hn, w_ref[:, OFF_Q:OFF_KV])
    scale = HEAD_DIM ** -0.5 * LOG2E
    for j in range(B_WIDTH // LANES):
        blk = _rope_tile(zq[:, j * LANES:(j + 1) * LANES], cos, sin) * scale
        q_ref[:, j * LANES:(j + 1) * LANES] = blk.astype(q_ref.dtype)

    zkv = _dot(hn, w_ref[:, OFF_KV:OFF_GATE])
    kv = []
    for j in range(6):
        blk = zkv[:, j * KV_W:(j + 1) * KV_W]
        kv.append(_rope_tile(blk, cos, sin) if j % 2 == 0 else blk)
    k_cmp, v_cmp, k_slc, v_slc, k_win, v_win = kv
    kc_ref[...] = k_cmp.astype(kc_ref.dtype)
    vc_ref[...] = v_cmp.astype(vc_ref.dtype)
    kw_ref[...] = k_win.astype(kw_ref.dtype)
    vw_ref[...] = v_win.astype(vw_ref.dtype)

    row = lax.broadcasted_iota(jnp.int32, (tm, LANES), 0)
    lane = lax.broadcasted_iota(jnp.int32, (tm, LANES), 1)
    key_block = ((pl.program_id(0) % (seq // tm)) * tm + row) // L_SEL
    onehot = jnp.where(lane == key_block, 1.0, 0.0).astype(ks0_ref.dtype)
    for g, (ks_ref, vs_ref) in enumerate(((ks0_ref, vs0_ref), (ks1_ref, vs1_ref))):
        own = (lane // HEAD_DIM) == g
        ks_ref[:, :LANES] = jnp.where(own, k_slc, 0.0).astype(ks_ref.dtype)
        ks_ref[:, LANES:] = onehot
        vs_ref[...] = jnp.where(own, v_slc, 1.0).astype(vs_ref.dtype)

    zg = _dot(hn, w_ref[:, OFF_GATE:W_ALL])
    gate_ref[...] = jax.nn.sigmoid(zg)


def _proj_call(x2, gmix, w_all, cos_t, sin_t, gv, ws_pairs, bs_exp, goa, seq):
    n = x2.shape[0]
    tm = TM_PROJ
    n_t = seq // tm
    row = lambda i: (i, 0)
    const2 = lambda i: (0, 0)
    pos = lambda i: (i % n_t, 0)
    out_shapes = [jax.ShapeDtypeStruct((n, A_WIDTH), _BF16), jax.ShapeDtypeStruct((n, B_WIDTH), _BF16)]
    kv_widths = [KV_W, KV_W, KV_W + LANES, KV_W + LANES, KV_W, KV_W, KV_W, KV_W]
    out_shapes += [jax.ShapeDtypeStruct((n, w), _BF16) for w in kv_widths]
    out_shapes += [jax.ShapeDtypeStruct((n, GATE_PAD), _F32)]
    out_specs = [pl.BlockSpec((tm, A_WIDTH), row), pl.BlockSpec((tm, B_WIDTH), row)]
    out_specs += [pl.BlockSpec((tm, w), row) for w in kv_widths]
    out_specs += [pl.BlockSpec((tm, GATE_PAD), row)]
    return pl.pallas_call(
        functools.partial(_proj_kernel, seq=seq),
        grid=(n // tm,),
        in_specs=[
            pl.BlockSpec((tm, D_MODEL), row),
            pl.BlockSpec((1, D_MODEL), const2),
            pl.BlockSpec((D_MODEL, W_ALL), const2),
            pl.BlockSpec((tm, LANES), pos),
            pl.BlockSpec((tm, LANES), pos),
            pl.BlockSpec((1, A_WIDTH), const2),
            pl.BlockSpec((A_HEADS // 2, CHUNK, 2 * CHUNK), lambda i: (0, 0, 0)),
            pl.BlockSpec((CHUNK, A_WIDTH), const2),
            pl.BlockSpec((1, A_WIDTH), const2),
        ],
        out_specs=out_specs,
        out_shape=out_shapes,
        compiler_params=pltpu.CompilerParams(dimension_semantics=("arbitrary",), vmem_limit_bytes=VMEM_LIMIT),
        name="proj",
    )(x2, gmix, w_all, cos_t, sin_t, gv, ws_pairs, bs_exp, goa)


def _compress_kernel(rk_ref, rv_ref, pek_ref, pev_ref, w1k_ref, w1v_ref,
                     tk_ref, bk_ref, w2k_ref, tv_ref, bv_ref, w2v_ref, kc_ref, vc_ref):
    def one(r_ref, pe_ref, w1_ref, top_ref, bot_ref, w2_ref, o_ref):
        r = r_ref[0]
        nr = r.shape[0]
        a = _dot(r, top_ref[...])
        b = _dot(r, bot_ref[...])
        pe_h = _dot(pe_ref[...], w1_ref[...])
        pe2 = jnp.concatenate([pe_h[0:1], pe_h[0:1]], axis=1)
        hid = a + pltpu.roll(b, nr - 1, 0) + pe2
        o_ref[0] = _dot(_gelu(hid).astype(_BF16), w2_ref[...]).astype(o_ref.dtype)

    one(rk_ref, pek_ref, w1k_ref, tk_ref, bk_ref, w2k_ref, kc_ref)
    one(rv_ref, pev_ref, w1v_ref, tv_ref, bv_ref, w2v_ref, vc_ref)


def _compress_weights(w1, w2, pe):
    half = L_CMP // 2
    w1r = w1.reshape(L_CMP, HEAD_DIM, CMP_HIDDEN)
    eye = jnp.eye(B_KV, dtype=w1.dtype)
    place = lambda part: jnp.einsum('ldj,gh->lgdhj', part, eye).reshape(half * KV_W, B_KV * CMP_HIDDEN)
    top = place(w1r[:half]).astype(_BF16)
    bot = place(w1r[half:]).astype(_BF16)
    w2bd = jnp.einsum('jd,gh->gjhd', w2, eye).reshape(B_KV * CMP_HIDDEN, KV_W).astype(_BF16)
    pe8 = jnp.broadcast_to(pe.reshape(1, L_CMP * HEAD_DIM), (8, L_CMP * HEAD_DIM)).astype(_BF16)
    return pe8, w1.astype(_BF16), top, bot, w2bd


def _compress_call(k_cmp, v_cmp, wk, wv, bsz, seq):
    nr = seq // STRIDE_CMP
    rk = k_cmp.reshape(bsz, nr, STRIDE_CMP * KV_W)
    rv = v_cmp.reshape(bsz, nr, STRIDE_CMP * KV_W)
    pek, w1k, tk, bk, w2k = wk
    pev, w1v, tv, bv, w2v = wv
    rspec = pl.BlockSpec((1, nr, STRIDE_CMP * KV_W), lambda b: (b, 0, 0))
    full = lambda a: pl.BlockSpec(a.shape, lambda b: (0,) * a.ndim)
    ospec = pl.BlockSpec((1, nr, KV_W), lambda b: (b, 0, 0))
    return pl.pallas_call(
        _compress_kernel,
        grid=(bsz,),
        in_specs=[rspec, rspec, full(pek), full(pev), full(w1k), full(w1v),
                  full(tk), full(bk), full(w2k), full(tv), full(bv), full(w2v)],
        out_specs=[ospec, ospec],
        out_shape=[jax.ShapeDtypeStruct((bsz, nr, KV_W), _BF16)] * 2,
        compiler_params=pltpu.CompilerParams(dimension_semantics=("arbitrary",), vmem_limit_bytes=VMEM_LIMIT),
        name="compress",
    )(rk, rv, pek, pev, w1k, w1v, tk, bk, w2k, tv, bv, w2v)


def _topk_rows_mask(sc_t, k):
    n_rows = sc_t.shape[0]
    row = lax.broadcasted_iota(jnp.int32, sc_t.shape, 0).astype(sc_t.dtype)
    taken = jnp.asarray(-3e38, sc_t.dtype)
    for _ in range(k):
        m = jnp.max(sc_t, axis=0, keepdims=True)
        idx = jnp.min(jnp.where(sc_t == m, row, jnp.asarray(n_rows, sc_t.dtype)), axis=0, keepdims=True)
        sc_t = jnp.where(row == idx, taken, sc_t)
    return jnp.where(sc_t == taken, 1.0, 0.0).astype(_F32)


def _attn_kernel(q_ref, gate_ref, gexp_ref, kc_ref, vc_ref, ks0_ref, ks1_ref, vs0_ref, vs1_ref,
                 kw_ref, vw_ref, gob_ref, *rest, seq, n_cast):
    cast_in, o_ref, cast_out = rest[:n_cast], rest[n_cast], rest[n_cast + 1:2 * n_cast + 1]
    m_scr, acc_scr = rest[2 * n_cast + 1:]
    for src, dst in zip(cast_in, cast_out):
        dst[...] = src[...].astype(dst.dtype)

    qb = pl.program_id(1)
    t0 = qb * Q_BLOCK
    n_cmp = kc_ref.shape[1]
    n_sb = seq // L_SEL
    k_top = min(N_SEL, n_sb)
    rows = B_HPG * Q_BLOCK
    ks_refs, vs_refs = (ks0_ref, ks1_ref), (vs0_ref, vs1_ref)

    lane_q = lax.broadcasted_iota(jnp.int32, (Q_BLOCK, LANES), 1)
    lo = lane_q < HEAD_DIM
    t_col = t0 + lax.broadcasted_iota(jnp.int32, (Q_BLOCK, 1), 0)

    def per_head(x):
        return x.reshape(B_HPG, Q_BLOCK, x.shape[-1])

    def add_bias(s, bias):
        return (per_head(s) + bias[None]).reshape(rows, s.shape[-1])

    qs = []
    for g in range(B_KV):
        own = (lane_q // HEAD_DIM) == g
        qs.append(jnp.concatenate(
            [jnp.where(own, q_ref[0, :, j * LANES:(j + 1) * LANES], jnp.zeros((), q_ref.dtype))
             for j in range(B_HPG)], axis=0))

    thr0 = (t0 - (L_CMP - 1)) // STRIDE_CMP
    n_thr = (Q_BLOCK - 1) // STRIDE_CMP + 2
    assert n_thr <= LANES
    thr_rel = (t_col - (L_CMP - 1)) // STRIDE_CMP - thr0
    q_thr = jnp.where(lane_q == thr_rel, 1.0, 0.0).astype(_BF16)
    c_row1 = lax.broadcasted_iota(jnp.int32, (n_cmp, LANES), 0)
    c_lane1 = lax.broadcasted_iota(jnp.int32, (n_cmp, LANES), 1)
    k_thr = jnp.where((c_lane1 < n_thr) & (c_row1 > thr0 + c_lane1), NEG, 0.0).astype(_BF16)
    kc_wide = jnp.concatenate([kc_ref[0], k_thr], axis=1)
    has_c = (t_col >= L_CMP - 1).astype(_F32)
    c_row = lax.broadcasted_iota(jnp.int32, (n_cmp, n_sb), 0) * STRIDE_CMP
    s_col = lax.broadcasted_iota(jnp.int32, (n_cmp, n_sb), 1) * L_SEL
    overlap = jnp.where((c_row < s_col + L_SEL) & (c_row + L_CMP > s_col), 1.0, 0.0).astype(_BF16)
    blk = lax.broadcasted_iota(jnp.int32, (Q_BLOCK, n_sb), 1)
    cur = t_col // L_SEL
    forced = (blk == 0) | (blk == cur) | (blk == cur - 1)
    valid = blk * L_SEL <= t_col

    o_c, sel_bias = [], []
    for g in range(B_KV):
        s_c = _dot_nt(jnp.concatenate([qs[g], jnp.concatenate([q_thr] * B_HPG, axis=0)], axis=1), kc_wide)
        e_c = jnp.exp2(s_c - jnp.max(s_c, axis=-1, keepdims=True))
        inv = per_head(1.0 / jnp.maximum(jnp.sum(e_c, axis=-1, keepdims=True), 1e-30)) * has_c[None]
        e_bf = e_c.astype(_BF16)
        o_c.append((per_head(_dot(e_bf, vc_ref[0])) * inv).reshape(rows, LANES))
        p_bf = per_head(e_bf) * inv.astype(_BF16)
        imp = _dot(functools.reduce(lambda a, b: a + b, [p_bf[j] for j in range(B_HPG)]), overlap)
        score = jnp.where(valid & jnp.logical_not(forced), imp, -FORCE)
        chosen = forced | (_topk_rows_mask(score.T, k_top - 3).T > 0.5)
        sb = jnp.where(chosen & valid, 0.0, NEG)
        if n_sb < LANES:
            sb = jnp.concatenate([sb, jnp.full((Q_BLOCK, LANES - n_sb), NEG, _F32)], axis=1)
        sel_bias.append(sb)

    w_len = WIN + WIN_Q
    lane_w = lax.broadcasted_iota(jnp.int32, (w_len, LANES), 1)
    acc_w_sub = [[] for _ in range(B_KV)]
    for hh in range(Q_BLOCK // WIN_Q):
        w_start = pl.multiple_of(jnp.maximum(t0 + hh * WIN_Q - WIN, 0), WIN_Q)
        t_sub = t_col[hh * WIN_Q:(hh + 1) * WIN_Q]
        diff_w = t_sub - (w_start + lax.broadcasted_iota(jnp.int32, (WIN_Q, w_len), 1))
        bias_w = jnp.where((diff_w >= 0) & (diff_w < WIN), 0.0, NEG)
        kw = kw_ref[0, pl.ds(w_start, w_len), :]
        vw = vw_ref[0, pl.ds(w_start, w_len), :]
        for g in range(B_KV):
            q_sub = per_head(qs[g])[:, hh * WIN_Q:(hh + 1) * WIN_Q, :].reshape(B_HPG * WIN_Q, LANES)
            s_w = (_dot_nt(q_sub, kw).reshape(B_HPG, WIN_Q, w_len) + bias_w[None]).reshape(B_HPG * WIN_Q, w_len)
            e_w = jnp.exp2(s_w - jnp.max(s_w, axis=-1, keepdims=True)).astype(_BF16)
            vw_aug = jnp.where((lane_w // HEAD_DIM) == g, vw, jnp.ones((), vw.dtype))
            acc_w_sub[g].append(_dot(e_w, vw_aug).reshape(B_HPG, WIN_Q, LANES))
    acc_w = [jnp.concatenate(acc_w_sub[g], axis=1).reshape(rows, LANES) for g in range(B_KV)]

    n_ck = (t0 + Q_BLOCK + SEL_CK - 1) // SEL_CK
    key_lane = lax.broadcasted_iota(jnp.int32, (Q_BLOCK, SEL_CK), 1)
    bias_diag = jnp.where((n_ck - 1) * SEL_CK + key_lane <= t_col, 0.0, NEG)
    q_wide = [jnp.concatenate([qs[g], jnp.concatenate([sel_bias[g].astype(_BF16)] * B_HPG, axis=0)], axis=1)
              for g in range(B_KV)]

    m_scr[...] = jnp.full(m_scr.shape, NEG, _F32)
    acc_scr[...] = jnp.zeros(acc_scr.shape, _F32)

    def sel_chunk(ci, diag):
        k0 = pl.multiple_of(ci * SEL_CK, SEL_CK)
        for g in range(B_KV):
            m = m_scr[g]
            s = _dot_nt(q_wide[g], ks_refs[g][0, pl.ds(k0, SEL_CK), :])
            if diag:
                s = add_bias(s, bias_diag)
            m_new = jnp.maximum(m, jnp.max(s, axis=-1, keepdims=True))
            p = jnp.exp2(s - jnp.concatenate([m_new] * (SEL_CK // LANES), axis=1)).astype(_BF16)
            acc_scr[g] = jnp.exp2(m - m_new) * acc_scr[g] + _dot(p, vs_refs[g][0, pl.ds(k0, SEL_CK), :])
            m_scr[g] = m_new

    @pl.loop(0, n_ck - 1)
    def _(ci):
        sel_chunk(ci, False)

    sel_chunk(n_ck - 1, True)
    acc_s = [acc_scr[g] for g in range(B_KV)]

    def numer(acc):
        return jnp.concatenate([jnp.where(lo, acc[0][j * Q_BLOCK:(j + 1) * Q_BLOCK],
                                          acc[1][j * Q_BLOCK:(j + 1) * Q_BLOCK]) for j in range(B_HPG)], axis=1)

    def denom(acc):
        return jnp.concatenate([pltpu.roll(jnp.where(lo, acc[1][j * Q_BLOCK:(j + 1) * Q_BLOCK],
                                                     acc[0][j * Q_BLOCK:(j + 1) * Q_BLOCK]), HEAD_DIM, 1)
                                for j in range(B_HPG)], axis=1)

    gates = gate_ref[0]
    g_hi = gates.astype(_BF16)
    g_lo = (gates - g_hi.astype(_F32)).astype(_BF16)
    gate_of = lambda r: _dot(g_hi, gexp_ref[r]) + _dot(g_lo, gexp_ref[r])
    ob = (gate_of(0) * numer(o_c)
          + gate_of(1) * numer(acc_s) * (1.0 / jnp.maximum(denom(acc_s), 1e-30))
          + gate_of(2) * numer(acc_w) * (1.0 / jnp.maximum(denom(acc_w), 1e-30)))
    o_ref[0] = _rms(ob, gob_ref[...]).astype(o_ref.dtype)


def _gate_expand():
    x = np.zeros((3, GATE_PAD, B_WIDTH), np.float32)
    for slot, h in enumerate(_PERM_HEADS):
        for r in range(3):
            x[r, 3 * h + r, slot * HEAD_DIM:(slot + 1) * HEAD_DIM] = 1.0
    return jnp.asarray(x, _BF16)


def _attn_call(q, gates, kc, vc, ks0, ks1, vs0, vs1, kw, vw, gob, to_cast, bsz, seq):
    assert seq // L_SEL <= LANES
    n_cmp = kc.shape[1]
    n_q = seq // Q_BLOCK
    steps = bsz * n_q
    qspec = lambda w: pl.BlockSpec((1, Q_BLOCK, w), lambda b, i: (b, i, 0))
    full = lambda r, w=KV_W: pl.BlockSpec((1, r, w), lambda b, i: (b, 0, 0), pipeline_mode=pl.Buffered(1))
    sliced = [a.reshape(steps, a.size // (steps * a.shape[-1]), a.shape[-1]) for a in to_cast]
    cast_specs = [pl.BlockSpec((1,) + a.shape[1:], lambda b, i: (b * n_q + i, 0, 0)) for a in sliced]
    outs = pl.pallas_call(
        functools.partial(_attn_kernel, seq=seq, n_cast=len(sliced)),
        grid=(bsz, n_q),
        in_specs=[qspec(B_WIDTH), qspec(GATE_PAD),
                  pl.BlockSpec((3, GATE_PAD, B_WIDTH), lambda b, i: (0, 0, 0)),
                  full(n_cmp), full(n_cmp), full(seq, KV_W + LANES), full(seq, KV_W + LANES),
                  full(seq), full(seq), full(seq), full(seq),
                  pl.BlockSpec((1, B_WIDTH), lambda b, i: (0, 0))] + cast_specs,
        out_specs=[qspec(B_WIDTH)] + cast_specs,
        out_shape=[jax.ShapeDtypeStruct((bsz, seq, B_WIDTH), _BF16)]
        + [jax.ShapeDtypeStruct(a.shape, _BF16) for a in sliced],
        scratch_shapes=[pltpu.VMEM((B_KV, B_HPG * Q_BLOCK, LANES), _F32),
                        pltpu.VMEM((B_KV, B_HPG * Q_BLOCK, LANES), _F32)],
        compiler_params=pltpu.CompilerParams(dimension_semantics=("arbitrary", "arbitrary"),
                                             vmem_limit_bytes=VMEM_LIMIT_ATTN),
        name="attn",
    )(q, gates, _gate_expand(), kc, vc, ks0, ks1, vs0, vs1, kw, vw, gob, *sliced)
    return outs[0], [o.reshape(a.shape) for o, a in zip(outs[1:], to_cast)]


def _pack_halves(x):
    w = x.shape[1] // 2
    bits = lambda v: lax.bitcast_convert_type(v.astype(_BF16).astype(_F32), jnp.uint32)
    return lax.bitcast_convert_type(bits(x[:, :w]) | (bits(x[:, w:]) >> 16), jnp.int32)


def _unpack_halves(p):
    u = lax.bitcast_convert_type(p, jnp.uint32)
    return (lax.bitcast_convert_type(u & jnp.uint32(0xFFFF0000), _F32),
            lax.bitcast_convert_type(u << 16, _F32))


def _post_kernel(x_ref, oa_ref, ob_ref, woa_ref, wob_ref, gmoe_ref, r_ref, h1_ref, hn_ref, rt_ref, cnt_ref):
    h1 = x_ref[...] + _dot(oa_ref[...], woa_ref[...]) + _dot(ob_ref[...], wob_ref[...])
    h1_ref[...] = h1
    hn = _rms(h1, gmoe_ref[...])
    hn_ref[...] = _pack_halves(hn)

    hn_hi = hn.astype(_BF16)
    hn_lo = (hn - hn_hi.astype(_F32)).astype(_BF16)
    hi_both = _dot(hn_hi, r_ref[...])
    logits = hi_both[:, :GATE_PAD] + (_dot(hn_lo, r_ref[:, :GATE_PAD]) + hi_both[:, GATE_PAD:])
    lane = lax.broadcasted_iota(jnp.int32, logits.shape, 1)
    first_idx = lambda hit: jnp.min(jnp.where(hit, lane, LANES), axis=-1, keepdims=True)

    is_g = lane < N_GROUPS
    lg = jnp.where(is_g, logits, NEG)
    mg = jnp.max(lg, axis=-1, keepdims=True)
    sg = jnp.sum(jnp.where(is_g, jnp.exp(lg - mg), 0.0), axis=-1, keepdims=True)
    pg_top = 1.0 / sg
    g_sel = first_idx(is_g & (lg == mg))

    e_lo = ROUTER_OFF + g_sel * EXPERTS_PER_GROUP
    is_e = (lane >= e_lo) & (lane < e_lo + EXPERTS_PER_GROUP)
    le = jnp.where(is_e, logits, NEG)
    m1 = jnp.max(le, axis=-1, keepdims=True)
    se = jnp.sum(jnp.where(is_e, jnp.exp(le - m1), 0.0), axis=-1, keepdims=True)
    i1 = first_idx(is_e & (le == m1))
    le2 = jnp.where(lane == i1, NEG, le)
    m2 = jnp.max(le2, axis=-1, keepdims=True)
    i2 = first_idx(is_e & (lane != i1) & (le2 == m2))
    pe1 = 1.0 / se
    pe2 = jnp.exp(m2 - m1) / se
    denom = pe1 + pe2
    rt = (jnp.where(lane == i1 - ROUTER_OFF, 1.0, 0.0)
          + jnp.where(lane == i2 - ROUTER_OFF + N_EXPERTS, 1.0, 0.0)
          + jnp.where(lane == RT_W, pg_top * pe1 / denom, 0.0)
          + jnp.where(lane == RT_W + 1, pg_top * pe2 / denom, 0.0))
    rt_ref[...] = rt

    @pl.when(pl.program_id(0) == 0)
    def _():
        cnt_ref[...] = jnp.zeros_like(cnt_ref)

    cnt_ref[...] += jnp.sum(rt, axis=0, keepdims=True)


def _post_call(x2, oa, ob, woa, wob, gmoe, r_cat):
    n = x2.shape[0]
    tm = TM_POST
    row = lambda i: (i, 0)
    const2 = lambda i: (0, 0)
    return pl.pallas_call(
        _post_kernel,
        grid=(n // tm,),
        in_specs=[pl.BlockSpec((tm, D_MODEL), row), pl.BlockSpec((tm, A_WIDTH), row),
                  pl.BlockSpec((tm, B_WIDTH), row), pl.BlockSpec((A_WIDTH, D_MODEL), const2),
                  pl.BlockSpec((B_WIDTH, D_MODEL), const2), pl.BlockSpec((1, D_MODEL), const2),
                  pl.BlockSpec((D_MODEL, 2 * GATE_PAD), const2)],
        out_specs=[pl.BlockSpec((tm, D_MODEL), row), pl.BlockSpec((tm, D_MODEL // 2), row),
                   pl.BlockSpec((tm, GATE_PAD), row), pl.BlockSpec((8, LANES), const2)],
        out_shape=[jax.ShapeDtypeStruct((n, D_MODEL), _F32), jax.ShapeDtypeStruct((n, D_MODEL // 2), jnp.int32),
                   jax.ShapeDtypeStruct((n, GATE_PAD), _F32), jax.ShapeDtypeStruct((8, LANES), _F32)],
        compiler_params=pltpu.CompilerParams(dimension_semantics=("arbitrary",), vmem_limit_bytes=VMEM_LIMIT),
        name="post",
    )(x2, oa, ob, woa, wob, gmoe, r_cat)


def _route_kernel(rt_ref, cnt_ref, dest_ref, meta_ref, off_ref, run_ref):
    tm = rt_ref.shape[0]
    lane = lax.broadcasted_iota(jnp.int32, (1, LANES), 1)
    first = lane < N_EXPERTS
    onehot = jnp.where(lane < 2 * N_EXPERTS, rt_ref[...], 0.0)

    @pl.when(pl.program_id(0) == 0)
    def _():
        cnt = jnp.where(lane < 2 * N_EXPERTS, cnt_ref[...], 0.0)
        c1 = jnp.where(first, cnt, 0.0)
        tot = c1 + jnp.where(first, pltpu.roll(cnt, LANES - N_EXPERTS, 1), 0.0)
        tiles = jnp.floor((tot + (TR_GMM - 1)) * (1.0 / TR_GMM))
        e_row = lax.broadcasted_iota(jnp.int32, (LANES, LANES), 0)
        e_col = lax.broadcasted_iota(jnp.int32, (LANES, LANES), 1)
        before = jnp.where(e_row < e_col, 1.0, 0.0).astype(_BF16)
        base = _dot(tiles.astype(_BF16), before) * TR_GMM
        off_ref[...] = jnp.where(first, base, 0.0) + pltpu.roll(jnp.where(first, base + c1, 0.0), N_EXPERTS, 1)
        run_ref[...] = jnp.zeros_like(run_ref)
        meta_ref[...] = tiles

    r_io = lax.broadcasted_iota(jnp.int32, (tm, tm), 0)
    c_io = lax.broadcasted_iota(jnp.int32, (tm, tm), 1)
    earlier = jnp.where(c_io < r_io, 1.0, 0.0).astype(_BF16)
    rank = _dot(earlier, onehot.astype(_BF16)) + run_ref[0:1, :]
    slot = onehot * (rank + off_ref[0:1, :])
    d1 = jnp.sum(jnp.where(first, slot, 0.0), axis=-1, keepdims=True)
    d2 = jnp.sum(jnp.where(first, 0.0, slot), axis=-1, keepdims=True)
    dest_ref[...] = (jnp.where(lane == 0, d1, 0.0) + jnp.where(lane == 1, d2, 0.0)).astype(jnp.int32)
    run_ref[...] += jnp.sum(onehot, axis=0, keepdims=True)


def _route_call(rt, cnt):
    n = rt.shape[0]
    tm = TM_ROUTE
    return pl.pallas_call(
        _route_kernel,
        grid=(n // tm,),
        in_specs=[pl.BlockSpec((tm, GATE_PAD), lambda i: (i, 0)), pl.BlockSpec((8, LANES), lambda i: (0, 0))],
        out_specs=[pl.BlockSpec((tm, LANES), lambda i: (i, 0)), pl.BlockSpec((8, LANES), lambda i: (0, 0))],
        out_shape=[jax.ShapeDtypeStruct((n, LANES), jnp.int32), jax.ShapeDtypeStruct((8, LANES), _F32)],
        scratch_shapes=[pltpu.VMEM((8, LANES), _F32)] * 2,
        compiler_params=pltpu.CompilerParams(dimension_semantics=("arbitrary",)),
        name="route",
    )(rt, cnt)


def _sc_mesh():
    return plsc.VectorSubcoreMesh(core_axis_name="c", subcore_axis_name="s")


def _sc_worker(n_rows):
    per = n_rows // (SC_CORES * SC_SUBCORES)
    return (lax.axis_index("s") * SC_CORES + lax.axis_index("c")) * per, per


def _dispatch_call(xp, d1, d2, n_slots):
    n, w = xp.shape
    assert n % (SC_CORES * SC_SUBCORES * SC_CHUNK) == 0

    @functools.partial(
        pl.kernel, mesh=_sc_mesh(), out_type=jax.ShapeDtypeStruct((n_slots, w), xp.dtype),
        scratch_types=[pltpu.VMEM((SC_CHUNK,), jnp.int32), pltpu.VMEM((SC_CHUNK,), jnp.int32),
                       pltpu.VMEM((SC_CHUNK, w), xp.dtype), pltpu.SemaphoreType.DMA],
        name="dispatch")
    def k(x_hbm, d1_hbm, d2_hbm, xs_hbm, i1_v, i2_v, rows_v, sem):
        row0, per = _sc_worker(n)

        @pl.loop(0, per // SC_CHUNK)
        def _(j):
            src = pl.ds(row0 + j * SC_CHUNK, SC_CHUNK)
            pltpu.sync_copy(d1_hbm.at[src], i1_v)
            pltpu.sync_copy(d2_hbm.at[src], i2_v)
            pltpu.sync_copy(x_hbm.at[src], rows_v)
            first = pltpu.async_copy(rows_v, xs_hbm.at[i1_v], sem)
            second = pltpu.async_copy(rows_v, xs_hbm.at[i2_v], sem)
            first.wait()
            second.wait()

    return k(xp, d1, d2)


def _combine_call(ys, d1, d2):
    n = d1.shape[0]
    w = ys.shape[1]
    assert n % (SC_CORES * SC_SUBCORES * SC_CHUNK) == 0
    out = jax.ShapeDtypeStruct((n, w), ys.dtype)

    @functools.partial(
        pl.kernel, mesh=_sc_mesh(), out_type=(out, out),
        scratch_types=[pltpu.VMEM((SC_CHUNK,), jnp.int32), pltpu.VMEM((SC_CHUNK, w), ys.dtype),
                       pltpu.SemaphoreType.DMA],
        name="combine")
    def k(ys_hbm, d1_hbm, d2_hbm, y1_hbm, y2_hbm, i_v, rows_v, sem):
        row0, per = _sc_worker(n)

        @pl.loop(0, per // SC_CHUNK)
        def _(j):
            dst = pl.ds(row0 + j * SC_CHUNK, SC_CHUNK)
            for d_hbm, y_hbm in ((d1_hbm, y1_hbm), (d2_hbm, y2_hbm)):
                pltpu.sync_copy(d_hbm.at[dst], i_v)
                pltpu.async_copy(ys_hbm.at[i_v], rows_v, sem).wait()
                pltpu.sync_copy(rows_v, y_hbm.at[dst])

    return k(ys, d1, d2)


def _gmm_kernel(te_ref, nu_ref, xs_ref, wg_ref, wu_ref, wd_ref, ys_ref):
    half = D_MODEL // 2

    @pl.when(pl.program_id(0) < nu_ref[0])
    def _():
        for r in range(TR_GMM // SUB_MOE):
            rows = slice(r * SUB_MOE, (r + 1) * SUB_MOE)
            a, b = _unpack_halves(xs_ref[rows, :])
            a, b = a.astype(_BF16), b.astype(_BF16)
            gate = _dot(a, wg_ref[0, :half, :]) + _dot(b, wg_ref[0, half:, :])
            up = _dot(a, wu_ref[0, :half, :]) + _dot(b, wu_ref[0, half:, :])
            hid = gate * jax.nn.sigmoid(gate) * up
            ys_ref[rows, :] = _pack_halves(_dot(hid.astype(_BF16), wd_ref[0]))


def _gmm_call(tile_expert, n_used, xs, w_gate, w_up, w_down):
    n_slots, w = xs.shape
    rows = lambda t, te, nu: (jnp.minimum(t, nu[0] - 1), 0)
    expert = lambda t, te, nu: (te[t], 0, 0)
    return pl.pallas_call(
        _gmm_kernel,
        grid_spec=pltpu.PrefetchScalarGridSpec(
            num_scalar_prefetch=2, grid=(n_slots // TR_GMM,),
            in_specs=[pl.BlockSpec((TR_GMM, w), rows),
                      pl.BlockSpec((1, D_MODEL, D_FF_EXPERT), expert),
                      pl.BlockSpec((1, D_MODEL, D_FF_EXPERT), expert),
                      pl.BlockSpec((1, D_FF_EXPERT, D_MODEL), expert)],
            out_specs=pl.BlockSpec((TR_GMM, w), rows)),
        out_shape=jax.ShapeDtypeStruct((n_slots, w), xs.dtype),
        compiler_params=pltpu.CompilerParams(dimension_semantics=("arbitrary",), vmem_limit_bytes=VMEM_LIMIT),
        name="gmm",
    )(tile_expert, n_used, xs, w_gate, w_up, w_down)


def _final_kernel(h1_ref, y1_ref, y2_ref, rt_ref, p_ref, gple_ref, wpg_ref, wpp_ref, gfin_ref, o_ref):
    for r in range(h1_ref.shape[0] // SUB_MOE):
        rows = slice(r * SUB_MOE, (r + 1) * SUB_MOE)
        rt = rt_ref[rows, :]
        w1, w2 = rt[:, RT_W:RT_W + 1], rt[:, RT_W + 1:RT_W + 2]
        a1, b1 = _unpack_halves(y1_ref[rows, :])
        a2, b2 = _unpack_halves(y2_ref[rows, :])
        h2 = h1_ref[rows, :] + jnp.concatenate([w1 * a1 + w2 * a2, w1 * b1 + w2 * b2], axis=1)
        gate = jax.nn.sigmoid(_dot(_rms(h2, gple_ref[...]).astype(_BF16), wpg_ref[...]))
        h3 = h2 + _dot(p_ref[rows, :].astype(_BF16), wpp_ref[...]) * gate
        o_ref[rows, :] = _rms(h3, gfin_ref[...])


def _final_call(h1, y1, y2, rt, p2, gple, wpg, wpp, gfin):
    n = h1.shape[0]
    tm = TM_POST
    row = lambda i: (i, 0)
    const2 = lambda i: (0, 0)
    return pl.pallas_call(
        _final_kernel,
        grid=(n // tm,),
        in_specs=[pl.BlockSpec((tm, D_MODEL), row), pl.BlockSpec((tm, D_MODEL // 2), row),
                  pl.BlockSpec((tm, D_MODEL // 2), row), pl.BlockSpec((tm, GATE_PAD), row),
                  pl.BlockSpec((tm, D_PLE), row), pl.BlockSpec((1, D_MODEL), const2),
                  pl.BlockSpec((D_MODEL, D_MODEL), const2), pl.BlockSpec((D_PLE, D_MODEL), const2),
                  pl.BlockSpec((1, D_MODEL), const2)],
        out_specs=pl.BlockSpec((tm, D_MODEL), row),
        out_shape=jax.ShapeDtypeStruct((n, D_MODEL), _F32),
        compiler_params=pltpu.CompilerParams(dimension_semantics=("arbitrary",), vmem_limit_bytes=VMEM_LIMIT),
        name="final",
    )(h1, y1, y2, rt, p2, gple, wpg, wpp, gfin)


def _rope_tables(seq):
    half = HEAD_DIM // 2
    inv = 1.0 / (ROPE_THETA ** (jnp.arange(half, dtype=_F32) / half))
    ang = jnp.arange(seq, dtype=_F32)[:, None] * inv[None, :]
    cos, sin = jnp.cos(ang), jnp.sin(ang)
    reps = LANES // HEAD_DIM
    cos_t = jnp.tile(jnp.concatenate([cos, cos], axis=1), (1, reps))
    sin_t = jnp.tile(jnp.concatenate([-sin, sin], axis=1), (1, reps))
    return cos_t, sin_t


def _layer(h, p_i, norm_mix, w_in, gmlp_v_norm, gmlp_w_s, gmlp_b_s,
           cmp_pe_k, cmp_w1_k, cmp_w2_k, cmp_pe_v, cmp_w1_v, cmp_w2_v,
           out_norm_a, out_norm_b, w_o, norm_moe, router_group, router_expert,
           moe_w_gate, moe_w_up, moe_w_down, norm_ple, w_ple_proj, w_ple_gate, norm_final):
    bsz, seq, _ = h.shape
    n = bsz * seq
    x2 = h.reshape(n, D_MODEL)
    row = lambda v: v.reshape(1, -1).astype(_F32)

    w_q = _perm_heads(w_in[:, OFF_Q:OFF_KV], 1)
    w_gate = jnp.pad(w_in[:, OFF_GATE:D_IN], ((0, 0), (0, GATE_PAD - N_GATES)))
    w_all = jnp.concatenate([w_in[:, :OFF_Q], w_q, w_in[:, OFF_KV:OFF_GATE], w_gate], axis=1).astype(_BF16)
    cos_t, sin_t = _rope_tables(seq)
    ws_pairs = gmlp_w_s.reshape(A_HEADS // 2, 2, CHUNK, CHUNK).transpose(0, 2, 1, 3).reshape(
        A_HEADS // 2, CHUNK, 2 * CHUNK)
    bs_exp = jnp.repeat(gmlp_b_s.T, HEAD_DIM, axis=1)

    oa, q, k_cmp, v_cmp, ks0, ks1, vs0, vs1, k_win, v_win, gates = _proj_call(
        x2, row(norm_mix), w_all, cos_t, sin_t, row(gmlp_v_norm), ws_pairs, bs_exp, row(out_norm_a), seq)

    kc, vc = _compress_call(k_cmp, v_cmp, _compress_weights(cmp_w1_k, cmp_w2_k, cmp_pe_k),
                            _compress_weights(cmp_w1_v, cmp_w2_v, cmp_pe_v), bsz, seq)

    b3 = lambda a: a.reshape(bsz, seq, a.shape[-1])
    ob, (wg_bf, wu_bf, wd_bf) = _attn_call(
        b3(q), b3(gates), kc, vc, b3(ks0), b3(ks1), b3(vs0), b3(vs1), b3(k_win), b3(v_win),
        row(_perm_heads(out_norm_b, 0)), [moe_w_gate, moe_w_up, moe_w_down], bsz, seq)

    r_cat = jnp.pad(jnp.concatenate([router_group, router_expert], axis=1),
                    ((0, 0), (0, GATE_PAD - N_GROUPS - N_EXPERTS)))
    r_hi = r_cat.astype(_BF16)
    r_cat = jnp.concatenate([r_hi, (r_cat - r_hi.astype(_F32)).astype(_BF16)], axis=1)
    h1, xp, rt, cnt = _post_call(x2, oa, ob.reshape(n, B_WIDTH), w_o[:A_WIDTH].astype(_BF16),
                            _perm_heads(w_o[A_WIDTH:], 0).astype(_BF16), row(norm_moe), r_cat)

    dest, meta = _route_call(rt, cnt)
    d1, d2 = dest[:, 0], dest[:, 1]
    n_tiles = 2 * n // TR_GMM + N_EXPERTS
    ends = jnp.cumsum(meta[0, :N_EXPERTS].astype(jnp.int32))
    tile_expert = jnp.minimum(jnp.sum(ends[None, :] <= jnp.arange(n_tiles)[:, None], axis=1),
                              N_EXPERTS - 1).astype(jnp.int32)
    xs = _dispatch_call(xp, d1, d2, n_tiles * TR_GMM)
    ys = _gmm_call(tile_expert, ends[-1:], xs, wg_bf, wu_bf, wd_bf)
    y1, y2 = _combine_call(ys, d1, d2)
    out = _final_call(h1, y1, y2, rt, p_i.reshape(n, D_PLE), row(norm_ple),
                      w_ple_gate.astype(_BF16), w_ple_proj.astype(_BF16), row(norm_final))
    return out.reshape(bsz, seq, D_MODEL)


def kernel(x, p, norm_mix, w_in, gmlp_v_norm, gmlp_w_s, gmlp_b_s, cmp_pe_k, cmp_w1_k, cmp_w2_k,
           cmp_pe_v, cmp_w1_v, cmp_w2_v, out_norm_a, out_norm_b, w_o, norm_moe, router_group,
           router_expert, moe_w_gate, moe_w_up, moe_w_down, norm_ple, w_ple_proj, w_ple_gate, norm_final):
    assert p.shape[0] == 1, "single-layer trunk"
    assert x.shape[1] % SEL_CK == 0 and x.shape[1] >= WIN + Q_BLOCK
    assert (x.shape[0] * x.shape[1]) % (SC_CORES * SC_SUBCORES * SC_CHUNK) == 0
    return _layer(x, p[0], norm_mix[0], w_in[0], gmlp_v_norm[0], gmlp_w_s[0], gmlp_b_s[0],
                  cmp_pe_k[0], cmp_w1_k[0], cmp_w2_k[0], cmp_pe_v[0], cmp_w1_v[0], cmp_w2_v[0],
                  out_norm_a[0], out_norm_b[0], w_o[0], norm_moe[0], router_group[0], router_expert[0],
                  moe_w_gate[0], moe_w_up[0], moe_w_down[0], norm_ple[0], w_ple_proj[0], w_ple_gate[0],
                  norm_final)
```

```python
import functools

import numpy as np
import jax
import jax.numpy as jnp
from jax import lax
from jax.experimental import pallas as pl
from jax.experimental.pallas import tpu as pltpu
from jax.experimental.pallas import tpu_sc as plsc

D_MODEL = 1024
HEAD_DIM = 64
A_HEADS = 8
A_WIDTH = A_HEADS * HEAD_DIM
B_HEADS = 8
B_WIDTH = B_HEADS * HEAD_DIM
B_KV = 2
B_HPG = B_HEADS // B_KV
KV_W = B_KV * HEAD_DIM
N_GATES = B_HEADS * 3
CHUNK = 128
L_CMP = 32
STRIDE_CMP = 16
CMP_HIDDEN = 256
L_SEL = 64
N_SEL = 16
WIN = 512
WIN_Q = 128
Q_BLOCK = 512
ROPE_THETA = 10000.0
N_GROUPS = 4
EXPERTS_PER_GROUP = 4
N_EXPERTS = N_GROUPS * EXPERTS_PER_GROUP
D_FF_EXPERT = 512
D_PLE = 256
EPS = 1e-6
LOG2E = 1.4426950408889634
NEG = -1e30
FORCE = 1e6

OFF_Q = 2 * A_WIDTH
OFF_KV = OFF_Q + B_WIDTH
OFF_GATE = OFF_KV + 6 * KV_W
D_IN = OFF_GATE + N_GATES

LANES = 128
GATE_PAD = LANES
ROUTER_OFF = N_GROUPS
W_ALL = OFF_GATE + GATE_PAD

TM_PROJ = 1024
TM_POST = 1024
TM_ROUTE = 512
TR_GMM = 512
SUB_MOE = 512
RT_W = 2 * N_EXPERTS
SC_CORES = 2
SC_SUBCORES = 16
SC_CHUNK = 128
SEL_CK = 512
VMEM_LIMIT = 56 * 1024 * 1024
VMEM_LIMIT_ATTN = 60 * 1024 * 1024

_PERM_HEADS = [0, 4, 1, 5, 2, 6, 3, 7]


def _perm_heads(a, axis):
    return jnp.concatenate([lax.slice_in_dim(a, h * HEAD_DIM, (h + 1) * HEAD_DIM, axis=axis) for h in _PERM_HEADS],
                           axis=axis)

_F32 = jnp.float32
_BF16 = jnp.bfloat16


def _dot(a, b):
    return jnp.dot(a, b, preferred_element_type=_F32)


def _dot_nt(a, b):
    return lax.dot_general(a, b, (((1,), (1,)), ((), ())), preferred_element_type=_F32)


def _rms(x, g):
    return x * lax.rsqrt(jnp.mean(x * x, axis=-1, keepdims=True) + EPS) * g


def _gelu(x):
    return 0.5 * x * (1.0 + jnp.tanh(0.7978845608028654 * (x + 0.044715 * (x * x * x))))


def _rope_tile(x, cos, sin_signed):
    lane = lax.broadcasted_iota(jnp.int32, x.shape, 1)
    first_half = (lane % HEAD_DIM) < (HEAD_DIM // 2)
    rot = jnp.where(first_half, pltpu.roll(x, LANES - HEAD_DIM // 2, 1), pltpu.roll(x, HEAD_DIM // 2, 1))
    return x * cos + rot * sin_signed


def _proj_kernel(x_ref, gmix_ref, w_ref, cos_ref, sin_ref, gv_ref, ws_ref, bs_ref, goa_ref,
                 oa_ref, q_ref, kc_ref, vc_ref, ks_ref, vs0_ref, vs1_ref, kw_ref, vw_ref, gate_ref,
                 *, seq):
    tm = x_ref.shape[0]
    hn = _rms(x_ref[...], gmix_ref[...]).astype(_BF16)
    cos = cos_ref[...]
    sin = sin_ref[...]

    zu = _gelu(_dot(hn, w_ref[:, 0:A_WIDTH]))
    zv = _gelu(_dot(hn, w_ref[:, A_WIDTH:2 * A_WIDTH]))
    vn = _rms(zv, gv_ref[...]).astype(_BF16)

    t_io = lax.broadcasted_iota(jnp.int32, (CHUNK, 2 * CHUNK), 0)
    s_io = lax.broadcasted_iota(jnp.int32, (CHUNK, 2 * CHUNK), 1) % CHUNK
    causal = s_io <= t_io
    lane = lax.broadcasted_iota(jnp.int32, (CHUNK, LANES), 1)
    lo = lane < HEAD_DIM
    bs = bs_ref[...]
    n_chunks = tm // CHUNK
    pair_cols = []
    for pr in range(A_HEADS // 2):
        wcat = jnp.where(causal, ws_ref[pr], 0.0).astype(_BF16)
        rhs = []
        for c in range(n_chunks):
            vblk = vn[c * CHUNK:(c + 1) * CHUNK, pr * LANES:(pr + 1) * LANES]
            zero = jnp.zeros_like(vblk)
            rhs.append(jnp.concatenate([jnp.where(lo, vblk, zero), jnp.where(lo, zero, vblk)], axis=0))
        out = _dot(wcat, jnp.concatenate(rhs, axis=1))
        pair_cols.append(jnp.concatenate([out[:, c * LANES:(c + 1) * LANES] for c in range(n_chunks)], axis=0))
    mixed = jnp.concatenate(pair_cols, axis=1) + jnp.concatenate([bs] * n_chunks, axis=0)
    oa = zu * mixed
    oa_ref[...] = _rms(oa, goa_ref[...]).astype(oa_ref.dtype)

    zq = _dot(hn, w_ref[:, OFF_Q:OFF_KV])
    scale = HEAD_DIM ** -0.5 * LOG2E
    for j in range(B_WIDTH // LANES):
        blk = _rope_tile(zq[:, j * LANES:(j + 1) * LANES], cos, sin) * scale
        q_ref[:, j * LANES:(j + 1) * LANES] = blk.astype(q_ref.dtype)

    zkv = _dot(hn, w_ref[:, OFF_KV:OFF_GATE])
    kv = []
    for j in range(6):
        blk = zkv[:, j * KV_W:(j + 1) * KV_W]
        kv.append(_rope_tile(blk, cos, sin) if j % 2 == 0 else blk)
    k_cmp, v_cmp, k_slc, v_slc, k_win, v_win = kv
    kc_ref[...] = k_cmp.astype(kc_ref.dtype)
    vc_ref[...] = v_cmp.astype(vc_ref.dtype)
    kw_ref[...] = k_win.astype(kw_ref.dtype)
    vw_ref[...] = v_win.astype(vw_ref.dtype)

    row = lax.broadcasted_iota(jnp.int32, (tm, LANES), 0)
    lane = lax.broadcasted_iota(jnp.int32, (tm, LANES), 1)
    key_block = ((pl.program_id(0) % (seq // tm)) * tm + row) // L_SEL
    ks_ref[:, :LANES] = k_slc.astype(ks_ref.dtype)
    ks_ref[:, LANES:] = jnp.where(lane == key_block, 1.0, 0.0).astype(ks_ref.dtype)
    for g, vs_ref in enumerate((vs0_ref, vs1_ref)):
        vs_ref[...] = jnp.where((lane // HEAD_DIM) == g, v_slc, 1.0).astype(vs_ref.dtype)

    zg = _dot(hn, w_ref[:, OFF_GATE:W_ALL])
    gate_ref[...] = jax.nn.sigmoid(zg)


def _proj_call(x2, gmix, w_all, cos_t, sin_t, gv, ws_pairs, bs_exp, goa, seq):
    n = x2.shape[0]
    tm = TM_PROJ
    n_t = seq // tm
    row = lambda i: (i, 0)
    const2 = lambda i: (0, 0)
    pos = lambda i: (i % n_t, 0)
    out_shapes = [jax.ShapeDtypeStruct((n, A_WIDTH), _BF16), jax.ShapeDtypeStruct((n, B_WIDTH), _BF16)]
    kv_widths = [KV_W, KV_W, KV_W + LANES, KV_W, KV_W, KV_W, KV_W]
    out_shapes += [jax.ShapeDtypeStruct((n, w), _BF16) for w in kv_widths]
    out_shapes += [jax.ShapeDtypeStruct((n, GATE_PAD), _F32)]
    out_specs = [pl.BlockSpec((tm, A_WIDTH), row), pl.BlockSpec((tm, B_WIDTH), row)]
    out_specs += [pl.BlockSpec((tm, w), row) for w in kv_widths]
    out_specs += [pl.BlockSpec((tm, GATE_PAD), row)]
    return pl.pallas_call(
        functools.partial(_proj_kernel, seq=seq),
        grid=(n // tm,),
        in_specs=[
            pl.BlockSpec((tm, D_MODEL), row),
            pl.BlockSpec((1, D_MODEL), const2),
            pl.BlockSpec((D_MODEL, W_ALL), const2),
            pl.BlockSpec((tm, LANES), pos),
            pl.BlockSpec((tm, LANES), pos),
            pl.BlockSpec((1, A_WIDTH), const2),
            pl.BlockSpec((A_HEADS // 2, CHUNK, 2 * CHUNK), lambda i: (0, 0, 0)),
            pl.BlockSpec((CHUNK, A_WIDTH), const2),
            pl.BlockSpec((1, A_WIDTH), const2),
        ],
        out_specs=out_specs,
        out_shape=out_shapes,
        compiler_params=pltpu.CompilerParams(dimension_semantics=("arbitrary",), vmem_limit_bytes=VMEM_LIMIT),
        name="proj",
    )(x2, gmix, w_all, cos_t, sin_t, gv, ws_pairs, bs_exp, goa)


def _compress_kernel(rk_ref, rv_ref, pek_ref, pev_ref, w1k_ref, w1v_ref,
                     tk_ref, bk_ref, w2k_ref, tv_ref, bv_ref, w2v_ref, kc_ref, vc_ref):
    def one(r_ref, pe_ref, w1_ref, top_ref, bot_ref, w2_ref, o_ref):
        r = r_ref[0]
        nr = r.shape[0]
        a = _dot(r, top_ref[...])
        b = _dot(r, bot_ref[...])
        pe_h = _dot(pe_ref[...], w1_ref[...])
        pe2 = jnp.concatenate([pe_h[0:1], pe_h[0:1]], axis=1)
        hid = a + pltpu.roll(b, nr - 1, 0) + pe2
        o_ref[0] = _dot(_gelu(hid).astype(_BF16), w2_ref[...]).astype(o_ref.dtype)

    one(rk_ref, pek_ref, w1k_ref, tk_ref, bk_ref, w2k_ref, kc_ref)
    one(rv_ref, pev_ref, w1v_ref, tv_ref, bv_ref, w2v_ref, vc_ref)


def _compress_weights(w1, w2, pe):
    half = L_CMP // 2
    w1r = w1.reshape(L_CMP, HEAD_DIM, CMP_HIDDEN)
    eye = jnp.eye(B_KV, dtype=w1.dtype)
    place = lambda part: jnp.einsum('ldj,gh->lgdhj', part, eye).reshape(half * KV_W, B_KV * CMP_HIDDEN)
    top = place(w1r[:half]).astype(_BF16)
    bot = place(w1r[half:]).astype(_BF16)
    w2bd = jnp.einsum('jd,gh->gjhd', w2, eye).reshape(B_KV * CMP_HIDDEN, KV_W).astype(_BF16)
    pe8 = jnp.broadcast_to(pe.reshape(1, L_CMP * HEAD_DIM), (8, L_CMP * HEAD_DIM)).astype(_BF16)
    return pe8, w1.astype(_BF16), top, bot, w2bd


def _compress_call(k_cmp, v_cmp, wk, wv, bsz, seq):
    nr = seq // STRIDE_CMP
    rk = k_cmp.reshape(bsz, nr, STRIDE_CMP * KV_W)
    rv = v_cmp.reshape(bsz, nr, STRIDE_CMP * KV_W)
    pek, w1k, tk, bk, w2k = wk
    pev, w1v, tv, bv, w2v = wv
    rspec = pl.BlockSpec((1, nr, STRIDE_CMP * KV_W), lambda b: (b, 0, 0))
    full = lambda a: pl.BlockSpec(a.shape, lambda b: (0,) * a.ndim)
    ospec = pl.BlockSpec((1, nr, KV_W), lambda b: (b, 0, 0))
    return pl.pallas_call(
        _compress_kernel,
        grid=(bsz,),
        in_specs=[rspec, rspec, full(pek), full(pev), full(w1k), full(w1v),
                  full(tk), full(bk), full(w2k), full(tv), full(bv), full(w2v)],
        out_specs=[ospec, ospec],
        out_shape=[jax.ShapeDtypeStruct((bsz, nr, KV_W), _BF16)] * 2,
        compiler_params=pltpu.CompilerParams(dimension_semantics=("arbitrary",), vmem_limit_bytes=VMEM_LIMIT),
        name="compress",
    )(rk, rv, pek, pev, w1k, w1v, tk, bk, w2k, tv, bv, w2v)


def _topk_rows_mask(sc_t, k):
    n_rows = sc_t.shape[0]
    row = lax.broadcasted_iota(jnp.int32, sc_t.shape, 0).astype(sc_t.dtype)
    taken = jnp.asarray(-3e38, sc_t.dtype)
    for _ in range(k):
        m = jnp.max(sc_t, axis=0, keepdims=True)
        idx = jnp.min(jnp.where(sc_t == m, row, jnp.asarray(n_rows, sc_t.dtype)), axis=0, keepdims=True)
        sc_t = jnp.where(row == idx, taken, sc_t)
    return jnp.where(sc_t == taken, 1.0, 0.0).astype(_F32)


def _attn_kernel(q_ref, gate_ref, gexp_ref, kc_ref, vc_ref, ks_ref, vs0_ref, vs1_ref,
                 kw_ref, vw_ref, gob_ref, *rest, seq, n_cast):
    cast_in, o_ref, cast_out = rest[:n_cast], rest[n_cast], rest[n_cast + 1:2 * n_cast + 1]
    m_scr, acc_scr = rest[2 * n_cast + 1:]
    for src, dst in zip(cast_in, cast_out):
        dst[...] = src[...].astype(dst.dtype)

    qb = pl.program_id(1)
    t0 = qb * Q_BLOCK
    n_cmp = kc_ref.shape[1]
    n_sb = seq // L_SEL
    k_top = min(N_SEL, n_sb)
    rows = B_HPG * Q_BLOCK
    vs_refs = (vs0_ref, vs1_ref)

    lane_q = lax.broadcasted_iota(jnp.int32, (Q_BLOCK, LANES), 1)
    lo = lane_q < HEAD_DIM
    t_col = t0 + lax.broadcasted_iota(jnp.int32, (Q_BLOCK, 1), 0)

    def per_head(x):
        return x.reshape(B_HPG, Q_BLOCK, x.shape[-1])

    def add_bias(s, bias):
        return (per_head(s) + bias[None]).reshape(rows, s.shape[-1])

    qs = []
    for g in range(B_KV):
        own = (lane_q // HEAD_DIM) == g
        qs.append(jnp.concatenate(
            [jnp.where(own, q_ref[0, :, j * LANES:(j + 1) * LANES], jnp.zeros((), q_ref.dtype))
             for j in range(B_HPG)], axis=0))

    thr0 = (t0 - (L_CMP - 1)) // STRIDE_CMP
    n_thr = (Q_BLOCK - 1) // STRIDE_CMP + 2
    assert n_thr <= LANES
    thr_rel = (t_col - (L_CMP - 1)) // STRIDE_CMP - thr0
    q_thr = jnp.where(lane_q == thr_rel, 1.0, 0.0).astype(_BF16)
    c_row1 = lax.broadcasted_iota(jnp.int32, (n_cmp, LANES), 0)
    c_lane1 = lax.broadcasted_iota(jnp.int32, (n_cmp, LANES), 1)
    k_thr = jnp.where((c_lane1 < n_thr) & (c_row1 > thr0 + c_lane1), NEG, 0.0).astype(_BF16)
    kc_wide = jnp.concatenate([kc_ref[0], k_thr], axis=1)
    has_c = (t_col >= L_CMP - 1).astype(_F32)
    c_row = lax.broadcasted_iota(jnp.int32, (n_cmp, n_sb), 0) * STRIDE_CMP
    s_col = lax.broadcasted_iota(jnp.int32, (n_cmp, n_sb), 1) * L_SEL
    overlap = jnp.where((c_row < s_col + L_SEL) & (c_row + L_CMP > s_col), 1.0, 0.0).astype(_BF16)
    blk = lax.broadcasted_iota(jnp.int32, (Q_BLOCK, n_sb), 1)
    cur = t_col // L_SEL
    forced = (blk == 0) | (blk == cur) | (blk == cur - 1)
    valid = blk * L_SEL <= t_col

    o_c, sel_bias = [], []
    for g in range(B_KV):
        s_c = _dot_nt(jnp.concatenate([qs[g], jnp.concatenate([q_thr] * B_HPG, axis=0)], axis=1), kc_wide)
        e_c = jnp.exp2(s_c - jnp.max(s_c, axis=-1, keepdims=True))
        inv = per_head(1.0 / jnp.maximum(jnp.sum(e_c, axis=-1, keepdims=True), 1e-30)) * has_c[None]
        e_bf = e_c.astype(_BF16)
        o_c.append((per_head(_dot(e_bf, vc_ref[0])) * inv).reshape(rows, LANES))
        p_bf = per_head(e_bf) * inv.astype(_BF16)
        imp = _dot(functools.reduce(lambda a, b: a + b, [p_bf[j] for j in range(B_HPG)]), overlap)
        score = jnp.where(valid & jnp.logical_not(forced), imp, -FORCE)
        chosen = forced | (_topk_rows_mask(score.T, k_top - 3).T > 0.5)
        sb = jnp.where(chosen & valid, 0.0, NEG)
        if n_sb < LANES:
            sb = jnp.concatenate([sb, jnp.full((Q_BLOCK, LANES - n_sb), NEG, _F32)], axis=1)
        sel_bias.append(sb)

    w_len = WIN + WIN_Q
    lane_w = lax.broadcasted_iota(jnp.int32, (w_len, LANES), 1)
    acc_w_sub = [[] for _ in range(B_KV)]
    for hh in range(Q_BLOCK // WIN_Q):
        w_start = pl.multiple_of(jnp.maximum(t0 + hh * WIN_Q - WIN, 0), WIN_Q)
        t_sub = t_col[hh * WIN_Q:(hh + 1) * WIN_Q]
        diff_w = t_sub - (w_start + lax.broadcasted_iota(jnp.int32, (WIN_Q, w_len), 1))
        bias_w = jnp.where((diff_w >= 0) & (diff_w < WIN), 0.0, NEG)
        kw = kw_ref[0, pl.ds(w_start, w_len), :]
        vw = vw_ref[0, pl.ds(w_start, w_len), :]
        for g in range(B_KV):
            q_sub = per_head(qs[g])[:, hh * WIN_Q:(hh + 1) * WIN_Q, :].reshape(B_HPG * WIN_Q, LANES)
            s_w = (_dot_nt(q_sub, kw).reshape(B_HPG, WIN_Q, w_len) + bias_w[None]).reshape(B_HPG * WIN_Q, w_len)
            e_w = jnp.exp2(s_w - jnp.max(s_w, axis=-1, keepdims=True)).astype(_BF16)
            vw_aug = jnp.where((lane_w // HEAD_DIM) == g, vw, jnp.ones((), vw.dtype))
            acc_w_sub[g].append(_dot(e_w, vw_aug).reshape(B_HPG, WIN_Q, LANES))
    acc_w = [jnp.concatenate(acc_w_sub[g], axis=1).reshape(rows, LANES) for g in range(B_KV)]

    n_ck = (t0 + Q_BLOCK + SEL_CK - 1) // SEL_CK
    key_lane = lax.broadcasted_iota(jnp.int32, (Q_BLOCK, SEL_CK), 1)
    bias_diag = jnp.where((n_ck - 1) * SEL_CK + key_lane <= t_col, 0.0, NEG)
    q_wide = [jnp.concatenate([qs[g], jnp.concatenate([sel_bias[g].astype(_BF16)] * B_HPG, axis=0)], axis=1)
              for g in range(B_KV)]

    m_scr[...] = jnp.full(m_scr.shape, NEG, _F32)
    acc_scr[...] = jnp.zeros(acc_scr.shape, _F32)

    def sel_chunk(ci, diag):
        k0 = pl.multiple_of(ci * SEL_CK, SEL_CK)
        for g in range(B_KV):
            m = m_scr[g]
            s = _dot_nt(q_wide[g], ks_ref[0, pl.ds(k0, SEL_CK), :])
            if diag:
                s = add_bias(s, bias_diag)
            m_new = jnp.maximum(m, jnp.max(s, axis=-1, keepdims=True))
            p = jnp.exp2(s - jnp.concatenate([m_new] * (SEL_CK // LANES), axis=1)).astype(_BF16)
            acc_scr[g] = jnp.exp2(m - m_new) * acc_scr[g] + _dot(p, vs_refs[g][0, pl.ds(k0, SEL_CK), :])
            m_scr[g] = m_new

    @pl.loop(0, n_ck - 1)
    def _(ci):
        sel_chunk(ci, False)

    sel_chunk(n_ck - 1, True)
    acc_s = [acc_scr[g] for g in range(B_KV)]

    def numer(acc):
        return jnp.concatenate([jnp.where(lo, acc[0][j * Q_BLOCK:(j + 1) * Q_BLOCK],
                                          acc[1][j * Q_BLOCK:(j + 1) * Q_BLOCK]) for j in range(B_HPG)], axis=1)

    def denom(acc):
        return jnp.concatenate([pltpu.roll(jnp.where(lo, acc[1][j * Q_BLOCK:(j + 1) * Q_BLOCK],
                                                     acc[0][j * Q_BLOCK:(j + 1) * Q_BLOCK]), HEAD_DIM, 1)
                                for j in range(B_HPG)], axis=1)

    gates = gate_ref[0]
    g_hi = gates.astype(_BF16)
    g_lo = (gates - g_hi.astype(_F32)).astype(_BF16)
    gate_of = lambda r: _dot(g_hi, gexp_ref[r]) + _dot(g_lo, gexp_ref[r])
    ob = (gate_of(0) * numer(o_c)
          + gate_of(1) * numer(acc_s) * (1.0 / jnp.maximum(denom(acc_s), 1e-30))
          + gate_of(2) * numer(acc_w) * (1.0 / jnp.maximum(denom(acc_w), 1e-30)))
    o_ref[0] = _rms(ob, gob_ref[...]).astype(o_ref.dtype)


def _gate_expand():
    x = np.zeros((3, GATE_PAD, B_WIDTH), np.float32)
    for slot, h in enumerate(_PERM_HEADS):
        for r in range(3):
            x[r, 3 * h + r, slot * HEAD_DIM:(slot + 1) * HEAD_DIM] = 1.0
    return jnp.asarray(x, _BF16)


def _attn_call(q, gates, kc, vc, ks, vs0, vs1, kw, vw, gob, to_cast, bsz, seq):
    assert seq // L_SEL <= LANES
    n_cmp = kc.shape[1]
    n_q = seq // Q_BLOCK
    steps = bsz * n_q
    qspec = lambda w: pl.BlockSpec((1, Q_BLOCK, w), lambda b, i: (b, i, 0))
    full = lambda r, w=KV_W: pl.BlockSpec((1, r, w), lambda b, i: (b, 0, 0), pipeline_mode=pl.Buffered(1))
    sliced = [a.reshape(steps, a.size // (steps * a.shape[-1]), a.shape[-1]) for a in to_cast]
    cast_specs = [pl.BlockSpec((1,) + a.shape[1:], lambda b, i: (b * n_q + i, 0, 0)) for a in sliced]
    outs = pl.pallas_call(
        functools.partial(_attn_kernel, seq=seq, n_cast=len(sliced)),
        grid=(bsz, n_q),
        in_specs=[qspec(B_WIDTH), qspec(GATE_PAD),
                  pl.BlockSpec((3, GATE_PAD, B_WIDTH), lambda b, i: (0, 0, 0)),
                  full(n_cmp), full(n_cmp), full(seq, KV_W + LANES),
                  full(seq), full(seq), full(seq), full(seq),
                  pl.BlockSpec((1, B_WIDTH), lambda b, i: (0, 0))] + cast_specs,
        out_specs=[qspec(B_WIDTH)] + cast_specs,
        out_shape=[jax.ShapeDtypeStruct((bsz, seq, B_WIDTH), _BF16)]
        + [jax.ShapeDtypeStruct(a.shape, _BF16) for a in sliced],
        scratch_shapes=[pltpu.VMEM((B_KV, B_HPG * Q_BLOCK, LANES), _F32),
                        pltpu.VMEM((B_KV, B_HPG * Q_BLOCK, LANES), _F32)],
        compiler_params=pltpu.CompilerParams(dimension_semantics=("arbitrary", "arbitrary"),
                                             vmem_limit_bytes=VMEM_LIMIT_ATTN),
        name="attn",
    )(q, gates, _gate_expand(), kc, vc, ks, vs0, vs1, kw, vw, gob, *sliced)
    return outs[0], [o.reshape(a.shape) for o, a in zip(outs[1:], to_cast)]


def _pack_halves(x):
    w = x.shape[1] // 2
    bits = lambda v: lax.bitcast_convert_type(v.astype(_BF16).astype(_F32), jnp.uint32)
    return lax.bitcast_convert_type(bits(x[:, :w]) | (bits(x[:, w:]) >> 16), jnp.int32)


def _unpack_halves(p):
    u = lax.bitcast_convert_type(p, jnp.uint32)
    return (lax.bitcast_convert_type(u & jnp.uint32(0xFFFF0000), _F32),
            lax.bitcast_convert_type(u << 16, _F32))


def _post_kernel(x_ref, oa_ref, ob_ref, woa_ref, wob_ref, gmoe_ref, r_ref, h1_ref, hn_ref, rt_ref, cnt_ref):
    h1 = x_ref[...] + _dot(oa_ref[...], woa_ref[...]) + _dot(ob_ref[...], wob_ref[...])
    h1_ref[...] = h1
    hn = _rms(h1, gmoe_ref[...])
    hn_ref[...] = _pack_halves(hn)

    hn_hi = hn.astype(_BF16)
    hn_lo = (hn - hn_hi.astype(_F32)).astype(_BF16)
    hi_both = _dot(hn_hi, r_ref[...])
    logits = hi_both[:, :GATE_PAD] + (_dot(hn_lo, r_ref[:, :GATE_PAD]) + hi_both[:, GATE_PAD:])
    lane = lax.broadcasted_iota(jnp.int32, logits.shape, 1)
    first_idx = lambda hit: jnp.min(jnp.where(hit, lane, LANES), axis=-1, keepdims=True)

    is_g = lane < N_GROUPS
    lg = jnp.where(is_g, logits, NEG)
    mg = jnp.max(lg, axis=-1, keepdims=True)
    sg = jnp.sum(jnp.where(is_g, jnp.exp(lg - mg), 0.0), axis=-1, keepdims=True)
    pg_top = 1.0 / sg
    g_sel = first_idx(is_g & (lg == mg))

    e_lo = ROUTER_OFF + g_sel * EXPERTS_PER_GROUP
    is_e = (lane >= e_lo) & (lane < e_lo + EXPERTS_PER_GROUP)
    le = jnp.where(is_e, logits, NEG)
    m1 = jnp.max(le, axis=-1, keepdims=True)
    se = jnp.sum(jnp.where(is_e, jnp.exp(le - m1), 0.0), axis=-1, keepdims=True)
    i1 = first_idx(is_e & (le == m1))
    le2 = jnp.where(lane == i1, NEG, le)
    m2 = jnp.max(le2, axis=-1, keepdims=True)
    i2 = first_idx(is_e & (lane != i1) & (le2 == m2))
    pe1 = 1.0 / se
    pe2 = jnp.exp(m2 - m1) / se
    denom = pe1 + pe2
    rt = (jnp.where(lane == i1 - ROUTER_OFF, 1.0, 0.0)
          + jnp.where(lane == i2 - ROUTER_OFF + N_EXPERTS, 1.0, 0.0)
          + jnp.where(lane == RT_W, pg_top * pe1 / denom, 0.0)
          + jnp.where(lane == RT_W + 1, pg_top * pe2 / denom, 0.0))
    rt_ref[...] = rt

    @pl.when(pl.program_id(0) == 0)
    def _():
        cnt_ref[...] = jnp.zeros_like(cnt_ref)

    cnt_ref[...] += jnp.sum(rt, axis=0, keepdims=True)


def _post_call(x2, oa, ob, woa, wob, gmoe, r_cat):
    n = x2.shape[0]
    tm = TM_POST
    row = lambda i: (i, 0)
    const2 = lambda i: (0, 0)
    return pl.pallas_call(
        _post_kernel,
        grid=(n // tm,),
        in_specs=[pl.BlockSpec((tm, D_MODEL), row), pl.BlockSpec((tm, A_WIDTH), row),
                  pl.BlockSpec((tm, B_WIDTH), row), pl.BlockSpec((A_WIDTH, D_MODEL), const2),
                  pl.BlockSpec((B_WIDTH, D_MODEL), const2), pl.BlockSpec((1, D_MODEL), const2),
                  pl.BlockSpec((D_MODEL, 2 * GATE_PAD), const2)],
        out_specs=[pl.BlockSpec((tm, D_MODEL), row), pl.BlockSpec((tm, D_MODEL // 2), row),
                   pl.BlockSpec((tm, GATE_PAD), row), pl.BlockSpec((8, LANES), const2)],
        out_shape=[jax.ShapeDtypeStruct((n, D_MODEL), _F32), jax.ShapeDtypeStruct((n, D_MODEL // 2), jnp.int32),
                   jax.ShapeDtypeStruct((n, GATE_PAD), _F32), jax.ShapeDtypeStruct((8, LANES), _F32)],
        compiler_params=pltpu.CompilerParams(dimension_semantics=("arbitrary",), vmem_limit_bytes=VMEM_LIMIT),
        name="post",
    )(x2, oa, ob, woa, wob, gmoe, r_cat)


def _route_kernel(rt_ref, cnt_ref, dest_ref, meta_ref, off_ref, run_ref):
    tm = rt_ref.shape[0]
    lane = lax.broadcasted_iota(jnp.int32, (1, LANES), 1)
    first = lane < N_EXPERTS
    onehot = jnp.where(lane < 2 * N_EXPERTS, rt_ref[...], 0.0)

    @pl.when(pl.program_id(0) == 0)
    def _():
        cnt = jnp.where(lane < 2 * N_EXPERTS, cnt_ref[...], 0.0)
        c1 = jnp.where(first, cnt, 0.0)
        tot = c1 + jnp.where(first, pltpu.roll(cnt, LANES - N_EXPERTS, 1), 0.0)
        tiles = jnp.floor((tot + (TR_GMM - 1)) * (1.0 / TR_GMM))
        e_row = lax.broadcasted_iota(jnp.int32, (LANES, LANES), 0)
        e_col = lax.broadcasted_iota(jnp.int32, (LANES, LANES), 1)
        before = jnp.where(e_row < e_col, 1.0, 0.0).astype(_BF16)
        base = _dot(tiles.astype(_BF16), before) * TR_GMM
        off_ref[...] = jnp.where(first, base, 0.0) + pltpu.roll(jnp.where(first, base + c1, 0.0), N_EXPERTS, 1)
        run_ref[...] = jnp.zeros_like(run_ref)
        meta_ref[...] = tiles

    r_io = lax.broadcasted_iota(jnp.int32, (tm, tm), 0)
    c_io = lax.broadcasted_iota(jnp.int32, (tm, tm), 1)
    earlier = jnp.where(c_io < r_io, 1.0, 0.0).astype(_BF16)
    rank = _dot(earlier, onehot.astype(_BF16)) + run_ref[0:1, :]
    slot = onehot * (rank + off_ref[0:1, :])
    d1 = jnp.sum(jnp.where(first, slot, 0.0), axis=-1, keepdims=True)
    d2 = jnp.sum(jnp.where(first, 0.0, slot), axis=-1, keepdims=True)
    dest_ref[...] = (jnp.where(lane == 0, d1, 0.0) + jnp.where(lane == 1, d2, 0.0)).astype(jnp.int32)
    run_ref[...] += jnp.sum(onehot, axis=0, keepdims=True)


def _route_call(rt, cnt):
    n = rt.shape[0]
    tm = TM_ROUTE
    return pl.pallas_call(
        _route_kernel,
        grid=(n // tm,),
        in_specs=[pl.BlockSpec((tm, GATE_PAD), lambda i: (i, 0)), pl.BlockSpec((8, LANES), lambda i: (0, 0))],
        out_specs=[pl.BlockSpec((tm, LANES), lambda i: (i, 0)), pl.BlockSpec((8, LANES), lambda i: (0, 0))],
        out_shape=[jax.ShapeDtypeStruct((n, LANES), jnp.int32), jax.ShapeDtypeStruct((8, LANES), _F32)],
        scratch_shapes=[pltpu.VMEM((8, LANES), _F32)] * 2,
        compiler_params=pltpu.CompilerParams(dimension_semantics=("arbitrary",)),
        name="route",
    )(rt, cnt)


def _sc_mesh():
    return plsc.VectorSubcoreMesh(core_axis_name="c", subcore_axis_name="s")


def _sc_worker(n_rows):
    per = n_rows // (SC_CORES * SC_SUBCORES)
    return (lax.axis_index("s") * SC_CORES + lax.axis_index("c")) * per, per


def _dispatch_call(xp, d1, d2, n_slots):
    n, w = xp.shape
    assert n % (SC_CORES * SC_SUBCORES * SC_CHUNK) == 0

    @functools.partial(
        pl.kernel, mesh=_sc_mesh(), out_type=jax.ShapeDtypeStruct((n_slots, w), xp.dtype),
        scratch_types=[pltpu.VMEM((SC_CHUNK,), jnp.int32), pltpu.VMEM((SC_CHUNK,), jnp.int32),
                       pltpu.VMEM((SC_CHUNK, w), xp.dtype), pltpu.SemaphoreType.DMA],
        name="dispatch")
    def k(x_hbm, d1_hbm, d2_hbm, xs_hbm, i1_v, i2_v, rows_v, sem):
        row0, per = _sc_worker(n)

        @pl.loop(0, per // SC_CHUNK)
        def _(j):
            src = pl.ds(row0 + j * SC_CHUNK, SC_CHUNK)
            pltpu.sync_copy(d1_hbm.at[src], i1_v)
            pltpu.sync_copy(d2_hbm.at[src], i2_v)
            pltpu.sync_copy(x_hbm.at[src], rows_v)
            first = pltpu.async_copy(rows_v, xs_hbm.at[i1_v], sem)
            second = pltpu.async_copy(rows_v, xs_hbm.at[i2_v], sem)
            first.wait()
            second.wait()

    return k(xp, d1, d2)


def _combine_call(ys, d1, d2):
    n = d1.shape[0]
    w = ys.shape[1]
    assert n % (SC_CORES * SC_SUBCORES * SC_CHUNK) == 0
    out = jax.ShapeDtypeStruct((n, w), ys.dtype)

    @functools.partial(
        pl.kernel, mesh=_sc_mesh(), out_type=(out, out),
        scratch_types=[pltpu.VMEM((SC_CHUNK,), jnp.int32), pltpu.VMEM((SC_CHUNK, w), ys.dtype),
                       pltpu.SemaphoreType.DMA],
        name="combine")
    def k(ys_hbm, d1_hbm, d2_hbm, y1_hbm, y2_hbm, i_v, rows_v, sem):
        row0, per = _sc_worker(n)

        @pl.loop(0, per // SC_CHUNK)
        def _(j):
            dst = pl.ds(row0 + j * SC_CHUNK, SC_CHUNK)
            for d_hbm, y_hbm in ((d1_hbm, y1_hbm), (d2_hbm, y2_hbm)):
                pltpu.sync_copy(d_hbm.at[dst], i_v)
                pltpu.async_copy(ys_hbm.at[i_v], rows_v, sem).wait()
                pltpu.sync_copy(rows_v, y_hbm.at[dst])

    return k(ys, d1, d2)


def _gmm_kernel(te_ref, nu_ref, xs_ref, wg0_ref, wu0_ref, wd0_ref, wg1_ref, wu1_ref, wd1_ref, ys_ref):
    half = D_MODEL // 2
    t0 = 2 * pl.program_id(0)

    def ffn(rows, wg_ref, wu_ref, wd_ref):
        a, b = _unpack_halves(xs_ref[rows, :])
        a, b = a.astype(_BF16), b.astype(_BF16)
        gate = _dot(a, wg_ref[0, :half, :]) + _dot(b, wg_ref[0, half:, :])
        up = _dot(a, wu_ref[0, :half, :]) + _dot(b, wu_ref[0, half:, :])
        hid = gate * jax.nn.sigmoid(gate) * up
        ys_ref[rows, :] = _pack_halves(_dot(hid.astype(_BF16), wd_ref[0]))

    both = t0 + 1 < nu_ref[0]
    same = te_ref[t0] == te_ref[t0 + 1]

    @pl.when(both & same)
    def _():
        ffn(slice(0, 2 * TR_GMM), wg0_ref, wu0_ref, wd0_ref)

    @pl.when((t0 < nu_ref[0]) & jnp.logical_not(both & same))
    def _():
        ffn(slice(0, TR_GMM), wg0_ref, wu0_ref, wd0_ref)

    @pl.when(both & jnp.logical_not(same))
    def _():
        ffn(slice(TR_GMM, 2 * TR_GMM), wg1_ref, wu1_ref, wd1_ref)


def _gmm_call(tile_expert, n_used, xs, w_gate, w_up, w_down):
    n_slots, w = xs.shape
    assert (n_slots // TR_GMM) % 2 == 0
    last = lambda nu: jnp.maximum((nu[0] - 1) // 2, 0)
    rows = lambda u, te, nu: (jnp.minimum(u, last(nu)), 0)
    expert0 = lambda u, te, nu: (te[2 * u], 0, 0)
    expert1 = lambda u, te, nu: (te[2 * u + 1], 0, 0)
    w_in = lambda idx: pl.BlockSpec((1, D_MODEL, D_FF_EXPERT), idx)
    w_out = lambda idx: pl.BlockSpec((1, D_FF_EXPERT, D_MODEL), idx)
    return pl.pallas_call(
        _gmm_kernel,
        grid_spec=pltpu.PrefetchScalarGridSpec(
            num_scalar_prefetch=2, grid=(n_slots // (2 * TR_GMM),),
            in_specs=[pl.BlockSpec((2 * TR_GMM, w), rows),
                      w_in(expert0), w_in(expert0), w_out(expert0), w_in(expert1), w_in(expert1), w_out(expert1)],
            out_specs=pl.BlockSpec((2 * TR_GMM, w), rows)),
        out_shape=jax.ShapeDtypeStruct((n_slots, w), xs.dtype),
        compiler_params=pltpu.CompilerParams(dimension_semantics=("arbitrary",), vmem_limit_bytes=VMEM_LIMIT),
        name="gmm",
    )(tile_expert, n_used, xs, w_gate, w_up, w_down, w_gate, w_up, w_down)


def _final_kernel(h1_ref, y1_ref, y2_ref, rt_ref, p_ref, gple_ref, wpg_ref, wpp_ref, gfin_ref, o_ref):
    for r in range(h1_ref.shape[0] // SUB_MOE):
        rows = slice(r * SUB_MOE, (r + 1) * SUB_MOE)
        rt = rt_ref[rows, :]
        w1, w2 = rt[:, RT_W:RT_W + 1], rt[:, RT_W + 1:RT_W + 2]
        a1, b1 = _unpack_halves(y1_ref[rows, :])
        a2, b2 = _unpack_halves(y2_ref[rows, :])
        h2 = h1_ref[rows, :] + jnp.concatenate([w1 * a1 + w2 * a2, w1 * b1 + w2 * b2], axis=1)
        gate = jax.nn.sigmoid(_dot(_rms(h2, gple_ref[...]).astype(_BF16), wpg_ref[...]))
        h3 = h2 + _dot(p_ref[rows, :].astype(_BF16), wpp_ref[...]) * gate
        o_ref[rows, :] = _rms(h3, gfin_ref[...])


def _final_call(h1, y1, y2, rt, p2, gple, wpg, wpp, gfin):
    n = h1.shape[0]
    tm = TM_POST
    row = lambda i: (i, 0)
    const2 = lambda i: (0, 0)
    return pl.pallas_call(
        _final_kernel,
        grid=(n // tm,),
        in_specs=[pl.BlockSpec((tm, D_MODEL), row), pl.BlockSpec((tm, D_MODEL // 2), row),
                  pl.BlockSpec((tm, D_MODEL // 2), row), pl.BlockSpec((tm, GATE_PAD), row),
                  pl.BlockSpec((tm, D_PLE), row), pl.BlockSpec((1, D_MODEL), const2),
                  pl.BlockSpec((D_MODEL, D_MODEL), const2), pl.BlockSpec((D_PLE, D_MODEL), const2),
                  pl.BlockSpec((1, D_MODEL), const2)],
        out_specs=pl.BlockSpec((tm, D_MODEL), row),
        out_shape=jax.ShapeDtypeStruct((n, D_MODEL), _F32),
        compiler_params=pltpu.CompilerParams(dimension_semantics=("arbitrary",), vmem_limit_bytes=VMEM_LIMIT),
        name="final",
    )(h1, y1, y2, rt, p2, gple, wpg, wpp, gfin)


def _rope_tables(seq):
    half = HEAD_DIM // 2
    inv = 1.0 / (ROPE_THETA ** (jnp.arange(half, dtype=_F32) / half))
    ang = jnp.arange(seq, dtype=_F32)[:, None] * inv[None, :]
    cos, sin = jnp.cos(ang), jnp.sin(ang)
    reps = LANES // HEAD_DIM
    cos_t = jnp.tile(jnp.concatenate([cos, cos], axis=1), (1, reps))
    sin_t = jnp.tile(jnp.concatenate([-sin, sin], axis=1), (1, reps))
    return cos_t, sin_t


def _layer(h, p_i, norm_mix, w_in, gmlp_v_norm, gmlp_w_s, gmlp_b_s,
           cmp_pe_k, cmp_w1_k, cmp_w2_k, cmp_pe_v, cmp_w1_v, cmp_w2_v,
           out_norm_a, out_norm_b, w_o, norm_moe, router_group, router_expert,
           moe_w_gate, moe_w_up, moe_w_down, norm_ple, w_ple_proj, w_ple_gate, norm_final):
    bsz, seq, _ = h.shape
    n = bsz * seq
    x2 = h.reshape(n, D_MODEL)
    row = lambda v: v.reshape(1, -1).astype(_F32)

    w_q = _perm_heads(w_in[:, OFF_Q:OFF_KV], 1)
    w_gate = jnp.pad(w_in[:, OFF_GATE:D_IN], ((0, 0), (0, GATE_PAD - N_GATES)))
    w_all = jnp.concatenate([w_in[:, :OFF_Q], w_q, w_in[:, OFF_KV:OFF_GATE], w_gate], axis=1).astype(_BF16)
    cos_t, sin_t = _rope_tables(seq)
    ws_pairs = gmlp_w_s.reshape(A_HEADS // 2, 2, CHUNK, CHUNK).transpose(0, 2, 1, 3).reshape(
        A_HEADS // 2, CHUNK, 2 * CHUNK)
    bs_exp = jnp.repeat(gmlp_b_s.T, HEAD_DIM, axis=1)

    oa, q, k_cmp, v_cmp, ks, vs0, vs1, k_win, v_win, gates = _proj_call(
        x2, row(norm_mix), w_all, cos_t, sin_t, row(gmlp_v_norm), ws_pairs, bs_exp, row(out_norm_a), seq)

    kc, vc = _compress_call(k_cmp, v_cmp, _compress_weights(cmp_w1_k, cmp_w2_k, cmp_pe_k),
                            _compress_weights(cmp_w1_v, cmp_w2_v, cmp_pe_v), bsz, seq)

    b3 = lambda a: a.reshape(bsz, seq, a.shape[-1])
    ob, (wg_bf, wu_bf, wd_bf) = _attn_call(
        b3(q), b3(gates), kc, vc, b3(ks), b3(vs0), b3(vs1), b3(k_win), b3(v_win),
        row(_perm_heads(out_norm_b, 0)), [moe_w_gate, moe_w_up, moe_w_down], bsz, seq)

    r_cat = jnp.pad(jnp.concatenate([router_group, router_expert], axis=1),
                    ((0, 0), (0, GATE_PAD - N_GROUPS - N_EXPERTS)))
    r_hi = r_cat.astype(_BF16)
    r_cat = jnp.concatenate([r_hi, (r_cat - r_hi.astype(_F32)).astype(_BF16)], axis=1)
    h1, xp, rt, cnt = _post_call(x2, oa, ob.reshape(n, B_WIDTH), w_o[:A_WIDTH].astype(_BF16),
                            _perm_heads(w_o[A_WIDTH:], 0).astype(_BF16), row(norm_moe), r_cat)

    dest, meta = _route_call(rt, cnt)
    d1, d2 = dest[:, 0], dest[:, 1]
    n_tiles = 2 * n // TR_GMM + N_EXPERTS
    ends = jnp.cumsum(meta[0, :N_EXPERTS].astype(jnp.int32))
    tile_expert = jnp.minimum(jnp.sum(ends[None, :] <= jnp.arange(n_tiles)[:, None], axis=1),
                              N_EXPERTS - 1).astype(jnp.int32)
    xs = _dispatch_call(xp, d1, d2, n_tiles * TR_GMM)
    ys = _gmm_call(tile_expert, ends[-1:], xs, wg_bf, wu_bf, wd_bf)
    y1, y2 = _combine_call(ys, d1, d2)
    out = _final_call(h1, y1, y2, rt, p_i.reshape(n, D_PLE), row(norm_ple),
                      w_ple_gate.astype(_BF16), w_ple_proj.astype(_BF16), row(norm_final))
    return out.reshape(bsz, seq, D_MODEL)


def kernel(x, p, norm_mix, w_in, gmlp_v_norm, gmlp_w_s, gmlp_b_s, cmp_pe_k, cmp_w1_k, cmp_w2_k,
           cmp_pe_v, cmp_w1_v, cmp_w2_v, out_norm_a, out_norm_b, w_o, norm_moe, router_group,
           router_expert, moe_w_gate, moe_w_up, moe_w_down, norm_ple, w_ple_proj, w_ple_gate, norm_final):
    assert p.shape[0] == 1, "single-layer trunk"
    assert x.shape[1] % SEL_CK == 0 and x.shape[1] >= WIN + Q_BLOCK
    assert (x.shape[0] * x.shape[1]) % (SC_CORES * SC_SUBCORES * SC_CHUNK) == 0
    return _layer(x, p[0], norm_mix[0], w_in[0], gmlp_v_norm[0], gmlp_w_s[0], gmlp_b_s[0],
                  cmp_pe_k[0], cmp_w1_k[0], cmp_w2_k[0], cmp_pe_v[0], cmp_w1_v[0], cmp_w2_v[0],
                  out_norm_a[0], out_norm_b[0], w_o[0], norm_moe[0], router_group[0], router_expert[0],
                  moe_w_gate[0], moe_w_up[0], moe_w_down[0], norm_ple[0], w_ple_proj[0], w_ple_gate[0],
                  norm_final)
```

```python
import functools

import numpy as np
import jax
import jax.numpy as jnp
from jax import lax
from jax.experimental import pallas as pl
from jax.experimental.pallas import tpu as pltpu
from jax.experimental.pallas import tpu_sc as plsc

D_MODEL = 1024
HEAD_DIM = 64
A_HEADS = 8
A_WIDTH = A_HEADS * HEAD_DIM
B_HEADS = 8
B_WIDTH = B_HEADS * HEAD_DIM
B_KV = 2
B_HPG = B_HEADS // B_KV
KV_W = B_KV * HEAD_DIM
N_GATES = B_HEADS * 3
CHUNK = 128
L_CMP = 32
STRIDE_CMP = 16
CMP_HIDDEN = 256
L_SEL = 64
N_SEL = 16
WIN = 512
WIN_Q = 128
Q_BLOCK = 512
ROPE_THETA = 10000.0
N_GROUPS = 4
EXPERTS_PER_GROUP = 4
N_EXPERTS = N_GROUPS * EXPERTS_PER_GROUP
D_FF_EXPERT = 512
D_PLE = 256
EPS = 1e-6
LOG2E = 1.4426950408889634
NEG = -1e30
FORCE = 1e6

OFF_Q = 2 * A_WIDTH
OFF_KV = OFF_Q + B_WIDTH
OFF_GATE = OFF_KV + 6 * KV_W
D_IN = OFF_GATE + N_GATES

LANES = 128
GATE_PAD = LANES
ROUTER_OFF = N_GROUPS
W_ALL = OFF_GATE + GATE_PAD

TM_PROJ = 1024
TM_POST = 1024
TM_ROUTE = 512
TR_GMM = 512
SUB_MOE = 512
RT_W = 2 * N_EXPERTS
SC_CORES = 2
SC_SUBCORES = 16
SC_CHUNK = 128
MOE_PARTS = 2
SEL_CK = 512
VMEM_LIMIT = 56 * 1024 * 1024
VMEM_LIMIT_ATTN = 60 * 1024 * 1024

_PERM_HEADS = [0, 4, 1, 5, 2, 6, 3, 7]


def _perm_heads(a, axis):
    return jnp.concatenate([lax.slice_in_dim(a, h * HEAD_DIM, (h + 1) * HEAD_DIM, axis=axis) for h in _PERM_HEADS],
                           axis=axis)

_F32 = jnp.float32
_BF16 = jnp.bfloat16


def _dot(a, b):
    return jnp.dot(a, b, preferred_element_type=_F32)


def _dot_nt(a, b):
    return lax.dot_general(a, b, (((1,), (1,)), ((), ())), preferred_element_type=_F32)


def _rms(x, g):
    return x * lax.rsqrt(jnp.mean(x * x, axis=-1, keepdims=True) + EPS) * g


def _gelu(x):
    return 0.5 * x * (1.0 + jnp.tanh(0.7978845608028654 * (x + 0.044715 * (x * x * x))))


def _rope_tile(x, cos, sin_signed):
    lane = lax.broadcasted_iota(jnp.int32, x.shape, 1)
    first_half = (lane % HEAD_DIM) < (HEAD_DIM // 2)
    rot = jnp.where(first_half, pltpu.roll(x, LANES - HEAD_DIM // 2, 1), pltpu.roll(x, HEAD_DIM // 2, 1))
    return x * cos + rot * sin_signed


def _proj_kernel(x_ref, gmix_ref, w_ref, cos_ref, sin_ref, gv_ref, ws_ref, bs_ref, goa_ref,
                 oa_ref, q_ref, kc_ref, vc_ref, ks_ref, vs0_ref, vs1_ref, kw_ref, vw_ref, gate_ref,
                 *, seq):
    tm = x_ref.shape[0]
    hn = _rms(x_ref[...], gmix_ref[...]).astype(_BF16)
    cos = cos_ref[...]
    sin = sin_ref[...]

    zu = _gelu(_dot(hn, w_ref[:, 0:A_WIDTH]))
    zv = _gelu(_dot(hn, w_ref[:, A_WIDTH:2 * A_WIDTH]))
    vn = _rms(zv, gv_ref[...]).astype(_BF16)

    t_io = lax.broadcasted_iota(jnp.int32, (CHUNK, 2 * CHUNK), 0)
    s_io = lax.broadcasted_iota(jnp.int32, (CHUNK, 2 * CHUNK), 1) % CHUNK
    causal = s_io <= t_io
    lane = lax.broadcasted_iota(jnp.int32, (CHUNK, LANES), 1)
    lo = lane < HEAD_DIM
    bs = bs_ref[...]
    n_chunks = tm // CHUNK
    pair_cols = []
    for pr in range(A_HEADS // 2):
        wcat = jnp.where(causal, ws_ref[pr], 0.0).astype(_BF16)
        rhs = []
        for c in range(n_chunks):
            vblk = vn[c * CHUNK:(c + 1) * CHUNK, pr * LANES:(pr + 1) * LANES]
            zero = jnp.zeros_like(vblk)
            rhs.append(jnp.concatenate([jnp.where(lo, vblk, zero), jnp.where(lo, zero, vblk)], axis=0))
        out = _dot(wcat, jnp.concatenate(rhs, axis=1))
        pair_cols.append(jnp.concatenate([out[:, c * LANES:(c + 1) * LANES] for c in range(n_chunks)], axis=0))
    mixed = jnp.concatenate(pair_cols, axis=1) + jnp.concatenate([bs] * n_chunks, axis=0)
    oa = zu * mixed
    oa_ref[...] = _rms(oa, goa_ref[...]).astype(oa_ref.dtype)

    zq = _dot(hn, w_ref[:, OFF_Q:OFF_KV])
    scale = HEAD_DIM ** -0.5 * LOG2E
    for j in range(B_WIDTH // LANES):
        blk = _rope_tile(zq[:, j * LANES:(j + 1) * LANES], cos, sin) * scale
        q_ref[:, j * LANES:(j + 1) * LANES] = blk.astype(q_ref.dtype)

    zkv = _dot(hn, w_ref[:, OFF_KV:OFF_GATE])
    kv = []
    for j in range(6):
        blk = zkv[:, j * KV_W:(j + 1) * KV_W]
        kv.append(_rope_tile(blk, cos, sin) if j % 2 == 0 else blk)
    k_cmp, v_cmp, k_slc, v_slc, k_win, v_win = kv
    kc_ref[...] = k_cmp.astype(kc_ref.dtype)
    vc_ref[...] = v_cmp.astype(vc_ref.dtype)
    kw_ref[...] = k_win.astype(kw_ref.dtype)
    vw_ref[...] = v_win.astype(vw_ref.dtype)

    row = lax.broadcasted_iota(jnp.int32, (tm, LANES), 0)
    lane = lax.broadcasted_iota(jnp.int32, (tm, LANES), 1)
    key_block = ((pl.program_id(0) % (seq // tm)) * tm + row) // L_SEL
    ks_ref[:, :LANES] = k_slc.astype(ks_ref.dtype)
    ks_ref[:, LANES:] = jnp.where(lane == key_block, 1.0, 0.0).astype(ks_ref.dtype)
    for g, vs_ref in enumerate((vs0_ref, vs1_ref)):
        vs_ref[...] = jnp.where((lane // HEAD_DIM) == g, v_slc, 1.0).astype(vs_ref.dtype)

    zg = _dot(hn, w_ref[:, OFF_GATE:W_ALL])
    gate_ref[...] = jax.nn.sigmoid(zg)


def _proj_call(x2, gmix, w_all, cos_t, sin_t, gv, ws_pairs, bs_exp, goa, seq):
    n = x2.shape[0]
    tm = TM_PROJ
    n_t = seq // tm
    row = lambda i: (i, 0)
    const2 = lambda i: (0, 0)
    pos = lambda i: (i % n_t, 0)
    out_shapes = [jax.ShapeDtypeStruct((n, A_WIDTH), _BF16), jax.ShapeDtypeStruct((n, B_WIDTH), _BF16)]
    kv_widths = [KV_W, KV_W, KV_W + LANES, KV_W, KV_W, KV_W, KV_W]
    out_shapes += [jax.ShapeDtypeStruct((n, w), _BF16) for w in kv_widths]
    out_shapes += [jax.ShapeDtypeStruct((n, GATE_PAD), _F32)]
    out_specs = [pl.BlockSpec((tm, A_WIDTH), row), pl.BlockSpec((tm, B_WIDTH), row)]
    out_specs += [pl.BlockSpec((tm, w), row) for w in kv_widths]
    out_specs += [pl.BlockSpec((tm, GATE_PAD), row)]
    return pl.pallas_call(
        functools.partial(_proj_kernel, seq=seq),
        grid=(n // tm,),
        in_specs=[
            pl.BlockSpec((tm, D_MODEL), row),
            pl.BlockSpec((1, D_MODEL), const2),
            pl.BlockSpec((D_MODEL, W_ALL), const2),
            pl.BlockSpec((tm, LANES), pos),
            pl.BlockSpec((tm, LANES), pos),
            pl.BlockSpec((1, A_WIDTH), const2),
            pl.BlockSpec((A_HEADS // 2, CHUNK, 2 * CHUNK), lambda i: (0, 0, 0)),
            pl.BlockSpec((CHUNK, A_WIDTH), const2),
            pl.BlockSpec((1, A_WIDTH), const2),
        ],
        out_specs=out_specs,
        out_shape=out_shapes,
        compiler_params=pltpu.CompilerParams(dimension_semantics=("arbitrary",), vmem_limit_bytes=VMEM_LIMIT),
        name="proj",
    )(x2, gmix, w_all, cos_t, sin_t, gv, ws_pairs, bs_exp, goa)


def _compress_kernel(rk_ref, rv_ref, pek_ref, pev_ref, w1k_ref, w1v_ref,
                     tk_ref, bk_ref, w2k_ref, tv_ref, bv_ref, w2v_ref, kc_ref, vc_ref):
    def one(r_ref, pe_ref, w1_ref, top_ref, bot_ref, w2_ref, o_ref):
        r = r_ref[0]
        nr = r.shape[0]
        a = _dot(r, top_ref[...])
        b = _dot(r, bot_ref[...])
        pe_h = _dot(pe_ref[...], w1_ref[...])
        pe2 = jnp.concatenate([pe_h[0:1], pe_h[0:1]], axis=1)
        hid = a + pltpu.roll(b, nr - 1, 0) + pe2
        o_ref[0] = _dot(_gelu(hid).astype(_BF16), w2_ref[...]).astype(o_ref.dtype)

    one(rk_ref, pek_ref, w1k_ref, tk_ref, bk_ref, w2k_ref, kc_ref)
    one(rv_ref, pev_ref, w1v_ref, tv_ref, bv_ref, w2v_ref, vc_ref)


def _compress_weights(w1, w2, pe):
    half = L_CMP // 2
    w1r = w1.reshape(L_CMP, HEAD_DIM, CMP_HIDDEN)
    eye = jnp.eye(B_KV, dtype=w1.dtype)
    place = lambda part: jnp.einsum('ldj,gh->lgdhj', part, eye).reshape(half * KV_W, B_KV * CMP_HIDDEN)
    top = place(w1r[:half]).astype(_BF16)
    bot = place(w1r[half:]).astype(_BF16)
    w2bd = jnp.einsum('jd,gh->gjhd', w2, eye).reshape(B_KV * CMP_HIDDEN, KV_W).astype(_BF16)
    pe8 = jnp.broadcast_to(pe.reshape(1, L_CMP * HEAD_DIM), (8, L_CMP * HEAD_DIM)).astype(_BF16)
    return pe8, w1.astype(_BF16), top, bot, w2bd


def _compress_call(k_cmp, v_cmp, wk, wv, bsz, seq):
    nr = seq // STRIDE_CMP
    rk = k_cmp.reshape(bsz, nr, STRIDE_CMP * KV_W)
    rv = v_cmp.reshape(bsz, nr, STRIDE_CMP * KV_W)
    pek, w1k, tk, bk, w2k = wk
    pev, w1v, tv, bv, w2v = wv
    rspec = pl.BlockSpec((1, nr, STRIDE_CMP * KV_W), lambda b: (b, 0, 0))
    full = lambda a: pl.BlockSpec(a.shape, lambda b: (0,) * a.ndim)
    ospec = pl.BlockSpec((1, nr, KV_W), lambda b: (b, 0, 0))
    return pl.pallas_call(
        _compress_kernel,
        grid=(bsz,),
        in_specs=[rspec, rspec, full(pek), full(pev), full(w1k), full(w1v),
                  full(tk), full(bk), full(w2k), full(tv), full(bv), full(w2v)],
        out_specs=[ospec, ospec],
        out_shape=[jax.ShapeDtypeStruct((bsz, nr, KV_W), _BF16)] * 2,
        compiler_params=pltpu.CompilerParams(dimension_semantics=("arbitrary",), vmem_limit_bytes=VMEM_LIMIT),
        name="compress",
    )(rk, rv, pek, pev, w1k, w1v, tk, bk, w2k, tv, bv, w2v)


def _topk_rows_mask(sc_t, k):
    n_rows = sc_t.shape[0]
    row = lax.broadcasted_iota(jnp.int32, sc_t.shape, 0).astype(sc_t.dtype)
    taken = jnp.asarray(-3e38, sc_t.dtype)
    for _ in range(k):
        m = jnp.max(sc_t, axis=0, keepdims=True)
        idx = jnp.min(jnp.where(sc_t == m, row, jnp.asarray(n_rows, sc_t.dtype)), axis=0, keepdims=True)
        sc_t = jnp.where(row == idx, taken, sc_t)
    return jnp.where(sc_t == taken, 1.0, 0.0).astype(_F32)


def _attn_kernel(q_ref, gate_ref, gexp_ref, kc_ref, vc_ref, ks_ref, vs0_ref, vs1_ref,
                 kw_ref, vw_ref, gob_ref, *rest, seq, n_cast):
    cast_in, o_ref, cast_out = rest[:n_cast], rest[n_cast], rest[n_cast + 1:2 * n_cast + 1]
    m_scr, acc_scr = rest[2 * n_cast + 1:]
    for src, dst in zip(cast_in, cast_out):
        dst[...] = src[...].astype(dst.dtype)

    qb = pl.program_id(1)
    t0 = qb * Q_BLOCK
    n_cmp = kc_ref.shape[1]
    n_sb = seq // L_SEL
    k_top = min(N_SEL, n_sb)
    rows = B_HPG * Q_BLOCK
    vs_refs = (vs0_ref, vs1_ref)

    lane_q = lax.broadcasted_iota(jnp.int32, (Q_BLOCK, LANES), 1)
    lo = lane_q < HEAD_DIM
    t_col = t0 + lax.broadcasted_iota(jnp.int32, (Q_BLOCK, 1), 0)

    def per_head(x):
        return x.reshape(B_HPG, Q_BLOCK, x.shape[-1])

    def add_bias(s, bias):
        return (per_head(s) + bias[None]).reshape(rows, s.shape[-1])

    qs = []
    for g in range(B_KV):
        own = jnp.where((lane_q // HEAD_DIM) == g, 1.0, 0.0).astype(q_ref.dtype)
        qs.append(jnp.concatenate([q_ref[0, :, j * LANES:(j + 1) * LANES] * own for j in range(B_HPG)], axis=0))

    thr0 = (t0 - (L_CMP - 1)) // STRIDE_CMP
    n_thr = (Q_BLOCK - 1) // STRIDE_CMP + 2
    assert n_thr <= LANES
    thr_rel = (t_col - (L_CMP - 1)) // STRIDE_CMP - thr0
    q_thr = jnp.where(lane_q == thr_rel, 1.0, 0.0).astype(_BF16)
    c_row1 = lax.broadcasted_iota(jnp.int32, (n_cmp, LANES), 0)
    c_lane1 = lax.broadcasted_iota(jnp.int32, (n_cmp, LANES), 1)
    k_thr = jnp.where((c_lane1 < n_thr) & (c_row1 > thr0 + c_lane1), NEG, 0.0).astype(_BF16)
    kc_wide = jnp.concatenate([kc_ref[0], k_thr], axis=1)
    has_c = (t_col >= L_CMP - 1).astype(_F32)
    c_row = lax.broadcasted_iota(jnp.int32, (n_cmp, n_sb), 0) * STRIDE_CMP
    s_col = lax.broadcasted_iota(jnp.int32, (n_cmp, n_sb), 1) * L_SEL
    overlap = jnp.where((c_row < s_col + L_SEL) & (c_row + L_CMP > s_col), 1.0, 0.0).astype(_BF16)
    blk = lax.broadcasted_iota(jnp.int32, (Q_BLOCK, n_sb), 1)
    cur = t_col // L_SEL
    forced = (blk == 0) | (blk == cur) | (blk == cur - 1)
    valid = blk * L_SEL <= t_col

    o_c, sel_bias = [], []
    for g in range(B_KV):
        s_c = _dot_nt(jnp.concatenate([qs[g], jnp.concatenate([q_thr] * B_HPG, axis=0)], axis=1), kc_wide)
        e_c = jnp.exp2(s_c - jnp.max(s_c, axis=-1, keepdims=True))
        inv = per_head(1.0 / jnp.maximum(jnp.sum(e_c, axis=-1, keepdims=True), 1e-30)) * has_c[None]
        e_bf = e_c.astype(_BF16)
        o_c.append((per_head(_dot(e_bf, vc_ref[0])) * inv).reshape(rows, LANES))
        p_bf = per_head(e_bf) * inv.astype(_BF16)
        imp = _dot(functools.reduce(lambda a, b: a + b, [p_bf[j] for j in range(B_HPG)]), overlap)
        score = jnp.where(valid & jnp.logical_not(forced), imp, -FORCE)
        chosen = forced | (_topk_rows_mask(score.T, k_top - 3).T > 0.5)
        sb = jnp.where(chosen & valid, 0.0, NEG)
        if n_sb < LANES:
            sb = jnp.concatenate([sb, jnp.full((Q_BLOCK, LANES - n_sb), NEG, _F32)], axis=1)
        sel_bias.append(sb)

    w_len = WIN + WIN_Q
    lane_w = lax.broadcasted_iota(jnp.int32, (w_len, LANES), 1)
    keep_w = [jnp.where((lane_w // HEAD_DIM) == g, 1.0, 0.0).astype(vw_ref.dtype) for g in range(B_KV)]
    fill_w = [1 - k for k in keep_w]
    acc_w_sub = [[] for _ in range(B_KV)]
    for hh in range(Q_BLOCK // WIN_Q):
        w_start = pl.multiple_of(jnp.maximum(t0 + hh * WIN_Q - WIN, 0), WIN_Q)
        t_sub = t_col[hh * WIN_Q:(hh + 1) * WIN_Q]
        diff_w = t_sub - (w_start + lax.broadcasted_iota(jnp.int32, (WIN_Q, w_len), 1))
        bias_w = jnp.where((diff_w >= 0) & (diff_w < WIN), 0.0, NEG)
        kw = kw_ref[0, pl.ds(w_start, w_len), :]
        vw = vw_ref[0, pl.ds(w_start, w_len), :]
        for g in range(B_KV):
            q_sub = per_head(qs[g])[:, hh * WIN_Q:(hh + 1) * WIN_Q, :].reshape(B_HPG * WIN_Q, LANES)
            s_w = (_dot_nt(q_sub, kw).reshape(B_HPG, WIN_Q, w_len) + bias_w[None]).reshape(B_HPG * WIN_Q, w_len)
            e_w = jnp.exp2(s_w - jnp.max(s_w, axis=-1, keepdims=True)).astype(_BF16)
            vw_aug = vw * keep_w[g] + fill_w[g]
            acc_w_sub[g].append(_dot(e_w, vw_aug).reshape(B_HPG, WIN_Q, LANES))
    acc_w = [jnp.concatenate(acc_w_sub[g], axis=1).reshape(rows, LANES) for g in range(B_KV)]

    n_ck = (t0 + Q_BLOCK + SEL_CK - 1) // SEL_CK
    key_lane = lax.broadcasted_iota(jnp.int32, (Q_BLOCK, SEL_CK), 1)
    bias_diag = jnp.where((n_ck - 1) * SEL_CK + key_lane <= t_col, 0.0, NEG)
    q_wide = [jnp.concatenate([qs[g], jnp.concatenate([sel_bias[g].astype(_BF16)] * B_HPG, axis=0)], axis=1)
              for g in range(B_KV)]

    m_scr[...] = jnp.full(m_scr.shape, NEG, _F32)
    acc_scr[...] = jnp.zeros(acc_scr.shape, _F32)

    def sel_chunk(ci, diag):
        k0 = pl.multiple_of(ci * SEL_CK, SEL_CK)
        for g in range(B_KV):
            m = m_scr[g]
            s = _dot_nt(q_wide[g], ks_ref[0, pl.ds(k0, SEL_CK), :])
            if diag:
                s = add_bias(s, bias_diag)
            m_new = jnp.maximum(m, jnp.max(s, axis=-1, keepdims=True))
            p = jnp.exp2(s - jnp.concatenate([m_new] * (SEL_CK // LANES), axis=1)).astype(_BF16)
            acc_scr[g] = jnp.exp2(m - m_new) * acc_scr[g] + _dot(p, vs_refs[g][0, pl.ds(k0, SEL_CK), :])
            m_scr[g] = m_new

    @pl.loop(0, n_ck - 1)
    def _(ci):
        sel_chunk(ci, False)

    sel_chunk(n_ck - 1, True)
    acc_s = [acc_scr[g] for g in range(B_KV)]

    def numer(acc):
        return jnp.concatenate([jnp.where(lo, acc[0][j * Q_BLOCK:(j + 1) * Q_BLOCK],
                                          acc[1][j * Q_BLOCK:(j + 1) * Q_BLOCK]) for j in range(B_HPG)], axis=1)

    def denom(acc):
        return jnp.concatenate([pltpu.roll(jnp.where(lo, acc[1][j * Q_BLOCK:(j + 1) * Q_BLOCK],
                                                     acc[0][j * Q_BLOCK:(j + 1) * Q_BLOCK]), HEAD_DIM, 1)
                                for j in range(B_HPG)], axis=1)

    gates = gate_ref[0]
    g_hi = gates.astype(_BF16)
    g_split = jnp.concatenate([g_hi, (gates - g_hi.astype(_F32)).astype(_BF16)], axis=1)
    gate_of = lambda r: _dot(g_split, gexp_ref[r])
    ob = (gate_of(0) * numer(o_c)
          + gate_of(1) * numer(acc_s) * (1.0 / jnp.maximum(denom(acc_s), 1e-30))
          + gate_of(2) * numer(acc_w) * (1.0 / jnp.maximum(denom(acc_w), 1e-30)))
    o_ref[0] = _rms(ob, gob_ref[...]).astype(o_ref.dtype)


def _gate_expand():
    x = np.zeros((3, GATE_PAD, B_WIDTH), np.float32)
    for slot, h in enumerate(_PERM_HEADS):
        for r in range(3):
            x[r, 3 * h + r, slot * HEAD_DIM:(slot + 1) * HEAD_DIM] = 1.0
    return jnp.asarray(np.concatenate([x, x], axis=1), _BF16)


def _attn_call(q, gates, kc, vc, ks, vs0, vs1, kw, vw, gob, to_cast, bsz, seq):
    assert seq // L_SEL <= LANES
    n_cmp = kc.shape[1]
    n_q = seq // Q_BLOCK
    steps = bsz * n_q
    qspec = lambda w: pl.BlockSpec((1, Q_BLOCK, w), lambda b, i: (b, i, 0))
    full = lambda r, w=KV_W: pl.BlockSpec((1, r, w), lambda b, i: (b, 0, 0), pipeline_mode=pl.Buffered(1))
    sliced = [a.reshape(steps, a.size // (steps * a.shape[-1]), a.shape[-1]) for a in to_cast]
    cast_specs = [pl.BlockSpec((1,) + a.shape[1:], lambda b, i: (b * n_q + i, 0, 0)) for a in sliced]
    outs = pl.pallas_call(
        functools.partial(_attn_kernel, seq=seq, n_cast=len(sliced)),
        grid=(bsz, n_q),
        in_specs=[qspec(B_WIDTH), qspec(GATE_PAD),
                  pl.BlockSpec((3, 2 * GATE_PAD, B_WIDTH), lambda b, i: (0, 0, 0)),
                  full(n_cmp), full(n_cmp), full(seq, KV_W + LANES),
                  full(seq), full(seq), full(seq), full(seq),
                  pl.BlockSpec((1, B_WIDTH), lambda b, i: (0, 0))] + cast_specs,
        out_specs=[qspec(B_WIDTH)] + cast_specs,
        out_shape=[jax.ShapeDtypeStruct((bsz, seq, B_WIDTH), _BF16)]
        + [jax.ShapeDtypeStruct(a.shape, _BF16) for a in sliced],
        scratch_shapes=[pltpu.VMEM((B_KV, B_HPG * Q_BLOCK, LANES), _F32),
                        pltpu.VMEM((B_KV, B_HPG * Q_BLOCK, LANES), _F32)],
        compiler_params=pltpu.CompilerParams(dimension_semantics=("arbitrary", "arbitrary"),
                                             vmem_limit_bytes=VMEM_LIMIT_ATTN),
        name="attn",
    )(q, gates, _gate_expand(), kc, vc, ks, vs0, vs1, kw, vw, gob, *sliced)
    return outs[0], [o.reshape(a.shape) for o, a in zip(outs[1:], to_cast)]


def _pack_halves(x):
    w = x.shape[1] // 2
    bits = lambda v: lax.bitcast_convert_type(v.astype(_BF16).astype(_F32), jnp.uint32)
    return lax.bitcast_convert_type(bits(x[:, :w]) | (bits(x[:, w:]) >> 16), jnp.int32)


def _unpack_halves(p):
    u = lax.bitcast_convert_type(p, jnp.uint32)
    return (lax.bitcast_convert_type(u & jnp.uint32(0xFFFF0000), _F32),
            lax.bitcast_convert_type(u << 16, _F32))


def _post_kernel(x_ref, oa_ref, ob_ref, woa_ref, wob_ref, gmoe_ref, r_ref, h1_ref, hn_ref, rt_ref, cnt_ref):
    h1 = x_ref[...] + _dot(oa_ref[...], woa_ref[...]) + _dot(ob_ref[...], wob_ref[...])
    h1_ref[...] = h1
    hn = _rms(h1, gmoe_ref[...])
    hn_ref[...] = _pack_halves(hn)

    hn_hi = hn.astype(_BF16)
    hn_lo = (hn - hn_hi.astype(_F32)).astype(_BF16)
    hi_both = _dot(hn_hi, r_ref[...])
    logits = hi_both[:, :GATE_PAD] + (_dot(hn_lo, r_ref[:, :GATE_PAD]) + hi_both[:, GATE_PAD:])
    lane = lax.broadcasted_iota(jnp.int32, logits.shape, 1)
    first_idx = lambda hit: jnp.min(jnp.where(hit, lane, LANES), axis=-1, keepdims=True)

    is_g = lane < N_GROUPS
    lg = jnp.where(is_g, logits, NEG)
    mg = jnp.max(lg, axis=-1, keepdims=True)
    sg = jnp.sum(jnp.where(is_g, jnp.exp(lg - mg), 0.0), axis=-1, keepdims=True)
    pg_top = 1.0 / sg
    g_sel = first_idx(is_g & (lg == mg))

    e_lo = ROUTER_OFF + g_sel * EXPERTS_PER_GROUP
    is_e = (lane >= e_lo) & (lane < e_lo + EXPERTS_PER_GROUP)
    le = jnp.where(is_e, logits, NEG)
    m1 = jnp.max(le, axis=-1, keepdims=True)
    se = jnp.sum(jnp.where(is_e, jnp.exp(le - m1), 0.0), axis=-1, keepdims=True)
    i1 = first_idx(is_e & (le == m1))
    le2 = jnp.where(lane == i1, NEG, le)
    m2 = jnp.max(le2, axis=-1, keepdims=True)
    i2 = first_idx(is_e & (lane != i1) & (le2 == m2))
    pe1 = 1.0 / se
    pe2 = jnp.exp(m2 - m1) / se
    denom = pe1 + pe2
    rt = (jnp.where(lane == i1 - ROUTER_OFF, 1.0, 0.0)
          + jnp.where(lane == i2 - ROUTER_OFF + N_EXPERTS, 1.0, 0.0)
          + jnp.where(lane == RT_W, pg_top * pe1 / denom, 0.0)
          + jnp.where(lane == RT_W + 1, pg_top * pe2 / denom, 0.0))
    rt_ref[...] = rt

    @pl.when(pl.program_id(0) == 0)
    def _():
        cnt_ref[...] = jnp.zeros_like(cnt_ref)

    cnt_ref[...] += jnp.sum(rt, axis=0, keepdims=True)


def _post_call(x2, oa, ob, woa, wob, gmoe, r_cat, part):
    n = x2.shape[0] // MOE_PARTS
    tm = TM_POST
    first = part * (n // tm)
    row_in = lambda i: (first + i, 0)
    row = lambda i: (i, 0)
    const2 = lambda i: (0, 0)
    return pl.pallas_call(
        _post_kernel,
        grid=(n // tm,),
        in_specs=[pl.BlockSpec((tm, D_MODEL), row_in), pl.BlockSpec((tm, A_WIDTH), row_in),
                  pl.BlockSpec((tm, B_WIDTH), row_in), pl.BlockSpec((A_WIDTH, D_MODEL), const2),
                  pl.BlockSpec((B_WIDTH, D_MODEL), const2), pl.BlockSpec((1, D_MODEL), const2),
                  pl.BlockSpec((D_MODEL, 2 * GATE_PAD), const2)],
        out_specs=[pl.BlockSpec((tm, D_MODEL), row), pl.BlockSpec((tm, D_MODEL // 2), row),
                   pl.BlockSpec((tm, GATE_PAD), row), pl.BlockSpec((8, LANES), const2)],
        out_shape=[jax.ShapeDtypeStruct((n, D_MODEL), _F32), jax.ShapeDtypeStruct((n, D_MODEL // 2), jnp.int32),
                   jax.ShapeDtypeStruct((n, GATE_PAD), _F32), jax.ShapeDtypeStruct((8, LANES), _F32)],
        compiler_params=pltpu.CompilerParams(dimension_semantics=("arbitrary",), vmem_limit_bytes=VMEM_LIMIT),
        name="post",
    )(x2, oa, ob, woa, wob, gmoe, r_cat)


def _route_kernel(rt_ref, cnt_ref, dest_ref, meta_ref, off_ref, run_ref):
    tm = rt_ref.shape[0]
    lane = lax.broadcasted_iota(jnp.int32, (1, LANES), 1)
    first = lane < N_EXPERTS
    onehot = jnp.where(lane < 2 * N_EXPERTS, rt_ref[...], 0.0)

    @pl.when(pl.program_id(0) == 0)
    def _():
        cnt = jnp.where(lane < 2 * N_EXPERTS, cnt_ref[...], 0.0)
        c1 = jnp.where(first, cnt, 0.0)
        tot = c1 + jnp.where(first, pltpu.roll(cnt, LANES - N_EXPERTS, 1), 0.0)
        tiles = jnp.floor((tot + (TR_GMM - 1)) * (1.0 / TR_GMM))
        e_row = lax.broadcasted_iota(jnp.int32, (LANES, LANES), 0)
        e_col = lax.broadcasted_iota(jnp.int32, (LANES, LANES), 1)
        before = jnp.where(e_row < e_col, 1.0, 0.0).astype(_BF16)
        base = _dot(tiles.astype(_BF16), before) * TR_GMM
        off_ref[...] = jnp.where(first, base, 0.0) + pltpu.roll(jnp.where(first, base + c1, 0.0), N_EXPERTS, 1)
        run_ref[...] = jnp.zeros_like(run_ref)
        meta_ref[...] = tiles

    r_io = lax.broadcasted_iota(jnp.int32, (tm, tm), 0)
    c_io = lax.broadcasted_iota(jnp.int32, (tm, tm), 1)
    earlier = jnp.where(c_io < r_io, 1.0, 0.0).astype(_BF16)
    rank = _dot(earlier, onehot.astype(_BF16)) + run_ref[0:1, :]
    slot = onehot * (rank + off_ref[0:1, :])
    d1 = jnp.sum(jnp.where(first, slot, 0.0), axis=-1, keepdims=True)
    d2 = jnp.sum(jnp.where(first, 0.0, slot), axis=-1, keepdims=True)
    dest_ref[...] = (jnp.where(lane == 0, d1, 0.0) + jnp.where(lane == 1, d2, 0.0)).astype(jnp.int32)
    run_ref[...] += jnp.sum(onehot, axis=0, keepdims=True)


def _route_call(rt, cnt):
    n = rt.shape[0]
    tm = TM_ROUTE
    return pl.pallas_call(
        _route_kernel,
        grid=(n // tm,),
        in_specs=[pl.BlockSpec((tm, GATE_PAD), lambda i: (i, 0)), pl.BlockSpec((8, LANES), lambda i: (0, 0))],
        out_specs=[pl.BlockSpec((tm, LANES), lambda i: (i, 0)), pl.BlockSpec((8, LANES), lambda i: (0, 0))],
        out_shape=[jax.ShapeDtypeStruct((n, LANES), jnp.int32), jax.ShapeDtypeStruct((8, LANES), _F32)],
        scratch_shapes=[pltpu.VMEM((8, LANES), _F32)] * 2,
        compiler_params=pltpu.CompilerParams(dimension_semantics=("arbitrary",)),
        name="route",
    )(rt, cnt)


def _sc_mesh():
    return plsc.VectorSubcoreMesh(core_axis_name="c", subcore_axis_name="s")


def _sc_worker(n_rows):
    per = n_rows // (SC_CORES * SC_SUBCORES)
    return (lax.axis_index("s") * SC_CORES + lax.axis_index("c")) * per, per


def _dispatch_call(xp, d1, d2, n_slots):
    n, w = xp.shape
    assert n % (SC_CORES * SC_SUBCORES * SC_CHUNK) == 0

    @functools.partial(
        pl.kernel, mesh=_sc_mesh(), out_type=jax.ShapeDtypeStruct((n_slots, w), xp.dtype),
        scratch_types=[pltpu.VMEM((SC_CHUNK,), jnp.int32), pltpu.VMEM((SC_CHUNK,), jnp.int32),
                       pltpu.VMEM((SC_CHUNK, w), xp.dtype), pltpu.SemaphoreType.DMA],
        name="dispatch")
    def k(x_hbm, d1_hbm, d2_hbm, xs_hbm, i1_v, i2_v, rows_v, sem):
        row0, per = _sc_worker(n)

        @pl.loop(0, per // SC_CHUNK)
        def _(j):
            src = pl.ds(row0 + j * SC_CHUNK, SC_CHUNK)
            pltpu.sync_copy(d1_hbm.at[src], i1_v)
            pltpu.sync_copy(d2_hbm.at[src], i2_v)
            pltpu.sync_copy(x_hbm.at[src], rows_v)
            first = pltpu.async_copy(rows_v, xs_hbm.at[i1_v], sem)
            second = pltpu.async_copy(rows_v, xs_hbm.at[i2_v], sem)
            first.wait()
            second.wait()

    return k(xp, d1, d2)


def _combine_call(ys, d1, d2):
    n = d1.shape[0]
    w = ys.shape[1]
    assert n % (SC_CORES * SC_SUBCORES * SC_CHUNK) == 0
    out = jax.ShapeDtypeStruct((n, w), ys.dtype)

    @functools.partial(
        pl.kernel, mesh=_sc_mesh(), out_type=(out, out),
        scratch_types=[pltpu.VMEM((SC_CHUNK,), jnp.int32), pltpu.VMEM((SC_CHUNK, w), ys.dtype),
                       pltpu.SemaphoreType.DMA],
        name="combine")
    def k(ys_hbm, d1_hbm, d2_hbm, y1_hbm, y2_hbm, i_v, rows_v, sem):
        row0, per = _sc_worker(n)

        @pl.loop(0, per // SC_CHUNK)
        def _(j):
            dst = pl.ds(row0 + j * SC_CHUNK, SC_CHUNK)
            for d_hbm, y_hbm in ((d1_hbm, y1_hbm), (d2_hbm, y2_hbm)):
                pltpu.sync_copy(d_hbm.at[dst], i_v)
                pltpu.async_copy(ys_hbm.at[i_v], rows_v, sem).wait()
                pltpu.sync_copy(rows_v, y_hbm.at[dst])

    return k(ys, d1, d2)


def _gmm_kernel(te_ref, nu_ref, xs_ref, wg0_ref, wu0_ref, wd0_ref, wg1_ref, wu1_ref, wd1_ref, ys_ref):
    half = D_MODEL // 2
    t0 = 2 * pl.program_id(0)

    def ffn(rows, wg_ref, wu_ref, wd_ref):
        a, b = _unpack_halves(xs_ref[rows, :])
        a, b = a.astype(_BF16), b.astype(_BF16)
        gate = _dot(a, wg_ref[0, :half, :]) + _dot(b, wg_ref[0, half:, :])
        up = _dot(a, wu_ref[0, :half, :]) + _dot(b, wu_ref[0, half:, :])
        hid = gate * jax.nn.sigmoid(gate) * up
        ys_ref[rows, :] = _pack_halves(_dot(hid.astype(_BF16), wd_ref[0]))

    both = t0 + 1 < nu_ref[0]
    same = te_ref[t0] == te_ref[t0 + 1]

    @pl.when(both & same)
    def _():
        ffn(slice(0, 2 * TR_GMM), wg0_ref, wu0_ref, wd0_ref)

    @pl.when((t0 < nu_ref[0]) & jnp.logical_not(both & same))
    def _():
        ffn(slice(0, TR_GMM), wg0_ref, wu0_ref, wd0_ref)

    @pl.when(both & jnp.logical_not(same))
    def _():
        ffn(slice(TR_GMM, 2 * TR_GMM), wg1_ref, wu1_ref, wd1_ref)


def _gmm_call(tile_expert, n_used, xs, w_gate, w_up, w_down):
    n_slots, w = xs.shape
    assert (n_slots // TR_GMM) % 2 == 0
    last = lambda nu: jnp.maximum((nu[0] - 1) // 2, 0)
    rows = lambda u, te, nu: (jnp.minimum(u, last(nu)), 0)
    expert0 = lambda u, te, nu: (te[2 * u], 0, 0)
    expert1 = lambda u, te, nu: (te[2 * u + 1], 0, 0)
    w_in = lambda idx: pl.BlockSpec((1, D_MODEL, D_FF_EXPERT), idx)
    w_out = lambda idx: pl.BlockSpec((1, D_FF_EXPERT, D_MODEL), idx)
    return pl.pallas_call(
        _gmm_kernel,
        grid_spec=pltpu.PrefetchScalarGridSpec(
            num_scalar_prefetch=2, grid=(n_slots // (2 * TR_GMM),),
            in_specs=[pl.BlockSpec((2 * TR_GMM, w), rows),
                      w_in(expert0), w_in(expert0), w_out(expert0), w_in(expert1), w_in(expert1), w_out(expert1)],
            out_specs=pl.BlockSpec((2 * TR_GMM, w), rows)),
        out_shape=jax.ShapeDtypeStruct((n_slots, w), xs.dtype),
        compiler_params=pltpu.CompilerParams(dimension_semantics=("arbitrary",), vmem_limit_bytes=VMEM_LIMIT),
        name="gmm",
    )(tile_expert, n_used, xs, w_gate, w_up, w_down, w_gate, w_up, w_down)


def _final_kernel(h1_ref, y1_ref, y2_ref, rt_ref, p_ref, gple_ref, wpg_ref, wpp_ref, gfin_ref, *rest):
    o_ref = rest[-1]
    for r in range(h1_ref.shape[0] // SUB_MOE):
        rows = slice(r * SUB_MOE, (r + 1) * SUB_MOE)
        rt = rt_ref[rows, :]
        w1, w2 = rt[:, RT_W:RT_W + 1], rt[:, RT_W + 1:RT_W + 2]
        a1, b1 = _unpack_halves(y1_ref[rows, :])
        a2, b2 = _unpack_halves(y2_ref[rows, :])
        h2 = h1_ref[rows, :] + jnp.concatenate([w1 * a1 + w2 * a2, w1 * b1 + w2 * b2], axis=1)
        gate = jax.nn.sigmoid(_dot(_rms(h2, gple_ref[...]).astype(_BF16), wpg_ref[...]))
        h3 = h2 + _dot(p_ref[rows, :].astype(_BF16), wpp_ref[...]) * gate
        o_ref[rows, :] = _rms(h3, gfin_ref[...])


def _final_call(h1, y1, y2, rt, p2, gple, wpg, wpp, gfin, part, out_prev):
    n = h1.shape[0]
    tm = TM_POST
    first = part * (n // tm)
    row = lambda i: (i, 0)
    row_full = lambda i: (first + i, 0)
    const2 = lambda i: (0, 0)
    in_specs = [pl.BlockSpec((tm, D_MODEL), row), pl.BlockSpec((tm, D_MODEL // 2), row),
                pl.BlockSpec((tm, D_MODEL // 2), row), pl.BlockSpec((tm, GATE_PAD), row),
                pl.BlockSpec((tm, D_PLE), row_full), pl.BlockSpec((1, D_MODEL), const2),
                pl.BlockSpec((D_MODEL, D_MODEL), const2), pl.BlockSpec((D_PLE, D_MODEL), const2),
                pl.BlockSpec((1, D_MODEL), const2)]
    args = [h1, y1, y2, rt, p2, gple, wpg, wpp, gfin]
    aliases = {}
    if out_prev is not None:
        in_specs.append(pl.BlockSpec(memory_space=pl.ANY))
        args.append(out_prev)
        aliases = {len(args) - 1: 0}
    return pl.pallas_call(
        _final_kernel,
        grid=(n // tm,),
        in_specs=in_specs,
        out_specs=pl.BlockSpec((tm, D_MODEL), row_full),
        out_shape=jax.ShapeDtypeStruct((MOE_PARTS * n, D_MODEL), _F32),
        input_output_aliases=aliases,
        compiler_params=pltpu.CompilerParams(dimension_semantics=("arbitrary",), vmem_limit_bytes=VMEM_LIMIT),
        name="final",
    )(*args)


def _rope_tables(seq):
    half = HEAD_DIM // 2
    inv = 1.0 / (ROPE_THETA ** (jnp.arange(half, dtype=_F32) / half))
    ang = jnp.arange(seq, dtype=_F32)[:, None] * inv[None, :]
    cos, sin = jnp.cos(ang), jnp.sin(ang)
    reps = LANES // HEAD_DIM
    cos_t = jnp.tile(jnp.concatenate([cos, cos], axis=1), (1, reps))
    sin_t = jnp.tile(jnp.concatenate([-sin, sin], axis=1), (1, reps))
    return cos_t, sin_t


def _layer(h, p_i, norm_mix, w_in, gmlp_v_norm, gmlp_w_s, gmlp_b_s,
           cmp_pe_k, cmp_w1_k, cmp_w2_k, cmp_pe_v, cmp_w1_v, cmp_w2_v,
           out_norm_a, out_norm_b, w_o, norm_moe, router_group, router_expert,
           moe_w_gate, moe_w_up, moe_w_down, norm_ple, w_ple_proj, w_ple_gate, norm_final):
    bsz, seq, _ = h.shape
    n = bsz * seq
    x2 = h.reshape(n, D_MODEL)
    row = lambda v: v.reshape(1, -1).astype(_F32)

    w_q = _perm_heads(w_in[:, OFF_Q:OFF_KV], 1)
    w_gate = jnp.pad(w_in[:, OFF_GATE:D_IN], ((0, 0), (0, GATE_PAD - N_GATES)))
    w_all = jnp.concatenate([w_in[:, :OFF_Q], w_q, w_in[:, OFF_KV:OFF_GATE], w_gate], axis=1).astype(_BF16)
    cos_t, sin_t = _rope_tables(seq)
    ws_pairs = gmlp_w_s.reshape(A_HEADS // 2, 2, CHUNK, CHUNK).transpose(0, 2, 1, 3).reshape(
        A_HEADS // 2, CHUNK, 2 * CHUNK)
    bs_exp = jnp.repeat(gmlp_b_s.T, HEAD_DIM, axis=1)

    oa, q, k_cmp, v_cmp, ks, vs0, vs1, k_win, v_win, gates = _proj_call(
        x2, row(norm_mix), w_all, cos_t, sin_t, row(gmlp_v_norm), ws_pairs, bs_exp, row(out_norm_a), seq)

    kc, vc = _compress_call(k_cmp, v_cmp, _compress_weights(cmp_w1_k, cmp_w2_k, cmp_pe_k),
                            _compress_weights(cmp_w1_v, cmp_w2_v, cmp_pe_v), bsz, seq)

    b3 = lambda a: a.reshape(bsz, seq, a.shape[-1])
    ob, (wg_bf, wu_bf, wd_bf) = _attn_call(
        b3(q), b3(gates), kc, vc, b3(ks), b3(vs0), b3(vs1), b3(k_win), b3(v_win),
        row(_perm_heads(out_norm_b, 0)), [moe_w_gate, moe_w_up, moe_w_down], bsz, seq)

    r_cat = jnp.pad(jnp.concatenate([router_group, router_expert], axis=1),
                    ((0, 0), (0, GATE_PAD - N_GROUPS - N_EXPERTS)))
    r_hi = r_cat.astype(_BF16)
    r_cat = jnp.concatenate([r_hi, (r_cat - r_hi.astype(_F32)).astype(_BF16)], axis=1)
    woa, wob = w_o[:A_WIDTH].astype(_BF16), _perm_heads(w_o[A_WIDTH:], 0).astype(_BF16)
    wpg, wpp = w_ple_gate.astype(_BF16), w_ple_proj.astype(_BF16)

    n_part = n // MOE_PARTS
    n_tiles = 2 * n_part // TR_GMM + N_EXPERTS
    out = None
    for part in range(MOE_PARTS):
        h1, xp, rt, cnt = _post_call(x2, oa, ob.reshape(n, B_WIDTH), woa, wob, row(norm_moe), r_cat, part)
        dest, meta = _route_call(rt, cnt)
        d1, d2 = dest[:, 0], dest[:, 1]
        ends = jnp.cumsum(meta[0, :N_EXPERTS].astype(jnp.int32))
        tile_expert = jnp.minimum(jnp.sum(ends[None, :] <= jnp.arange(n_tiles)[:, None], axis=1),
                                  N_EXPERTS - 1).astype(jnp.int32)
        xs = _dispatch_call(xp, d1, d2, n_tiles * TR_GMM)
        ys = _gmm_call(tile_expert, ends[-1:], xs, wg_bf, wu_bf, wd_bf)
        y1, y2 = _combine_call(ys, d1, d2)
        out = _final_call(h1, y1, y2, rt, p_i.reshape(n, D_PLE), row(norm_ple), wpg, wpp, row(norm_final),
                          part, out)
    return out.reshape(bsz, seq, D_MODEL)


def kernel(x, p, norm_mix, w_in, gmlp_v_norm, gmlp_w_s, gmlp_b_s, cmp_pe_k, cmp_w1_k, cmp_w2_k,
           cmp_pe_v, cmp_w1_v, cmp_w2_v, out_norm_a, out_norm_b, w_o, norm_moe, router_group,
           router_expert, moe_w_gate, moe_w_up, moe_w_down, norm_ple, w_ple_proj, w_ple_gate, norm_final):
    assert p.shape[0] == 1, "single-layer trunk"
    assert x.shape[1] % SEL_CK == 0 and x.shape[1] >= WIN + Q_BLOCK
    assert (x.shape[0] * x.shape[1]) % (MOE_PARTS * TM_POST) == 0
    return _layer(x, p[0], norm_mix[0], w_in[0], gmlp_v_norm[0], gmlp_w_s[0], gmlp_b_s[0],
                  cmp_pe_k[0], cmp_w1_k[0], cmp_w2_k[0], cmp_pe_v[0], cmp_w1_v[0], cmp_w2_v[0],
                  out_norm_a[0], out_norm_b[0], w_o[0], norm_moe[0], router_group[0], router_expert[0],
                  moe_w_gate[0], moe_w_up[0], moe_w_down[0], norm_ple[0], w_ple_proj[0], w_ple_gate[0],
                  norm_final)
```

```python
import functools

import numpy as np
import jax
import jax.numpy as jnp
from jax import lax
from jax.experimental import pallas as pl
from jax.experimental.pallas import tpu as pltpu
from jax.experimental.pallas import tpu_sc as plsc

D_MODEL = 1024
HEAD_DIM = 64
A_HEADS = 8
A_WIDTH = A_HEADS * HEAD_DIM
B_HEADS = 8
B_WIDTH = B_HEADS * HEAD_DIM
B_KV = 2
B_HPG = B_HEADS // B_KV
KV_W = B_KV * HEAD_DIM
N_GATES = B_HEADS * 3
CHUNK = 128
L_CMP = 32
STRIDE_CMP = 16
CMP_HIDDEN = 256
L_SEL = 64
N_SEL = 16
WIN = 512
WIN_Q = 128
Q_BLOCK = 512
ROPE_THETA = 10000.0
N_GROUPS = 4
EXPERTS_PER_GROUP = 4
N_EXPERTS = N_GROUPS * EXPERTS_PER_GROUP
D_FF_EXPERT = 512
D_PLE = 256
EPS = 1e-6
LOG2E = 1.4426950408889634
NEG = -1e30
FORCE = 1e6

OFF_Q = 2 * A_WIDTH
OFF_KV = OFF_Q + B_WIDTH
OFF_GATE = OFF_KV + 6 * KV_W
D_IN = OFF_GATE + N_GATES

LANES = 128
GATE_PAD = LANES
ROUTER_OFF = N_GROUPS
W_ALL = OFF_GATE + GATE_PAD

TM_PROJ = 1024
TM_POST = 1024
TM_ROUTE = 512
TR_GMM = 512
SUB_MOE = 512
RT_W = 2 * N_EXPERTS
SC_CORES = 2
SC_SUBCORES = 16
SC_CHUNK = 128
MOE_PARTS = 1
SEL_CK = 512
VMEM_LIMIT = 56 * 1024 * 1024
VMEM_LIMIT_ATTN = 60 * 1024 * 1024

_PERM_HEADS = [0, 4, 1, 5, 2, 6, 3, 7]


def _perm_heads(a, axis):
    return jnp.concatenate([lax.slice_in_dim(a, h * HEAD_DIM, (h + 1) * HEAD_DIM, axis=axis) for h in _PERM_HEADS],
                           axis=axis)

_F32 = jnp.float32
_BF16 = jnp.bfloat16


def _dot(a, b):
    return jnp.dot(a, b, preferred_element_type=_F32)


def _dot_nt(a, b):
    return lax.dot_general(a, b, (((1,), (1,)), ((), ())), preferred_element_type=_F32)


def _rms(x, g):
    return x * lax.rsqrt(jnp.mean(x * x, axis=-1, keepdims=True) + EPS) * g


def _gelu(x):
    return 0.5 * x * (1.0 + jnp.tanh(0.7978845608028654 * (x + 0.044715 * (x * x * x))))


def _rope_tile(x, cos, sin_signed):
    lane = lax.broadcasted_iota(jnp.int32, x.shape, 1)
    first_half = (lane % HEAD_DIM) < (HEAD_DIM // 2)
    rot = jnp.where(first_half, pltpu.roll(x, LANES - HEAD_DIM // 2, 1), pltpu.roll(x, HEAD_DIM // 2, 1))
    return x * cos + rot * sin_signed


def _proj_kernel(x_ref, gmix_ref, w_ref, cos_ref, sin_ref, gv_ref, ws_ref, bs_ref, goa_ref,
                 oa_ref, q_ref, kc_ref, vc_ref, ks_ref, vs0_ref, vs1_ref, kw_ref, vw_ref, gate_ref,
                 *, seq):
    tm = x_ref.shape[0]
    hn = _rms(x_ref[...], gmix_ref[...]).astype(_BF16)
    cos = cos_ref[...]
    sin = sin_ref[...]

    zu = _gelu(_dot(hn, w_ref[:, 0:A_WIDTH]))
    zv = _gelu(_dot(hn, w_ref[:, A_WIDTH:2 * A_WIDTH]))
    vn = _rms(zv, gv_ref[...]).astype(_BF16)

    t_io = lax.broadcasted_iota(jnp.int32, (CHUNK, 2 * CHUNK), 0)
    s_io = lax.broadcasted_iota(jnp.int32, (CHUNK, 2 * CHUNK), 1) % CHUNK
    causal = s_io <= t_io
    lane = lax.broadcasted_iota(jnp.int32, (CHUNK, LANES), 1)
    lo = lane < HEAD_DIM
    bs = bs_ref[...]
    n_chunks = tm // CHUNK
    pair_cols = []
    for pr in range(A_HEADS // 2):
        wcat = jnp.where(causal, ws_ref[pr], 0.0).astype(_BF16)
        rhs = []
        for c in range(n_chunks):
            vblk = vn[c * CHUNK:(c + 1) * CHUNK, pr * LANES:(pr + 1) * LANES]
            zero = jnp.zeros_like(vblk)
            rhs.append(jnp.concatenate([jnp.where(lo, vblk, zero), jnp.where(lo, zero, vblk)], axis=0))
        out = _dot(wcat, jnp.concatenate(rhs, axis=1))
        pair_cols.append(jnp.concatenate([out[:, c * LANES:(c + 1) * LANES] for c in range(n_chunks)], axis=0))
    mixed = jnp.concatenate(pair_cols, axis=1) + jnp.concatenate([bs] * n_chunks, axis=0)
    oa = zu * mixed
    oa_ref[...] = _rms(oa, goa_ref[...]).astype(oa_ref.dtype)

    zq = _dot(hn, w_ref[:, OFF_Q:OFF_KV])
    scale = HEAD_DIM ** -0.5 * LOG2E
    for j in range(B_WIDTH // LANES):
        blk = _rope_tile(zq[:, j * LANES:(j + 1) * LANES], cos, sin) * scale
        q_ref[:, j * LANES:(j + 1) * LANES] = blk.astype(q_ref.dtype)

    zkv = _dot(hn, w_ref[:, OFF_KV:OFF_GATE])
    kv = []
    for j in range(6):
        blk = zkv[:, j * KV_W:(j + 1) * KV_W]
        kv.append(_rope_tile(blk, cos, sin) if j % 2 == 0 else blk)
    k_cmp, v_cmp, k_slc, v_slc, k_win, v_win = kv
    kc_ref[...] = k_cmp.astype(kc_ref.dtype)
    vc_ref[...] = v_cmp.astype(vc_ref.dtype)
    kw_ref[...] = k_win.astype(kw_ref.dtype)
    vw_ref[...] = v_win.astype(vw_ref.dtype)

    row = lax.broadcasted_iota(jnp.int32, (tm, LANES), 0)
    lane = lax.broadcasted_iota(jnp.int32, (tm, LANES), 1)
    key_block = ((pl.program_id(0) % (seq // tm)) * tm + row) // L_SEL
    ks_ref[:, :LANES] = k_slc.astype(ks_ref.dtype)
    ks_ref[:, LANES:] = jnp.where(lane == key_block, 1.0, 0.0).astype(ks_ref.dtype)
    for g, vs_ref in enumerate((vs0_ref, vs1_ref)):
        vs_ref[...] = jnp.where((lane // HEAD_DIM) == g, v_slc, 1.0).astype(vs_ref.dtype)

    zg = _dot(hn, w_ref[:, OFF_GATE:W_ALL])
    gate_ref[...] = jax.nn.sigmoid(zg)


def _proj_call(x2, gmix, w_all, cos_t, sin_t, gv, ws_pairs, bs_exp, goa, seq):
    n = x2.shape[0]
    tm = TM_PROJ
    n_t = seq // tm
    row = lambda i: (i, 0)
    const2 = lambda i: (0, 0)
    pos = lambda i: (i % n_t, 0)
    out_shapes = [jax.ShapeDtypeStruct((n, A_WIDTH), _BF16), jax.ShapeDtypeStruct((n, B_WIDTH), _BF16)]
    kv_widths = [KV_W, KV_W, KV_W + LANES, KV_W, KV_W, KV_W, KV_W]
    out_shapes += [jax.ShapeDtypeStruct((n, w), _BF16) for w in kv_widths]
    out_shapes += [jax.ShapeDtypeStruct((n, GATE_PAD), _F32)]
    out_specs = [pl.BlockSpec((tm, A_WIDTH), row), pl.BlockSpec((tm, B_WIDTH), row)]
    out_specs += [pl.BlockSpec((tm, w), row) for w in kv_widths]
    out_specs += [pl.BlockSpec((tm, GATE_PAD), row)]
    return pl.pallas_call(
        functools.partial(_proj_kernel, seq=seq),
        grid=(n // tm,),
        in_specs=[
            pl.BlockSpec((tm, D_MODEL), row),
            pl.BlockSpec((1, D_MODEL), const2),
            pl.BlockSpec((D_MODEL, W_ALL), const2),
            pl.BlockSpec((tm, LANES), pos),
            pl.BlockSpec((tm, LANES), pos),
            pl.BlockSpec((1, A_WIDTH), const2),
            pl.BlockSpec((A_HEADS // 2, CHUNK, 2 * CHUNK), lambda i: (0, 0, 0)),
            pl.BlockSpec((CHUNK, A_WIDTH), const2),
            pl.BlockSpec((1, A_WIDTH), const2),
        ],
        out_specs=out_specs,
        out_shape=out_shapes,
        compiler_params=pltpu.CompilerParams(dimension_semantics=("arbitrary",), vmem_limit_bytes=VMEM_LIMIT),
        name="proj",
    )(x2, gmix, w_all, cos_t, sin_t, gv, ws_pairs, bs_exp, goa)


def _compress_kernel(rk_ref, rv_ref, pek_ref, pev_ref, w1k_ref, w1v_ref,
                     tk_ref, bk_ref, w2k_ref, tv_ref, bv_ref, w2v_ref, kc_ref, vc_ref):
    def one(r_ref, pe_ref, w1_ref, top_ref, bot_ref, w2_ref, o_ref):
        r = r_ref[0]
        nr = r.shape[0]
        a = _dot(r, top_ref[...])
        b = _dot(r, bot_ref[...])
        pe_h = _dot(pe_ref[...], w1_ref[...])
        pe2 = jnp.concatenate([pe_h[0:1], pe_h[0:1]], axis=1)
        hid = a + pltpu.roll(b, nr - 1, 0) + pe2
        o_ref[0] = _dot(_gelu(hid).astype(_BF16), w2_ref[...]).astype(o_ref.dtype)

    one(rk_ref, pek_ref, w1k_ref, tk_ref, bk_ref, w2k_ref, kc_ref)
    one(rv_ref, pev_ref, w1v_ref, tv_ref, bv_ref, w2v_ref, vc_ref)


def _compress_weights(w1, w2, pe):
    half = L_CMP // 2
    w1r = w1.reshape(L_CMP, HEAD_DIM, CMP_HIDDEN)
    eye = jnp.eye(B_KV, dtype=w1.dtype)
    place = lambda part: jnp.einsum('ldj,gh->lgdhj', part, eye).reshape(half * KV_W, B_KV * CMP_HIDDEN)
    top = place(w1r[:half]).astype(_BF16)
    bot = place(w1r[half:]).astype(_BF16)
    w2bd = jnp.einsum('jd,gh->gjhd', w2, eye).reshape(B_KV * CMP_HIDDEN, KV_W).astype(_BF16)
    pe8 = jnp.broadcast_to(pe.reshape(1, L_CMP * HEAD_DIM), (8, L_CMP * HEAD_DIM)).astype(_BF16)
    return pe8, w1.astype(_BF16), top, bot, w2bd


def _compress_call(k_cmp, v_cmp, wk, wv, bsz, seq):
    nr = seq // STRIDE_CMP
    rk = k_cmp.reshape(bsz, nr, STRIDE_CMP * KV_W)
    rv = v_cmp.reshape(bsz, nr, STRIDE_CMP * KV_W)
    pek, w1k, tk, bk, w2k = wk
    pev, w1v, tv, bv, w2v = wv
    rspec = pl.BlockSpec((1, nr, STRIDE_CMP * KV_W), lambda b: (b, 0, 0))
    full = lambda a: pl.BlockSpec(a.shape, lambda b: (0,) * a.ndim)
    ospec = pl.BlockSpec((1, nr, KV_W), lambda b: (b, 0, 0))
    return pl.pallas_call(
        _compress_kernel,
        grid=(bsz,),
        in_specs=[rspec, rspec, full(pek), full(pev), full(w1k), full(w1v),
                  full(tk), full(bk), full(w2k), full(tv), full(bv), full(w2v)],
        out_specs=[ospec, ospec],
        out_shape=[jax.ShapeDtypeStruct((bsz, nr, KV_W), _BF16)] * 2,
        compiler_params=pltpu.CompilerParams(dimension_semantics=("arbitrary",), vmem_limit_bytes=VMEM_LIMIT),
        name="compress",
    )(rk, rv, pek, pev, w1k, w1v, tk, bk, w2k, tv, bv, w2v)


def _topk_rows_mask(sc_t, k):
    n_rows = sc_t.shape[0]
    row = lax.broadcasted_iota(jnp.int32, sc_t.shape, 0).astype(sc_t.dtype)
    taken = jnp.asarray(-3e38, sc_t.dtype)
    for _ in range(k):
        m = jnp.max(sc_t, axis=0, keepdims=True)
        idx = jnp.min(jnp.where(sc_t == m, row, jnp.asarray(n_rows, sc_t.dtype)), axis=0, keepdims=True)
        sc_t = jnp.where(row == idx, taken, sc_t)
    return jnp.where(sc_t == taken, 1.0, 0.0).astype(_F32)


def _attn_kernel(q_ref, gate_ref, gexp_ref, kc_ref, vc_ref, ks_ref, vs0_ref, vs1_ref,
                 kw_ref, vw_ref, gob_ref, *rest, seq, n_cast):
    cast_in, o_ref, cast_out = rest[:n_cast], rest[n_cast], rest[n_cast + 1:2 * n_cast + 1]
    m_scr, acc_scr = rest[2 * n_cast + 1:]
    for src, dst in zip(cast_in, cast_out):
        dst[...] = src[...].astype(dst.dtype)

    qb = pl.program_id(1)
    t0 = qb * Q_BLOCK
    n_cmp = kc_ref.shape[1]
    n_sb = seq // L_SEL
    k_top = min(N_SEL, n_sb)
    rows = B_HPG * Q_BLOCK
    vs_refs = (vs0_ref, vs1_ref)

    lane_q = lax.broadcasted_iota(jnp.int32, (Q_BLOCK, LANES), 1)
    lo = lane_q < HEAD_DIM
    t_col = t0 + lax.broadcasted_iota(jnp.int32, (Q_BLOCK, 1), 0)

    def per_head(x):
        return x.reshape(B_HPG, Q_BLOCK, x.shape[-1])

    def add_bias(s, bias):
        return (per_head(s) + bias[None]).reshape(rows, s.shape[-1])

    qs = []
    for g in range(B_KV):
        own = jnp.where((lane_q // HEAD_DIM) == g, 1.0, 0.0).astype(q_ref.dtype)
        qs.append(jnp.concatenate([q_ref[0, :, j * LANES:(j + 1) * LANES] * own for j in range(B_HPG)], axis=0))

    thr0 = (t0 - (L_CMP - 1)) // STRIDE_CMP
    n_thr = (Q_BLOCK - 1) // STRIDE_CMP + 2
    assert n_thr <= LANES
    thr_rel = (t_col - (L_CMP - 1)) // STRIDE_CMP - thr0
    q_thr = jnp.where(lane_q == thr_rel, 1.0, 0.0).astype(_BF16)
    c_row1 = lax.broadcasted_iota(jnp.int32, (n_cmp, LANES), 0)
    c_lane1 = lax.broadcasted_iota(jnp.int32, (n_cmp, LANES), 1)
    k_thr = jnp.where((c_lane1 < n_thr) & (c_row1 > thr0 + c_lane1), NEG, 0.0).astype(_BF16)
    kc_wide = jnp.concatenate([kc_ref[0], k_thr], axis=1)
    has_c = (t_col >= L_CMP - 1).astype(_F32)
    c_row = lax.broadcasted_iota(jnp.int32, (n_cmp, n_sb), 0) * STRIDE_CMP
    s_col = lax.broadcasted_iota(jnp.int32, (n_cmp, n_sb), 1) * L_SEL
    overlap = jnp.where((c_row < s_col + L_SEL) & (c_row + L_CMP > s_col), 1.0, 0.0).astype(_BF16)
    blk = lax.broadcasted_iota(jnp.int32, (Q_BLOCK, n_sb), 1)
    cur = t_col // L_SEL
    forced = (blk == 0) | (blk == cur) | (blk == cur - 1)
    valid = blk * L_SEL <= t_col

    o_c, sel_bias = [], []
    for g in range(B_KV):
        s_c = _dot_nt(jnp.concatenate([qs[g], jnp.concatenate([q_thr] * B_HPG, axis=0)], axis=1), kc_wide)
        e_c = jnp.exp2(s_c - jnp.max(s_c, axis=-1, keepdims=True))
        inv = per_head(1.0 / jnp.maximum(jnp.sum(e_c, axis=-1, keepdims=True), 1e-30)) * has_c[None]
        e_bf = e_c.astype(_BF16)
        o_c.append((per_head(_dot(e_bf, vc_ref[0])) * inv).reshape(rows, LANES))
        p_bf = per_head(e_bf) * inv.astype(_BF16)
        imp = _dot(functools.reduce(lambda a, b: a + b, [p_bf[j] for j in range(B_HPG)]), overlap)
        score = jnp.where(valid & jnp.logical_not(forced), imp, -FORCE)
        chosen = forced | (_topk_rows_mask(score.T, k_top - 3).T > 0.5)
        sb = jnp.where(chosen & valid, 0.0, NEG)
        if n_sb < LANES:
            sb = jnp.concatenate([sb, jnp.full((Q_BLOCK, LANES - n_sb), NEG, _F32)], axis=1)
        sel_bias.append(sb)

    w_len = WIN + WIN_Q
    lane_w = lax.broadcasted_iota(jnp.int32, (w_len, LANES), 1)
    keep_w = [jnp.where((lane_w // HEAD_DIM) == g, 1.0, 0.0).astype(vw_ref.dtype) for g in range(B_KV)]
    fill_w = [1 - k for k in keep_w]
    acc_w_sub = [[] for _ in range(B_KV)]
    for hh in range(Q_BLOCK // WIN_Q):
        w_start = pl.multiple_of(jnp.maximum(t0 + hh * WIN_Q - WIN, 0), WIN_Q)
        t_sub = t_col[hh * WIN_Q:(hh + 1) * WIN_Q]
        diff_w = t_sub - (w_start + lax.broadcasted_iota(jnp.int32, (WIN_Q, w_len), 1))
        bias_w = jnp.where((diff_w >= 0) & (diff_w < WIN), 0.0, NEG)
        kw = kw_ref[0, pl.ds(w_start, w_len), :]
        vw = vw_ref[0, pl.ds(w_start, w_len), :]
        for g in range(B_KV):
            q_sub = per_head(qs[g])[:, hh * WIN_Q:(hh + 1) * WIN_Q, :].reshape(B_HPG * WIN_Q, LANES)
            s_w = (_dot_nt(q_sub, kw).reshape(B_HPG, WIN_Q, w_len) + bias_w[None]).reshape(B_HPG * WIN_Q, w_len)
            e_w = jnp.exp2(s_w - jnp.max(s_w, axis=-1, keepdims=True)).astype(_BF16)
            vw_aug = vw * keep_w[g] + fill_w[g]
            acc_w_sub[g].append(_dot(e_w, vw_aug).reshape(B_HPG, WIN_Q, LANES))
    acc_w = [jnp.concatenate(acc_w_sub[g], axis=1).reshape(rows, LANES) for g in range(B_KV)]

    n_ck = (t0 + Q_BLOCK + SEL_CK - 1) // SEL_CK
    key_lane = lax.broadcasted_iota(jnp.int32, (Q_BLOCK, SEL_CK), 1)
    bias_diag = jnp.where((n_ck - 1) * SEL_CK + key_lane <= t_col, 0.0, NEG)
    q_wide = [jnp.concatenate([qs[g], jnp.concatenate([sel_bias[g].astype(_BF16)] * B_HPG, axis=0)], axis=1)
              for g in range(B_KV)]

    m_scr[...] = jnp.full(m_scr.shape, NEG, _F32)
    acc_scr[...] = jnp.zeros(acc_scr.shape, _F32)

    def sel_chunk(ci, diag):
        k0 = pl.multiple_of(ci * SEL_CK, SEL_CK)
        for g in range(B_KV):
            m = m_scr[g]
            s = _dot_nt(q_wide[g], ks_ref[0, pl.ds(k0, SEL_CK), :])
            if diag:
                s = add_bias(s, bias_diag)
            m_new = jnp.maximum(m, jnp.max(s, axis=-1, keepdims=True))
            p = jnp.exp2(s - jnp.concatenate([m_new] * (SEL_CK // LANES), axis=1)).astype(_BF16)
            acc_scr[g] = jnp.exp2(m - m_new) * acc_scr[g] + _dot(p, vs_refs[g][0, pl.ds(k0, SEL_CK), :])
            m_scr[g] = m_new

    @pl.loop(0, n_ck - 1)
    def _(ci):
        sel_chunk(ci, False)

    sel_chunk(n_ck - 1, True)
    acc_s = [acc_scr[g] for g in range(B_KV)]

    def numer(acc):
        return jnp.concatenate([jnp.where(lo, acc[0][j * Q_BLOCK:(j + 1) * Q_BLOCK],
                                          acc[1][j * Q_BLOCK:(j + 1) * Q_BLOCK]) for j in range(B_HPG)], axis=1)

    def denom(acc):
        return jnp.concatenate([pltpu.roll(jnp.where(lo, acc[1][j * Q_BLOCK:(j + 1) * Q_BLOCK],
                                                     acc[0][j * Q_BLOCK:(j + 1) * Q_BLOCK]), HEAD_DIM, 1)
                                for j in range(B_HPG)], axis=1)

    gates = gate_ref[0]
    g_hi = gates.astype(_BF16)
    g_split = jnp.concatenate([g_hi, (gates - g_hi.astype(_F32)).astype(_BF16)], axis=1)
    gate_of = lambda r: _dot(g_split, gexp_ref[r])
    ob = (gate_of(0) * numer(o_c)
          + gate_of(1) * numer(acc_s) * (1.0 / jnp.maximum(denom(acc_s), 1e-30))
          + gate_of(2) * numer(acc_w) * (1.0 / jnp.maximum(denom(acc_w), 1e-30)))
    o_ref[0] = _rms(ob, gob_ref[...]).astype(o_ref.dtype)


def _gate_expand():
    x = np.zeros((3, GATE_PAD, B_WIDTH), np.float32)
    for slot, h in enumerate(_PERM_HEADS):
        for r in range(3):
            x[r, 3 * h + r, slot * HEAD_DIM:(slot + 1) * HEAD_DIM] = 1.0
    return jnp.asarray(np.concatenate([x, x], axis=1), _BF16)


def _attn_call(q, gates, kc, vc, ks, vs0, vs1, kw, vw, gob, to_cast, bsz, seq):
    assert seq // L_SEL <= LANES
    n_cmp = kc.shape[1]
    n_q = seq // Q_BLOCK
    steps = bsz * n_q
    qspec = lambda w: pl.BlockSpec((1, Q_BLOCK, w), lambda b, i: (b, i, 0))
    full = lambda r, w=KV_W: pl.BlockSpec((1, r, w), lambda b, i: (b, 0, 0), pipeline_mode=pl.Buffered(1))
    sliced = [a.reshape(steps, a.size // (steps * a.shape[-1]), a.shape[-1]) for a in to_cast]
    cast_specs = [pl.BlockSpec((1,) + a.shape[1:], lambda b, i: (b * n_q + i, 0, 0)) for a in sliced]
    outs = pl.pallas_call(
        functools.partial(_attn_kernel, seq=seq, n_cast=len(sliced)),
        grid=(bsz, n_q),
        in_specs=[qspec(B_WIDTH), qspec(GATE_PAD),
                  pl.BlockSpec((3, 2 * GATE_PAD, B_WIDTH), lambda b, i: (0, 0, 0)),
                  full(n_cmp), full(n_cmp), full(seq, KV_W + LANES),
                  full(seq), full(seq), full(seq), full(seq),
                  pl.BlockSpec((1, B_WIDTH), lambda b, i: (0, 0))] + cast_specs,
        out_specs=[qspec(B_WIDTH)] + cast_specs,
        out_shape=[jax.ShapeDtypeStruct((bsz, seq, B_WIDTH), _BF16)]
        + [jax.ShapeDtypeStruct(a.shape, _BF16) for a in sliced],
        scratch_shapes=[pltpu.VMEM((B_KV, B_HPG * Q_BLOCK, LANES), _F32),
                        pltpu.VMEM((B_KV, B_HPG * Q_BLOCK, LANES), _F32)],
        compiler_params=pltpu.CompilerParams(dimension_semantics=("arbitrary", "arbitrary"),
                                             vmem_limit_bytes=VMEM_LIMIT_ATTN),
        name="attn",
    )(q, gates, _gate_expand(), kc, vc, ks, vs0, vs1, kw, vw, gob, *sliced)
    return outs[0], [o.reshape(a.shape) for o, a in zip(outs[1:], to_cast)]


def _pack_halves(x):
    w = x.shape[1] // 2
    bits = lambda v: lax.bitcast_convert_type(v.astype(_BF16).astype(_F32), jnp.uint32)
    return lax.bitcast_convert_type(bits(x[:, :w]) | (bits(x[:, w:]) >> 16), jnp.int32)


def _unpack_halves(p):
    u = lax.bitcast_convert_type(p, jnp.uint32)
    return (lax.bitcast_convert_type(u & jnp.uint32(0xFFFF0000), _F32),
            lax.bitcast_convert_type(u << 16, _F32))


def _post_kernel(x_ref, oa_ref, ob_ref, woa_ref, wob_ref, gmoe_ref, r_ref, h1_ref, hn_ref, rt_ref, cnt_ref):
    h1 = x_ref[...] + _dot(oa_ref[...], woa_ref[...]) + _dot(ob_ref[...], wob_ref[...])
    h1_ref[...] = h1
    hn = _rms(h1, gmoe_ref[...])
    hn_ref[...] = _pack_halves(hn)

    hn_hi = hn.astype(_BF16)
    hn_lo = (hn - hn_hi.astype(_F32)).astype(_BF16)
    hi_both = _dot(hn_hi, r_ref[...])
    logits = hi_both[:, :GATE_PAD] + (_dot(hn_lo, r_ref[:, :GATE_PAD]) + hi_both[:, GATE_PAD:])
    lane = lax.broadcasted_iota(jnp.int32, logits.shape, 1)
    first_idx = lambda hit: jnp.min(jnp.where(hit, lane, LANES), axis=-1, keepdims=True)

    is_g = lane < N_GROUPS
    lg = jnp.where(is_g, logits, NEG)
    mg = jnp.max(lg, axis=-1, keepdims=True)
    sg = jnp.sum(jnp.where(is_g, jnp.exp(lg - mg), 0.0), axis=-1, keepdims=True)
    pg_top = 1.0 / sg
    g_sel = first_idx(is_g & (lg == mg))

    e_lo = ROUTER_OFF + g_sel * EXPERTS_PER_GROUP
    is_e = (lane >= e_lo) & (lane < e_lo + EXPERTS_PER_GROUP)
    le = jnp.where(is_e, logits, NEG)
    m1 = jnp.max(le, axis=-1, keepdims=True)
    se = jnp.sum(jnp.where(is_e, jnp.exp(le - m1), 0.0), axis=-1, keepdims=True)
    i1 = first_idx(is_e & (le == m1))
    le2 = jnp.where(lane == i1, NEG, le)
    m2 = jnp.max(le2, axis=-1, keepdims=True)
    i2 = first_idx(is_e & (lane != i1) & (le2 == m2))
    pe1 = 1.0 / se
    pe2 = jnp.exp(m2 - m1) / se
    denom = pe1 + pe2
    rt = (jnp.where(lane == i1 - ROUTER_OFF, 1.0, 0.0)
          + jnp.where(lane == i2 - ROUTER_OFF + N_EXPERTS, 1.0, 0.0)
          + jnp.where(lane == RT_W, pg_top * pe1 / denom, 0.0)
          + jnp.where(lane == RT_W + 1, pg_top * pe2 / denom, 0.0))
    rt_ref[...] = rt

    @pl.when(pl.program_id(0) == 0)
    def _():
        cnt_ref[...] = jnp.zeros_like(cnt_ref)

    cnt_ref[...] += jnp.sum(rt, axis=0, keepdims=True)


def _post_call(x2, oa, ob, woa, wob, gmoe, r_cat, part):
    n = x2.shape[0] // MOE_PARTS
    tm = TM_POST
    first = part * (n // tm)
    row_in = lambda i: (first + i, 0)
    row = lambda i: (i, 0)
    const2 = lambda i: (0, 0)
    return pl.pallas_call(
        _post_kernel,
        grid=(n // tm,),
        in_specs=[pl.BlockSpec((tm, D_MODEL), row_in), pl.BlockSpec((tm, A_WIDTH), row_in),
                  pl.BlockSpec((tm, B_WIDTH), row_in), pl.BlockSpec((A_WIDTH, D_MODEL), const2),
                  pl.BlockSpec((B_WIDTH, D_MODEL), const2), pl.BlockSpec((1, D_MODEL), const2),
                  pl.BlockSpec((D_MODEL, 2 * GATE_PAD), const2)],
        out_specs=[pl.BlockSpec((tm, D_MODEL), row), pl.BlockSpec((tm, D_MODEL // 2), row),
                   pl.BlockSpec((tm, GATE_PAD), row), pl.BlockSpec((8, LANES), const2)],
        out_shape=[jax.ShapeDtypeStruct((n, D_MODEL), _F32), jax.ShapeDtypeStruct((n, D_MODEL // 2), jnp.int32),
                   jax.ShapeDtypeStruct((n, GATE_PAD), _F32), jax.ShapeDtypeStruct((8, LANES), _F32)],
        compiler_params=pltpu.CompilerParams(dimension_semantics=("arbitrary",), vmem_limit_bytes=VMEM_LIMIT),
        name="post",
    )(x2, oa, ob, woa, wob, gmoe, r_cat)


def _route_kernel(rt_ref, cnt_ref, dest_ref, meta_ref, off_ref, run_ref):
    tm = rt_ref.shape[0]
    lane = lax.broadcasted_iota(jnp.int32, (1, LANES), 1)
    first = lane < N_EXPERTS
    onehot = jnp.where(lane < 2 * N_EXPERTS, rt_ref[...], 0.0)

    @pl.when(pl.program_id(0) == 0)
    def _():
        cnt = jnp.where(lane < 2 * N_EXPERTS, cnt_ref[...], 0.0)
        c1 = jnp.where(first, cnt, 0.0)
        tot = c1 + jnp.where(first, pltpu.roll(cnt, LANES - N_EXPERTS, 1), 0.0)
        tiles = jnp.floor((tot + (TR_GMM - 1)) * (1.0 / TR_GMM))
        e_row = lax.broadcasted_iota(jnp.int32, (LANES, LANES), 0)
        e_col = lax.broadcasted_iota(jnp.int32, (LANES, LANES), 1)
        before = jnp.where(e_row < e_col, 1.0, 0.0).astype(_BF16)
        base = _dot(tiles.astype(_BF16), before) * TR_GMM
        off_ref[...] = jnp.where(first, base, 0.0) + pltpu.roll(jnp.where(first, base + c1, 0.0), N_EXPERTS, 1)
        run_ref[...] = jnp.zeros_like(run_ref)
        meta_ref[...] = tiles

    r_io = lax.broadcasted_iota(jnp.int32, (tm, tm), 0)
    c_io = lax.broadcasted_iota(jnp.int32, (tm, tm), 1)
    earlier = jnp.where(c_io < r_io, 1.0, 0.0).astype(_BF16)
    rank = _dot(earlier, onehot.astype(_BF16)) + run_ref[0:1, :]
    slot = onehot * (rank + off_ref[0:1, :])
    d1 = jnp.sum(jnp.where(first, slot, 0.0), axis=-1, keepdims=True)
    d2 = jnp.sum(jnp.where(first, 0.0, slot), axis=-1, keepdims=True)
    dest_ref[...] = (jnp.where(lane == 0, d1, 0.0) + jnp.where(lane == 1, d2, 0.0)).astype(jnp.int32)
    run_ref[...] += jnp.sum(onehot, axis=0, keepdims=True)


def _route_call(rt, cnt):
    n = rt.shape[0]
    tm = TM_ROUTE
    return pl.pallas_call(
        _route_kernel,
        grid=(n // tm,),
        in_specs=[pl.BlockSpec((tm, GATE_PAD), lambda i: (i, 0)), pl.BlockSpec((8, LANES), lambda i: (0, 0))],
        out_specs=[pl.BlockSpec((tm, LANES), lambda i: (i, 0)), pl.BlockSpec((8, LANES), lambda i: (0, 0))],
        out_shape=[jax.ShapeDtypeStruct((n, LANES), jnp.int32), jax.ShapeDtypeStruct((8, LANES), _F32)],
        scratch_shapes=[pltpu.VMEM((8, LANES), _F32)] * 2,
        compiler_params=pltpu.CompilerParams(dimension_semantics=("arbitrary",)),
        name="route",
    )(rt, cnt)


def _sc_mesh():
    return plsc.VectorSubcoreMesh(core_axis_name="c", subcore_axis_name="s")


def _sc_worker(n_rows):
    per = n_rows // (SC_CORES * SC_SUBCORES)
    return (lax.axis_index("s") * SC_CORES + lax.axis_index("c")) * per, per


def _dispatch_call(xp, d1, d2, n_slots):
    n, w = xp.shape
    assert n % (SC_CORES * SC_SUBCORES * SC_CHUNK) == 0

    @functools.partial(
        pl.kernel, mesh=_sc_mesh(), out_type=jax.ShapeDtypeStruct((n_slots, w), xp.dtype),
        scratch_types=[pltpu.VMEM((SC_CHUNK,), jnp.int32), pltpu.VMEM((SC_CHUNK,), jnp.int32),
                       pltpu.VMEM((SC_CHUNK, w), xp.dtype), pltpu.SemaphoreType.DMA],
        name="dispatch")
    def k(x_hbm, d1_hbm, d2_hbm, xs_hbm, i1_v, i2_v, rows_v, sem):
        row0, per = _sc_worker(n)

        @pl.loop(0, per // SC_CHUNK)
        def _(j):
            src = pl.ds(row0 + j * SC_CHUNK, SC_CHUNK)
            pltpu.sync_copy(d1_hbm.at[src], i1_v)
            pltpu.sync_copy(d2_hbm.at[src], i2_v)
            pltpu.sync_copy(x_hbm.at[src], rows_v)
            first = pltpu.async_copy(rows_v, xs_hbm.at[i1_v], sem)
            second = pltpu.async_copy(rows_v, xs_hbm.at[i2_v], sem)
            first.wait()
            second.wait()

    return k(xp, d1, d2)


def _combine_call(ys, d1, d2):
    n = d1.shape[0]
    w = ys.shape[1]
    assert n % (SC_CORES * SC_SUBCORES * SC_CHUNK) == 0
    out = jax.ShapeDtypeStruct((n, w), ys.dtype)

    @functools.partial(
        pl.kernel, mesh=_sc_mesh(), out_type=(out, out),
        scratch_types=[pltpu.VMEM((SC_CHUNK,), jnp.int32), pltpu.VMEM((SC_CHUNK, w), ys.dtype),
                       pltpu.SemaphoreType.DMA],
        name="combine")
    def k(ys_hbm, d1_hbm, d2_hbm, y1_hbm, y2_hbm, i_v, rows_v, sem):
        row0, per = _sc_worker(n)

        @pl.loop(0, per // SC_CHUNK)
        def _(j):
            dst = pl.ds(row0 + j * SC_CHUNK, SC_CHUNK)
            for d_hbm, y_hbm in ((d1_hbm, y1_hbm), (d2_hbm, y2_hbm)):
                pltpu.sync_copy(d_hbm.at[dst], i_v)
                pltpu.async_copy(ys_hbm.at[i_v], rows_v, sem).wait()
                pltpu.sync_copy(rows_v, y_hbm.at[dst])

    return k(ys, d1, d2)


def _gmm_kernel(te_ref, nu_ref, xs_ref, wg0_ref, wu0_ref, wd0_ref, wg1_ref, wu1_ref, wd1_ref, ys_ref):
    half = D_MODEL // 2
    t0 = 2 * pl.program_id(0)

    def ffn(rows, wg_ref, wu_ref, wd_ref):
        a, b = _unpack_halves(xs_ref[rows, :])
        a, b = a.astype(_BF16), b.astype(_BF16)
        gate = _dot(a, wg_ref[0, :half, :]) + _dot(b, wg_ref[0, half:, :])
        up = _dot(a, wu_ref[0, :half, :]) + _dot(b, wu_ref[0, half:, :])
        hid = gate * jax.nn.sigmoid(gate) * up
        ys_ref[rows, :] = _pack_halves(_dot(hid.astype(_BF16), wd_ref[0]))

    both = t0 + 1 < nu_ref[0]
    same = te_ref[t0] == te_ref[t0 + 1]

    @pl.when(both & same)
    def _():
        ffn(slice(0, 2 * TR_GMM), wg0_ref, wu0_ref, wd0_ref)

    @pl.when((t0 < nu_ref[0]) & jnp.logical_not(both & same))
    def _():
        ffn(slice(0, TR_GMM), wg0_ref, wu0_ref, wd0_ref)

    @pl.when(both & jnp.logical_not(same))
    def _():
        ffn(slice(TR_GMM, 2 * TR_GMM), wg1_ref, wu1_ref, wd1_ref)


def _gmm_call(tile_expert, n_used, xs, w_gate, w_up, w_down):
    n_slots, w = xs.shape
    assert (n_slots // TR_GMM) % 2 == 0
    last = lambda nu: jnp.maximum((nu[0] - 1) // 2, 0)
    rows = lambda u, te, nu: (jnp.minimum(u, last(nu)), 0)
    expert0 = lambda u, te, nu: (te[2 * u], 0, 0)
    expert1 = lambda u, te, nu: (te[2 * u + 1], 0, 0)
    w_in = lambda idx: pl.BlockSpec((1, D_MODEL, D_FF_EXPERT), idx)
    w_out = lambda idx: pl.BlockSpec((1, D_FF_EXPERT, D_MODEL), idx)
    return pl.pallas_call(
        _gmm_kernel,
        grid_spec=pltpu.PrefetchScalarGridSpec(
            num_scalar_prefetch=2, grid=(n_slots // (2 * TR_GMM),),
            in_specs=[pl.BlockSpec((2 * TR_GMM, w), rows),
                      w_in(expert0), w_in(expert0), w_out(expert0), w_in(expert1), w_in(expert1), w_out(expert1)],
            out_specs=pl.BlockSpec((2 * TR_GMM, w), rows)),
        out_shape=jax.ShapeDtypeStruct((n_slots, w), xs.dtype),
        compiler_params=pltpu.CompilerParams(dimension_semantics=("arbitrary",), vmem_limit_bytes=VMEM_LIMIT),
        name="gmm",
    )(tile_expert, n_used, xs, w_gate, w_up, w_down, w_gate, w_up, w_down)


def _final_kernel(h1_ref, y1_ref, y2_ref, rt_ref, p_ref, gple_ref, wpg_ref, wpp_ref, gfin_ref, *rest):
    o_ref = rest[-1]
    for r in range(h1_ref.shape[0] // SUB_MOE):
        rows = slice(r * SUB_MOE, (r + 1) * SUB_MOE)
        rt = rt_ref[rows, :]
        w1, w2 = rt[:, RT_W:RT_W + 1], rt[:, RT_W + 1:RT_W + 2]
        a1, b1 = _unpack_halves(y1_ref[rows, :])
        a2, b2 = _unpack_halves(y2_ref[rows, :])
        h2 = h1_ref[rows, :] + jnp.concatenate([w1 * a1 + w2 * a2, w1 * b1 + w2 * b2], axis=1)
        gate = jax.nn.sigmoid(_dot(_rms(h2, gple_ref[...]).astype(_BF16), wpg_ref[...]))
        h3 = h2 + _dot(p_ref[rows, :].astype(_BF16), wpp_ref[...]) * gate
        o_ref[rows, :] = _rms(h3, gfin_ref[...])


def _final_call(h1, y1, y2, rt, p2, gple, wpg, wpp, gfin, part, out_prev):
    n = h1.shape[0]
    tm = TM_POST
    first = part * (n // tm)
    row = lambda i: (i, 0)
    row_full = lambda i: (first + i, 0)
    const2 = lambda i: (0, 0)
    in_specs = [pl.BlockSpec((tm, D_MODEL), row), pl.BlockSpec((tm, D_MODEL // 2), row),
                pl.BlockSpec((tm, D_MODEL // 2), row), pl.BlockSpec((tm, GATE_PAD), row),
                pl.BlockSpec((tm, D_PLE), row_full), pl.BlockSpec((1, D_MODEL), const2),
                pl.BlockSpec((D_MODEL, D_MODEL), const2), pl.BlockSpec((D_PLE, D_MODEL), const2),
                pl.BlockSpec((1, D_MODEL), const2)]
    args = [h1, y1, y2, rt, p2, gple, wpg, wpp, gfin]
    aliases = {}
    if out_prev is not None:
        in_specs.append(pl.BlockSpec(memory_space=pl.ANY))
        args.append(out_prev)
        aliases = {len(args) - 1: 0}
    return pl.pallas_call(
        _final_kernel,
        grid=(n // tm,),
        in_specs=in_specs,
        out_specs=pl.BlockSpec((tm, D_MODEL), row_full),
        out_shape=jax.ShapeDtypeStruct((MOE_PARTS * n, D_MODEL), _F32),
        input_output_aliases=aliases,
        compiler_params=pltpu.CompilerParams(dimension_semantics=("arbitrary",), vmem_limit_bytes=VMEM_LIMIT),
        name="final",
    )(*args)


def _rope_tables(seq):
    half = HEAD_DIM // 2
    inv = 1.0 / (ROPE_THETA ** (jnp.arange(half, dtype=_F32) / half))
    ang = jnp.arange(seq, dtype=_F32)[:, None] * inv[None, :]
    cos, sin = jnp.cos(ang), jnp.sin(ang)
    reps = LANES // HEAD_DIM
    cos_t = jnp.tile(jnp.concatenate([cos, cos], axis=1), (1, reps))
    sin_t = jnp.tile(jnp.concatenate([-sin, sin], axis=1), (1, reps))
    return cos_t, sin_t


def _layer(h, p_i, norm_mix, w_in, gmlp_v_norm, gmlp_w_s, gmlp_b_s,
           cmp_pe_k, cmp_w1_k, cmp_w2_k, cmp_pe_v, cmp_w1_v, cmp_w2_v,
           out_norm_a, out_norm_b, w_o, norm_moe, router_group, router_expert,
           moe_w_gate, moe_w_up, moe_w_down, norm_ple, w_ple_proj, w_ple_gate, norm_final):
    bsz, seq, _ = h.shape
    n = bsz * seq
    x2 = h.reshape(n, D_MODEL)
    row = lambda v: v.reshape(1, -1).astype(_F32)

    w_q = _perm_heads(w_in[:, OFF_Q:OFF_KV], 1)
    w_gate = jnp.pad(w_in[:, OFF_GATE:D_IN], ((0, 0), (0, GATE_PAD - N_GATES)))
    w_all = jnp.concatenate([w_in[:, :OFF_Q], w_q, w_in[:, OFF_KV:OFF_GATE], w_gate], axis=1).astype(_BF16)
    cos_t, sin_t = _rope_tables(seq)
    ws_pairs = gmlp_w_s.reshape(A_HEADS // 2, 2, CHUNK, CHUNK).transpose(0, 2, 1, 3).reshape(
        A_HEADS // 2, CHUNK, 2 * CHUNK)
    bs_exp = jnp.repeat(gmlp_b_s.T, HEAD_DIM, axis=1)

    oa, q, k_cmp, v_cmp, ks, vs0, vs1, k_win, v_win, gates = _proj_call(
        x2, row(norm_mix), w_all, cos_t, sin_t, row(gmlp_v_norm), ws_pairs, bs_exp, row(out_norm_a), seq)

    kc, vc = _compress_call(k_cmp, v_cmp, _compress_weights(cmp_w1_k, cmp_w2_k, cmp_pe_k),
                            _compress_weights(cmp_w1_v, cmp_w2_v, cmp_pe_v), bsz, seq)

    b3 = lambda a: a.reshape(bsz, seq, a.shape[-1])
    ob, (wg_bf, wu_bf, wd_bf) = _attn_call(
        b3(q), b3(gates), kc, vc, b3(ks), b3(vs0), b3(vs1), b3(k_win), b3(v_win),
        row(_perm_heads(out_norm_b, 0)), [moe_w_gate, moe_w_up, moe_w_down], bsz, seq)

    r_cat = jnp.pad(jnp.concatenate([router_group, router_expert], axis=1),
                    ((0, 0), (0, GATE_PAD - N_GROUPS - N_EXPERTS)))
    r_hi = r_cat.astype(_BF16)
    r_cat = jnp.concatenate([r_hi, (r_cat - r_hi.astype(_F32)).astype(_BF16)], axis=1)
    woa, wob = w_o[:A_WIDTH].astype(_BF16), _perm_heads(w_o[A_WIDTH:], 0).astype(_BF16)
    wpg, wpp = w_ple_gate.astype(_BF16), w_ple_proj.astype(_BF16)

    n_part = n // MOE_PARTS
    n_tiles = 2 * n_part // TR_GMM + N_EXPERTS
    out = None
    for part in range(MOE_PARTS):
        h1, xp, rt, cnt = _post_call(x2, oa, ob.reshape(n, B_WIDTH), woa, wob, row(norm_moe), r_cat, part)
        dest, meta = _route_call(rt, cnt)
        d1, d2 = dest[:, 0], dest[:, 1]
        ends = jnp.cumsum(meta[0, :N_EXPERTS].astype(jnp.int32))
        tile_expert = jnp.minimum(jnp.sum(ends[None, :] <= jnp.arange(n_tiles)[:, None], axis=1),
                                  N_EXPERTS - 1).astype(jnp.int32)
        xs = _dispatch_call(xp, d1, d2, n_tiles * TR_GMM)
        ys = _gmm_call(tile_expert, ends[-1:], xs, wg_bf, wu_bf, wd_bf)
        y1, y2 = _combine_call(ys, d1, d2)
        out = _final_call(h1, y1, y2, rt, p_i.reshape(n, D_PLE), row(norm_ple), wpg, wpp, row(norm_final),
                          part, out)
    return out.reshape(bsz, seq, D_MODEL)


def kernel(x, p, norm_mix, w_in, gmlp_v_norm, gmlp_w_s, gmlp_b_s, cmp_pe_k, cmp_w1_k, cmp_w2_k,
           cmp_pe_v, cmp_w1_v, cmp_w2_v, out_norm_a, out_norm_b, w_o, norm_moe, router_group,
           router_expert, moe_w_gate, moe_w_up, moe_w_down, norm_ple, w_ple_proj, w_ple_gate, norm_final):
    assert p.shape[0] == 1, "single-layer trunk"
    assert x.shape[1] % SEL_CK == 0 and x.shape[1] >= WIN + Q_BLOCK
    assert (x.shape[0] * x.shape[1]) % (MOE_PARTS * TM_POST) == 0
    return _layer(x, p[0], norm_mix[0], w_in[0], gmlp_v_norm[0], gmlp_w_s[0], gmlp_b_s[0],
                  cmp_pe_k[0], cmp_w1_k[0], cmp_w2_k[0], cmp_pe_v[0], cmp_w1_v[0], cmp_w2_v[0],
                  out_norm_a[0], out_norm_b[0], w_o[0], norm_moe[0], router_group[0], router_expert[0],
                  moe_w_gate[0], moe_w_up[0], moe_w_down[0], norm_ple[0], w_ple_proj[0], w_ple_gate[0],
                  norm_final)
```

```python
import functools

import numpy as np
import jax
import jax.numpy as jnp
from jax import lax
from jax.experimental import pallas as pl
from jax.experimental.pallas import tpu as pltpu
from jax.experimental.pallas import tpu_sc as plsc

D_MODEL = 1024
HEAD_DIM = 64
A_HEADS = 8
A_WIDTH = A_HEADS * HEAD_DIM
B_HEADS = 8
B_WIDTH = B_HEADS * HEAD_DIM
B_KV = 2
B_HPG = B_HEADS // B_KV
KV_W = B_KV * HEAD_DIM
N_GATES = B_HEADS * 3
CHUNK = 128
L_CMP = 32
STRIDE_CMP = 16
CMP_HIDDEN = 256
L_SEL = 64
N_SEL = 16
WIN = 512
WIN_Q = 128
Q_BLOCK = 512
ROPE_THETA = 10000.0
N_GROUPS = 4
EXPERTS_PER_GROUP = 4
N_EXPERTS = N_GROUPS * EXPERTS_PER_GROUP
D_FF_EXPERT = 512
D_PLE = 256
EPS = 1e-6
LOG2E = 1.4426950408889634
NEG = -1e30
FORCE = 1e6

OFF_Q = 2 * A_WIDTH
OFF_KV = OFF_Q + B_WIDTH
OFF_GATE = OFF_KV + 6 * KV_W
D_IN = OFF_GATE + N_GATES

LANES = 128
GATE_PAD = LANES
ROUTER_OFF = N_GROUPS
W_ALL = OFF_GATE + GATE_PAD

TM_PROJ = 1024
TM_POST = 1024
TM_ROUTE = 512
TR_GMM = 512
SUB_MOE = 512
RT_W = 2 * N_EXPERTS
RT_ROWS = 40
ROUTER_ROWS = 24
SC_CORES = 2
SC_SUBCORES = 16
SC_CHUNK = 128
MOE_PARTS = 1
SEL_CK = 512
VMEM_LIMIT = 56 * 1024 * 1024
VMEM_LIMIT_ATTN = 60 * 1024 * 1024

_PERM_HEADS = [0, 4, 1, 5, 2, 6, 3, 7]


def _perm_heads(a, axis):
    return jnp.concatenate([lax.slice_in_dim(a, h * HEAD_DIM, (h + 1) * HEAD_DIM, axis=axis) for h in _PERM_HEADS],
                           axis=axis)

_F32 = jnp.float32
_BF16 = jnp.bfloat16


def _dot(a, b):
    return jnp.dot(a, b, preferred_element_type=_F32)


def _dot_nt(a, b):
    return lax.dot_general(a, b, (((1,), (1,)), ((), ())), preferred_element_type=_F32)


def _rms(x, g):
    return x * lax.rsqrt(jnp.mean(x * x, axis=-1, keepdims=True) + EPS) * g


def _gelu(x):
    return 0.5 * x * (1.0 + jnp.tanh(0.7978845608028654 * (x + 0.044715 * (x * x * x))))


def _rope_tile(x, cos, sin_signed):
    lane = lax.broadcasted_iota(jnp.int32, x.shape, 1)
    first_half = (lane % HEAD_DIM) < (HEAD_DIM // 2)
    rot = jnp.where(first_half, pltpu.roll(x, LANES - HEAD_DIM // 2, 1), pltpu.roll(x, HEAD_DIM // 2, 1))
    return x * cos + rot * sin_signed


def _proj_kernel(x_ref, gmix_ref, w_ref, cos_ref, sin_ref, gv_ref, ws_ref, bs_ref, goa_ref,
                 oa_ref, q_ref, kc_ref, vc_ref, ks_ref, vs0_ref, vs1_ref, kw_ref, vw_ref, gate_ref,
                 *, seq):
    tm = x_ref.shape[0]
    hn = _rms(x_ref[...], gmix_ref[...]).astype(_BF16)
    cos = cos_ref[...]
    sin = sin_ref[...]

    zu = _gelu(_dot(hn, w_ref[:, 0:A_WIDTH]))
    zv = _gelu(_dot(hn, w_ref[:, A_WIDTH:2 * A_WIDTH]))
    vn = _rms(zv, gv_ref[...]).astype(_BF16)

    t_io = lax.broadcasted_iota(jnp.int32, (CHUNK, 2 * CHUNK), 0)
    s_io = lax.broadcasted_iota(jnp.int32, (CHUNK, 2 * CHUNK), 1) % CHUNK
    causal = s_io <= t_io
    lane = lax.broadcasted_iota(jnp.int32, (CHUNK, LANES), 1)
    lo = lane < HEAD_DIM
    bs = bs_ref[...]
    n_chunks = tm // CHUNK
    pair_cols = []
    for pr in range(A_HEADS // 2):
        wcat = jnp.where(causal, ws_ref[pr], 0.0).astype(_BF16)
        rhs = []
        for c in range(n_chunks):
            vblk = vn[c * CHUNK:(c + 1) * CHUNK, pr * LANES:(pr + 1) * LANES]
            zero = jnp.zeros_like(vblk)
            rhs.append(jnp.concatenate([jnp.where(lo, vblk, zero), jnp.where(lo, zero, vblk)], axis=0))
        out = _dot(wcat, jnp.concatenate(rhs, axis=1))
        pair_cols.append(jnp.concatenate([out[:, c * LANES:(c + 1) * LANES] for c in range(n_chunks)], axis=0))
    mixed = jnp.concatenate(pair_cols, axis=1) + jnp.concatenate([bs] * n_chunks, axis=0)
    oa = zu * mixed
    oa_ref[...] = _rms(oa, goa_ref[...]).astype(oa_ref.dtype)

    zq = _dot(hn, w_ref[:, OFF_Q:OFF_KV])
    scale = HEAD_DIM ** -0.5 * LOG2E
    for j in range(B_WIDTH // LANES):
        blk = _rope_tile(zq[:, j * LANES:(j + 1) * LANES], cos, sin) * scale
        q_ref[:, j * LANES:(j + 1) * LANES] = blk.astype(q_ref.dtype)

    zkv = _dot(hn, w_ref[:, OFF_KV:OFF_GATE])
    kv = []
    for j in range(6):
        blk = zkv[:, j * KV_W:(j + 1) * KV_W]
        kv.append(_rope_tile(blk, cos, sin) if j % 2 == 0 else blk)
    k_cmp, v_cmp, k_slc, v_slc, k_win, v_win = kv
    kc_ref[...] = k_cmp.astype(kc_ref.dtype)
    vc_ref[...] = v_cmp.astype(vc_ref.dtype)
    kw_ref[...] = k_win.astype(kw_ref.dtype)
    vw_ref[...] = v_win.astype(vw_ref.dtype)

    row = lax.broadcasted_iota(jnp.int32, (tm, LANES), 0)
    lane = lax.broadcasted_iota(jnp.int32, (tm, LANES), 1)
    key_block = ((pl.program_id(0) % (seq // tm)) * tm + row) // L_SEL
    ks_ref[:, :LANES] = k_slc.astype(ks_ref.dtype)
    ks_ref[:, LANES:] = jnp.where(lane == key_block, 1.0, 0.0).astype(ks_ref.dtype)
    for g, vs_ref in enumerate((vs0_ref, vs1_ref)):
        vs_ref[...] = jnp.where((lane // HEAD_DIM) == g, v_slc, 1.0).astype(vs_ref.dtype)

    zg = _dot(hn, w_ref[:, OFF_GATE:W_ALL])
    gate_ref[...] = jax.nn.sigmoid(zg)


def _proj_call(x2, gmix, w_all, cos_t, sin_t, gv, ws_pairs, bs_exp, goa, seq):
    n = x2.shape[0]
    tm = TM_PROJ
    n_t = seq // tm
    row = lambda i: (i, 0)
    const2 = lambda i: (0, 0)
    pos = lambda i: (i % n_t, 0)
    out_shapes = [jax.ShapeDtypeStruct((n, A_WIDTH), _BF16), jax.ShapeDtypeStruct((n, B_WIDTH), _BF16)]
    kv_widths = [KV_W, KV_W, KV_W + LANES, KV_W, KV_W, KV_W, KV_W]
    out_shapes += [jax.ShapeDtypeStruct((n, w), _BF16) for w in kv_widths]
    out_shapes += [jax.ShapeDtypeStruct((n, GATE_PAD), _F32)]
    out_specs = [pl.BlockSpec((tm, A_WIDTH), row), pl.BlockSpec((tm, B_WIDTH), row)]
    out_specs += [pl.BlockSpec((tm, w), row) for w in kv_widths]
    out_specs += [pl.BlockSpec((tm, GATE_PAD), row)]
    return pl.pallas_call(
        functools.partial(_proj_kernel, seq=seq),
        grid=(n // tm,),
        in_specs=[
            pl.BlockSpec((tm, D_MODEL), row),
            pl.BlockSpec((1, D_MODEL), const2),
            pl.BlockSpec((D_MODEL, W_ALL), const2),
            pl.BlockSpec((tm, LANES), pos),
            pl.BlockSpec((tm, LANES), pos),
            pl.BlockSpec((1, A_WIDTH), const2),
            pl.BlockSpec((A_HEADS // 2, CHUNK, 2 * CHUNK), lambda i: (0, 0, 0)),
            pl.BlockSpec((CHUNK, A_WIDTH), const2),
            pl.BlockSpec((1, A_WIDTH), const2),
        ],
        out_specs=out_specs,
        out_shape=out_shapes,
        compiler_params=pltpu.CompilerParams(dimension_semantics=("arbitrary",), vmem_limit_bytes=VMEM_LIMIT),
        name="proj",
    )(x2, gmix, w_all, cos_t, sin_t, gv, ws_pairs, bs_exp, goa)


def _compress_kernel(rk_ref, rv_ref, pek_ref, pev_ref, w1k_ref, w1v_ref,
                     tk_ref, bk_ref, w2k_ref, tv_ref, bv_ref, w2v_ref, kc_ref, vc_ref):
    def one(r_ref, pe_ref, w1_ref, top_ref, bot_ref, w2_ref, o_ref):
        r = r_ref[0]
        nr = r.shape[0]
        a = _dot(r, top_ref[...])
        b = _dot(r, bot_ref[...])
        pe_h = _dot(pe_ref[...], w1_ref[...])
        pe2 = jnp.concatenate([pe_h[0:1], pe_h[0:1]], axis=1)
        hid = a + pltpu.roll(b, nr - 1, 0) + pe2
        o_ref[0] = _dot(_gelu(hid).astype(_BF16), w2_ref[...]).astype(o_ref.dtype)

    one(rk_ref, pek_ref, w1k_ref, tk_ref, bk_ref, w2k_ref, kc_ref)
    one(rv_ref, pev_ref, w1v_ref, tv_ref, bv_ref, w2v_ref, vc_ref)


def _compress_weights(w1, w2, pe):
    half = L_CMP // 2
    w1r = w1.reshape(L_CMP, HEAD_DIM, CMP_HIDDEN)
    eye = jnp.eye(B_KV, dtype=w1.dtype)
    place = lambda part: jnp.einsum('ldj,gh->lgdhj', part, eye).reshape(half * KV_W, B_KV * CMP_HIDDEN)
    top = place(w1r[:half]).astype(_BF16)
    bot = place(w1r[half:]).astype(_BF16)
    w2bd = jnp.einsum('jd,gh->gjhd', w2, eye).reshape(B_KV * CMP_HIDDEN, KV_W).astype(_BF16)
    pe8 = jnp.broadcast_to(pe.reshape(1, L_CMP * HEAD_DIM), (8, L_CMP * HEAD_DIM)).astype(_BF16)
    return pe8, w1.astype(_BF16), top, bot, w2bd


def _compress_call(k_cmp, v_cmp, wk, wv, bsz, seq):
    nr = seq // STRIDE_CMP
    rk = k_cmp.reshape(bsz, nr, STRIDE_CMP * KV_W)
    rv = v_cmp.reshape(bsz, nr, STRIDE_CMP * KV_W)
    pek, w1k, tk, bk, w2k = wk
    pev, w1v, tv, bv, w2v = wv
    rspec = pl.BlockSpec((1, nr, STRIDE_CMP * KV_W), lambda b: (b, 0, 0))
    full = lambda a: pl.BlockSpec(a.shape, lambda b: (0,) * a.ndim)
    ospec = pl.BlockSpec((1, nr, KV_W), lambda b: (b, 0, 0))
    return pl.pallas_call(
        _compress_kernel,
        grid=(bsz,),
        in_specs=[rspec, rspec, full(pek), full(pev), full(w1k), full(w1v),
                  full(tk), full(bk), full(w2k), full(tv), full(bv), full(w2v)],
        out_specs=[ospec, ospec],
        out_shape=[jax.ShapeDtypeStruct((bsz, nr, KV_W), _BF16)] * 2,
        compiler_params=pltpu.CompilerParams(dimension_semantics=("arbitrary",), vmem_limit_bytes=VMEM_LIMIT),
        name="compress",
    )(rk, rv, pek, pev, w1k, w1v, tk, bk, w2k, tv, bv, w2v)


def _topk_rows_mask(sc_t, k):
    n_rows = sc_t.shape[0]
    row = lax.broadcasted_iota(jnp.int32, sc_t.shape, 0).astype(sc_t.dtype)
    taken = jnp.asarray(-3e38, sc_t.dtype)
    for _ in range(k):
        m = jnp.max(sc_t, axis=0, keepdims=True)
        idx = jnp.min(jnp.where(sc_t == m, row, jnp.asarray(n_rows, sc_t.dtype)), axis=0, keepdims=True)
        sc_t = jnp.where(row == idx, taken, sc_t)
    return jnp.where(sc_t == taken, 1.0, 0.0).astype(_F32)


def _attn_kernel(q_ref, gate_ref, gexp_ref, kc_ref, vc_ref, ks_ref, vs0_ref, vs1_ref,
                 kw_ref, vw_ref, gob_ref, *rest, seq, n_cast):
    cast_in, o_ref, cast_out = rest[:n_cast], rest[n_cast], rest[n_cast + 1:2 * n_cast + 1]
    m_scr, acc_scr = rest[2 * n_cast + 1:]
    for src, dst in zip(cast_in, cast_out):
        dst[...] = src[...].astype(dst.dtype)

    qb = pl.program_id(1)
    t0 = qb * Q_BLOCK
    n_cmp = kc_ref.shape[1]
    n_sb = seq // L_SEL
    k_top = min(N_SEL, n_sb)
    rows = B_HPG * Q_BLOCK
    vs_refs = (vs0_ref, vs1_ref)

    lane_q = lax.broadcasted_iota(jnp.int32, (Q_BLOCK, LANES), 1)
    lo = lane_q < HEAD_DIM
    t_col = t0 + lax.broadcasted_iota(jnp.int32, (Q_BLOCK, 1), 0)

    def per_head(x):
        return x.reshape(B_HPG, Q_BLOCK, x.shape[-1])

    def add_bias(s, bias):
        return (per_head(s) + bias[None]).reshape(rows, s.shape[-1])

    qs = []
    for g in range(B_KV):
        own = jnp.where((lane_q // HEAD_DIM) == g, 1.0, 0.0).astype(q_ref.dtype)
        qs.append(jnp.concatenate([q_ref[0, :, j * LANES:(j + 1) * LANES] * own for j in range(B_HPG)], axis=0))

    thr0 = (t0 - (L_CMP - 1)) // STRIDE_CMP
    n_thr = (Q_BLOCK - 1) // STRIDE_CMP + 2
    assert n_thr <= LANES
    thr_rel = (t_col - (L_CMP - 1)) // STRIDE_CMP - thr0
    q_thr = jnp.where(lane_q == thr_rel, 1.0, 0.0).astype(_BF16)
    c_row1 = lax.broadcasted_iota(jnp.int32, (n_cmp, LANES), 0)
    c_lane1 = lax.broadcasted_iota(jnp.int32, (n_cmp, LANES), 1)
    k_thr = jnp.where((c_lane1 < n_thr) & (c_row1 > thr0 + c_lane1), NEG, 0.0).astype(_BF16)
    kc_wide = jnp.concatenate([kc_ref[0], k_thr], axis=1)
    has_c = (t_col >= L_CMP - 1).astype(_F32)
    c_row = lax.broadcasted_iota(jnp.int32, (n_cmp, n_sb), 0) * STRIDE_CMP
    s_col = lax.broadcasted_iota(jnp.int32, (n_cmp, n_sb), 1) * L_SEL
    overlap = jnp.where((c_row < s_col + L_SEL) & (c_row + L_CMP > s_col), 1.0, 0.0).astype(_BF16)
    blk = lax.broadcasted_iota(jnp.int32, (Q_BLOCK, n_sb), 1)
    cur = t_col // L_SEL
    forced = (blk == 0) | (blk == cur) | (blk == cur - 1)
    valid = blk * L_SEL <= t_col

    o_c, sel_bias = [], []
    for g in range(B_KV):
        s_c = _dot_nt(jnp.concatenate([qs[g], jnp.concatenate([q_thr] * B_HPG, axis=0)], axis=1), kc_wide)
        e_c = jnp.exp2(s_c - jnp.max(s_c, axis=-1, keepdims=True))
        inv = per_head(1.0 / jnp.maximum(jnp.sum(e_c, axis=-1, keepdims=True), 1e-30)) * has_c[None]
        e_bf = e_c.astype(_BF16)
        o_c.append((per_head(_dot(e_bf, vc_ref[0])) * inv).reshape(rows, LANES))
        p_bf = per_head(e_bf) * inv.astype(_BF16)
        imp = _dot(functools.reduce(lambda a, b: a + b, [p_bf[j] for j in range(B_HPG)]), overlap)
        score = jnp.where(valid & jnp.logical_not(forced), imp, -FORCE)
        chosen = forced | (_topk_rows_mask(score.T, k_top - 3).T > 0.5)
        sb = jnp.where(chosen & valid, 0.0, NEG)
        if n_sb < LANES:
            sb = jnp.concatenate([sb, jnp.full((Q_BLOCK, LANES - n_sb), NEG, _F32)], axis=1)
        sel_bias.append(sb)

    w_len = WIN + WIN_Q
    lane_w = lax.broadcasted_iota(jnp.int32, (w_len, LANES), 1)
    keep_w = [jnp.where((lane_w // HEAD_DIM) == g, 1.0, 0.0).astype(vw_ref.dtype) for g in range(B_KV)]
    fill_w = [1 - k for k in keep_w]
    acc_w_sub = [[] for _ in range(B_KV)]
    for hh in range(Q_BLOCK // WIN_Q):
        w_start = pl.multiple_of(jnp.maximum(t0 + hh * WIN_Q - WIN, 0), WIN_Q)
        t_sub = t_col[hh * WIN_Q:(hh + 1) * WIN_Q]
        diff_w = t_sub - (w_start + lax.broadcasted_iota(jnp.int32, (WIN_Q, w_len), 1))
        bias_w = jnp.where((diff_w >= 0) & (diff_w < WIN), 0.0, NEG)
        kw = kw_ref[0, pl.ds(w_start, w_len), :]
        vw = vw_ref[0, pl.ds(w_start, w_len), :]
        for g in range(B_KV):
            q_sub = per_head(qs[g])[:, hh * WIN_Q:(hh + 1) * WIN_Q, :].reshape(B_HPG * WIN_Q, LANES)
            s_w = (_dot_nt(q_sub, kw).reshape(B_HPG, WIN_Q, w_len) + bias_w[None]).reshape(B_HPG * WIN_Q, w_len)
            e_w = jnp.exp2(s_w - jnp.max(s_w, axis=-1, keepdims=True)).astype(_BF16)
            vw_aug = vw * keep_w[g] + fill_w[g]
            acc_w_sub[g].append(_dot(e_w, vw_aug).reshape(B_HPG, WIN_Q, LANES))
    acc_w = [jnp.concatenate(acc_w_sub[g], axis=1).reshape(rows, LANES) for g in range(B_KV)]

    n_ck = (t0 + Q_BLOCK + SEL_CK - 1) // SEL_CK
    key_lane = lax.broadcasted_iota(jnp.int32, (Q_BLOCK, SEL_CK), 1)
    bias_diag = jnp.where((n_ck - 1) * SEL_CK + key_lane <= t_col, 0.0, NEG)
    q_wide = [jnp.concatenate([qs[g], jnp.concatenate([sel_bias[g].astype(_BF16)] * B_HPG, axis=0)], axis=1)
              for g in range(B_KV)]

    m_scr[...] = jnp.full(m_scr.shape, NEG, _F32)
    acc_scr[...] = jnp.zeros(acc_scr.shape, _F32)

    def sel_chunk(ci, diag):
        k0 = pl.multiple_of(ci * SEL_CK, SEL_CK)
        for g in range(B_KV):
            m = m_scr[g]
            s = _dot_nt(q_wide[g], ks_ref[0, pl.ds(k0, SEL_CK), :])
            if diag:
                s = add_bias(s, bias_diag)
            m_new = jnp.maximum(m, jnp.max(s, axis=-1, keepdims=True))
            p = jnp.exp2(s - jnp.concatenate([m_new] * (SEL_CK // LANES), axis=1)).astype(_BF16)
            acc_scr[g] = jnp.exp2(m - m_new) * acc_scr[g] + _dot(p, vs_refs[g][0, pl.ds(k0, SEL_CK), :])
            m_scr[g] = m_new

    @pl.loop(0, n_ck - 1)
    def _(ci):
        sel_chunk(ci, False)

    sel_chunk(n_ck - 1, True)
    acc_s = [acc_scr[g] for g in range(B_KV)]

    def numer(acc):
        return jnp.concatenate([jnp.where(lo, acc[0][j * Q_BLOCK:(j + 1) * Q_BLOCK],
                                          acc[1][j * Q_BLOCK:(j + 1) * Q_BLOCK]) for j in range(B_HPG)], axis=1)

    def denom(acc):
        return jnp.concatenate([pltpu.roll(jnp.where(lo, acc[1][j * Q_BLOCK:(j + 1) * Q_BLOCK],
                                                     acc[0][j * Q_BLOCK:(j + 1) * Q_BLOCK]), HEAD_DIM, 1)
                                for j in range(B_HPG)], axis=1)

    gates = gate_ref[0]
    g_hi = gates.astype(_BF16)
    g_split = jnp.concatenate([g_hi, (gates - g_hi.astype(_F32)).astype(_BF16)], axis=1)
    gate_of = lambda r: _dot(g_split, gexp_ref[r])
    ob = (gate_of(0) * numer(o_c)
          + gate_of(1) * numer(acc_s) * (1.0 / jnp.maximum(denom(acc_s), 1e-30))
          + gate_of(2) * numer(acc_w) * (1.0 / jnp.maximum(denom(acc_w), 1e-30)))
    o_ref[0] = _rms(ob, gob_ref[...]).astype(o_ref.dtype)


def _gate_expand():
    x = np.zeros((3, GATE_PAD, B_WIDTH), np.float32)
    for slot, h in enumerate(_PERM_HEADS):
        for r in range(3):
            x[r, 3 * h + r, slot * HEAD_DIM:(slot + 1) * HEAD_DIM] = 1.0
    return jnp.asarray(np.concatenate([x, x], axis=1), _BF16)


def _attn_call(q, gates, kc, vc, ks, vs0, vs1, kw, vw, gob, to_cast, bsz, seq):
    assert seq // L_SEL <= LANES
    n_cmp = kc.shape[1]
    n_q = seq // Q_BLOCK
    steps = bsz * n_q
    qspec = lambda w: pl.BlockSpec((1, Q_BLOCK, w), lambda b, i: (b, i, 0))
    full = lambda r, w=KV_W: pl.BlockSpec((1, r, w), lambda b, i: (b, 0, 0), pipeline_mode=pl.Buffered(1))
    sliced = [a.reshape(steps, a.size // (steps * a.shape[-1]), a.shape[-1]) for a in to_cast]
    cast_specs = [pl.BlockSpec((1,) + a.shape[1:], lambda b, i: (b * n_q + i, 0, 0)) for a in sliced]
    outs = pl.pallas_call(
        functools.partial(_attn_kernel, seq=seq, n_cast=len(sliced)),
        grid=(bsz, n_q),
        in_specs=[qspec(B_WIDTH), qspec(GATE_PAD),
                  pl.BlockSpec((3, 2 * GATE_PAD, B_WIDTH), lambda b, i: (0, 0, 0)),
                  full(n_cmp), full(n_cmp), full(seq, KV_W + LANES),
                  full(seq), full(seq), full(seq), full(seq),
                  pl.BlockSpec((1, B_WIDTH), lambda b, i: (0, 0))] + cast_specs,
        out_specs=[qspec(B_WIDTH)] + cast_specs,
        out_shape=[jax.ShapeDtypeStruct((bsz, seq, B_WIDTH), _BF16)]
        + [jax.ShapeDtypeStruct(a.shape, _BF16) for a in sliced],
        scratch_shapes=[pltpu.VMEM((B_KV, B_HPG * Q_BLOCK, LANES), _F32),
                        pltpu.VMEM((B_KV, B_HPG * Q_BLOCK, LANES), _F32)],
        compiler_params=pltpu.CompilerParams(dimension_semantics=("arbitrary", "arbitrary"),
                                             vmem_limit_bytes=VMEM_LIMIT_ATTN),
        name="attn",
    )(q, gates, _gate_expand(), kc, vc, ks, vs0, vs1, kw, vw, gob, *sliced)
    return outs[0], [o.reshape(a.shape) for o, a in zip(outs[1:], to_cast)]


def _pack_halves(x):
    w = x.shape[1] // 2
    bits = lambda v: lax.bitcast_convert_type(v.astype(_BF16).astype(_F32), jnp.uint32)
    return lax.bitcast_convert_type(bits(x[:, :w]) | (bits(x[:, w:]) >> 16), jnp.int32)


def _unpack_halves(p):
    u = lax.bitcast_convert_type(p, jnp.uint32)
    return (lax.bitcast_convert_type(u & jnp.uint32(0xFFFF0000), _F32),
            lax.bitcast_convert_type(u << 16, _F32))


def _post_kernel(x_ref, oa_ref, ob_ref, woa_ref, wob_ref, gmoe_ref, r_ref, h1_ref, hn_ref, rt_ref, cnt_ref):
    h1 = x_ref[...] + _dot(oa_ref[...], woa_ref[...]) + _dot(ob_ref[...], wob_ref[...])
    h1_ref[...] = h1
    hn = _rms(h1, gmoe_ref[...])
    hn_ref[...] = _pack_halves(hn)

    hn_hi = hn.astype(_BF16)
    hn_lo = (hn - hn_hi.astype(_F32)).astype(_BF16)
    hi_both = _dot(hn_hi, r_ref[...])
    logits = hi_both[:, :GATE_PAD] + (_dot(hn_lo, r_ref[:, :GATE_PAD]) + hi_both[:, GATE_PAD:])
    lt = logits.T[:ROUTER_ROWS]
    row = lax.broadcasted_iota(jnp.int32, lt.shape, 0)
    first_idx = lambda hit: jnp.min(jnp.where(hit, row, LANES), axis=0, keepdims=True)

    is_g = row < N_GROUPS
    lg = jnp.where(is_g, lt, NEG)
    mg = jnp.max(lg, axis=0, keepdims=True)
    sg = jnp.sum(jnp.where(is_g, jnp.exp(lg - mg), 0.0), axis=0, keepdims=True)
    pg_top = 1.0 / sg
    g_sel = first_idx(is_g & (lg == mg))

    e_lo = ROUTER_OFF + g_sel * EXPERTS_PER_GROUP
    is_e = (row >= e_lo) & (row < e_lo + EXPERTS_PER_GROUP)
    le = jnp.where(is_e, lt, NEG)
    m1 = jnp.max(le, axis=0, keepdims=True)
    se = jnp.sum(jnp.where(is_e, jnp.exp(le - m1), 0.0), axis=0, keepdims=True)
    i1 = first_idx(is_e & (le == m1))
    le2 = jnp.where(row == i1, NEG, le)
    m2 = jnp.max(le2, axis=0, keepdims=True)
    i2 = first_idx(is_e & (row != i1) & (le2 == m2))
    pe1 = 1.0 / se
    pe2 = jnp.exp(m2 - m1) / se
    denom = pe1 + pe2
    rrow = lax.broadcasted_iota(jnp.int32, (RT_ROWS, lt.shape[1]), 0)
    rt_t = (jnp.where(rrow == i1 - ROUTER_OFF, 1.0, 0.0)
            + jnp.where(rrow == i2 - ROUTER_OFF + N_EXPERTS, 1.0, 0.0)
            + jnp.where(rrow == RT_W, pg_top * pe1 / denom, 0.0)
            + jnp.where(rrow == RT_W + 1, pg_top * pe2 / denom, 0.0))
    rt = jnp.concatenate([rt_t, jnp.zeros((LANES - RT_ROWS, lt.shape[1]), _F32)], axis=0).T
    rt_ref[...] = rt

    @pl.when(pl.program_id(0) == 0)
    def _():
        cnt_ref[...] = jnp.zeros_like(cnt_ref)

    cnt_ref[...] += jnp.sum(rt, axis=0, keepdims=True)


def _post_call(x2, oa, ob, woa, wob, gmoe, r_cat, part):
    n = x2.shape[0] // MOE_PARTS
    tm = TM_POST
    first = part * (n // tm)
    row_in = lambda i: (first + i, 0)
    row = lambda i: (i, 0)
    const2 = lambda i: (0, 0)
    return pl.pallas_call(
        _post_kernel,
        grid=(n // tm,),
        in_specs=[pl.BlockSpec((tm, D_MODEL), row_in), pl.BlockSpec((tm, A_WIDTH), row_in),
                  pl.BlockSpec((tm, B_WIDTH), row_in), pl.BlockSpec((A_WIDTH, D_MODEL), const2),
                  pl.BlockSpec((B_WIDTH, D_MODEL), const2), pl.BlockSpec((1, D_MODEL), const2),
                  pl.BlockSpec((D_MODEL, 2 * GATE_PAD), const2)],
        out_specs=[pl.BlockSpec((tm, D_MODEL), row), pl.BlockSpec((tm, D_MODEL // 2), row),
                   pl.BlockSpec((tm, GATE_PAD), row), pl.BlockSpec((8, LANES), const2)],
        out_shape=[jax.ShapeDtypeStruct((n, D_MODEL), _F32), jax.ShapeDtypeStruct((n, D_MODEL // 2), jnp.int32),
                   jax.ShapeDtypeStruct((n, GATE_PAD), _F32), jax.ShapeDtypeStruct((8, LANES), _F32)],
        compiler_params=pltpu.CompilerParams(dimension_semantics=("arbitrary",), vmem_limit_bytes=VMEM_LIMIT),
        name="post",
    )(x2, oa, ob, woa, wob, gmoe, r_cat)


def _route_kernel(rt_ref, cnt_ref, dest_ref, meta_ref, off_ref, run_ref):
    tm = rt_ref.shape[0]
    lane = lax.broadcasted_iota(jnp.int32, (1, LANES), 1)
    first = lane < N_EXPERTS
    onehot = jnp.where(lane < 2 * N_EXPERTS, rt_ref[...], 0.0)

    @pl.when(pl.program_id(0) == 0)
    def _():
        cnt = jnp.where(lane < 2 * N_EXPERTS, cnt_ref[...], 0.0)
        c1 = jnp.where(first, cnt, 0.0)
        tot = c1 + jnp.where(first, pltpu.roll(cnt, LANES - N_EXPERTS, 1), 0.0)
        tiles = jnp.floor((tot + (TR_GMM - 1)) * (1.0 / TR_GMM))
        e_row = lax.broadcasted_iota(jnp.int32, (LANES, LANES), 0)
        e_col = lax.broadcasted_iota(jnp.int32, (LANES, LANES), 1)
        before = jnp.where(e_row < e_col, 1.0, 0.0).astype(_BF16)
        base = _dot(tiles.astype(_BF16), before) * TR_GMM
        off_ref[...] = jnp.where(first, base, 0.0) + pltpu.roll(jnp.where(first, base + c1, 0.0), N_EXPERTS, 1)
        run_ref[...] = jnp.zeros_like(run_ref)
        meta_ref[...] = tiles

    r_io = lax.broadcasted_iota(jnp.int32, (tm, tm), 0)
    c_io = lax.broadcasted_iota(jnp.int32, (tm, tm), 1)
    earlier = jnp.where(c_io < r_io, 1.0, 0.0).astype(_BF16)
    rank = _dot(earlier, onehot.astype(_BF16)) + run_ref[0:1, :]
    slot = onehot * (rank + off_ref[0:1, :])
    d1 = jnp.sum(jnp.where(first, slot, 0.0), axis=-1, keepdims=True)
    d2 = jnp.sum(jnp.where(first, 0.0, slot), axis=-1, keepdims=True)
    dest_ref[...] = (jnp.where(lane == 0, d1, 0.0) + jnp.where(lane == 1, d2, 0.0)).astype(jnp.int32)
    run_ref[...] += jnp.sum(onehot, axis=0, keepdims=True)


def _route_call(rt, cnt):
    n = rt.shape[0]
    tm = TM_ROUTE
    return pl.pallas_call(
        _route_kernel,
        grid=(n // tm,),
        in_specs=[pl.BlockSpec((tm, GATE_PAD), lambda i: (i, 0)), pl.BlockSpec((8, LANES), lambda i: (0, 0))],
        out_specs=[pl.BlockSpec((tm, LANES), lambda i: (i, 0)), pl.BlockSpec((8, LANES), lambda i: (0, 0))],
        out_shape=[jax.ShapeDtypeStruct((n, LANES), jnp.int32), jax.ShapeDtypeStruct((8, LANES), _F32)],
        scratch_shapes=[pltpu.VMEM((8, LANES), _F32)] * 2,
        compiler_params=pltpu.CompilerParams(dimension_semantics=("arbitrary",)),
        name="route",
    )(rt, cnt)


def _sc_mesh():
    return plsc.VectorSubcoreMesh(core_axis_name="c", subcore_axis_name="s")


def _sc_worker(n_rows):
    per = n_rows // (SC_CORES * SC_SUBCORES)
    return (lax.axis_index("s") * SC_CORES + lax.axis_index("c")) * per, per


def _dispatch_call(xp, d1, d2, n_slots):
    n, w = xp.shape
    assert n % (SC_CORES * SC_SUBCORES * SC_CHUNK) == 0

    @functools.partial(
        pl.kernel, mesh=_sc_mesh(), out_type=jax.ShapeDtypeStruct((n_slots, w), xp.dtype),
        scratch_types=[pltpu.VMEM((SC_CHUNK,), jnp.int32), pltpu.VMEM((SC_CHUNK,), jnp.int32),
                       pltpu.VMEM((SC_CHUNK, w), xp.dtype), pltpu.SemaphoreType.DMA],
        name="dispatch")
    def k(x_hbm, d1_hbm, d2_hbm, xs_hbm, i1_v, i2_v, rows_v, sem):
        row0, per = _sc_worker(n)

        @pl.loop(0, per // SC_CHUNK)
        def _(j):
            src = pl.ds(row0 + j * SC_CHUNK, SC_CHUNK)
            pltpu.sync_copy(d1_hbm.at[src], i1_v)
            pltpu.sync_copy(d2_hbm.at[src], i2_v)
            pltpu.sync_copy(x_hbm.at[src], rows_v)
            first = pltpu.async_copy(rows_v, xs_hbm.at[i1_v], sem)
            second = pltpu.async_copy(rows_v, xs_hbm.at[i2_v], sem)
            first.wait()
            second.wait()

    return k(xp, d1, d2)


def _combine_call(ys, d1, d2):
    n = d1.shape[0]
    w = ys.shape[1]
    assert n % (SC_CORES * SC_SUBCORES * SC_CHUNK) == 0
    out = jax.ShapeDtypeStruct((n, w), ys.dtype)

    @functools.partial(
        pl.kernel, mesh=_sc_mesh(), out_type=(out, out),
        scratch_types=[pltpu.VMEM((SC_CHUNK,), jnp.int32), pltpu.VMEM((SC_CHUNK, w), ys.dtype),
                       pltpu.SemaphoreType.DMA],
        name="combine")
    def k(ys_hbm, d1_hbm, d2_hbm, y1_hbm, y2_hbm, i_v, rows_v, sem):
        row0, per = _sc_worker(n)

        @pl.loop(0, per // SC_CHUNK)
        def _(j):
            dst = pl.ds(row0 + j * SC_CHUNK, SC_CHUNK)
            for d_hbm, y_hbm in ((d1_hbm, y1_hbm), (d2_hbm, y2_hbm)):
                pltpu.sync_copy(d_hbm.at[dst], i_v)
                pltpu.async_copy(ys_hbm.at[i_v], rows_v, sem).wait()
                pltpu.sync_copy(rows_v, y_hbm.at[dst])

    return k(ys, d1, d2)


def _gmm_kernel(te_ref, nu_ref, xs_ref, wg0_ref, wu0_ref, wd0_ref, wg1_ref, wu1_ref, wd1_ref, ys_ref):
    half = D_MODEL // 2
    t0 = 2 * pl.program_id(0)

    def ffn(rows, wg_ref, wu_ref, wd_ref):
        a, b = _unpack_halves(xs_ref[rows, :])
        a, b = a.astype(_BF16), b.astype(_BF16)
        gate = _dot(a, wg_ref[0, :half, :]) + _dot(b, wg_ref[0, half:, :])
        up = _dot(a, wu_ref[0, :half, :]) + _dot(b, wu_ref[0, half:, :])
        hid = gate * jax.nn.sigmoid(gate) * up
        ys_ref[rows, :] = _pack_halves(_dot(hid.astype(_BF16), wd_ref[0]))

    both = t0 + 1 < nu_ref[0]
    same = te_ref[t0] == te_ref[t0 + 1]

    @pl.when(both & same)
    def _():
        ffn(slice(0, 2 * TR_GMM), wg0_ref, wu0_ref, wd0_ref)

    @pl.when((t0 < nu_ref[0]) & jnp.logical_not(both & same))
    def _():
        ffn(slice(0, TR_GMM), wg0_ref, wu0_ref, wd0_ref)

    @pl.when(both & jnp.logical_not(same))
    def _():
        ffn(slice(TR_GMM, 2 * TR_GMM), wg1_ref, wu1_ref, wd1_ref)


def _gmm_call(tile_expert, n_used, xs, w_gate, w_up, w_down):
    n_slots, w = xs.shape
    assert (n_slots // TR_GMM) % 2 == 0
    last = lambda nu: jnp.maximum((nu[0] - 1) // 2, 0)
    rows = lambda u, te, nu: (jnp.minimum(u, last(nu)), 0)
    expert0 = lambda u, te, nu: (te[2 * u], 0, 0)
    expert1 = lambda u, te, nu: (te[2 * u + 1], 0, 0)
    w_in = lambda idx: pl.BlockSpec((1, D_MODEL, D_FF_EXPERT), idx)
    w_out = lambda idx: pl.BlockSpec((1, D_FF_EXPERT, D_MODEL), idx)
    return pl.pallas_call(
        _gmm_kernel,
        grid_spec=pltpu.PrefetchScalarGridSpec(
            num_scalar_prefetch=2, grid=(n_slots // (2 * TR_GMM),),
            in_specs=[pl.BlockSpec((2 * TR_GMM, w), rows),
                      w_in(expert0), w_in(expert0), w_out(expert0), w_in(expert1), w_in(expert1), w_out(expert1)],
            out_specs=pl.BlockSpec((2 * TR_GMM, w), rows)),
        out_shape=jax.ShapeDtypeStruct((n_slots, w), xs.dtype),
        compiler_params=pltpu.CompilerParams(dimension_semantics=("arbitrary",), vmem_limit_bytes=VMEM_LIMIT),
        name="gmm",
    )(tile_expert, n_used, xs, w_gate, w_up, w_down, w_gate, w_up, w_down)


def _final_kernel(h1_ref, y1_ref, y2_ref, rt_ref, p_ref, gple_ref, wpg_ref, wpp_ref, gfin_ref, *rest):
    o_ref = rest[-1]
    for r in range(h1_ref.shape[0] // SUB_MOE):
        rows = slice(r * SUB_MOE, (r + 1) * SUB_MOE)
        rt = rt_ref[rows, :]
        w1, w2 = rt[:, RT_W:RT_W + 1], rt[:, RT_W + 1:RT_W + 2]
        a1, b1 = _unpack_halves(y1_ref[rows, :])
        a2, b2 = _unpack_halves(y2_ref[rows, :])
        h2 = h1_ref[rows, :] + jnp.concatenate([w1 * a1 + w2 * a2, w1 * b1 + w2 * b2], axis=1)
        gate = jax.nn.sigmoid(_dot(_rms(h2, gple_ref[...]).astype(_BF16), wpg_ref[...]))
        h3 = h2 + _dot(p_ref[rows, :].astype(_BF16), wpp_ref[...]) * gate
        o_ref[rows, :] = _rms(h3, gfin_ref[...])


def _final_call(h1, y1, y2, rt, p2, gple, wpg, wpp, gfin, part, out_prev):
    n = h1.shape[0]
    tm = TM_POST
    first = part * (n // tm)
    row = lambda i: (i, 0)
    row_full = lambda i: (first + i, 0)
    const2 = lambda i: (0, 0)
    in_specs = [pl.BlockSpec((tm, D_MODEL), row), pl.BlockSpec((tm, D_MODEL // 2), row),
                pl.BlockSpec((tm, D_MODEL // 2), row), pl.BlockSpec((tm, GATE_PAD), row),
                pl.BlockSpec((tm, D_PLE), row_full), pl.BlockSpec((1, D_MODEL), const2),
                pl.BlockSpec((D_MODEL, D_MODEL), const2), pl.BlockSpec((D_PLE, D_MODEL), const2),
                pl.BlockSpec((1, D_MODEL), const2)]
    args = [h1, y1, y2, rt, p2, gple, wpg, wpp, gfin]
    aliases = {}
    if out_prev is not None:
        in_specs.append(pl.BlockSpec(memory_space=pl.ANY))
        args.append(out_prev)
        aliases = {len(args) - 1: 0}
    return pl.pallas_call(
        _final_kernel,
        grid=(n // tm,),
        in_specs=in_specs,
        out_specs=pl.BlockSpec((tm, D_MODEL), row_full),
        out_shape=jax.ShapeDtypeStruct((MOE_PARTS * n, D_MODEL), _F32),
        input_output_aliases=aliases,
        compiler_params=pltpu.CompilerParams(dimension_semantics=("arbitrary",), vmem_limit_bytes=VMEM_LIMIT),
        name="final",
    )(*args)


def _rope_tables(seq):
    half = HEAD_DIM // 2
    inv = 1.0 / (ROPE_THETA ** (jnp.arange(half, dtype=_F32) / half))
    ang = jnp.arange(seq, dtype=_F32)[:, None] * inv[None, :]
    cos, sin = jnp.cos(ang), jnp.sin(ang)
    reps = LANES // HEAD_DIM
    cos_t = jnp.tile(jnp.concatenate([cos, cos], axis=1), (1, reps))
    sin_t = jnp.tile(jnp.concatenate([-sin, sin], axis=1), (1, reps))
    return cos_t, sin_t


def _layer(h, p_i, norm_mix, w_in, gmlp_v_norm, gmlp_w_s, gmlp_b_s,
           cmp_pe_k, cmp_w1_k, cmp_w2_k, cmp_pe_v, cmp_w1_v, cmp_w2_v,
           out_norm_a, out_norm_b, w_o, norm_moe, router_group, router_expert,
           moe_w_gate, moe_w_up, moe_w_down, norm_ple, w_ple_proj, w_ple_gate, norm_final):
    bsz, seq, _ = h.shape
    n = bsz * seq
    x2 = h.reshape(n, D_MODEL)
    row = lambda v: v.reshape(1, -1).astype(_F32)

    w_q = _perm_heads(w_in[:, OFF_Q:OFF_KV], 1)
    w_gate = jnp.pad(w_in[:, OFF_GATE:D_IN], ((0, 0), (0, GATE_PAD - N_GATES)))
    w_all = jnp.concatenate([w_in[:, :OFF_Q], w_q, w_in[:, OFF_KV:OFF_GATE], w_gate], axis=1).astype(_BF16)
    cos_t, sin_t = _rope_tables(seq)
    ws_pairs = gmlp_w_s.reshape(A_HEADS // 2, 2, CHUNK, CHUNK).transpose(0, 2, 1, 3).reshape(
        A_HEADS // 2, CHUNK, 2 * CHUNK)
    bs_exp = jnp.repeat(gmlp_b_s.T, HEAD_DIM, axis=1)

    oa, q, k_cmp, v_cmp, ks, vs0, vs1, k_win, v_win, gates = _proj_call(
        x2, row(norm_mix), w_all, cos_t, sin_t, row(gmlp_v_norm), ws_pairs, bs_exp, row(out_norm_a), seq)

    kc, vc = _compress_call(k_cmp, v_cmp, _compress_weights(cmp_w1_k, cmp_w2_k, cmp_pe_k),
                            _compress_weights(cmp_w1_v, cmp_w2_v, cmp_pe_v), bsz, seq)

    b3 = lambda a: a.reshape(bsz, seq, a.shape[-1])
    ob, (wg_bf, wu_bf, wd_bf) = _attn_call(
        b3(q), b3(gates), kc, vc, b3(ks), b3(vs0), b3(vs1), b3(k_win), b3(v_win),
        row(_perm_heads(out_norm_b, 0)), [moe_w_gate, moe_w_up, moe_w_down], bsz, seq)

    r_cat = jnp.pad(jnp.concatenate([router_group, router_expert], axis=1),
                    ((0, 0), (0, GATE_PAD - N_GROUPS - N_EXPERTS)))
    r_hi = r_cat.astype(_BF16)
    r_cat = jnp.concatenate([r_hi, (r_cat - r_hi.astype(_F32)).astype(_BF16)], axis=1)
    woa, wob = w_o[:A_WIDTH].astype(_BF16), _perm_heads(w_o[A_WIDTH:], 0).astype(_BF16)
    wpg, wpp = w_ple_gate.astype(_BF16), w_ple_proj.astype(_BF16)

    n_part = n // MOE_PARTS
    n_tiles = 2 * n_part // TR_GMM + N_EXPERTS
    out = None
    for part in range(MOE_PARTS):
        h1, xp, rt, cnt = _post_call(x2, oa, ob.reshape(n, B_WIDTH), woa, wob, row(norm_moe), r_cat, part)
        dest, meta = _route_call(rt, cnt)
        d1, d2 = dest[:, 0], dest[:, 1]
        ends = jnp.cumsum(meta[0, :N_EXPERTS].astype(jnp.int32))
        tile_expert = jnp.minimum(jnp.sum(ends[None, :] <= jnp.arange(n_tiles)[:, None], axis=1),
                                  N_EXPERTS - 1).astype(jnp.int32)
        xs = _dispatch_call(xp, d1, d2, n_tiles * TR_GMM)
        ys = _gmm_call(tile_expert, ends[-1:], xs, wg_bf, wu_bf, wd_bf)
        y1, y2 = _combine_call(ys, d1, d2)
        out = _final_call(h1, y1, y2, rt, p_i.reshape(n, D_PLE), row(norm_ple), wpg, wpp, row(norm_final),
                          part, out)
    return out.reshape(bsz, seq, D_MODEL)


def kernel(x, p, norm_mix, w_in, gmlp_v_norm, gmlp_w_s, gmlp_b_s, cmp_pe_k, cmp_w1_k, cmp_w2_k,
           cmp_pe_v, cmp_w1_v, cmp_w2_v, out_norm_a, out_norm_b, w_o, norm_moe, router_group,
           router_expert, moe_w_gate, moe_w_up, moe_w_down, norm_ple, w_ple_proj, w_ple_gate, norm_final):
    assert p.shape[0] == 1, "single-layer trunk"
    assert x.shape[1] % SEL_CK == 0 and x.shape[1] >= WIN + Q_BLOCK
    assert (x.shape[0] * x.shape[1]) % (MOE_PARTS * TM_POST) == 0
    return _layer(x, p[0], norm_mix[0], w_in[0], gmlp_v_norm[0], gmlp_w_s[0], gmlp_b_s[0],
                  cmp_pe_k[0], cmp_w1_k[0], cmp_w2_k[0], cmp_pe_v[0], cmp_w1_v[0], cmp_w2_v[0],
                  out_norm_a[0], out_norm_b[0], w_o[0], norm_moe[0], router_group[0], router_expert[0],
                  moe_w_gate[0], moe_w_up[0], moe_w_down[0], norm_ple[0], w_ple_proj[0], w_ple_gate[0],
                  norm_final)
```

```python
import functools

import numpy as np
import jax
import jax.numpy as jnp
from jax import lax
from jax.experimental import pallas as pl
from jax.experimental.pallas import tpu as pltpu
from jax.experimental.pallas import tpu_sc as plsc

D_MODEL = 1024
HEAD_DIM = 64
A_HEADS = 8
A_WIDTH = A_HEADS * HEAD_DIM
B_HEADS = 8
B_WIDTH = B_HEADS * HEAD_DIM
B_KV = 2
B_HPG = B_HEADS // B_KV
KV_W = B_KV * HEAD_DIM
N_GATES = B_HEADS * 3
CHUNK = 128
L_CMP = 32
STRIDE_CMP = 16
CMP_HIDDEN = 256
L_SEL = 64
N_SEL = 16
WIN = 512
WIN_Q = 128
Q_BLOCK = 512
ROPE_THETA = 10000.0
N_GROUPS = 4
EXPERTS_PER_GROUP = 4
N_EXPERTS = N_GROUPS * EXPERTS_PER_GROUP
D_FF_EXPERT = 512
D_PLE = 256
EPS = 1e-6
LOG2E = 1.4426950408889634
NEG = -1e30
FORCE = 1e6

OFF_Q = 2 * A_WIDTH
OFF_KV = OFF_Q + B_WIDTH
OFF_GATE = OFF_KV + 6 * KV_W
D_IN = OFF_GATE + N_GATES

LANES = 128
GATE_PAD = LANES
ROUTER_OFF = N_GROUPS
W_ALL = OFF_GATE + GATE_PAD

TM_PROJ = 1024
TM_POST = 1024
TM_ROUTE = 512
TR_GMM = 512
SUB_MOE = 512
RT_W = 2 * N_EXPERTS
RT_ROWS = 40
ROUTER_ROWS = 24
SC_CORES = 2
SC_SUBCORES = 16
SC_CHUNK = 128
MOE_PARTS = 1
SEL_CK = 512
VMEM_LIMIT = 56 * 1024 * 1024
VMEM_LIMIT_ATTN = 60 * 1024 * 1024

_PERM_HEADS = [0, 4, 1, 5, 2, 6, 3, 7]


def _perm_heads(a, axis):
    return jnp.concatenate([lax.slice_in_dim(a, h * HEAD_DIM, (h + 1) * HEAD_DIM, axis=axis) for h in _PERM_HEADS],
                           axis=axis)

_F32 = jnp.float32
_BF16 = jnp.bfloat16


def _dot(a, b):
    return jnp.dot(a, b, preferred_element_type=_F32)


def _dot_nt(a, b):
    return lax.dot_general(a, b, (((1,), (1,)), ((), ())), preferred_element_type=_F32)


def _rms(x, g):
    return x * lax.rsqrt(jnp.mean(x * x, axis=-1, keepdims=True) + EPS) * g


def _gelu(x):
    return 0.5 * x * (1.0 + jnp.tanh(0.7978845608028654 * (x + 0.044715 * (x * x * x))))


def _rope_tile(x, cos, sin_signed):
    lane = lax.broadcasted_iota(jnp.int32, x.shape, 1)
    first_half = (lane % HEAD_DIM) < (HEAD_DIM // 2)
    rot = jnp.where(first_half, pltpu.roll(x, LANES - HEAD_DIM // 2, 1), pltpu.roll(x, HEAD_DIM // 2, 1))
    return x * cos + rot * sin_signed


def _proj_kernel(x_ref, gmix_ref, w_ref, cos_ref, sin_ref, gv_ref, ws_ref, bs_ref, goa_ref,
                 oa_ref, q_ref, kc_ref, vc_ref, ks_ref, vs0_ref, vs1_ref, kw_ref, vw_ref, gate_ref,
                 *, seq):
    tm = x_ref.shape[0]
    hn = _rms(x_ref[...], gmix_ref[...]).astype(_BF16)
    cos = cos_ref[...]
    sin = sin_ref[...]

    zu = _gelu(_dot(hn, w_ref[:, 0:A_WIDTH]))
    zv = _gelu(_dot(hn, w_ref[:, A_WIDTH:2 * A_WIDTH]))
    vn = _rms(zv, gv_ref[...]).astype(_BF16)

    t_io = lax.broadcasted_iota(jnp.int32, (CHUNK, 2 * CHUNK), 0)
    s_io = lax.broadcasted_iota(jnp.int32, (CHUNK, 2 * CHUNK), 1) % CHUNK
    causal = s_io <= t_io
    lane = lax.broadcasted_iota(jnp.int32, (CHUNK, LANES), 1)
    lo = lane < HEAD_DIM
    bs = bs_ref[...]
    n_chunks = tm // CHUNK
    pair_cols = []
    for pr in range(A_HEADS // 2):
        wcat = jnp.where(causal, ws_ref[pr], 0.0).astype(_BF16)
        rhs = []
        for c in range(n_chunks):
            vblk = vn[c * CHUNK:(c + 1) * CHUNK, pr * LANES:(pr + 1) * LANES]
            zero = jnp.zeros_like(vblk)
            rhs.append(jnp.concatenate([jnp.where(lo, vblk, zero), jnp.where(lo, zero, vblk)], axis=0))
        out = _dot(wcat, jnp.concatenate(rhs, axis=1))
        pair_cols.append(jnp.concatenate([out[:, c * LANES:(c + 1) * LANES] for c in range(n_chunks)], axis=0))
    mixed = jnp.concatenate(pair_cols, axis=1) + jnp.concatenate([bs] * n_chunks, axis=0)
    oa = zu * mixed
    oa_ref[...] = _rms(oa, goa_ref[...]).astype(oa_ref.dtype)

    zq = _dot(hn, w_ref[:, OFF_Q:OFF_KV])
    scale = HEAD_DIM ** -0.5 * LOG2E
    for j in range(B_WIDTH // LANES):
        blk = _rope_tile(zq[:, j * LANES:(j + 1) * LANES], cos, sin) * scale
        q_ref[:, j * LANES:(j + 1) * LANES] = blk.astype(q_ref.dtype)

    zkv = _dot(hn, w_ref[:, OFF_KV:OFF_GATE])
    kv = []
    for j in range(6):
        blk = zkv[:, j * KV_W:(j + 1) * KV_W]
        kv.append(_rope_tile(blk, cos, sin) if j % 2 == 0 else blk)
    k_cmp, v_cmp, k_slc, v_slc, k_win, v_win = kv
    kc_ref[...] = k_cmp.astype(kc_ref.dtype)
    vc_ref[...] = v_cmp.astype(vc_ref.dtype)
    kw_ref[...] = k_win.astype(kw_ref.dtype)
    vw_ref[...] = v_win.astype(vw_ref.dtype)

    row = lax.broadcasted_iota(jnp.int32, (tm, LANES), 0)
    lane = lax.broadcasted_iota(jnp.int32, (tm, LANES), 1)
    key_block = ((pl.program_id(0) % (seq // tm)) * tm + row) // L_SEL
    ks_ref[:, :LANES] = k_slc.astype(ks_ref.dtype)
    ks_ref[:, LANES:] = jnp.where(lane == key_block, 1.0, 0.0).astype(ks_ref.dtype)
    for g, vs_ref in enumerate((vs0_ref, vs1_ref)):
        vs_ref[...] = jnp.where((lane // HEAD_DIM) == g, v_slc, 1.0).astype(vs_ref.dtype)

    zg = _dot(hn, w_ref[:, OFF_GATE:W_ALL])
    gate_ref[...] = jax.nn.sigmoid(zg)


def _proj_call(x2, gmix, w_all, cos_t, sin_t, gv, ws_pairs, bs_exp, goa, seq):
    n = x2.shape[0]
    tm = TM_PROJ
    n_t = seq // tm
    row = lambda i: (i, 0)
    const2 = lambda i: (0, 0)
    pos = lambda i: (i % n_t, 0)
    out_shapes = [jax.ShapeDtypeStruct((n, A_WIDTH), _BF16), jax.ShapeDtypeStruct((n, B_WIDTH), _BF16)]
    kv_widths = [KV_W, KV_W, KV_W + LANES, KV_W, KV_W, KV_W, KV_W]
    out_shapes += [jax.ShapeDtypeStruct((n, w), _BF16) for w in kv_widths]
    out_shapes += [jax.ShapeDtypeStruct((n, GATE_PAD), _F32)]
    out_specs = [pl.BlockSpec((tm, A_WIDTH), row), pl.BlockSpec((tm, B_WIDTH), row)]
    out_specs += [pl.BlockSpec((tm, w), row) for w in kv_widths]
    out_specs += [pl.BlockSpec((tm, GATE_PAD), row)]
    return pl.pallas_call(
        functools.partial(_proj_kernel, seq=seq),
        grid=(n // tm,),
        in_specs=[
            pl.BlockSpec((tm, D_MODEL), row),
            pl.BlockSpec((1, D_MODEL), const2),
            pl.BlockSpec((D_MODEL, W_ALL), const2),
            pl.BlockSpec((tm, LANES), pos),
            pl.BlockSpec((tm, LANES), pos),
            pl.BlockSpec((1, A_WIDTH), const2),
            pl.BlockSpec((A_HEADS // 2, CHUNK, 2 * CHUNK), lambda i: (0, 0, 0)),
            pl.BlockSpec((CHUNK, A_WIDTH), const2),
            pl.BlockSpec((1, A_WIDTH), const2),
        ],
        out_specs=out_specs,
        out_shape=out_shapes,
        compiler_params=pltpu.CompilerParams(dimension_semantics=("arbitrary",), vmem_limit_bytes=VMEM_LIMIT),
        name="proj",
    )(x2, gmix, w_all, cos_t, sin_t, gv, ws_pairs, bs_exp, goa)


def _compress_kernel(rk_ref, rv_ref, pek_ref, pev_ref, w1k_ref, w1v_ref,
                     tk_ref, bk_ref, w2k_ref, tv_ref, bv_ref, w2v_ref, kc_ref, vc_ref):
    def one(r_ref, pe_ref, w1_ref, top_ref, bot_ref, w2_ref, o_ref):
        r = r_ref[0]
        nr = r.shape[0]
        a = _dot(r, top_ref[...])
        b = _dot(r, bot_ref[...])
        pe_h = _dot(pe_ref[...], w1_ref[...])
        pe2 = jnp.concatenate([pe_h[0:1], pe_h[0:1]], axis=1)
        hid = a + pltpu.roll(b, nr - 1, 0) + pe2
        o_ref[0] = _dot(_gelu(hid).astype(_BF16), w2_ref[...]).astype(o_ref.dtype)

    one(rk_ref, pek_ref, w1k_ref, tk_ref, bk_ref, w2k_ref, kc_ref)
    one(rv_ref, pev_ref, w1v_ref, tv_ref, bv_ref, w2v_ref, vc_ref)


def _compress_weights(w1, w2, pe):
    half = L_CMP // 2
    w1r = w1.reshape(L_CMP, HEAD_DIM, CMP_HIDDEN)
    eye = jnp.eye(B_KV, dtype=w1.dtype)
    place = lambda part: jnp.einsum('ldj,gh->lgdhj', part, eye).reshape(half * KV_W, B_KV * CMP_HIDDEN)
    top = place(w1r[:half]).astype(_BF16)
    bot = place(w1r[half:]).astype(_BF16)
    w2bd = jnp.einsum('jd,gh->gjhd', w2, eye).reshape(B_KV * CMP_HIDDEN, KV_W).astype(_BF16)
    pe8 = jnp.broadcast_to(pe.reshape(1, L_CMP * HEAD_DIM), (8, L_CMP * HEAD_DIM)).astype(_BF16)
    return pe8, w1.astype(_BF16), top, bot, w2bd


def _compress_call(k_cmp, v_cmp, wk, wv, bsz, seq):
    nr = seq // STRIDE_CMP
    rk = k_cmp.reshape(bsz, nr, STRIDE_CMP * KV_W)
    rv = v_cmp.reshape(bsz, nr, STRIDE_CMP * KV_W)
    pek, w1k, tk, bk, w2k = wk
    pev, w1v, tv, bv, w2v = wv
    rspec = pl.BlockSpec((1, nr, STRIDE_CMP * KV_W), lambda b: (b, 0, 0))
    full = lambda a: pl.BlockSpec(a.shape, lambda b: (0,) * a.ndim)
    ospec = pl.BlockSpec((1, nr, KV_W), lambda b: (b, 0, 0))
    return pl.pallas_call(
        _compress_kernel,
        grid=(bsz,),
        in_specs=[rspec, rspec, full(pek), full(pev), full(w1k), full(w1v),
                  full(tk), full(bk), full(w2k), full(tv), full(bv), full(w2v)],
        out_specs=[ospec, ospec],
        out_shape=[jax.ShapeDtypeStruct((bsz, nr, KV_W), _BF16)] * 2,
        compiler_params=pltpu.CompilerParams(dimension_semantics=("arbitrary",), vmem_limit_bytes=VMEM_LIMIT),
        name="compress",
    )(rk, rv, pek, pev, w1k, w1v, tk, bk, w2k, tv, bv, w2v)


def _topk_rows_mask(sc_t, k):
    n_rows = sc_t.shape[0]
    row = lax.broadcasted_iota(jnp.int32, sc_t.shape, 0).astype(sc_t.dtype)
    taken = jnp.asarray(-3e38, sc_t.dtype)
    for _ in range(k):
        m = jnp.max(sc_t, axis=0, keepdims=True)
        idx = jnp.min(jnp.where(sc_t == m, row, jnp.asarray(n_rows, sc_t.dtype)), axis=0, keepdims=True)
        sc_t = jnp.where(row == idx, taken, sc_t)
    return jnp.where(sc_t == taken, 1.0, 0.0).astype(_F32)


def _attn_kernel(q_ref, gate_ref, gexp_ref, kc_ref, vc_ref, ks_ref, vs0_ref, vs1_ref,
                 kw_ref, vw_ref, gob_ref, *rest, seq, n_cast):
    cast_in, o_ref, cast_out = rest[:n_cast], rest[n_cast], rest[n_cast + 1:2 * n_cast + 1]
    m_scr, acc_scr = rest[2 * n_cast + 1:]
    for src, dst in zip(cast_in, cast_out):
        dst[...] = src[...].astype(dst.dtype)

    qb = pl.program_id(1)
    t0 = qb * Q_BLOCK
    n_cmp = kc_ref.shape[1]
    n_sb = seq // L_SEL
    k_top = min(N_SEL, n_sb)
    rows = B_HPG * Q_BLOCK
    vs_refs = (vs0_ref, vs1_ref)

    lane_q = lax.broadcasted_iota(jnp.int32, (Q_BLOCK, LANES), 1)
    lo = lane_q < HEAD_DIM
    t_col = t0 + lax.broadcasted_iota(jnp.int32, (Q_BLOCK, 1), 0)

    def per_head(x):
        return x.reshape(B_HPG, Q_BLOCK, x.shape[-1])

    def add_bias(s, bias):
        return (per_head(s) + bias[None]).reshape(rows, s.shape[-1])

    qs = []
    for g in range(B_KV):
        own = jnp.where((lane_q // HEAD_DIM) == g, 1.0, 0.0).astype(q_ref.dtype)
        qs.append(jnp.concatenate([q_ref[0, :, j * LANES:(j + 1) * LANES] * own for j in range(B_HPG)], axis=0))

    thr0 = (t0 - (L_CMP - 1)) // STRIDE_CMP
    n_thr = (Q_BLOCK - 1) // STRIDE_CMP + 2
    assert n_thr <= LANES
    thr_rel = (t_col - (L_CMP - 1)) // STRIDE_CMP - thr0
    q_thr = jnp.where(lane_q == thr_rel, 1.0, 0.0).astype(_BF16)
    c_row1 = lax.broadcasted_iota(jnp.int32, (n_cmp, LANES), 0)
    c_lane1 = lax.broadcasted_iota(jnp.int32, (n_cmp, LANES), 1)
    k_thr = jnp.where((c_lane1 < n_thr) & (c_row1 > thr0 + c_lane1), NEG, 0.0).astype(_BF16)
    kc_wide = jnp.concatenate([kc_ref[0], k_thr], axis=1)
    has_c = (t_col >= L_CMP - 1).astype(_F32)
    c_row = lax.broadcasted_iota(jnp.int32, (n_cmp, n_sb), 0) * STRIDE_CMP
    s_col = lax.broadcasted_iota(jnp.int32, (n_cmp, n_sb), 1) * L_SEL
    overlap = jnp.where((c_row < s_col + L_SEL) & (c_row + L_CMP > s_col), 1.0, 0.0).astype(_BF16)
    blk = lax.broadcasted_iota(jnp.int32, (Q_BLOCK, n_sb), 1)
    cur = t_col // L_SEL
    forced = (blk == 0) | (blk == cur) | (blk == cur - 1)
    valid = blk * L_SEL <= t_col

    o_c, sel_bias = [], []
    for g in range(B_KV):
        s_c = _dot_nt(jnp.concatenate([qs[g], jnp.concatenate([q_thr] * B_HPG, axis=0)], axis=1), kc_wide)
        e_c = jnp.exp2(s_c - jnp.max(s_c, axis=-1, keepdims=True))
        inv = per_head(1.0 / jnp.maximum(jnp.sum(e_c, axis=-1, keepdims=True), 1e-30)) * has_c[None]
        e_bf = e_c.astype(_BF16)
        o_c.append((per_head(_dot(e_bf, vc_ref[0])) * inv).reshape(rows, LANES))
        p_bf = per_head(e_bf) * inv.astype(_BF16)
        imp = _dot(functools.reduce(lambda a, b: a + b, [p_bf[j] for j in range(B_HPG)]), overlap)
        score = jnp.where(valid & jnp.logical_not(forced), imp, -FORCE)
        chosen = forced | (_topk_rows_mask(score.T, k_top - 3).T > 0.5)
        sb = jnp.where(chosen & valid, 0.0, NEG)
        if n_sb < LANES:
            sb = jnp.concatenate([sb, jnp.full((Q_BLOCK, LANES - n_sb), NEG, _F32)], axis=1)
        sel_bias.append(sb)

    w_len = WIN + WIN_Q
    lane_w = lax.broadcasted_iota(jnp.int32, (w_len, LANES), 1)
    keep_w = [jnp.where((lane_w // HEAD_DIM) == g, 1.0, 0.0).astype(vw_ref.dtype) for g in range(B_KV)]
    fill_w = [1 - k for k in keep_w]
    acc_w_sub = [[] for _ in range(B_KV)]
    for hh in range(Q_BLOCK // WIN_Q):
        w_start = pl.multiple_of(jnp.maximum(t0 + hh * WIN_Q - WIN, 0), WIN_Q)
        t_sub = t_col[hh * WIN_Q:(hh + 1) * WIN_Q]
        diff_w = t_sub - (w_start + lax.broadcasted_iota(jnp.int32, (WIN_Q, w_len), 1))
        bias_w = jnp.where((diff_w >= 0) & (diff_w < WIN), 0.0, NEG)
        kw = kw_ref[0, pl.ds(w_start, w_len), :]
        vw = vw_ref[0, pl.ds(w_start, w_len), :]
        for g in range(B_KV):
            q_sub = per_head(qs[g])[:, hh * WIN_Q:(hh + 1) * WIN_Q, :].reshape(B_HPG * WIN_Q, LANES)
            s_w = (_dot_nt(q_sub, kw).reshape(B_HPG, WIN_Q, w_len) + bias_w[None]).reshape(B_HPG * WIN_Q, w_len)
            e_w = jnp.exp2(s_w - jnp.max(s_w, axis=-1, keepdims=True)).astype(_BF16)
            vw_aug = vw * keep_w[g] + fill_w[g]
            acc_w_sub[g].append(_dot(e_w, vw_aug).reshape(B_HPG, WIN_Q, LANES))
    acc_w = [jnp.concatenate(acc_w_sub[g], axis=1).reshape(rows, LANES) for g in range(B_KV)]

    n_ck = (t0 + Q_BLOCK + SEL_CK - 1) // SEL_CK
    key_lane = lax.broadcasted_iota(jnp.int32, (Q_BLOCK, SEL_CK), 1)
    bias_diag = jnp.where((n_ck - 1) * SEL_CK + key_lane <= t_col, 0.0, NEG)
    q_wide = [jnp.concatenate([qs[g], jnp.concatenate([sel_bias[g].astype(_BF16)] * B_HPG, axis=0)], axis=1)
              for g in range(B_KV)]

    m_scr[...] = jnp.full(m_scr.shape, NEG, _F32)
    acc_scr[...] = jnp.zeros(acc_scr.shape, _F32)

    def sel_chunk(ci, diag):
        k0 = pl.multiple_of(ci * SEL_CK, SEL_CK)
        for g in range(B_KV):
            m = m_scr[g]
            s = _dot_nt(q_wide[g], ks_ref[0, pl.ds(k0, SEL_CK), :])
            if diag:
                s = add_bias(s, bias_diag)
            m_new = jnp.maximum(m, jnp.max(s, axis=-1, keepdims=True))
            p = jnp.exp2(s - jnp.concatenate([m_new] * (SEL_CK // LANES), axis=1)).astype(_BF16)
            acc_scr[g] = jnp.exp2(m - m_new) * acc_scr[g] + _dot(p, vs_refs[g][0, pl.ds(k0, SEL_CK), :])
            m_scr[g] = m_new

    @pl.loop(0, n_ck - 1)
    def _(ci):
        sel_chunk(ci, False)

    sel_chunk(n_ck - 1, True)
    acc_s = [acc_scr[g] for g in range(B_KV)]

    def numer(acc):
        return jnp.concatenate([jnp.where(lo, acc[0][j * Q_BLOCK:(j + 1) * Q_BLOCK],
                                          acc[1][j * Q_BLOCK:(j + 1) * Q_BLOCK]) for j in range(B_HPG)], axis=1)

    def denom(acc):
        return jnp.concatenate([pltpu.roll(jnp.where(lo, acc[1][j * Q_BLOCK:(j + 1) * Q_BLOCK],
                                                     acc[0][j * Q_BLOCK:(j + 1) * Q_BLOCK]), HEAD_DIM, 1)
                                for j in range(B_HPG)], axis=1)

    gates = gate_ref[0]
    g_hi = gates.astype(_BF16)
    g_split = jnp.concatenate([g_hi, (gates - g_hi.astype(_F32)).astype(_BF16)], axis=1)
    gate_of = lambda r: _dot(g_split, gexp_ref[r])
    ob = (gate_of(0) * numer(o_c)
          + gate_of(1) * numer(acc_s) * (1.0 / jnp.maximum(denom(acc_s), 1e-30))
          + gate_of(2) * numer(acc_w) * (1.0 / jnp.maximum(denom(acc_w), 1e-30)))
    o_ref[0] = _rms(ob, gob_ref[...]).astype(o_ref.dtype)


def _gate_expand():
    x = np.zeros((3, GATE_PAD, B_WIDTH), np.float32)
    for slot, h in enumerate(_PERM_HEADS):
        for r in range(3):
            x[r, 3 * h + r, slot * HEAD_DIM:(slot + 1) * HEAD_DIM] = 1.0
    return jnp.asarray(np.concatenate([x, x], axis=1), _BF16)


def _attn_call(q, gates, kc, vc, ks, vs0, vs1, kw, vw, gob, to_cast, bsz, seq):
    assert seq // L_SEL <= LANES
    n_cmp = kc.shape[1]
    n_q = seq // Q_BLOCK
    steps = bsz * n_q
    qspec = lambda w: pl.BlockSpec((1, Q_BLOCK, w), lambda b, i: (b, i, 0))
    full = lambda r, w=KV_W: pl.BlockSpec((1, r, w), lambda b, i: (b, 0, 0), pipeline_mode=pl.Buffered(1))
    sliced = [a.reshape(steps, a.size // (steps * a.shape[-1]), a.shape[-1]) for a in to_cast]
    cast_specs = [pl.BlockSpec((1,) + a.shape[1:], lambda b, i: (b * n_q + i, 0, 0)) for a in sliced]
    outs = pl.pallas_call(
        functools.partial(_attn_kernel, seq=seq, n_cast=len(sliced)),
        grid=(bsz, n_q),
        in_specs=[qspec(B_WIDTH), qspec(GATE_PAD),
                  pl.BlockSpec((3, 2 * GATE_PAD, B_WIDTH), lambda b, i: (0, 0, 0)),
                  full(n_cmp), full(n_cmp), full(seq, KV_W + LANES),
                  full(seq), full(seq), full(seq), full(seq),
                  pl.BlockSpec((1, B_WIDTH), lambda b, i: (0, 0))] + cast_specs,
        out_specs=[qspec(B_WIDTH)] + cast_specs,
        out_shape=[jax.ShapeDtypeStruct((bsz, seq, B_WIDTH), _BF16)]
        + [jax.ShapeDtypeStruct(a.shape, _BF16) for a in sliced],
        scratch_shapes=[pltpu.VMEM((B_KV, B_HPG * Q_BLOCK, LANES), _F32),
                        pltpu.VMEM((B_KV, B_HPG * Q_BLOCK, LANES), _F32)],
        compiler_params=pltpu.CompilerParams(dimension_semantics=("arbitrary", "arbitrary"),
                                             vmem_limit_bytes=VMEM_LIMIT_ATTN),
        name="attn",
    )(q, gates, _gate_expand(), kc, vc, ks, vs0, vs1, kw, vw, gob, *sliced)
    return outs[0], [o.reshape(a.shape) for o, a in zip(outs[1:], to_cast)]


def _pack_halves(x):
    w = x.shape[1] // 2
    bits = lambda v: lax.bitcast_convert_type(v.astype(_BF16).astype(_F32), jnp.uint32)
    return lax.bitcast_convert_type(bits(x[:, :w]) | (bits(x[:, w:]) >> 16), jnp.int32)


def _unpack_halves(p):
    u = lax.bitcast_convert_type(p, jnp.uint32)
    return (lax.bitcast_convert_type(u & jnp.uint32(0xFFFF0000), _F32),
            lax.bitcast_convert_type(u << 16, _F32))


def _post_kernel(x_ref, oa_ref, ob_ref, woa_ref, wob_ref, gmoe_ref, r_ref, h1_ref, hn_ref, rt_ref, cnt_ref):
    h1 = x_ref[...] + _dot(oa_ref[...], woa_ref[...]) + _dot(ob_ref[...], wob_ref[...])
    h1_ref[...] = h1
    hn = _rms(h1, gmoe_ref[...])
    hn_ref[...] = _pack_halves(hn)

    hn_hi = hn.astype(_BF16)
    hn_lo = (hn - hn_hi.astype(_F32)).astype(_BF16)
    hi_both = _dot(hn_hi, r_ref[...])
    logits = hi_both[:, :GATE_PAD] + (_dot(hn_lo, r_ref[:, :GATE_PAD]) + hi_both[:, GATE_PAD:])
    lt = logits.T[:ROUTER_ROWS]
    row = lax.broadcasted_iota(jnp.int32, lt.shape, 0)
    first_idx = lambda hit: jnp.min(jnp.where(hit, row, LANES), axis=0, keepdims=True)

    is_g = row < N_GROUPS
    lg = jnp.where(is_g, lt, NEG)
    mg = jnp.max(lg, axis=0, keepdims=True)
    sg = jnp.sum(jnp.where(is_g, jnp.exp(lg - mg), 0.0), axis=0, keepdims=True)
    pg_top = 1.0 / sg
    g_sel = first_idx(is_g & (lg == mg))

    e_lo = ROUTER_OFF + g_sel * EXPERTS_PER_GROUP
    is_e = (row >= e_lo) & (row < e_lo + EXPERTS_PER_GROUP)
    le = jnp.where(is_e, lt, NEG)
    m1 = jnp.max(le, axis=0, keepdims=True)
    se = jnp.sum(jnp.where(is_e, jnp.exp(le - m1), 0.0), axis=0, keepdims=True)
    i1 = first_idx(is_e & (le == m1))
    le2 = jnp.where(row == i1, NEG, le)
    m2 = jnp.max(le2, axis=0, keepdims=True)
    i2 = first_idx(is_e & (row != i1) & (le2 == m2))
    pe1 = 1.0 / se
    pe2 = jnp.exp(m2 - m1) / se
    denom = pe1 + pe2
    rrow = lax.broadcasted_iota(jnp.int32, (RT_ROWS, lt.shape[1]), 0)
    rt_t = (jnp.where(rrow == i1 - ROUTER_OFF, 1.0, 0.0)
            + jnp.where(rrow == i2 - ROUTER_OFF + N_EXPERTS, 1.0, 0.0)
            + jnp.where(rrow == RT_W, pg_top * pe1 / denom, 0.0)
            + jnp.where(rrow == RT_W + 1, pg_top * pe2 / denom, 0.0))
    rt = jnp.concatenate([rt_t, jnp.zeros((LANES - RT_ROWS, lt.shape[1]), _F32)], axis=0).T
    rt_ref[...] = rt

    @pl.when(pl.program_id(0) == 0)
    def _():
        cnt_ref[...] = jnp.zeros_like(cnt_ref)

    cnt_ref[...] += jnp.sum(rt, axis=0, keepdims=True)


def _post_call(x2, oa, ob, woa, wob, gmoe, r_cat, part):
    n = x2.shape[0] // MOE_PARTS
    tm = TM_POST
    first = part * (n // tm)
    row_in = lambda i: (first + i, 0)
    row = lambda i: (i, 0)
    const2 = lambda i: (0, 0)
    return pl.pallas_call(
        _post_kernel,
        grid=(n // tm,),
        in_specs=[pl.BlockSpec((tm, D_MODEL), row_in), pl.BlockSpec((tm, A_WIDTH), row_in),
                  pl.BlockSpec((tm, B_WIDTH), row_in), pl.BlockSpec((A_WIDTH, D_MODEL), const2),
                  pl.BlockSpec((B_WIDTH, D_MODEL), const2), pl.BlockSpec((1, D_MODEL), const2),
                  pl.BlockSpec((D_MODEL, 2 * GATE_PAD), const2)],
        out_specs=[pl.BlockSpec((tm, D_MODEL), row), pl.BlockSpec((tm, D_MODEL // 2), row),
                   pl.BlockSpec((tm, GATE_PAD), row), pl.BlockSpec((8, LANES), const2)],
        out_shape=[jax.ShapeDtypeStruct((n, D_MODEL), _F32), jax.ShapeDtypeStruct((n, D_MODEL // 2), jnp.int32),
                   jax.ShapeDtypeStruct((n, GATE_PAD), _F32), jax.ShapeDtypeStruct((8, LANES), _F32)],
        compiler_params=pltpu.CompilerParams(dimension_semantics=("arbitrary",), vmem_limit_bytes=VMEM_LIMIT),
        name="post",
    )(x2, oa, ob, woa, wob, gmoe, r_cat)


def _route_kernel(rt_ref, cnt_ref, earlier_ref, dest_ref, meta_ref, off_ref, run_ref):
    tm = rt_ref.shape[0]
    lane = lax.broadcasted_iota(jnp.int32, (1, LANES), 1)
    first = lane < N_EXPERTS
    onehot = jnp.where(lane < 2 * N_EXPERTS, rt_ref[...], 0.0)

    @pl.when(pl.program_id(0) == 0)
    def _():
        cnt = jnp.where(lane < 2 * N_EXPERTS, cnt_ref[...], 0.0)
        c1 = jnp.where(first, cnt, 0.0)
        tot = c1 + jnp.where(first, pltpu.roll(cnt, LANES - N_EXPERTS, 1), 0.0)
        tiles = jnp.floor((tot + (TR_GMM - 1)) * (1.0 / TR_GMM))
        e_row = lax.broadcasted_iota(jnp.int32, (LANES, LANES), 0)
        e_col = lax.broadcasted_iota(jnp.int32, (LANES, LANES), 1)
        before = jnp.where(e_row < e_col, 1.0, 0.0).astype(_BF16)
        base = _dot(tiles.astype(_BF16), before) * TR_GMM
        off_ref[...] = jnp.where(first, base, 0.0) + pltpu.roll(jnp.where(first, base + c1, 0.0), N_EXPERTS, 1)
        run_ref[...] = jnp.zeros_like(run_ref)
        meta_ref[...] = tiles

    rank = _dot(earlier_ref[...], onehot.astype(_BF16)) + run_ref[0:1, :]
    slot = onehot * (rank + off_ref[0:1, :])
    slot_t = slot.T
    d1 = jnp.sum(slot_t[:N_EXPERTS], axis=0, keepdims=True)
    d2 = jnp.sum(slot_t[N_EXPERTS:2 * N_EXPERTS], axis=0, keepdims=True)
    dest_ref[...] = jnp.concatenate([d1, d2, jnp.zeros((6, tm), _F32)], axis=0).astype(jnp.int32)
    run_ref[...] += jnp.sum(onehot, axis=0, keepdims=True)


def _route_call(rt, cnt):
    n = rt.shape[0]
    tm = TM_ROUTE
    return pl.pallas_call(
        _route_kernel,
        grid=(n // tm,),
        in_specs=[pl.BlockSpec((tm, GATE_PAD), lambda i: (i, 0)), pl.BlockSpec((8, LANES), lambda i: (0, 0)),
                  pl.BlockSpec((tm, tm), lambda i: (0, 0))],
        out_specs=[pl.BlockSpec((8, tm), lambda i: (0, i)), pl.BlockSpec((8, LANES), lambda i: (0, 0))],
        out_shape=[jax.ShapeDtypeStruct((8, n), jnp.int32), jax.ShapeDtypeStruct((8, LANES), _F32)],
        scratch_shapes=[pltpu.VMEM((8, LANES), _F32)] * 2,
        compiler_params=pltpu.CompilerParams(dimension_semantics=("arbitrary",)),
        name="route",
    )(rt, cnt, jnp.asarray(np.tril(np.ones((tm, tm), np.float32), -1), _BF16))


def _sc_mesh():
    return plsc.VectorSubcoreMesh(core_axis_name="c", subcore_axis_name="s")


def _sc_worker(n_rows):
    per = n_rows // (SC_CORES * SC_SUBCORES)
    return (lax.axis_index("s") * SC_CORES + lax.axis_index("c")) * per, per


def _dispatch_call(xp, d1, d2, n_slots):
    n, w = xp.shape
    assert n % (SC_CORES * SC_SUBCORES * SC_CHUNK) == 0

    @functools.partial(
        pl.kernel, mesh=_sc_mesh(), out_type=jax.ShapeDtypeStruct((n_slots, w), xp.dtype),
        scratch_types=[pltpu.VMEM((SC_CHUNK,), jnp.int32), pltpu.VMEM((SC_CHUNK,), jnp.int32),
                       pltpu.VMEM((SC_CHUNK, w), xp.dtype), pltpu.SemaphoreType.DMA],
        name="dispatch")
    def k(x_hbm, d1_hbm, d2_hbm, xs_hbm, i1_v, i2_v, rows_v, sem):
        row0, per = _sc_worker(n)

        @pl.loop(0, per // SC_CHUNK)
        def _(j):
            src = pl.ds(row0 + j * SC_CHUNK, SC_CHUNK)
            pltpu.sync_copy(d1_hbm.at[src], i1_v)
            pltpu.sync_copy(d2_hbm.at[src], i2_v)
            pltpu.sync_copy(x_hbm.at[src], rows_v)
            first = pltpu.async_copy(rows_v, xs_hbm.at[i1_v], sem)
            second = pltpu.async_copy(rows_v, xs_hbm.at[i2_v], sem)
            first.wait()
            second.wait()

    return k(xp, d1, d2)


def _combine_call(ys, d1, d2):
    n = d1.shape[0]
    w = ys.shape[1]
    assert n % (SC_CORES * SC_SUBCORES * SC_CHUNK) == 0
    out = jax.ShapeDtypeStruct((n, w), ys.dtype)

    @functools.partial(
        pl.kernel, mesh=_sc_mesh(), out_type=(out, out),
        scratch_types=[pltpu.VMEM((SC_CHUNK,), jnp.int32), pltpu.VMEM((SC_CHUNK, w), ys.dtype),
                       pltpu.SemaphoreType.DMA],
        name="combine")
    def k(ys_hbm, d1_hbm, d2_hbm, y1_hbm, y2_hbm, i_v, rows_v, sem):
        row0, per = _sc_worker(n)

        @pl.loop(0, per // SC_CHUNK)
        def _(j):
            dst = pl.ds(row0 + j * SC_CHUNK, SC_CHUNK)
            for d_hbm, y_hbm in ((d1_hbm, y1_hbm), (d2_hbm, y2_hbm)):
                pltpu.sync_copy(d_hbm.at[dst], i_v)
                pltpu.async_copy(ys_hbm.at[i_v], rows_v, sem).wait()
                pltpu.sync_copy(rows_v, y_hbm.at[dst])

    return k(ys, d1, d2)


def _gmm_kernel(te_ref, nu_ref, xs_ref, wg0_ref, wu0_ref, wd0_ref, wg1_ref, wu1_ref, wd1_ref, ys_ref):
    half = D_MODEL // 2
    t0 = 2 * pl.program_id(0)

    def ffn(rows, wg_ref, wu_ref, wd_ref):
        a, b = _unpack_halves(xs_ref[rows, :])
        a, b = a.astype(_BF16), b.astype(_BF16)
        gate = _dot(a, wg_ref[0, :half, :]) + _dot(b, wg_ref[0, half:, :])
        up = _dot(a, wu_ref[0, :half, :]) + _dot(b, wu_ref[0, half:, :])
        hid = gate * jax.nn.sigmoid(gate) * up
        ys_ref[rows, :] = _pack_halves(_dot(hid.astype(_BF16), wd_ref[0]))

    both = t0 + 1 < nu_ref[0]
    same = te_ref[t0] == te_ref[t0 + 1]

    @pl.when(both & same)
    def _():
        ffn(slice(0, 2 * TR_GMM), wg0_ref, wu0_ref, wd0_ref)

    @pl.when((t0 < nu_ref[0]) & jnp.logical_not(both & same))
    def _():
        ffn(slice(0, TR_GMM), wg0_ref, wu0_ref, wd0_ref)

    @pl.when(both & jnp.logical_not(same))
    def _():
        ffn(slice(TR_GMM, 2 * TR_GMM), wg1_ref, wu1_ref, wd1_ref)


def _gmm_call(tile_expert, n_used, xs, w_gate, w_up, w_down):
    n_slots, w = xs.shape
    assert (n_slots // TR_GMM) % 2 == 0
    last = lambda nu: jnp.maximum((nu[0] - 1) // 2, 0)
    rows = lambda u, te, nu: (jnp.minimum(u, last(nu)), 0)
    expert0 = lambda u, te, nu: (te[2 * u], 0, 0)
    expert1 = lambda u, te, nu: (te[2 * u + 1], 0, 0)
    w_in = lambda idx: pl.BlockSpec((1, D_MODEL, D_FF_EXPERT), idx)
    w_out = lambda idx: pl.BlockSpec((1, D_FF_EXPERT, D_MODEL), idx)
    return pl.pallas_call(
        _gmm_kernel,
        grid_spec=pltpu.PrefetchScalarGridSpec(
            num_scalar_prefetch=2, grid=(n_slots // (2 * TR_GMM),),
            in_specs=[pl.BlockSpec((2 * TR_GMM, w), rows),
                      w_in(expert0), w_in(expert0), w_out(expert0), w_in(expert1), w_in(expert1), w_out(expert1)],
            out_specs=pl.BlockSpec((2 * TR_GMM, w), rows)),
        out_shape=jax.ShapeDtypeStruct((n_slots, w), xs.dtype),
        compiler_params=pltpu.CompilerParams(dimension_semantics=("arbitrary",), vmem_limit_bytes=VMEM_LIMIT),
        name="gmm",
    )(tile_expert, n_used, xs, w_gate, w_up, w_down, w_gate, w_up, w_down)


def _final_kernel(h1_ref, y1_ref, y2_ref, rt_ref, p_ref, gple_ref, wpg_ref, wpp_ref, gfin_ref, *rest):
    o_ref = rest[-1]
    for r in range(h1_ref.shape[0] // SUB_MOE):
        rows = slice(r * SUB_MOE, (r + 1) * SUB_MOE)
        rt = rt_ref[rows, :]
        w1, w2 = rt[:, RT_W:RT_W + 1], rt[:, RT_W + 1:RT_W + 2]
        a1, b1 = _unpack_halves(y1_ref[rows, :])
        a2, b2 = _unpack_halves(y2_ref[rows, :])
        h2 = h1_ref[rows, :] + jnp.concatenate([w1 * a1 + w2 * a2, w1 * b1 + w2 * b2], axis=1)
        gate = jax.nn.sigmoid(_dot(_rms(h2, gple_ref[...]).astype(_BF16), wpg_ref[...]))
        h3 = h2 + _dot(p_ref[rows, :].astype(_BF16), wpp_ref[...]) * gate
        o_ref[rows, :] = _rms(h3, gfin_ref[...])


def _final_call(h1, y1, y2, rt, p2, gple, wpg, wpp, gfin, part, out_prev):
    n = h1.shape[0]
    tm = TM_POST
    first = part * (n // tm)
    row = lambda i: (i, 0)
    row_full = lambda i: (first + i, 0)
    const2 = lambda i: (0, 0)
    in_specs = [pl.BlockSpec((tm, D_MODEL), row), pl.BlockSpec((tm, D_MODEL // 2), row),
                pl.BlockSpec((tm, D_MODEL // 2), row), pl.BlockSpec((tm, GATE_PAD), row),
                pl.BlockSpec((tm, D_PLE), row_full), pl.BlockSpec((1, D_MODEL), const2),
                pl.BlockSpec((D_MODEL, D_MODEL), const2), pl.BlockSpec((D_PLE, D_MODEL), const2),
                pl.BlockSpec((1, D_MODEL), const2)]
    args = [h1, y1, y2, rt, p2, gple, wpg, wpp, gfin]
    aliases = {}
    if out_prev is not None:
        in_specs.append(pl.BlockSpec(memory_space=pl.ANY))
        args.append(out_prev)
        aliases = {len(args) - 1: 0}
    return pl.pallas_call(
        _final_kernel,
        grid=(n // tm,),
        in_specs=in_specs,
        out_specs=pl.BlockSpec((tm, D_MODEL), row_full),
        out_shape=jax.ShapeDtypeStruct((MOE_PARTS * n, D_MODEL), _F32),
        input_output_aliases=aliases,
        compiler_params=pltpu.CompilerParams(dimension_semantics=("arbitrary",), vmem_limit_bytes=VMEM_LIMIT),
        name="final",
    )(*args)


def _rope_tables(seq):
    half = HEAD_DIM // 2
    inv = 1.0 / (ROPE_THETA ** (jnp.arange(half, dtype=_F32) / half))
    ang = jnp.arange(seq, dtype=_F32)[:, None] * inv[None, :]
    cos, sin = jnp.cos(ang), jnp.sin(ang)
    reps = LANES // HEAD_DIM
    cos_t = jnp.tile(jnp.concatenate([cos, cos], axis=1), (1, reps))
    sin_t = jnp.tile(jnp.concatenate([-sin, sin], axis=1), (1, reps))
    return cos_t, sin_t


def _layer(h, p_i, norm_mix, w_in, gmlp_v_norm, gmlp_w_s, gmlp_b_s,
           cmp_pe_k, cmp_w1_k, cmp_w2_k, cmp_pe_v, cmp_w1_v, cmp_w2_v,
           out_norm_a, out_norm_b, w_o, norm_moe, router_group, router_expert,
           moe_w_gate, moe_w_up, moe_w_down, norm_ple, w_ple_proj, w_ple_gate, norm_final):
    bsz, seq, _ = h.shape
    n = bsz * seq
    x2 = h.reshape(n, D_MODEL)
    row = lambda v: v.reshape(1, -1).astype(_F32)

    w_q = _perm_heads(w_in[:, OFF_Q:OFF_KV], 1)
    w_gate = jnp.pad(w_in[:, OFF_GATE:D_IN], ((0, 0), (0, GATE_PAD - N_GATES)))
    w_all = jnp.concatenate([w_in[:, :OFF_Q], w_q, w_in[:, OFF_KV:OFF_GATE], w_gate], axis=1).astype(_BF16)
    cos_t, sin_t = _rope_tables(seq)
    ws_pairs = gmlp_w_s.reshape(A_HEADS // 2, 2, CHUNK, CHUNK).transpose(0, 2, 1, 3).reshape(
        A_HEADS // 2, CHUNK, 2 * CHUNK)
    bs_exp = jnp.repeat(gmlp_b_s.T, HEAD_DIM, axis=1)

    oa, q, k_cmp, v_cmp, ks, vs0, vs1, k_win, v_win, gates = _proj_call(
        x2, row(norm_mix), w_all, cos_t, sin_t, row(gmlp_v_norm), ws_pairs, bs_exp, row(out_norm_a), seq)

    kc, vc = _compress_call(k_cmp, v_cmp, _compress_weights(cmp_w1_k, cmp_w2_k, cmp_pe_k),
                            _compress_weights(cmp_w1_v, cmp_w2_v, cmp_pe_v), bsz, seq)

    b3 = lambda a: a.reshape(bsz, seq, a.shape[-1])
    ob, (wg_bf, wu_bf, wd_bf) = _attn_call(
        b3(q), b3(gates), kc, vc, b3(ks), b3(vs0), b3(vs1), b3(k_win), b3(v_win),
        row(_perm_heads(out_norm_b, 0)), [moe_w_gate, moe_w_up, moe_w_down], bsz, seq)

    r_cat = jnp.pad(jnp.concatenate([router_group, router_expert], axis=1),
                    ((0, 0), (0, GATE_PAD - N_GROUPS - N_EXPERTS)))
    r_hi = r_cat.astype(_BF16)
    r_cat = jnp.concatenate([r_hi, (r_cat - r_hi.astype(_F32)).astype(_BF16)], axis=1)
    woa, wob = w_o[:A_WIDTH].astype(_BF16), _perm_heads(w_o[A_WIDTH:], 0).astype(_BF16)
    wpg, wpp = w_ple_gate.astype(_BF16), w_ple_proj.astype(_BF16)

    n_part = n // MOE_PARTS
    n_tiles = 2 * n_part // TR_GMM + N_EXPERTS
    out = None
    for part in range(MOE_PARTS):
        h1, xp, rt, cnt = _post_call(x2, oa, ob.reshape(n, B_WIDTH), woa, wob, row(norm_moe), r_cat, part)
        dest, meta = _route_call(rt, cnt)
        d1, d2 = dest[0], dest[1]
        ends = jnp.cumsum(meta[0, :N_EXPERTS].astype(jnp.int32))
        tile_expert = jnp.minimum(jnp.sum(ends[None, :] <= jnp.arange(n_tiles)[:, None], axis=1),
                                  N_EXPERTS - 1).astype(jnp.int32)
        xs = _dispatch_call(xp, d1, d2, n_tiles * TR_GMM)
        ys = _gmm_call(tile_expert, ends[-1:], xs, wg_bf, wu_bf, wd_bf)
        y1, y2 = _combine_call(ys, d1, d2)
        out = _final_call(h1, y1, y2, rt, p_i.reshape(n, D_PLE), row(norm_ple), wpg, wpp, row(norm_final),
                          part, out)
    return out.reshape(bsz, seq, D_MODEL)


def kernel(x, p, norm_mix, w_in, gmlp_v_norm, gmlp_w_s, gmlp_b_s, cmp_pe_k, cmp_w1_k, cmp_w2_k,
           cmp_pe_v, cmp_w1_v, cmp_w2_v, out_norm_a, out_norm_b, w_o, norm_moe, router_group,
           router_expert, moe_w_gate, moe_w_up, moe_w_down, norm_ple, w_ple_proj, w_ple_gate, norm_final):
    assert p.shape[0] == 1, "single-layer trunk"
    assert x.shape[1] % SEL_CK == 0 and x.shape[1] >= WIN + Q_BLOCK
    assert (x.shape[0] * x.shape[1]) % (MOE_PARTS * TM_POST) == 0
    return _layer(x, p[0], norm_mix[0], w_in[0], gmlp_v_norm[0], gmlp_w_s[0], gmlp_b_s[0],
                  cmp_pe_k[0], cmp_w1_k[0], cmp_w2_k[0], cmp_pe_v[0], cmp_w1_v[0], cmp_w2_v[0],
                  out_norm_a[0], out_norm_b[0], w_o[0], norm_moe[0], router_group[0], router_expert[0],
                  moe_w_gate[0], moe_w_up[0], moe_w_down[0], norm_ple[0], w_ple_proj[0], w_ple_gate[0],
                  norm_final)
```

```python
import functools

import numpy as np
import jax
import jax.numpy as jnp
from jax import lax
from jax.experimental import pallas as pl
from jax.experimental.pallas import tpu as pltpu
from jax.experimental.pallas import tpu_sc as plsc

D_MODEL = 1024
HEAD_DIM = 64
A_HEADS = 8
A_WIDTH = A_HEADS * HEAD_DIM
B_HEADS = 8
B_WIDTH = B_HEADS * HEAD_DIM
B_KV = 2
B_HPG = B_HEADS // B_KV
KV_W = B_KV * HEAD_DIM
N_GATES = B_HEADS * 3
CHUNK = 128
L_CMP = 32
STRIDE_CMP = 16
CMP_HIDDEN = 256
L_SEL = 64
N_SEL = 16
WIN = 512
WIN_Q = 128
Q_BLOCK = 512
ROPE_THETA = 10000.0
N_GROUPS = 4
EXPERTS_PER_GROUP = 4
N_EXPERTS = N_GROUPS * EXPERTS_PER_GROUP
D_FF_EXPERT = 512
D_PLE = 256
EPS = 1e-6
LOG2E = 1.4426950408889634
NEG = -1e30
FORCE = 1e6

OFF_Q = 2 * A_WIDTH
OFF_KV = OFF_Q + B_WIDTH
OFF_GATE = OFF_KV + 6 * KV_W
D_IN = OFF_GATE + N_GATES

LANES = 128
SUBLANES = 8
GATE_PAD = LANES
ROUTER_OFF = N_GROUPS
W_ALL = OFF_GATE + GATE_PAD

TM_PROJ = 1024
TM_POST = 1024
TM_ROUTE = 512
TR_GMM = 512
SUB_MOE = 512
RT_W = 2 * N_EXPERTS
RT_ROWS = 40
ROUTER_ROWS = 24
SC_CORES = 2
SC_SUBCORES = 16
SC_CHUNK = 128
SEL_CK = 512
VMEM_LIMIT = 56 * 1024 * 1024
VMEM_LIMIT_ATTN = 60 * 1024 * 1024

_PERM_HEADS = [0, 4, 1, 5, 2, 6, 3, 7]


def _perm_heads(a, axis):
    return jnp.concatenate([lax.slice_in_dim(a, h * HEAD_DIM, (h + 1) * HEAD_DIM, axis=axis) for h in _PERM_HEADS],
                           axis=axis)

_F32 = jnp.float32
_BF16 = jnp.bfloat16


def _dot(a, b):
    return jnp.dot(a, b, preferred_element_type=_F32)


def _dot_nt(a, b):
    return lax.dot_general(a, b, (((1,), (1,)), ((), ())), preferred_element_type=_F32)


def _rms(x, g):
    return x * lax.rsqrt(jnp.mean(x * x, axis=-1, keepdims=True) + EPS) * g


def _gelu(x):
    return 0.5 * x * (1.0 + jnp.tanh(0.7978845608028654 * (x + 0.044715 * (x * x * x))))


def _rope_tile(x, cos, sin_signed):
    lane = lax.broadcasted_iota(jnp.int32, x.shape, 1)
    first_half = (lane % HEAD_DIM) < (HEAD_DIM // 2)
    rot = jnp.where(first_half, pltpu.roll(x, LANES - HEAD_DIM // 2, 1), pltpu.roll(x, HEAD_DIM // 2, 1))
    return x * cos + rot * sin_signed


def _proj_kernel(x_ref, gmix_ref, w_ref, cos_ref, sin_ref, gv_ref, ws_ref, bs_ref, goa_ref,
                 oa_ref, q_ref, kc_ref, vc_ref, ks_ref, vs0_ref, vs1_ref, kw_ref, vw_ref, gate_ref,
                 *, seq):
    tm = x_ref.shape[0]
    hn = _rms(x_ref[...], gmix_ref[...]).astype(_BF16)
    cos = cos_ref[...]
    sin = sin_ref[...]

    zu = _gelu(_dot(hn, w_ref[:, 0:A_WIDTH]))
    zv = _gelu(_dot(hn, w_ref[:, A_WIDTH:2 * A_WIDTH]))
    vn = _rms(zv, gv_ref[...]).astype(_BF16)

    t_io = lax.broadcasted_iota(jnp.int32, (CHUNK, 2 * CHUNK), 0)
    s_io = lax.broadcasted_iota(jnp.int32, (CHUNK, 2 * CHUNK), 1) % CHUNK
    causal = s_io <= t_io
    lane = lax.broadcasted_iota(jnp.int32, (CHUNK, LANES), 1)
    lo = lane < HEAD_DIM
    bs = bs_ref[...]
    n_chunks = tm // CHUNK
    pair_cols = []
    for pr in range(A_HEADS // 2):
        wcat = jnp.where(causal, ws_ref[pr], 0.0).astype(_BF16)
        rhs = []
        for c in range(n_chunks):
            vblk = vn[c * CHUNK:(c + 1) * CHUNK, pr * LANES:(pr + 1) * LANES]
            zero = jnp.zeros_like(vblk)
            rhs.append(jnp.concatenate([jnp.where(lo, vblk, zero), jnp.where(lo, zero, vblk)], axis=0))
        out = _dot(wcat, jnp.concatenate(rhs, axis=1))
        pair_cols.append(jnp.concatenate([out[:, c * LANES:(c + 1) * LANES] for c in range(n_chunks)], axis=0))
    mixed = jnp.concatenate(pair_cols, axis=1) + jnp.concatenate([bs] * n_chunks, axis=0)
    oa = zu * mixed
    oa_ref[...] = _rms(oa, goa_ref[...]).astype(oa_ref.dtype)

    zq = _dot(hn, w_ref[:, OFF_Q:OFF_KV])
    scale = HEAD_DIM ** -0.5 * LOG2E
    for j in range(B_WIDTH // LANES):
        blk = _rope_tile(zq[:, j * LANES:(j + 1) * LANES], cos, sin) * scale
        q_ref[:, j * LANES:(j + 1) * LANES] = blk.astype(q_ref.dtype)

    zkv = _dot(hn, w_ref[:, OFF_KV:OFF_GATE])
    kv = []
    for j in range(6):
        blk = zkv[:, j * KV_W:(j + 1) * KV_W]
        kv.append(_rope_tile(blk, cos, sin) if j % 2 == 0 else blk)
    k_cmp, v_cmp, k_slc, v_slc, k_win, v_win = kv
    kc_ref[...] = k_cmp.astype(kc_ref.dtype)
    vc_ref[...] = v_cmp.astype(vc_ref.dtype)
    kw_ref[...] = k_win.astype(kw_ref.dtype)
    vw_ref[...] = v_win.astype(vw_ref.dtype)

    row = lax.broadcasted_iota(jnp.int32, (tm, LANES), 0)
    lane = lax.broadcasted_iota(jnp.int32, (tm, LANES), 1)
    key_block = ((pl.program_id(0) % (seq // tm)) * tm + row) // L_SEL
    ks_ref[:, :LANES] = k_slc.astype(ks_ref.dtype)
    ks_ref[:, LANES:] = jnp.where(lane == key_block, 1.0, 0.0).astype(ks_ref.dtype)
    for g, vs_ref in enumerate((vs0_ref, vs1_ref)):
        vs_ref[...] = jnp.where((lane // HEAD_DIM) == g, v_slc, 1.0).astype(vs_ref.dtype)

    zg = _dot(hn, w_ref[:, OFF_GATE:W_ALL])
    gate_ref[...] = jax.nn.sigmoid(zg)


def _proj_call(x2, gmix, w_all, cos_t, sin_t, gv, ws_pairs, bs_exp, goa, seq):
    n = x2.shape[0]
    tm = TM_PROJ
    n_t = seq // tm
    row = lambda i: (i, 0)
    const2 = lambda i: (0, 0)
    pos = lambda i: (i % n_t, 0)
    out_shapes = [jax.ShapeDtypeStruct((n, A_WIDTH), _BF16), jax.ShapeDtypeStruct((n, B_WIDTH), _BF16)]
    kv_widths = [KV_W, KV_W, KV_W + LANES, KV_W, KV_W, KV_W, KV_W]
    out_shapes += [jax.ShapeDtypeStruct((n, w), _F32 if j < 2 else _BF16) for j, w in enumerate(kv_widths)]
    out_shapes += [jax.ShapeDtypeStruct((n, GATE_PAD), _F32)]
    out_specs = [pl.BlockSpec((tm, A_WIDTH), row), pl.BlockSpec((tm, B_WIDTH), row)]
    out_specs += [pl.BlockSpec((tm, w), row) for w in kv_widths]
    out_specs += [pl.BlockSpec((tm, GATE_PAD), row)]
    return pl.pallas_call(
        functools.partial(_proj_kernel, seq=seq),
        grid=(n // tm,),
        in_specs=[
            pl.BlockSpec((tm, D_MODEL), row),
            pl.BlockSpec((1, D_MODEL), const2),
            pl.BlockSpec((D_MODEL, W_ALL), const2),
            pl.BlockSpec((tm, LANES), pos),
            pl.BlockSpec((tm, LANES), pos),
            pl.BlockSpec((1, A_WIDTH), const2),
            pl.BlockSpec((A_HEADS // 2, CHUNK, 2 * CHUNK), lambda i: (0, 0, 0)),
            pl.BlockSpec((CHUNK, A_WIDTH), const2),
            pl.BlockSpec((1, A_WIDTH), const2),
        ],
        out_specs=out_specs,
        out_shape=out_shapes,
        compiler_params=pltpu.CompilerParams(dimension_semantics=("arbitrary",), vmem_limit_bytes=VMEM_LIMIT),
        name="proj",
    )(x2, gmix, w_all, cos_t, sin_t, gv, ws_pairs, bs_exp, goa)


def _compress_kernel(rk_ref, rv_ref, pek_ref, pev_ref, w1k_ref, w1v_ref,
                     tk_ref, bk_ref, w2k_ref, tv_ref, bv_ref, w2v_ref, kc_ref, vc_ref):
    def one(r_ref, pe_ref, w1_ref, top_ref, bot_ref, w2_ref, o_ref):
        nr = r_ref.shape[1] // STRIDE_CMP
        r = jnp.concatenate([r_ref[0, pl.ds(l, nr, stride=STRIDE_CMP), :] for l in range(STRIDE_CMP)],
                            axis=1).astype(_BF16)
        a = _dot(r, top_ref[...])
        b = _dot(r, bot_ref[...])
        pe_h = _dot(pe_ref[...], w1_ref[...])
        pe2 = jnp.concatenate([pe_h[0:1], pe_h[0:1]], axis=1)
        hid = a + pltpu.roll(b, nr - 1, 0) + pe2
        o_ref[0] = _dot(_gelu(hid).astype(_BF16), w2_ref[...]).astype(o_ref.dtype)

    one(rk_ref, pek_ref, w1k_ref, tk_ref, bk_ref, w2k_ref, kc_ref)
    one(rv_ref, pev_ref, w1v_ref, tv_ref, bv_ref, w2v_ref, vc_ref)


def _compress_weights(w1, w2, pe):
    half = L_CMP // 2
    w1r = w1.reshape(L_CMP, HEAD_DIM, CMP_HIDDEN)
    eye = jnp.eye(B_KV, dtype=w1.dtype)
    place = lambda part: jnp.einsum('ldj,gh->lgdhj', part, eye).reshape(half * KV_W, B_KV * CMP_HIDDEN)
    top = place(w1r[:half]).astype(_BF16)
    bot = place(w1r[half:]).astype(_BF16)
    w2bd = jnp.einsum('jd,gh->gjhd', w2, eye).reshape(B_KV * CMP_HIDDEN, KV_W).astype(_BF16)
    pe8 = jnp.broadcast_to(pe.reshape(1, L_CMP * HEAD_DIM), (SUBLANES, L_CMP * HEAD_DIM)).astype(_BF16)
    return pe8, w1.astype(_BF16), top, bot, w2bd


def _compress_call(k_cmp, v_cmp, wk, wv, bsz, seq):
    nr = seq // STRIDE_CMP
    rk = k_cmp.reshape(bsz, seq, KV_W)
    rv = v_cmp.reshape(bsz, seq, KV_W)
    pek, w1k, tk, bk, w2k = wk
    pev, w1v, tv, bv, w2v = wv
    rspec = pl.BlockSpec((1, seq, KV_W), lambda b: (b, 0, 0))
    full = lambda a: pl.BlockSpec(a.shape, lambda b: (0,) * a.ndim)
    ospec = pl.BlockSpec((1, nr, KV_W), lambda b: (b, 0, 0))
    return pl.pallas_call(
        _compress_kernel,
        grid=(bsz,),
        in_specs=[rspec, rspec, full(pek), full(pev), full(w1k), full(w1v),
                  full(tk), full(bk), full(w2k), full(tv), full(bv), full(w2v)],
        out_specs=[ospec, ospec],
        out_shape=[jax.ShapeDtypeStruct((bsz, nr, KV_W), _BF16)] * 2,
        compiler_params=pltpu.CompilerParams(dimension_semantics=("arbitrary",), vmem_limit_bytes=VMEM_LIMIT),
        name="compress",
    )(rk, rv, pek, pev, w1k, w1v, tk, bk, w2k, tv, bv, w2v)


def _topk_rows_mask(sc_t, k):
    n_rows = sc_t.shape[0]
    row = lax.broadcasted_iota(jnp.int32, sc_t.shape, 0).astype(sc_t.dtype)
    taken = jnp.asarray(-3e38, sc_t.dtype)
    for _ in range(k):
        m = jnp.max(sc_t, axis=0, keepdims=True)
        idx = jnp.min(jnp.where(sc_t == m, row, jnp.asarray(n_rows, sc_t.dtype)), axis=0, keepdims=True)
        sc_t = jnp.where(row == idx, taken, sc_t)
    return jnp.where(sc_t == taken, 1.0, 0.0).astype(_F32)


def _attn_kernel(q_ref, gate_ref, gexp_ref, kc_ref, vc_ref, ks_ref, vs0_ref, vs1_ref,
                 kw_ref, vw_ref, gob_ref, *rest, seq, n_cast):
    cast_in, o_ref, cast_out = rest[:n_cast], rest[n_cast], rest[n_cast + 1:2 * n_cast + 1]
    m_scr, acc_scr = rest[2 * n_cast + 1:]
    for src, dst in zip(cast_in, cast_out):
        dst[...] = src[...].astype(dst.dtype)

    qb = pl.program_id(1)
    t0 = qb * Q_BLOCK
    n_cmp = kc_ref.shape[1]
    n_sb = seq // L_SEL
    k_top = min(N_SEL, n_sb)
    rows = B_HPG * Q_BLOCK
    vs_refs = (vs0_ref, vs1_ref)

    lane_q = lax.broadcasted_iota(jnp.int32, (Q_BLOCK, LANES), 1)
    lo = lane_q < HEAD_DIM
    t_col = t0 + lax.broadcasted_iota(jnp.int32, (Q_BLOCK, 1), 0)

    def per_head(x):
        return x.reshape(B_HPG, Q_BLOCK, x.shape[-1])

    def add_bias(s, bias):
        return (per_head(s) + bias[None]).reshape(rows, s.shape[-1])

    qs = []
    for g in range(B_KV):
        own = jnp.where((lane_q // HEAD_DIM) == g, 1.0, 0.0).astype(q_ref.dtype)
        qs.append(jnp.concatenate([q_ref[0, :, j * LANES:(j + 1) * LANES] * own for j in range(B_HPG)], axis=0))

    thr0 = (t0 - (L_CMP - 1)) // STRIDE_CMP
    n_thr = (Q_BLOCK - 1) // STRIDE_CMP + 2
    assert n_thr <= LANES
    thr_rel = (t_col - (L_CMP - 1)) // STRIDE_CMP - thr0
    q_thr = jnp.where(lane_q == thr_rel, 1.0, 0.0).astype(_BF16)
    c_row1 = lax.broadcasted_iota(jnp.int32, (n_cmp, LANES), 0)
    c_lane1 = lax.broadcasted_iota(jnp.int32, (n_cmp, LANES), 1)
    k_thr = jnp.where((c_lane1 < n_thr) & (c_row1 > thr0 + c_lane1), NEG, 0.0).astype(_BF16)
    kc_wide = jnp.concatenate([kc_ref[0], k_thr], axis=1)
    has_c = (t_col >= L_CMP - 1).astype(_F32)
    c_row = lax.broadcasted_iota(jnp.int32, (n_cmp, n_sb), 0) * STRIDE_CMP
    s_col = lax.broadcasted_iota(jnp.int32, (n_cmp, n_sb), 1) * L_SEL
    overlap = jnp.where((c_row < s_col + L_SEL) & (c_row + L_CMP > s_col), 1.0, 0.0).astype(_BF16)
    blk = lax.broadcasted_iota(jnp.int32, (Q_BLOCK, n_sb), 1)
    cur = t_col // L_SEL
    forced = (blk == 0) | (blk == cur) | (blk == cur - 1)
    valid = blk * L_SEL <= t_col

    o_c, sel_bias = [], []
    for g in range(B_KV):
        s_c = _dot_nt(jnp.concatenate([qs[g], jnp.concatenate([q_thr] * B_HPG, axis=0)], axis=1), kc_wide)
        e_c = jnp.exp2(s_c - jnp.max(s_c, axis=-1, keepdims=True))
        inv = per_head(1.0 / jnp.maximum(jnp.sum(e_c, axis=-1, keepdims=True), 1e-30)) * has_c[None]
        e_bf = e_c.astype(_BF16)
        o_c.append((per_head(_dot(e_bf, vc_ref[0])) * inv).reshape(rows, LANES))
        p_bf = per_head(e_bf) * inv.astype(_BF16)
        imp = _dot(functools.reduce(lambda a, b: a + b, [p_bf[j] for j in range(B_HPG)]), overlap)
        score = jnp.where(valid & jnp.logical_not(forced), imp, -FORCE)
        chosen = forced | (_topk_rows_mask(score.T, k_top - 3).T > 0.5)
        sb = jnp.where(chosen & valid, 0.0, NEG)
        if n_sb < LANES:
            sb = jnp.concatenate([sb, jnp.full((Q_BLOCK, LANES - n_sb), NEG, _F32)], axis=1)
        sel_bias.append(sb)

    w_len = WIN + WIN_Q
    lane_w = lax.broadcasted_iota(jnp.int32, (w_len, LANES), 1)
    keep_w = [jnp.where((lane_w // HEAD_DIM) == g, 1.0, 0.0).astype(vw_ref.dtype) for g in range(B_KV)]
    fill_w = [1 - k for k in keep_w]
    acc_w_sub = [[] for _ in range(B_KV)]
    for hh in range(Q_BLOCK // WIN_Q):
        w_start = pl.multiple_of(jnp.maximum(t0 + hh * WIN_Q - WIN, 0), WIN_Q)
        t_sub = t_col[hh * WIN_Q:(hh + 1) * WIN_Q]
        diff_w = t_sub - (w_start + lax.broadcasted_iota(jnp.int32, (WIN_Q, w_len), 1))
        bias_w = jnp.where((diff_w >= 0) & (diff_w < WIN), 0.0, NEG)
        kw = kw_ref[0, pl.ds(w_start, w_len), :]
        vw = vw_ref[0, pl.ds(w_start, w_len), :]
        for g in range(B_KV):
            q_sub = per_head(qs[g])[:, hh * WIN_Q:(hh + 1) * WIN_Q, :].reshape(B_HPG * WIN_Q, LANES)
            s_w = (_dot_nt(q_sub, kw).reshape(B_HPG, WIN_Q, w_len) + bias_w[None]).reshape(B_HPG * WIN_Q, w_len)
            e_w = jnp.exp2(s_w - jnp.max(s_w, axis=-1, keepdims=True)).astype(_BF16)
            vw_aug = vw * keep_w[g] + fill_w[g]
            acc_w_sub[g].append(_dot(e_w, vw_aug).reshape(B_HPG, WIN_Q, LANES))
    acc_w = [jnp.concatenate(acc_w_sub[g], axis=1).reshape(rows, LANES) for g in range(B_KV)]

    n_ck = (t0 + Q_BLOCK + SEL_CK - 1) // SEL_CK
    key_lane = lax.broadcasted_iota(jnp.int32, (Q_BLOCK, SEL_CK), 1)
    bias_diag = jnp.where((n_ck - 1) * SEL_CK + key_lane <= t_col, 0.0, NEG)
    q_wide = [jnp.concatenate([qs[g], jnp.concatenate([sel_bias[g].astype(_BF16)] * B_HPG, axis=0)], axis=1)
              for g in range(B_KV)]

    m_scr[...] = jnp.full(m_scr.shape, NEG, _F32)
    acc_scr[...] = jnp.zeros(acc_scr.shape, _F32)

    def sel_chunk(ci, diag):
        k0 = pl.multiple_of(ci * SEL_CK, SEL_CK)
        for g in range(B_KV):
            m = m_scr[g]
            s = _dot_nt(q_wide[g], ks_ref[0, pl.ds(k0, SEL_CK), :])
            if diag:
                s = add_bias(s, bias_diag)
            m_new = jnp.maximum(m, jnp.max(s, axis=-1, keepdims=True))
            p = jnp.exp2(s - jnp.concatenate([m_new] * (SEL_CK // LANES), axis=1)).astype(_BF16)
            acc_scr[g] = jnp.exp2(m - m_new) * acc_scr[g] + _dot(p, vs_refs[g][0, pl.ds(k0, SEL_CK), :])
            m_scr[g] = m_new

    @pl.loop(0, n_ck - 1)
    def _(ci):
        sel_chunk(ci, False)

    sel_chunk(n_ck - 1, True)
    acc_s = [acc_scr[g] for g in range(B_KV)]

    def numer(acc):
        return jnp.concatenate([jnp.where(lo, acc[0][j * Q_BLOCK:(j + 1) * Q_BLOCK],
                                          acc[1][j * Q_BLOCK:(j + 1) * Q_BLOCK]) for j in range(B_HPG)], axis=1)

    def denom(acc):
        return jnp.concatenate([pltpu.roll(jnp.where(lo, acc[1][j * Q_BLOCK:(j + 1) * Q_BLOCK],
                                                     acc[0][j * Q_BLOCK:(j + 1) * Q_BLOCK]), HEAD_DIM, 1)
                                for j in range(B_HPG)], axis=1)

    gates = gate_ref[0]
    g_hi = gates.astype(_BF16)
    g_split = jnp.concatenate([g_hi, (gates - g_hi.astype(_F32)).astype(_BF16)], axis=1)
    gate_of = lambda r: _dot(g_split, gexp_ref[r])
    ob = (gate_of(0) * numer(o_c)
          + gate_of(1) * numer(acc_s) * (1.0 / jnp.maximum(denom(acc_s), 1e-30))
          + gate_of(2) * numer(acc_w) * (1.0 / jnp.maximum(denom(acc_w), 1e-30)))
    o_ref[0] = _rms(ob, gob_ref[...]).astype(o_ref.dtype)


def _gate_expand():
    x = np.zeros((3, GATE_PAD, B_WIDTH), np.float32)
    for slot, h in enumerate(_PERM_HEADS):
        for r in range(3):
            x[r, 3 * h + r, slot * HEAD_DIM:(slot + 1) * HEAD_DIM] = 1.0
    return jnp.asarray(np.concatenate([x, x], axis=1), _BF16)


def _attn_call(q, gates, kc, vc, ks, vs0, vs1, kw, vw, gob, to_cast, bsz, seq):
    assert seq // L_SEL <= LANES
    n_cmp = kc.shape[1]
    n_q = seq // Q_BLOCK
    steps = bsz * n_q
    qspec = lambda w: pl.BlockSpec((1, Q_BLOCK, w), lambda b, i: (b, i, 0))
    full = lambda r, w=KV_W: pl.BlockSpec((1, r, w), lambda b, i: (b, 0, 0), pipeline_mode=pl.Buffered(1))
    sliced = [a.reshape(steps, a.size // (steps * a.shape[-1]), a.shape[-1]) for a in to_cast]
    cast_specs = [pl.BlockSpec((1,) + a.shape[1:], lambda b, i: (b * n_q + i, 0, 0)) for a in sliced]
    outs = pl.pallas_call(
        functools.partial(_attn_kernel, seq=seq, n_cast=len(sliced)),
        grid=(bsz, n_q),
        in_specs=[qspec(B_WIDTH), qspec(GATE_PAD),
                  pl.BlockSpec((3, 2 * GATE_PAD, B_WIDTH), lambda b, i: (0, 0, 0)),
                  full(n_cmp), full(n_cmp), full(seq, KV_W + LANES),
                  full(seq), full(seq), full(seq), full(seq),
                  pl.BlockSpec((1, B_WIDTH), lambda b, i: (0, 0))] + cast_specs,
        out_specs=[qspec(B_WIDTH)] + cast_specs,
        out_shape=[jax.ShapeDtypeStruct((bsz, seq, B_WIDTH), _BF16)]
        + [jax.ShapeDtypeStruct(a.shape, _BF16) for a in sliced],
        scratch_shapes=[pltpu.VMEM((B_KV, B_HPG * Q_BLOCK, LANES), _F32),
                        pltpu.VMEM((B_KV, B_HPG * Q_BLOCK, LANES), _F32)],
        compiler_params=pltpu.CompilerParams(dimension_semantics=("arbitrary", "arbitrary"),
                                             vmem_limit_bytes=VMEM_LIMIT_ATTN),
        name="attn",
    )(q, gates, _gate_expand(), kc, vc, ks, vs0, vs1, kw, vw, gob, *sliced)
    return outs[0], [o.reshape(a.shape) for o, a in zip(outs[1:], to_cast)]


def _pack_halves(x):
    w = x.shape[1] // 2
    bits = lambda v: lax.bitcast_convert_type(v.astype(_BF16).astype(_F32), jnp.uint32)
    return lax.bitcast_convert_type(bits(x[:, :w]) | (bits(x[:, w:]) >> 16), jnp.int32)


def _unpack_halves(p):
    u = lax.bitcast_convert_type(p, jnp.uint32)
    return (lax.bitcast_convert_type(u & jnp.uint32(0xFFFF0000), _F32),
            lax.bitcast_convert_type(u << 16, _F32))


def _post_kernel(x_ref, oa_ref, ob_ref, woa_ref, wob_ref, gmoe_ref, r_ref, h1_ref, hn_ref, rt_ref, cnt_ref):
    h1 = x_ref[...] + _dot(oa_ref[...], woa_ref[...]) + _dot(ob_ref[...], wob_ref[...])
    h1_ref[...] = h1
    hn = _rms(h1, gmoe_ref[...])
    hn_ref[...] = _pack_halves(hn)

    hn_hi = hn.astype(_BF16)
    hn_lo = (hn - hn_hi.astype(_F32)).astype(_BF16)
    hi_both = _dot(hn_hi, r_ref[...])
    logits = hi_both[:, :GATE_PAD] + (_dot(hn_lo, r_ref[:, :GATE_PAD]) + hi_both[:, GATE_PAD:])
    lt = logits.T[:ROUTER_ROWS]
    row = lax.broadcasted_iota(jnp.int32, lt.shape, 0)
    first_idx = lambda hit: jnp.min(jnp.where(hit, row, LANES), axis=0, keepdims=True)

    is_g = row < N_GROUPS
    lg = jnp.where(is_g, lt, NEG)
    mg = jnp.max(lg, axis=0, keepdims=True)
    sg = jnp.sum(jnp.where(is_g, jnp.exp(lg - mg), 0.0), axis=0, keepdims=True)
    pg_top = 1.0 / sg
    g_sel = first_idx(is_g & (lg == mg))

    e_lo = ROUTER_OFF + g_sel * EXPERTS_PER_GROUP
    is_e = (row >= e_lo) & (row < e_lo + EXPERTS_PER_GROUP)
    le = jnp.where(is_e, lt, NEG)
    m1 = jnp.max(le, axis=0, keepdims=True)
    se = jnp.sum(jnp.where(is_e, jnp.exp(le - m1), 0.0), axis=0, keepdims=True)
    i1 = first_idx(is_e & (le == m1))
    le2 = jnp.where(row == i1, NEG, le)
    m2 = jnp.max(le2, axis=0, keepdims=True)
    i2 = first_idx(is_e & (row != i1) & (le2 == m2))
    pe1 = 1.0 / se
    pe2 = jnp.exp(m2 - m1) / se
    denom = pe1 + pe2
    rrow = lax.broadcasted_iota(jnp.int32, (RT_ROWS, lt.shape[1]), 0)
    rt_t = (jnp.where(rrow == i1 - ROUTER_OFF, 1.0, 0.0)
            + jnp.where(rrow == i2 - ROUTER_OFF + N_EXPERTS, 1.0, 0.0)
            + jnp.where(rrow == RT_W, pg_top * pe1 / denom, 0.0)
            + jnp.where(rrow == RT_W + 1, pg_top * pe2 / denom, 0.0))
    rt = jnp.concatenate([rt_t, jnp.zeros((LANES - RT_ROWS, lt.shape[1]), _F32)], axis=0).T
    rt_ref[...] = rt

    @pl.when(pl.program_id(0) == 0)
    def _():
        cnt_ref[...] = jnp.zeros_like(cnt_ref)

    cnt_ref[...] += jnp.sum(rt, axis=0, keepdims=True)


def _post_call(x2, oa, ob, woa, wob, gmoe, r_cat):
    n = x2.shape[0]
    tm = TM_POST
    row = lambda i: (i, 0)
    const2 = lambda i: (0, 0)
    return pl.pallas_call(
        _post_kernel,
        grid=(n // tm,),
        in_specs=[pl.BlockSpec((tm, D_MODEL), row), pl.BlockSpec((tm, A_WIDTH), row),
                  pl.BlockSpec((tm, B_WIDTH), row), pl.BlockSpec((A_WIDTH, D_MODEL), const2),
                  pl.BlockSpec((B_WIDTH, D_MODEL), const2), pl.BlockSpec((1, D_MODEL), const2),
                  pl.BlockSpec((D_MODEL, 2 * GATE_PAD), const2)],
        out_specs=[pl.BlockSpec((tm, D_MODEL), row), pl.BlockSpec((tm, D_MODEL // 2), row),
                   pl.BlockSpec((tm, GATE_PAD), row), pl.BlockSpec((SUBLANES, LANES), const2)],
        out_shape=[jax.ShapeDtypeStruct((n, D_MODEL), _F32), jax.ShapeDtypeStruct((n, D_MODEL // 2), jnp.int32),
                   jax.ShapeDtypeStruct((n, GATE_PAD), _F32), jax.ShapeDtypeStruct((SUBLANES, LANES), _F32)],
        compiler_params=pltpu.CompilerParams(dimension_semantics=("arbitrary",), vmem_limit_bytes=VMEM_LIMIT),
        name="post",
    )(x2, oa, ob, woa, wob, gmoe, r_cat)


def _route_kernel(rt_ref, cnt_ref, earlier_ref, dest_ref, meta_ref, off_ref, run_ref):
    tm = rt_ref.shape[0]
    lane = lax.broadcasted_iota(jnp.int32, (1, LANES), 1)
    first = lane < N_EXPERTS
    onehot = jnp.where(lane < 2 * N_EXPERTS, rt_ref[...], 0.0)

    @pl.when(pl.program_id(0) == 0)
    def _():
        cnt = jnp.where(lane < 2 * N_EXPERTS, cnt_ref[...], 0.0)
        c1 = jnp.where(first, cnt, 0.0)
        tot = c1 + jnp.where(first, pltpu.roll(cnt, LANES - N_EXPERTS, 1), 0.0)
        tiles = jnp.floor((tot + (TR_GMM - 1)) * (1.0 / TR_GMM))
        e_row = lax.broadcasted_iota(jnp.int32, (LANES, LANES), 0)
        e_col = lax.broadcasted_iota(jnp.int32, (LANES, LANES), 1)
        before = jnp.where(e_row < e_col, 1.0, 0.0).astype(_BF16)
        base = _dot(tiles.astype(_BF16), before) * TR_GMM
        off_ref[...] = jnp.where(first, base, 0.0) + pltpu.roll(jnp.where(first, base + c1, 0.0), N_EXPERTS, 1)
        run_ref[...] = jnp.zeros_like(run_ref)
        meta_ref[...] = tiles

    rank = _dot(earlier_ref[...], onehot.astype(_BF16)) + run_ref[0:1, :]
    slot = onehot * (rank + off_ref[0:1, :])
    slot_t = slot.T
    d1 = jnp.sum(slot_t[:N_EXPERTS], axis=0, keepdims=True)
    d2 = jnp.sum(slot_t[N_EXPERTS:2 * N_EXPERTS], axis=0, keepdims=True)
    dest_ref[...] = jnp.concatenate([d1, d2, jnp.zeros((SUBLANES - 2, tm), _F32)], axis=0).astype(jnp.int32)
    run_ref[...] += jnp.sum(onehot, axis=0, keepdims=True)


def _route_call(rt, cnt):
    n = rt.shape[0]
    tm = TM_ROUTE
    return pl.pallas_call(
        _route_kernel,
        grid=(n // tm,),
        in_specs=[pl.BlockSpec((tm, GATE_PAD), lambda i: (i, 0)), pl.BlockSpec((SUBLANES, LANES), lambda i: (0, 0)),
                  pl.BlockSpec((tm, tm), lambda i: (0, 0))],
        out_specs=[pl.BlockSpec((SUBLANES, tm), lambda i: (0, i)), pl.BlockSpec((SUBLANES, LANES), lambda i: (0, 0))],
        out_shape=[jax.ShapeDtypeStruct((SUBLANES, n), jnp.int32), jax.ShapeDtypeStruct((SUBLANES, LANES), _F32)],
        scratch_shapes=[pltpu.VMEM((SUBLANES, LANES), _F32)] * 2,
        compiler_params=pltpu.CompilerParams(dimension_semantics=("arbitrary",)),
        name="route",
    )(rt, cnt, jnp.asarray(np.tril(np.ones((tm, tm), np.float32), -1), _BF16))


def _sc_mesh():
    return plsc.VectorSubcoreMesh(core_axis_name="c", subcore_axis_name="s")


def _sc_worker(n_rows):
    per = n_rows // (SC_CORES * SC_SUBCORES)
    return (lax.axis_index("s") * SC_CORES + lax.axis_index("c")) * per, per


def _dispatch_call(xp, d1, d2, n_slots):
    n, w = xp.shape
    assert n % (SC_CORES * SC_SUBCORES * SC_CHUNK) == 0

    @functools.partial(
        pl.kernel, mesh=_sc_mesh(), out_type=jax.ShapeDtypeStruct((n_slots, w), xp.dtype),
        scratch_types=[pltpu.VMEM((SC_CHUNK,), jnp.int32), pltpu.VMEM((SC_CHUNK,), jnp.int32),
                       pltpu.VMEM((SC_CHUNK, w), xp.dtype), pltpu.SemaphoreType.DMA],
        name="dispatch")
    def k(x_hbm, d1_hbm, d2_hbm, xs_hbm, i1_v, i2_v, rows_v, sem):
        row0, per = _sc_worker(n)

        @pl.loop(0, per // SC_CHUNK)
        def _(j):
            src = pl.ds(row0 + j * SC_CHUNK, SC_CHUNK)
            pltpu.sync_copy(d1_hbm.at[src], i1_v)
            pltpu.sync_copy(d2_hbm.at[src], i2_v)
            pltpu.sync_copy(x_hbm.at[src], rows_v)
            first = pltpu.async_copy(rows_v, xs_hbm.at[i1_v], sem)
            second = pltpu.async_copy(rows_v, xs_hbm.at[i2_v], sem)
            first.wait()
            second.wait()

    return k(xp, d1, d2)


def _combine_call(ys, d1, d2):
    n = d1.shape[0]
    w = ys.shape[1]
    assert n % (SC_CORES * SC_SUBCORES * SC_CHUNK) == 0
    out = jax.ShapeDtypeStruct((n, w), ys.dtype)

    @functools.partial(
        pl.kernel, mesh=_sc_mesh(), out_type=(out, out),
        scratch_types=[pltpu.VMEM((SC_CHUNK,), jnp.int32), pltpu.VMEM((SC_CHUNK, w), ys.dtype),
                       pltpu.SemaphoreType.DMA],
        name="combine")
    def k(ys_hbm, d1_hbm, d2_hbm, y1_hbm, y2_hbm, i_v, rows_v, sem):
        row0, per = _sc_worker(n)

        @pl.loop(0, per // SC_CHUNK)
        def _(j):
            dst = pl.ds(row0 + j * SC_CHUNK, SC_CHUNK)
            for d_hbm, y_hbm in ((d1_hbm, y1_hbm), (d2_hbm, y2_hbm)):
                pltpu.sync_copy(d_hbm.at[dst], i_v)
                pltpu.async_copy(ys_hbm.at[i_v], rows_v, sem).wait()
                pltpu.sync_copy(rows_v, y_hbm.at[dst])

    return k(ys, d1, d2)


def _gmm_kernel(te_ref, nu_ref, xs_ref, wg0_ref, wu0_ref, wd0_ref, wg1_ref, wu1_ref, wd1_ref, ys_ref):
    half = D_MODEL // 2
    t0 = 2 * pl.program_id(0)

    def ffn(rows, wg_ref, wu_ref, wd_ref):
        a, b = _unpack_halves(xs_ref[rows, :])
        a, b = a.astype(_BF16), b.astype(_BF16)
        gate = _dot(a, wg_ref[0, :half, :]) + _dot(b, wg_ref[0, half:, :])
        up = _dot(a, wu_ref[0, :half, :]) + _dot(b, wu_ref[0, half:, :])
        hid = gate * jax.nn.sigmoid(gate) * up
        ys_ref[rows, :] = _pack_halves(_dot(hid.astype(_BF16), wd_ref[0]))

    both = t0 + 1 < nu_ref[0]
    same = te_ref[t0] == te_ref[t0 + 1]

    @pl.when(both & same)
    def _():
        ffn(slice(0, 2 * TR_GMM), wg0_ref, wu0_ref, wd0_ref)

    @pl.when((t0 < nu_ref[0]) & jnp.logical_not(both & same))
    def _():
        ffn(slice(0, TR_GMM), wg0_ref, wu0_ref, wd0_ref)

    @pl.when(both & jnp.logical_not(same))
    def _():
        ffn(slice(TR_GMM, 2 * TR_GMM), wg1_ref, wu1_ref, wd1_ref)


def _gmm_call(tile_expert, n_used, xs, w_gate, w_up, w_down):
    n_slots, w = xs.shape
    assert (n_slots // TR_GMM) % 2 == 0
    last = lambda nu: jnp.maximum((nu[0] - 1) // 2, 0)
    rows = lambda u, te, nu: (jnp.minimum(u, last(nu)), 0)
    expert0 = lambda u, te, nu: (te[2 * u], 0, 0)
    expert1 = lambda u, te, nu: (te[2 * u + 1], 0, 0)
    w_in = lambda idx: pl.BlockSpec((1, D_MODEL, D_FF_EXPERT), idx)
    w_out = lambda idx: pl.BlockSpec((1, D_FF_EXPERT, D_MODEL), idx)
    return pl.pallas_call(
        _gmm_kernel,
        grid_spec=pltpu.PrefetchScalarGridSpec(
            num_scalar_prefetch=2, grid=(n_slots // (2 * TR_GMM),),
            in_specs=[pl.BlockSpec((2 * TR_GMM, w), rows),
                      w_in(expert0), w_in(expert0), w_out(expert0), w_in(expert1), w_in(expert1), w_out(expert1)],
            out_specs=pl.BlockSpec((2 * TR_GMM, w), rows)),
        out_shape=jax.ShapeDtypeStruct((n_slots, w), xs.dtype),
        compiler_params=pltpu.CompilerParams(dimension_semantics=("arbitrary",), vmem_limit_bytes=VMEM_LIMIT),
        name="gmm",
    )(tile_expert, n_used, xs, w_gate, w_up, w_down, w_gate, w_up, w_down)


def _final_kernel(h1_ref, y1_ref, y2_ref, rt_ref, p_ref, gple_ref, wpg_ref, wpp_ref, gfin_ref, o_ref):
    for r in range(h1_ref.shape[0] // SUB_MOE):
        rows = slice(r * SUB_MOE, (r + 1) * SUB_MOE)
        rt = rt_ref[rows, :]
        w1, w2 = rt[:, RT_W:RT_W + 1], rt[:, RT_W + 1:RT_W + 2]
        a1, b1 = _unpack_halves(y1_ref[rows, :])
        a2, b2 = _unpack_halves(y2_ref[rows, :])
        h2 = h1_ref[rows, :] + jnp.concatenate([w1 * a1 + w2 * a2, w1 * b1 + w2 * b2], axis=1)
        gate = jax.nn.sigmoid(_dot(_rms(h2, gple_ref[...]).astype(_BF16), wpg_ref[...]))
        h3 = h2 + _dot(p_ref[rows, :].astype(_BF16), wpp_ref[...]) * gate
        o_ref[rows, :] = _rms(h3, gfin_ref[...])


def _final_call(h1, y1, y2, rt, p2, gple, wpg, wpp, gfin):
    n = h1.shape[0]
    tm = TM_POST
    row = lambda i: (i, 0)
    const2 = lambda i: (0, 0)
    return pl.pallas_call(
        _final_kernel,
        grid=(n // tm,),
        in_specs=[pl.BlockSpec((tm, D_MODEL), row), pl.BlockSpec((tm, D_MODEL // 2), row),
                  pl.BlockSpec((tm, D_MODEL // 2), row), pl.BlockSpec((tm, GATE_PAD), row),
                  pl.BlockSpec((tm, D_PLE), row), pl.BlockSpec((1, D_MODEL), const2),
                  pl.BlockSpec((D_MODEL, D_MODEL), const2), pl.BlockSpec((D_PLE, D_MODEL), const2),
                  pl.BlockSpec((1, D_MODEL), const2)],
        out_specs=pl.BlockSpec((tm, D_MODEL), row),
        out_shape=jax.ShapeDtypeStruct((n, D_MODEL), _F32),
        compiler_params=pltpu.CompilerParams(dimension_semantics=("arbitrary",), vmem_limit_bytes=VMEM_LIMIT),
        name="final",
    )(h1, y1, y2, rt, p2, gple, wpg, wpp, gfin)


def _rope_tables(seq):
    half = HEAD_DIM // 2
    inv = 1.0 / (ROPE_THETA ** (jnp.arange(half, dtype=_F32) / half))
    ang = jnp.arange(seq, dtype=_F32)[:, None] * inv[None, :]
    cos, sin = jnp.cos(ang), jnp.sin(ang)
    reps = LANES // HEAD_DIM
    cos_t = jnp.tile(jnp.concatenate([cos, cos], axis=1), (1, reps))
    sin_t = jnp.tile(jnp.concatenate([-sin, sin], axis=1), (1, reps))
    return cos_t, sin_t


def _layer(h, p_i, norm_mix, w_in, gmlp_v_norm, gmlp_w_s, gmlp_b_s,
           cmp_pe_k, cmp_w1_k, cmp_w2_k, cmp_pe_v, cmp_w1_v, cmp_w2_v,
           out_norm_a, out_norm_b, w_o, norm_moe, router_group, router_expert,
           moe_w_gate, moe_w_up, moe_w_down, norm_ple, w_ple_proj, w_ple_gate, norm_final):
    bsz, seq, _ = h.shape
    n = bsz * seq
    x2 = h.reshape(n, D_MODEL)
    row = lambda v: v.reshape(1, -1).astype(_F32)

    w_q = _perm_heads(w_in[:, OFF_Q:OFF_KV], 1)
    w_gate = jnp.pad(w_in[:, OFF_GATE:D_IN], ((0, 0), (0, GATE_PAD - N_GATES)))
    w_all = jnp.concatenate([w_in[:, :OFF_Q], w_q, w_in[:, OFF_KV:OFF_GATE], w_gate], axis=1).astype(_BF16)
    cos_t, sin_t = _rope_tables(seq)
    ws_pairs = gmlp_w_s.reshape(A_HEADS // 2, 2, CHUNK, CHUNK).transpose(0, 2, 1, 3).reshape(
        A_HEADS // 2, CHUNK, 2 * CHUNK)
    bs_exp = jnp.repeat(gmlp_b_s.T, HEAD_DIM, axis=1)

    oa, q, k_cmp, v_cmp, ks, vs0, vs1, k_win, v_win, gates = _proj_call(
        x2, row(norm_mix), w_all, cos_t, sin_t, row(gmlp_v_norm), ws_pairs, bs_exp, row(out_norm_a), seq)

    kc, vc = _compress_call(k_cmp, v_cmp, _compress_weights(cmp_w1_k, cmp_w2_k, cmp_pe_k),
                            _compress_weights(cmp_w1_v, cmp_w2_v, cmp_pe_v), bsz, seq)

    b3 = lambda a: a.reshape(bsz, seq, a.shape[-1])
    ob, (wg_bf, wu_bf, wd_bf) = _attn_call(
        b3(q), b3(gates), kc, vc, b3(ks), b3(vs0), b3(vs1), b3(k_win), b3(v_win),
        row(_perm_heads(out_norm_b, 0)), [moe_w_gate, moe_w_up, moe_w_down], bsz, seq)

    r_cat = jnp.pad(jnp.concatenate([router_group, router_expert], axis=1),
                    ((0, 0), (0, GATE_PAD - N_GROUPS - N_EXPERTS)))
    r_hi = r_cat.astype(_BF16)
    r_cat = jnp.concatenate([r_hi, (r_cat - r_hi.astype(_F32)).astype(_BF16)], axis=1)
    woa, wob = w_o[:A_WIDTH].astype(_BF16), _perm_heads(w_o[A_WIDTH:], 0).astype(_BF16)
    wpg, wpp = w_ple_gate.astype(_BF16), w_ple_proj.astype(_BF16)

    h1, xp, rt, cnt = _post_call(x2, oa, ob.reshape(n, B_WIDTH), woa, wob, row(norm_moe), r_cat)

    dest, meta = _route_call(rt, cnt)
    d1, d2 = dest[0], dest[1]
    n_tiles = 2 * n // TR_GMM + N_EXPERTS
    ends = jnp.cumsum(meta[0, :N_EXPERTS].astype(jnp.int32))
    tile_expert = jnp.minimum(jnp.sum(ends[None, :] <= jnp.arange(n_tiles)[:, None], axis=1),
                              N_EXPERTS - 1).astype(jnp.int32)
    xs = _dispatch_call(xp, d1, d2, n_tiles * TR_GMM)
    ys = _gmm_call(tile_expert, ends[-1:], xs, wg_bf, wu_bf, wd_bf)
    y1, y2 = _combine_call(ys, d1, d2)
    out = _final_call(h1, y1, y2, rt, p_i.reshape(n, D_PLE), row(norm_ple), wpg, wpp, row(norm_final))
    return out.reshape(bsz, seq, D_MODEL)


def kernel(x, p, norm_mix, w_in, gmlp_v_norm, gmlp_w_s, gmlp_b_s, cmp_pe_k, cmp_w1_k, cmp_w2_k,
           cmp_pe_v, cmp_w1_v, cmp_w2_v, out_norm_a, out_norm_b, w_o, norm_moe, router_group,
           router_expert, moe_w_gate, moe_w_up, moe_w_down, norm_ple, w_ple_proj, w_ple_gate, norm_final):
    assert p.shape[0] == 1, "single-layer trunk"
    assert x.shape[1] % SEL_CK == 0 and x.shape[1] >= WIN + Q_BLOCK
    assert (x.shape[0] * x.shape[1]) % (SC_CORES * SC_SUBCORES * SC_CHUNK) == 0
    return _layer(x, p[0], norm_mix[0], w_in[0], gmlp_v_norm[0], gmlp_w_s[0], gmlp_b_s[0],
                  cmp_pe_k[0], cmp_w1_k[0], cmp_w2_k[0], cmp_pe_v[0], cmp_w1_v[0], cmp_w2_v[0],
                  out_norm_a[0], out_norm_b[0], w_o[0], norm_moe[0], router_group[0], router_expert[0],
                  moe_w_gate[0], moe_w_up[0], moe_w_down[0], norm_ple[0], w_ple_proj[0], w_ple_gate[0],
                  norm_final)
```

```python
import functools

import numpy as np
import jax
import jax.numpy as jnp
from jax import lax
from jax.experimental import pallas as pl
from jax.experimental.pallas import tpu as pltpu
from jax.experimental.pallas import tpu_sc as plsc

D_MODEL = 1024
HEAD_DIM = 64
A_HEADS = 8
A_WIDTH = A_HEADS * HEAD_DIM
B_HEADS = 8
B_WIDTH = B_HEADS * HEAD_DIM
B_KV = 2
B_HPG = B_HEADS // B_KV
KV_W = B_KV * HEAD_DIM
N_GATES = B_HEADS * 3
CHUNK = 128
L_CMP = 32
STRIDE_CMP = 16
CMP_HIDDEN = 256
L_SEL = 64
N_SEL = 16
WIN = 512
WIN_Q = 128
Q_BLOCK = 512
ROPE_THETA = 10000.0
N_GROUPS = 4
EXPERTS_PER_GROUP = 4
N_EXPERTS = N_GROUPS * EXPERTS_PER_GROUP
D_FF_EXPERT = 512
D_PLE = 256
EPS = 1e-6
LOG2E = 1.4426950408889634
NEG = -1e30
FORCE = 1e6

OFF_Q = 2 * A_WIDTH
OFF_KV = OFF_Q + B_WIDTH
OFF_GATE = OFF_KV + 6 * KV_W
D_IN = OFF_GATE + N_GATES

LANES = 128
SUBLANES = 8
GATE_PAD = LANES
ROUTER_OFF = N_GROUPS
W_ALL = OFF_GATE + GATE_PAD

TM_PROJ = 1024
TM_POST = 1024
TM_ROUTE = 512
TR_GMM = 512
SUB_MOE = 256
RT_W = 2 * N_EXPERTS
RT_ROWS = 40
ROUTER_ROWS = 24
SC_CORES = 2
SC_SUBCORES = 16
SC_CHUNK = 128
SEL_CK = 512
VMEM_LIMIT = 56 * 1024 * 1024
VMEM_LIMIT_ATTN = 60 * 1024 * 1024

_PERM_HEADS = [0, 4, 1, 5, 2, 6, 3, 7]


def _perm_heads(a, axis):
    return jnp.concatenate([lax.slice_in_dim(a, h * HEAD_DIM, (h + 1) * HEAD_DIM, axis=axis) for h in _PERM_HEADS],
                           axis=axis)

_F32 = jnp.float32
_BF16 = jnp.bfloat16


def _dot(a, b):
    return jnp.dot(a, b, preferred_element_type=_F32)


def _dot_nt(a, b):
    return lax.dot_general(a, b, (((1,), (1,)), ((), ())), preferred_element_type=_F32)


def _rms(x, g):
    return x * lax.rsqrt(jnp.mean(x * x, axis=-1, keepdims=True) + EPS) * g


def _gelu(x):
    return 0.5 * x * (1.0 + jnp.tanh(0.7978845608028654 * (x + 0.044715 * (x * x * x))))


def _rope_tile(x, cos, sin_signed):
    lane = lax.broadcasted_iota(jnp.int32, x.shape, 1)
    first_half = (lane % HEAD_DIM) < (HEAD_DIM // 2)
    rot = jnp.where(first_half, pltpu.roll(x, LANES - HEAD_DIM // 2, 1), pltpu.roll(x, HEAD_DIM // 2, 1))
    return x * cos + rot * sin_signed


def _proj_kernel(x_ref, gmix_ref, w_ref, cos_ref, sin_ref, gv_ref, ws_ref, bs_ref, goa_ref,
                 oa_ref, q_ref, kc_ref, vc_ref, ks_ref, vs0_ref, vs1_ref, kw_ref, vw_ref, gate_ref,
                 *, seq):
    tm = x_ref.shape[0]
    hn = _rms(x_ref[...], gmix_ref[...]).astype(_BF16)
    cos = cos_ref[...]
    sin = sin_ref[...]

    zu = _gelu(_dot(hn, w_ref[:, 0:A_WIDTH]))
    zv = _gelu(_dot(hn, w_ref[:, A_WIDTH:2 * A_WIDTH]))
    vn = _rms(zv, gv_ref[...]).astype(_BF16)

    t_io = lax.broadcasted_iota(jnp.int32, (CHUNK, 2 * CHUNK), 0)
    s_io = lax.broadcasted_iota(jnp.int32, (CHUNK, 2 * CHUNK), 1) % CHUNK
    causal = s_io <= t_io
    lane = lax.broadcasted_iota(jnp.int32, (CHUNK, LANES), 1)
    lo = lane < HEAD_DIM
    bs = bs_ref[...]
    n_chunks = tm // CHUNK
    pair_cols = []
    for pr in range(A_HEADS // 2):
        wcat = jnp.where(causal, ws_ref[pr], 0.0).astype(_BF16)
        rhs = []
        for c in range(n_chunks):
            vblk = vn[c * CHUNK:(c + 1) * CHUNK, pr * LANES:(pr + 1) * LANES]
            zero = jnp.zeros_like(vblk)
            rhs.append(jnp.concatenate([jnp.where(lo, vblk, zero), jnp.where(lo, zero, vblk)], axis=0))
        out = _dot(wcat, jnp.concatenate(rhs, axis=1))
        pair_cols.append(jnp.concatenate([out[:, c * LANES:(c + 1) * LANES] for c in range(n_chunks)], axis=0))
    mixed = jnp.concatenate(pair_cols, axis=1) + jnp.concatenate([bs] * n_chunks, axis=0)
    oa = zu * mixed
    oa_ref[...] = _rms(oa, goa_ref[...]).astype(oa_ref.dtype)

    zq = _dot(hn, w_ref[:, OFF_Q:OFF_KV])
    scale = HEAD_DIM ** -0.5 * LOG2E
    for j in range(B_WIDTH // LANES):
        blk = _rope_tile(zq[:, j * LANES:(j + 1) * LANES], cos, sin) * scale
        q_ref[:, j * LANES:(j + 1) * LANES] = blk.astype(q_ref.dtype)

    zkv = _dot(hn, w_ref[:, OFF_KV:OFF_GATE])
    kv = []
    for j in range(6):
        blk = zkv[:, j * KV_W:(j + 1) * KV_W]
        kv.append(_rope_tile(blk, cos, sin) if j % 2 == 0 else blk)
    k_cmp, v_cmp, k_slc, v_slc, k_win, v_win = kv
    kc_ref[...] = k_cmp.astype(kc_ref.dtype)
    vc_ref[...] = v_cmp.astype(vc_ref.dtype)
    kw_ref[...] = k_win.astype(kw_ref.dtype)
    vw_ref[...] = v_win.astype(vw_ref.dtype)

    row = lax.broadcasted_iota(jnp.int32, (tm, LANES), 0)
    lane = lax.broadcasted_iota(jnp.int32, (tm, LANES), 1)
    key_block = ((pl.program_id(0) % (seq // tm)) * tm + row) // L_SEL
    ks_ref[:, :LANES] = k_slc.astype(ks_ref.dtype)
    ks_ref[:, LANES:] = jnp.where(lane == key_block, 1.0, 0.0).astype(ks_ref.dtype)
    for g, vs_ref in enumerate((vs0_ref, vs1_ref)):
        vs_ref[...] = jnp.where((lane // HEAD_DIM) == g, v_slc, 1.0).astype(vs_ref.dtype)

    zg = _dot(hn, w_ref[:, OFF_GATE:W_ALL])
    gate_ref[...] = jax.nn.sigmoid(zg)


def _proj_call(x2, gmix, w_all, cos_t, sin_t, gv, ws_pairs, bs_exp, goa, seq):
    n = x2.shape[0]
    tm = TM_PROJ
    n_t = seq // tm
    row = lambda i: (i, 0)
    const2 = lambda i: (0, 0)
    pos = lambda i: (i % n_t, 0)
    out_shapes = [jax.ShapeDtypeStruct((n, A_WIDTH), _BF16), jax.ShapeDtypeStruct((n, B_WIDTH), _BF16)]
    kv_widths = [KV_W, KV_W, KV_W + LANES, KV_W, KV_W, KV_W, KV_W]
    out_shapes += [jax.ShapeDtypeStruct((n, w), _F32 if j < 2 else _BF16) for j, w in enumerate(kv_widths)]
    out_shapes += [jax.ShapeDtypeStruct((n, GATE_PAD), _F32)]
    out_specs = [pl.BlockSpec((tm, A_WIDTH), row), pl.BlockSpec((tm, B_WIDTH), row)]
    out_specs += [pl.BlockSpec((tm, w), row) for w in kv_widths]
    out_specs += [pl.BlockSpec((tm, GATE_PAD), row)]
    return pl.pallas_call(
        functools.partial(_proj_kernel, seq=seq),
        grid=(n // tm,),
        in_specs=[
            pl.BlockSpec((tm, D_MODEL), row),
            pl.BlockSpec((1, D_MODEL), const2),
            pl.BlockSpec((D_MODEL, W_ALL), const2),
            pl.BlockSpec((tm, LANES), pos),
            pl.BlockSpec((tm, LANES), pos),
            pl.BlockSpec((1, A_WIDTH), const2),
            pl.BlockSpec((A_HEADS // 2, CHUNK, 2 * CHUNK), lambda i: (0, 0, 0)),
            pl.BlockSpec((CHUNK, A_WIDTH), const2),
            pl.BlockSpec((1, A_WIDTH), const2),
        ],
        out_specs=out_specs,
        out_shape=out_shapes,
        compiler_params=pltpu.CompilerParams(dimension_semantics=("arbitrary",), vmem_limit_bytes=VMEM_LIMIT),
        name="proj",
    )(x2, gmix, w_all, cos_t, sin_t, gv, ws_pairs, bs_exp, goa)


def _compress_kernel(rk_ref, rv_ref, pek_ref, pev_ref, w1k_ref, w1v_ref,
                     tk_ref, bk_ref, w2k_ref, tv_ref, bv_ref, w2v_ref, kc_ref, vc_ref):
    def one(r_ref, pe_ref, w1_ref, top_ref, bot_ref, w2_ref, o_ref):
        nr = r_ref.shape[1] // STRIDE_CMP
        r = jnp.concatenate([r_ref[0, pl.ds(l, nr, stride=STRIDE_CMP), :] for l in range(STRIDE_CMP)],
                            axis=1).astype(_BF16)
        a = _dot(r, top_ref[...])
        b = _dot(r, bot_ref[...])
        pe_h = _dot(pe_ref[...], w1_ref[...])
        pe2 = jnp.concatenate([pe_h[0:1], pe_h[0:1]], axis=1)
        hid = a + pltpu.roll(b, nr - 1, 0) + pe2
        o_ref[0] = _dot(_gelu(hid).astype(_BF16), w2_ref[...]).astype(o_ref.dtype)

    one(rk_ref, pek_ref, w1k_ref, tk_ref, bk_ref, w2k_ref, kc_ref)
    one(rv_ref, pev_ref, w1v_ref, tv_ref, bv_ref, w2v_ref, vc_ref)


def _compress_weights(w1, w2, pe):
    half = L_CMP // 2
    w1r = w1.reshape(L_CMP, HEAD_DIM, CMP_HIDDEN)
    eye = jnp.eye(B_KV, dtype=w1.dtype)
    place = lambda part: jnp.einsum('ldj,gh->lgdhj', part, eye).reshape(half * KV_W, B_KV * CMP_HIDDEN)
    top = place(w1r[:half]).astype(_BF16)
    bot = place(w1r[half:]).astype(_BF16)
    w2bd = jnp.einsum('jd,gh->gjhd', w2, eye).reshape(B_KV * CMP_HIDDEN, KV_W).astype(_BF16)
    pe8 = jnp.broadcast_to(pe.reshape(1, L_CMP * HEAD_DIM), (SUBLANES, L_CMP * HEAD_DIM)).astype(_BF16)
    return pe8, w1.astype(_BF16), top, bot, w2bd


def _compress_call(k_cmp, v_cmp, wk, wv, bsz, seq):
    nr = seq // STRIDE_CMP
    rk = k_cmp.reshape(bsz, seq, KV_W)
    rv = v_cmp.reshape(bsz, seq, KV_W)
    pek, w1k, tk, bk, w2k = wk
    pev, w1v, tv, bv, w2v = wv
    rspec = pl.BlockSpec((1, seq, KV_W), lambda b: (b, 0, 0))
    full = lambda a: pl.BlockSpec(a.shape, lambda b: (0,) * a.ndim)
    ospec = pl.BlockSpec((1, nr, KV_W), lambda b: (b, 0, 0))
    return pl.pallas_call(
        _compress_kernel,
        grid=(bsz,),
        in_specs=[rspec, rspec, full(pek), full(pev), full(w1k), full(w1v),
                  full(tk), full(bk), full(w2k), full(tv), full(bv), full(w2v)],
        out_specs=[ospec, ospec],
        out_shape=[jax.ShapeDtypeStruct((bsz, nr, KV_W), _BF16)] * 2,
        compiler_params=pltpu.CompilerParams(dimension_semantics=("arbitrary",), vmem_limit_bytes=VMEM_LIMIT),
        name="compress",
    )(rk, rv, pek, pev, w1k, w1v, tk, bk, w2k, tv, bv, w2v)


def _topk_rows_mask(sc_t, k):
    n_rows = sc_t.shape[0]
    row = lax.broadcasted_iota(jnp.int32, sc_t.shape, 0).astype(sc_t.dtype)
    taken = jnp.asarray(-3e38, sc_t.dtype)
    for _ in range(k):
        m = jnp.max(sc_t, axis=0, keepdims=True)
        idx = jnp.min(jnp.where(sc_t == m, row, jnp.asarray(n_rows, sc_t.dtype)), axis=0, keepdims=True)
        sc_t = jnp.where(row == idx, taken, sc_t)
    return jnp.where(sc_t == taken, 1.0, 0.0).astype(_F32)


def _attn_kernel(q_ref, gate_ref, gexp_ref, kc_ref, vc_ref, ks_ref, vs0_ref, vs1_ref,
                 kw_ref, vw_ref, gob_ref, *rest, seq, n_cast):
    cast_in, o_ref, cast_out = rest[:n_cast], rest[n_cast], rest[n_cast + 1:2 * n_cast + 1]
    m_scr, acc_scr = rest[2 * n_cast + 1:]
    for src, dst in zip(cast_in, cast_out):
        dst[...] = src[...].astype(dst.dtype)

    qb = pl.program_id(1)
    t0 = qb * Q_BLOCK
    n_cmp = kc_ref.shape[1]
    n_sb = seq // L_SEL
    k_top = min(N_SEL, n_sb)
    rows = B_HPG * Q_BLOCK
    vs_refs = (vs0_ref, vs1_ref)

    lane_q = lax.broadcasted_iota(jnp.int32, (Q_BLOCK, LANES), 1)
    lo = lane_q < HEAD_DIM
    t_col = t0 + lax.broadcasted_iota(jnp.int32, (Q_BLOCK, 1), 0)

    def per_head(x):
        return x.reshape(B_HPG, Q_BLOCK, x.shape[-1])

    def add_bias(s, bias):
        return (per_head(s) + bias[None]).reshape(rows, s.shape[-1])

    qs = []
    for g in range(B_KV):
        own = jnp.where((lane_q // HEAD_DIM) == g, 1.0, 0.0).astype(q_ref.dtype)
        qs.append(jnp.concatenate([q_ref[0, :, j * LANES:(j + 1) * LANES] * own for j in range(B_HPG)], axis=0))

    thr0 = (t0 - (L_CMP - 1)) // STRIDE_CMP
    n_thr = (Q_BLOCK - 1) // STRIDE_CMP + 2
    assert n_thr <= LANES
    thr_rel = (t_col - (L_CMP - 1)) // STRIDE_CMP - thr0
    q_thr = jnp.where(lane_q == thr_rel, 1.0, 0.0).astype(_BF16)
    c_row1 = lax.broadcasted_iota(jnp.int32, (n_cmp, LANES), 0)
    c_lane1 = lax.broadcasted_iota(jnp.int32, (n_cmp, LANES), 1)
    k_thr = jnp.where((c_lane1 < n_thr) & (c_row1 > thr0 + c_lane1), NEG, 0.0).astype(_BF16)
    kc_wide = jnp.concatenate([kc_ref[0], k_thr], axis=1)
    has_c = (t_col >= L_CMP - 1).astype(_F32)
    c_row = lax.broadcasted_iota(jnp.int32, (n_cmp, n_sb), 0) * STRIDE_CMP
    s_col = lax.broadcasted_iota(jnp.int32, (n_cmp, n_sb), 1) * L_SEL
    overlap = jnp.where((c_row < s_col + L_SEL) & (c_row + L_CMP > s_col), 1.0, 0.0).astype(_BF16)
    blk = lax.broadcasted_iota(jnp.int32, (Q_BLOCK, n_sb), 1)
    cur = t_col // L_SEL
    forced = (blk == 0) | (blk == cur) | (blk == cur - 1)
    valid = blk * L_SEL <= t_col

    o_c, sel_bias = [], []
    for g in range(B_KV):
        s_c = _dot_nt(jnp.concatenate([qs[g], jnp.concatenate([q_thr] * B_HPG, axis=0)], axis=1), kc_wide)
        e_c = jnp.exp2(s_c - jnp.max(s_c, axis=-1, keepdims=True))
        inv = per_head(1.0 / jnp.maximum(jnp.sum(e_c, axis=-1, keepdims=True), 1e-30)) * has_c[None]
        e_bf = e_c.astype(_BF16)
        o_c.append((per_head(_dot(e_bf, vc_ref[0])) * inv).reshape(rows, LANES))
        p_bf = per_head(e_bf) * inv.astype(_BF16)
        imp = _dot(functools.reduce(lambda a, b: a + b, [p_bf[j] for j in range(B_HPG)]), overlap)
        score = jnp.where(valid & jnp.logical_not(forced), imp, -FORCE)
        chosen = forced | (_topk_rows_mask(score.T, k_top - 3).T > 0.5)
        sb = jnp.where(chosen & valid, 0.0, NEG)
        if n_sb < LANES:
            sb = jnp.concatenate([sb, jnp.full((Q_BLOCK, LANES - n_sb), NEG, _F32)], axis=1)
        sel_bias.append(sb)

    w_len = WIN + WIN_Q
    lane_w = lax.broadcasted_iota(jnp.int32, (w_len, LANES), 1)
    keep_w = [jnp.where((lane_w // HEAD_DIM) == g, 1.0, 0.0).astype(vw_ref.dtype) for g in range(B_KV)]
    fill_w = [1 - k for k in keep_w]
    acc_w_sub = [[] for _ in range(B_KV)]
    for hh in range(Q_BLOCK // WIN_Q):
        w_start = pl.multiple_of(jnp.maximum(t0 + hh * WIN_Q - WIN, 0), WIN_Q)
        t_sub = t_col[hh * WIN_Q:(hh + 1) * WIN_Q]
        diff_w = t_sub - (w_start + lax.broadcasted_iota(jnp.int32, (WIN_Q, w_len), 1))
        bias_w = jnp.where((diff_w >= 0) & (diff_w < WIN), 0.0, NEG)
        kw = kw_ref[0, pl.ds(w_start, w_len), :]
        vw = vw_ref[0, pl.ds(w_start, w_len), :]
        for g in range(B_KV):
            q_sub = per_head(qs[g])[:, hh * WIN_Q:(hh + 1) * WIN_Q, :].reshape(B_HPG * WIN_Q, LANES)
            s_w = (_dot_nt(q_sub, kw).reshape(B_HPG, WIN_Q, w_len) + bias_w[None]).reshape(B_HPG * WIN_Q, w_len)
            e_w = jnp.exp2(s_w - jnp.max(s_w, axis=-1, keepdims=True)).astype(_BF16)
            vw_aug = vw * keep_w[g] + fill_w[g]
            acc_w_sub[g].append(_dot(e_w, vw_aug).reshape(B_HPG, WIN_Q, LANES))
    acc_w = [jnp.concatenate(acc_w_sub[g], axis=1).reshape(rows, LANES) for g in range(B_KV)]

    n_ck = (t0 + Q_BLOCK + SEL_CK - 1) // SEL_CK
    key_lane = lax.broadcasted_iota(jnp.int32, (Q_BLOCK, SEL_CK), 1)
    bias_diag = jnp.where((n_ck - 1) * SEL_CK + key_lane <= t_col, 0.0, NEG)
    q_wide = [jnp.concatenate([qs[g], jnp.concatenate([sel_bias[g].astype(_BF16)] * B_HPG, axis=0)], axis=1)
              for g in range(B_KV)]

    m_scr[...] = jnp.full(m_scr.shape, NEG, _F32)
    acc_scr[...] = jnp.zeros(acc_scr.shape, _F32)

    def sel_chunk(ci, diag):
        k0 = pl.multiple_of(ci * SEL_CK, SEL_CK)
        for g in range(B_KV):
            m = m_scr[g]
            s = _dot_nt(q_wide[g], ks_ref[0, pl.ds(k0, SEL_CK), :])
            if diag:
                s = add_bias(s, bias_diag)
            m_new = jnp.maximum(m, jnp.max(s, axis=-1, keepdims=True))
            p = jnp.exp2(s - jnp.concatenate([m_new] * (SEL_CK // LANES), axis=1)).astype(_BF16)
            acc_scr[g] = jnp.exp2(m - m_new) * acc_scr[g] + _dot(p, vs_refs[g][0, pl.ds(k0, SEL_CK), :])
            m_scr[g] = m_new

    n_before = n_ck - 1

    @pl.loop(0, n_before // 2)
    def _(pi):
        sel_chunk(2 * pi, False)
        sel_chunk(2 * pi + 1, False)

    @pl.when(n_before % 2 == 1)
    def _():
        sel_chunk(n_before - 1, False)

    sel_chunk(n_ck - 1, True)
    acc_s = [acc_scr[g] for g in range(B_KV)]

    def numer(acc):
        return jnp.concatenate([jnp.where(lo, acc[0][j * Q_BLOCK:(j + 1) * Q_BLOCK],
                                          acc[1][j * Q_BLOCK:(j + 1) * Q_BLOCK]) for j in range(B_HPG)], axis=1)

    def denom(acc):
        return jnp.concatenate([pltpu.roll(jnp.where(lo, acc[1][j * Q_BLOCK:(j + 1) * Q_BLOCK],
                                                     acc[0][j * Q_BLOCK:(j + 1) * Q_BLOCK]), HEAD_DIM, 1)
                                for j in range(B_HPG)], axis=1)

    gates = gate_ref[0]
    g_hi = gates.astype(_BF16)
    g_split = jnp.concatenate([g_hi, (gates - g_hi.astype(_F32)).astype(_BF16)], axis=1)
    gate_of = lambda r: _dot(g_split, gexp_ref[r])
    ob = (gate_of(0) * numer(o_c)
          + gate_of(1) * numer(acc_s) * (1.0 / jnp.maximum(denom(acc_s), 1e-30))
          + gate_of(2) * numer(acc_w) * (1.0 / jnp.maximum(denom(acc_w), 1e-30)))
    o_ref[0] = _rms(ob, gob_ref[...]).astype(o_ref.dtype)


def _gate_expand():
    x = np.zeros((3, GATE_PAD, B_WIDTH), np.float32)
    for slot, h in enumerate(_PERM_HEADS):
        for r in range(3):
            x[r, 3 * h + r, slot * HEAD_DIM:(slot + 1) * HEAD_DIM] = 1.0
    return jnp.asarray(np.concatenate([x, x], axis=1), _BF16)


def _attn_call(q, gates, kc, vc, ks, vs0, vs1, kw, vw, gob, to_cast, bsz, seq):
    assert seq // L_SEL <= LANES
    n_cmp = kc.shape[1]
    n_q = seq // Q_BLOCK
    steps = bsz * n_q
    qspec = lambda w: pl.BlockSpec((1, Q_BLOCK, w), lambda b, i: (b, i, 0))
    full = lambda r, w=KV_W: pl.BlockSpec((1, r, w), lambda b, i: (b, 0, 0), pipeline_mode=pl.Buffered(1))
    sliced = [a.reshape(steps, a.size // (steps * a.shape[-1]), a.shape[-1]) for a in to_cast]
    cast_specs = [pl.BlockSpec((1,) + a.shape[1:], lambda b, i: (b * n_q + i, 0, 0)) for a in sliced]
    outs = pl.pallas_call(
        functools.partial(_attn_kernel, seq=seq, n_cast=len(sliced)),
        grid=(bsz, n_q),
        in_specs=[qspec(B_WIDTH), qspec(GATE_PAD),
                  pl.BlockSpec((3, 2 * GATE_PAD, B_WIDTH), lambda b, i: (0, 0, 0)),
                  full(n_cmp), full(n_cmp), full(seq, KV_W + LANES),
                  full(seq), full(seq), full(seq), full(seq),
                  pl.BlockSpec((1, B_WIDTH), lambda b, i: (0, 0))] + cast_specs,
        out_specs=[qspec(B_WIDTH)] + cast_specs,
        out_shape=[jax.ShapeDtypeStruct((bsz, seq, B_WIDTH), _BF16)]
        + [jax.ShapeDtypeStruct(a.shape, _BF16) for a in sliced],
        scratch_shapes=[pltpu.VMEM((B_KV, B_HPG * Q_BLOCK, LANES), _F32),
                        pltpu.VMEM((B_KV, B_HPG * Q_BLOCK, LANES), _F32)],
        compiler_params=pltpu.CompilerParams(dimension_semantics=("arbitrary", "arbitrary"),
                                             vmem_limit_bytes=VMEM_LIMIT_ATTN),
        name="attn",
    )(q, gates, _gate_expand(), kc, vc, ks, vs0, vs1, kw, vw, gob, *sliced)
    return outs[0], [o.reshape(a.shape) for o, a in zip(outs[1:], to_cast)]


def _pack_halves(x):
    w = x.shape[1] // 2
    bits = lambda v: lax.bitcast_convert_type(v.astype(_BF16).astype(_F32), jnp.uint32)
    return lax.bitcast_convert_type(bits(x[:, :w]) | (bits(x[:, w:]) >> 16), jnp.int32)


def _unpack_halves(p):
    u = lax.bitcast_convert_type(p, jnp.uint32)
    return (lax.bitcast_convert_type(u & jnp.uint32(0xFFFF0000), _F32),
            lax.bitcast_convert_type(u << 16, _F32))


def _post_kernel(x_ref, oa_ref, ob_ref, woa_ref, wob_ref, gmoe_ref, r_ref, h1_ref, hn_ref, rt_ref, cnt_ref):
    h1 = x_ref[...] + _dot(oa_ref[...], woa_ref[...]) + _dot(ob_ref[...], wob_ref[...])
    h1_ref[...] = h1
    hn = _rms(h1, gmoe_ref[...])
    hn_ref[...] = _pack_halves(hn)

    hn_hi = hn.astype(_BF16)
    hn_lo = (hn - hn_hi.astype(_F32)).astype(_BF16)
    hi_both = _dot(hn_hi, r_ref[...])
    logits = hi_both[:, :GATE_PAD] + (_dot(hn_lo, r_ref[:, :GATE_PAD]) + hi_both[:, GATE_PAD:])
    lt = logits.T[:ROUTER_ROWS]
    row = lax.broadcasted_iota(jnp.int32, lt.shape, 0)
    first_idx = lambda hit: jnp.min(jnp.where(hit, row, LANES), axis=0, keepdims=True)

    is_g = row < N_GROUPS
    lg = jnp.where(is_g, lt, NEG)
    mg = jnp.max(lg, axis=0, keepdims=True)
    sg = jnp.sum(jnp.where(is_g, jnp.exp(lg - mg), 0.0), axis=0, keepdims=True)
    pg_top = 1.0 / sg
    g_sel = first_idx(is_g & (lg == mg))

    e_lo = ROUTER_OFF + g_sel * EXPERTS_PER_GROUP
    is_e = (row >= e_lo) & (row < e_lo + EXPERTS_PER_GROUP)
    le = jnp.where(is_e, lt, NEG)
    m1 = jnp.max(le, axis=0, keepdims=True)
    se = jnp.sum(jnp.where(is_e, jnp.exp(le - m1), 0.0), axis=0, keepdims=True)
    i1 = first_idx(is_e & (le == m1))
    le2 = jnp.where(row == i1, NEG, le)
    m2 = jnp.max(le2, axis=0, keepdims=True)
    i2 = first_idx(is_e & (row != i1) & (le2 == m2))
    pe1 = 1.0 / se
    pe2 = jnp.exp(m2 - m1) / se
    denom = pe1 + pe2
    rrow = lax.broadcasted_iota(jnp.int32, (RT_ROWS, lt.shape[1]), 0)
    rt_t = (jnp.where(rrow == i1 - ROUTER_OFF, 1.0, 0.0)
            + jnp.where(rrow == i2 - ROUTER_OFF + N_EXPERTS, 1.0, 0.0)
            + jnp.where(rrow == RT_W, pg_top * pe1 / denom, 0.0)
            + jnp.where(rrow == RT_W + 1, pg_top * pe2 / denom, 0.0))
    rt = jnp.concatenate([rt_t, jnp.zeros((LANES - RT_ROWS, lt.shape[1]), _F32)], axis=0).T
    rt_ref[...] = rt

    @pl.when(pl.program_id(0) == 0)
    def _():
        cnt_ref[...] = jnp.zeros_like(cnt_ref)

    cnt_ref[...] += jnp.sum(rt, axis=0, keepdims=True)


def _post_call(x2, oa, ob, woa, wob, gmoe, r_cat):
    n = x2.shape[0]
    tm = TM_POST
    row = lambda i: (i, 0)
    const2 = lambda i: (0, 0)
    return pl.pallas_call(
        _post_kernel,
        grid=(n // tm,),
        in_specs=[pl.BlockSpec((tm, D_MODEL), row), pl.BlockSpec((tm, A_WIDTH), row),
                  pl.BlockSpec((tm, B_WIDTH), row), pl.BlockSpec((A_WIDTH, D_MODEL), const2),
                  pl.BlockSpec((B_WIDTH, D_MODEL), const2), pl.BlockSpec((1, D_MODEL), const2),
                  pl.BlockSpec((D_MODEL, 2 * GATE_PAD), const2)],
        out_specs=[pl.BlockSpec((tm, D_MODEL), row), pl.BlockSpec((tm, D_MODEL // 2), row),
                   pl.BlockSpec((tm, GATE_PAD), row), pl.BlockSpec((SUBLANES, LANES), const2)],
        out_shape=[jax.ShapeDtypeStruct((n, D_MODEL), _F32), jax.ShapeDtypeStruct((n, D_MODEL // 2), jnp.int32),
                   jax.ShapeDtypeStruct((n, GATE_PAD), _F32), jax.ShapeDtypeStruct((SUBLANES, LANES), _F32)],
        compiler_params=pltpu.CompilerParams(dimension_semantics=("arbitrary",), vmem_limit_bytes=VMEM_LIMIT),
        name="post",
    )(x2, oa, ob, woa, wob, gmoe, r_cat)


def _route_kernel(rt_ref, cnt_ref, earlier_ref, dest_ref, meta_ref, off_ref, run_ref):
    tm = rt_ref.shape[0]
    lane = lax.broadcasted_iota(jnp.int32, (1, LANES), 1)
    first = lane < N_EXPERTS
    onehot = jnp.where(lane < 2 * N_EXPERTS, rt_ref[...], 0.0)

    @pl.when(pl.program_id(0) == 0)
    def _():
        cnt = jnp.where(lane < 2 * N_EXPERTS, cnt_ref[...], 0.0)
        c1 = jnp.where(first, cnt, 0.0)
        tot = c1 + jnp.where(first, pltpu.roll(cnt, LANES - N_EXPERTS, 1), 0.0)
        tiles = jnp.floor((tot + (TR_GMM - 1)) * (1.0 / TR_GMM))
        e_row = lax.broadcasted_iota(jnp.int32, (LANES, LANES), 0)
        e_col = lax.broadcasted_iota(jnp.int32, (LANES, LANES), 1)
        before = jnp.where(e_row < e_col, 1.0, 0.0).astype(_BF16)
        base = _dot(tiles.astype(_BF16), before) * TR_GMM
        off_ref[...] = jnp.where(first, base, 0.0) + pltpu.roll(jnp.where(first, base + c1, 0.0), N_EXPERTS, 1)
        run_ref[...] = jnp.zeros_like(run_ref)
        meta_ref[...] = tiles

    rank = _dot(earlier_ref[...], onehot.astype(_BF16)) + run_ref[0:1, :]
    slot = onehot * (rank + off_ref[0:1, :])
    slot_t = slot.T
    d1 = jnp.sum(slot_t[:N_EXPERTS], axis=0, keepdims=True)
    d2 = jnp.sum(slot_t[N_EXPERTS:2 * N_EXPERTS], axis=0, keepdims=True)
    dest_ref[...] = jnp.concatenate([d1, d2, jnp.zeros((SUBLANES - 2, tm), _F32)], axis=0).astype(jnp.int32)
    run_ref[...] += jnp.sum(onehot, axis=0, keepdims=True)


def _route_call(rt, cnt):
    n = rt.shape[0]
    tm = TM_ROUTE
    return pl.pallas_call(
        _route_kernel,
        grid=(n // tm,),
        in_specs=[pl.BlockSpec((tm, GATE_PAD), lambda i: (i, 0)), pl.BlockSpec((SUBLANES, LANES), lambda i: (0, 0)),
                  pl.BlockSpec((tm, tm), lambda i: (0, 0))],
        out_specs=[pl.BlockSpec((SUBLANES, tm), lambda i: (0, i)), pl.BlockSpec((SUBLANES, LANES), lambda i: (0, 0))],
        out_shape=[jax.ShapeDtypeStruct((SUBLANES, n), jnp.int32), jax.ShapeDtypeStruct((SUBLANES, LANES), _F32)],
        scratch_shapes=[pltpu.VMEM((SUBLANES, LANES), _F32)] * 2,
        compiler_params=pltpu.CompilerParams(dimension_semantics=("arbitrary",)),
        name="route",
    )(rt, cnt, jnp.asarray(np.tril(np.ones((tm, tm), np.float32), -1), _BF16))


def _sc_mesh():
    return plsc.VectorSubcoreMesh(core_axis_name="c", subcore_axis_name="s")


def _sc_worker(n_rows):
    per = n_rows // (SC_CORES * SC_SUBCORES)
    return (lax.axis_index("s") * SC_CORES + lax.axis_index("c")) * per, per


def _dispatch_call(xp, d1, d2, n_slots):
    n, w = xp.shape
    assert n % (SC_CORES * SC_SUBCORES * SC_CHUNK) == 0

    @functools.partial(
        pl.kernel, mesh=_sc_mesh(), out_type=jax.ShapeDtypeStruct((n_slots, w), xp.dtype),
        scratch_types=[pltpu.VMEM((SC_CHUNK,), jnp.int32), pltpu.VMEM((SC_CHUNK,), jnp.int32),
                       pltpu.VMEM((SC_CHUNK, w), xp.dtype), pltpu.SemaphoreType.DMA],
        name="dispatch")
    def k(x_hbm, d1_hbm, d2_hbm, xs_hbm, i1_v, i2_v, rows_v, sem):
        row0, per = _sc_worker(n)

        @pl.loop(0, per // SC_CHUNK)
        def _(j):
            src = pl.ds(row0 + j * SC_CHUNK, SC_CHUNK)
            pltpu.sync_copy(d1_hbm.at[src], i1_v)
            pltpu.sync_copy(d2_hbm.at[src], i2_v)
            pltpu.sync_copy(x_hbm.at[src], rows_v)
            first = pltpu.async_copy(rows_v, xs_hbm.at[i1_v], sem)
            second = pltpu.async_copy(rows_v, xs_hbm.at[i2_v], sem)
            first.wait()
            second.wait()

    return k(xp, d1, d2)


def _combine_call(ys, d1, d2):
    n = d1.shape[0]
    w = ys.shape[1]
    assert n % (SC_CORES * SC_SUBCORES * SC_CHUNK) == 0
    out = jax.ShapeDtypeStruct((n, w), ys.dtype)

    @functools.partial(
        pl.kernel, mesh=_sc_mesh(), out_type=(out, out),
        scratch_types=[pltpu.VMEM((SC_CHUNK,), jnp.int32), pltpu.VMEM((SC_CHUNK, w), ys.dtype),
                       pltpu.SemaphoreType.DMA],
        name="combine")
    def k(ys_hbm, d1_hbm, d2_hbm, y1_hbm, y2_hbm, i_v, rows_v, sem):
        row0, per = _sc_worker(n)

        @pl.loop(0, per // SC_CHUNK)
        def _(j):
            dst = pl.ds(row0 + j * SC_CHUNK, SC_CHUNK)
            for d_hbm, y_hbm in ((d1_hbm, y1_hbm), (d2_hbm, y2_hbm)):
                pltpu.sync_copy(d_hbm.at[dst], i_v)
                pltpu.async_copy(ys_hbm.at[i_v], rows_v, sem).wait()
                pltpu.sync_copy(rows_v, y_hbm.at[dst])

    return k(ys, d1, d2)


def _gmm_kernel(te_ref, nu_ref, xs_ref, wg0_ref, wu0_ref, wd0_ref, wg1_ref, wu1_ref, wd1_ref, ys_ref):
    half = D_MODEL // 2
    t0 = 2 * pl.program_id(0)

    def ffn(rows, wg_ref, wu_ref, wd_ref):
        a, b = _unpack_halves(xs_ref[rows, :])
        a, b = a.astype(_BF16), b.astype(_BF16)
        gate = _dot(a, wg_ref[0, :half, :]) + _dot(b, wg_ref[0, half:, :])
        up = _dot(a, wu_ref[0, :half, :]) + _dot(b, wu_ref[0, half:, :])
        hid = gate * jax.nn.sigmoid(gate) * up
        ys_ref[rows, :] = _pack_halves(_dot(hid.astype(_BF16), wd_ref[0]))

    both = t0 + 1 < nu_ref[0]
    same = te_ref[t0] == te_ref[t0 + 1]

    @pl.when(both & same)
    def _():
        ffn(slice(0, 2 * TR_GMM), wg0_ref, wu0_ref, wd0_ref)

    @pl.when((t0 < nu_ref[0]) & jnp.logical_not(both & same))
    def _():
        ffn(slice(0, TR_GMM), wg0_ref, wu0_ref, wd0_ref)

    @pl.when(both & jnp.logical_not(same))
    def _():
        ffn(slice(TR_GMM, 2 * TR_GMM), wg1_ref, wu1_ref, wd1_ref)


def _gmm_call(tile_expert, n_used, xs, w_gate, w_up, w_down):
    n_slots, w = xs.shape
    assert (n_slots // TR_GMM) % 2 == 0
    last = lambda nu: jnp.maximum((nu[0] - 1) // 2, 0)
    rows = lambda u, te, nu: (jnp.minimum(u, last(nu)), 0)
    expert0 = lambda u, te, nu: (te[2 * u], 0, 0)
    expert1 = lambda u, te, nu: (te[2 * u + 1], 0, 0)
    w_in = lambda idx: pl.BlockSpec((1, D_MODEL, D_FF_EXPERT), idx)
    w_out = lambda idx: pl.BlockSpec((1, D_FF_EXPERT, D_MODEL), idx)
    return pl.pallas_call(
        _gmm_kernel,
        grid_spec=pltpu.PrefetchScalarGridSpec(
            num_scalar_prefetch=2, grid=(n_slots // (2 * TR_GMM),),
            in_specs=[pl.BlockSpec((2 * TR_GMM, w), rows),
                      w_in(expert0), w_in(expert0), w_out(expert0), w_in(expert1), w_in(expert1), w_out(expert1)],
            out_specs=pl.BlockSpec((2 * TR_GMM, w), rows)),
        out_shape=jax.ShapeDtypeStruct((n_slots, w), xs.dtype),
        compiler_params=pltpu.CompilerParams(dimension_semantics=("arbitrary",), vmem_limit_bytes=VMEM_LIMIT),
        name="gmm",
    )(tile_expert, n_used, xs, w_gate, w_up, w_down, w_gate, w_up, w_down)


def _final_kernel(h1_ref, y1_ref, y2_ref, rt_ref, p_ref, gple_ref, wpg_ref, wpp_ref, gfin_ref, o_ref):
    for r in range(h1_ref.shape[0] // SUB_MOE):
        rows = slice(r * SUB_MOE, (r + 1) * SUB_MOE)
        rt = rt_ref[rows, :]
        w1, w2 = rt[:, RT_W:RT_W + 1], rt[:, RT_W + 1:RT_W + 2]
        a1, b1 = _unpack_halves(y1_ref[rows, :])
        a2, b2 = _unpack_halves(y2_ref[rows, :])
        h2 = h1_ref[rows, :] + jnp.concatenate([w1 * a1 + w2 * a2, w1 * b1 + w2 * b2], axis=1)
        gate = jax.nn.sigmoid(_dot(_rms(h2, gple_ref[...]).astype(_BF16), wpg_ref[...]))
        h3 = h2 + _dot(p_ref[rows, :].astype(_BF16), wpp_ref[...]) * gate
        o_ref[rows, :] = _rms(h3, gfin_ref[...])


def _final_call(h1, y1, y2, rt, p2, gple, wpg, wpp, gfin):
    n = h1.shape[0]
    tm = TM_POST
    row = lambda i: (i, 0)
    const2 = lambda i: (0, 0)
    return pl.pallas_call(
        _final_kernel,
        grid=(n // tm,),
        in_specs=[pl.BlockSpec((tm, D_MODEL), row), pl.BlockSpec((tm, D_MODEL // 2), row),
                  pl.BlockSpec((tm, D_MODEL // 2), row), pl.BlockSpec((tm, GATE_PAD), row),
                  pl.BlockSpec((tm, D_PLE), row), pl.BlockSpec((1, D_MODEL), const2),
                  pl.BlockSpec((D_MODEL, D_MODEL), const2), pl.BlockSpec((D_PLE, D_MODEL), const2),
                  pl.BlockSpec((1, D_MODEL), const2)],
        out_specs=pl.BlockSpec((tm, D_MODEL), row),
        out_shape=jax.ShapeDtypeStruct((n, D_MODEL), _F32),
        compiler_params=pltpu.CompilerParams(dimension_semantics=("arbitrary",), vmem_limit_bytes=VMEM_LIMIT),
        name="final",
    )(h1, y1, y2, rt, p2, gple, wpg, wpp, gfin)


def _rope_tables(seq):
    half = HEAD_DIM // 2
    inv = 1.0 / (ROPE_THETA ** (jnp.arange(half, dtype=_F32) / half))
    ang = jnp.arange(seq, dtype=_F32)[:, None] * inv[None, :]
    cos, sin = jnp.cos(ang), jnp.sin(ang)
    reps = LANES // HEAD_DIM
    cos_t = jnp.tile(jnp.concatenate([cos, cos], axis=1), (1, reps))
    sin_t = jnp.tile(jnp.concatenate([-sin, sin], axis=1), (1, reps))
    return cos_t, sin_t


def _layer(h, p_i, norm_mix, w_in, gmlp_v_norm, gmlp_w_s, gmlp_b_s,
           cmp_pe_k, cmp_w1_k, cmp_w2_k, cmp_pe_v, cmp_w1_v, cmp_w2_v,
           out_norm_a, out_norm_b, w_o, norm_moe, router_group, router_expert,
           moe_w_gate, moe_w_up, moe_w_down, norm_ple, w_ple_proj, w_ple_gate, norm_final):
    bsz, seq, _ = h.shape
    n = bsz * seq
    x2 = h.reshape(n, D_MODEL)
    row = lambda v: v.reshape(1, -1).astype(_F32)

    w_q = _perm_heads(w_in[:, OFF_Q:OFF_KV], 1)
    w_gate = jnp.pad(w_in[:, OFF_GATE:D_IN], ((0, 0), (0, GATE_PAD - N_GATES)))
    w_all = jnp.concatenate([w_in[:, :OFF_Q], w_q, w_in[:, OFF_KV:OFF_GATE], w_gate], axis=1).astype(_BF16)
    cos_t, sin_t = _rope_tables(seq)
    ws_pairs = gmlp_w_s.reshape(A_HEADS // 2, 2, CHUNK, CHUNK).transpose(0, 2, 1, 3).reshape(
        A_HEADS // 2, CHUNK, 2 * CHUNK)
    bs_exp = jnp.repeat(gmlp_b_s.T, HEAD_DIM, axis=1)

    oa, q, k_cmp, v_cmp, ks, vs0, vs1, k_win, v_win, gates = _proj_call(
        x2, row(norm_mix), w_all, cos_t, sin_t, row(gmlp_v_norm), ws_pairs, bs_exp, row(out_norm_a), seq)

    kc, vc = _compress_call(k_cmp, v_cmp, _compress_weights(cmp_w1_k, cmp_w2_k, cmp_pe_k),
                            _compress_weights(cmp_w1_v, cmp_w2_v, cmp_pe_v), bsz, seq)

    b3 = lambda a: a.reshape(bsz, seq, a.shape[-1])
    ob, (wg_bf, wu_bf, wd_bf) = _attn_call(
        b3(q), b3(gates), kc, vc, b3(ks), b3(vs0), b3(vs1), b3(k_win), b3(v_win),
        row(_perm_heads(out_norm_b, 0)), [moe_w_gate, moe_w_up, moe_w_down], bsz, seq)

    r_cat = jnp.pad(jnp.concatenate([router_group, router_expert], axis=1),
                    ((0, 0), (0, GATE_PAD - N_GROUPS - N_EXPERTS)))
    r_hi = r_cat.astype(_BF16)
    r_cat = jnp.concatenate([r_hi, (r_cat - r_hi.astype(_F32)).astype(_BF16)], axis=1)
    woa, wob = w_o[:A_WIDTH].astype(_BF16), _perm_heads(w_o[A_WIDTH:], 0).astype(_BF16)
    wpg, wpp = w_ple_gate.astype(_BF16), w_ple_proj.astype(_BF16)

    h1, xp, rt, cnt = _post_call(x2, oa, ob.reshape(n, B_WIDTH), woa, wob, row(norm_moe), r_cat)

    dest, meta = _route_call(rt, cnt)
    d1, d2 = dest[0], dest[1]
    n_tiles = 2 * n // TR_GMM + N_EXPERTS
    ends = jnp.cumsum(meta[0, :N_EXPERTS].astype(jnp.int32))
    tile_expert = jnp.minimum(jnp.sum(ends[None, :] <= jnp.arange(n_tiles)[:, None], axis=1),
                              N_EXPERTS - 1).astype(jnp.int32)
    xs = _dispatch_call(xp, d1, d2, n_tiles * TR_GMM)
    ys = _gmm_call(tile_expert, ends[-1:], xs, wg_bf, wu_bf, wd_bf)
    y1, y2 = _combine_call(ys, d1, d2)
    out = _final_call(h1, y1, y2, rt, p_i.reshape(n, D_PLE), row(norm_ple), wpg, wpp, row(norm_final))
    return out.reshape(bsz, seq, D_MODEL)


def kernel(x, p, norm_mix, w_in, gmlp_v_norm, gmlp_w_s, gmlp_b_s, cmp_pe_k, cmp_w1_k, cmp_w2_k,
           cmp_pe_v, cmp_w1_v, cmp_w2_v, out_norm_a, out_norm_b, w_o, norm_moe, router_group,
           router_expert, moe_w_gate, moe_w_up, moe_w_down, norm_ple, w_ple_proj, w_ple_gate, norm_final):
    assert p.shape[0] == 1, "single-layer trunk"
    assert x.shape[1] % SEL_CK == 0 and x.shape[1] >= WIN + Q_BLOCK
    assert (x.shape[0] * x.shape[1]) % (SC_CORES * SC_SUBCORES * SC_CHUNK) == 0
    return _layer(x, p[0], norm_mix[0], w_in[0], gmlp_v_norm[0], gmlp_w_s[0], gmlp_b_s[0],
                  cmp_pe_k[0], cmp_w1_k[0], cmp_w2_k[0], cmp_pe_v[0], cmp_w1_v[0], cmp_w2_v[0],
                  out_norm_a[0], out_norm_b[0], w_o[0], norm_moe[0], router_group[0], router_expert[0],
                  moe_w_gate[0], moe_w_up[0], moe_w_down[0], norm_ple[0], w_ple_proj[0], w_ple_gate[0],
                  norm_final)
```

```python
import functools

import numpy as np
import jax
import jax.numpy as jnp
from jax import lax
from jax.experimental import pallas as pl
from jax.experimental.pallas import tpu as pltpu
from jax.experimental.pallas import tpu_sc as plsc

D_MODEL = 1024
HEAD_DIM = 64
A_HEADS = 8
A_WIDTH = A_HEADS * HEAD_DIM
B_HEADS = 8
B_WIDTH = B_HEADS * HEAD_DIM
B_KV = 2
B_HPG = B_HEADS // B_KV
KV_W = B_KV * HEAD_DIM
N_GATES = B_HEADS * 3
CHUNK = 128
L_CMP = 32
STRIDE_CMP = 16
CMP_HIDDEN = 256
L_SEL = 64
N_SEL = 16
WIN = 512
WIN_Q = 128
Q_BLOCK = 512
ROPE_THETA = 10000.0
N_GROUPS = 4
EXPERTS_PER_GROUP = 4
N_EXPERTS = N_GROUPS * EXPERTS_PER_GROUP
D_FF_EXPERT = 512
D_PLE = 256
EPS = 1e-6
LOG2E = 1.4426950408889634
NEG = -1e30
FORCE = 1e6

OFF_Q = 2 * A_WIDTH
OFF_KV = OFF_Q + B_WIDTH
OFF_GATE = OFF_KV + 6 * KV_W
D_IN = OFF_GATE + N_GATES

LANES = 128
SUBLANES = 8
GATE_PAD = LANES
ROUTER_OFF = N_GROUPS
W_ALL = OFF_GATE + GATE_PAD

TM_PROJ = 1024
TM_POST = 1024
TM_ROUTE = 512
TR_GMM = 512
SUB_MOE = 256
RT_W = 2 * N_EXPERTS
RT_ROWS = 40
ROUTER_ROWS = 24
SC_CORES = 2
SC_SUBCORES = 16
SC_CHUNK = 128
SEL_CK = 512
SEL_UNROLL = 4
VMEM_LIMIT = 56 * 1024 * 1024
VMEM_LIMIT_ATTN = 60 * 1024 * 1024

_PERM_HEADS = [0, 4, 1, 5, 2, 6, 3, 7]


def _perm_heads(a, axis):
    return jnp.concatenate([lax.slice_in_dim(a, h * HEAD_DIM, (h + 1) * HEAD_DIM, axis=axis) for h in _PERM_HEADS],
                           axis=axis)

_F32 = jnp.float32
_BF16 = jnp.bfloat16


def _dot(a, b):
    return jnp.dot(a, b, preferred_element_type=_F32)


def _dot_nt(a, b):
    return lax.dot_general(a, b, (((1,), (1,)), ((), ())), preferred_element_type=_F32)


def _rms(x, g):
    return x * lax.rsqrt(jnp.mean(x * x, axis=-1, keepdims=True) + EPS) * g


def _gelu(x):
    return 0.5 * x * (1.0 + jnp.tanh(0.7978845608028654 * (x + 0.044715 * (x * x * x))))


def _rope_tile(x, cos, sin_signed):
    lane = lax.broadcasted_iota(jnp.int32, x.shape, 1)
    first_half = (lane % HEAD_DIM) < (HEAD_DIM // 2)
    rot = jnp.where(first_half, pltpu.roll(x, LANES - HEAD_DIM // 2, 1), pltpu.roll(x, HEAD_DIM // 2, 1))
    return x * cos + rot * sin_signed


def _proj_kernel(x_ref, gmix_ref, w_ref, cos_ref, sin_ref, gv_ref, ws_ref, bs_ref, goa_ref,
                 oa_ref, q_ref, kc_ref, vc_ref, ks_ref, vs0_ref, vs1_ref, kw_ref, vw_ref, gate_ref,
                 *, seq):
    tm = x_ref.shape[0]
    hn = _rms(x_ref[...], gmix_ref[...]).astype(_BF16)
    cos = cos_ref[...]
    sin = sin_ref[...]

    zu = _gelu(_dot(hn, w_ref[:, 0:A_WIDTH]))
    zv = _gelu(_dot(hn, w_ref[:, A_WIDTH:2 * A_WIDTH]))
    vn = _rms(zv, gv_ref[...]).astype(_BF16)

    t_io = lax.broadcasted_iota(jnp.int32, (CHUNK, 2 * CHUNK), 0)
    s_io = lax.broadcasted_iota(jnp.int32, (CHUNK, 2 * CHUNK), 1) % CHUNK
    causal = s_io <= t_io
    lane = lax.broadcasted_iota(jnp.int32, (CHUNK, LANES), 1)
    lo = lane < HEAD_DIM
    bs = bs_ref[...]
    n_chunks = tm // CHUNK
    pair_cols = []
    for pr in range(A_HEADS // 2):
        wcat = jnp.where(causal, ws_ref[pr], 0.0).astype(_BF16)
        rhs = []
        for c in range(n_chunks):
            vblk = vn[c * CHUNK:(c + 1) * CHUNK, pr * LANES:(pr + 1) * LANES]
            zero = jnp.zeros_like(vblk)
            rhs.append(jnp.concatenate([jnp.where(lo, vblk, zero), jnp.where(lo, zero, vblk)], axis=0))
        out = _dot(wcat, jnp.concatenate(rhs, axis=1))
        pair_cols.append(jnp.concatenate([out[:, c * LANES:(c + 1) * LANES] for c in range(n_chunks)], axis=0))
    mixed = jnp.concatenate(pair_cols, axis=1) + jnp.concatenate([bs] * n_chunks, axis=0)
    oa = zu * mixed
    oa_ref[...] = _rms(oa, goa_ref[...]).astype(oa_ref.dtype)

    zq = _dot(hn, w_ref[:, OFF_Q:OFF_KV])
    scale = HEAD_DIM ** -0.5 * LOG2E
    for j in range(B_WIDTH // LANES):
        blk = _rope_tile(zq[:, j * LANES:(j + 1) * LANES], cos, sin) * scale
        q_ref[:, j * LANES:(j + 1) * LANES] = blk.astype(q_ref.dtype)

    zkv = _dot(hn, w_ref[:, OFF_KV:OFF_GATE])
    kv = []
    for j in range(6):
        blk = zkv[:, j * KV_W:(j + 1) * KV_W]
        kv.append(_rope_tile(blk, cos, sin) if j % 2 == 0 else blk)
    k_cmp, v_cmp, k_slc, v_slc, k_win, v_win = kv
    kc_ref[...] = k_cmp.astype(kc_ref.dtype)
    vc_ref[...] = v_cmp.astype(vc_ref.dtype)
    kw_ref[...] = k_win.astype(kw_ref.dtype)
    vw_ref[...] = v_win.astype(vw_ref.dtype)

    row = lax.broadcasted_iota(jnp.int32, (tm, LANES), 0)
    lane = lax.broadcasted_iota(jnp.int32, (tm, LANES), 1)
    key_block = ((pl.program_id(0) % (seq // tm)) * tm + row) // L_SEL
    ks_ref[:, :LANES] = k_slc.astype(ks_ref.dtype)
    ks_ref[:, LANES:] = jnp.where(lane == key_block, 1.0, 0.0).astype(ks_ref.dtype)
    for g, vs_ref in enumerate((vs0_ref, vs1_ref)):
        vs_ref[...] = jnp.where((lane // HEAD_DIM) == g, v_slc, 1.0).astype(vs_ref.dtype)

    zg = _dot(hn, w_ref[:, OFF_GATE:W_ALL])
    gate_ref[...] = jax.nn.sigmoid(zg)


def _proj_call(x2, gmix, w_all, cos_t, sin_t, gv, ws_pairs, bs_exp, goa, seq):
    n = x2.shape[0]
    tm = TM_PROJ
    n_t = seq // tm
    row = lambda i: (i, 0)
    const2 = lambda i: (0, 0)
    pos = lambda i: (i % n_t, 0)
    out_shapes = [jax.ShapeDtypeStruct((n, A_WIDTH), _BF16), jax.ShapeDtypeStruct((n, B_WIDTH), _BF16)]
    kv_widths = [KV_W, KV_W, KV_W + LANES, KV_W, KV_W, KV_W, KV_W]
    out_shapes += [jax.ShapeDtypeStruct((n, w), _F32 if j < 2 else _BF16) for j, w in enumerate(kv_widths)]
    out_shapes += [jax.ShapeDtypeStruct((n, GATE_PAD), _F32)]
    out_specs = [pl.BlockSpec((tm, A_WIDTH), row), pl.BlockSpec((tm, B_WIDTH), row)]
    out_specs += [pl.BlockSpec((tm, w), row) for w in kv_widths]
    out_specs += [pl.BlockSpec((tm, GATE_PAD), row)]
    return pl.pallas_call(
        functools.partial(_proj_kernel, seq=seq),
        grid=(n // tm,),
        in_specs=[
            pl.BlockSpec((tm, D_MODEL), row),
            pl.BlockSpec((1, D_MODEL), const2),
            pl.BlockSpec((D_MODEL, W_ALL), const2),
            pl.BlockSpec((tm, LANES), pos),
            pl.BlockSpec((tm, LANES), pos),
            pl.BlockSpec((1, A_WIDTH), const2),
            pl.BlockSpec((A_HEADS // 2, CHUNK, 2 * CHUNK), lambda i: (0, 0, 0)),
            pl.BlockSpec((CHUNK, A_WIDTH), const2),
            pl.BlockSpec((1, A_WIDTH), const2),
        ],
        out_specs=out_specs,
        out_shape=out_shapes,
        compiler_params=pltpu.CompilerParams(dimension_semantics=("arbitrary",), vmem_limit_bytes=VMEM_LIMIT),
        name="proj",
    )(x2, gmix, w_all, cos_t, sin_t, gv, ws_pairs, bs_exp, goa)


def _compress_kernel(rk_ref, rv_ref, pek_ref, pev_ref, w1k_ref, w1v_ref,
                     tk_ref, bk_ref, w2k_ref, tv_ref, bv_ref, w2v_ref, kc_ref, vc_ref):
    def one(r_ref, pe_ref, w1_ref, top_ref, bot_ref, w2_ref, o_ref):
        nr = r_ref.shape[1] // STRIDE_CMP
        r = jnp.concatenate([r_ref[0, pl.ds(l, nr, stride=STRIDE_CMP), :] for l in range(STRIDE_CMP)],
                            axis=1).astype(_BF16)
        a = _dot(r, top_ref[...])
        b = _dot(r, bot_ref[...])
        pe_h = _dot(pe_ref[...], w1_ref[...])
        pe2 = jnp.concatenate([pe_h[0:1], pe_h[0:1]], axis=1)
        hid = a + pltpu.roll(b, nr - 1, 0) + pe2
        o_ref[0] = _dot(_gelu(hid).astype(_BF16), w2_ref[...]).astype(o_ref.dtype)

    one(rk_ref, pek_ref, w1k_ref, tk_ref, bk_ref, w2k_ref, kc_ref)
    one(rv_ref, pev_ref, w1v_ref, tv_ref, bv_ref, w2v_ref, vc_ref)


def _compress_weights(w1, w2, pe):
    half = L_CMP // 2
    w1r = w1.reshape(L_CMP, HEAD_DIM, CMP_HIDDEN)
    eye = jnp.eye(B_KV, dtype=w1.dtype)
    place = lambda part: jnp.einsum('ldj,gh->lgdhj', part, eye).reshape(half * KV_W, B_KV * CMP_HIDDEN)
    top = place(w1r[:half]).astype(_BF16)
    bot = place(w1r[half:]).astype(_BF16)
    w2bd = jnp.einsum('jd,gh->gjhd', w2, eye).reshape(B_KV * CMP_HIDDEN, KV_W).astype(_BF16)
    pe8 = jnp.broadcast_to(pe.reshape(1, L_CMP * HEAD_DIM), (SUBLANES, L_CMP * HEAD_DIM)).astype(_BF16)
    return pe8, w1.astype(_BF16), top, bot, w2bd


def _compress_call(k_cmp, v_cmp, wk, wv, bsz, seq):
    nr = seq // STRIDE_CMP
    rk = k_cmp.reshape(bsz, seq, KV_W)
    rv = v_cmp.reshape(bsz, seq, KV_W)
    pek, w1k, tk, bk, w2k = wk
    pev, w1v, tv, bv, w2v = wv
    rspec = pl.BlockSpec((1, seq, KV_W), lambda b: (b, 0, 0))
    full = lambda a: pl.BlockSpec(a.shape, lambda b: (0,) * a.ndim)
    ospec = pl.BlockSpec((1, nr, KV_W), lambda b: (b, 0, 0))
    return pl.pallas_call(
        _compress_kernel,
        grid=(bsz,),
        in_specs=[rspec, rspec, full(pek), full(pev), full(w1k), full(w1v),
                  full(tk), full(bk), full(w2k), full(tv), full(bv), full(w2v)],
        out_specs=[ospec, ospec],
        out_shape=[jax.ShapeDtypeStruct((bsz, nr, KV_W), _BF16)] * 2,
        compiler_params=pltpu.CompilerParams(dimension_semantics=("arbitrary",), vmem_limit_bytes=VMEM_LIMIT),
        name="compress",
    )(rk, rv, pek, pev, w1k, w1v, tk, bk, w2k, tv, bv, w2v)


def _topk_rows_mask(sc_t, k):
    n_rows = sc_t.shape[0]
    row = lax.broadcasted_iota(jnp.int32, sc_t.shape, 0).astype(sc_t.dtype)
    taken = jnp.asarray(-3e38, sc_t.dtype)
    for _ in range(k):
        m = jnp.max(sc_t, axis=0, keepdims=True)
        idx = jnp.min(jnp.where(sc_t == m, row, jnp.asarray(n_rows, sc_t.dtype)), axis=0, keepdims=True)
        sc_t = jnp.where(row == idx, taken, sc_t)
    return jnp.where(sc_t == taken, 1.0, 0.0).astype(_F32)


def _attn_kernel(q_ref, gate_ref, gexp_ref, kc_ref, vc_ref, ks_ref, vs0_ref, vs1_ref,
                 kw_ref, vw_ref, gob_ref, *rest, seq, n_cast):
    cast_in, o_ref, cast_out = rest[:n_cast], rest[n_cast], rest[n_cast + 1:2 * n_cast + 1]
    m_scr, acc_scr = rest[2 * n_cast + 1:]
    for src, dst in zip(cast_in, cast_out):
        dst[...] = src[...].astype(dst.dtype)

    qb = pl.program_id(1)
    t0 = qb * Q_BLOCK
    n_cmp = kc_ref.shape[1]
    n_sb = seq // L_SEL
    k_top = min(N_SEL, n_sb)
    rows = B_HPG * Q_BLOCK
    vs_refs = (vs0_ref, vs1_ref)

    lane_q = lax.broadcasted_iota(jnp.int32, (Q_BLOCK, LANES), 1)
    lo = lane_q < HEAD_DIM
    t_col = t0 + lax.broadcasted_iota(jnp.int32, (Q_BLOCK, 1), 0)

    def per_head(x):
        return x.reshape(B_HPG, Q_BLOCK, x.shape[-1])

    def add_bias(s, bias):
        return (per_head(s) + bias[None]).reshape(rows, s.shape[-1])

    qs = []
    for g in range(B_KV):
        own = jnp.where((lane_q // HEAD_DIM) == g, 1.0, 0.0).astype(q_ref.dtype)
        qs.append(jnp.concatenate([q_ref[0, :, j * LANES:(j + 1) * LANES] * own for j in range(B_HPG)], axis=0))

    thr0 = (t0 - (L_CMP - 1)) // STRIDE_CMP
    n_thr = (Q_BLOCK - 1) // STRIDE_CMP + 2
    assert n_thr <= LANES
    thr_rel = (t_col - (L_CMP - 1)) // STRIDE_CMP - thr0
    q_thr = jnp.where(lane_q == thr_rel, 1.0, 0.0).astype(_BF16)
    c_row1 = lax.broadcasted_iota(jnp.int32, (n_cmp, LANES), 0)
    c_lane1 = lax.broadcasted_iota(jnp.int32, (n_cmp, LANES), 1)
    k_thr = jnp.where((c_lane1 < n_thr) & (c_row1 > thr0 + c_lane1), NEG, 0.0).astype(_BF16)
    kc_wide = jnp.concatenate([kc_ref[0], k_thr], axis=1)
    has_c = (t_col >= L_CMP - 1).astype(_F32)
    c_row = lax.broadcasted_iota(jnp.int32, (n_cmp, n_sb), 0) * STRIDE_CMP
    s_col = lax.broadcasted_iota(jnp.int32, (n_cmp, n_sb), 1) * L_SEL
    overlap = jnp.where((c_row < s_col + L_SEL) & (c_row + L_CMP > s_col), 1.0, 0.0).astype(_BF16)
    blk = lax.broadcasted_iota(jnp.int32, (Q_BLOCK, n_sb), 1)
    cur = t_col // L_SEL
    forced = (blk == 0) | (blk == cur) | (blk == cur - 1)
    valid = blk * L_SEL <= t_col

    o_c, sel_bias = [], []
    for g in range(B_KV):
        s_c = _dot_nt(jnp.concatenate([qs[g], jnp.concatenate([q_thr] * B_HPG, axis=0)], axis=1), kc_wide)
        e_c = jnp.exp2(s_c - jnp.max(s_c, axis=-1, keepdims=True))
        inv = per_head(1.0 / jnp.maximum(jnp.sum(e_c, axis=-1, keepdims=True), 1e-30)) * has_c[None]
        e_bf = e_c.astype(_BF16)
        o_c.append((per_head(_dot(e_bf, vc_ref[0])) * inv).reshape(rows, LANES))
        p_bf = per_head(e_bf) * inv.astype(_BF16)
        imp = _dot(functools.reduce(lambda a, b: a + b, [p_bf[j] for j in range(B_HPG)]), overlap)
        score = jnp.where(valid & jnp.logical_not(forced), imp, -FORCE)
        chosen = forced | (_topk_rows_mask(score.T, k_top - 3).T > 0.5)
        sb = jnp.where(chosen & valid, 0.0, NEG)
        if n_sb < LANES:
            sb = jnp.concatenate([sb, jnp.full((Q_BLOCK, LANES - n_sb), NEG, _F32)], axis=1)
        sel_bias.append(sb)

    w_len = WIN + WIN_Q
    lane_w = lax.broadcasted_iota(jnp.int32, (w_len, LANES), 1)
    keep_w = [jnp.where((lane_w // HEAD_DIM) == g, 1.0, 0.0).astype(vw_ref.dtype) for g in range(B_KV)]
    fill_w = [1 - k for k in keep_w]
    acc_w_sub = [[] for _ in range(B_KV)]
    for hh in range(Q_BLOCK // WIN_Q):
        w_start = pl.multiple_of(jnp.maximum(t0 + hh * WIN_Q - WIN, 0), WIN_Q)
        t_sub = t_col[hh * WIN_Q:(hh + 1) * WIN_Q]
        diff_w = t_sub - (w_start + lax.broadcasted_iota(jnp.int32, (WIN_Q, w_len), 1))
        bias_w = jnp.where((diff_w >= 0) & (diff_w < WIN), 0.0, NEG)
        kw = kw_ref[0, pl.ds(w_start, w_len), :]
        vw = vw_ref[0, pl.ds(w_start, w_len), :]
        for g in range(B_KV):
            q_sub = per_head(qs[g])[:, hh * WIN_Q:(hh + 1) * WIN_Q, :].reshape(B_HPG * WIN_Q, LANES)
            s_w = (_dot_nt(q_sub, kw).reshape(B_HPG, WIN_Q, w_len) + bias_w[None]).reshape(B_HPG * WIN_Q, w_len)
            e_w = jnp.exp2(s_w - jnp.max(s_w, axis=-1, keepdims=True)).astype(_BF16)
            vw_aug = vw * keep_w[g] + fill_w[g]
            acc_w_sub[g].append(_dot(e_w, vw_aug).reshape(B_HPG, WIN_Q, LANES))
    acc_w = [jnp.concatenate(acc_w_sub[g], axis=1).reshape(rows, LANES) for g in range(B_KV)]

    n_ck = (t0 + Q_BLOCK + SEL_CK - 1) // SEL_CK
    key_lane = lax.broadcasted_iota(jnp.int32, (Q_BLOCK, SEL_CK), 1)
    bias_diag = jnp.where((n_ck - 1) * SEL_CK + key_lane <= t_col, 0.0, NEG)
    q_wide = [jnp.concatenate([qs[g], jnp.concatenate([sel_bias[g].astype(_BF16)] * B_HPG, axis=0)], axis=1)
              for g in range(B_KV)]

    m_scr[...] = jnp.full(m_scr.shape, NEG, _F32)
    acc_scr[...] = jnp.zeros(acc_scr.shape, _F32)

    def sel_chunk(ci, diag):
        k0 = pl.multiple_of(ci * SEL_CK, SEL_CK)
        for g in range(B_KV):
            m = m_scr[g]
            s = _dot_nt(q_wide[g], ks_ref[0, pl.ds(k0, SEL_CK), :])
            if diag:
                s = add_bias(s, bias_diag)
            m_new = jnp.maximum(m, jnp.max(s, axis=-1, keepdims=True))
            p = jnp.exp2(s - jnp.concatenate([m_new] * (SEL_CK // LANES), axis=1)).astype(_BF16)
            acc_scr[g] = jnp.exp2(m - m_new) * acc_scr[g] + _dot(p, vs_refs[g][0, pl.ds(k0, SEL_CK), :])
            m_scr[g] = m_new

    n_before = n_ck - 1

    @pl.loop(0, n_before // SEL_UNROLL)
    def _(ti):
        for u in range(SEL_UNROLL):
            sel_chunk(SEL_UNROLL * ti + u, False)

    done = (n_before // SEL_UNROLL) * SEL_UNROLL
    group = SEL_UNROLL // 2
    while group >= 1:
        @pl.when((n_before // group) % 2 == 1)
        def _(group=group, start=done):
            for u in range(group):
                sel_chunk(start + u, False)

        done = done + jnp.where((n_before // group) % 2 == 1, group, 0)
        group //= 2

    sel_chunk(n_ck - 1, True)
    acc_s = [acc_scr[g] for g in range(B_KV)]

    def numer(acc):
        return jnp.concatenate([jnp.where(lo, acc[0][j * Q_BLOCK:(j + 1) * Q_BLOCK],
                                          acc[1][j * Q_BLOCK:(j + 1) * Q_BLOCK]) for j in range(B_HPG)], axis=1)

    def denom(acc):
        return jnp.concatenate([pltpu.roll(jnp.where(lo, acc[1][j * Q_BLOCK:(j + 1) * Q_BLOCK],
                                                     acc[0][j * Q_BLOCK:(j + 1) * Q_BLOCK]), HEAD_DIM, 1)
                                for j in range(B_HPG)], axis=1)

    gates = gate_ref[0]
    g_hi = gates.astype(_BF16)
    g_split = jnp.concatenate([g_hi, (gates - g_hi.astype(_F32)).astype(_BF16)], axis=1)
    gate_of = lambda r: _dot(g_split, gexp_ref[r])
    ob = (gate_of(0) * numer(o_c)
          + gate_of(1) * numer(acc_s) * (1.0 / jnp.maximum(denom(acc_s), 1e-30))
          + gate_of(2) * numer(acc_w) * (1.0 / jnp.maximum(denom(acc_w), 1e-30)))
    o_ref[0] = _rms(ob, gob_ref[...]).astype(o_ref.dtype)


def _gate_expand():
    x = np.zeros((3, GATE_PAD, B_WIDTH), np.float32)
    for slot, h in enumerate(_PERM_HEADS):
        for r in range(3):
            x[r, 3 * h + r, slot * HEAD_DIM:(slot + 1) * HEAD_DIM] = 1.0
    return jnp.asarray(np.concatenate([x, x], axis=1), _BF16)


def _attn_call(q, gates, kc, vc, ks, vs0, vs1, kw, vw, gob, to_cast, bsz, seq):
    assert seq // L_SEL <= LANES
    n_cmp = kc.shape[1]
    n_q = seq // Q_BLOCK
    steps = bsz * n_q
    qspec = lambda w: pl.BlockSpec((1, Q_BLOCK, w), lambda b, i: (b, i, 0))
    full = lambda r, w=KV_W: pl.BlockSpec((1, r, w), lambda b, i: (b, 0, 0), pipeline_mode=pl.Buffered(1))
    sliced = [a.reshape(steps, a.size // (steps * a.shape[-1]), a.shape[-1]) for a in to_cast]
    cast_specs = [pl.BlockSpec((1,) + a.shape[1:], lambda b, i: (b * n_q + i, 0, 0)) for a in sliced]
    outs = pl.pallas_call(
        functools.partial(_attn_kernel, seq=seq, n_cast=len(sliced)),
        grid=(bsz, n_q),
        in_specs=[qspec(B_WIDTH), qspec(GATE_PAD),
                  pl.BlockSpec((3, 2 * GATE_PAD, B_WIDTH), lambda b, i: (0, 0, 0)),
                  full(n_cmp), full(n_cmp), full(seq, KV_W + LANES),
                  full(seq), full(seq), full(seq), full(seq),
                  pl.BlockSpec((1, B_WIDTH), lambda b, i: (0, 0))] + cast_specs,
        out_specs=[qspec(B_WIDTH)] + cast_specs,
        out_shape=[jax.ShapeDtypeStruct((bsz, seq, B_WIDTH), _BF16)]
        + [jax.ShapeDtypeStruct(a.shape, _BF16) for a in sliced],
        scratch_shapes=[pltpu.VMEM((B_KV, B_HPG * Q_BLOCK, LANES), _F32),
                        pltpu.VMEM((B_KV, B_HPG * Q_BLOCK, LANES), _F32)],
        compiler_params=pltpu.CompilerParams(dimension_semantics=("arbitrary", "arbitrary"),
                                             vmem_limit_bytes=VMEM_LIMIT_ATTN),
        name="attn",
    )(q, gates, _gate_expand(), kc, vc, ks, vs0, vs1, kw, vw, gob, *sliced)
    return outs[0], [o.reshape(a.shape) for o, a in zip(outs[1:], to_cast)]


def _pack_halves(x):
    w = x.shape[1] // 2
    bits = lambda v: lax.bitcast_convert_type(v.astype(_BF16).astype(_F32), jnp.uint32)
    return lax.bitcast_convert_type(bits(x[:, :w]) | (bits(x[:, w:]) >> 16), jnp.int32)


def _unpack_halves(p):
    u = lax.bitcast_convert_type(p, jnp.uint32)
    return (lax.bitcast_convert_type(u & jnp.uint32(0xFFFF0000), _F32),
            lax.bitcast_convert_type(u << 16, _F32))


def _post_kernel(x_ref, oa_ref, ob_ref, woa_ref, wob_ref, gmoe_ref, r_ref, h1_ref, hn_ref, rt_ref, cnt_ref):
    h1 = x_ref[...] + _dot(oa_ref[...], woa_ref[...]) + _dot(ob_ref[...], wob_ref[...])
    h1_ref[...] = h1
    hn = _rms(h1, gmoe_ref[...])
    hn_ref[...] = _pack_halves(hn)

    hn_hi = hn.astype(_BF16)
    hn_lo = (hn - hn_hi.astype(_F32)).astype(_BF16)
    hi_both = _dot(hn_hi, r_ref[...])
    logits = hi_both[:, :GATE_PAD] + (_dot(hn_lo, r_ref[:, :GATE_PAD]) + hi_both[:, GATE_PAD:])
    lt = logits.T[:ROUTER_ROWS]
    row = lax.broadcasted_iota(jnp.int32, lt.shape, 0)
    first_idx = lambda hit: jnp.min(jnp.where(hit, row, LANES), axis=0, keepdims=True)

    is_g = row < N_GROUPS
    lg = jnp.where(is_g, lt, NEG)
    mg = jnp.max(lg, axis=0, keepdims=True)
    sg = jnp.sum(jnp.where(is_g, jnp.exp(lg - mg), 0.0), axis=0, keepdims=True)
    pg_top = 1.0 / sg
    g_sel = first_idx(is_g & (lg == mg))

    e_lo = ROUTER_OFF + g_sel * EXPERTS_PER_GROUP
    is_e = (row >= e_lo) & (row < e_lo + EXPERTS_PER_GROUP)
    le = jnp.where(is_e, lt, NEG)
    m1 = jnp.max(le, axis=0, keepdims=True)
    se = jnp.sum(jnp.where(is_e, jnp.exp(le - m1), 0.0), axis=0, keepdims=True)
    i1 = first_idx(is_e & (le == m1))
    le2 = jnp.where(row == i1, NEG, le)
    m2 = jnp.max(le2, axis=0, keepdims=True)
    i2 = first_idx(is_e & (row != i1) & (le2 == m2))
    pe1 = 1.0 / se
    pe2 = jnp.exp(m2 - m1) / se
    denom = pe1 + pe2
    rrow = lax.broadcasted_iota(jnp.int32, (RT_ROWS, lt.shape[1]), 0)
    rt_t = (jnp.where(rrow == i1 - ROUTER_OFF, 1.0, 0.0)
            + jnp.where(rrow == i2 - ROUTER_OFF + N_EXPERTS, 1.0, 0.0)
            + jnp.where(rrow == RT_W, pg_top * pe1 / denom, 0.0)
            + jnp.where(rrow == RT_W + 1, pg_top * pe2 / denom, 0.0))
    rt = jnp.concatenate([rt_t, jnp.zeros((LANES - RT_ROWS, lt.shape[1]), _F32)], axis=0).T
    rt_ref[...] = rt

    @pl.when(pl.program_id(0) == 0)
    def _():
        cnt_ref[...] = jnp.zeros_like(cnt_ref)

    cnt_ref[...] += jnp.sum(rt, axis=0, keepdims=True)


def _post_call(x2, oa, ob, woa, wob, gmoe, r_cat):
    n = x2.shape[0]
    tm = TM_POST
    row = lambda i: (i, 0)
    const2 = lambda i: (0, 0)
    return pl.pallas_call(
        _post_kernel,
        grid=(n // tm,),
        in_specs=[pl.BlockSpec((tm, D_MODEL), row), pl.BlockSpec((tm, A_WIDTH), row),
                  pl.BlockSpec((tm, B_WIDTH), row), pl.BlockSpec((A_WIDTH, D_MODEL), const2),
                  pl.BlockSpec((B_WIDTH, D_MODEL), const2), pl.BlockSpec((1, D_MODEL), const2),
                  pl.BlockSpec((D_MODEL, 2 * GATE_PAD), const2)],
        out_specs=[pl.BlockSpec((tm, D_MODEL), row), pl.BlockSpec((tm, D_MODEL // 2), row),
                   pl.BlockSpec((tm, GATE_PAD), row), pl.BlockSpec((SUBLANES, LANES), const2)],
        out_shape=[jax.ShapeDtypeStruct((n, D_MODEL), _F32), jax.ShapeDtypeStruct((n, D_MODEL // 2), jnp.int32),
                   jax.ShapeDtypeStruct((n, GATE_PAD), _F32), jax.ShapeDtypeStruct((SUBLANES, LANES), _F32)],
        compiler_params=pltpu.CompilerParams(dimension_semantics=("arbitrary",), vmem_limit_bytes=VMEM_LIMIT),
        name="post",
    )(x2, oa, ob, woa, wob, gmoe, r_cat)


def _route_kernel(rt_ref, cnt_ref, earlier_ref, dest_ref, meta_ref, off_ref, run_ref):
    tm = rt_ref.shape[0]
    lane = lax.broadcasted_iota(jnp.int32, (1, LANES), 1)
    first = lane < N_EXPERTS
    onehot = jnp.where(lane < 2 * N_EXPERTS, rt_ref[...], 0.0)

    @pl.when(pl.program_id(0) == 0)
    def _():
        cnt = jnp.where(lane < 2 * N_EXPERTS, cnt_ref[...], 0.0)
        c1 = jnp.where(first, cnt, 0.0)
        tot = c1 + jnp.where(first, pltpu.roll(cnt, LANES - N_EXPERTS, 1), 0.0)
        tiles = jnp.floor((tot + (TR_GMM - 1)) * (1.0 / TR_GMM))
        e_row = lax.broadcasted_iota(jnp.int32, (LANES, LANES), 0)
        e_col = lax.broadcasted_iota(jnp.int32, (LANES, LANES), 1)
        before = jnp.where(e_row < e_col, 1.0, 0.0).astype(_BF16)
        base = _dot(tiles.astype(_BF16), before) * TR_GMM
        off_ref[...] = jnp.where(first, base, 0.0) + pltpu.roll(jnp.where(first, base + c1, 0.0), N_EXPERTS, 1)
        run_ref[...] = jnp.zeros_like(run_ref)
        meta_ref[...] = tiles

    rank = _dot(earlier_ref[...], onehot.astype(_BF16)) + run_ref[0:1, :]
    slot = onehot * (rank + off_ref[0:1, :])
    slot_t = slot.T
    d1 = jnp.sum(slot_t[:N_EXPERTS], axis=0, keepdims=True)
    d2 = jnp.sum(slot_t[N_EXPERTS:2 * N_EXPERTS], axis=0, keepdims=True)
    dest_ref[...] = jnp.concatenate([d1, d2, jnp.zeros((SUBLANES - 2, tm), _F32)], axis=0).astype(jnp.int32)
    run_ref[...] += jnp.sum(onehot, axis=0, keepdims=True)


def _route_call(rt, cnt):
    n = rt.shape[0]
    tm = TM_ROUTE
    return pl.pallas_call(
        _route_kernel,
        grid=(n // tm,),
        in_specs=[pl.BlockSpec((tm, GATE_PAD), lambda i: (i, 0)), pl.BlockSpec((SUBLANES, LANES), lambda i: (0, 0)),
                  pl.BlockSpec((tm, tm), lambda i: (0, 0))],
        out_specs=[pl.BlockSpec((SUBLANES, tm), lambda i: (0, i)), pl.BlockSpec((SUBLANES, LANES), lambda i: (0, 0))],
        out_shape=[jax.ShapeDtypeStruct((SUBLANES, n), jnp.int32), jax.ShapeDtypeStruct((SUBLANES, LANES), _F32)],
        scratch_shapes=[pltpu.VMEM((SUBLANES, LANES), _F32)] * 2,
        compiler_params=pltpu.CompilerParams(dimension_semantics=("arbitrary",)),
        name="route",
    )(rt, cnt, jnp.asarray(np.tril(np.ones((tm, tm), np.float32), -1), _BF16))


def _sc_mesh():
    return plsc.VectorSubcoreMesh(core_axis_name="c", subcore_axis_name="s")


def _sc_worker(n_rows):
    per = n_rows // (SC_CORES * SC_SUBCORES)
    return (lax.axis_index("s") * SC_CORES + lax.axis_index("c")) * per, per


def _dispatch_call(xp, d1, d2, n_slots):
    n, w = xp.shape
    assert n % (SC_CORES * SC_SUBCORES * SC_CHUNK) == 0

    @functools.partial(
        pl.kernel, mesh=_sc_mesh(), out_type=jax.ShapeDtypeStruct((n_slots, w), xp.dtype),
        scratch_types=[pltpu.VMEM((SC_CHUNK,), jnp.int32), pltpu.VMEM((SC_CHUNK,), jnp.int32),
                       pltpu.VMEM((SC_CHUNK, w), xp.dtype), pltpu.SemaphoreType.DMA],
        name="dispatch")
    def k(x_hbm, d1_hbm, d2_hbm, xs_hbm, i1_v, i2_v, rows_v, sem):
        row0, per = _sc_worker(n)

        @pl.loop(0, per // SC_CHUNK)
        def _(j):
            src = pl.ds(row0 + j * SC_CHUNK, SC_CHUNK)
            pltpu.sync_copy(d1_hbm.at[src], i1_v)
            pltpu.sync_copy(d2_hbm.at[src], i2_v)
            pltpu.sync_copy(x_hbm.at[src], rows_v)
            first = pltpu.async_copy(rows_v, xs_hbm.at[i1_v], sem)
            second = pltpu.async_copy(rows_v, xs_hbm.at[i2_v], sem)
            first.wait()
            second.wait()

    return k(xp, d1, d2)


def _combine_call(ys, d1, d2):
    n = d1.shape[0]
    w = ys.shape[1]
    assert n % (SC_CORES * SC_SUBCORES * SC_CHUNK) == 0
    out = jax.ShapeDtypeStruct((n, w), ys.dtype)

    @functools.partial(
        pl.kernel, mesh=_sc_mesh(), out_type=(out, out),
        scratch_types=[pltpu.VMEM((SC_CHUNK,), jnp.int32), pltpu.VMEM((SC_CHUNK, w), ys.dtype),
                       pltpu.SemaphoreType.DMA],
        name="combine")
    def k(ys_hbm, d1_hbm, d2_hbm, y1_hbm, y2_hbm, i_v, rows_v, sem):
        row0, per = _sc_worker(n)

        @pl.loop(0, per // SC_CHUNK)
        def _(j):
            dst = pl.ds(row0 + j * SC_CHUNK, SC_CHUNK)
            for d_hbm, y_hbm in ((d1_hbm, y1_hbm), (d2_hbm, y2_hbm)):
                pltpu.sync_copy(d_hbm.at[dst], i_v)
                pltpu.async_copy(ys_hbm.at[i_v], rows_v, sem).wait()
                pltpu.sync_copy(rows_v, y_hbm.at[dst])

    return k(ys, d1, d2)


def _gmm_kernel(te_ref, nu_ref, xs_ref, wg0_ref, wu0_ref, wd0_ref, wg1_ref, wu1_ref, wd1_ref, ys_ref):
    half = D_MODEL // 2
    t0 = 2 * pl.program_id(0)

    def ffn(rows, wg_ref, wu_ref, wd_ref):
        a, b = _unpack_halves(xs_ref[rows, :])
        a, b = a.astype(_BF16), b.astype(_BF16)
        gate = _dot(a, wg_ref[0, :half, :]) + _dot(b, wg_ref[0, half:, :])
        up = _dot(a, wu_ref[0, :half, :]) + _dot(b, wu_ref[0, half:, :])
        hid = gate * jax.nn.sigmoid(gate) * up
        ys_ref[rows, :] = _pack_halves(_dot(hid.astype(_BF16), wd_ref[0]))

    both = t0 + 1 < nu_ref[0]
    same = te_ref[t0] == te_ref[t0 + 1]

    @pl.when(both & same)
    def _():
        ffn(slice(0, 2 * TR_GMM), wg0_ref, wu0_ref, wd0_ref)

    @pl.when((t0 < nu_ref[0]) & jnp.logical_not(both & same))
    def _():
        ffn(slice(0, TR_GMM), wg0_ref, wu0_ref, wd0_ref)

    @pl.when(both & jnp.logical_not(same))
    def _():
        ffn(slice(TR_GMM, 2 * TR_GMM), wg1_ref, wu1_ref, wd1_ref)


def _gmm_call(tile_expert, n_used, xs, w_gate, w_up, w_down):
    n_slots, w = xs.shape
    assert (n_slots // TR_GMM) % 2 == 0
    last = lambda nu: jnp.maximum((nu[0] - 1) // 2, 0)
    rows = lambda u, te, nu: (jnp.minimum(u, last(nu)), 0)
    expert0 = lambda u, te, nu: (te[2 * u], 0, 0)
    expert1 = lambda u, te, nu: (te[2 * u + 1], 0, 0)
    w_in = lambda idx: pl.BlockSpec((1, D_MODEL, D_FF_EXPERT), idx)
    w_out = lambda idx: pl.BlockSpec((1, D_FF_EXPERT, D_MODEL), idx)
    return pl.pallas_call(
        _gmm_kernel,
        grid_spec=pltpu.PrefetchScalarGridSpec(
            num_scalar_prefetch=2, grid=(n_slots // (2 * TR_GMM),),
            in_specs=[pl.BlockSpec((2 * TR_GMM, w), rows),
                      w_in(expert0), w_in(expert0), w_out(expert0), w_in(expert1), w_in(expert1), w_out(expert1)],
            out_specs=pl.BlockSpec((2 * TR_GMM, w), rows)),
        out_shape=jax.ShapeDtypeStruct((n_slots, w), xs.dtype),
        compiler_params=pltpu.CompilerParams(dimension_semantics=("arbitrary",), vmem_limit_bytes=VMEM_LIMIT),
        name="gmm",
    )(tile_expert, n_used, xs, w_gate, w_up, w_down, w_gate, w_up, w_down)


def _final_kernel(h1_ref, y1_ref, y2_ref, rt_ref, p_ref, gple_ref, wpg_ref, wpp_ref, gfin_ref, o_ref):
    for r in range(h1_ref.shape[0] // SUB_MOE):
        rows = slice(r * SUB_MOE, (r + 1) * SUB_MOE)
        rt = rt_ref[rows, :]
        w1, w2 = rt[:, RT_W:RT_W + 1], rt[:, RT_W + 1:RT_W + 2]
        a1, b1 = _unpack_halves(y1_ref[rows, :])
        a2, b2 = _unpack_halves(y2_ref[rows, :])
        h2 = h1_ref[rows, :] + jnp.concatenate([w1 * a1 + w2 * a2, w1 * b1 + w2 * b2], axis=1)
        gate = jax.nn.sigmoid(_dot(_rms(h2, gple_ref[...]).astype(_BF16), wpg_ref[...]))
        h3 = h2 + _dot(p_ref[rows, :].astype(_BF16), wpp_ref[...]) * gate
        o_ref[rows, :] = _rms(h3, gfin_ref[...])


def _final_call(h1, y1, y2, rt, p2, gple, wpg, wpp, gfin):
    n = h1.shape[0]
    tm = TM_POST
    row = lambda i: (i, 0)
    const2 = lambda i: (0, 0)
    return pl.pallas_call(
        _final_kernel,
        grid=(n // tm,),
        in_specs=[pl.BlockSpec((tm, D_MODEL), row), pl.BlockSpec((tm, D_MODEL // 2), row),
                  pl.BlockSpec((tm, D_MODEL // 2), row), pl.BlockSpec((tm, GATE_PAD), row),
                  pl.BlockSpec((tm, D_PLE), row), pl.BlockSpec((1, D_MODEL), const2),
                  pl.BlockSpec((D_MODEL, D_MODEL), const2), pl.BlockSpec((D_PLE, D_MODEL), const2),
                  pl.BlockSpec((1, D_MODEL), const2)],
        out_specs=pl.BlockSpec((tm, D_MODEL), row),
        out_shape=jax.ShapeDtypeStruct((n, D_MODEL), _F32),
        compiler_params=pltpu.CompilerParams(dimension_semantics=("arbitrary",), vmem_limit_bytes=VMEM_LIMIT),
        name="final",
    )(h1, y1, y2, rt, p2, gple, wpg, wpp, gfin)


def _rope_tables(seq):
    half = HEAD_DIM // 2
    inv = 1.0 / (ROPE_THETA ** (jnp.arange(half, dtype=_F32) / half))
    ang = jnp.arange(seq, dtype=_F32)[:, None] * inv[None, :]
    cos, sin = jnp.cos(ang), jnp.sin(ang)
    reps = LANES // HEAD_DIM
    cos_t = jnp.tile(jnp.concatenate([cos, cos], axis=1), (1, reps))
    sin_t = jnp.tile(jnp.concatenate([-sin, sin], axis=1), (1, reps))
    return cos_t, sin_t


def _layer(h, p_i, norm_mix, w_in, gmlp_v_norm, gmlp_w_s, gmlp_b_s,
           cmp_pe_k, cmp_w1_k, cmp_w2_k, cmp_pe_v, cmp_w1_v, cmp_w2_v,
           out_norm_a, out_norm_b, w_o, norm_moe, router_group, router_expert,
           moe_w_gate, moe_w_up, moe_w_down, norm_ple, w_ple_proj, w_ple_gate, norm_final):
    bsz, seq, _ = h.shape
    n = bsz * seq
    x2 = h.reshape(n, D_MODEL)
    row = lambda v: v.reshape(1, -1).astype(_F32)

    w_q = _perm_heads(w_in[:, OFF_Q:OFF_KV], 1)
    w_gate = jnp.pad(w_in[:, OFF_GATE:D_IN], ((0, 0), (0, GATE_PAD - N_GATES)))
    w_all = jnp.concatenate([w_in[:, :OFF_Q], w_q, w_in[:, OFF_KV:OFF_GATE], w_gate], axis=1).astype(_BF16)
    cos_t, sin_t = _rope_tables(seq)
    ws_pairs = gmlp_w_s.reshape(A_HEADS // 2, 2, CHUNK, CHUNK).transpose(0, 2, 1, 3).reshape(
        A_HEADS // 2, CHUNK, 2 * CHUNK)
    bs_exp = jnp.repeat(gmlp_b_s.T, HEAD_DIM, axis=1)

    oa, q, k_cmp, v_cmp, ks, vs0, vs1, k_win, v_win, gates = _proj_call(
        x2, row(norm_mix), w_all, cos_t, sin_t, row(gmlp_v_norm), ws_pairs, bs_exp, row(out_norm_a), seq)

    kc, vc = _compress_call(k_cmp, v_cmp, _compress_weights(cmp_w1_k, cmp_w2_k, cmp_pe_k),
                            _compress_weights(cmp_w1_v, cmp_w2_v, cmp_pe_v), bsz, seq)

    b3 = lambda a: a.reshape(bsz, seq, a.shape[-1])
    ob, (wg_bf, wu_bf, wd_bf) = _attn_call(
        b3(q), b3(gates), kc, vc, b3(ks), b3(vs0), b3(vs1), b3(k_win), b3(v_win),
        row(_perm_heads(out_norm_b, 0)), [moe_w_gate, moe_w_up, moe_w_down], bsz, seq)

    r_cat = jnp.pad(jnp.concatenate([router_group, router_expert], axis=1),
                    ((0, 0), (0, GATE_PAD - N_GROUPS - N_EXPERTS)))
    r_hi = r_cat.astype(_BF16)
    r_cat = jnp.concatenate([r_hi, (r_cat - r_hi.astype(_F32)).astype(_BF16)], axis=1)
    woa, wob = w_o[:A_WIDTH].astype(_BF16), _perm_heads(w_o[A_WIDTH:], 0).astype(_BF16)
    wpg, wpp = w_ple_gate.astype(_BF16), w_ple_proj.astype(_BF16)

    h1, xp, rt, cnt = _post_call(x2, oa, ob.reshape(n, B_WIDTH), woa, wob, row(norm_moe), r_cat)

    dest, meta = _route_call(rt, cnt)
    d1, d2 = dest[0], dest[1]
    n_tiles = 2 * n // TR_GMM + N_EXPERTS
    ends = jnp.cumsum(meta[0, :N_EXPERTS].astype(jnp.int32))
    tile_expert = jnp.minimum(jnp.sum(ends[None, :] <= jnp.arange(n_tiles)[:, None], axis=1),
                              N_EXPERTS - 1).astype(jnp.int32)
    xs = _dispatch_call(xp, d1, d2, n_tiles * TR_GMM)
    ys = _gmm_call(tile_expert, ends[-1:], xs, wg_bf, wu_bf, wd_bf)
    y1, y2 = _combine_call(ys, d1, d2)
    out = _final_call(h1, y1, y2, rt, p_i.reshape(n, D_PLE), row(norm_ple), wpg, wpp, row(norm_final))
    return out.reshape(bsz, seq, D_MODEL)


def kernel(x, p, norm_mix, w_in, gmlp_v_norm, gmlp_w_s, gmlp_b_s, cmp_pe_k, cmp_w1_k, cmp_w2_k,
           cmp_pe_v, cmp_w1_v, cmp_w2_v, out_norm_a, out_norm_b, w_o, norm_moe, router_group,
           router_expert, moe_w_gate, moe_w_up, moe_w_down, norm_ple, w_ple_proj, w_ple_gate, norm_final):
    assert p.shape[0] == 1, "single-layer trunk"
    assert x.shape[1] % SEL_CK == 0 and x.shape[1] >= WIN + Q_BLOCK
    assert (x.shape[0] * x.shape[1]) % (SC_CORES * SC_SUBCORES * SC_CHUNK) == 0
    return _layer(x, p[0], norm_mix[0], w_in[0], gmlp_v_norm[0], gmlp_w_s[0], gmlp_b_s[0],
                  cmp_pe_k[0], cmp_w1_k[0], cmp_w2_k[0], cmp_pe_v[0], cmp_w1_v[0], cmp_w2_v[0],
                  out_norm_a[0], out_norm_b[0], w_o[0], norm_moe[0], router_group[0], router_expert[0],
                  moe_w_gate[0], moe_w_up[0], moe_w_down[0], norm_ple[0], w_ple_proj[0], w_ple_gate[0],
                  norm_final)
```

```python
import functools

import numpy as np
import jax
import jax.numpy as jnp
from jax import lax
from jax.experimental import pallas as pl
from jax.experimental.pallas import tpu as pltpu
from jax.experimental.pallas import tpu_sc as plsc

D_MODEL = 1024
HEAD_DIM = 64
A_HEADS = 8
A_WIDTH = A_HEADS * HEAD_DIM
B_HEADS = 8
B_WIDTH = B_HEADS * HEAD_DIM
B_KV = 2
B_HPG = B_HEADS // B_KV
KV_W = B_KV * HEAD_DIM
N_GATES = B_HEADS * 3
CHUNK = 128
L_CMP = 32
STRIDE_CMP = 16
CMP_HIDDEN = 256
L_SEL = 64
N_SEL = 16
WIN = 512
WIN_Q = 128
Q_BLOCK = 512
ROPE_THETA = 10000.0
N_GROUPS = 4
EXPERTS_PER_GROUP = 4
N_EXPERTS = N_GROUPS * EXPERTS_PER_GROUP
D_FF_EXPERT = 512
D_PLE = 256
EPS = 1e-6
LOG2E = 1.4426950408889634
NEG = -1e30
FORCE = 1e6

OFF_Q = 2 * A_WIDTH
OFF_KV = OFF_Q + B_WIDTH
OFF_GATE = OFF_KV + 6 * KV_W
D_IN = OFF_GATE + N_GATES

LANES = 128
SUBLANES = 8
GATE_PAD = LANES
ROUTER_OFF = N_GROUPS
W_ALL = OFF_GATE + GATE_PAD

TM_PROJ = 1024
TM_POST = 1024
TM_ROUTE = 512
TR_GMM = 512
SUB_MOE = 256
RT_W = 2 * N_EXPERTS
RT_ROWS = 40
ROUTER_ROWS = 24
SC_CORES = 2
SC_SUBCORES = 16
SC_CHUNK = 128
SEL_CK = 512
VMEM_LIMIT = 56 * 1024 * 1024
VMEM_LIMIT_ATTN = 60 * 1024 * 1024

_PERM_HEADS = [0, 4, 1, 5, 2, 6, 3, 7]


def _perm_heads(a, axis):
    return jnp.concatenate([lax.slice_in_dim(a, h * HEAD_DIM, (h + 1) * HEAD_DIM, axis=axis) for h in _PERM_HEADS],
                           axis=axis)

_F32 = jnp.float32
_BF16 = jnp.bfloat16


def _dot(a, b):
    return jnp.dot(a, b, preferred_element_type=_F32)


def _dot_nt(a, b):
    return lax.dot_general(a, b, (((1,), (1,)), ((), ())), preferred_element_type=_F32)


def _rms(x, g):
    return x * lax.rsqrt(jnp.mean(x * x, axis=-1, keepdims=True) + EPS) * g


def _gelu(x):
    return 0.5 * x * (1.0 + jnp.tanh(0.7978845608028654 * (x + 0.044715 * (x * x * x))))


def _rope_tile(x, cos, sin_signed):
    lane = lax.broadcasted_iota(jnp.int32, x.shape, 1)
    first_half = (lane % HEAD_DIM) < (HEAD_DIM // 2)
    rot = jnp.where(first_half, pltpu.roll(x, LANES - HEAD_DIM // 2, 1), pltpu.roll(x, HEAD_DIM // 2, 1))
    return x * cos + rot * sin_signed


def _proj_kernel(x_ref, gmix_ref, w_ref, cos_ref, sin_ref, gv_ref, ws_ref, bs_ref, goa_ref,
                 oa_ref, q_ref, kc_ref, vc_ref, ks_ref, vs0_ref, vs1_ref, kw_ref, vw_ref, gate_ref,
                 *, seq):
    tm = x_ref.shape[0]
    hn = _rms(x_ref[...], gmix_ref[...]).astype(_BF16)
    cos = cos_ref[...]
    sin = sin_ref[...]

    zu = _gelu(_dot(hn, w_ref[:, 0:A_WIDTH]))
    zv = _gelu(_dot(hn, w_ref[:, A_WIDTH:2 * A_WIDTH]))
    vn = _rms(zv, gv_ref[...]).astype(_BF16)

    t_io = lax.broadcasted_iota(jnp.int32, (CHUNK, 2 * CHUNK), 0)
    s_io = lax.broadcasted_iota(jnp.int32, (CHUNK, 2 * CHUNK), 1) % CHUNK
    causal = s_io <= t_io
    lane = lax.broadcasted_iota(jnp.int32, (CHUNK, LANES), 1)
    lo = lane < HEAD_DIM
    bs = bs_ref[...]
    n_chunks = tm // CHUNK
    pair_cols = []
    for pr in range(A_HEADS // 2):
        wcat = jnp.where(causal, ws_ref[pr], 0.0).astype(_BF16)
        rhs = []
        for c in range(n_chunks):
            vblk = vn[c * CHUNK:(c + 1) * CHUNK, pr * LANES:(pr + 1) * LANES]
            zero = jnp.zeros_like(vblk)
            rhs.append(jnp.concatenate([jnp.where(lo, vblk, zero), jnp.where(lo, zero, vblk)], axis=0))
        out = _dot(wcat, jnp.concatenate(rhs, axis=1))
        pair_cols.append(jnp.concatenate([out[:, c * LANES:(c + 1) * LANES] for c in range(n_chunks)], axis=0))
    mixed = jnp.concatenate(pair_cols, axis=1) + jnp.concatenate([bs] * n_chunks, axis=0)
    oa = zu * mixed
    oa_ref[...] = _rms(oa, goa_ref[...]).astype(oa_ref.dtype)

    zq = _dot(hn, w_ref[:, OFF_Q:OFF_KV])
    scale = HEAD_DIM ** -0.5 * LOG2E
    for j in range(B_WIDTH // LANES):
        blk = _rope_tile(zq[:, j * LANES:(j + 1) * LANES], cos, sin) * scale
        q_ref[:, j * LANES:(j + 1) * LANES] = blk.astype(q_ref.dtype)

    zkv = _dot(hn, w_ref[:, OFF_KV:OFF_GATE])
    kv = []
    for j in range(6):
        blk = zkv[:, j * KV_W:(j + 1) * KV_W]
        kv.append(_rope_tile(blk, cos, sin) if j % 2 == 0 else blk)
    k_cmp, v_cmp, k_slc, v_slc, k_win, v_win = kv
    kc_ref[...] = k_cmp.astype(kc_ref.dtype)
    vc_ref[...] = v_cmp.astype(vc_ref.dtype)
    kw_ref[...] = k_win.astype(kw_ref.dtype)
    vw_ref[...] = v_win.astype(vw_ref.dtype)

    row = lax.broadcasted_iota(jnp.int32, (tm, LANES), 0)
    lane = lax.broadcasted_iota(jnp.int32, (tm, LANES), 1)
    key_block = ((pl.program_id(0) % (seq // tm)) * tm + row) // L_SEL
    ks_ref[:, :LANES] = k_slc.astype(ks_ref.dtype)
    ks_ref[:, LANES:] = jnp.where(lane == key_block, 1.0, 0.0).astype(ks_ref.dtype)
    for g, vs_ref in enumerate((vs0_ref, vs1_ref)):
        vs_ref[...] = jnp.where((lane // HEAD_DIM) == g, v_slc, 1.0).astype(vs_ref.dtype)

    zg = _dot(hn, w_ref[:, OFF_GATE:W_ALL])
    gate_ref[...] = jax.nn.sigmoid(zg)


def _proj_call(x2, gmix, w_all, cos_t, sin_t, gv, ws_pairs, bs_exp, goa, seq):
    n = x2.shape[0]
    tm = TM_PROJ
    n_t = seq // tm
    row = lambda i: (i, 0)
    const2 = lambda i: (0, 0)
    pos = lambda i: (i % n_t, 0)
    out_shapes = [jax.ShapeDtypeStruct((n, A_WIDTH), _BF16), jax.ShapeDtypeStruct((n, B_WIDTH), _BF16)]
    kv_widths = [KV_W, KV_W, KV_W + LANES, KV_W, KV_W, KV_W, KV_W]
    out_shapes += [jax.ShapeDtypeStruct((n, w), _F32 if j < 2 else _BF16) for j, w in enumerate(kv_widths)]
    out_shapes += [jax.ShapeDtypeStruct((n, GATE_PAD), _F32)]
    out_specs = [pl.BlockSpec((tm, A_WIDTH), row), pl.BlockSpec((tm, B_WIDTH), row)]
    out_specs += [pl.BlockSpec((tm, w), row) for w in kv_widths]
    out_specs += [pl.BlockSpec((tm, GATE_PAD), row)]
    return pl.pallas_call(
        functools.partial(_proj_kernel, seq=seq),
        grid=(n // tm,),
        in_specs=[
            pl.BlockSpec((tm, D_MODEL), row),
            pl.BlockSpec((1, D_MODEL), const2),
            pl.BlockSpec((D_MODEL, W_ALL), const2),
            pl.BlockSpec((tm, LANES), pos),
            pl.BlockSpec((tm, LANES), pos),
            pl.BlockSpec((1, A_WIDTH), const2),
            pl.BlockSpec((A_HEADS // 2, CHUNK, 2 * CHUNK), lambda i: (0, 0, 0)),
            pl.BlockSpec((CHUNK, A_WIDTH), const2),
            pl.BlockSpec((1, A_WIDTH), const2),
        ],
        out_specs=out_specs,
        out_shape=out_shapes,
        compiler_params=pltpu.CompilerParams(dimension_semantics=("arbitrary",), vmem_limit_bytes=VMEM_LIMIT),
        name="proj",
    )(x2, gmix, w_all, cos_t, sin_t, gv, ws_pairs, bs_exp, goa)


def _compress_kernel(rk_ref, rv_ref, pek_ref, pev_ref, w1k_ref, w1v_ref,
                     tk_ref, bk_ref, w2k_ref, tv_ref, bv_ref, w2v_ref, kc_ref, vc_ref):
    def one(r_ref, pe_ref, w1_ref, top_ref, bot_ref, w2_ref, o_ref):
        nr = r_ref.shape[1] // STRIDE_CMP
        r = jnp.concatenate([r_ref[0, pl.ds(l, nr, stride=STRIDE_CMP), :] for l in range(STRIDE_CMP)],
                            axis=1).astype(_BF16)
        a = _dot(r, top_ref[...])
        b = _dot(r, bot_ref[...])
        pe_h = _dot(pe_ref[...], w1_ref[...])
        pe2 = jnp.concatenate([pe_h[0:1], pe_h[0:1]], axis=1)
        hid = a + pltpu.roll(b, nr - 1, 0) + pe2
        o_ref[0] = _dot(_gelu(hid).astype(_BF16), w2_ref[...]).astype(o_ref.dtype)

    one(rk_ref, pek_ref, w1k_ref, tk_ref, bk_ref, w2k_ref, kc_ref)
    one(rv_ref, pev_ref, w1v_ref, tv_ref, bv_ref, w2v_ref, vc_ref)


def _compress_weights(w1, w2, pe):
    half = L_CMP // 2
    w1r = w1.reshape(L_CMP, HEAD_DIM, CMP_HIDDEN)
    eye = jnp.eye(B_KV, dtype=w1.dtype)
    place = lambda part: jnp.einsum('ldj,gh->lgdhj', part, eye).reshape(half * KV_W, B_KV * CMP_HIDDEN)
    top = place(w1r[:half]).astype(_BF16)
    bot = place(w1r[half:]).astype(_BF16)
    w2bd = jnp.einsum('jd,gh->gjhd', w2, eye).reshape(B_KV * CMP_HIDDEN, KV_W).astype(_BF16)
    pe8 = jnp.broadcast_to(pe.reshape(1, L_CMP * HEAD_DIM), (SUBLANES, L_CMP * HEAD_DIM)).astype(_BF16)
    return pe8, w1.astype(_BF16), top, bot, w2bd


def _compress_call(k_cmp, v_cmp, wk, wv, bsz, seq):
    nr = seq // STRIDE_CMP
    rk = k_cmp.reshape(bsz, seq, KV_W)
    rv = v_cmp.reshape(bsz, seq, KV_W)
    pek, w1k, tk, bk, w2k = wk
    pev, w1v, tv, bv, w2v = wv
    rspec = pl.BlockSpec((1, seq, KV_W), lambda b: (b, 0, 0))
    full = lambda a: pl.BlockSpec(a.shape, lambda b: (0,) * a.ndim)
    ospec = pl.BlockSpec((1, nr, KV_W), lambda b: (b, 0, 0))
    return pl.pallas_call(
        _compress_kernel,
        grid=(bsz,),
        in_specs=[rspec, rspec, full(pek), full(pev), full(w1k), full(w1v),
                  full(tk), full(bk), full(w2k), full(tv), full(bv), full(w2v)],
        out_specs=[ospec, ospec],
        out_shape=[jax.ShapeDtypeStruct((bsz, nr, KV_W), _BF16)] * 2,
        compiler_params=pltpu.CompilerParams(dimension_semantics=("arbitrary",), vmem_limit_bytes=VMEM_LIMIT),
        name="compress",
    )(rk, rv, pek, pev, w1k, w1v, tk, bk, w2k, tv, bv, w2v)


def _topk_rows_mask(sc_t, k):
    n_rows = sc_t.shape[0]
    row = lax.broadcasted_iota(jnp.int32, sc_t.shape, 0).astype(sc_t.dtype)
    taken = jnp.asarray(-3e38, sc_t.dtype)
    for _ in range(k):
        m = jnp.max(sc_t, axis=0, keepdims=True)
        idx = jnp.min(jnp.where(sc_t == m, row, jnp.asarray(n_rows, sc_t.dtype)), axis=0, keepdims=True)
        sc_t = jnp.where(row == idx, taken, sc_t)
    return jnp.where(sc_t == taken, 1.0, 0.0).astype(_F32)


def _attn_kernel(q_ref, gate_ref, gexp_ref, kc_ref, vc_ref, ks_ref, vs0_ref, vs1_ref,
                 kw_ref, vw_ref, gob_ref, *rest, seq, n_cast):
    cast_in, o_ref, cast_out = rest[:n_cast], rest[n_cast], rest[n_cast + 1:2 * n_cast + 1]
    m_scr, acc_scr = rest[2 * n_cast + 1:]
    for src, dst in zip(cast_in, cast_out):
        dst[...] = src[...].astype(dst.dtype)

    qb = pl.program_id(1)
    t0 = qb * Q_BLOCK
    n_cmp = kc_ref.shape[1]
    n_sb = seq // L_SEL
    k_top = min(N_SEL, n_sb)
    rows = B_HPG * Q_BLOCK
    vs_refs = (vs0_ref, vs1_ref)

    lane_q = lax.broadcasted_iota(jnp.int32, (Q_BLOCK, LANES), 1)
    lo = lane_q < HEAD_DIM
    t_col = t0 + lax.broadcasted_iota(jnp.int32, (Q_BLOCK, 1), 0)

    def per_head(x):
        return x.reshape(B_HPG, Q_BLOCK, x.shape[-1])

    def add_bias(s, bias):
        return (per_head(s) + bias[None]).reshape(rows, s.shape[-1])

    qs = []
    for g in range(B_KV):
        own = jnp.where((lane_q // HEAD_DIM) == g, 1.0, 0.0).astype(q_ref.dtype)
        qs.append(jnp.concatenate([q_ref[0, :, j * LANES:(j + 1) * LANES] * own for j in range(B_HPG)], axis=0))

    thr0 = (t0 - (L_CMP - 1)) // STRIDE_CMP
    n_thr = (Q_BLOCK - 1) // STRIDE_CMP + 2
    assert n_thr <= LANES
    thr_rel = (t_col - (L_CMP - 1)) // STRIDE_CMP - thr0
    q_thr = jnp.where(lane_q == thr_rel, 1.0, 0.0).astype(_BF16)
    c_row1 = lax.broadcasted_iota(jnp.int32, (n_cmp, LANES), 0)
    c_lane1 = lax.broadcasted_iota(jnp.int32, (n_cmp, LANES), 1)
    k_thr = jnp.where((c_lane1 < n_thr) & (c_row1 > thr0 + c_lane1), NEG, 0.0).astype(_BF16)
    kc_wide = jnp.concatenate([kc_ref[0], k_thr], axis=1)
    has_c = (t_col >= L_CMP - 1).astype(_F32)
    c_row = lax.broadcasted_iota(jnp.int32, (n_cmp, n_sb), 0) * STRIDE_CMP
    s_col = lax.broadcasted_iota(jnp.int32, (n_cmp, n_sb), 1) * L_SEL
    overlap = jnp.where((c_row < s_col + L_SEL) & (c_row + L_CMP > s_col), 1.0, 0.0).astype(_BF16)
    blk = lax.broadcasted_iota(jnp.int32, (Q_BLOCK, n_sb), 1)
    cur = t_col // L_SEL
    forced = (blk == 0) | (blk == cur) | (blk == cur - 1)
    valid = blk * L_SEL <= t_col

    o_c, sel_bias = [], []
    for g in range(B_KV):
        s_c = _dot_nt(jnp.concatenate([qs[g], jnp.concatenate([q_thr] * B_HPG, axis=0)], axis=1), kc_wide)
        e_c = jnp.exp2(s_c - jnp.max(s_c, axis=-1, keepdims=True))
        inv = per_head(1.0 / jnp.maximum(jnp.sum(e_c, axis=-1, keepdims=True), 1e-30)) * has_c[None]
        e_bf = e_c.astype(_BF16)
        o_c.append((per_head(_dot(e_bf, vc_ref[0])) * inv).reshape(rows, LANES))
        p_bf = per_head(e_bf) * inv.astype(_BF16)
        imp = _dot(functools.reduce(lambda a, b: a + b, [p_bf[j] for j in range(B_HPG)]), overlap)
        score = jnp.where(valid & jnp.logical_not(forced), imp, -FORCE)
        chosen = forced | (_topk_rows_mask(score.T, k_top - 3).T > 0.5)
        sb = jnp.where(chosen & valid, 0.0, NEG)
        if n_sb < LANES:
            sb = jnp.concatenate([sb, jnp.full((Q_BLOCK, LANES - n_sb), NEG, _F32)], axis=1)
        sel_bias.append(sb)

    w_len = WIN + WIN_Q
    lane_w = lax.broadcasted_iota(jnp.int32, (w_len, LANES), 1)
    keep_w = [jnp.where((lane_w // HEAD_DIM) == g, 1.0, 0.0).astype(vw_ref.dtype) for g in range(B_KV)]
    fill_w = [1 - k for k in keep_w]
    acc_w_sub = [[] for _ in range(B_KV)]
    for hh in range(Q_BLOCK // WIN_Q):
        w_start = pl.multiple_of(jnp.maximum(t0 + hh * WIN_Q - WIN, 0), WIN_Q)
        t_sub = t_col[hh * WIN_Q:(hh + 1) * WIN_Q]
        diff_w = t_sub - (w_start + lax.broadcasted_iota(jnp.int32, (WIN_Q, w_len), 1))
        bias_w = jnp.where((diff_w >= 0) & (diff_w < WIN), 0.0, NEG)
        kw = kw_ref[0, pl.ds(w_start, w_len), :]
        vw = vw_ref[0, pl.ds(w_start, w_len), :]
        for g in range(B_KV):
            q_sub = per_head(qs[g])[:, hh * WIN_Q:(hh + 1) * WIN_Q, :].reshape(B_HPG * WIN_Q, LANES)
            s_w = (_dot_nt(q_sub, kw).reshape(B_HPG, WIN_Q, w_len) + bias_w[None]).reshape(B_HPG * WIN_Q, w_len)
            e_w = jnp.exp2(s_w - jnp.max(s_w, axis=-1, keepdims=True)).astype(_BF16)
            vw_aug = vw * keep_w[g] + fill_w[g]
            acc_w_sub[g].append(_dot(e_w, vw_aug).reshape(B_HPG, WIN_Q, LANES))
    acc_w = [jnp.concatenate(acc_w_sub[g], axis=1).reshape(rows, LANES) for g in range(B_KV)]

    n_ck = (t0 + Q_BLOCK + SEL_CK - 1) // SEL_CK
    key_lane = lax.broadcasted_iota(jnp.int32, (Q_BLOCK, SEL_CK), 1)
    bias_diag = jnp.where((n_ck - 1) * SEL_CK + key_lane <= t_col, 0.0, NEG)
    q_wide = [jnp.concatenate([qs[g], jnp.concatenate([sel_bias[g].astype(_BF16)] * B_HPG, axis=0)], axis=1)
              for g in range(B_KV)]

    m_scr[...] = jnp.full(m_scr.shape, NEG, _F32)
    acc_scr[...] = jnp.zeros(acc_scr.shape, _F32)

    def sel_chunk(ci, diag):
        k0 = pl.multiple_of(ci * SEL_CK, SEL_CK)
        for g in range(B_KV):
            m = m_scr[g]
            s = _dot_nt(q_wide[g], ks_ref[0, pl.ds(k0, SEL_CK), :])
            if diag:
                s = add_bias(s, bias_diag)
            m_new = jnp.maximum(m, jnp.max(s, axis=-1, keepdims=True))
            p = jnp.exp2(s - jnp.concatenate([m_new] * (SEL_CK // LANES), axis=1)).astype(_BF16)
            acc_scr[g] = jnp.exp2(m - m_new) * acc_scr[g] + _dot(p, vs_refs[g][0, pl.ds(k0, SEL_CK), :])
            m_scr[g] = m_new

    n_before = n_ck - 1

    @pl.loop(0, n_before // 2)
    def _(pi):
        sel_chunk(2 * pi, False)
        sel_chunk(2 * pi + 1, False)

    @pl.when(n_before % 2 == 1)
    def _():
        sel_chunk(n_before - 1, False)

    sel_chunk(n_ck - 1, True)
    acc_s = [acc_scr[g] for g in range(B_KV)]

    def numer(acc):
        return jnp.concatenate([jnp.where(lo, acc[0][j * Q_BLOCK:(j + 1) * Q_BLOCK],
                                          acc[1][j * Q_BLOCK:(j + 1) * Q_BLOCK]) for j in range(B_HPG)], axis=1)

    def denom(acc):
        return jnp.concatenate([pltpu.roll(jnp.where(lo, acc[1][j * Q_BLOCK:(j + 1) * Q_BLOCK],
                                                     acc[0][j * Q_BLOCK:(j + 1) * Q_BLOCK]), HEAD_DIM, 1)
                                for j in range(B_HPG)], axis=1)

    gates = gate_ref[0]
    g_hi = gates.astype(_BF16)
    g_split = jnp.concatenate([g_hi, (gates - g_hi.astype(_F32)).astype(_BF16)], axis=1)
    gate_of = lambda r: _dot(g_split, gexp_ref[r])
    ob = (gate_of(0) * numer(o_c)
          + gate_of(1) * numer(acc_s) * (1.0 / jnp.maximum(denom(acc_s), 1e-30))
          + gate_of(2) * numer(acc_w) * (1.0 / jnp.maximum(denom(acc_w), 1e-30)))
    o_ref[0] = _rms(ob, gob_ref[...]).astype(o_ref.dtype)


def _gate_expand():
    x = np.zeros((3, GATE_PAD, B_WIDTH), np.float32)
    for slot, h in enumerate(_PERM_HEADS):
        for r in range(3):
            x[r, 3 * h + r, slot * HEAD_DIM:(slot + 1) * HEAD_DIM] = 1.0
    return jnp.asarray(np.concatenate([x, x], axis=1), _BF16)


def _attn_call(q, gates, kc, vc, ks, vs0, vs1, kw, vw, gob, to_cast, bsz, seq):
    assert seq // L_SEL <= LANES
    n_cmp = kc.shape[1]
    n_q = seq // Q_BLOCK
    steps = bsz * n_q
    qspec = lambda w: pl.BlockSpec((1, Q_BLOCK, w), lambda b, i: (b, i, 0))
    full = lambda r, w=KV_W: pl.BlockSpec((1, r, w), lambda b, i: (b, 0, 0), pipeline_mode=pl.Buffered(1))
    sliced = [a.reshape(steps, a.size // (steps * a.shape[-1]), a.shape[-1]) for a in to_cast]
    cast_specs = [pl.BlockSpec((1,) + a.shape[1:], lambda b, i: (b * n_q + i, 0, 0)) for a in sliced]
    outs = pl.pallas_call(
        functools.partial(_attn_kernel, seq=seq, n_cast=len(sliced)),
        grid=(bsz, n_q),
        in_specs=[qspec(B_WIDTH), qspec(GATE_PAD),
                  pl.BlockSpec((3, 2 * GATE_PAD, B_WIDTH), lambda b, i: (0, 0, 0)),
                  full(n_cmp), full(n_cmp), full(seq, KV_W + LANES),
                  full(seq), full(seq), full(seq), full(seq),
                  pl.BlockSpec((1, B_WIDTH), lambda b, i: (0, 0))] + cast_specs,
        out_specs=[qspec(B_WIDTH)] + cast_specs,
        out_shape=[jax.ShapeDtypeStruct((bsz, seq, B_WIDTH), _BF16)]
        + [jax.ShapeDtypeStruct(a.shape, _BF16) for a in sliced],
        scratch_shapes=[pltpu.VMEM((B_KV, B_HPG * Q_BLOCK, LANES), _F32),
                        pltpu.VMEM((B_KV, B_HPG * Q_BLOCK, LANES), _F32)],
        compiler_params=pltpu.CompilerParams(dimension_semantics=("arbitrary", "arbitrary"),
                                             vmem_limit_bytes=VMEM_LIMIT_ATTN),
        name="attn",
    )(q, gates, _gate_expand(), kc, vc, ks, vs0, vs1, kw, vw, gob, *sliced)
    return outs[0], [o.reshape(a.shape) for o, a in zip(outs[1:], to_cast)]


def _pack_halves(x):
    w = x.shape[1] // 2
    bits = lambda v: lax.bitcast_convert_type(v.astype(_BF16).astype(_F32), jnp.uint32)
    return lax.bitcast_convert_type(bits(x[:, :w]) | (bits(x[:, w:]) >> 16), jnp.int32)


def _unpack_halves(p):
    u = lax.bitcast_convert_type(p, jnp.uint32)
    return (lax.bitcast_convert_type(u & jnp.uint32(0xFFFF0000), _F32),
            lax.bitcast_convert_type(u << 16, _F32))


def _post_kernel(x_ref, oa_ref, ob_ref, woa_ref, wob_ref, gmoe_ref, r_ref, h1_ref, hn_ref, rt_ref, cnt_ref):
    h1 = x_ref[...] + _dot(oa_ref[...], woa_ref[...]) + _dot(ob_ref[...], wob_ref[...])
    h1_ref[...] = h1
    hn = _rms(h1, gmoe_ref[...])
    hn_ref[...] = _pack_halves(hn)

    hn_hi = hn.astype(_BF16)
    hn_lo = (hn - hn_hi.astype(_F32)).astype(_BF16)
    hi_both = _dot(hn_hi, r_ref[...])
    logits = hi_both[:, :GATE_PAD] + (_dot(hn_lo, r_ref[:, :GATE_PAD]) + hi_both[:, GATE_PAD:])
    lt = logits.T[:ROUTER_ROWS]
    row = lax.broadcasted_iota(jnp.int32, lt.shape, 0)
    first_idx = lambda hit: jnp.min(jnp.where(hit, row, LANES), axis=0, keepdims=True)

    is_g = row < N_GROUPS
    lg = jnp.where(is_g, lt, NEG)
    mg = jnp.max(lg, axis=0, keepdims=True)
    sg = jnp.sum(jnp.where(is_g, jnp.exp(lg - mg), 0.0), axis=0, keepdims=True)
    pg_top = 1.0 / sg
    g_sel = first_idx(is_g & (lg == mg))

    e_lo = ROUTER_OFF + g_sel * EXPERTS_PER_GROUP
    is_e = (row >= e_lo) & (row < e_lo + EXPERTS_PER_GROUP)
    le = jnp.where(is_e, lt, NEG)
    m1 = jnp.max(le, axis=0, keepdims=True)
    se = jnp.sum(jnp.where(is_e, jnp.exp(le - m1), 0.0), axis=0, keepdims=True)
    i1 = first_idx(is_e & (le == m1))
    le2 = jnp.where(row == i1, NEG, le)
    m2 = jnp.max(le2, axis=0, keepdims=True)
    i2 = first_idx(is_e & (row != i1) & (le2 == m2))
    pe1 = 1.0 / se
    pe2 = jnp.exp(m2 - m1) / se
    denom = pe1 + pe2
    rrow = lax.broadcasted_iota(jnp.int32, (RT_ROWS, lt.shape[1]), 0)
    rt_t = (jnp.where(rrow == i1 - ROUTER_OFF, 1.0, 0.0)
            + jnp.where(rrow == i2 - ROUTER_OFF + N_EXPERTS, 1.0, 0.0)
            + jnp.where(rrow == RT_W, pg_top * pe1 / denom, 0.0)
            + jnp.where(rrow == RT_W + 1, pg_top * pe2 / denom, 0.0))
    rt = jnp.concatenate([rt_t, jnp.zeros((LANES - RT_ROWS, lt.shape[1]), _F32)], axis=0).T
    rt_ref[...] = rt

    @pl.when(pl.program_id(0) == 0)
    def _():
        cnt_ref[...] = jnp.zeros_like(cnt_ref)

    cnt_ref[...] += jnp.sum(rt, axis=0, keepdims=True)


def _post_call(x2, oa, ob, woa, wob, gmoe, r_cat):
    n = x2.shape[0]
    tm = TM_POST
    row = lambda i: (i, 0)
    const2 = lambda i: (0, 0)
    return pl.pallas_call(
        _post_kernel,
        grid=(n // tm,),
        in_specs=[pl.BlockSpec((tm, D_MODEL), row), pl.BlockSpec((tm, A_WIDTH), row),
                  pl.BlockSpec((tm, B_WIDTH), row), pl.BlockSpec((A_WIDTH, D_MODEL), const2),
                  pl.BlockSpec((B_WIDTH, D_MODEL), const2), pl.BlockSpec((1, D_MODEL), const2),
                  pl.BlockSpec((D_MODEL, 2 * GATE_PAD), const2)],
        out_specs=[pl.BlockSpec((tm, D_MODEL), row), pl.BlockSpec((tm, D_MODEL // 2), row),
                   pl.BlockSpec((tm, GATE_PAD), row), pl.BlockSpec((SUBLANES, LANES), const2)],
        out_shape=[jax.ShapeDtypeStruct((n, D_MODEL), _F32), jax.ShapeDtypeStruct((n, D_MODEL // 2), jnp.int32),
                   jax.ShapeDtypeStruct((n, GATE_PAD), _F32), jax.ShapeDtypeStruct((SUBLANES, LANES), _F32)],
        compiler_params=pltpu.CompilerParams(dimension_semantics=("arbitrary",), vmem_limit_bytes=VMEM_LIMIT),
        name="post",
    )(x2, oa, ob, woa, wob, gmoe, r_cat)


def _route_kernel(rt_ref, cnt_ref, earlier_ref, dest_ref, meta_ref, off_ref, run_ref):
    tm = rt_ref.shape[0]
    lane = lax.broadcasted_iota(jnp.int32, (1, LANES), 1)
    first = lane < N_EXPERTS
    onehot = jnp.where(lane < 2 * N_EXPERTS, rt_ref[...], 0.0)

    @pl.when(pl.program_id(0) == 0)
    def _():
        cnt = jnp.where(lane < 2 * N_EXPERTS, cnt_ref[...], 0.0)
        c1 = jnp.where(first, cnt, 0.0)
        tot = c1 + jnp.where(first, pltpu.roll(cnt, LANES - N_EXPERTS, 1), 0.0)
        tiles = jnp.floor((tot + (TR_GMM - 1)) * (1.0 / TR_GMM))
        e_row = lax.broadcasted_iota(jnp.int32, (LANES, LANES), 0)
        e_col = lax.broadcasted_iota(jnp.int32, (LANES, LANES), 1)
        before = jnp.where(e_row < e_col, 1.0, 0.0).astype(_BF16)
        base = _dot(tiles.astype(_BF16), before) * TR_GMM
        off_ref[...] = jnp.where(first, base, 0.0) + pltpu.roll(jnp.where(first, base + c1, 0.0), N_EXPERTS, 1)
        run_ref[...] = jnp.zeros_like(run_ref)
        meta_ref[...] = tiles

    rank = _dot(earlier_ref[...], onehot.astype(_BF16)) + run_ref[0:1, :]
    slot = onehot * (rank + off_ref[0:1, :])
    slot_t = slot.T
    d1 = jnp.sum(slot_t[:N_EXPERTS], axis=0, keepdims=True)
    d2 = jnp.sum(slot_t[N_EXPERTS:2 * N_EXPERTS], axis=0, keepdims=True)
    dest_ref[...] = jnp.concatenate([d1, d2, jnp.zeros((SUBLANES - 2, tm), _F32)], axis=0).astype(jnp.int32)
    run_ref[...] += jnp.sum(onehot, axis=0, keepdims=True)


def _route_call(rt, cnt):
    n = rt.shape[0]
    tm = TM_ROUTE
    return pl.pallas_call(
        _route_kernel,
        grid=(n // tm,),
        in_specs=[pl.BlockSpec((tm, GATE_PAD), lambda i: (i, 0)), pl.BlockSpec((SUBLANES, LANES), lambda i: (0, 0)),
                  pl.BlockSpec((tm, tm), lambda i: (0, 0))],
        out_specs=[pl.BlockSpec((SUBLANES, tm), lambda i: (0, i)), pl.BlockSpec((SUBLANES, LANES), lambda i: (0, 0))],
        out_shape=[jax.ShapeDtypeStruct((SUBLANES, n), jnp.int32), jax.ShapeDtypeStruct((SUBLANES, LANES), _F32)],
        scratch_shapes=[pltpu.VMEM((SUBLANES, LANES), _F32)] * 2,
        compiler_params=pltpu.CompilerParams(dimension_semantics=("arbitrary",)),
        name="route",
    )(rt, cnt, jnp.asarray(np.tril(np.ones((tm, tm), np.float32), -1), _BF16))


def _sc_mesh():
    return plsc.VectorSubcoreMesh(core_axis_name="c", subcore_axis_name="s")


def _sc_worker(n_rows):
    per = n_rows // (SC_CORES * SC_SUBCORES)
    return (lax.axis_index("s") * SC_CORES + lax.axis_index("c")) * per, per


def _dispatch_call(xp, d1, d2, n_slots):
    n, w = xp.shape
    assert n % (SC_CORES * SC_SUBCORES * SC_CHUNK) == 0

    @functools.partial(
        pl.kernel, mesh=_sc_mesh(), out_type=jax.ShapeDtypeStruct((n_slots, w), xp.dtype),
        scratch_types=[pltpu.VMEM((SC_CHUNK,), jnp.int32), pltpu.VMEM((SC_CHUNK,), jnp.int32),
                       pltpu.VMEM((SC_CHUNK, w), xp.dtype), pltpu.SemaphoreType.DMA],
        name="dispatch")
    def k(x_hbm, d1_hbm, d2_hbm, xs_hbm, i1_v, i2_v, rows_v, sem):
        row0, per = _sc_worker(n)

        @pl.loop(0, per // SC_CHUNK)
        def _(j):
            src = pl.ds(row0 + j * SC_CHUNK, SC_CHUNK)
            pltpu.sync_copy(d1_hbm.at[src], i1_v)
            pltpu.sync_copy(d2_hbm.at[src], i2_v)
            pltpu.sync_copy(x_hbm.at[src], rows_v)
            first = pltpu.async_copy(rows_v, xs_hbm.at[i1_v], sem)
            second = pltpu.async_copy(rows_v, xs_hbm.at[i2_v], sem)
            first.wait()
            second.wait()

    return k(xp, d1, d2)


def _combine_call(ys, d1, d2):
    n = d1.shape[0]
    w = ys.shape[1]
    assert n % (SC_CORES * SC_SUBCORES * SC_CHUNK) == 0
    out = jax.ShapeDtypeStruct((n, w), ys.dtype)

    @functools.partial(
        pl.kernel, mesh=_sc_mesh(), out_type=(out, out),
        scratch_types=[pltpu.VMEM((SC_CHUNK,), jnp.int32), pltpu.VMEM((SC_CHUNK, w), ys.dtype),
                       pltpu.SemaphoreType.DMA],
        name="combine")
    def k(ys_hbm, d1_hbm, d2_hbm, y1_hbm, y2_hbm, i_v, rows_v, sem):
        row0, per = _sc_worker(n)

        @pl.loop(0, per // SC_CHUNK)
        def _(j):
            dst = pl.ds(row0 + j * SC_CHUNK, SC_CHUNK)
            for d_hbm, y_hbm in ((d1_hbm, y1_hbm), (d2_hbm, y2_hbm)):
                pltpu.sync_copy(d_hbm.at[dst], i_v)
                pltpu.async_copy(ys_hbm.at[i_v], rows_v, sem).wait()
                pltpu.sync_copy(rows_v, y_hbm.at[dst])

    return k(ys, d1, d2)


def _gmm_kernel(te_ref, nu_ref, xs_ref, wg0_ref, wu0_ref, wd0_ref, wg1_ref, wu1_ref, wd1_ref, ys_ref):
    half = D_MODEL // 2
    t0 = 2 * pl.program_id(0)

    def ffn(rows, wg_ref, wu_ref, wd_ref):
        a, b = _unpack_halves(xs_ref[rows, :])
        a, b = a.astype(_BF16), b.astype(_BF16)
        gate = _dot(a, wg_ref[0, :half, :]) + _dot(b, wg_ref[0, half:, :])
        up = _dot(a, wu_ref[0, :half, :]) + _dot(b, wu_ref[0, half:, :])
        hid = gate * jax.nn.sigmoid(gate) * up
        ys_ref[rows, :] = _pack_halves(_dot(hid.astype(_BF16), wd_ref[0]))

    both = t0 + 1 < nu_ref[0]
    same = te_ref[t0] == te_ref[t0 + 1]

    @pl.when(both & same)
    def _():
        ffn(slice(0, 2 * TR_GMM), wg0_ref, wu0_ref, wd0_ref)

    @pl.when((t0 < nu_ref[0]) & jnp.logical_not(both & same))
    def _():
        ffn(slice(0, TR_GMM), wg0_ref, wu0_ref, wd0_ref)

    @pl.when(both & jnp.logical_not(same))
    def _():
        ffn(slice(TR_GMM, 2 * TR_GMM), wg1_ref, wu1_ref, wd1_ref)


def _gmm_call(tile_expert, n_used, xs, w_gate, w_up, w_down):
    n_slots, w = xs.shape
    assert (n_slots // TR_GMM) % 2 == 0
    last = lambda nu: jnp.maximum((nu[0] - 1) // 2, 0)
    rows = lambda u, te, nu: (jnp.minimum(u, last(nu)), 0)
    expert0 = lambda u, te, nu: (te[2 * u], 0, 0)
    expert1 = lambda u, te, nu: (te[2 * u + 1], 0, 0)
    w_in = lambda idx: pl.BlockSpec((1, D_MODEL, D_FF_EXPERT), idx)
    w_out = lambda idx: pl.BlockSpec((1, D_FF_EXPERT, D_MODEL), idx)
    return pl.pallas_call(
        _gmm_kernel,
        grid_spec=pltpu.PrefetchScalarGridSpec(
            num_scalar_prefetch=2, grid=(n_slots // (2 * TR_GMM),),
            in_specs=[pl.BlockSpec((2 * TR_GMM, w), rows),
                      w_in(expert0), w_in(expert0), w_out(expert0), w_in(expert1), w_in(expert1), w_out(expert1)],
            out_specs=pl.BlockSpec((2 * TR_GMM, w), rows)),
        out_shape=jax.ShapeDtypeStruct((n_slots, w), xs.dtype),
        compiler_params=pltpu.CompilerParams(dimension_semantics=("arbitrary",), vmem_limit_bytes=VMEM_LIMIT),
        name="gmm",
    )(tile_expert, n_used, xs, w_gate, w_up, w_down, w_gate, w_up, w_down)


def _final_kernel(h1_ref, y1_ref, y2_ref, rt_ref, p_ref, gple_ref, wpg_ref, wpp_ref, gfin_ref, o_ref):
    for r in range(h1_ref.shape[0] // SUB_MOE):
        rows = slice(r * SUB_MOE, (r + 1) * SUB_MOE)
        rt = rt_ref[rows, :]
        w1, w2 = rt[:, RT_W:RT_W + 1], rt[:, RT_W + 1:RT_W + 2]
        a1, b1 = _unpack_halves(y1_ref[rows, :])
        a2, b2 = _unpack_halves(y2_ref[rows, :])
        h2 = h1_ref[rows, :] + jnp.concatenate([w1 * a1 + w2 * a2, w1 * b1 + w2 * b2], axis=1)
        gate = jax.nn.sigmoid(_dot(_rms(h2, gple_ref[...]).astype(_BF16), wpg_ref[...]))
        h3 = h2 + _dot(p_ref[rows, :].astype(_BF16), wpp_ref[...]) * gate
        o_ref[rows, :] = _rms(h3, gfin_ref[...])


def _final_call(h1, y1, y2, rt, p2, gple, wpg, wpp, gfin):
    n = h1.shape[0]
    tm = TM_POST
    row = lambda i: (i, 0)
    const2 = lambda i: (0, 0)
    return pl.pallas_call(
        _final_kernel,
        grid=(n // tm,),
        in_specs=[pl.BlockSpec((tm, D_MODEL), row), pl.BlockSpec((tm, D_MODEL // 2), row),
                  pl.BlockSpec((tm, D_MODEL // 2), row), pl.BlockSpec((tm, GATE_PAD), row),
                  pl.BlockSpec((tm, D_PLE), row), pl.BlockSpec((1, D_MODEL), const2),
                  pl.BlockSpec((D_MODEL, D_MODEL), const2), pl.BlockSpec((D_PLE, D_MODEL), const2),
                  pl.BlockSpec((1, D_MODEL), const2)],
        out_specs=pl.BlockSpec((tm, D_MODEL), row),
        out_shape=jax.ShapeDtypeStruct((n, D_MODEL), _F32),
        compiler_params=pltpu.CompilerParams(dimension_semantics=("arbitrary",), vmem_limit_bytes=VMEM_LIMIT),
        name="final",
    )(h1, y1, y2, rt, p2, gple, wpg, wpp, gfin)


def _rope_tables(seq):
    half = HEAD_DIM // 2
    inv = 1.0 / (ROPE_THETA ** (jnp.arange(half, dtype=_F32) / half))
    ang = jnp.arange(seq, dtype=_F32)[:, None] * inv[None, :]
    cos, sin = jnp.cos(ang), jnp.sin(ang)
    reps = LANES // HEAD_DIM
    cos_t = jnp.tile(jnp.concatenate([cos, cos], axis=1), (1, reps))
    sin_t = jnp.tile(jnp.concatenate([-sin, sin], axis=1), (1, reps))
    return cos_t, sin_t


def _layer(h, p_i, norm_mix, w_in, gmlp_v_norm, gmlp_w_s, gmlp_b_s,
           cmp_pe_k, cmp_w1_k, cmp_w2_k, cmp_pe_v, cmp_w1_v, cmp_w2_v,
           out_norm_a, out_norm_b, w_o, norm_moe, router_group, router_expert,
           moe_w_gate, moe_w_up, moe_w_down, norm_ple, w_ple_proj, w_ple_gate, norm_final):
    bsz, seq, _ = h.shape
    n = bsz * seq
    x2 = h.reshape(n, D_MODEL)
    row = lambda v: v.reshape(1, -1).astype(_F32)

    w_q = _perm_heads(w_in[:, OFF_Q:OFF_KV], 1)
    w_gate = jnp.pad(w_in[:, OFF_GATE:D_IN], ((0, 0), (0, GATE_PAD - N_GATES)))
    w_all = jnp.concatenate([w_in[:, :OFF_Q], w_q, w_in[:, OFF_KV:OFF_GATE], w_gate], axis=1).astype(_BF16)
    cos_t, sin_t = _rope_tables(seq)
    ws_pairs = gmlp_w_s.reshape(A_HEADS // 2, 2, CHUNK, CHUNK).transpose(0, 2, 1, 3).reshape(
        A_HEADS // 2, CHUNK, 2 * CHUNK)
    bs_exp = jnp.repeat(gmlp_b_s.T, HEAD_DIM, axis=1)

    oa, q, k_cmp, v_cmp, ks, vs0, vs1, k_win, v_win, gates = _proj_call(
        x2, row(norm_mix), w_all, cos_t, sin_t, row(gmlp_v_norm), ws_pairs, bs_exp, row(out_norm_a), seq)

    kc, vc = _compress_call(k_cmp, v_cmp, _compress_weights(cmp_w1_k, cmp_w2_k, cmp_pe_k),
                            _compress_weights(cmp_w1_v, cmp_w2_v, cmp_pe_v), bsz, seq)

    b3 = lambda a: a.reshape(bsz, seq, a.shape[-1])
    ob, (wg_bf, wu_bf, wd_bf) = _attn_call(
        b3(q), b3(gates), kc, vc, b3(ks), b3(vs0), b3(vs1), b3(k_win), b3(v_win),
        row(_perm_heads(out_norm_b, 0)), [moe_w_gate, moe_w_up, moe_w_down], bsz, seq)

    r_cat = jnp.pad(jnp.concatenate([router_group, router_expert], axis=1),
                    ((0, 0), (0, GATE_PAD - N_GROUPS - N_EXPERTS)))
    r_hi = r_cat.astype(_BF16)
    r_cat = jnp.concatenate([r_hi, (r_cat - r_hi.astype(_F32)).astype(_BF16)], axis=1)
    woa, wob = w_o[:A_WIDTH].astype(_BF16), _perm_heads(w_o[A_WIDTH:], 0).astype(_BF16)
    wpg, wpp = w_ple_gate.astype(_BF16), w_ple_proj.astype(_BF16)

    h1, xp, rt, cnt = _post_call(x2, oa, ob.reshape(n, B_WIDTH), woa, wob, row(norm_moe), r_cat)

    dest, meta = _route_call(rt, cnt)
    d1, d2 = dest[0], dest[1]
    n_tiles = 2 * n // TR_GMM + N_EXPERTS
    ends = jnp.cumsum(meta[0, :N_EXPERTS].astype(jnp.int32))
    tile_expert = jnp.minimum(jnp.sum(ends[None, :] <= jnp.arange(n_tiles)[:, None], axis=1),
                              N_EXPERTS - 1).astype(jnp.int32)
    xs = _dispatch_call(xp, d1, d2, n_tiles * TR_GMM)
    ys = _gmm_call(tile_expert, ends[-1:], xs, wg_bf, wu_bf, wd_bf)
    y1, y2 = _combine_call(ys, d1, d2)
    out = _final_call(h1, y1, y2, rt, p_i.reshape(n, D_PLE), row(norm_ple), wpg, wpp, row(norm_final))
    return out.reshape(bsz, seq, D_MODEL)


def kernel(x, p, norm_mix, w_in, gmlp_v_norm, gmlp_w_s, gmlp_b_s, cmp_pe_k, cmp_w1_k, cmp_w2_k,
           cmp_pe_v, cmp_w1_v, cmp_w2_v, out_norm_a, out_norm_b, w_o, norm_moe, router_group,
           router_expert, moe_w_gate, moe_w_up, moe_w_down, norm_ple, w_ple_proj, w_ple_gate, norm_final):
    assert p.shape[0] == 1, "single-layer trunk"
    assert x.shape[1] % SEL_CK == 0 and x.shape[1] >= WIN + Q_BLOCK
    assert (x.shape[0] * x.shape[1]) % (SC_CORES * SC_SUBCORES * SC_CHUNK) == 0
    return _layer(x, p[0], norm_mix[0], w_in[0], gmlp_v_norm[0], gmlp_w_s[0], gmlp_b_s[0],
                  cmp_pe_k[0], cmp_w1_k[0], cmp_w2_k[0], cmp_pe_v[0], cmp_w1_v[0], cmp_w2_v[0],
                  out_norm_a[0], out_norm_b[0], w_o[0], norm_moe[0], router_group[0], router_expert[0],
                  moe_w_gate[0], moe_w_up[0], moe_w_down[0], norm_ple[0], w_ple_proj[0], w_ple_gate[0],
                  norm_final)
```

```python
import functools

import numpy as np
import jax
import jax.numpy as jnp
from jax import lax
from jax.experimental import pallas as pl
from jax.experimental.pallas import tpu as pltpu
from jax.experimental.pallas import tpu_sc as plsc

D_MODEL = 1024
HEAD_DIM = 64
A_HEADS = 8
A_WIDTH = A_HEADS * HEAD_DIM
B_HEADS = 8
B_WIDTH = B_HEADS * HEAD_DIM
B_KV = 2
B_HPG = B_HEADS // B_KV
KV_W = B_KV * HEAD_DIM
N_GATES = B_HEADS * 3
CHUNK = 128
L_CMP = 32
STRIDE_CMP = 16
CMP_HIDDEN = 256
L_SEL = 64
N_SEL = 16
WIN = 512
WIN_Q = 128
Q_BLOCK = 512
ROPE_THETA = 10000.0
N_GROUPS = 4
EXPERTS_PER_GROUP = 4
N_EXPERTS = N_GROUPS * EXPERTS_PER_GROUP
D_FF_EXPERT = 512
D_PLE = 256
EPS = 1e-6
LOG2E = 1.4426950408889634
NEG = -1e30
FORCE = 1e6

OFF_Q = 2 * A_WIDTH
OFF_KV = OFF_Q + B_WIDTH
OFF_GATE = OFF_KV + 6 * KV_W
D_IN = OFF_GATE + N_GATES

LANES = 128
SUBLANES = 8
GATE_PAD = LANES
ROUTER_OFF = N_GROUPS
W_ALL = OFF_GATE + GATE_PAD

TM_PROJ = 1024
TM_POST = 1024
TM_ROUTE = 512
TR_GMM = 512
SUB_MOE = 256
RT_W = 2 * N_EXPERTS
RT_ROWS = 40
ROUTER_ROWS = 24
SC_CORES = 2
SC_SUBCORES = 16
SC_CHUNK = 128
SEL_CK = 512
VMEM_LIMIT = 56 * 1024 * 1024
VMEM_LIMIT_ATTN = 60 * 1024 * 1024

_PERM_HEADS = [0, 4, 1, 5, 2, 6, 3, 7]


def _perm_heads(a, axis):
    return jnp.concatenate([lax.slice_in_dim(a, h * HEAD_DIM, (h + 1) * HEAD_DIM, axis=axis) for h in _PERM_HEADS],
                           axis=axis)

_F32 = jnp.float32
_BF16 = jnp.bfloat16


def _dot(a, b):
    return jnp.dot(a, b, preferred_element_type=_F32)


def _dot_nt(a, b):
    return lax.dot_general(a, b, (((1,), (1,)), ((), ())), preferred_element_type=_F32)


def _rms(x, g):
    return x * lax.rsqrt(jnp.mean(x * x, axis=-1, keepdims=True) + EPS) * g


def _gelu(x):
    return 0.5 * x * (1.0 + jnp.tanh(0.7978845608028654 * (x + 0.044715 * (x * x * x))))


def _rope_tile(x, cos, sin_signed):
    lane = lax.broadcasted_iota(jnp.int32, x.shape, 1)
    first_half = (lane % HEAD_DIM) < (HEAD_DIM // 2)
    rot = jnp.where(first_half, pltpu.roll(x, LANES - HEAD_DIM // 2, 1), pltpu.roll(x, HEAD_DIM // 2, 1))
    return x * cos + rot * sin_signed


def _proj_kernel(x_ref, gmix_ref, w_ref, cos_ref, sin_ref, gv_ref, ws_ref, bs_ref, goa_ref,
                 oa_ref, q_ref, kc_ref, vc_ref, ks_ref, vs0_ref, vs1_ref, kw_ref, vw_ref, gate_ref,
                 *, seq):
    tm = x_ref.shape[0]
    hn = _rms(x_ref[...], gmix_ref[...]).astype(_BF16)
    cos = cos_ref[...]
    sin = sin_ref[...]

    zu = _gelu(_dot(hn, w_ref[:, 0:A_WIDTH]))
    zv = _gelu(_dot(hn, w_ref[:, A_WIDTH:2 * A_WIDTH]))
    vn = _rms(zv, gv_ref[...]).astype(_BF16)

    t_io = lax.broadcasted_iota(jnp.int32, (CHUNK, 2 * CHUNK), 0)
    s_io = lax.broadcasted_iota(jnp.int32, (CHUNK, 2 * CHUNK), 1) % CHUNK
    causal = s_io <= t_io
    lane = lax.broadcasted_iota(jnp.int32, (CHUNK, LANES), 1)
    lo = lane < HEAD_DIM
    bs = bs_ref[...]
    n_chunks = tm // CHUNK
    pair_cols = []
    for pr in range(A_HEADS // 2):
        wcat = jnp.where(causal, ws_ref[pr], 0.0).astype(_BF16)
        rhs = []
        for c in range(n_chunks):
            vblk = vn[c * CHUNK:(c + 1) * CHUNK, pr * LANES:(pr + 1) * LANES]
            zero = jnp.zeros_like(vblk)
            rhs.append(jnp.concatenate([jnp.where(lo, vblk, zero), jnp.where(lo, zero, vblk)], axis=0))
        out = _dot(wcat, jnp.concatenate(rhs, axis=1))
        pair_cols.append(jnp.concatenate([out[:, c * LANES:(c + 1) * LANES] for c in range(n_chunks)], axis=0))
    mixed = jnp.concatenate(pair_cols, axis=1) + jnp.concatenate([bs] * n_chunks, axis=0)
    oa = zu * mixed
    oa_ref[...] = _rms(oa, goa_ref[...]).astype(oa_ref.dtype)

    zq = _dot(hn, w_ref[:, OFF_Q:OFF_KV])
    scale = HEAD_DIM ** -0.5 * LOG2E
    for j in range(B_WIDTH // LANES):
        blk = _rope_tile(zq[:, j * LANES:(j + 1) * LANES], cos, sin) * scale
        q_ref[:, j * LANES:(j + 1) * LANES] = blk.astype(q_ref.dtype)

    zkv = _dot(hn, w_ref[:, OFF_KV:OFF_GATE])
    kv = []
    for j in range(6):
        blk = zkv[:, j * KV_W:(j + 1) * KV_W]
        kv.append(_rope_tile(blk, cos, sin) if j % 2 == 0 else blk)
    k_cmp, v_cmp, k_slc, v_slc, k_win, v_win = kv
    kc_ref[...] = k_cmp.astype(kc_ref.dtype)
    vc_ref[...] = v_cmp.astype(vc_ref.dtype)
    kw_ref[...] = k_win.astype(kw_ref.dtype)
    vw_ref[...] = v_win.astype(vw_ref.dtype)

    row = lax.broadcasted_iota(jnp.int32, (tm, LANES), 0)
    lane = lax.broadcasted_iota(jnp.int32, (tm, LANES), 1)
    key_block = ((pl.program_id(0) % (seq // tm)) * tm + row) // L_SEL
    ks_ref[:, :LANES] = k_slc.astype(ks_ref.dtype)
    ks_ref[:, LANES:] = jnp.where(lane == key_block, 1.0, 0.0).astype(ks_ref.dtype)
    for g, vs_ref in enumerate((vs0_ref, vs1_ref)):
        vs_ref[...] = jnp.where((lane // HEAD_DIM) == g, v_slc, 1.0).astype(vs_ref.dtype)

    zg = _dot(hn, w_ref[:, OFF_GATE:W_ALL])
    gate_ref[...] = jax.nn.sigmoid(zg)


def _proj_call(x2, gmix, w_all, cos_t, sin_t, gv, ws_pairs, bs_exp, goa, seq):
    n = x2.shape[0]
    tm = TM_PROJ
    n_t = seq // tm
    row = lambda i: (i, 0)
    const2 = lambda i: (0, 0)
    pos = lambda i: (i % n_t, 0)
    out_shapes = [jax.ShapeDtypeStruct((n, A_WIDTH), _BF16), jax.ShapeDtypeStruct((n, B_WIDTH), _BF16)]
    kv_widths = [KV_W, KV_W, KV_W + LANES, KV_W, KV_W, KV_W, KV_W]
    out_shapes += [jax.ShapeDtypeStruct((n, w), _F32 if j < 2 else _BF16) for j, w in enumerate(kv_widths)]
    out_shapes += [jax.ShapeDtypeStruct((n, GATE_PAD), _F32)]
    out_specs = [pl.BlockSpec((tm, A_WIDTH), row), pl.BlockSpec((tm, B_WIDTH), row)]
    out_specs += [pl.BlockSpec((tm, w), row) for w in kv_widths]
    out_specs += [pl.BlockSpec((tm, GATE_PAD), row)]
    return pl.pallas_call(
        functools.partial(_proj_kernel, seq=seq),
        grid=(n // tm,),
        in_specs=[
            pl.BlockSpec((tm, D_MODEL), row),
            pl.BlockSpec((1, D_MODEL), const2),
            pl.BlockSpec((D_MODEL, W_ALL), const2),
            pl.BlockSpec((tm, LANES), pos),
            pl.BlockSpec((tm, LANES), pos),
            pl.BlockSpec((1, A_WIDTH), const2),
            pl.BlockSpec((A_HEADS // 2, CHUNK, 2 * CHUNK), lambda i: (0, 0, 0)),
            pl.BlockSpec((CHUNK, A_WIDTH), const2),
            pl.BlockSpec((1, A_WIDTH), const2),
        ],
        out_specs=out_specs,
        out_shape=out_shapes,
        compiler_params=pltpu.CompilerParams(dimension_semantics=("arbitrary",), vmem_limit_bytes=VMEM_LIMIT),
        name="proj",
    )(x2, gmix, w_all, cos_t, sin_t, gv, ws_pairs, bs_exp, goa)


def _compress_kernel(rk_ref, rv_ref, pek_ref, pev_ref, w1k_ref, w1v_ref,
                     tk_ref, bk_ref, w2k_ref, tv_ref, bv_ref, w2v_ref, kc_ref, vc_ref):
    def one(r_ref, pe_ref, w1_ref, top_ref, bot_ref, w2_ref, o_ref):
        nr = r_ref.shape[1] // STRIDE_CMP
        r = jnp.concatenate([r_ref[0, pl.ds(l, nr, stride=STRIDE_CMP), :] for l in range(STRIDE_CMP)],
                            axis=1).astype(_BF16)
        a = _dot(r, top_ref[...])
        b = _dot(r, bot_ref[...])
        pe_h = _dot(pe_ref[...], w1_ref[...])
        pe2 = jnp.concatenate([pe_h[0:1], pe_h[0:1]], axis=1)
        hid = a + pltpu.roll(b, nr - 1, 0) + pe2
        o_ref[0] = _dot(_gelu(hid).astype(_BF16), w2_ref[...]).astype(o_ref.dtype)

    one(rk_ref, pek_ref, w1k_ref, tk_ref, bk_ref, w2k_ref, kc_ref)
    one(rv_ref, pev_ref, w1v_ref, tv_ref, bv_ref, w2v_ref, vc_ref)


def _compress_weights(w1, w2, pe):
    half = L_CMP // 2
    w1r = w1.reshape(L_CMP, HEAD_DIM, CMP_HIDDEN)
    eye = jnp.eye(B_KV, dtype=w1.dtype)
    place = lambda part: jnp.einsum('ldj,gh->lgdhj', part, eye).reshape(half * KV_W, B_KV * CMP_HIDDEN)
    top = place(w1r[:half]).astype(_BF16)
    bot = place(w1r[half:]).astype(_BF16)
    w2bd = jnp.einsum('jd,gh->gjhd', w2, eye).reshape(B_KV * CMP_HIDDEN, KV_W).astype(_BF16)
    pe8 = jnp.broadcast_to(pe.reshape(1, L_CMP * HEAD_DIM), (SUBLANES, L_CMP * HEAD_DIM)).astype(_BF16)
    return pe8, w1.astype(_BF16), top, bot, w2bd


def _compress_call(k_cmp, v_cmp, wk, wv, bsz, seq):
    nr = seq // STRIDE_CMP
    rk = k_cmp.reshape(bsz, seq, KV_W)
    rv = v_cmp.reshape(bsz, seq, KV_W)
    pek, w1k, tk, bk, w2k = wk
    pev, w1v, tv, bv, w2v = wv
    rspec = pl.BlockSpec((1, seq, KV_W), lambda b: (b, 0, 0))
    full = lambda a: pl.BlockSpec(a.shape, lambda b: (0,) * a.ndim)
    ospec = pl.BlockSpec((1, nr, KV_W), lambda b: (b, 0, 0))
    return pl.pallas_call(
        _compress_kernel,
        grid=(bsz,),
        in_specs=[rspec, rspec, full(pek), full(pev), full(w1k), full(w1v),
                  full(tk), full(bk), full(w2k), full(tv), full(bv), full(w2v)],
        out_specs=[ospec, ospec],
        out_shape=[jax.ShapeDtypeStruct((bsz, nr, KV_W), _BF16)] * 2,
        compiler_params=pltpu.CompilerParams(dimension_semantics=("arbitrary",), vmem_limit_bytes=VMEM_LIMIT),
        name="compress",
    )(rk, rv, pek, pev, w1k, w1v, tk, bk, w2k, tv, bv, w2v)


def _topk_rows_mask(sc_t, k):
    n_rows = sc_t.shape[0]
    row = lax.broadcasted_iota(jnp.int32, sc_t.shape, 0).astype(sc_t.dtype)
    taken = jnp.asarray(-3e38, sc_t.dtype)
    for _ in range(k):
        m = jnp.max(sc_t, axis=0, keepdims=True)
        idx = jnp.min(jnp.where(sc_t == m, row, jnp.asarray(n_rows, sc_t.dtype)), axis=0, keepdims=True)
        sc_t = jnp.where(row == idx, taken, sc_t)
    return jnp.where(sc_t == taken, 1.0, 0.0).astype(_F32)


def _attn_kernel(q_ref, gate_ref, gexp_ref, kc_ref, vc_ref, ks_ref, vs0_ref, vs1_ref,
                 kw_ref, vw_ref, gob_ref, *rest, seq, n_cast):
    cast_in, o_ref, cast_out = rest[:n_cast], rest[n_cast], rest[n_cast + 1:2 * n_cast + 1]
    m_scr, acc_scr = rest[2 * n_cast + 1:]
    for src, dst in zip(cast_in, cast_out):
        dst[...] = src[...].astype(dst.dtype)

    qb = pl.program_id(1)
    t0 = qb * Q_BLOCK
    n_cmp = kc_ref.shape[1]
    n_sb = seq // L_SEL
    k_top = min(N_SEL, n_sb)
    rows = B_HPG * Q_BLOCK
    vs_refs = (vs0_ref, vs1_ref)

    lane_q = lax.broadcasted_iota(jnp.int32, (Q_BLOCK, LANES), 1)
    lo = lane_q < HEAD_DIM
    t_col = t0 + lax.broadcasted_iota(jnp.int32, (Q_BLOCK, 1), 0)

    def per_head(x):
        return x.reshape(B_HPG, Q_BLOCK, x.shape[-1])

    def add_bias(s, bias):
        return (per_head(s) + bias[None]).reshape(rows, s.shape[-1])

    qs = []
    for g in range(B_KV):
        own = jnp.where((lane_q // HEAD_DIM) == g, 1.0, 0.0).astype(q_ref.dtype)
        qs.append(jnp.concatenate([q_ref[0, :, j * LANES:(j + 1) * LANES] * own for j in range(B_HPG)], axis=0))

    thr0 = (t0 - (L_CMP - 1)) // STRIDE_CMP
    n_thr = (Q_BLOCK - 1) // STRIDE_CMP + 2
    assert n_thr <= LANES
    thr_rel = (t_col - (L_CMP - 1)) // STRIDE_CMP - thr0
    q_thr = jnp.where(lane_q == thr_rel, 1.0, 0.0).astype(_BF16)
    c_row1 = lax.broadcasted_iota(jnp.int32, (n_cmp, LANES), 0)
    c_lane1 = lax.broadcasted_iota(jnp.int32, (n_cmp, LANES), 1)
    k_thr = jnp.where((c_lane1 < n_thr) & (c_row1 > thr0 + c_lane1), NEG, 0.0).astype(_BF16)
    kc_wide = jnp.concatenate([kc_ref[0], k_thr], axis=1)
    has_c = (t_col >= L_CMP - 1).astype(_F32)
    c_row = lax.broadcasted_iota(jnp.int32, (n_cmp, n_sb), 0) * STRIDE_CMP
    s_col = lax.broadcasted_iota(jnp.int32, (n_cmp, n_sb), 1) * L_SEL
    overlap = jnp.where((c_row < s_col + L_SEL) & (c_row + L_CMP > s_col), 1.0, 0.0).astype(_BF16)
    blk = lax.broadcasted_iota(jnp.int32, (Q_BLOCK, n_sb), 1)
    cur = t_col // L_SEL
    forced = (blk == 0) | (blk == cur) | (blk == cur - 1)
    valid = blk * L_SEL <= t_col

    o_c, sel_bias = [], []
    for g in range(B_KV):
        s_c = _dot_nt(jnp.concatenate([qs[g], jnp.concatenate([q_thr] * B_HPG, axis=0)], axis=1), kc_wide)
        e_c = jnp.exp2(s_c - jnp.max(s_c, axis=-1, keepdims=True))
        inv = per_head(1.0 / jnp.maximum(jnp.sum(e_c, axis=-1, keepdims=True), 1e-30)) * has_c[None]
        e_bf = e_c.astype(_BF16)
        o_c.append((per_head(_dot(e_bf, vc_ref[0])) * inv).reshape(rows, LANES))
        p_bf = per_head(e_bf) * inv.astype(_BF16)
        imp = _dot(functools.reduce(lambda a, b: a + b, [p_bf[j] for j in range(B_HPG)]), overlap)
        score = jnp.where(valid & jnp.logical_not(forced), imp, -FORCE)
        chosen = forced | (_topk_rows_mask(score.T, k_top - 3).T > 0.5)
        sb = jnp.where(chosen & valid, 0.0, NEG)
        if n_sb < LANES:
            sb = jnp.concatenate([sb, jnp.full((Q_BLOCK, LANES - n_sb), NEG, _F32)], axis=1)
        sel_bias.append(sb)

    w_len = WIN + WIN_Q
    lane_w = lax.broadcasted_iota(jnp.int32, (w_len, LANES), 1)
    keep_w = [jnp.where((lane_w // HEAD_DIM) == g, 1.0, 0.0).astype(vw_ref.dtype) for g in range(B_KV)]
    fill_w = [1 - k for k in keep_w]
    acc_w_sub = [[] for _ in range(B_KV)]
    for hh in range(Q_BLOCK // WIN_Q):
        w_start = pl.multiple_of(jnp.maximum(t0 + hh * WIN_Q - WIN, 0), WIN_Q)
        t_sub = t_col[hh * WIN_Q:(hh + 1) * WIN_Q]
        diff_w = t_sub - (w_start + lax.broadcasted_iota(jnp.int32, (WIN_Q, w_len), 1))
        bias_w = jnp.where((diff_w >= 0) & (diff_w < WIN), 0.0, NEG)
        kw = kw_ref[0, pl.ds(w_start, w_len), :]
        vw = vw_ref[0, pl.ds(w_start, w_len), :]
        for g in range(B_KV):
            q_sub = per_head(qs[g])[:, hh * WIN_Q:(hh + 1) * WIN_Q, :].reshape(B_HPG * WIN_Q, LANES)
            s_w = (_dot_nt(q_sub, kw).reshape(B_HPG, WIN_Q, w_len) + bias_w[None]).reshape(B_HPG * WIN_Q, w_len)
            e_w = jnp.exp2(s_w - jnp.max(s_w, axis=-1, keepdims=True)).astype(_BF16)
            vw_aug = vw * keep_w[g] + fill_w[g]
            acc_w_sub[g].append(_dot(e_w, vw_aug).reshape(B_HPG, WIN_Q, LANES))
    acc_w = [jnp.concatenate(acc_w_sub[g], axis=1).reshape(rows, LANES) for g in range(B_KV)]

    n_ck = (t0 + Q_BLOCK + SEL_CK - 1) // SEL_CK
    key_lane = lax.broadcasted_iota(jnp.int32, (Q_BLOCK, SEL_CK), 1)
    bias_diag = jnp.where((n_ck - 1) * SEL_CK + key_lane <= t_col, 0.0, NEG)
    q_wide = [jnp.concatenate([qs[g], jnp.concatenate([sel_bias[g].astype(_BF16)] * B_HPG, axis=0)], axis=1)
              for g in range(B_KV)]

    m_scr[...] = jnp.full(m_scr.shape, NEG, _F32)
    acc_scr[...] = jnp.zeros(acc_scr.shape, _F32)

    def sel_chunk(ci):
        k0 = pl.multiple_of(ci * SEL_CK, SEL_CK)
        bias = jnp.where(ci == n_ck - 1, bias_diag, 0.0)
        for g in range(B_KV):
            m = m_scr[g]
            s = _dot_nt(q_wide[g], ks_ref[0, pl.ds(k0, SEL_CK), :])
            s = add_bias(s, bias)
            m_new = jnp.maximum(m, jnp.max(s, axis=-1, keepdims=True))
            p = jnp.exp2(s - jnp.concatenate([m_new] * (SEL_CK // LANES), axis=1)).astype(_BF16)
            acc_scr[g] = jnp.exp2(m - m_new) * acc_scr[g] + _dot(p, vs_refs[g][0, pl.ds(k0, SEL_CK), :])
            m_scr[g] = m_new

    @pl.loop(0, n_ck // 2)
    def _(pi):
        sel_chunk(2 * pi)
        sel_chunk(2 * pi + 1)

    @pl.when(n_ck % 2 == 1)
    def _():
        sel_chunk(n_ck - 1)

    acc_s = [acc_scr[g] for g in range(B_KV)]

    def numer(acc):
        return jnp.concatenate([jnp.where(lo, acc[0][j * Q_BLOCK:(j + 1) * Q_BLOCK],
                                          acc[1][j * Q_BLOCK:(j + 1) * Q_BLOCK]) for j in range(B_HPG)], axis=1)

    def denom(acc):
        return jnp.concatenate([pltpu.roll(jnp.where(lo, acc[1][j * Q_BLOCK:(j + 1) * Q_BLOCK],
                                                     acc[0][j * Q_BLOCK:(j + 1) * Q_BLOCK]), HEAD_DIM, 1)
                                for j in range(B_HPG)], axis=1)

    gates = gate_ref[0]
    g_hi = gates.astype(_BF16)
    g_split = jnp.concatenate([g_hi, (gates - g_hi.astype(_F32)).astype(_BF16)], axis=1)
    gate_of = lambda r: _dot(g_split, gexp_ref[r])
    ob = (gate_of(0) * numer(o_c)
          + gate_of(1) * numer(acc_s) * (1.0 / jnp.maximum(denom(acc_s), 1e-30))
          + gate_of(2) * numer(acc_w) * (1.0 / jnp.maximum(denom(acc_w), 1e-30)))
    o_ref[0] = _rms(ob, gob_ref[...]).astype(o_ref.dtype)


def _gate_expand():
    x = np.zeros((3, GATE_PAD, B_WIDTH), np.float32)
    for slot, h in enumerate(_PERM_HEADS):
        for r in range(3):
            x[r, 3 * h + r, slot * HEAD_DIM:(slot + 1) * HEAD_DIM] = 1.0
    return jnp.asarray(np.concatenate([x, x], axis=1), _BF16)


def _attn_call(q, gates, kc, vc, ks, vs0, vs1, kw, vw, gob, to_cast, bsz, seq):
    assert seq // L_SEL <= LANES
    n_cmp = kc.shape[1]
    n_q = seq // Q_BLOCK
    steps = bsz * n_q
    qspec = lambda w: pl.BlockSpec((1, Q_BLOCK, w), lambda b, i: (b, i, 0))
    full = lambda r, w=KV_W: pl.BlockSpec((1, r, w), lambda b, i: (b, 0, 0), pipeline_mode=pl.Buffered(1))
    sliced = [a.reshape(steps, a.size // (steps * a.shape[-1]), a.shape[-1]) for a in to_cast]
    cast_specs = [pl.BlockSpec((1,) + a.shape[1:], lambda b, i: (b * n_q + i, 0, 0)) for a in sliced]
    outs = pl.pallas_call(
        functools.partial(_attn_kernel, seq=seq, n_cast=len(sliced)),
        grid=(bsz, n_q),
        in_specs=[qspec(B_WIDTH), qspec(GATE_PAD),
                  pl.BlockSpec((3, 2 * GATE_PAD, B_WIDTH), lambda b, i: (0, 0, 0)),
                  full(n_cmp), full(n_cmp), full(seq, KV_W + LANES),
                  full(seq), full(seq), full(seq), full(seq),
                  pl.BlockSpec((1, B_WIDTH), lambda b, i: (0, 0))] + cast_specs,
        out_specs=[qspec(B_WIDTH)] + cast_specs,
        out_shape=[jax.ShapeDtypeStruct((bsz, seq, B_WIDTH), _BF16)]
        + [jax.ShapeDtypeStruct(a.shape, _BF16) for a in sliced],
        scratch_shapes=[pltpu.VMEM((B_KV, B_HPG * Q_BLOCK, LANES), _F32),
                        pltpu.VMEM((B_KV, B_HPG * Q_BLOCK, LANES), _F32)],
        compiler_params=pltpu.CompilerParams(dimension_semantics=("arbitrary", "arbitrary"),
                                             vmem_limit_bytes=VMEM_LIMIT_ATTN),
        name="attn",
    )(q, gates, _gate_expand(), kc, vc, ks, vs0, vs1, kw, vw, gob, *sliced)
    return outs[0], [o.reshape(a.shape) for o, a in zip(outs[1:], to_cast)]


def _pack_halves(x):
    w = x.shape[1] // 2
    bits = lambda v: lax.bitcast_convert_type(v.astype(_BF16).astype(_F32), jnp.uint32)
    return lax.bitcast_convert_type(bits(x[:, :w]) | (bits(x[:, w:]) >> 16), jnp.int32)


def _unpack_halves(p):
    u = lax.bitcast_convert_type(p, jnp.uint32)
    return (lax.bitcast_convert_type(u & jnp.uint32(0xFFFF0000), _F32),
            lax.bitcast_convert_type(u << 16, _F32))


def _post_kernel(x_ref, oa_ref, ob_ref, woa_ref, wob_ref, gmoe_ref, r_ref, h1_ref, hn_ref, rt_ref, cnt_ref):
    h1 = x_ref[...] + _dot(oa_ref[...], woa_ref[...]) + _dot(ob_ref[...], wob_ref[...])
    h1_ref[...] = h1
    hn = _rms(h1, gmoe_ref[...])
    hn_ref[...] = _pack_halves(hn)

    hn_hi = hn.astype(_BF16)
    hn_lo = (hn - hn_hi.astype(_F32)).astype(_BF16)
    hi_both = _dot(hn_hi, r_ref[...])
    logits = hi_both[:, :GATE_PAD] + (_dot(hn_lo, r_ref[:, :GATE_PAD]) + hi_both[:, GATE_PAD:])
    lt = logits.T[:ROUTER_ROWS]
    row = lax.broadcasted_iota(jnp.int32, lt.shape, 0)
    first_idx = lambda hit: jnp.min(jnp.where(hit, row, LANES), axis=0, keepdims=True)

    is_g = row < N_GROUPS
    lg = jnp.where(is_g, lt, NEG)
    mg = jnp.max(lg, axis=0, keepdims=True)
    sg = jnp.sum(jnp.where(is_g, jnp.exp(lg - mg), 0.0), axis=0, keepdims=True)
    pg_top = 1.0 / sg
    g_sel = first_idx(is_g & (lg == mg))

    e_lo = ROUTER_OFF + g_sel * EXPERTS_PER_GROUP
    is_e = (row >= e_lo) & (row < e_lo + EXPERTS_PER_GROUP)
    le = jnp.where(is_e, lt, NEG)
    m1 = jnp.max(le, axis=0, keepdims=True)
    se = jnp.sum(jnp.where(is_e, jnp.exp(le - m1), 0.0), axis=0, keepdims=True)
    i1 = first_idx(is_e & (le == m1))
    le2 = jnp.where(row == i1, NEG, le)
    m2 = jnp.max(le2, axis=0, keepdims=True)
    i2 = first_idx(is_e & (row != i1) & (le2 == m2))
    pe1 = 1.0 / se
    pe2 = jnp.exp(m2 - m1) / se
    denom = pe1 + pe2
    rrow = lax.broadcasted_iota(jnp.int32, (RT_ROWS, lt.shape[1]), 0)
    rt_t = (jnp.where(rrow == i1 - ROUTER_OFF, 1.0, 0.0)
            + jnp.where(rrow == i2 - ROUTER_OFF + N_EXPERTS, 1.0, 0.0)
            + jnp.where(rrow == RT_W, pg_top * pe1 / denom, 0.0)
            + jnp.where(rrow == RT_W + 1, pg_top * pe2 / denom, 0.0))
    rt = jnp.concatenate([rt_t, jnp.zeros((LANES - RT_ROWS, lt.shape[1]), _F32)], axis=0).T
    rt_ref[...] = rt

    @pl.when(pl.program_id(0) == 0)
    def _():
        cnt_ref[...] = jnp.zeros_like(cnt_ref)

    cnt_ref[...] += jnp.sum(rt, axis=0, keepdims=True)


def _post_call(x2, oa, ob, woa, wob, gmoe, r_cat):
    n = x2.shape[0]
    tm = TM_POST
    row = lambda i: (i, 0)
    const2 = lambda i: (0, 0)
    return pl.pallas_call(
        _post_kernel,
        grid=(n // tm,),
        in_specs=[pl.BlockSpec((tm, D_MODEL), row), pl.BlockSpec((tm, A_WIDTH), row),
                  pl.BlockSpec((tm, B_WIDTH), row), pl.BlockSpec((A_WIDTH, D_MODEL), const2),
                  pl.BlockSpec((B_WIDTH, D_MODEL), const2), pl.BlockSpec((1, D_MODEL), const2),
                  pl.BlockSpec((D_MODEL, 2 * GATE_PAD), const2)],
        out_specs=[pl.BlockSpec((tm, D_MODEL), row), pl.BlockSpec((tm, D_MODEL // 2), row),
                   pl.BlockSpec((tm, GATE_PAD), row), pl.BlockSpec((SUBLANES, LANES), const2)],
        out_shape=[jax.ShapeDtypeStruct((n, D_MODEL), _F32), jax.ShapeDtypeStruct((n, D_MODEL // 2), jnp.int32),
                   jax.ShapeDtypeStruct((n, GATE_PAD), _F32), jax.ShapeDtypeStruct((SUBLANES, LANES), _F32)],
        compiler_params=pltpu.CompilerParams(dimension_semantics=("arbitrary",), vmem_limit_bytes=VMEM_LIMIT),
        name="post",
    )(x2, oa, ob, woa, wob, gmoe, r_cat)


def _route_kernel(rt_ref, cnt_ref, earlier_ref, dest_ref, meta_ref, off_ref, run_ref):
    tm = rt_ref.shape[0]
    lane = lax.broadcasted_iota(jnp.int32, (1, LANES), 1)
    first = lane < N_EXPERTS
    onehot = jnp.where(lane < 2 * N_EXPERTS, rt_ref[...], 0.0)

    @pl.when(pl.program_id(0) == 0)
    def _():
        cnt = jnp.where(lane < 2 * N_EXPERTS, cnt_ref[...], 0.0)
        c1 = jnp.where(first, cnt, 0.0)
        tot = c1 + jnp.where(first, pltpu.roll(cnt, LANES - N_EXPERTS, 1), 0.0)
        tiles = jnp.floor((tot + (TR_GMM - 1)) * (1.0 / TR_GMM))
        e_row = lax.broadcasted_iota(jnp.int32, (LANES, LANES), 0)
        e_col = lax.broadcasted_iota(jnp.int32, (LANES, LANES), 1)
        before = jnp.where(e_row < e_col, 1.0, 0.0).astype(_BF16)
        base = _dot(tiles.astype(_BF16), before) * TR_GMM
        off_ref[...] = jnp.where(first, base, 0.0) + pltpu.roll(jnp.where(first, base + c1, 0.0), N_EXPERTS, 1)
        run_ref[...] = jnp.zeros_like(run_ref)
        meta_ref[...] = tiles

    rank = _dot(earlier_ref[...], onehot.astype(_BF16)) + run_ref[0:1, :]
    slot = onehot * (rank + off_ref[0:1, :])
    slot_t = slot.T
    d1 = jnp.sum(slot_t[:N_EXPERTS], axis=0, keepdims=True)
    d2 = jnp.sum(slot_t[N_EXPERTS:2 * N_EXPERTS], axis=0, keepdims=True)
    dest_ref[...] = jnp.concatenate([d1, d2, jnp.zeros((SUBLANES - 2, tm), _F32)], axis=0).astype(jnp.int32)
    run_ref[...] += jnp.sum(onehot, axis=0, keepdims=True)


def _route_call(rt, cnt):
    n = rt.shape[0]
    tm = TM_ROUTE
    return pl.pallas_call(
        _route_kernel,
        grid=(n // tm,),
        in_specs=[pl.BlockSpec((tm, GATE_PAD), lambda i: (i, 0)), pl.BlockSpec((SUBLANES, LANES), lambda i: (0, 0)),
                  pl.BlockSpec((tm, tm), lambda i: (0, 0))],
        out_specs=[pl.BlockSpec((SUBLANES, tm), lambda i: (0, i)), pl.BlockSpec((SUBLANES, LANES), lambda i: (0, 0))],
        out_shape=[jax.ShapeDtypeStruct((SUBLANES, n), jnp.int32), jax.ShapeDtypeStruct((SUBLANES, LANES), _F32)],
        scratch_shapes=[pltpu.VMEM((SUBLANES, LANES), _F32)] * 2,
        compiler_params=pltpu.CompilerParams(dimension_semantics=("arbitrary",)),
        name="route",
    )(rt, cnt, jnp.asarray(np.tril(np.ones((tm, tm), np.float32), -1), _BF16))


def _sc_mesh():
    return plsc.VectorSubcoreMesh(core_axis_name="c", subcore_axis_name="s")


def _sc_worker(n_rows):
    per = n_rows // (SC_CORES * SC_SUBCORES)
    return (lax.axis_index("s") * SC_CORES + lax.axis_index("c")) * per, per


def _dispatch_call(xp, d1, d2, n_slots):
    n, w = xp.shape
    assert n % (SC_CORES * SC_SUBCORES * SC_CHUNK) == 0

    @functools.partial(
        pl.kernel, mesh=_sc_mesh(), out_type=jax.ShapeDtypeStruct((n_slots, w), xp.dtype),
        scratch_types=[pltpu.VMEM((SC_CHUNK,), jnp.int32), pltpu.VMEM((SC_CHUNK,), jnp.int32),
                       pltpu.VMEM((SC_CHUNK, w), xp.dtype), pltpu.SemaphoreType.DMA],
        name="dispatch")
    def k(x_hbm, d1_hbm, d2_hbm, xs_hbm, i1_v, i2_v, rows_v, sem):
        row0, per = _sc_worker(n)

        @pl.loop(0, per // SC_CHUNK)
        def _(j):
            src = pl.ds(row0 + j * SC_CHUNK, SC_CHUNK)
            pltpu.sync_copy(d1_hbm.at[src], i1_v)
            pltpu.sync_copy(d2_hbm.at[src], i2_v)
            pltpu.sync_copy(x_hbm.at[src], rows_v)
            first = pltpu.async_copy(rows_v, xs_hbm.at[i1_v], sem)
            second = pltpu.async_copy(rows_v, xs_hbm.at[i2_v], sem)
            first.wait()
            second.wait()

    return k(xp, d1, d2)


def _combine_call(ys, d1, d2):
    n = d1.shape[0]
    w = ys.shape[1]
    assert n % (SC_CORES * SC_SUBCORES * SC_CHUNK) == 0
    out = jax.ShapeDtypeStruct((n, w), ys.dtype)

    @functools.partial(
        pl.kernel, mesh=_sc_mesh(), out_type=(out, out),
        scratch_types=[pltpu.VMEM((SC_CHUNK,), jnp.int32), pltpu.VMEM((SC_CHUNK, w), ys.dtype),
                       pltpu.SemaphoreType.DMA],
        name="combine")
    def k(ys_hbm, d1_hbm, d2_hbm, y1_hbm, y2_hbm, i_v, rows_v, sem):
        row0, per = _sc_worker(n)

        @pl.loop(0, per // SC_CHUNK)
        def _(j):
            dst = pl.ds(row0 + j * SC_CHUNK, SC_CHUNK)
            for d_hbm, y_hbm in ((d1_hbm, y1_hbm), (d2_hbm, y2_hbm)):
                pltpu.sync_copy(d_hbm.at[dst], i_v)
                pltpu.async_copy(ys_hbm.at[i_v], rows_v, sem).wait()
                pltpu.sync_copy(rows_v, y_hbm.at[dst])

    return k(ys, d1, d2)


def _gmm_kernel(te_ref, nu_ref, xs_ref, wg0_ref, wu0_ref, wd0_ref, wg1_ref, wu1_ref, wd1_ref, ys_ref):
    half = D_MODEL // 2
    t0 = 2 * pl.program_id(0)

    def ffn(rows, wg_ref, wu_ref, wd_ref):
        a, b = _unpack_halves(xs_ref[rows, :])
        a, b = a.astype(_BF16), b.astype(_BF16)
        gate = _dot(a, wg_ref[0, :half, :]) + _dot(b, wg_ref[0, half:, :])
        up = _dot(a, wu_ref[0, :half, :]) + _dot(b, wu_ref[0, half:, :])
        hid = gate * jax.nn.sigmoid(gate) * up
        ys_ref[rows, :] = _pack_halves(_dot(hid.astype(_BF16), wd_ref[0]))

    both = t0 + 1 < nu_ref[0]
    same = te_ref[t0] == te_ref[t0 + 1]

    @pl.when(both & same)
    def _():
        ffn(slice(0, 2 * TR_GMM), wg0_ref, wu0_ref, wd0_ref)

    @pl.when((t0 < nu_ref[0]) & jnp.logical_not(both & same))
    def _():
        ffn(slice(0, TR_GMM), wg0_ref, wu0_ref, wd0_ref)

    @pl.when(both & jnp.logical_not(same))
    def _():
        ffn(slice(TR_GMM, 2 * TR_GMM), wg1_ref, wu1_ref, wd1_ref)


def _gmm_call(tile_expert, n_used, xs, w_gate, w_up, w_down):
    n_slots, w = xs.shape
    assert (n_slots // TR_GMM) % 2 == 0
    last = lambda nu: jnp.maximum((nu[0] - 1) // 2, 0)
    rows = lambda u, te, nu: (jnp.minimum(u, last(nu)), 0)
    expert0 = lambda u, te, nu: (te[2 * u], 0, 0)
    expert1 = lambda u, te, nu: (te[2 * u + 1], 0, 0)
    w_in = lambda idx: pl.BlockSpec((1, D_MODEL, D_FF_EXPERT), idx)
    w_out = lambda idx: pl.BlockSpec((1, D_FF_EXPERT, D_MODEL), idx)
    return pl.pallas_call(
        _gmm_kernel,
        grid_spec=pltpu.PrefetchScalarGridSpec(
            num_scalar_prefetch=2, grid=(n_slots // (2 * TR_GMM),),
            in_specs=[pl.BlockSpec((2 * TR_GMM, w), rows),
                      w_in(expert0), w_in(expert0), w_out(expert0), w_in(expert1), w_in(expert1), w_out(expert1)],
            out_specs=pl.BlockSpec((2 * TR_GMM, w), rows)),
        out_shape=jax.ShapeDtypeStruct((n_slots, w), xs.dtype),
        compiler_params=pltpu.CompilerParams(dimension_semantics=("arbitrary",), vmem_limit_bytes=VMEM_LIMIT),
        name="gmm",
    )(tile_expert, n_used, xs, w_gate, w_up, w_down, w_gate, w_up, w_down)


def _final_kernel(h1_ref, y1_ref, y2_ref, rt_ref, p_ref, gple_ref, wpg_ref, wpp_ref, gfin_ref, o_ref):
    for r in range(h1_ref.shape[0] // SUB_MOE):
        rows = slice(r * SUB_MOE, (r + 1) * SUB_MOE)
        rt = rt_ref[rows, :]
        w1, w2 = rt[:, RT_W:RT_W + 1], rt[:, RT_W + 1:RT_W + 2]
        a1, b1 = _unpack_halves(y1_ref[rows, :])
        a2, b2 = _unpack_halves(y2_ref[rows, :])
        h2 = h1_ref[rows, :] + jnp.concatenate([w1 * a1 + w2 * a2, w1 * b1 + w2 * b2], axis=1)
        gate = jax.nn.sigmoid(_dot(_rms(h2, gple_ref[...]).astype(_BF16), wpg_ref[...]))
        h3 = h2 + _dot(p_ref[rows, :].astype(_BF16), wpp_ref[...]) * gate
        o_ref[rows, :] = _rms(h3, gfin_ref[...])


def _final_call(h1, y1, y2, rt, p2, gple, wpg, wpp, gfin):
    n = h1.shape[0]
    tm = TM_POST
    row = lambda i: (i, 0)
    const2 = lambda i: (0, 0)
    return pl.pallas_call(
        _final_kernel,
        grid=(n // tm,),
        in_specs=[pl.BlockSpec((tm, D_MODEL), row), pl.BlockSpec((tm, D_MODEL // 2), row),
                  pl.BlockSpec((tm, D_MODEL // 2), row), pl.BlockSpec((tm, GATE_PAD), row),
                  pl.BlockSpec((tm, D_PLE), row), pl.BlockSpec((1, D_MODEL), const2),
                  pl.BlockSpec((D_MODEL, D_MODEL), const2), pl.BlockSpec((D_PLE, D_MODEL), const2),
                  pl.BlockSpec((1, D_MODEL), const2)],
        out_specs=pl.BlockSpec((tm, D_MODEL), row),
        out_shape=jax.ShapeDtypeStruct((n, D_MODEL), _F32),
        compiler_params=pltpu.CompilerParams(dimension_semantics=("arbitrary",), vmem_limit_bytes=VMEM_LIMIT),
        name="final",
    )(h1, y1, y2, rt, p2, gple, wpg, wpp, gfin)


def _rope_tables(seq):
    half = HEAD_DIM // 2
    inv = 1.0 / (ROPE_THETA ** (jnp.arange(half, dtype=_F32) / half))
    ang = jnp.arange(seq, dtype=_F32)[:, None] * inv[None, :]
    cos, sin = jnp.cos(ang), jnp.sin(ang)
    reps = LANES // HEAD_DIM
    cos_t = jnp.tile(jnp.concatenate([cos, cos], axis=1), (1, reps))
    sin_t = jnp.tile(jnp.concatenate([-sin, sin], axis=1), (1, reps))
    return cos_t, sin_t


def _layer(h, p_i, norm_mix, w_in, gmlp_v_norm, gmlp_w_s, gmlp_b_s,
           cmp_pe_k, cmp_w1_k, cmp_w2_k, cmp_pe_v, cmp_w1_v, cmp_w2_v,
           out_norm_a, out_norm_b, w_o, norm_moe, router_group, router_expert,
           moe_w_gate, moe_w_up, moe_w_down, norm_ple, w_ple_proj, w_ple_gate, norm_final):
    bsz, seq, _ = h.shape
    n = bsz * seq
    x2 = h.reshape(n, D_MODEL)
    row = lambda v: v.reshape(1, -1).astype(_F32)

    w_q = _perm_heads(w_in[:, OFF_Q:OFF_KV], 1)
    w_gate = jnp.pad(w_in[:, OFF_GATE:D_IN], ((0, 0), (0, GATE_PAD - N_GATES)))
    w_all = jnp.concatenate([w_in[:, :OFF_Q], w_q, w_in[:, OFF_KV:OFF_GATE], w_gate], axis=1).astype(_BF16)
    cos_t, sin_t = _rope_tables(seq)
    ws_pairs = gmlp_w_s.reshape(A_HEADS // 2, 2, CHUNK, CHUNK).transpose(0, 2, 1, 3).reshape(
        A_HEADS // 2, CHUNK, 2 * CHUNK)
    bs_exp = jnp.repeat(gmlp_b_s.T, HEAD_DIM, axis=1)

    oa, q, k_cmp, v_cmp, ks, vs0, vs1, k_win, v_win, gates = _proj_call(
        x2, row(norm_mix), w_all, cos_t, sin_t, row(gmlp_v_norm), ws_pairs, bs_exp, row(out_norm_a), seq)

    kc, vc = _compress_call(k_cmp, v_cmp, _compress_weights(cmp_w1_k, cmp_w2_k, cmp_pe_k),
                            _compress_weights(cmp_w1_v, cmp_w2_v, cmp_pe_v), bsz, seq)

    b3 = lambda a: a.reshape(bsz, seq, a.shape[-1])
    ob, (wg_bf, wu_bf, wd_bf) = _attn_call(
        b3(q), b3(gates), kc, vc, b3(ks), b3(vs0), b3(vs1), b3(k_win), b3(v_win),
        row(_perm_heads(out_norm_b, 0)), [moe_w_gate, moe_w_up, moe_w_down], bsz, seq)

    r_cat = jnp.pad(jnp.concatenate([router_group, router_expert], axis=1),
                    ((0, 0), (0, GATE_PAD - N_GROUPS - N_EXPERTS)))
    r_hi = r_cat.astype(_BF16)
    r_cat = jnp.concatenate([r_hi, (r_cat - r_hi.astype(_F32)).astype(_BF16)], axis=1)
    woa, wob = w_o[:A_WIDTH].astype(_BF16), _perm_heads(w_o[A_WIDTH:], 0).astype(_BF16)
    wpg, wpp = w_ple_gate.astype(_BF16), w_ple_proj.astype(_BF16)

    h1, xp, rt, cnt = _post_call(x2, oa, ob.reshape(n, B_WIDTH), woa, wob, row(norm_moe), r_cat)

    dest, meta = _route_call(rt, cnt)
    d1, d2 = dest[0], dest[1]
    n_tiles = 2 * n // TR_GMM + N_EXPERTS
    ends = jnp.cumsum(meta[0, :N_EXPERTS].astype(jnp.int32))
    tile_expert = jnp.minimum(jnp.sum(ends[None, :] <= jnp.arange(n_tiles)[:, None], axis=1),
                              N_EXPERTS - 1).astype(jnp.int32)
    xs = _dispatch_call(xp, d1, d2, n_tiles * TR_GMM)
    ys = _gmm_call(tile_expert, ends[-1:], xs, wg_bf, wu_bf, wd_bf)
    y1, y2 = _combine_call(ys, d1, d2)
    out = _final_call(h1, y1, y2, rt, p_i.reshape(n, D_PLE), row(norm_ple), wpg, wpp, row(norm_final))
    return out.reshape(bsz, seq, D_MODEL)


def kernel(x, p, norm_mix, w_in, gmlp_v_norm, gmlp_w_s, gmlp_b_s, cmp_pe_k, cmp_w1_k, cmp_w2_k,
           cmp_pe_v, cmp_w1_v, cmp_w2_v, out_norm_a, out_norm_b, w_o, norm_moe, router_group,
           router_expert, moe_w_gate, moe_w_up, moe_w_down, norm_ple, w_ple_proj, w_ple_gate, norm_final):
    assert p.shape[0] == 1, "single-layer trunk"
    assert x.shape[1] % SEL_CK == 0 and x.shape[1] >= WIN + Q_BLOCK
    assert (x.shape[0] * x.shape[1]) % (SC_CORES * SC_SUBCORES * SC_CHUNK) == 0
    return _layer(x, p[0], norm_mix[0], w_in[0], gmlp_v_norm[0], gmlp_w_s[0], gmlp_b_s[0],
                  cmp_pe_k[0], cmp_w1_k[0], cmp_w2_k[0], cmp_pe_v[0], cmp_w1_v[0], cmp_w2_v[0],
                  out_norm_a[0], out_norm_b[0], w_o[0], norm_moe[0], router_group[0], router_expert[0],
                  moe_w_gate[0], moe_w_up[0], moe_w_down[0], norm_ple[0], w_ple_proj[0], w_ple_gate[0],
                  norm_final)
```
